```python
import math
import jax, jax.numpy as jnp
from jax import lax
import numpy as np

D_MODEL = 1024
BATCH = 4
SEQ = 4096
DEPTH = 2
DEC_BATCH = 16
DEC_SEQ = 32
PAST_LEN = 1024

CHUNK = 64
S5_GROUPS = 32
S5_GROUP_CH = 16
S5_STATE = 64
S5_WIDTH = S5_GROUPS * S5_GROUP_CH
GM_HEADS = 4
GM_HEAD_DIM = 128
GM_WIDTH = GM_HEADS * GM_HEAD_DIM
GM_CHUNK = 128
MIX_WIDTH = S5_WIDTH + GM_WIDTH
IN_PROJ = S5_WIDTH + 2 * GM_WIDTH
N_MEM = 256
XA_HEADS = 4
XA_HEAD_DIM = D_MODEL // XA_HEADS
N_EXPERTS = 32
TOP_K = 4
D_FF = D_MODEL
SWIGLU_LIMIT = 7.0
SWIGLU_ALPHA = 1.702
MOE_BLOCK = 128
EPS = 1e-6
EIG_CLIP = -1e-4

kernel_name = 'hybrid_s5_gmlp_memxattn_moe_stream_step'


def rms_norm(x, g):
    xf = x.astype(jnp.float32)
    y = xf * lax.rsqrt(jnp.mean(xf * xf, axis=-1, keepdims=True) + EPS)
    return (y * g.astype(jnp.float32)).astype(x.dtype)


def _complex_affine_combine(e1, e2):
    a1r, a1i, b1r, b1i = e1
    a2r, a2i, b2r, b2i = e2
    return (a2r * a1r - a2i * a1i,
            a2r * a1i + a2i * a1r,
            a2r * b1r - a2i * b1i + b2r,
            a2r * b1i + a2i * b1r + b2i)


def s5_mixer(u, lam_re, lam_im, b_re, b_im, c_re, c_im, d, log_step, w_glu, h0_re, h0_im):
    bsz, seqlen, _ = u.shape
    f32 = jnp.float32
    uf = u.astype(f32).reshape(bsz, seqlen, S5_GROUPS, S5_GROUP_CH)
    step = jnp.exp(log_step.astype(f32))[:, None]
    lr = jnp.minimum(lam_re.astype(f32), EIG_CLIP)
    li = lam_im.astype(f32)
    mag = jnp.exp(lr * step)
    ar = mag * jnp.cos(li * step)
    ai = mag * jnp.sin(li * step)
    den = lr * lr + li * li
    fr = ((ar - 1.0) * lr + ai * li) / den
    fi = (ai * lr - (ar - 1.0) * li) / den
    br = b_re.astype(f32)
    bi = b_im.astype(f32)
    bbar_r = fr[..., None] * br - fi[..., None] * bi
    bbar_i = fr[..., None] * bi + fi[..., None] * br
    bu_r = jnp.einsum('blgh,gnh->blgn', uf, bbar_r)
    bu_i = jnp.einsum('blgh,gnh->blgn', uf, bbar_i)
    a_r = jnp.broadcast_to(ar, bu_r.shape)
    a_i = jnp.broadcast_to(ai, bu_i.shape)
    _, _, hr, hi = lax.associative_scan(_complex_affine_combine, (a_r, a_i, bu_r, bu_i), axis=1)
    if h0_re is not None:
        t = jnp.arange(1, seqlen + 1, dtype=f32)[:, None, None]
        pm = jnp.exp(lr * step * t)
        pr = pm * jnp.cos(li * step * t)
        pim = pm * jnp.sin(li * step * t)
        h0r = h0_re.astype(f32)[:, None]
        h0i = h0_im.astype(f32)[:, None]
        hr = hr + pr * h0r - pim * h0i
        hi = hi + pr * h0i + pim * h0r
    y = (jnp.einsum('blgn,ghn->blgh', hr, c_re.astype(f32))
         - jnp.einsum('blgn,ghn->blgh', hi, c_im.astype(f32))
         + d.astype(f32) * uf)
    z = jax.nn.gelu(y).reshape(bsz, seqlen, S5_WIDTH).astype(u.dtype) @ w_glu
    out = z[..., :S5_WIDTH] * jax.nn.sigmoid(z[..., S5_WIDTH:])
    return out, hr[:, -1], hi[:, -1]


def gmlp_mixer(u, v, ln_g, ln_b, w_s, b_s):
    bsz, seqlen, _ = u.shape
    f32 = jnp.float32
    uh = jax.nn.gelu(u).reshape(bsz, seqlen, GM_HEADS, GM_HEAD_DIM)
    vf = jax.nn.gelu(v).reshape(bsz, seqlen, GM_HEADS, GM_HEAD_DIM).astype(f32)
    mu = jnp.mean(vf, axis=-1, keepdims=True)
    var = jnp.mean(jnp.square(vf - mu), axis=-1, keepdims=True)
    vn = (vf - mu) * lax.rsqrt(var + EPS) * ln_g.astype(f32) + ln_b.astype(f32)
    pad = (-seqlen) % GM_CHUNK
    vp = jnp.pad(vn, ((0, 0), (0, pad), (0, 0), (0, 0)))
    n_chunks = (seqlen + pad) // GM_CHUNK
    vc = vp.reshape(bsz, n_chunks, GM_CHUNK, GM_HEADS, GM_HEAD_DIM)
    mask = jnp.tril(jnp.ones((GM_CHUNK, GM_CHUNK), dtype=bool))
    w = jnp.where(mask[None], w_s.astype(f32), 0.0)
    s = jnp.einsum('hts,bcshd->bcthd', w, vc) + b_s.astype(f32).T[:, :, None]
    s = s.reshape(bsz, n_chunks * GM_CHUNK, GM_HEADS, GM_HEAD_DIM)[:, :seqlen]
    out = (uh.astype(f32) * s).reshape(bsz, seqlen, GM_WIDTH).astype(u.dtype)
    return out, vn.astype(u.dtype)


def memory_kv(mem, g, w_k, w_v):
    bsz, n_mem, _ = mem.shape
    m = rms_norm(mem, g)
    k = (m @ w_k).reshape(bsz, n_mem, XA_HEADS, XA_HEAD_DIM)
    v = (m @ w_v).reshape(bsz, n_mem, XA_HEADS, XA_HEAD_DIM)
    return k, v


def cross_attention(h, k, v, w_q, w_o):
    bsz, seqlen, _ = h.shape
    q = (h @ w_q).reshape(bsz, seqlen, XA_HEADS, XA_HEAD_DIM)
    s = jnp.einsum('blhd,bmhd->bhlm', q, k).astype(jnp.float32) * (XA_HEAD_DIM ** -0.5)
    p = jax.nn.softmax(s, axis=-1)
    o = jnp.einsum('bhlm,bmhd->blhd', p.astype(v.dtype), v).reshape(bsz, seqlen, D_MODEL)
    return o @ w_o


def moe_ffn(h, router_w, router_b, w_gate, b_gate, w_up, b_up, w_down, b_down):
    shape = h.shape
    x = h.reshape(-1, D_MODEL)
    n_tok = x.shape[0]
    logits = (x @ router_w + router_b).astype(jnp.float32)
    top_val, top_idx = lax.top_k(logits, TOP_K)
    gates = jax.nn.softmax(top_val, axis=-1)
    n_assign = n_tok * TOP_K
    e_flat = top_idx.reshape(-1).astype(jnp.int32)
    tok_flat = jnp.repeat(jnp.arange(n_tok, dtype=jnp.int32), TOP_K)
    g_flat = gates.reshape(-1)
    order = jnp.argsort(e_flat, stable=True)
    e_sorted = e_flat[order]
    tok_sorted = tok_flat[order]
    g_sorted = g_flat[order]
    counts = jnp.bincount(e_flat, length=N_EXPERTS)
    padded = ((counts + MOE_BLOCK - 1) // MOE_BLOCK) * MOE_BLOCK
    pend = jnp.cumsum(padded)
    pstart = pend - padded
    start = jnp.cumsum(counts) - counts
    dest = pstart[e_sorted] + (jnp.arange(n_assign, dtype=jnp.int32) - start[e_sorted])
    n_blocks = n_assign // MOE_BLOCK + N_EXPERTS
    n_rows = n_blocks * MOE_BLOCK
    row_tok = jnp.full((n_rows,), n_tok, dtype=jnp.int32).at[dest].set(tok_sorted)
    row_gate = jnp.zeros((n_rows,), jnp.float32).at[dest].set(g_sorted)
    blk_start = jnp.arange(n_blocks, dtype=pend.dtype) * MOE_BLOCK
    blk_exp = jnp.minimum(jnp.searchsorted(pend, blk_start, side='right'), N_EXPERTS - 1)
    x_pad = jnp.concatenate([x, jnp.zeros((1, D_MODEL), x.dtype)], axis=0)

    def expert_block(args):
        tok, gate, e = args
        xb = x_pad[tok]
        g = jnp.minimum(xb @ w_gate[e] + b_gate[e], SWIGLU_LIMIT)
        u = jnp.clip(xb @ w_up[e] + b_up[e], -SWIGLU_LIMIT, SWIGLU_LIMIT)
        a = g * jax.nn.sigmoid(SWIGLU_ALPHA * g) * (u + 1.0)
        y = a @ w_down[e] + b_down[e]
        return y.astype(jnp.float32) * gate[:, None]

    ys = lax.map(expert_block, (row_tok.reshape(n_blocks, MOE_BLOCK),
                                row_gate.reshape(n_blocks, MOE_BLOCK), blk_exp))
    out = jax.ops.segment_sum(ys.reshape(n_rows, D_MODEL), row_tok, num_segments=n_tok + 1)[:n_tok]
    return out.astype(h.dtype).reshape(shape)


def trunk_layer(x, mem_k, mem_v, h0_re, h0_im, p):
    h = rms_norm(x, p['norm_mix_g'])
    proj = h @ p['w_in']
    u_s5 = proj[..., :S5_WIDTH]
    u_gm = proj[..., S5_WIDTH:S5_WIDTH + GM_WIDTH]
    v_gm = proj[..., S5_WIDTH + GM_WIDTH:]
    o_s5, hr, hi = s5_mixer(u_s5, p['s5_lambda_re'], p['s5_lambda_im'], p['s5_b_re'], p['s5_b_im'],
                            p['s5_c_re'], p['s5_c_im'], p['s5_d'], p['s5_log_step'], p['s5_w_glu'],
                            h0_re, h0_im)
    o_gm, vn = gmlp_mixer(u_gm, v_gm, p['gm_ln_g'], p['gm_ln_b'], p['gm_w_s'], p['gm_b_s'])
    og = p['out_norm_g']
    mixed = jnp.concatenate([rms_norm(o_s5, og[:S5_WIDTH]), rms_norm(o_gm, og[S5_WIDTH:])], axis=-1)
    x = x + mixed @ p['w_out']
    x = x + cross_attention(rms_norm(x, p['xa_norm_g']), mem_k, mem_v, p['w_xq'], p['w_xo'])
    x = x + moe_ffn(rms_norm(x, p['ffn_norm_g']), p['router_w'], p['router_b'], p['e_w_gate'],
                    p['e_b_gate'], p['e_w_up'], p['e_b_up'], p['e_w_down'], p['e_b_down'])
    return x, hr, hi, vn


def setup_inputs(seed: int = 0) -> dict:
    key = jax.random.key(seed)
    ks = iter(jax.random.split(key, 64))
    f32 = jnp.float32

    def nrm(shape, scale):
        return jax.random.normal(next(ks), shape, f32) * scale

    def gain(shape):
        return 1.0 + nrm(shape, 0.02)

    L = DEPTH
    n_idx = jnp.arange(S5_STATE, dtype=f32)
    return {
        'x_prompt': nrm((BATCH, SEQ, D_MODEL), 1.0),
        'x_sample': nrm((DEC_BATCH, DEC_SEQ, D_MODEL), 1.0),
        'mem_prompt': nrm((BATCH, N_MEM, D_MODEL), 1.0),
        'state_s5_re': nrm((L, DEC_BATCH, S5_GROUPS, S5_STATE), 0.1),
        'state_s5_im': nrm((L, DEC_BATCH, S5_GROUPS, S5_STATE), 0.1),
        'cache_mem_k': nrm((L, DEC_BATCH, N_MEM, XA_HEADS, XA_HEAD_DIM), 1.0),
        'cache_mem_v': nrm((L, DEC_BATCH, N_MEM, XA_HEADS, XA_HEAD_DIM), 1.0),
        'norm_mix_g': gain((L, D_MODEL)),
        'w_in': nrm((L, D_MODEL, IN_PROJ), D_MODEL ** -0.5),
        's5_lambda_re': -0.5 + nrm((L, S5_GROUPS, S5_STATE), 0.01),
        's5_lambda_im': math.pi * n_idx + nrm((L, S5_GROUPS, S5_STATE), 0.01),
        's5_b_re': nrm((L, S5_GROUPS, S5_STATE, S5_GROUP_CH), (2 * S5_GROUP_CH) ** -0.5),
        's5_b_im': nrm((L, S5_GROUPS, S5_STATE, S5_GROUP_CH), (2 * S5_GROUP_CH) ** -0.5),
        's5_c_re': nrm((L, S5_GROUPS, S5_GROUP_CH, S5_STATE), (2 * S5_STATE) ** -0.5),
        's5_c_im': nrm((L, S5_GROUPS, S5_GROUP_CH, S5_STATE), (2 * S5_STATE) ** -0.5),
        's5_d': nrm((L, S5_GROUPS, S5_GROUP_CH), 1.0),
        's5_log_step': jax.random.uniform(next(ks), (L, S5_GROUPS), f32, math.log(1e-3), math.log(1e-1)),
        's5_w_glu': nrm((L, S5_WIDTH, 2 * S5_WIDTH), S5_WIDTH ** -0.5),
        'gm_ln_g': gain((L, GM_HEADS, GM_HEAD_DIM)),
        'gm_ln_b': nrm((L, GM_HEADS, GM_HEAD_DIM), 0.02),
        'gm_w_s': nrm((L, GM_HEADS, GM_CHUNK, GM_CHUNK), GM_CHUNK ** -0.5),
        'gm_b_s': gain((L, GM_HEADS, GM_CHUNK)),
        'out_norm_g': gain((L, MIX_WIDTH)),
        'w_out': nrm((L, MIX_WIDTH, D_MODEL), MIX_WIDTH ** -0.5),
        'xa_norm_g': gain((L, D_MODEL)),
        'mem_norm_g': gain((L, D_MODEL)),
        'w_xq': nrm((L, D_MODEL, D_MODEL), D_MODEL ** -0.5),
        'w_xk': nrm((L, D_MODEL, D_MODEL), D_MODEL ** -0.5),
        'w_xv': nrm((L, D_MODEL, D_MODEL), D_MODEL ** -0.5),
        'w_xo': nrm((L, D_MODEL, D_MODEL), D_MODEL ** -0.5),
        'ffn_norm_g': gain((L, D_MODEL)),
        'router_w': nrm((L, D_MODEL, N_EXPERTS), D_MODEL ** -0.5),
        'router_b': nrm((L, N_EXPERTS), 0.01),
        'e_w_gate': nrm((L, N_EXPERTS, D_MODEL, D_FF), D_MODEL ** -0.5),
        'e_b_gate': nrm((L, N_EXPERTS, D_FF), 0.01),
        'e_w_up': nrm((L, N_EXPERTS, D_MODEL, D_FF), D_MODEL ** -0.5),
        'e_b_up': nrm((L, N_EXPERTS, D_FF), 0.01),
        'e_w_down': nrm((L, N_EXPERTS, D_FF, D_MODEL), D_FF ** -0.5),
        'e_b_down': nrm((L, N_EXPERTS, D_MODEL), 0.01),
        'final_norm_g': gain((D_MODEL,)),
    }


def reference(x_prompt, x_sample, mem_prompt, state_s5_re, state_s5_im, cache_mem_k, cache_mem_v,
              norm_mix_g, w_in, s5_lambda_re, s5_lambda_im, s5_b_re, s5_b_im, s5_c_re, s5_c_im, s5_d,
              s5_log_step, s5_w_glu, gm_ln_g, gm_ln_b, gm_w_s, gm_b_s, out_norm_g, w_out,
              xa_norm_g, mem_norm_g, w_xq, w_xk, w_xv, w_xo, ffn_norm_g, router_w, router_b,
              e_w_gate, e_b_gate, e_w_up, e_b_up, e_w_down, e_b_down, final_norm_g):
    yp = x_prompt
    ys = x_sample
    s5r_p, s5i_p, mk_p, mv_p = [], [], [], []
    s5r_s, s5i_s, gmv_s = [], [], []
    for l in range(DEPTH):
        p = {
            'norm_mix_g': norm_mix_g[l], 'w_in': w_in[l],
            's5_lambda_re': s5_lambda_re[l], 's5_lambda_im': s5_lambda_im[l],
            's5_b_re': s5_b_re[l], 's5_b_im': s5_b_im[l], 's5_c_re': s5_c_re[l], 's5_c_im': s5_c_im[l],
            's5_d': s5_d[l], 's5_log_step': s5_log_step[l], 's5_w_glu': s5_w_glu[l],
            'gm_ln_g': gm_ln_g[l], 'gm_ln_b': gm_ln_b[l], 'gm_w_s': gm_w_s[l], 'gm_b_s': gm_b_s[l],
            'out_norm_g': out_norm_g[l], 'w_out': w_out[l],
            'xa_norm_g': xa_norm_g[l], 'w_xq': w_xq[l], 'w_xo': w_xo[l],
            'ffn_norm_g': ffn_norm_g[l], 'router_w': router_w[l], 'router_b': router_b[l],
            'e_w_gate': e_w_gate[l], 'e_b_gate': e_b_gate[l], 'e_w_up': e_w_up[l], 'e_b_up': e_b_up[l],
            'e_w_down': e_w_down[l], 'e_b_down': e_b_down[l],
        }
        mk, mv = memory_kv(mem_prompt, mem_norm_g[l], w_xk[l], w_xv[l])
        yp, hr_p, hi_p, _ = trunk_layer(yp, mk, mv, None, None, p)
        s5r_p.append(hr_p)
        s5i_p.append(hi_p)
        mk_p.append(mk)
        mv_p.append(mv)
        ys, hr_s, hi_s, vn_s = trunk_layer(ys, cache_mem_k[l], cache_mem_v[l],
                                           state_s5_re[l], state_s5_im[l], p)
        s5r_s.append(hr_s)
        s5i_s.append(hi_s)
        gmv_s.append(vn_s)
    y_prompt = rms_norm(yp, final_norm_g)
    y_sample = rms_norm(ys, final_norm_g)
    return (y_prompt, y_sample, jnp.stack(s5r_p), jnp.stack(s5i_p), jnp.stack(mk_p), jnp.stack(mv_p),
            jnp.stack(s5r_s), jnp.stack(s5i_s), jnp.stack(gmv_s))
```

```python
import functools

import jax
import jax.numpy as jnp
from jax import lax
from jax.experimental import pallas as pl
from jax.experimental.pallas import tpu as pltpu

F32 = jnp.float32
BF16 = jnp.bfloat16

D_MODEL = 1024
BATCH = 4
SEQ = 4096
DEPTH = 2
DEC_BATCH = 16
DEC_SEQ = 32
S5_GROUPS = 32
S5_GROUP_CH = 16
S5_STATE = 64
S5_WIDTH = S5_GROUPS * S5_GROUP_CH
S5_COLS = S5_GROUPS * S5_STATE
GM_HEADS = 4
GM_HEAD_DIM = 128
GM_WIDTH = GM_HEADS * GM_HEAD_DIM
GM_CHUNK = 128
IN_PROJ = S5_WIDTH + 2 * GM_WIDTH
N_MEM = 256
XA_HEADS = 4
XA_HEAD_DIM = D_MODEL // XA_HEADS
N_EXPERTS = 32
TOP_K = 4
D_FF = D_MODEL
SWIGLU_LIMIT = 7.0
SWIGLU_ALPHA = 1.702
EPS = 1e-6
EIG_CLIP = -1e-4

N_PROMPT = BATCH * SEQ
N_SAMPLE = DEC_BATCH * DEC_SEQ
N_TOK = N_PROMPT + N_SAMPLE

SUBLANES = 8
LANES = 128
MXU_DIM = 256
VMEM_LIMIT = 48 * 1024 * 1024

ROW_BLOCK = 512
S5_SEG_PROMPT = 64
S5_COL_BLOCK = 512
GM_ROWS_PROMPT = 256
XA_ROWS_PROMPT = 512
XA_SAMPLE_BATCHES = 4
MOE_ROWS = 256
MOE_BLOCKS = -(-N_TOK * TOP_K // MOE_ROWS) + N_EXPERTS
MOE_CAST_ROWS = 128


def _params(*sem):
    return pltpu.CompilerParams(dimension_semantics=sem, vmem_limit_bytes=VMEM_LIMIT)


def _rms(x, g):
    return x * lax.rsqrt(jnp.mean(x * x, axis=-1, keepdims=True) + EPS) * g


def _dot(a, b):
    return jnp.dot(a, b, preferred_element_type=F32)


def _norm_matmul_kernel(x_ref, g_ref, w_ref, o_ref):
    o_ref[...] = _dot(_rms(x_ref[...], g_ref[...]).astype(BF16), w_ref[...])


def _norm_matmul(x, g, w, rows):
    n, d = x.shape
    dout = w.shape[1]
    return pl.pallas_call(
        _norm_matmul_kernel,
        grid=(n // rows,),
        in_specs=[pl.BlockSpec((rows, d), lambda i: (i, 0)),
                  pl.BlockSpec((1, d), lambda i: (0, 0)),
                  pl.BlockSpec((d, dout), lambda i: (0, 0))],
        out_specs=pl.BlockSpec((rows, dout), lambda i: (i, 0)),
        out_shape=jax.ShapeDtypeStruct((n, dout), F32),
        compiler_params=_params("parallel"),
        name="norm_matmul",
    )(x, g.reshape(1, d), w)


def _out_proj_kernel(x_ref, a_ref, b_ref, wa_ref, wb_ref, o_ref):
    o_ref[...] = x_ref[...] + _dot(a_ref[...], wa_ref[...]) + _dot(b_ref[...], wb_ref[...])


def _out_proj(x, a, b, w):
    n, d = x.shape
    wa, wb = w[:S5_WIDTH], w[S5_WIDTH:]
    rows = ROW_BLOCK
    return pl.pallas_call(
        _out_proj_kernel,
        grid=(n // rows,),
        in_specs=[pl.BlockSpec((rows, d), lambda i: (i, 0)),
                  pl.BlockSpec((rows, S5_WIDTH), lambda i: (i, 0)),
                  pl.BlockSpec((rows, GM_WIDTH), lambda i: (i, 0)),
                  pl.BlockSpec((S5_WIDTH, d), lambda i: (0, 0)),
                  pl.BlockSpec((GM_WIDTH, d), lambda i: (0, 0))],
        out_specs=pl.BlockSpec((rows, d), lambda i: (i, 0)),
        out_shape=jax.ShapeDtypeStruct((n, d), F32),
        compiler_params=_params("parallel"),
        name="out_proj",
    )(x, a, b, wa, wb)


def _router_kernel(x_ref, g_ref, w_ref, b_ref, h_ref, l_ref):
    h = _rms(x_ref[...], g_ref[...]).astype(BF16)
    h_ref[...] = h
    l_ref[...] = _dot(h, w_ref[...]) + b_ref[...]


def _router(x, g, w, b):
    n, d = x.shape
    rows = ROW_BLOCK
    return pl.pallas_call(
        _router_kernel,
        grid=(n // rows,),
        in_specs=[pl.BlockSpec((rows, d), lambda i: (i, 0)),
                  pl.BlockSpec((1, d), lambda i: (0, 0)),
                  pl.BlockSpec((d, N_EXPERTS), lambda i: (0, 0)),
                  pl.BlockSpec((1, N_EXPERTS), lambda i: (0, 0))],
        out_specs=[pl.BlockSpec((rows, d), lambda i: (i, 0)),
                   pl.BlockSpec((rows, N_EXPERTS), lambda i: (i, 0))],
        out_shape=[jax.ShapeDtypeStruct((n, d), BF16),
                   jax.ShapeDtypeStruct((n, N_EXPERTS), F32)],
        compiler_params=_params("parallel"),
        name="router",
    )(x, g.reshape(1, d), w, b.reshape(1, N_EXPERTS))


def _final_norm_kernel(x_ref, g_ref, o_ref):
    o_ref[...] = _rms(x_ref[...], g_ref[...])


def _final_norm(x, g):
    n, d = x.shape
    rows = ROW_BLOCK
    return pl.pallas_call(
        _final_norm_kernel,
        grid=(n // rows,),
        in_specs=[pl.BlockSpec((rows, d), lambda i: (i, 0)),
                  pl.BlockSpec((1, d), lambda i: (0, 0))],
        out_specs=pl.BlockSpec((rows, d), lambda i: (i, 0)),
        out_shape=jax.ShapeDtypeStruct((n, d), F32),
        compiler_params=_params("parallel"),
        name="final_norm",
    )(x, g.reshape(1, d))


def _cmul_add(ar, ai, hr, hi, br, bi):
    return ar * hr - ai * hi + br, ar * hi + ai * hr + bi


def _s5_kernel(*refs, seg, chained):
    if chained:
        (u_ref, wbr_ref, wbi_ref, wcr_ref, wci_ref, ar_ref, ai_ref, pr_ref, pi_ref, d_ref,
         wglu_ref, og_ref, o_ref, er_ref, ei_ref,
         up_ref, hr_ref, hi_ref, hbr_ref, hbi_ref, sr_ref, si_ref, op_ref, cr_ref, ci_ref) = refs
    else:
        (u_ref, wbr_ref, wbi_ref, wcr_ref, wci_ref, ar_ref, ai_ref, pr_ref, pi_ref, d_ref,
         wglu_ref, og_ref, h0r_ref, h0i_ref, o_ref, er_ref, ei_ref,
         up_ref, hr_ref, hi_ref, hbr_ref, hbi_ref, sr_ref, si_ref, op_ref) = refs
    rows = seg * SUBLANES
    half = S5_COLS // 2

    n_lane_blk = S5_WIDTH // LANES
    for j in range(SUBLANES):
        for i0 in range(0, seg, SUBLANES):
            for c in range(n_lane_blk):
                up_ref[c, pl.ds(i0 * SUBLANES + j, SUBLANES, stride=SUBLANES), :] = (
                    u_ref[j * seg + i0:j * seg + i0 + SUBLANES, c * LANES:(c + 1) * LANES])

    up = jnp.concatenate([up_ref[c] for c in range(n_lane_blk)], axis=1)
    ub = up.astype(BF16)
    for k in range(2):
        uk = ub[:, k * MXU_DIM:(k + 1) * MXU_DIM]
        hr_ref[:, k * half:(k + 1) * half] = _dot(uk, wbr_ref[k])
        hi_ref[:, k * half:(k + 1) * half] = _dot(uk, wbi_ref[k])

    for cb in range(S5_COLS // S5_COL_BLOCK):
        cols = slice(cb * S5_COL_BLOCK, (cb + 1) * S5_COL_BLOCK)
        ar = jnp.broadcast_to(ar_ref[:, cols], (SUBLANES, S5_COL_BLOCK))
        ai = jnp.broadcast_to(ai_ref[:, cols], (SUBLANES, S5_COL_BLOCK))

        def step(i, carry, cols=cols, ar=ar, ai=ai):
            r = pl.multiple_of(i * SUBLANES, SUBLANES)
            h_r, h_i = _cmul_add(ar, ai, carry[0], carry[1],
                                 hr_ref[pl.ds(r, SUBLANES), cols], hi_ref[pl.ds(r, SUBLANES), cols])
            hr_ref[pl.ds(r, SUBLANES), cols] = h_r
            hi_ref[pl.ds(r, SUBLANES), cols] = h_i
            return h_r, h_i

        zero = jnp.zeros((SUBLANES, S5_COL_BLOCK), F32)
        lax.fori_loop(0, seg, step, (zero, zero), unroll=2)

    end_r = hr_ref[rows - SUBLANES:rows, :]
    end_i = hi_ref[rows - SUBLANES:rows, :]
    as_r = pr_ref[seg - 1:seg, :]
    as_i = pi_ref[seg - 1:seg, :]
    if chained:
        @pl.when(pl.program_id(1) == 0)
        def _():
            cr_ref[...] = jnp.zeros_like(cr_ref)
            ci_ref[...] = jnp.zeros_like(ci_ref)

        cur_r, cur_i = cr_ref[...], ci_ref[...]
        for j in range(SUBLANES):
            sr_ref[j:j + 1, :] = cur_r
            si_ref[j:j + 1, :] = cur_i
            cur_r, cur_i = _cmul_add(as_r, as_i, cur_r, cur_i, end_r[j:j + 1], end_i[j:j + 1])
        cr_ref[...] = cur_r
        ci_ref[...] = cur_i
    else:
        sr_ref[...] = h0r_ref[...]
        si_ref[...] = h0i_ref[...]
    t_r, t_i = _cmul_add(as_r, as_i, sr_ref[...], si_ref[...], end_r, end_i)
    er_ref[...] = t_r
    ei_ref[...] = t_i

    for cb in range(S5_COLS // S5_COL_BLOCK):
        cols = slice(cb * S5_COL_BLOCK, (cb + 1) * S5_COL_BLOCK)
        s_r = jnp.concatenate([sr_ref[:, cols]] * 2, axis=0)
        s_i = jnp.concatenate([si_ref[:, cols]] * 2, axis=0)

        def fix(i2, _, cols=cols, s_r=s_r, s_i=s_i):
            r = pl.multiple_of(i2 * 2 * SUBLANES, 2 * SUBLANES)
            shape = (SUBLANES, S5_COL_BLOCK)
            p_r = jnp.concatenate(
                [jnp.broadcast_to(pr_ref[pl.ds(2 * i2, 1), cols], shape),
                 jnp.broadcast_to(pr_ref[pl.ds(2 * i2 + 1, 1), cols], shape)], axis=0)
            p_i = jnp.concatenate(
                [jnp.broadcast_to(pi_ref[pl.ds(2 * i2, 1), cols], shape),
                 jnp.broadcast_to(pi_ref[pl.ds(2 * i2 + 1, 1), cols], shape)], axis=0)
            h_r, h_i = _cmul_add(p_r, p_i, s_r, s_i,
                                 hr_ref[pl.ds(r, 2 * SUBLANES), cols],
                                 hi_ref[pl.ds(r, 2 * SUBLANES), cols])
            hbr_ref[pl.ds(r, 2 * SUBLANES), cols] = h_r.astype(BF16)
            hbi_ref[pl.ds(r, 2 * SUBLANES), cols] = h_i.astype(BF16)
            return 0

        lax.fori_loop(0, seg // 2, fix, 0)

    slabs_per_window = MXU_DIM // (MXU_DIM // S5_STATE * S5_GROUP_CH)
    ys = []
    for w in range(S5_WIDTH // MXU_DIM):
        acc = d_ref[:, w * MXU_DIM:(w + 1) * MXU_DIM] * jnp.concatenate(
            [up_ref[c] for c in range(w * MXU_DIM // LANES, (w + 1) * MXU_DIM // LANES)], axis=1)
        for s in range(slabs_per_window):
            j = w * slabs_per_window + s
            acc += _dot(hbr_ref[:, j * MXU_DIM:(j + 1) * MXU_DIM], wcr_ref[j])
            acc += _dot(hbi_ref[:, j * MXU_DIM:(j + 1) * MXU_DIM], wci_ref[j])
        ys.append(jax.nn.gelu(acc).astype(BF16))

    z = _dot(jnp.concatenate(ys, axis=1), wglu_ref[...])
    o = _rms(z[:, :S5_WIDTH] * jax.nn.sigmoid(z[:, S5_WIDTH:]), og_ref[...])
    for c in range(n_lane_blk):
        op_ref[c] = o[:, c * LANES:(c + 1) * LANES]

    for j in range(SUBLANES):
        for i0 in range(0, seg, 2 * SUBLANES):
            for c in range(n_lane_blk):
                o_ref[j * seg + i0:j * seg + i0 + 2 * SUBLANES, c * LANES:(c + 1) * LANES] = (
                    op_ref[c, pl.ds(i0 * SUBLANES + j, 2 * SUBLANES, stride=SUBLANES), :].astype(BF16))


def _s5_weights(lam_re, lam_im, b_re, b_im, c_re, c_im, d, log_step, seg):
    step = jnp.exp(log_step)[:, None]
    lr = jnp.minimum(lam_re, EIG_CLIP)
    li = lam_im
    mag = jnp.exp(lr * step)
    ar = mag * jnp.cos(li * step)
    ai = mag * jnp.sin(li * step)
    den = lr * lr + li * li
    fr = ((ar - 1.0) * lr + ai * li) / den
    fi = (ai * lr - (ar - 1.0) * li) / den
    bbar_r = fr[..., None] * b_re - fi[..., None] * b_im
    bbar_i = fr[..., None] * b_im + fi[..., None] * b_re
    t = jnp.arange(1, seg + 1, dtype=F32)[:, None, None]
    pm = jnp.exp(lr * step * t)
    p_r = (pm * jnp.cos(li * step * t)).reshape(seg, S5_COLS)
    p_i = (pm * jnp.sin(li * step * t)).reshape(seg, S5_COLS)

    gk = MXU_DIM // S5_GROUP_CH

    def in_map(bbar):
        bb = bbar.reshape(S5_GROUPS // gk, gk, S5_STATE, S5_GROUP_CH)
        w = jnp.einsum('kgnh,gG->kghGn', bb, jnp.eye(gk, dtype=F32))
        return w.reshape(S5_GROUPS // gk, MXU_DIM, gk * S5_STATE).astype(BF16)

    gs = MXU_DIM // S5_STATE
    n_slab = S5_COLS // MXU_DIM
    per_win = MXU_DIM // (gs * S5_GROUP_CH)

    def out_map(c):
        cc = c.reshape(n_slab // per_win, per_win, gs, S5_GROUP_CH, S5_STATE)
        w = jnp.einsum('wsghn,sS,gG->wsgnSGh', cc, jnp.eye(per_win, dtype=F32), jnp.eye(gs, dtype=F32))
        return w.reshape(n_slab, MXU_DIM, MXU_DIM).astype(BF16)

    return dict(wbr=in_map(bbar_r), wbi=in_map(bbar_i), wcr=out_map(c_re), wci=out_map(-c_im),
                ar=ar.reshape(1, S5_COLS), ai=ai.reshape(1, S5_COLS), pr=p_r, pi=p_i,
                d=d.reshape(1, S5_WIDTH))


def _s5_call(proj, sw, w_glu, og, seg, chained, h0=None, prev=None):
    rows = seg * SUBLANES
    const2 = lambda *_: (0, 0)
    const3 = lambda *_: (0, 0, 0)
    if chained:
        n_chunk = SEQ // rows
        grid = (BATCH, n_chunk)
        row_map = lambda b, c: (b * n_chunk + c, 0)
        end_map = lambda b, c: (b, 0, 0)
        n_end = BATCH
        sem = ("parallel", "arbitrary")
    else:
        grid = (DEC_BATCH // SUBLANES,)
        row_map = lambda k: (N_PROMPT // rows + k, 0)
        end_map = lambda k: (k, 0, 0)
        n_end = DEC_BATCH // SUBLANES
        sem = ("parallel",)
    half = S5_COLS // 2
    in_specs = [pl.BlockSpec((rows, S5_WIDTH), row_map),
                pl.BlockSpec((2, MXU_DIM, half), const3),
                pl.BlockSpec((2, MXU_DIM, half), const3),
                pl.BlockSpec((S5_COLS // MXU_DIM, MXU_DIM, MXU_DIM), const3),
                pl.BlockSpec((S5_COLS // MXU_DIM, MXU_DIM, MXU_DIM), const3),
                pl.BlockSpec((1, S5_COLS), const2),
                pl.BlockSpec((1, S5_COLS), const2),
                pl.BlockSpec((seg, S5_COLS), const2),
                pl.BlockSpec((seg, S5_COLS), const2),
                pl.BlockSpec((1, S5_WIDTH), const2),
                pl.BlockSpec((S5_WIDTH, 2 * S5_WIDTH), const2),
                pl.BlockSpec((1, S5_WIDTH), const2)]
    args = [proj, sw['wbr'], sw['wbi'], sw['wcr'], sw['wci'], sw['ar'], sw['ai'],
            sw['pr'][:seg], sw['pi'][:seg], sw['d'], w_glu, og.reshape(1, S5_WIDTH)]
    scratch = [pltpu.VMEM((S5_WIDTH // LANES, rows, LANES), F32),
               pltpu.VMEM((rows, S5_COLS), F32), pltpu.VMEM((rows, S5_COLS), F32),
               pltpu.VMEM((rows, S5_COLS), BF16), pltpu.VMEM((rows, S5_COLS), BF16),
               pltpu.VMEM((SUBLANES, S5_COLS), F32), pltpu.VMEM((SUBLANES, S5_COLS), F32),
               pltpu.VMEM((S5_WIDTH // LANES, rows, LANES), F32)]
    aliases = {}
    if chained:
        scratch += [pltpu.VMEM((1, S5_COLS), F32), pltpu.VMEM((1, S5_COLS), F32)]
    else:
        in_specs += [pl.BlockSpec((None, SUBLANES, S5_COLS), end_map),
                     pl.BlockSpec((None, SUBLANES, S5_COLS), end_map)]
        args += [h0[0], h0[1]]
    if prev is not None:
        in_specs.append(pl.BlockSpec(memory_space=pl.ANY))
        args.append(prev)
        aliases = {len(args) - 1: 0}
    kern = functools.partial(_s5_kernel, seg=seg, chained=chained)
    if prev is not None:
        kern = _drop_last_input(kern, len(args))
    return pl.pallas_call(
        kern,
        grid=grid,
        in_specs=in_specs,
        out_specs=[pl.BlockSpec((rows, S5_WIDTH), row_map),
                   pl.BlockSpec((None, SUBLANES, S5_COLS), end_map),
                   pl.BlockSpec((None, SUBLANES, S5_COLS), end_map)],
        out_shape=[jax.ShapeDtypeStruct((N_TOK, S5_WIDTH), BF16),
                   jax.ShapeDtypeStruct((n_end, SUBLANES, S5_COLS), F32),
                   jax.ShapeDtypeStruct((n_end, SUBLANES, S5_COLS), F32)],
        scratch_shapes=scratch,
        input_output_aliases=aliases,
        compiler_params=_params(*sem),
        name="s5_chained" if chained else "s5_streams",
    )(*args)


def _drop_last_input(kern, n_in):
    def wrapped(*refs):
        return kern(*refs[:n_in - 1], *refs[n_in:])
    return wrapped


def _gmlp_kernel(*refs, chunk, emit_vn):
    if emit_vn:
        u_ref, v_ref, lng_ref, lnb_ref, ws_ref, bst_ref, og_ref, o_ref, vn_ref, acc_ref = refs
    else:
        u_ref, v_ref, lng_ref, lnb_ref, ws_ref, bst_ref, og_ref, o_ref, acc_ref = refs
    rows = u_ref.shape[0]
    u = jax.nn.gelu(u_ref[...])
    v = jax.nn.gelu(v_ref[...])
    tri = (lax.broadcasted_iota(jnp.int32, (chunk, chunk), 0)
           >= lax.broadcasted_iota(jnp.int32, (chunk, chunk), 1))
    for h in range(GM_HEADS):
        cols = slice(h * GM_HEAD_DIM, (h + 1) * GM_HEAD_DIM)
        vh = v[:, cols]
        mu = jnp.mean(vh, axis=-1, keepdims=True)
        cen = vh - mu
        var = jnp.mean(cen * cen, axis=-1, keepdims=True)
        vn = cen * lax.rsqrt(var + EPS) * lng_ref[:, cols] + lnb_ref[:, cols]
        if emit_vn:
            vn_ref[:, cols] = vn
        w = jnp.where(tri, ws_ref[h], 0.0).astype(BF16)
        vb = vn.astype(BF16)
        bias = bst_ref[:, h:h + 1]
        for c in range(rows // chunk):
            rs = slice(c * chunk, (c + 1) * chunk)
            acc_ref[rs, cols] = u[rs, cols] * (_dot(w, vb[rs]) + bias)
    o_ref[...] = _rms(acc_ref[...], og_ref[...]).astype(BF16)


def _gmlp_call(proj, ln_g, ln_b, w_s, b_s, og, rows, chunk, first_block, n_blocks, emit_vn, prev=None):
    const2 = lambda i: (0, 0)
    in_specs = [pl.BlockSpec((rows, GM_WIDTH), lambda i: (first_block + i, 1)),
                pl.BlockSpec((rows, GM_WIDTH), lambda i: (first_block + i, 2)),
                pl.BlockSpec((1, GM_WIDTH), const2),
                pl.BlockSpec((1, GM_WIDTH), const2),
                pl.BlockSpec((GM_HEADS, chunk, chunk), lambda i: (0, 0, 0)),
                pl.BlockSpec((chunk, GM_HEADS), const2),
                pl.BlockSpec((1, GM_WIDTH), const2)]
    args = [proj, proj, ln_g.reshape(1, GM_WIDTH), ln_b.reshape(1, GM_WIDTH),
            w_s[:, :chunk, :chunk], b_s[:, :chunk].T, og.reshape(1, GM_WIDTH)]
    out_specs = [pl.BlockSpec((rows, GM_WIDTH), lambda i: (first_block + i, 0))]
    out_shape = [jax.ShapeDtypeStruct((N_TOK, GM_WIDTH), BF16)]
    if emit_vn:
        out_specs.append(pl.BlockSpec((rows, GM_WIDTH), lambda i: (i, 0)))
        out_shape.append(jax.ShapeDtypeStruct((n_blocks * rows, GM_WIDTH), F32))
    kern = functools.partial(_gmlp_kernel, chunk=chunk, emit_vn=emit_vn)
    aliases = {}
    if prev is not None:
        in_specs.append(pl.BlockSpec(memory_space=pl.ANY))
        args.append(prev)
        aliases = {len(args) - 1: 0}
        kern = _drop_last_input(kern, len(args))
    return pl.pallas_call(
        kern,
        grid=(n_blocks,),
        in_specs=in_specs,
        out_specs=out_specs,
        out_shape=out_shape,
        scratch_shapes=[pltpu.VMEM((rows, GM_WIDTH), F32)],
        input_output_aliases=aliases,
        compiler_params=_params("parallel"),
        name="gmlp_vn" if emit_vn else "gmlp",
    )(*args)


def _xattn_kernel(x_ref, g_ref, wq_ref, k_ref, v_ref, wo_ref, o_ref, att_ref, *, n_batch):
    rows = x_ref.shape[0]
    tb = rows // n_batch
    x = x_ref[...]
    q = _dot(_rms(x, g_ref[...]).astype(BF16), wq_ref[...]).astype(BF16)
    scale = XA_HEAD_DIM ** -0.5
    for b in range(n_batch):
        rs = slice(b * tb, (b + 1) * tb)
        for h in range(XA_HEADS):
            cols = slice(h * XA_HEAD_DIM, (h + 1) * XA_HEAD_DIM)
            s = lax.dot_general(q[rs, cols], k_ref[b, :, cols], (((1,), (1,)), ((), ())),
                                preferred_element_type=F32) * scale
            e = jnp.exp(s - jnp.max(s, axis=-1, keepdims=True))
            p = e / jnp.sum(e, axis=-1, keepdims=True)
            att_ref[rs, cols] = _dot(p.astype(BF16), v_ref[b, :, cols]).astype(BF16)
    o_ref[...] = x + _dot(att_ref[...], wo_ref[...])


def _xattn_call(x, g, wq, k, v, wo, rows, n_batch, first_block, n_blocks, blocks_per_kv, prev=None):
    const2 = lambda i: (0, 0)
    in_specs = [pl.BlockSpec((rows, D_MODEL), lambda i: (first_block + i, 0)),
                pl.BlockSpec((1, D_MODEL), const2),
                pl.BlockSpec((D_MODEL, D_MODEL), const2),
                pl.BlockSpec((n_batch, N_MEM, D_MODEL), lambda i: (i // blocks_per_kv, 0, 0)),
                pl.BlockSpec((n_batch, N_MEM, D_MODEL), lambda i: (i // blocks_per_kv, 0, 0)),
                pl.BlockSpec((D_MODEL, D_MODEL), const2)]
    args = [x, g.reshape(1, D_MODEL), wq, k, v, wo]
    kern = functools.partial(_xattn_kernel, n_batch=n_batch)
    aliases = {}
    if prev is not None:
        in_specs.append(pl.BlockSpec(memory_space=pl.ANY))
        args.append(prev)
        aliases = {len(args) - 1: 0}
        kern = _drop_last_input(kern, len(args))
    return pl.pallas_call(
        kern,
        grid=(n_blocks,),
        in_specs=in_specs,
        out_specs=pl.BlockSpec((rows, D_MODEL), lambda i: (first_block + i, 0)),
        out_shape=jax.ShapeDtypeStruct((N_TOK, D_MODEL), F32),
        scratch_shapes=[pltpu.VMEM((rows, D_MODEL), BF16)],
        input_output_aliases=aliases,
        compiler_params=_params("parallel"),
        name="xattn",
    )(*args)


def _moe_kernel(be_ref, nu_ref, x_ref, wg_ref, bg_ref, wu_ref, bu_ref, wd_ref, bd_ref, o_ref,
                wgb_ref, wub_ref, wdb_ref):
    i = pl.program_id(0)
    fresh = jnp.logical_or(i == 0, be_ref[i] != be_ref[jnp.maximum(i - 1, 0)])

    @pl.when(jnp.logical_and(i < nu_ref[0], fresh))
    def _():
        def cast(r, _):
            rs = pl.ds(pl.multiple_of(r * MOE_CAST_ROWS, MOE_CAST_ROWS), MOE_CAST_ROWS)
            wgb_ref[rs, :] = wg_ref[rs, :].astype(BF16)
            wub_ref[rs, :] = wu_ref[rs, :].astype(BF16)
            wdb_ref[rs, :] = wd_ref[rs, :].astype(BF16)
            return 0
        lax.fori_loop(0, D_MODEL // MOE_CAST_ROWS, cast, 0)

    @pl.when(i < nu_ref[0])
    def _():
        x = x_ref[...]
        g = jnp.minimum(_dot(x, wgb_ref[...]) + bg_ref[...], SWIGLU_LIMIT)
        u = jnp.clip(_dot(x, wub_ref[...]) + bu_ref[...], -SWIGLU_LIMIT, SWIGLU_LIMIT)
        a = g * jax.nn.sigmoid(SWIGLU_ALPHA * g) * (u + 1.0)
        o_ref[...] = _dot(a.astype(BF16), wdb_ref[...]) + bd_ref[...]


def _moe_call(layer, xs, blk_exp, n_used, w_gate, b_gate, w_up, b_up, w_down, b_down):
    def row_map(i, be, nu):
        return (jnp.minimum(i, nu[0] - 1), 0)

    def w_map(i, be, nu):
        return (layer, be[i], 0, 0)

    w_spec = pl.BlockSpec((None, None, D_MODEL, D_FF), w_map)
    b_spec = pl.BlockSpec((None, None, 1, D_FF), w_map)
    grid_spec = pltpu.PrefetchScalarGridSpec(
        num_scalar_prefetch=2,
        grid=(MOE_BLOCKS,),
        in_specs=[pl.BlockSpec((MOE_ROWS, D_MODEL), row_map),
                  w_spec, b_spec, w_spec, b_spec, w_spec, b_spec],
        out_specs=pl.BlockSpec((MOE_ROWS, D_MODEL), row_map),
        scratch_shapes=[pltpu.VMEM((D_MODEL, D_FF), BF16)] * 3,
    )
    shape4 = (DEPTH, N_EXPERTS, 1, D_FF)
    return pl.pallas_call(
        _moe_kernel,
        grid_spec=grid_spec,
        out_shape=jax.ShapeDtypeStruct((MOE_BLOCKS * MOE_ROWS, D_MODEL), F32),
        compiler_params=pltpu.CompilerParams(dimension_semantics=("arbitrary",),
                                             vmem_limit_bytes=56 * 1024 * 1024),
        name="moe_experts",
    )(blk_exp, n_used, xs, w_gate, b_gate.reshape(shape4), w_up, b_up.reshape(shape4),
      w_down, b_down.reshape(shape4))


def _route(logits):
    top_val, top_idx = lax.top_k(logits, TOP_K)
    gates = jax.nn.softmax(top_val, axis=-1)
    onehot = jnp.sum((top_idx[:, :, None] == jnp.arange(N_EXPERTS, dtype=jnp.int32)).astype(jnp.int32),
                     axis=1)
    counts = jnp.sum(onehot, axis=0)
    rank = jnp.cumsum(onehot, axis=0) - onehot
    padded = ((counts + MOE_ROWS - 1) // MOE_ROWS) * MOE_ROWS
    pend = jnp.cumsum(padded)
    pstart = pend - padded
    dest = jnp.take_along_axis(rank + pstart[None, :], top_idx, axis=1)
    tok = jnp.repeat(jnp.arange(N_TOK, dtype=jnp.int32), TOP_K)
    row_tok = jnp.zeros((MOE_BLOCKS * MOE_ROWS,), jnp.int32).at[dest.reshape(-1)].set(tok)
    n_used = (pend[-1] // MOE_ROWS).astype(jnp.int32)
    blk = jnp.minimum(jnp.arange(MOE_BLOCKS, dtype=jnp.int32), n_used - 1) * MOE_ROWS
    blk_exp = jnp.minimum(jnp.searchsorted(pend, blk, side='right'), N_EXPERTS - 1).astype(jnp.int32)
    return gates, dest, row_tok, blk_exp, n_used.reshape(1)


def kernel(x_prompt, x_sample, mem_prompt, state_s5_re, state_s5_im, cache_mem_k, cache_mem_v, norm_mix_g, w_in, s5_lambda_re, s5_lambda_im, s5_b_re, s5_b_im, s5_c_re, s5_c_im, s5_d, s5_log_step, s5_w_glu, gm_ln_g, gm_ln_b, gm_w_s, gm_b_s, out_norm_g, w_out, xa_norm_g, mem_norm_g, w_xq, w_xk, w_xv, w_xo, ffn_norm_g, router_w, router_b, e_w_gate, e_b_gate, e_w_up, e_b_up, e_w_down, e_b_down, final_norm_g):
    x = jnp.concatenate([x_prompt.reshape(N_PROMPT, D_MODEL), x_sample.reshape(N_SAMPLE, D_MODEL)], axis=0)
    mem = mem_prompt.reshape(BATCH * N_MEM, D_MODEL)
    outs = {k: [] for k in ('s5r_p', 's5i_p', 'mk_p', 'mv_p', 's5r_s', 's5i_s', 'gmv_s')}
    for l in range(DEPTH):
        proj = _norm_matmul(x, norm_mix_g[l], w_in[l].astype(BF16), ROW_BLOCK)
        sw = _s5_weights(s5_lambda_re[l], s5_lambda_im[l], s5_b_re[l], s5_b_im[l], s5_c_re[l], s5_c_im[l],
                         s5_d[l], s5_log_step[l], S5_SEG_PROMPT)
        w_glu = s5_w_glu[l].astype(BF16)
        og = out_norm_g[l]
        ms5, er_p, ei_p = _s5_call(proj, sw, w_glu, og[:S5_WIDTH], S5_SEG_PROMPT, True)
        h0 = (state_s5_re[l].reshape(DEC_BATCH // SUBLANES, SUBLANES, S5_COLS),
              state_s5_im[l].reshape(DEC_BATCH // SUBLANES, SUBLANES, S5_COLS))
        ms5, er_s, ei_s = _s5_call(proj, sw, w_glu, og[:S5_WIDTH], DEC_SEQ, False, h0=h0, prev=ms5)
        outs['s5r_p'].append(er_p[:, SUBLANES - 1].reshape(BATCH, S5_GROUPS, S5_STATE))
        outs['s5i_p'].append(ei_p[:, SUBLANES - 1].reshape(BATCH, S5_GROUPS, S5_STATE))
        outs['s5r_s'].append(er_s.reshape(DEC_BATCH, S5_GROUPS, S5_STATE))
        outs['s5i_s'].append(ei_s.reshape(DEC_BATCH, S5_GROUPS, S5_STATE))

        (mgm,) = _gmlp_call(proj, gm_ln_g[l], gm_ln_b[l], gm_w_s[l], gm_b_s[l], og[S5_WIDTH:],
                            GM_ROWS_PROMPT, GM_CHUNK, 0, N_PROMPT // GM_ROWS_PROMPT, False)
        mgm, vn = _gmlp_call(proj, gm_ln_g[l], gm_ln_b[l], gm_w_s[l], gm_b_s[l], og[S5_WIDTH:],
                             N_SAMPLE, DEC_SEQ, N_PROMPT // N_SAMPLE, 1, True, prev=mgm)
        outs['gmv_s'].append(vn.reshape(DEC_BATCH, DEC_SEQ, GM_HEADS, GM_HEAD_DIM))
        x = _out_proj(x, ms5, mgm, w_out[l].astype(BF16))

        mk = _norm_matmul(mem, mem_norm_g[l], w_xk[l].astype(BF16), ROW_BLOCK)
        mv = _norm_matmul(mem, mem_norm_g[l], w_xv[l].astype(BF16), ROW_BLOCK)
        outs['mk_p'].append(mk.reshape(BATCH, N_MEM, XA_HEADS, XA_HEAD_DIM))
        outs['mv_p'].append(mv.reshape(BATCH, N_MEM, XA_HEADS, XA_HEAD_DIM))
        wq = w_xq[l].astype(BF16)
        wo = w_xo[l].astype(BF16)
        xa = _xattn_call(x, xa_norm_g[l], wq, mk.reshape(BATCH, N_MEM, D_MODEL).astype(BF16),
                         mv.reshape(BATCH, N_MEM, D_MODEL).astype(BF16), wo,
                         XA_ROWS_PROMPT, 1, 0, N_PROMPT // XA_ROWS_PROMPT, SEQ // XA_ROWS_PROMPT)
        rows_s = XA_SAMPLE_BATCHES * DEC_SEQ
        x = _xattn_call(x, xa_norm_g[l], wq,
                        cache_mem_k[l].reshape(DEC_BATCH, N_MEM, D_MODEL).astype(BF16),
                        cache_mem_v[l].reshape(DEC_BATCH, N_MEM, D_MODEL).astype(BF16), wo,
                        rows_s, XA_SAMPLE_BATCHES, N_PROMPT // rows_s, N_SAMPLE // rows_s, 1, prev=xa)

        h, logits = _router(x, ffn_norm_g[l], router_w[l].astype(BF16), router_b[l])
        gates, dest, row_tok, blk_exp, n_used = _route(logits)
        ys = _moe_call(l, h[row_tok], blk_exp, n_used, e_w_gate, e_b_gate, e_w_up, e_b_up, e_w_down, e_b_down)
        x = x + jnp.sum(ys[dest] * gates[:, :, None], axis=1)

    y = _final_norm(x, final_norm_g)
    y_prompt = y[:N_PROMPT].reshape(BATCH, SEQ, D_MODEL)
    y_sample = y[N_PROMPT:].reshape(DEC_BATCH, DEC_SEQ, D_MODEL)
    st = jnp.stack
    return (y_prompt, y_sample, st(outs['s5r_p']), st(outs['s5i_p']), st(outs['mk_p']), st(outs['mv_p']),
            st(outs['s5r_s']), st(outs['s5i_s']), st(outs['gmv_s']))
```

```python
import functools

import jax
import jax.numpy as jnp
from jax import lax
from jax.experimental import pallas as pl
from jax.experimental.pallas import tpu as pltpu

F32 = jnp.float32
BF16 = jnp.bfloat16

D_MODEL = 1024
BATCH = 4
SEQ = 4096
DEPTH = 2
DEC_BATCH = 16
DEC_SEQ = 32
S5_GROUPS = 32
S5_GROUP_CH = 16
S5_STATE = 64
S5_WIDTH = S5_GROUPS * S5_GROUP_CH
S5_COLS = S5_GROUPS * S5_STATE
GM_HEADS = 4
GM_HEAD_DIM = 128
GM_WIDTH = GM_HEADS * GM_HEAD_DIM
GM_CHUNK = 128
IN_PROJ = S5_WIDTH + 2 * GM_WIDTH
N_MEM = 256
XA_HEADS = 4
XA_HEAD_DIM = D_MODEL // XA_HEADS
N_EXPERTS = 32
TOP_K = 4
D_FF = D_MODEL
SWIGLU_LIMIT = 7.0
SWIGLU_ALPHA = 1.702
EPS = 1e-6
EIG_CLIP = -1e-4

N_PROMPT = BATCH * SEQ
N_SAMPLE = DEC_BATCH * DEC_SEQ
N_TOK = N_PROMPT + N_SAMPLE

SUBLANES = 8
LANES = 128
MXU_DIM = 256
VMEM_LIMIT = 48 * 1024 * 1024

ROW_BLOCK = 512
S5_SEG_PROMPT = 64
S5_COL_BLOCK = 512
GM_ROWS_PROMPT = 256
XA_ROWS_PROMPT = 512
XA_SAMPLE_BATCHES = 4
MOE_ROWS = 256
MOE_BLOCKS = -(-N_TOK * TOP_K // MOE_ROWS) + N_EXPERTS
MOE_CAST_ROWS = 128


def _params(*sem):
    return pltpu.CompilerParams(dimension_semantics=sem, vmem_limit_bytes=VMEM_LIMIT)


def _rms(x, g):
    return x * lax.rsqrt(jnp.mean(x * x, axis=-1, keepdims=True) + EPS) * g


def _dot(a, b):
    return jnp.dot(a, b, preferred_element_type=F32)


def _norm_matmul_kernel(x_ref, g_ref, w_ref, o_ref):
    o_ref[...] = _dot(_rms(x_ref[...], g_ref[...]).astype(BF16), w_ref[...])


def _norm_matmul(x, g, w, rows):
    n, d = x.shape
    dout = w.shape[1]
    return pl.pallas_call(
        _norm_matmul_kernel,
        grid=(n // rows,),
        in_specs=[pl.BlockSpec((rows, d), lambda i: (i, 0)),
                  pl.BlockSpec((1, d), lambda i: (0, 0)),
                  pl.BlockSpec((d, dout), lambda i: (0, 0))],
        out_specs=pl.BlockSpec((rows, dout), lambda i: (i, 0)),
        out_shape=jax.ShapeDtypeStruct((n, dout), F32),
        compiler_params=_params("parallel"),
        name="norm_matmul",
    )(x, g.reshape(1, d), w)


def _out_proj_kernel(x_ref, a_ref, b_ref, wa_ref, wb_ref, o_ref):
    o_ref[...] = x_ref[...] + _dot(a_ref[...], wa_ref[...]) + _dot(b_ref[...], wb_ref[...])


def _out_proj(x, a, b, w):
    n, d = x.shape
    wa, wb = w[:S5_WIDTH], w[S5_WIDTH:]
    rows = ROW_BLOCK
    return pl.pallas_call(
        _out_proj_kernel,
        grid=(n // rows,),
        in_specs=[pl.BlockSpec((rows, d), lambda i: (i, 0)),
                  pl.BlockSpec((rows, S5_WIDTH), lambda i: (i, 0)),
                  pl.BlockSpec((rows, GM_WIDTH), lambda i: (i, 0)),
                  pl.BlockSpec((S5_WIDTH, d), lambda i: (0, 0)),
                  pl.BlockSpec((GM_WIDTH, d), lambda i: (0, 0))],
        out_specs=pl.BlockSpec((rows, d), lambda i: (i, 0)),
        out_shape=jax.ShapeDtypeStruct((n, d), F32),
        compiler_params=_params("parallel"),
        name="out_proj",
    )(x, a, b, wa, wb)


def _router_kernel(x_ref, g_ref, wt_ref, b_ref, h_ref, e_ref, gate_ref, cnt_ref):
    h = _rms(x_ref[...], g_ref[...])
    h_ref[...] = h
    logits = lax.dot_general(wt_ref[...], h.astype(BF16), (((1,), (1,)), ((), ())),
                             preferred_element_type=F32) + b_ref[...]
    sub = lax.broadcasted_iota(jnp.int32, logits.shape, 0)
    work = logits
    chosen = jnp.zeros(logits.shape, F32)
    vals, idxs = [], []
    for _ in range(TOP_K):
        m = jnp.max(work, axis=0, keepdims=True)
        idx = jnp.min(jnp.where(work == m, sub, N_EXPERTS), axis=0, keepdims=True)
        sel = sub == idx
        vals.append(m)
        idxs.append(idx)
        work = jnp.where(sel, -jnp.inf, work)
        chosen = jnp.where(sel, 1.0, chosen)
    ex = [jnp.exp(v - vals[0]) for v in vals]
    den = ex[0] + ex[1] + ex[2] + ex[3]
    gate_ref[...] = jnp.concatenate([e / den for e in ex], axis=0)
    e_ref[...] = jnp.concatenate(idxs, axis=0)
    cnt_ref[...] = jnp.broadcast_to(jnp.sum(chosen, axis=1, keepdims=True), (N_EXPERTS, LANES))


def _router(x, g, w, b):
    n, d = x.shape
    rows = ROW_BLOCK
    return pl.pallas_call(
        _router_kernel,
        grid=(n // rows,),
        in_specs=[pl.BlockSpec((rows, d), lambda i: (i, 0)),
                  pl.BlockSpec((1, d), lambda i: (0, 0)),
                  pl.BlockSpec((N_EXPERTS, d), lambda i: (0, 0)),
                  pl.BlockSpec((N_EXPERTS, 1), lambda i: (0, 0))],
        out_specs=[pl.BlockSpec((rows, d), lambda i: (i, 0)),
                   pl.BlockSpec((TOP_K, rows), lambda i: (0, i)),
                   pl.BlockSpec((TOP_K, rows), lambda i: (0, i)),
                   pl.BlockSpec((None, N_EXPERTS, LANES), lambda i: (i, 0, 0))],
        out_shape=[jax.ShapeDtypeStruct((n, d), F32),
                   jax.ShapeDtypeStruct((TOP_K, n), jnp.int32),
                   jax.ShapeDtypeStruct((TOP_K, n), F32),
                   jax.ShapeDtypeStruct((n // rows, N_EXPERTS, LANES), F32)],
        compiler_params=_params("parallel"),
        name="router",
    )(x, g.reshape(1, d), w.T.astype(BF16), b.reshape(N_EXPERTS, 1))


def _dest_kernel(e_ref, base_ref, tri_ref, d_ref):
    e = e_ref[...]
    sub = lax.broadcasted_iota(jnp.int32, (N_EXPERTS, e.shape[1]), 0)
    hot = [sub == e[k:k + 1, :] for k in range(TOP_K)]
    chosen = sum(jnp.where(hk, 1.0, 0.0) for hk in hot)
    pos = base_ref[...] + _dot(chosen.astype(BF16), tri_ref[...])
    d_ref[...] = jnp.concatenate(
        [jnp.sum(jnp.where(hk, pos, 0.0), axis=0, keepdims=True) for hk in hot], axis=0).astype(jnp.int32)


def _dest(e_t, base):
    n = e_t.shape[1]
    rows = ROW_BLOCK
    tri = jnp.triu(jnp.ones((rows, rows), F32), 1).astype(BF16)
    return pl.pallas_call(
        _dest_kernel,
        grid=(n // rows,),
        in_specs=[pl.BlockSpec((TOP_K, rows), lambda i: (0, i)),
                  pl.BlockSpec((None, N_EXPERTS, 1), lambda i: (i, 0, 0)),
                  pl.BlockSpec((rows, rows), lambda i: (0, 0))],
        out_specs=pl.BlockSpec((TOP_K, rows), lambda i: (0, i)),
        out_shape=jax.ShapeDtypeStruct((TOP_K, n), jnp.int32),
        compiler_params=_params("parallel"),
        name="moe_dest",
    )(e_t, base, tri)


def _combine_kernel(*refs, final):
    if final:
        x_ref, g0, g1, g2, g3, gate_ref, fg_ref, o_ref = refs
    else:
        x_ref, g0, g1, g2, g3, gate_ref, o_ref = refs
    acc = x_ref[...]
    for k, g in enumerate((g0, g1, g2, g3)):
        acc = acc + gate_ref[:, k:k + 1] * g[...]
    o_ref[...] = _rms(acc, fg_ref[...]) if final else acc


def _combine(x, picked, gates, final_g=None):
    n, d = x.shape
    rows = ROW_BLOCK
    picked = picked.reshape(TOP_K, n, d)
    final = final_g is not None
    in_specs = [pl.BlockSpec((rows, d), lambda i: (i, 0))]
    in_specs += [pl.BlockSpec((None, rows, d), functools.partial(lambda i, k: (k, i, 0), k=k))
                 for k in range(TOP_K)]
    in_specs.append(pl.BlockSpec((rows, TOP_K), lambda i: (i, 0)))
    args = [x, picked, picked, picked, picked, gates]
    if final:
        in_specs.append(pl.BlockSpec((1, d), lambda i: (0, 0)))
        args.append(final_g.reshape(1, d))
    return pl.pallas_call(
        functools.partial(_combine_kernel, final=final),
        grid=(n // rows,),
        in_specs=in_specs,
        out_specs=pl.BlockSpec((rows, d), lambda i: (i, 0)),
        out_shape=jax.ShapeDtypeStruct((n, d), F32),
        compiler_params=_params("parallel"),
        name="moe_combine",
    )(*args)


def _cmul_add(ar, ai, hr, hi, br, bi):
    return ar * hr - ai * hi + br, ar * hi + ai * hr + bi


def _s5_kernel(*refs, seg, chained):
    if chained:
        (u_ref, wbr_ref, wbi_ref, wcr_ref, wci_ref, ar_ref, ai_ref, pr_ref, pi_ref, d_ref,
         wglu_ref, og_ref, o_ref, er_ref, ei_ref,
         up_ref, hr_ref, hi_ref, hbr_ref, hbi_ref, sr_ref, si_ref, op_ref, cr_ref, ci_ref) = refs
    else:
        (u_ref, wbr_ref, wbi_ref, wcr_ref, wci_ref, ar_ref, ai_ref, pr_ref, pi_ref, d_ref,
         wglu_ref, og_ref, h0r_ref, h0i_ref, o_ref, er_ref, ei_ref,
         up_ref, hr_ref, hi_ref, hbr_ref, hbi_ref, sr_ref, si_ref, op_ref) = refs
    rows = seg * SUBLANES
    half = S5_COLS // 2

    n_lane_blk = S5_WIDTH // LANES
    for j in range(SUBLANES):
        for i0 in range(0, seg, SUBLANES):
            for c in range(n_lane_blk):
                up_ref[c, pl.ds(i0 * SUBLANES + j, SUBLANES, stride=SUBLANES), :] = (
                    u_ref[j * seg + i0:j * seg + i0 + SUBLANES, c * LANES:(c + 1) * LANES])

    up = jnp.concatenate([up_ref[c] for c in range(n_lane_blk)], axis=1)
    ub = up.astype(BF16)
    for k in range(2):
        uk = ub[:, k * MXU_DIM:(k + 1) * MXU_DIM]
        hr_ref[:, k * half:(k + 1) * half] = _dot(uk, wbr_ref[k])
        hi_ref[:, k * half:(k + 1) * half] = _dot(uk, wbi_ref[k])

    for cb in range(S5_COLS // S5_COL_BLOCK):
        cols = slice(cb * S5_COL_BLOCK, (cb + 1) * S5_COL_BLOCK)
        ar = jnp.broadcast_to(ar_ref[:, cols], (SUBLANES, S5_COL_BLOCK))
        ai = jnp.broadcast_to(ai_ref[:, cols], (SUBLANES, S5_COL_BLOCK))

        def step(i, carry, cols=cols, ar=ar, ai=ai):
            r = pl.multiple_of(i * SUBLANES, SUBLANES)
            h_r, h_i = _cmul_add(ar, ai, carry[0], carry[1],
                                 hr_ref[pl.ds(r, SUBLANES), cols], hi_ref[pl.ds(r, SUBLANES), cols])
            hr_ref[pl.ds(r, SUBLANES), cols] = h_r
            hi_ref[pl.ds(r, SUBLANES), cols] = h_i
            return h_r, h_i

        zero = jnp.zeros((SUBLANES, S5_COL_BLOCK), F32)
        lax.fori_loop(0, seg, step, (zero, zero), unroll=2)

    end_r = hr_ref[rows - SUBLANES:rows, :]
    end_i = hi_ref[rows - SUBLANES:rows, :]
    as_r = pr_ref[seg - 1:seg, :]
    as_i = pi_ref[seg - 1:seg, :]
    if chained:
        @pl.when(pl.program_id(1) == 0)
        def _():
            cr_ref[...] = jnp.zeros_like(cr_ref)
            ci_ref[...] = jnp.zeros_like(ci_ref)

        cur_r, cur_i = cr_ref[...], ci_ref[...]
        for j in range(SUBLANES):
            sr_ref[j:j + 1, :] = cur_r
            si_ref[j:j + 1, :] = cur_i
            cur_r, cur_i = _cmul_add(as_r, as_i, cur_r, cur_i, end_r[j:j + 1], end_i[j:j + 1])
        cr_ref[...] = cur_r
        ci_ref[...] = cur_i
    else:
        sr_ref[...] = h0r_ref[...]
        si_ref[...] = h0i_ref[...]
    t_r, t_i = _cmul_add(as_r, as_i, sr_ref[...], si_ref[...], end_r, end_i)
    er_ref[...] = t_r
    ei_ref[...] = t_i

    for cb in range(S5_COLS // S5_COL_BLOCK):
        cols = slice(cb * S5_COL_BLOCK, (cb + 1) * S5_COL_BLOCK)
        s_r = jnp.concatenate([sr_ref[:, cols]] * 2, axis=0)
        s_i = jnp.concatenate([si_ref[:, cols]] * 2, axis=0)

        def fix(i2, _, cols=cols, s_r=s_r, s_i=s_i):
            r = pl.multiple_of(i2 * 2 * SUBLANES, 2 * SUBLANES)
            shape = (SUBLANES, S5_COL_BLOCK)
            p_r = jnp.concatenate(
                [jnp.broadcast_to(pr_ref[pl.ds(2 * i2, 1), cols], shape),
                 jnp.broadcast_to(pr_ref[pl.ds(2 * i2 + 1, 1), cols], shape)], axis=0)
            p_i = jnp.concatenate(
                [jnp.broadcast_to(pi_ref[pl.ds(2 * i2, 1), cols], shape),
                 jnp.broadcast_to(pi_ref[pl.ds(2 * i2 + 1, 1), cols], shape)], axis=0)
            h_r, h_i = _cmul_add(p_r, p_i, s_r, s_i,
                                 hr_ref[pl.ds(r, 2 * SUBLANES), cols],
                                 hi_ref[pl.ds(r, 2 * SUBLANES), cols])
            hbr_ref[pl.ds(r, 2 * SUBLANES), cols] = h_r.astype(BF16)
            hbi_ref[pl.ds(r, 2 * SUBLANES), cols] = h_i.astype(BF16)
            return 0

        lax.fori_loop(0, seg // 2, fix, 0)

    slabs_per_window = MXU_DIM // (MXU_DIM // S5_STATE * S5_GROUP_CH)
    ys = []
    for w in range(S5_WIDTH // MXU_DIM):
        acc = d_ref[:, w * MXU_DIM:(w + 1) * MXU_DIM] * jnp.concatenate(
            [up_ref[c] for c in range(w * MXU_DIM // LANES, (w + 1) * MXU_DIM // LANES)], axis=1)
        for s in range(slabs_per_window):
            j = w * slabs_per_window + s
            acc += _dot(hbr_ref[:, j * MXU_DIM:(j + 1) * MXU_DIM], wcr_ref[j])
            acc += _dot(hbi_ref[:, j * MXU_DIM:(j + 1) * MXU_DIM], wci_ref[j])
        ys.append(jax.nn.gelu(acc).astype(BF16))

    z = _dot(jnp.concatenate(ys, axis=1), wglu_ref[...])
    o = _rms(z[:, :S5_WIDTH] * jax.nn.sigmoid(z[:, S5_WIDTH:]), og_ref[...])
    for c in range(n_lane_blk):
        op_ref[c] = o[:, c * LANES:(c + 1) * LANES]

    for j in range(SUBLANES):
        for i0 in range(0, seg, 2 * SUBLANES):
            for c in range(n_lane_blk):
                o_ref[j * seg + i0:j * seg + i0 + 2 * SUBLANES, c * LANES:(c + 1) * LANES] = (
                    op_ref[c, pl.ds(i0 * SUBLANES + j, 2 * SUBLANES, stride=SUBLANES), :].astype(BF16))


def _s5_weights(lam_re, lam_im, b_re, b_im, c_re, c_im, d, log_step, seg):
    step = jnp.exp(log_step)[:, None]
    lr = jnp.minimum(lam_re, EIG_CLIP)
    li = lam_im
    mag = jnp.exp(lr * step)
    ar = mag * jnp.cos(li * step)
    ai = mag * jnp.sin(li * step)
    den = lr * lr + li * li
    fr = ((ar - 1.0) * lr + ai * li) / den
    fi = (ai * lr - (ar - 1.0) * li) / den
    bbar_r = fr[..., None] * b_re - fi[..., None] * b_im
    bbar_i = fr[..., None] * b_im + fi[..., None] * b_re
    t = jnp.arange(1, seg + 1, dtype=F32)[:, None, None]
    pm = jnp.exp(lr * step * t)
    p_r = (pm * jnp.cos(li * step * t)).reshape(seg, S5_COLS)
    p_i = (pm * jnp.sin(li * step * t)).reshape(seg, S5_COLS)

    gk = MXU_DIM // S5_GROUP_CH

    def in_map(bbar):
        bb = bbar.reshape(S5_GROUPS // gk, gk, S5_STATE, S5_GROUP_CH)
        w = jnp.einsum('kgnh,gG->kghGn', bb, jnp.eye(gk, dtype=F32))
        return w.reshape(S5_GROUPS // gk, MXU_DIM, gk * S5_STATE).astype(BF16)

    gs = MXU_DIM // S5_STATE
    n_slab = S5_COLS // MXU_DIM
    per_win = MXU_DIM // (gs * S5_GROUP_CH)

    def out_map(c):
        cc = c.reshape(n_slab // per_win, per_win, gs, S5_GROUP_CH, S5_STATE)
        w = jnp.einsum('wsghn,sS,gG->wsgnSGh', cc, jnp.eye(per_win, dtype=F32), jnp.eye(gs, dtype=F32))
        return w.reshape(n_slab, MXU_DIM, MXU_DIM).astype(BF16)

    return dict(wbr=in_map(bbar_r), wbi=in_map(bbar_i), wcr=out_map(c_re), wci=out_map(-c_im),
                ar=ar.reshape(1, S5_COLS), ai=ai.reshape(1, S5_COLS), pr=p_r, pi=p_i,
                d=d.reshape(1, S5_WIDTH))


def _s5_call(proj, sw, w_glu, og, seg, chained, h0=None, prev=None):
    rows = seg * SUBLANES
    const2 = lambda *_: (0, 0)
    const3 = lambda *_: (0, 0, 0)
    if chained:
        n_chunk = SEQ // rows
        grid = (BATCH, n_chunk)
        row_map = lambda b, c: (b * n_chunk + c, 0)
        end_map = lambda b, c: (b, 0, 0)
        n_end = BATCH
        sem = ("parallel", "arbitrary")
    else:
        grid = (DEC_BATCH // SUBLANES,)
        row_map = lambda k: (N_PROMPT // rows + k, 0)
        end_map = lambda k: (k, 0, 0)
        n_end = DEC_BATCH // SUBLANES
        sem = ("parallel",)
    half = S5_COLS // 2
    in_specs = [pl.BlockSpec((rows, S5_WIDTH), row_map),
                pl.BlockSpec((2, MXU_DIM, half), const3),
                pl.BlockSpec((2, MXU_DIM, half), const3),
                pl.BlockSpec((S5_COLS // MXU_DIM, MXU_DIM, MXU_DIM), const3),
                pl.BlockSpec((S5_COLS // MXU_DIM, MXU_DIM, MXU_DIM), const3),
                pl.BlockSpec((1, S5_COLS), const2),
                pl.BlockSpec((1, S5_COLS), const2),
                pl.BlockSpec((seg, S5_COLS), const2),
                pl.BlockSpec((seg, S5_COLS), const2),
                pl.BlockSpec((1, S5_WIDTH), const2),
                pl.BlockSpec((S5_WIDTH, 2 * S5_WIDTH), const2),
                pl.BlockSpec((1, S5_WIDTH), const2)]
    args = [proj, sw['wbr'], sw['wbi'], sw['wcr'], sw['wci'], sw['ar'], sw['ai'],
            sw['pr'][:seg], sw['pi'][:seg], sw['d'], w_glu, og.reshape(1, S5_WIDTH)]
    scratch = [pltpu.VMEM((S5_WIDTH // LANES, rows, LANES), F32),
               pltpu.VMEM((rows, S5_COLS), F32), pltpu.VMEM((rows, S5_COLS), F32),
               pltpu.VMEM((rows, S5_COLS), BF16), pltpu.VMEM((rows, S5_COLS), BF16),
               pltpu.VMEM((SUBLANES, S5_COLS), F32), pltpu.VMEM((SUBLANES, S5_COLS), F32),
               pltpu.VMEM((S5_WIDTH // LANES, rows, LANES), F32)]
    aliases = {}
    if chained:
        scratch += [pltpu.VMEM((1, S5_COLS), F32), pltpu.VMEM((1, S5_COLS), F32)]
    else:
        in_specs += [pl.BlockSpec((None, SUBLANES, S5_COLS), end_map),
                     pl.BlockSpec((None, SUBLANES, S5_COLS), end_map)]
        args += [h0[0], h0[1]]
    if prev is not None:
        in_specs.append(pl.BlockSpec(memory_space=pl.ANY))
        args.append(prev)
        aliases = {len(args) - 1: 0}
    kern = functools.partial(_s5_kernel, seg=seg, chained=chained)
    if prev is not None:
        kern = _drop_last_input(kern, len(args))
    return pl.pallas_call(
        kern,
        grid=grid,
        in_specs=in_specs,
        out_specs=[pl.BlockSpec((rows, S5_WIDTH), row_map),
                   pl.BlockSpec((None, SUBLANES, S5_COLS), end_map),
                   pl.BlockSpec((None, SUBLANES, S5_COLS), end_map)],
        out_shape=[jax.ShapeDtypeStruct((N_TOK, S5_WIDTH), BF16),
                   jax.ShapeDtypeStruct((n_end, SUBLANES, S5_COLS), F32),
                   jax.ShapeDtypeStruct((n_end, SUBLANES, S5_COLS), F32)],
        scratch_shapes=scratch,
        input_output_aliases=aliases,
        compiler_params=_params(*sem),
        name="s5_chained" if chained else "s5_streams",
    )(*args)


def _drop_last_input(kern, n_in):
    def wrapped(*refs):
        return kern(*refs[:n_in - 1], *refs[n_in:])
    return wrapped


def _gmlp_kernel(*refs, chunk, emit_vn):
    if emit_vn:
        u_ref, v_ref, lng_ref, lnb_ref, ws_ref, bst_ref, og_ref, o_ref, vn_ref, acc_ref = refs
    else:
        u_ref, v_ref, lng_ref, lnb_ref, ws_ref, bst_ref, og_ref, o_ref, acc_ref = refs
    rows = u_ref.shape[0]
    u = jax.nn.gelu(u_ref[...])
    v = jax.nn.gelu(v_ref[...])
    tri = (lax.broadcasted_iota(jnp.int32, (chunk, chunk), 0)
           >= lax.broadcasted_iota(jnp.int32, (chunk, chunk), 1))
    for h in range(GM_HEADS):
        cols = slice(h * GM_HEAD_DIM, (h + 1) * GM_HEAD_DIM)
        vh = v[:, cols]
        mu = jnp.mean(vh, axis=-1, keepdims=True)
        cen = vh - mu
        var = jnp.mean(cen * cen, axis=-1, keepdims=True)
        vn = cen * lax.rsqrt(var + EPS) * lng_ref[:, cols] + lnb_ref[:, cols]
        if emit_vn:
            vn_ref[:, cols] = vn
        w = jnp.where(tri, ws_ref[h], 0.0).astype(BF16)
        vb = vn.astype(BF16)
        bias = bst_ref[:, h:h + 1]
        for c in range(rows // chunk):
            rs = slice(c * chunk, (c + 1) * chunk)
            acc_ref[rs, cols] = u[rs, cols] * (_dot(w, vb[rs]) + bias)
    o_ref[...] = _rms(acc_ref[...], og_ref[...]).astype(BF16)


def _gmlp_call(proj, ln_g, ln_b, w_s, b_s, og, rows, chunk, first_block, n_blocks, emit_vn, prev=None):
    const2 = lambda i: (0, 0)
    in_specs = [pl.BlockSpec((rows, GM_WIDTH), lambda i: (first_block + i, 1)),
                pl.BlockSpec((rows, GM_WIDTH), lambda i: (first_block + i, 2)),
                pl.BlockSpec((1, GM_WIDTH), const2),
                pl.BlockSpec((1, GM_WIDTH), const2),
                pl.BlockSpec((GM_HEADS, chunk, chunk), lambda i: (0, 0, 0)),
                pl.BlockSpec((chunk, GM_HEADS), const2),
                pl.BlockSpec((1, GM_WIDTH), const2)]
    args = [proj, proj, ln_g.reshape(1, GM_WIDTH), ln_b.reshape(1, GM_WIDTH),
            w_s[:, :chunk, :chunk], b_s[:, :chunk].T, og.reshape(1, GM_WIDTH)]
    out_specs = [pl.BlockSpec((rows, GM_WIDTH), lambda i: (first_block + i, 0))]
    out_shape = [jax.ShapeDtypeStruct((N_TOK, GM_WIDTH), BF16)]
    if emit_vn:
        out_specs.append(pl.BlockSpec((rows, GM_WIDTH), lambda i: (i, 0)))
        out_shape.append(jax.ShapeDtypeStruct((n_blocks * rows, GM_WIDTH), F32))
    kern = functools.partial(_gmlp_kernel, chunk=chunk, emit_vn=emit_vn)
    aliases = {}
    if prev is not None:
        in_specs.append(pl.BlockSpec(memory_space=pl.ANY))
        args.append(prev)
        aliases = {len(args) - 1: 0}
        kern = _drop_last_input(kern, len(args))
    return pl.pallas_call(
        kern,
        grid=(n_blocks,),
        in_specs=in_specs,
        out_specs=out_specs,
        out_shape=out_shape,
        scratch_shapes=[pltpu.VMEM((rows, GM_WIDTH), F32)],
        input_output_aliases=aliases,
        compiler_params=_params("parallel"),
        name="gmlp_vn" if emit_vn else "gmlp",
    )(*args)


def _xattn_kernel(x_ref, g_ref, wq_ref, k_ref, v_ref, wo_ref, o_ref, att_ref, *, n_batch):
    rows = x_ref.shape[0]
    tb = rows // n_batch
    x = x_ref[...]
    q = _dot(_rms(x, g_ref[...]).astype(BF16), wq_ref[...]).astype(BF16)
    scale = XA_HEAD_DIM ** -0.5
    for b in range(n_batch):
        rs = slice(b * tb, (b + 1) * tb)
        for h in range(XA_HEADS):
            cols = slice(h * XA_HEAD_DIM, (h + 1) * XA_HEAD_DIM)
            s = lax.dot_general(q[rs, cols], k_ref[b, :, cols], (((1,), (1,)), ((), ())),
                                preferred_element_type=F32) * scale
            e = jnp.exp(s - jnp.max(s, axis=-1, keepdims=True))
            p = e / jnp.sum(e, axis=-1, keepdims=True)
            att_ref[rs, cols] = _dot(p.astype(BF16), v_ref[b, :, cols]).astype(BF16)
    o_ref[...] = x + _dot(att_ref[...], wo_ref[...])


def _xattn_call(x, g, wq, k, v, wo, rows, n_batch, first_block, n_blocks, blocks_per_kv, prev=None):
    const2 = lambda i: (0, 0)
    in_specs = [pl.BlockSpec((rows, D_MODEL), lambda i: (first_block + i, 0)),
                pl.BlockSpec((1, D_MODEL), const2),
                pl.BlockSpec((D_MODEL, D_MODEL), const2),
                pl.BlockSpec((n_batch, N_MEM, D_MODEL), lambda i: (i // blocks_per_kv, 0, 0)),
                pl.BlockSpec((n_batch, N_MEM, D_MODEL), lambda i: (i // blocks_per_kv, 0, 0)),
                pl.BlockSpec((D_MODEL, D_MODEL), const2)]
    args = [x, g.reshape(1, D_MODEL), wq, k, v, wo]
    kern = functools.partial(_xattn_kernel, n_batch=n_batch)
    aliases = {}
    if prev is not None:
        in_specs.append(pl.BlockSpec(memory_space=pl.ANY))
        args.append(prev)
        aliases = {len(args) - 1: 0}
        kern = _drop_last_input(kern, len(args))
    return pl.pallas_call(
        kern,
        grid=(n_blocks,),
        in_specs=in_specs,
        out_specs=pl.BlockSpec((rows, D_MODEL), lambda i: (first_block + i, 0)),
        out_shape=jax.ShapeDtypeStruct((N_TOK, D_MODEL), F32),
        scratch_shapes=[pltpu.VMEM((rows, D_MODEL), BF16)],
        input_output_aliases=aliases,
        compiler_params=_params("parallel"),
        name="xattn",
    )(*args)


def _moe_kernel(be_ref, nu_ref, x_ref, wg_ref, bg_ref, wu_ref, bu_ref, wd_ref, bd_ref, o_ref,
                wgb_ref, wub_ref, wdb_ref):
    i = pl.program_id(0)
    fresh = jnp.logical_or(i == 0, be_ref[i] != be_ref[jnp.maximum(i - 1, 0)])

    @pl.when(jnp.logical_and(i < nu_ref[0], fresh))
    def _():
        def cast(r, _):
            rs = pl.ds(pl.multiple_of(r * MOE_CAST_ROWS, MOE_CAST_ROWS), MOE_CAST_ROWS)
            wgb_ref[rs, :] = wg_ref[rs, :].astype(BF16)
            wub_ref[rs, :] = wu_ref[rs, :].astype(BF16)
            wdb_ref[rs, :] = wd_ref[rs, :].astype(BF16)
            return 0
        lax.fori_loop(0, D_MODEL // MOE_CAST_ROWS, cast, 0)

    @pl.when(i < nu_ref[0])
    def _():
        x = x_ref[...].astype(BF16)
        g = jnp.minimum(_dot(x, wgb_ref[...]) + bg_ref[...], SWIGLU_LIMIT)
        u = jnp.clip(_dot(x, wub_ref[...]) + bu_ref[...], -SWIGLU_LIMIT, SWIGLU_LIMIT)
        a = g * jax.nn.sigmoid(SWIGLU_ALPHA * g) * (u + 1.0)
        o_ref[...] = _dot(a.astype(BF16), wdb_ref[...]) + bd_ref[...]


def _moe_call(layer, xs, blk_exp, n_used, w_gate, b_gate, w_up, b_up, w_down, b_down):
    def row_map(i, be, nu):
        return (jnp.minimum(i, nu[0] - 1), 0)

    def w_map(i, be, nu):
        return (layer, be[i], 0, 0)

    w_spec = pl.BlockSpec((None, None, D_MODEL, D_FF), w_map)
    b_spec = pl.BlockSpec((None, None, 1, D_FF), w_map)
    grid_spec = pltpu.PrefetchScalarGridSpec(
        num_scalar_prefetch=2,
        grid=(MOE_BLOCKS,),
        in_specs=[pl.BlockSpec((MOE_ROWS, D_MODEL), row_map),
                  w_spec, b_spec, w_spec, b_spec, w_spec, b_spec],
        out_specs=pl.BlockSpec((MOE_ROWS, D_MODEL), row_map),
        scratch_shapes=[pltpu.VMEM((D_MODEL, D_FF), BF16)] * 3,
    )
    shape4 = (DEPTH, N_EXPERTS, 1, D_FF)
    return pl.pallas_call(
        _moe_kernel,
        grid_spec=grid_spec,
        out_shape=jax.ShapeDtypeStruct((MOE_BLOCKS * MOE_ROWS, D_MODEL), F32),
        compiler_params=pltpu.CompilerParams(dimension_semantics=("arbitrary",),
                                             vmem_limit_bytes=56 * 1024 * 1024),
        name="moe_experts",
    )(blk_exp, n_used, xs, w_gate, b_gate.reshape(shape4), w_up, b_up.reshape(shape4),
      w_down, b_down.reshape(shape4))


def _route(e_t, cnt):
    blk_cnt = cnt[:, :, 0].astype(jnp.int32)
    counts = jnp.sum(blk_cnt, axis=0)
    padded = ((counts + MOE_ROWS - 1) // MOE_ROWS) * MOE_ROWS
    pend = jnp.cumsum(padded)
    pstart = pend - padded
    base = pstart[None, :] + jnp.cumsum(blk_cnt, axis=0) - blk_cnt
    dest_t = _dest(e_t, base.astype(F32)[:, :, None])
    n_used = (pend[-1] // MOE_ROWS).astype(jnp.int32)
    blk = jnp.minimum(jnp.arange(MOE_BLOCKS, dtype=jnp.int32), n_used - 1) * MOE_ROWS
    blk_exp = jnp.minimum(jnp.searchsorted(pend, blk, side='right'), N_EXPERTS - 1).astype(jnp.int32)
    tok = jnp.tile(jnp.arange(N_TOK, dtype=jnp.int32), TOP_K)
    _, tok_sorted = lax.sort((dest_t.reshape(-1), tok), num_keys=1)
    tok_sorted = jnp.concatenate([tok_sorted, jnp.zeros((MOE_ROWS,), jnp.int32)])
    pad_before = pstart - (jnp.cumsum(counts) - counts)
    offs = blk - pad_before[blk_exp]
    row_tok = jax.vmap(lambda o: lax.dynamic_slice(tok_sorted, (o,), (MOE_ROWS,)))(offs)
    return dest_t, row_tok.reshape(-1), blk_exp, n_used.reshape(1)


def kernel(x_prompt, x_sample, mem_prompt, state_s5_re, state_s5_im, cache_mem_k, cache_mem_v, norm_mix_g, w_in, s5_lambda_re, s5_lambda_im, s5_b_re, s5_b_im, s5_c_re, s5_c_im, s5_d, s5_log_step, s5_w_glu, gm_ln_g, gm_ln_b, gm_w_s, gm_b_s, out_norm_g, w_out, xa_norm_g, mem_norm_g, w_xq, w_xk, w_xv, w_xo, ffn_norm_g, router_w, router_b, e_w_gate, e_b_gate, e_w_up, e_b_up, e_w_down, e_b_down, final_norm_g):
    x = jnp.concatenate([x_prompt.reshape(N_PROMPT, D_MODEL), x_sample.reshape(N_SAMPLE, D_MODEL)], axis=0)
    mem = mem_prompt.reshape(BATCH * N_MEM, D_MODEL)
    outs = {k: [] for k in ('s5r_p', 's5i_p', 'mk_p', 'mv_p', 's5r_s', 's5i_s', 'gmv_s')}
    for l in range(DEPTH):
        proj = _norm_matmul(x, norm_mix_g[l], w_in[l].astype(BF16), ROW_BLOCK)
        sw = _s5_weights(s5_lambda_re[l], s5_lambda_im[l], s5_b_re[l], s5_b_im[l], s5_c_re[l], s5_c_im[l],
                         s5_d[l], s5_log_step[l], S5_SEG_PROMPT)
        w_glu = s5_w_glu[l].astype(BF16)
        og = out_norm_g[l]
        ms5, er_p, ei_p = _s5_call(proj, sw, w_glu, og[:S5_WIDTH], S5_SEG_PROMPT, True)
        h0 = (state_s5_re[l].reshape(DEC_BATCH // SUBLANES, SUBLANES, S5_COLS),
              state_s5_im[l].reshape(DEC_BATCH // SUBLANES, SUBLANES, S5_COLS))
        ms5, er_s, ei_s = _s5_call(proj, sw, w_glu, og[:S5_WIDTH], DEC_SEQ, False, h0=h0, prev=ms5)
        outs['s5r_p'].append(er_p[:, SUBLANES - 1].reshape(BATCH, S5_GROUPS, S5_STATE))
        outs['s5i_p'].append(ei_p[:, SUBLANES - 1].reshape(BATCH, S5_GROUPS, S5_STATE))
        outs['s5r_s'].append(er_s.reshape(DEC_BATCH, S5_GROUPS, S5_STATE))
        outs['s5i_s'].append(ei_s.reshape(DEC_BATCH, S5_GROUPS, S5_STATE))

        (mgm,) = _gmlp_call(proj, gm_ln_g[l], gm_ln_b[l], gm_w_s[l], gm_b_s[l], og[S5_WIDTH:],
                            GM_ROWS_PROMPT, GM_CHUNK, 0, N_PROMPT // GM_ROWS_PROMPT, False)
        mgm, vn = _gmlp_call(proj, gm_ln_g[l], gm_ln_b[l], gm_w_s[l], gm_b_s[l], og[S5_WIDTH:],
                             N_SAMPLE, DEC_SEQ, N_PROMPT // N_SAMPLE, 1, True, prev=mgm)
        outs['gmv_s'].append(vn.reshape(DEC_BATCH, DEC_SEQ, GM_HEADS, GM_HEAD_DIM))
        x = _out_proj(x, ms5, mgm, w_out[l].astype(BF16))

        mk = _norm_matmul(mem, mem_norm_g[l], w_xk[l].astype(BF16), ROW_BLOCK)
        mv = _norm_matmul(mem, mem_norm_g[l], w_xv[l].astype(BF16), ROW_BLOCK)
        outs['mk_p'].append(mk.reshape(BATCH, N_MEM, XA_HEADS, XA_HEAD_DIM))
        outs['mv_p'].append(mv.reshape(BATCH, N_MEM, XA_HEADS, XA_HEAD_DIM))
        wq = w_xq[l].astype(BF16)
        wo = w_xo[l].astype(BF16)
        xa = _xattn_call(x, xa_norm_g[l], wq, mk.reshape(BATCH, N_MEM, D_MODEL).astype(BF16),
                         mv.reshape(BATCH, N_MEM, D_MODEL).astype(BF16), wo,
                         XA_ROWS_PROMPT, 1, 0, N_PROMPT // XA_ROWS_PROMPT, SEQ // XA_ROWS_PROMPT)
        rows_s = XA_SAMPLE_BATCHES * DEC_SEQ
        x = _xattn_call(x, xa_norm_g[l], wq,
                        cache_mem_k[l].reshape(DEC_BATCH, N_MEM, D_MODEL).astype(BF16),
                        cache_mem_v[l].reshape(DEC_BATCH, N_MEM, D_MODEL).astype(BF16), wo,
                        rows_s, XA_SAMPLE_BATCHES, N_PROMPT // rows_s, N_SAMPLE // rows_s, 1, prev=xa)

        h, e_t, gates_t, cnt = _router(x, ffn_norm_g[l], router_w[l], router_b[l])
        dest_t, row_tok, blk_exp, n_used = _route(e_t, cnt)
        ys = _moe_call(l, h[row_tok], blk_exp, n_used, e_w_gate, e_b_gate, e_w_up, e_b_up, e_w_down, e_b_down)
        x = _combine(x, ys[dest_t.reshape(-1)], gates_t.T, final_norm_g if l == DEPTH - 1 else None)

    y = x
    y_prompt = y[:N_PROMPT].reshape(BATCH, SEQ, D_MODEL)
    y_sample = y[N_PROMPT:].reshape(DEC_BATCH, DEC_SEQ, D_MODEL)
    st = jnp.stack
    return (y_prompt, y_sample, st(outs['s5r_p']), st(outs['s5i_p']), st(outs['mk_p']), st(outs['mv_p']),
            st(outs['s5r_s']), st(outs['s5i_s']), st(outs['gmv_s']))
```

```python
import functools

import jax
import jax.numpy as jnp
from jax import lax
from jax.experimental import pallas as pl
from jax.experimental.pallas import tpu as pltpu

F32 = jnp.float32
BF16 = jnp.bfloat16

D_MODEL = 1024
BATCH = 4
SEQ = 4096
DEPTH = 2
DEC_BATCH = 16
DEC_SEQ = 32
S5_GROUPS = 32
S5_GROUP_CH = 16
S5_STATE = 64
S5_WIDTH = S5_GROUPS * S5_GROUP_CH
S5_COLS = S5_GROUPS * S5_STATE
GM_HEADS = 4
GM_HEAD_DIM = 128
GM_WIDTH = GM_HEADS * GM_HEAD_DIM
GM_CHUNK = 128
IN_PROJ = S5_WIDTH + 2 * GM_WIDTH
N_MEM = 256
XA_HEADS = 4
XA_HEAD_DIM = D_MODEL // XA_HEADS
N_EXPERTS = 32
TOP_K = 4
D_FF = D_MODEL
SWIGLU_LIMIT = 7.0
SWIGLU_ALPHA = 1.702
EPS = 1e-6
EIG_CLIP = -1e-4

N_PROMPT = BATCH * SEQ
N_SAMPLE = DEC_BATCH * DEC_SEQ
N_TOK = N_PROMPT + N_SAMPLE

SUBLANES = 8
LANES = 128
MXU_DIM = 256
VMEM_LIMIT = 48 * 1024 * 1024

ROW_BLOCK = 512
S5_SEG_PROMPT = 64
S5_COL_BLOCK = 512
GM_ROWS_PROMPT = 256
XA_ROWS_PROMPT = 512
XA_SAMPLE_BATCHES = 4
MOE_ROWS = 256
MOE_BLOCKS = -(-N_TOK * TOP_K // MOE_ROWS) + N_EXPERTS
MOE_CAST_ROWS = 128


def _params(*sem):
    return pltpu.CompilerParams(dimension_semantics=sem, vmem_limit_bytes=VMEM_LIMIT)


def _rms(x, g):
    return x * lax.rsqrt(jnp.mean(x * x, axis=-1, keepdims=True) + EPS) * g


def _dot(a, b):
    return jnp.dot(a, b, preferred_element_type=F32)


def _norm_matmul_kernel(x_ref, g_ref, w_ref, o_ref):
    o_ref[...] = _dot(_rms(x_ref[...], g_ref[...]).astype(BF16), w_ref[...])


def _norm_matmul(x, g, w, rows):
    n, d = x.shape
    dout = w.shape[1]
    return pl.pallas_call(
        _norm_matmul_kernel,
        grid=(n // rows,),
        in_specs=[pl.BlockSpec((rows, d), lambda i: (i, 0)),
                  pl.BlockSpec((1, d), lambda i: (0, 0)),
                  pl.BlockSpec((d, dout), lambda i: (0, 0))],
        out_specs=pl.BlockSpec((rows, dout), lambda i: (i, 0)),
        out_shape=jax.ShapeDtypeStruct((n, dout), F32),
        compiler_params=_params("parallel"),
        name="norm_matmul",
    )(x, g.reshape(1, d), w)


def _out_proj_kernel(x_ref, a_ref, b_ref, wa_ref, wb_ref, o_ref):
    o_ref[...] = x_ref[...] + _dot(a_ref[...], wa_ref[...]) + _dot(b_ref[...], wb_ref[...])


def _out_proj(x, a, b, w):
    n, d = x.shape
    wa, wb = w[:S5_WIDTH], w[S5_WIDTH:]
    rows = ROW_BLOCK
    return pl.pallas_call(
        _out_proj_kernel,
        grid=(n // rows,),
        in_specs=[pl.BlockSpec((rows, d), lambda i: (i, 0)),
                  pl.BlockSpec((rows, S5_WIDTH), lambda i: (i, 0)),
                  pl.BlockSpec((rows, GM_WIDTH), lambda i: (i, 0)),
                  pl.BlockSpec((S5_WIDTH, d), lambda i: (0, 0)),
                  pl.BlockSpec((GM_WIDTH, d), lambda i: (0, 0))],
        out_specs=pl.BlockSpec((rows, d), lambda i: (i, 0)),
        out_shape=jax.ShapeDtypeStruct((n, d), F32),
        compiler_params=_params("parallel"),
        name="out_proj",
    )(x, a, b, wa, wb)


def _router_kernel(x_ref, g_ref, wt_ref, b_ref, h_ref, e_ref, gate_ref, cnt_ref):
    h = _rms(x_ref[...], g_ref[...])
    h_ref[...] = h
    logits = lax.dot_general(wt_ref[...], h.astype(BF16), (((1,), (1,)), ((), ())),
                             preferred_element_type=F32) + b_ref[...]
    sub = lax.broadcasted_iota(jnp.int32, logits.shape, 0)
    work = logits
    chosen = jnp.zeros(logits.shape, F32)
    vals, idxs = [], []
    for _ in range(TOP_K):
        m = jnp.max(work, axis=0, keepdims=True)
        idx = jnp.min(jnp.where(work == m, sub, N_EXPERTS), axis=0, keepdims=True)
        sel = sub == idx
        vals.append(m)
        idxs.append(idx)
        work = jnp.where(sel, -jnp.inf, work)
        chosen = jnp.where(sel, 1.0, chosen)
    ex = [jnp.exp(v - vals[0]) for v in vals]
    den = ex[0] + ex[1] + ex[2] + ex[3]
    gate_ref[...] = jnp.concatenate([e / den for e in ex], axis=0)
    e_ref[...] = jnp.concatenate(idxs, axis=0)
    cnt_ref[...] = jnp.broadcast_to(jnp.sum(chosen, axis=1, keepdims=True), (N_EXPERTS, LANES))


def _router(x, g, w, b):
    n, d = x.shape
    rows = ROW_BLOCK
    return pl.pallas_call(
        _router_kernel,
        grid=(n // rows,),
        in_specs=[pl.BlockSpec((rows, d), lambda i: (i, 0)),
                  pl.BlockSpec((1, d), lambda i: (0, 0)),
                  pl.BlockSpec((N_EXPERTS, d), lambda i: (0, 0)),
                  pl.BlockSpec((N_EXPERTS, 1), lambda i: (0, 0))],
        out_specs=[pl.BlockSpec((rows, d), lambda i: (i, 0)),
                   pl.BlockSpec((TOP_K, rows), lambda i: (0, i)),
                   pl.BlockSpec((TOP_K, rows), lambda i: (0, i)),
                   pl.BlockSpec((None, N_EXPERTS, LANES), lambda i: (i, 0, 0))],
        out_shape=[jax.ShapeDtypeStruct((n, d), F32),
                   jax.ShapeDtypeStruct((TOP_K, n), jnp.int32),
                   jax.ShapeDtypeStruct((TOP_K, n), F32),
                   jax.ShapeDtypeStruct((n // rows, N_EXPERTS, LANES), F32)],
        compiler_params=_params("parallel"),
        name="router",
    )(x, g.reshape(1, d), w.T.astype(BF16), b.reshape(N_EXPERTS, 1))


def _dest_kernel(e_ref, base_ref, tri_ref, d_ref):
    e = e_ref[...]
    sub = lax.broadcasted_iota(jnp.int32, (N_EXPERTS, e.shape[1]), 0)
    hot = [sub == e[k:k + 1, :] for k in range(TOP_K)]
    chosen = sum(jnp.where(hk, 1.0, 0.0) for hk in hot)
    pos = base_ref[...] + _dot(chosen.astype(BF16), tri_ref[...])
    d_ref[...] = jnp.concatenate(
        [jnp.sum(jnp.where(hk, pos, 0.0), axis=0, keepdims=True) for hk in hot], axis=0).astype(jnp.int32)


def _dest(e_t, base):
    n = e_t.shape[1]
    rows = ROW_BLOCK
    tri = jnp.triu(jnp.ones((rows, rows), F32), 1).astype(BF16)
    return pl.pallas_call(
        _dest_kernel,
        grid=(n // rows,),
        in_specs=[pl.BlockSpec((TOP_K, rows), lambda i: (0, i)),
                  pl.BlockSpec((None, N_EXPERTS, 1), lambda i: (i, 0, 0)),
                  pl.BlockSpec((rows, rows), lambda i: (0, 0))],
        out_specs=pl.BlockSpec((TOP_K, rows), lambda i: (0, i)),
        out_shape=jax.ShapeDtypeStruct((TOP_K, n), jnp.int32),
        compiler_params=_params("parallel"),
        name="moe_dest",
    )(e_t, base, tri)


def _combine_kernel(*refs, final):
    if final:
        x_ref, g0, g1, g2, g3, gate_ref, fg_ref, o_ref = refs
    else:
        x_ref, g0, g1, g2, g3, gate_ref, o_ref = refs
    acc = x_ref[...]
    for k, g in enumerate((g0, g1, g2, g3)):
        acc = acc + gate_ref[:, k:k + 1] * g[...]
    o_ref[...] = _rms(acc, fg_ref[...]) if final else acc


def _combine(x, picked, gates, final_g=None):
    n, d = x.shape
    rows = ROW_BLOCK
    picked = picked.reshape(TOP_K, n, d)
    final = final_g is not None
    in_specs = [pl.BlockSpec((rows, d), lambda i: (i, 0))]
    in_specs += [pl.BlockSpec((None, rows, d), functools.partial(lambda i, k: (k, i, 0), k=k))
                 for k in range(TOP_K)]
    in_specs.append(pl.BlockSpec((rows, TOP_K), lambda i: (i, 0)))
    args = [x, picked, picked, picked, picked, gates]
    if final:
        in_specs.append(pl.BlockSpec((1, d), lambda i: (0, 0)))
        args.append(final_g.reshape(1, d))
    return pl.pallas_call(
        functools.partial(_combine_kernel, final=final),
        grid=(n // rows,),
        in_specs=in_specs,
        out_specs=pl.BlockSpec((rows, d), lambda i: (i, 0)),
        out_shape=jax.ShapeDtypeStruct((n, d), F32),
        compiler_params=_params("parallel"),
        name="moe_combine",
    )(*args)


def _cmul_add(ar, ai, hr, hi, br, bi):
    return ar * hr - ai * hi + br, ar * hi + ai * hr + bi


def _s5_kernel(*refs, seg, chained):
    if chained:
        (u_ref, wbr_ref, wbi_ref, wcr_ref, wci_ref, ar_ref, ai_ref, pr_ref, pi_ref, d_ref,
         wglu_ref, og_ref, o_ref, er_ref, ei_ref,
         up_ref, hr_ref, hi_ref, hbr_ref, hbi_ref, sr_ref, si_ref, op_ref, cr_ref, ci_ref) = refs
    else:
        (u_ref, wbr_ref, wbi_ref, wcr_ref, wci_ref, ar_ref, ai_ref, pr_ref, pi_ref, d_ref,
         wglu_ref, og_ref, h0r_ref, h0i_ref, o_ref, er_ref, ei_ref,
         up_ref, hr_ref, hi_ref, hbr_ref, hbi_ref, sr_ref, si_ref, op_ref) = refs
    rows = seg * SUBLANES
    half = S5_COLS // 2

    n_lane_blk = S5_WIDTH // LANES
    for j in range(SUBLANES):
        for i0 in range(0, seg, SUBLANES):
            for c in range(n_lane_blk):
                up_ref[c, pl.ds(i0 * SUBLANES + j, SUBLANES, stride=SUBLANES), :] = (
                    u_ref[j * seg + i0:j * seg + i0 + SUBLANES, c * LANES:(c + 1) * LANES])

    up = jnp.concatenate([up_ref[c] for c in range(n_lane_blk)], axis=1)
    ub = up.astype(BF16)
    for k in range(2):
        uk = ub[:, k * MXU_DIM:(k + 1) * MXU_DIM]
        hr_ref[:, k * half:(k + 1) * half] = _dot(uk, wbr_ref[k])
        hi_ref[:, k * half:(k + 1) * half] = _dot(uk, wbi_ref[k])

    for cb in range(S5_COLS // S5_COL_BLOCK):
        cols = slice(cb * S5_COL_BLOCK, (cb + 1) * S5_COL_BLOCK)
        ar = jnp.broadcast_to(ar_ref[:, cols], (SUBLANES, S5_COL_BLOCK))
        ai = jnp.broadcast_to(ai_ref[:, cols], (SUBLANES, S5_COL_BLOCK))

        def step(i, carry, cols=cols, ar=ar, ai=ai):
            r = pl.multiple_of(i * SUBLANES, SUBLANES)
            h_r, h_i = _cmul_add(ar, ai, carry[0], carry[1],
                                 hr_ref[pl.ds(r, SUBLANES), cols], hi_ref[pl.ds(r, SUBLANES), cols])
            hr_ref[pl.ds(r, SUBLANES), cols] = h_r
            hi_ref[pl.ds(r, SUBLANES), cols] = h_i
            return h_r, h_i

        zero = jnp.zeros((SUBLANES, S5_COL_BLOCK), F32)
        lax.fori_loop(0, seg, step, (zero, zero), unroll=2)

    end_r = hr_ref[rows - SUBLANES:rows, :]
    end_i = hi_ref[rows - SUBLANES:rows, :]
    as_r = pr_ref[seg - 1:seg, :]
    as_i = pi_ref[seg - 1:seg, :]
    if chained:
        @pl.when(pl.program_id(1) == 0)
        def _():
            cr_ref[...] = jnp.zeros_like(cr_ref)
            ci_ref[...] = jnp.zeros_like(ci_ref)

        cur_r, cur_i = cr_ref[...], ci_ref[...]
        for j in range(SUBLANES):
            sr_ref[j:j + 1, :] = cur_r
            si_ref[j:j + 1, :] = cur_i
            cur_r, cur_i = _cmul_add(as_r, as_i, cur_r, cur_i, end_r[j:j + 1], end_i[j:j + 1])
        cr_ref[...] = cur_r
        ci_ref[...] = cur_i
    else:
        sr_ref[...] = h0r_ref[...]
        si_ref[...] = h0i_ref[...]
    t_r, t_i = _cmul_add(as_r, as_i, sr_ref[...], si_ref[...], end_r, end_i)
    er_ref[...] = t_r
    ei_ref[...] = t_i

    for cb in range(S5_COLS // S5_COL_BLOCK):
        cols = slice(cb * S5_COL_BLOCK, (cb + 1) * S5_COL_BLOCK)
        s_r = jnp.concatenate([sr_ref[:, cols]] * 2, axis=0)
        s_i = jnp.concatenate([si_ref[:, cols]] * 2, axis=0)

        def fix(i2, _, cols=cols, s_r=s_r, s_i=s_i):
            r = pl.multiple_of(i2 * 2 * SUBLANES, 2 * SUBLANES)
            shape = (SUBLANES, S5_COL_BLOCK)
            p_r = jnp.concatenate(
                [jnp.broadcast_to(pr_ref[pl.ds(2 * i2, 1), cols], shape),
                 jnp.broadcast_to(pr_ref[pl.ds(2 * i2 + 1, 1), cols], shape)], axis=0)
            p_i = jnp.concatenate(
                [jnp.broadcast_to(pi_ref[pl.ds(2 * i2, 1), cols], shape),
                 jnp.broadcast_to(pi_ref[pl.ds(2 * i2 + 1, 1), cols], shape)], axis=0)
            h_r, h_i = _cmul_add(p_r, p_i, s_r, s_i,
                                 hr_ref[pl.ds(r, 2 * SUBLANES), cols],
                                 hi_ref[pl.ds(r, 2 * SUBLANES), cols])
            hbr_ref[pl.ds(r, 2 * SUBLANES), cols] = h_r.astype(BF16)
            hbi_ref[pl.ds(r, 2 * SUBLANES), cols] = h_i.astype(BF16)
            return 0

        lax.fori_loop(0, seg // 2, fix, 0)

    slabs_per_window = MXU_DIM // (MXU_DIM // S5_STATE * S5_GROUP_CH)
    ys = []
    for w in range(S5_WIDTH // MXU_DIM):
        acc = d_ref[:, w * MXU_DIM:(w + 1) * MXU_DIM] * jnp.concatenate(
            [up_ref[c] for c in range(w * MXU_DIM // LANES, (w + 1) * MXU_DIM // LANES)], axis=1)
        for s in range(slabs_per_window):
            j = w * slabs_per_window + s
            acc += _dot(hbr_ref[:, j * MXU_DIM:(j + 1) * MXU_DIM], wcr_ref[j])
            acc += _dot(hbi_ref[:, j * MXU_DIM:(j + 1) * MXU_DIM], wci_ref[j])
        ys.append(jax.nn.gelu(acc).astype(BF16))

    z = _dot(jnp.concatenate(ys, axis=1), wglu_ref[...])
    o = _rms(z[:, :S5_WIDTH] * jax.nn.sigmoid(z[:, S5_WIDTH:]), og_ref[...])
    for c in range(n_lane_blk):
        op_ref[c] = o[:, c * LANES:(c + 1) * LANES]

    for j in range(SUBLANES):
        for i0 in range(0, seg, 2 * SUBLANES):
            for c in range(n_lane_blk):
                o_ref[j * seg + i0:j * seg + i0 + 2 * SUBLANES, c * LANES:(c + 1) * LANES] = (
                    op_ref[c, pl.ds(i0 * SUBLANES + j, 2 * SUBLANES, stride=SUBLANES), :].astype(BF16))


def _s5_weights(lam_re, lam_im, b_re, b_im, c_re, c_im, d, log_step, seg):
    step = jnp.exp(log_step)[:, None]
    lr = jnp.minimum(lam_re, EIG_CLIP)
    li = lam_im
    mag = jnp.exp(lr * step)
    ar = mag * jnp.cos(li * step)
    ai = mag * jnp.sin(li * step)
    den = lr * lr + li * li
    fr = ((ar - 1.0) * lr + ai * li) / den
    fi = (ai * lr - (ar - 1.0) * li) / den
    bbar_r = fr[..., None] * b_re - fi[..., None] * b_im
    bbar_i = fr[..., None] * b_im + fi[..., None] * b_re
    t = jnp.arange(1, seg + 1, dtype=F32)[:, None, None]
    pm = jnp.exp(lr * step * t)
    p_r = (pm * jnp.cos(li * step * t)).reshape(seg, S5_COLS)
    p_i = (pm * jnp.sin(li * step * t)).reshape(seg, S5_COLS)

    gk = MXU_DIM // S5_GROUP_CH

    def in_map(bbar):
        bb = bbar.reshape(S5_GROUPS // gk, gk, S5_STATE, S5_GROUP_CH)
        w = jnp.einsum('kgnh,gG->kghGn', bb, jnp.eye(gk, dtype=F32))
        return w.reshape(S5_GROUPS // gk, MXU_DIM, gk * S5_STATE).astype(BF16)

    gs = MXU_DIM // S5_STATE
    n_slab = S5_COLS // MXU_DIM
    per_win = MXU_DIM // (gs * S5_GROUP_CH)

    def out_map(c):
        cc = c.reshape(n_slab // per_win, per_win, gs, S5_GROUP_CH, S5_STATE)
        w = jnp.einsum('wsghn,sS,gG->wsgnSGh', cc, jnp.eye(per_win, dtype=F32), jnp.eye(gs, dtype=F32))
        return w.reshape(n_slab, MXU_DIM, MXU_DIM).astype(BF16)

    return dict(wbr=in_map(bbar_r), wbi=in_map(bbar_i), wcr=out_map(c_re), wci=out_map(-c_im),
                ar=ar.reshape(1, S5_COLS), ai=ai.reshape(1, S5_COLS), pr=p_r, pi=p_i,
                d=d.reshape(1, S5_WIDTH))


def _s5_call(proj, sw, w_glu, og, seg, chained, h0=None, prev=None):
    rows = seg * SUBLANES
    const2 = lambda *_: (0, 0)
    const3 = lambda *_: (0, 0, 0)
    if chained:
        n_chunk = SEQ // rows
        grid = (BATCH, n_chunk)
        row_map = lambda b, c: (b * n_chunk + c, 0)
        end_map = lambda b, c: (b, 0, 0)
        n_end = BATCH
        sem = ("parallel", "arbitrary")
    else:
        grid = (DEC_BATCH // SUBLANES,)
        row_map = lambda k: (N_PROMPT // rows + k, 0)
        end_map = lambda k: (k, 0, 0)
        n_end = DEC_BATCH // SUBLANES
        sem = ("parallel",)
    half = S5_COLS // 2
    in_specs = [pl.BlockSpec((rows, S5_WIDTH), row_map),
                pl.BlockSpec((2, MXU_DIM, half), const3),
                pl.BlockSpec((2, MXU_DIM, half), const3),
                pl.BlockSpec((S5_COLS // MXU_DIM, MXU_DIM, MXU_DIM), const3),
                pl.BlockSpec((S5_COLS // MXU_DIM, MXU_DIM, MXU_DIM), const3),
                pl.BlockSpec((1, S5_COLS), const2),
                pl.BlockSpec((1, S5_COLS), const2),
                pl.BlockSpec((seg, S5_COLS), const2),
                pl.BlockSpec((seg, S5_COLS), const2),
                pl.BlockSpec((1, S5_WIDTH), const2),
                pl.BlockSpec((S5_WIDTH, 2 * S5_WIDTH), const2),
                pl.BlockSpec((1, S5_WIDTH), const2)]
    args = [proj, sw['wbr'], sw['wbi'], sw['wcr'], sw['wci'], sw['ar'], sw['ai'],
            sw['pr'][:seg], sw['pi'][:seg], sw['d'], w_glu, og.reshape(1, S5_WIDTH)]
    scratch = [pltpu.VMEM((S5_WIDTH // LANES, rows, LANES), F32),
               pltpu.VMEM((rows, S5_COLS), F32), pltpu.VMEM((rows, S5_COLS), F32),
               pltpu.VMEM((rows, S5_COLS), BF16), pltpu.VMEM((rows, S5_COLS), BF16),
               pltpu.VMEM((SUBLANES, S5_COLS), F32), pltpu.VMEM((SUBLANES, S5_COLS), F32),
               pltpu.VMEM((S5_WIDTH // LANES, rows, LANES), F32)]
    aliases = {}
    if chained:
        scratch += [pltpu.VMEM((1, S5_COLS), F32), pltpu.VMEM((1, S5_COLS), F32)]
    else:
        in_specs += [pl.BlockSpec((None, SUBLANES, S5_COLS), end_map),
                     pl.BlockSpec((None, SUBLANES, S5_COLS), end_map)]
        args += [h0[0], h0[1]]
    if prev is not None:
        in_specs.append(pl.BlockSpec(memory_space=pl.ANY))
        args.append(prev)
        aliases = {len(args) - 1: 0}
    kern = functools.partial(_s5_kernel, seg=seg, chained=chained)
    if prev is not None:
        kern = _drop_last_input(kern, len(args))
    return pl.pallas_call(
        kern,
        grid=grid,
        in_specs=in_specs,
        out_specs=[pl.BlockSpec((rows, S5_WIDTH), row_map),
                   pl.BlockSpec((None, SUBLANES, S5_COLS), end_map),
                   pl.BlockSpec((None, SUBLANES, S5_COLS), end_map)],
        out_shape=[jax.ShapeDtypeStruct((N_TOK, S5_WIDTH), BF16),
                   jax.ShapeDtypeStruct((n_end, SUBLANES, S5_COLS), F32),
                   jax.ShapeDtypeStruct((n_end, SUBLANES, S5_COLS), F32)],
        scratch_shapes=scratch,
        input_output_aliases=aliases,
        compiler_params=_params(*sem),
        name="s5_chained" if chained else "s5_streams",
    )(*args)


def _drop_last_input(kern, n_in):
    def wrapped(*refs):
        return kern(*refs[:n_in - 1], *refs[n_in:])
    return wrapped


def _gmlp_kernel(*refs, chunk, emit_vn):
    if emit_vn:
        u_ref, v_ref, lng_ref, lnb_ref, ws_ref, bst_ref, og_ref, o_ref, vn_ref, acc_ref = refs
    else:
        u_ref, v_ref, lng_ref, lnb_ref, ws_ref, bst_ref, og_ref, o_ref, acc_ref = refs
    rows = u_ref.shape[0]
    u = jax.nn.gelu(u_ref[...])
    v = jax.nn.gelu(v_ref[...])
    tri = (lax.broadcasted_iota(jnp.int32, (chunk, chunk), 0)
           >= lax.broadcasted_iota(jnp.int32, (chunk, chunk), 1))
    for h in range(GM_HEADS):
        cols = slice(h * GM_HEAD_DIM, (h + 1) * GM_HEAD_DIM)
        vh = v[:, cols]
        mu = jnp.mean(vh, axis=-1, keepdims=True)
        cen = vh - mu
        var = jnp.mean(cen * cen, axis=-1, keepdims=True)
        vn = cen * lax.rsqrt(var + EPS) * lng_ref[:, cols] + lnb_ref[:, cols]
        if emit_vn:
            vn_ref[:, cols] = vn
        w = jnp.where(tri, ws_ref[h], 0.0).astype(BF16)
        vb = vn.astype(BF16)
        bias = bst_ref[:, h:h + 1]
        for c in range(rows // chunk):
            rs = slice(c * chunk, (c + 1) * chunk)
            acc_ref[rs, cols] = u[rs, cols] * (_dot(w, vb[rs]) + bias)
    o_ref[...] = _rms(acc_ref[...], og_ref[...]).astype(BF16)


def _gmlp_call(proj, ln_g, ln_b, w_s, b_s, og, rows, chunk, first_block, n_blocks, emit_vn, prev=None):
    const2 = lambda i: (0, 0)
    in_specs = [pl.BlockSpec((rows, GM_WIDTH), lambda i: (first_block + i, 1)),
                pl.BlockSpec((rows, GM_WIDTH), lambda i: (first_block + i, 2)),
                pl.BlockSpec((1, GM_WIDTH), const2),
                pl.BlockSpec((1, GM_WIDTH), const2),
                pl.BlockSpec((GM_HEADS, chunk, chunk), lambda i: (0, 0, 0)),
                pl.BlockSpec((chunk, GM_HEADS), const2),
                pl.BlockSpec((1, GM_WIDTH), const2)]
    args = [proj, proj, ln_g.reshape(1, GM_WIDTH), ln_b.reshape(1, GM_WIDTH),
            w_s[:, :chunk, :chunk], b_s[:, :chunk].T, og.reshape(1, GM_WIDTH)]
    out_specs = [pl.BlockSpec((rows, GM_WIDTH), lambda i: (first_block + i, 0))]
    out_shape = [jax.ShapeDtypeStruct((N_TOK, GM_WIDTH), BF16)]
    if emit_vn:
        out_specs.append(pl.BlockSpec((rows, GM_WIDTH), lambda i: (i, 0)))
        out_shape.append(jax.ShapeDtypeStruct((n_blocks * rows, GM_WIDTH), F32))
    kern = functools.partial(_gmlp_kernel, chunk=chunk, emit_vn=emit_vn)
    aliases = {}
    if prev is not None:
        in_specs.append(pl.BlockSpec(memory_space=pl.ANY))
        args.append(prev)
        aliases = {len(args) - 1: 0}
        kern = _drop_last_input(kern, len(args))
    return pl.pallas_call(
        kern,
        grid=(n_blocks,),
        in_specs=in_specs,
        out_specs=out_specs,
        out_shape=out_shape,
        scratch_shapes=[pltpu.VMEM((rows, GM_WIDTH), F32)],
        input_output_aliases=aliases,
        compiler_params=_params("parallel"),
        name="gmlp_vn" if emit_vn else "gmlp",
    )(*args)


def _xattn_kernel(x_ref, g_ref, wq_ref, k_ref, v_ref, wo_ref, o_ref, att_ref, *, n_batch):
    rows = x_ref.shape[0]
    tb = rows // n_batch
    x = x_ref[...]
    q = _dot(_rms(x, g_ref[...]).astype(BF16), wq_ref[...]).astype(BF16)
    scale = XA_HEAD_DIM ** -0.5
    for b in range(n_batch):
        rs = slice(b * tb, (b + 1) * tb)
        for h in range(XA_HEADS):
            cols = slice(h * XA_HEAD_DIM, (h + 1) * XA_HEAD_DIM)
            s = lax.dot_general(q[rs, cols], k_ref[b, :, cols], (((1,), (1,)), ((), ())),
                                preferred_element_type=F32) * scale
            e = jnp.exp(s - jnp.max(s, axis=-1, keepdims=True))
            p = e / jnp.sum(e, axis=-1, keepdims=True)
            att_ref[rs, cols] = _dot(p.astype(BF16), v_ref[b, :, cols]).astype(BF16)
    o_ref[...] = x + _dot(att_ref[...], wo_ref[...])


def _xattn_call(x, g, wq, k, v, wo, rows, n_batch, first_block, n_blocks, blocks_per_kv, prev=None):
    const2 = lambda i: (0, 0)
    in_specs = [pl.BlockSpec((rows, D_MODEL), lambda i: (first_block + i, 0)),
                pl.BlockSpec((1, D_MODEL), const2),
                pl.BlockSpec((D_MODEL, D_MODEL), const2),
                pl.BlockSpec((n_batch, N_MEM, D_MODEL), lambda i: (i // blocks_per_kv, 0, 0)),
                pl.BlockSpec((n_batch, N_MEM, D_MODEL), lambda i: (i // blocks_per_kv, 0, 0)),
                pl.BlockSpec((D_MODEL, D_MODEL), const2)]
    args = [x, g.reshape(1, D_MODEL), wq, k, v, wo]
    kern = functools.partial(_xattn_kernel, n_batch=n_batch)
    aliases = {}
    if prev is not None:
        in_specs.append(pl.BlockSpec(memory_space=pl.ANY))
        args.append(prev)
        aliases = {len(args) - 1: 0}
        kern = _drop_last_input(kern, len(args))
    return pl.pallas_call(
        kern,
        grid=(n_blocks,),
        in_specs=in_specs,
        out_specs=pl.BlockSpec((rows, D_MODEL), lambda i: (first_block + i, 0)),
        out_shape=jax.ShapeDtypeStruct((N_TOK, D_MODEL), F32),
        scratch_shapes=[pltpu.VMEM((rows, D_MODEL), BF16)],
        input_output_aliases=aliases,
        compiler_params=_params("parallel"),
        name="xattn",
    )(*args)


def _moe_kernel(be_ref, nu_ref, x_ref, wg_ref, bg_ref, wu_ref, bu_ref, wd_ref, bd_ref, o_ref,
                wgb_ref, wub_ref, wdb_ref):
    i = pl.program_id(0)
    fresh = jnp.logical_or(i == 0, be_ref[i] != be_ref[jnp.maximum(i - 1, 0)])

    @pl.when(jnp.logical_and(i < nu_ref[0], fresh))
    def _():
        def cast(r, _):
            rs = pl.ds(pl.multiple_of(r * MOE_CAST_ROWS, MOE_CAST_ROWS), MOE_CAST_ROWS)
            wgb_ref[rs, :] = wg_ref[rs, :].astype(BF16)
            wub_ref[rs, :] = wu_ref[rs, :].astype(BF16)
            wdb_ref[rs, :] = wd_ref[rs, :].astype(BF16)
            return 0
        lax.fori_loop(0, D_MODEL // MOE_CAST_ROWS, cast, 0)

    @pl.when(i < nu_ref[0])
    def _():
        x = x_ref[...].astype(BF16)
        g = jnp.minimum(_dot(x, wgb_ref[...]) + bg_ref[...], SWIGLU_LIMIT)
        u = jnp.clip(_dot(x, wub_ref[...]) + bu_ref[...], -SWIGLU_LIMIT, SWIGLU_LIMIT)
        a = g * jax.nn.sigmoid(SWIGLU_ALPHA * g) * (u + 1.0)
        o_ref[...] = _dot(a.astype(BF16), wdb_ref[...]) + bd_ref[...]


def _moe_call(layer, xs, blk_exp, n_used, w_gate, b_gate, w_up, b_up, w_down, b_down):
    def row_map(i, be, nu):
        return (jnp.minimum(i, nu[0] - 1), 0)

    def w_map(i, be, nu):
        return (layer, be[i], 0, 0)

    w_spec = pl.BlockSpec((None, None, D_MODEL, D_FF), w_map)
    b_spec = pl.BlockSpec((None, None, 1, D_FF), w_map)
    grid_spec = pltpu.PrefetchScalarGridSpec(
        num_scalar_prefetch=2,
        grid=(MOE_BLOCKS,),
        in_specs=[pl.BlockSpec((MOE_ROWS, D_MODEL), row_map),
                  w_spec, b_spec, w_spec, b_spec, w_spec, b_spec],
        out_specs=pl.BlockSpec((MOE_ROWS, D_MODEL), row_map),
        scratch_shapes=[pltpu.VMEM((D_MODEL, D_FF), BF16)] * 3,
    )
    shape4 = (DEPTH, N_EXPERTS, 1, D_FF)
    return pl.pallas_call(
        _moe_kernel,
        grid_spec=grid_spec,
        out_shape=jax.ShapeDtypeStruct((MOE_BLOCKS * MOE_ROWS, D_MODEL), F32),
        compiler_params=pltpu.CompilerParams(dimension_semantics=("arbitrary",),
                                             vmem_limit_bytes=56 * 1024 * 1024),
        name="moe_experts",
    )(blk_exp, n_used, xs, w_gate, b_gate.reshape(shape4), w_up, b_up.reshape(shape4),
      w_down, b_down.reshape(shape4))


def _route(e_t, cnt):
    blk_cnt = cnt[:, :, 0].astype(jnp.int32)
    counts = jnp.sum(blk_cnt, axis=0)
    padded = ((counts + MOE_ROWS - 1) // MOE_ROWS) * MOE_ROWS
    pend = jnp.cumsum(padded)
    pstart = pend - padded
    base = pstart[None, :] + jnp.cumsum(blk_cnt, axis=0) - blk_cnt
    dest_t = _dest(e_t, base.astype(F32)[:, :, None])
    n_used = (pend[-1] // MOE_ROWS).astype(jnp.int32)
    blk = jnp.minimum(jnp.arange(MOE_BLOCKS, dtype=jnp.int32), n_used - 1) * MOE_ROWS
    blk_exp = jnp.minimum(jnp.sum((pend[None, :] <= blk[:, None]).astype(jnp.int32), axis=1), N_EXPERTS - 1)
    tok = jnp.tile(jnp.arange(N_TOK, dtype=jnp.int32), TOP_K)
    row_tok = jnp.zeros((MOE_BLOCKS * MOE_ROWS,), jnp.int32).at[dest_t.reshape(-1)].set(
        tok, unique_indices=True)
    return dest_t, row_tok, blk_exp, n_used.reshape(1)


def kernel(x_prompt, x_sample, mem_prompt, state_s5_re, state_s5_im, cache_mem_k, cache_mem_v, norm_mix_g, w_in, s5_lambda_re, s5_lambda_im, s5_b_re, s5_b_im, s5_c_re, s5_c_im, s5_d, s5_log_step, s5_w_glu, gm_ln_g, gm_ln_b, gm_w_s, gm_b_s, out_norm_g, w_out, xa_norm_g, mem_norm_g, w_xq, w_xk, w_xv, w_xo, ffn_norm_g, router_w, router_b, e_w_gate, e_b_gate, e_w_up, e_b_up, e_w_down, e_b_down, final_norm_g):
    x = jnp.concatenate([x_prompt.reshape(N_PROMPT, D_MODEL), x_sample.reshape(N_SAMPLE, D_MODEL)], axis=0)
    mem = mem_prompt.reshape(BATCH * N_MEM, D_MODEL)
    outs = {k: [] for k in ('s5r_p', 's5i_p', 'mk_p', 'mv_p', 's5r_s', 's5i_s', 'gmv_s')}
    for l in range(DEPTH):
        proj = _norm_matmul(x, norm_mix_g[l], w_in[l].astype(BF16), ROW_BLOCK)
        sw = _s5_weights(s5_lambda_re[l], s5_lambda_im[l], s5_b_re[l], s5_b_im[l], s5_c_re[l], s5_c_im[l],
                         s5_d[l], s5_log_step[l], S5_SEG_PROMPT)
        w_glu = s5_w_glu[l].astype(BF16)
        og = out_norm_g[l]
        ms5, er_p, ei_p = _s5_call(proj, sw, w_glu, og[:S5_WIDTH], S5_SEG_PROMPT, True)
        h0 = (state_s5_re[l].reshape(DEC_BATCH // SUBLANES, SUBLANES, S5_COLS),
              state_s5_im[l].reshape(DEC_BATCH // SUBLANES, SUBLANES, S5_COLS))
        ms5, er_s, ei_s = _s5_call(proj, sw, w_glu, og[:S5_WIDTH], DEC_SEQ, False, h0=h0, prev=ms5)
        outs['s5r_p'].append(er_p[:, SUBLANES - 1].reshape(BATCH, S5_GROUPS, S5_STATE))
        outs['s5i_p'].append(ei_p[:, SUBLANES - 1].reshape(BATCH, S5_GROUPS, S5_STATE))
        outs['s5r_s'].append(er_s.reshape(DEC_BATCH, S5_GROUPS, S5_STATE))
        outs['s5i_s'].append(ei_s.reshape(DEC_BATCH, S5_GROUPS, S5_STATE))

        (mgm,) = _gmlp_call(proj, gm_ln_g[l], gm_ln_b[l], gm_w_s[l], gm_b_s[l], og[S5_WIDTH:],
                            GM_ROWS_PROMPT, GM_CHUNK, 0, N_PROMPT // GM_ROWS_PROMPT, False)
        mgm, vn = _gmlp_call(proj, gm_ln_g[l], gm_ln_b[l], gm_w_s[l], gm_b_s[l], og[S5_WIDTH:],
                             N_SAMPLE, DEC_SEQ, N_PROMPT // N_SAMPLE, 1, True, prev=mgm)
        outs['gmv_s'].append(vn.reshape(DEC_BATCH, DEC_SEQ, GM_HEADS, GM_HEAD_DIM))
        x = _out_proj(x, ms5, mgm, w_out[l].astype(BF16))

        mk = _norm_matmul(mem, mem_norm_g[l], w_xk[l].astype(BF16), ROW_BLOCK)
        mv = _norm_matmul(mem, mem_norm_g[l], w_xv[l].astype(BF16), ROW_BLOCK)
        outs['mk_p'].append(mk.reshape(BATCH, N_MEM, XA_HEADS, XA_HEAD_DIM))
        outs['mv_p'].append(mv.reshape(BATCH, N_MEM, XA_HEADS, XA_HEAD_DIM))
        wq = w_xq[l].astype(BF16)
        wo = w_xo[l].astype(BF16)
        xa = _xattn_call(x, xa_norm_g[l], wq, mk.reshape(BATCH, N_MEM, D_MODEL).astype(BF16),
                         mv.reshape(BATCH, N_MEM, D_MODEL).astype(BF16), wo,
                         XA_ROWS_PROMPT, 1, 0, N_PROMPT // XA_ROWS_PROMPT, SEQ // XA_ROWS_PROMPT)
        rows_s = XA_SAMPLE_BATCHES * DEC_SEQ
        x = _xattn_call(x, xa_norm_g[l], wq,
                        cache_mem_k[l].reshape(DEC_BATCH, N_MEM, D_MODEL).astype(BF16),
                        cache_mem_v[l].reshape(DEC_BATCH, N_MEM, D_MODEL).astype(BF16), wo,
                        rows_s, XA_SAMPLE_BATCHES, N_PROMPT // rows_s, N_SAMPLE // rows_s, 1, prev=xa)

        h, e_t, gates_t, cnt = _router(x, ffn_norm_g[l], router_w[l], router_b[l])
        dest_t, row_tok, blk_exp, n_used = _route(e_t, cnt)
        ys = _moe_call(l, h[row_tok], blk_exp, n_used, e_w_gate, e_b_gate, e_w_up, e_b_up, e_w_down, e_b_down)
        x = _combine(x, ys[dest_t.reshape(-1)], gates_t.T, final_norm_g if l == DEPTH - 1 else None)

    y = x
    y_prompt = y[:N_PROMPT].reshape(BATCH, SEQ, D_MODEL)
    y_sample = y[N_PROMPT:].reshape(DEC_BATCH, DEC_SEQ, D_MODEL)
    st = jnp.stack
    return (y_prompt, y_sample, st(outs['s5r_p']), st(outs['s5i_p']), st(outs['mk_p']), st(outs['mv_p']),
            st(outs['s5r_s']), st(outs['s5i_s']), st(outs['gmv_s']))
```

```python
import functools

import jax
import jax.numpy as jnp
from jax import lax
from jax.experimental import pallas as pl
from jax.experimental.pallas import tpu as pltpu

F32 = jnp.float32
BF16 = jnp.bfloat16

D_MODEL = 1024
BATCH = 4
SEQ = 4096
DEPTH = 2
DEC_BATCH = 16
DEC_SEQ = 32
S5_GROUPS = 32
S5_GROUP_CH = 16
S5_STATE = 64
S5_WIDTH = S5_GROUPS * S5_GROUP_CH
S5_COLS = S5_GROUPS * S5_STATE
GM_HEADS = 4
GM_HEAD_DIM = 128
GM_WIDTH = GM_HEADS * GM_HEAD_DIM
GM_CHUNK = 128
IN_PROJ = S5_WIDTH + 2 * GM_WIDTH
N_MEM = 256
XA_HEADS = 4
XA_HEAD_DIM = D_MODEL // XA_HEADS
N_EXPERTS = 32
TOP_K = 4
D_FF = D_MODEL
SWIGLU_LIMIT = 7.0
SWIGLU_ALPHA = 1.702
EPS = 1e-6
EIG_CLIP = -1e-4

N_PROMPT = BATCH * SEQ
N_SAMPLE = DEC_BATCH * DEC_SEQ
N_TOK = N_PROMPT + N_SAMPLE

SUBLANES = 8
LANES = 128
MXU_DIM = 256
VMEM_LIMIT = 48 * 1024 * 1024

ROW_BLOCK = 512
S5_SEG_PROMPT = 64
S5_COL_BLOCK = 512
GM_ROWS_PROMPT = 256
XA_ROWS_PROMPT = 512
XA_SAMPLE_BATCHES = 4
MOE_ROWS = 512
MOE_BLOCKS = -(-N_TOK * TOP_K // MOE_ROWS) + N_EXPERTS
MOE_CAST_ROWS = 128


def _params(*sem):
    return pltpu.CompilerParams(dimension_semantics=sem, vmem_limit_bytes=VMEM_LIMIT)


def _rms(x, g):
    return x * lax.rsqrt(jnp.mean(x * x, axis=-1, keepdims=True) + EPS) * g


def _dot(a, b):
    return jnp.dot(a, b, preferred_element_type=F32)


def _norm_matmul_kernel(x_ref, g_ref, w_ref, o_ref):
    o_ref[...] = _dot(_rms(x_ref[...], g_ref[...]).astype(BF16), w_ref[...])


def _norm_matmul(x, g, w, rows):
    n, d = x.shape
    dout = w.shape[1]
    return pl.pallas_call(
        _norm_matmul_kernel,
        grid=(n // rows,),
        in_specs=[pl.BlockSpec((rows, d), lambda i: (i, 0)),
                  pl.BlockSpec((1, d), lambda i: (0, 0)),
                  pl.BlockSpec((d, dout), lambda i: (0, 0))],
        out_specs=pl.BlockSpec((rows, dout), lambda i: (i, 0)),
        out_shape=jax.ShapeDtypeStruct((n, dout), F32),
        compiler_params=_params("parallel"),
        name="norm_matmul",
    )(x, g.reshape(1, d), w)


def _out_proj_kernel(x_ref, a_ref, b_ref, wa_ref, wb_ref, o_ref):
    o_ref[...] = x_ref[...] + _dot(a_ref[...], wa_ref[...]) + _dot(b_ref[...], wb_ref[...])


def _out_proj(x, a, b, w):
    n, d = x.shape
    wa, wb = w[:S5_WIDTH], w[S5_WIDTH:]
    rows = ROW_BLOCK
    return pl.pallas_call(
        _out_proj_kernel,
        grid=(n // rows,),
        in_specs=[pl.BlockSpec((rows, d), lambda i: (i, 0)),
                  pl.BlockSpec((rows, S5_WIDTH), lambda i: (i, 0)),
                  pl.BlockSpec((rows, GM_WIDTH), lambda i: (i, 0)),
                  pl.BlockSpec((S5_WIDTH, d), lambda i: (0, 0)),
                  pl.BlockSpec((GM_WIDTH, d), lambda i: (0, 0))],
        out_specs=pl.BlockSpec((rows, d), lambda i: (i, 0)),
        out_shape=jax.ShapeDtypeStruct((n, d), F32),
        compiler_params=_params("parallel"),
        name="out_proj",
    )(x, a, b, wa, wb)


def _router_kernel(x_ref, g_ref, wt_ref, b_ref, h_ref, e_ref, gate_ref, cnt_ref):
    h = _rms(x_ref[...], g_ref[...])
    h_ref[...] = h
    logits = lax.dot_general(wt_ref[...], h.astype(BF16), (((1,), (1,)), ((), ())),
                             preferred_element_type=F32) + b_ref[...]
    sub = lax.broadcasted_iota(jnp.int32, logits.shape, 0)
    work = logits
    chosen = jnp.zeros(logits.shape, F32)
    vals, idxs = [], []
    for _ in range(TOP_K):
        m = jnp.max(work, axis=0, keepdims=True)
        idx = jnp.min(jnp.where(work == m, sub, N_EXPERTS), axis=0, keepdims=True)
        sel = sub == idx
        vals.append(m)
        idxs.append(idx)
        work = jnp.where(sel, -jnp.inf, work)
        chosen = jnp.where(sel, 1.0, chosen)
    ex = [jnp.exp(v - vals[0]) for v in vals]
    den = ex[0] + ex[1] + ex[2] + ex[3]
    gate_ref[...] = jnp.concatenate([e / den for e in ex], axis=0)
    e_ref[...] = jnp.concatenate(idxs, axis=0)
    cnt_ref[...] = jnp.broadcast_to(jnp.sum(chosen, axis=1, keepdims=True), (N_EXPERTS, LANES))


def _router(x, g, w, b):
    n, d = x.shape
    rows = ROW_BLOCK
    return pl.pallas_call(
        _router_kernel,
        grid=(n // rows,),
        in_specs=[pl.BlockSpec((rows, d), lambda i: (i, 0)),
                  pl.BlockSpec((1, d), lambda i: (0, 0)),
                  pl.BlockSpec((N_EXPERTS, d), lambda i: (0, 0)),
                  pl.BlockSpec((N_EXPERTS, 1), lambda i: (0, 0))],
        out_specs=[pl.BlockSpec((rows, d), lambda i: (i, 0)),
                   pl.BlockSpec((TOP_K, rows), lambda i: (0, i)),
                   pl.BlockSpec((TOP_K, rows), lambda i: (0, i)),
                   pl.BlockSpec((None, N_EXPERTS, LANES), lambda i: (i, 0, 0))],
        out_shape=[jax.ShapeDtypeStruct((n, d), F32),
                   jax.ShapeDtypeStruct((TOP_K, n), jnp.int32),
                   jax.ShapeDtypeStruct((TOP_K, n), F32),
                   jax.ShapeDtypeStruct((n // rows, N_EXPERTS, LANES), F32)],
        compiler_params=_params("parallel"),
        name="router",
    )(x, g.reshape(1, d), w.T.astype(BF16), b.reshape(N_EXPERTS, 1))


def _dest_kernel(e_ref, base_ref, tri_ref, d_ref):
    e = e_ref[...]
    sub = lax.broadcasted_iota(jnp.int32, (N_EXPERTS, e.shape[1]), 0)
    hot = [sub == e[k:k + 1, :] for k in range(TOP_K)]
    chosen = sum(jnp.where(hk, 1.0, 0.0) for hk in hot)
    pos = base_ref[...] + _dot(chosen.astype(BF16), tri_ref[...])
    d_ref[...] = jnp.concatenate(
        [jnp.sum(jnp.where(hk, pos, 0.0), axis=0, keepdims=True) for hk in hot], axis=0).astype(jnp.int32)


def _dest(e_t, base):
    n = e_t.shape[1]
    rows = ROW_BLOCK
    tri = jnp.triu(jnp.ones((rows, rows), F32), 1).astype(BF16)
    return pl.pallas_call(
        _dest_kernel,
        grid=(n // rows,),
        in_specs=[pl.BlockSpec((TOP_K, rows), lambda i: (0, i)),
                  pl.BlockSpec((None, N_EXPERTS, 1), lambda i: (i, 0, 0)),
                  pl.BlockSpec((rows, rows), lambda i: (0, 0))],
        out_specs=pl.BlockSpec((TOP_K, rows), lambda i: (0, i)),
        out_shape=jax.ShapeDtypeStruct((TOP_K, n), jnp.int32),
        compiler_params=_params("parallel"),
        name="moe_dest",
    )(e_t, base, tri)


def _combine_kernel(*refs, final):
    if final:
        x_ref, g0, g1, g2, g3, gate_ref, fg_ref, o_ref = refs
    else:
        x_ref, g0, g1, g2, g3, gate_ref, o_ref = refs
    acc = x_ref[...]
    for k, g in enumerate((g0, g1, g2, g3)):
        acc = acc + gate_ref[:, k:k + 1] * g[...]
    o_ref[...] = _rms(acc, fg_ref[...]) if final else acc


def _combine(x, picked, gates, final_g=None):
    n, d = x.shape
    rows = ROW_BLOCK
    picked = picked.reshape(TOP_K, n, d)
    final = final_g is not None
    in_specs = [pl.BlockSpec((rows, d), lambda i: (i, 0))]
    in_specs += [pl.BlockSpec((None, rows, d), functools.partial(lambda i, k: (k, i, 0), k=k))
                 for k in range(TOP_K)]
    in_specs.append(pl.BlockSpec((rows, TOP_K), lambda i: (i, 0)))
    args = [x, picked, picked, picked, picked, gates]
    if final:
        in_specs.append(pl.BlockSpec((1, d), lambda i: (0, 0)))
        args.append(final_g.reshape(1, d))
    return pl.pallas_call(
        functools.partial(_combine_kernel, final=final),
        grid=(n // rows,),
        in_specs=in_specs,
        out_specs=pl.BlockSpec((rows, d), lambda i: (i, 0)),
        out_shape=jax.ShapeDtypeStruct((n, d), F32),
        compiler_params=_params("parallel"),
        name="moe_combine",
    )(*args)


def _cmul_add(ar, ai, hr, hi, br, bi):
    return ar * hr - ai * hi + br, ar * hi + ai * hr + bi


def _s5_kernel(*refs, seg, chained):
    if chained:
        (u_ref, wbr_ref, wbi_ref, wcr_ref, wci_ref, ar_ref, ai_ref, pr_ref, pi_ref, d_ref,
         wglu_ref, og_ref, o_ref, er_ref, ei_ref,
         up_ref, hr_ref, hi_ref, hbr_ref, hbi_ref, sr_ref, si_ref, op_ref, cr_ref, ci_ref) = refs
    else:
        (u_ref, wbr_ref, wbi_ref, wcr_ref, wci_ref, ar_ref, ai_ref, pr_ref, pi_ref, d_ref,
         wglu_ref, og_ref, h0r_ref, h0i_ref, o_ref, er_ref, ei_ref,
         up_ref, hr_ref, hi_ref, hbr_ref, hbi_ref, sr_ref, si_ref, op_ref) = refs
    rows = seg * SUBLANES
    half = S5_COLS // 2

    n_lane_blk = S5_WIDTH // LANES
    for j in range(SUBLANES):
        for i0 in range(0, seg, SUBLANES):
            for c in range(n_lane_blk):
                up_ref[c, pl.ds(i0 * SUBLANES + j, SUBLANES, stride=SUBLANES), :] = (
                    u_ref[j * seg + i0:j * seg + i0 + SUBLANES, c * LANES:(c + 1) * LANES])

    up = jnp.concatenate([up_ref[c] for c in range(n_lane_blk)], axis=1)
    ub = up.astype(BF16)
    for k in range(2):
        uk = ub[:, k * MXU_DIM:(k + 1) * MXU_DIM]
        hr_ref[:, k * half:(k + 1) * half] = _dot(uk, wbr_ref[k])
        hi_ref[:, k * half:(k + 1) * half] = _dot(uk, wbi_ref[k])

    for cb in range(S5_COLS // S5_COL_BLOCK):
        cols = slice(cb * S5_COL_BLOCK, (cb + 1) * S5_COL_BLOCK)
        ar = jnp.broadcast_to(ar_ref[:, cols], (SUBLANES, S5_COL_BLOCK))
        ai = jnp.broadcast_to(ai_ref[:, cols], (SUBLANES, S5_COL_BLOCK))

        def step(i, carry, cols=cols, ar=ar, ai=ai):
            r = pl.multiple_of(i * SUBLANES, SUBLANES)
            h_r, h_i = _cmul_add(ar, ai, carry[0], carry[1],
                                 hr_ref[pl.ds(r, SUBLANES), cols], hi_ref[pl.ds(r, SUBLANES), cols])
            hr_ref[pl.ds(r, SUBLANES), cols] = h_r
            hi_ref[pl.ds(r, SUBLANES), cols] = h_i
            return h_r, h_i

        zero = jnp.zeros((SUBLANES, S5_COL_BLOCK), F32)
        lax.fori_loop(0, seg, step, (zero, zero), unroll=2)

    end_r = hr_ref[rows - SUBLANES:rows, :]
    end_i = hi_ref[rows - SUBLANES:rows, :]
    as_r = pr_ref[seg - 1:seg, :]
    as_i = pi_ref[seg - 1:seg, :]
    if chained:
        @pl.when(pl.program_id(1) == 0)
        def _():
            cr_ref[...] = jnp.zeros_like(cr_ref)
            ci_ref[...] = jnp.zeros_like(ci_ref)

        cur_r, cur_i = cr_ref[...], ci_ref[...]
        for j in range(SUBLANES):
            sr_ref[j:j + 1, :] = cur_r
            si_ref[j:j + 1, :] = cur_i
            cur_r, cur_i = _cmul_add(as_r, as_i, cur_r, cur_i, end_r[j:j + 1], end_i[j:j + 1])
        cr_ref[...] = cur_r
        ci_ref[...] = cur_i
    else:
        sr_ref[...] = h0r_ref[...]
        si_ref[...] = h0i_ref[...]
    t_r, t_i = _cmul_add(as_r, as_i, sr_ref[...], si_ref[...], end_r, end_i)
    er_ref[...] = t_r
    ei_ref[...] = t_i

    for cb in range(S5_COLS // S5_COL_BLOCK):
        cols = slice(cb * S5_COL_BLOCK, (cb + 1) * S5_COL_BLOCK)
        s_r = jnp.concatenate([sr_ref[:, cols]] * 2, axis=0)
        s_i = jnp.concatenate([si_ref[:, cols]] * 2, axis=0)

        def fix(i2, _, cols=cols, s_r=s_r, s_i=s_i):
            r = pl.multiple_of(i2 * 2 * SUBLANES, 2 * SUBLANES)
            shape = (SUBLANES, S5_COL_BLOCK)
            p_r = jnp.concatenate(
                [jnp.broadcast_to(pr_ref[pl.ds(2 * i2, 1), cols], shape),
                 jnp.broadcast_to(pr_ref[pl.ds(2 * i2 + 1, 1), cols], shape)], axis=0)
            p_i = jnp.concatenate(
                [jnp.broadcast_to(pi_ref[pl.ds(2 * i2, 1), cols], shape),
                 jnp.broadcast_to(pi_ref[pl.ds(2 * i2 + 1, 1), cols], shape)], axis=0)
            h_r, h_i = _cmul_add(p_r, p_i, s_r, s_i,
                                 hr_ref[pl.ds(r, 2 * SUBLANES), cols],
                                 hi_ref[pl.ds(r, 2 * SUBLANES), cols])
            hbr_ref[pl.ds(r, 2 * SUBLANES), cols] = h_r.astype(BF16)
            hbi_ref[pl.ds(r, 2 * SUBLANES), cols] = h_i.astype(BF16)
            return 0

        lax.fori_loop(0, seg // 2, fix, 0)

    slabs_per_window = MXU_DIM // (MXU_DIM // S5_STATE * S5_GROUP_CH)
    ys = []
    for w in range(S5_WIDTH // MXU_DIM):
        acc = d_ref[:, w * MXU_DIM:(w + 1) * MXU_DIM] * jnp.concatenate(
            [up_ref[c] for c in range(w * MXU_DIM // LANES, (w + 1) * MXU_DIM // LANES)], axis=1)
        for s in range(slabs_per_window):
            j = w * slabs_per_window + s
            acc += _dot(hbr_ref[:, j * MXU_DIM:(j + 1) * MXU_DIM], wcr_ref[j])
            acc += _dot(hbi_ref[:, j * MXU_DIM:(j + 1) * MXU_DIM], wci_ref[j])
        ys.append(jax.nn.gelu(acc).astype(BF16))

    z = _dot(jnp.concatenate(ys, axis=1), wglu_ref[...])
    o = _rms(z[:, :S5_WIDTH] * jax.nn.sigmoid(z[:, S5_WIDTH:]), og_ref[...])
    for c in range(n_lane_blk):
        op_ref[c] = o[:, c * LANES:(c + 1) * LANES]

    for j in range(SUBLANES):
        for i0 in range(0, seg, 2 * SUBLANES):
            for c in range(n_lane_blk):
                o_ref[j * seg + i0:j * seg + i0 + 2 * SUBLANES, c * LANES:(c + 1) * LANES] = (
                    op_ref[c, pl.ds(i0 * SUBLANES + j, 2 * SUBLANES, stride=SUBLANES), :].astype(BF16))


def _s5_weights(lam_re, lam_im, b_re, b_im, c_re, c_im, d, log_step, seg):
    step = jnp.exp(log_step)[:, None]
    lr = jnp.minimum(lam_re, EIG_CLIP)
    li = lam_im
    mag = jnp.exp(lr * step)
    ar = mag * jnp.cos(li * step)
    ai = mag * jnp.sin(li * step)
    den = lr * lr + li * li
    fr = ((ar - 1.0) * lr + ai * li) / den
    fi = (ai * lr - (ar - 1.0) * li) / den
    bbar_r = fr[..., None] * b_re - fi[..., None] * b_im
    bbar_i = fr[..., None] * b_im + fi[..., None] * b_re
    t = jnp.arange(1, seg + 1, dtype=F32)[:, None, None]
    pm = jnp.exp(lr * step * t)
    p_r = (pm * jnp.cos(li * step * t)).reshape(seg, S5_COLS)
    p_i = (pm * jnp.sin(li * step * t)).reshape(seg, S5_COLS)

    gk = MXU_DIM // S5_GROUP_CH

    def in_map(bbar):
        bb = bbar.reshape(S5_GROUPS // gk, gk, S5_STATE, S5_GROUP_CH)
        w = jnp.einsum('kgnh,gG->kghGn', bb, jnp.eye(gk, dtype=F32))
        return w.reshape(S5_GROUPS // gk, MXU_DIM, gk * S5_STATE).astype(BF16)

    gs = MXU_DIM // S5_STATE
    n_slab = S5_COLS // MXU_DIM
    per_win = MXU_DIM // (gs * S5_GROUP_CH)

    def out_map(c):
        cc = c.reshape(n_slab // per_win, per_win, gs, S5_GROUP_CH, S5_STATE)
        w = jnp.einsum('wsghn,sS,gG->wsgnSGh', cc, jnp.eye(per_win, dtype=F32), jnp.eye(gs, dtype=F32))
        return w.reshape(n_slab, MXU_DIM, MXU_DIM).astype(BF16)

    return dict(wbr=in_map(bbar_r), wbi=in_map(bbar_i), wcr=out_map(c_re), wci=out_map(-c_im),
                ar=ar.reshape(1, S5_COLS), ai=ai.reshape(1, S5_COLS), pr=p_r, pi=p_i,
                d=d.reshape(1, S5_WIDTH))


def _s5_call(proj, sw, w_glu, og, seg, chained, h0=None, prev=None):
    rows = seg * SUBLANES
    const2 = lambda *_: (0, 0)
    const3 = lambda *_: (0, 0, 0)
    if chained:
        n_chunk = SEQ // rows
        grid = (BATCH, n_chunk)
        row_map = lambda b, c: (b * n_chunk + c, 0)
        end_map = lambda b, c: (b, 0, 0)
        n_end = BATCH
        sem = ("parallel", "arbitrary")
    else:
        grid = (DEC_BATCH // SUBLANES,)
        row_map = lambda k: (N_PROMPT // rows + k, 0)
        end_map = lambda k: (k, 0, 0)
        n_end = DEC_BATCH // SUBLANES
        sem = ("parallel",)
    half = S5_COLS // 2
    in_specs = [pl.BlockSpec((rows, S5_WIDTH), row_map),
                pl.BlockSpec((2, MXU_DIM, half), const3),
                pl.BlockSpec((2, MXU_DIM, half), const3),
                pl.BlockSpec((S5_COLS // MXU_DIM, MXU_DIM, MXU_DIM), const3),
                pl.BlockSpec((S5_COLS // MXU_DIM, MXU_DIM, MXU_DIM), const3),
                pl.BlockSpec((1, S5_COLS), const2),
                pl.BlockSpec((1, S5_COLS), const2),
                pl.BlockSpec((seg, S5_COLS), const2),
                pl.BlockSpec((seg, S5_COLS), const2),
                pl.BlockSpec((1, S5_WIDTH), const2),
                pl.BlockSpec((S5_WIDTH, 2 * S5_WIDTH), const2),
                pl.BlockSpec((1, S5_WIDTH), const2)]
    args = [proj, sw['wbr'], sw['wbi'], sw['wcr'], sw['wci'], sw['ar'], sw['ai'],
            sw['pr'][:seg], sw['pi'][:seg], sw['d'], w_glu, og.reshape(1, S5_WIDTH)]
    scratch = [pltpu.VMEM((S5_WIDTH // LANES, rows, LANES), F32),
               pltpu.VMEM((rows, S5_COLS), F32), pltpu.VMEM((rows, S5_COLS), F32),
               pltpu.VMEM((rows, S5_COLS), BF16), pltpu.VMEM((rows, S5_COLS), BF16),
               pltpu.VMEM((SUBLANES, S5_COLS), F32), pltpu.VMEM((SUBLANES, S5_COLS), F32),
               pltpu.VMEM((S5_WIDTH // LANES, rows, LANES), F32)]
    aliases = {}
    if chained:
        scratch += [pltpu.VMEM((1, S5_COLS), F32), pltpu.VMEM((1, S5_COLS), F32)]
    else:
        in_specs += [pl.BlockSpec((None, SUBLANES, S5_COLS), end_map),
                     pl.BlockSpec((None, SUBLANES, S5_COLS), end_map)]
        args += [h0[0], h0[1]]
    if prev is not None:
        in_specs.append(pl.BlockSpec(memory_space=pl.ANY))
        args.append(prev)
        aliases = {len(args) - 1: 0}
    kern = functools.partial(_s5_kernel, seg=seg, chained=chained)
    if prev is not None:
        kern = _drop_last_input(kern, len(args))
    return pl.pallas_call(
        kern,
        grid=grid,
        in_specs=in_specs,
        out_specs=[pl.BlockSpec((rows, S5_WIDTH), row_map),
                   pl.BlockSpec((None, SUBLANES, S5_COLS), end_map),
                   pl.BlockSpec((None, SUBLANES, S5_COLS), end_map)],
        out_shape=[jax.ShapeDtypeStruct((N_TOK, S5_WIDTH), BF16),
                   jax.ShapeDtypeStruct((n_end, SUBLANES, S5_COLS), F32),
                   jax.ShapeDtypeStruct((n_end, SUBLANES, S5_COLS), F32)],
        scratch_shapes=scratch,
        input_output_aliases=aliases,
        compiler_params=_params(*sem),
        name="s5_chained" if chained else "s5_streams",
    )(*args)


def _drop_last_input(kern, n_in):
    def wrapped(*refs):
        return kern(*refs[:n_in - 1], *refs[n_in:])
    return wrapped


def _gmlp_kernel(*refs, chunk, emit_vn):
    if emit_vn:
        u_ref, v_ref, lng_ref, lnb_ref, ws_ref, bst_ref, og_ref, o_ref, vn_ref, acc_ref = refs
    else:
        u_ref, v_ref, lng_ref, lnb_ref, ws_ref, bst_ref, og_ref, o_ref, acc_ref = refs
    rows = u_ref.shape[0]
    u = jax.nn.gelu(u_ref[...])
    v = jax.nn.gelu(v_ref[...])
    tri = (lax.broadcasted_iota(jnp.int32, (chunk, chunk), 0)
           >= lax.broadcasted_iota(jnp.int32, (chunk, chunk), 1))
    for h in range(GM_HEADS):
        cols = slice(h * GM_HEAD_DIM, (h + 1) * GM_HEAD_DIM)
        vh = v[:, cols]
        mu = jnp.mean(vh, axis=-1, keepdims=True)
        cen = vh - mu
        var = jnp.mean(cen * cen, axis=-1, keepdims=True)
        vn = cen * lax.rsqrt(var + EPS) * lng_ref[:, cols] + lnb_ref[:, cols]
        if emit_vn:
            vn_ref[:, cols] = vn
        w = jnp.where(tri, ws_ref[h], 0.0).astype(BF16)
        vb = vn.astype(BF16)
        bias = bst_ref[:, h:h + 1]
        for c in range(rows // chunk):
            rs = slice(c * chunk, (c + 1) * chunk)
            acc_ref[rs, cols] = u[rs, cols] * (_dot(w, vb[rs]) + bias)
    o_ref[...] = _rms(acc_ref[...], og_ref[...]).astype(BF16)


def _gmlp_call(proj, ln_g, ln_b, w_s, b_s, og, rows, chunk, first_block, n_blocks, emit_vn, prev=None):
    const2 = lambda i: (0, 0)
    in_specs = [pl.BlockSpec((rows, GM_WIDTH), lambda i: (first_block + i, 1)),
                pl.BlockSpec((rows, GM_WIDTH), lambda i: (first_block + i, 2)),
                pl.BlockSpec((1, GM_WIDTH), const2),
                pl.BlockSpec((1, GM_WIDTH), const2),
                pl.BlockSpec((GM_HEADS, chunk, chunk), lambda i: (0, 0, 0)),
                pl.BlockSpec((chunk, GM_HEADS), const2),
                pl.BlockSpec((1, GM_WIDTH), const2)]
    args = [proj, proj, ln_g.reshape(1, GM_WIDTH), ln_b.reshape(1, GM_WIDTH),
            w_s[:, :chunk, :chunk], b_s[:, :chunk].T, og.reshape(1, GM_WIDTH)]
    out_specs = [pl.BlockSpec((rows, GM_WIDTH), lambda i: (first_block + i, 0))]
    out_shape = [jax.ShapeDtypeStruct((N_TOK, GM_WIDTH), BF16)]
    if emit_vn:
        out_specs.append(pl.BlockSpec((rows, GM_WIDTH), lambda i: (i, 0)))
        out_shape.append(jax.ShapeDtypeStruct((n_blocks * rows, GM_WIDTH), F32))
    kern = functools.partial(_gmlp_kernel, chunk=chunk, emit_vn=emit_vn)
    aliases = {}
    if prev is not None:
        in_specs.append(pl.BlockSpec(memory_space=pl.ANY))
        args.append(prev)
        aliases = {len(args) - 1: 0}
        kern = _drop_last_input(kern, len(args))
    return pl.pallas_call(
        kern,
        grid=(n_blocks,),
        in_specs=in_specs,
        out_specs=out_specs,
        out_shape=out_shape,
        scratch_shapes=[pltpu.VMEM((rows, GM_WIDTH), F32)],
        input_output_aliases=aliases,
        compiler_params=_params("parallel"),
        name="gmlp_vn" if emit_vn else "gmlp",
    )(*args)


def _xattn_kernel(x_ref, g_ref, wq_ref, k_ref, v_ref, wo_ref, o_ref, att_ref, *, n_batch):
    rows = x_ref.shape[0]
    tb = rows // n_batch
    x = x_ref[...]
    q = _dot(_rms(x, g_ref[...]).astype(BF16), wq_ref[...]).astype(BF16)
    scale = XA_HEAD_DIM ** -0.5
    for b in range(n_batch):
        rs = slice(b * tb, (b + 1) * tb)
        for h in range(XA_HEADS):
            cols = slice(h * XA_HEAD_DIM, (h + 1) * XA_HEAD_DIM)
            s = lax.dot_general(q[rs, cols], k_ref[b, :, cols], (((1,), (1,)), ((), ())),
                                preferred_element_type=F32) * scale
            e = jnp.exp(s - jnp.max(s, axis=-1, keepdims=True))
            p = e / jnp.sum(e, axis=-1, keepdims=True)
            att_ref[rs, cols] = _dot(p.astype(BF16), v_ref[b, :, cols]).astype(BF16)
    o_ref[...] = x + _dot(att_ref[...], wo_ref[...])


def _xattn_call(x, g, wq, k, v, wo, rows, n_batch, first_block, n_blocks, blocks_per_kv, prev=None):
    const2 = lambda i: (0, 0)
    in_specs = [pl.BlockSpec((rows, D_MODEL), lambda i: (first_block + i, 0)),
                pl.BlockSpec((1, D_MODEL), const2),
                pl.BlockSpec((D_MODEL, D_MODEL), const2),
                pl.BlockSpec((n_batch, N_MEM, D_MODEL), lambda i: (i // blocks_per_kv, 0, 0)),
                pl.BlockSpec((n_batch, N_MEM, D_MODEL), lambda i: (i // blocks_per_kv, 0, 0)),
                pl.BlockSpec((D_MODEL, D_MODEL), const2)]
    args = [x, g.reshape(1, D_MODEL), wq, k, v, wo]
    kern = functools.partial(_xattn_kernel, n_batch=n_batch)
    aliases = {}
    if prev is not None:
        in_specs.append(pl.BlockSpec(memory_space=pl.ANY))
        args.append(prev)
        aliases = {len(args) - 1: 0}
        kern = _drop_last_input(kern, len(args))
    return pl.pallas_call(
        kern,
        grid=(n_blocks,),
        in_specs=in_specs,
        out_specs=pl.BlockSpec((rows, D_MODEL), lambda i: (first_block + i, 0)),
        out_shape=jax.ShapeDtypeStruct((N_TOK, D_MODEL), F32),
        scratch_shapes=[pltpu.VMEM((rows, D_MODEL), BF16)],
        input_output_aliases=aliases,
        compiler_params=_params("parallel"),
        name="xattn",
    )(*args)


def _moe_kernel(be_ref, nu_ref, x_ref, wg_ref, bg_ref, wu_ref, bu_ref, wd_ref, bd_ref, o_ref,
                wgb_ref, wub_ref, wdb_ref):
    i = pl.program_id(0)
    fresh = jnp.logical_or(i == 0, be_ref[i] != be_ref[jnp.maximum(i - 1, 0)])

    @pl.when(jnp.logical_and(i < nu_ref[0], fresh))
    def _():
        def cast(r, _):
            rs = pl.ds(pl.multiple_of(r * MOE_CAST_ROWS, MOE_CAST_ROWS), MOE_CAST_ROWS)
            wgb_ref[rs, :] = wg_ref[rs, :].astype(BF16)
            wub_ref[rs, :] = wu_ref[rs, :].astype(BF16)
            wdb_ref[rs, :] = wd_ref[rs, :].astype(BF16)
            return 0
        lax.fori_loop(0, D_MODEL // MOE_CAST_ROWS, cast, 0)

    @pl.when(i < nu_ref[0])
    def _():
        x = x_ref[...].astype(BF16)
        g = jnp.minimum(_dot(x, wgb_ref[...]) + bg_ref[...], SWIGLU_LIMIT)
        u = jnp.clip(_dot(x, wub_ref[...]) + bu_ref[...], -SWIGLU_LIMIT, SWIGLU_LIMIT)
        a = g * jax.nn.sigmoid(SWIGLU_ALPHA * g) * (u + 1.0)
        o_ref[...] = _dot(a.astype(BF16), wdb_ref[...]) + bd_ref[...]


def _moe_call(layer, xs, blk_exp, n_used, w_gate, b_gate, w_up, b_up, w_down, b_down):
    def row_map(i, be, nu):
        return (jnp.minimum(i, nu[0] - 1), 0)

    def w_map(i, be, nu):
        return (layer, be[i], 0, 0)

    w_spec = pl.BlockSpec((None, None, D_MODEL, D_FF), w_map)
    b_spec = pl.BlockSpec((None, None, 1, D_FF), w_map)
    grid_spec = pltpu.PrefetchScalarGridSpec(
        num_scalar_prefetch=2,
        grid=(MOE_BLOCKS,),
        in_specs=[pl.BlockSpec((MOE_ROWS, D_MODEL), row_map),
                  w_spec, b_spec, w_spec, b_spec, w_spec, b_spec],
        out_specs=pl.BlockSpec((MOE_ROWS, D_MODEL), row_map),
        scratch_shapes=[pltpu.VMEM((D_MODEL, D_FF), BF16)] * 3,
    )
    shape4 = (DEPTH, N_EXPERTS, 1, D_FF)
    return pl.pallas_call(
        _moe_kernel,
        grid_spec=grid_spec,
        out_shape=jax.ShapeDtypeStruct((MOE_BLOCKS * MOE_ROWS, D_MODEL), F32),
        compiler_params=pltpu.CompilerParams(dimension_semantics=("arbitrary",),
                                             vmem_limit_bytes=56 * 1024 * 1024),
        name="moe_experts",
    )(blk_exp, n_used, xs, w_gate, b_gate.reshape(shape4), w_up, b_up.reshape(shape4),
      w_down, b_down.reshape(shape4))


def _route(e_t, cnt):
    blk_cnt = cnt[:, :, 0].astype(jnp.int32)
    counts = jnp.sum(blk_cnt, axis=0)
    padded = ((counts + MOE_ROWS - 1) // MOE_ROWS) * MOE_ROWS
    pend = jnp.cumsum(padded)
    pstart = pend - padded
    base = pstart[None, :] + jnp.cumsum(blk_cnt, axis=0) - blk_cnt
    dest_t = _dest(e_t, base.astype(F32)[:, :, None])
    n_used = (pend[-1] // MOE_ROWS).astype(jnp.int32)
    blk = jnp.minimum(jnp.arange(MOE_BLOCKS, dtype=jnp.int32), n_used - 1) * MOE_ROWS
    blk_exp = jnp.minimum(jnp.sum((pend[None, :] <= blk[:, None]).astype(jnp.int32), axis=1), N_EXPERTS - 1)
    tok = jnp.tile(jnp.arange(N_TOK, dtype=jnp.int32), TOP_K)
    row_tok = jnp.zeros((MOE_BLOCKS * MOE_ROWS,), jnp.int32).at[dest_t.reshape(-1)].set(
        tok, unique_indices=True)
    return dest_t, row_tok, blk_exp, n_used.reshape(1)


def kernel(x_prompt, x_sample, mem_prompt, state_s5_re, state_s5_im, cache_mem_k, cache_mem_v, norm_mix_g, w_in, s5_lambda_re, s5_lambda_im, s5_b_re, s5_b_im, s5_c_re, s5_c_im, s5_d, s5_log_step, s5_w_glu, gm_ln_g, gm_ln_b, gm_w_s, gm_b_s, out_norm_g, w_out, xa_norm_g, mem_norm_g, w_xq, w_xk, w_xv, w_xo, ffn_norm_g, router_w, router_b, e_w_gate, e_b_gate, e_w_up, e_b_up, e_w_down, e_b_down, final_norm_g):
    x = jnp.concatenate([x_prompt.reshape(N_PROMPT, D_MODEL), x_sample.reshape(N_SAMPLE, D_MODEL)], axis=0)
    mem = mem_prompt.reshape(BATCH * N_MEM, D_MODEL)
    outs = {k: [] for k in ('s5r_p', 's5i_p', 'mk_p', 'mv_p', 's5r_s', 's5i_s', 'gmv_s')}
    for l in range(DEPTH):
        proj = _norm_matmul(x, norm_mix_g[l], w_in[l].astype(BF16), ROW_BLOCK)
        sw = _s5_weights(s5_lambda_re[l], s5_lambda_im[l], s5_b_re[l], s5_b_im[l], s5_c_re[l], s5_c_im[l],
                         s5_d[l], s5_log_step[l], S5_SEG_PROMPT)
        w_glu = s5_w_glu[l].astype(BF16)
        og = out_norm_g[l]
        ms5, er_p, ei_p = _s5_call(proj, sw, w_glu, og[:S5_WIDTH], S5_SEG_PROMPT, True)
        h0 = (state_s5_re[l].reshape(DEC_BATCH // SUBLANES, SUBLANES, S5_COLS),
              state_s5_im[l].reshape(DEC_BATCH // SUBLANES, SUBLANES, S5_COLS))
        ms5, er_s, ei_s = _s5_call(proj, sw, w_glu, og[:S5_WIDTH], DEC_SEQ, False, h0=h0, prev=ms5)
        outs['s5r_p'].append(er_p[:, SUBLANES - 1].reshape(BATCH, S5_GROUPS, S5_STATE))
        outs['s5i_p'].append(ei_p[:, SUBLANES - 1].reshape(BATCH, S5_GROUPS, S5_STATE))
        outs['s5r_s'].append(er_s.reshape(DEC_BATCH, S5_GROUPS, S5_STATE))
        outs['s5i_s'].append(ei_s.reshape(DEC_BATCH, S5_GROUPS, S5_STATE))

        (mgm,) = _gmlp_call(proj, gm_ln_g[l], gm_ln_b[l], gm_w_s[l], gm_b_s[l], og[S5_WIDTH:],
                            GM_ROWS_PROMPT, GM_CHUNK, 0, N_PROMPT // GM_ROWS_PROMPT, False)
        mgm, vn = _gmlp_call(proj, gm_ln_g[l], gm_ln_b[l], gm_w_s[l], gm_b_s[l], og[S5_WIDTH:],
                             N_SAMPLE, DEC_SEQ, N_PROMPT // N_SAMPLE, 1, True, prev=mgm)
        outs['gmv_s'].append(vn.reshape(DEC_BATCH, DEC_SEQ, GM_HEADS, GM_HEAD_DIM))
        x = _out_proj(x, ms5, mgm, w_out[l].astype(BF16))

        mk = _norm_matmul(mem, mem_norm_g[l], w_xk[l].astype(BF16), ROW_BLOCK)
        mv = _norm_matmul(mem, mem_norm_g[l], w_xv[l].astype(BF16), ROW_BLOCK)
        outs['mk_p'].append(mk.reshape(BATCH, N_MEM, XA_HEADS, XA_HEAD_DIM))
        outs['mv_p'].append(mv.reshape(BATCH, N_MEM, XA_HEADS, XA_HEAD_DIM))
        wq = w_xq[l].astype(BF16)
        wo = w_xo[l].astype(BF16)
        xa = _xattn_call(x, xa_norm_g[l], wq, mk.reshape(BATCH, N_MEM, D_MODEL).astype(BF16),
                         mv.reshape(BATCH, N_MEM, D_MODEL).astype(BF16), wo,
                         XA_ROWS_PROMPT, 1, 0, N_PROMPT // XA_ROWS_PROMPT, SEQ // XA_ROWS_PROMPT)
        rows_s = XA_SAMPLE_BATCHES * DEC_SEQ
        x = _xattn_call(x, xa_norm_g[l], wq,
                        cache_mem_k[l].reshape(DEC_BATCH, N_MEM, D_MODEL).astype(BF16),
                        cache_mem_v[l].reshape(DEC_BATCH, N_MEM, D_MODEL).astype(BF16), wo,
                        rows_s, XA_SAMPLE_BATCHES, N_PROMPT // rows_s, N_SAMPLE // rows_s, 1, prev=xa)

        h, e_t, gates_t, cnt = _router(x, ffn_norm_g[l], router_w[l], router_b[l])
        dest_t, row_tok, blk_exp, n_used = _route(e_t, cnt)
        ys = _moe_call(l, h[row_tok], blk_exp, n_used, e_w_gate, e_b_gate, e_w_up, e_b_up, e_w_down, e_b_down)
        x = _combine(x, ys[dest_t.reshape(-1)], gates_t.T, final_norm_g if l == DEPTH - 1 else None)

    y = x
    y_prompt = y[:N_PROMPT].reshape(BATCH, SEQ, D_MODEL)
    y_sample = y[N_PROMPT:].reshape(DEC_BATCH, DEC_SEQ, D_MODEL)
    st = jnp.stack
    return (y_prompt, y_sample, st(outs['s5r_p']), st(outs['s5i_p']), st(outs['mk_p']), st(outs['mv_p']),
            st(outs['s5r_s']), st(outs['s5i_s']), st(outs['gmv_s']))
```

```python
import functools

import jax
import jax.numpy as jnp
from jax import lax
from jax.experimental import pallas as pl
from jax.experimental.pallas import tpu as pltpu

F32 = jnp.float32
BF16 = jnp.bfloat16

D_MODEL = 1024
BATCH = 4
SEQ = 4096
DEPTH = 2
DEC_BATCH = 16
DEC_SEQ = 32
S5_GROUPS = 32
S5_GROUP_CH = 16
S5_STATE = 64
S5_WIDTH = S5_GROUPS * S5_GROUP_CH
S5_COLS = S5_GROUPS * S5_STATE
GM_HEADS = 4
GM_HEAD_DIM = 128
GM_WIDTH = GM_HEADS * GM_HEAD_DIM
GM_CHUNK = 128
IN_PROJ = S5_WIDTH + 2 * GM_WIDTH
N_MEM = 256
XA_HEADS = 4
XA_HEAD_DIM = D_MODEL // XA_HEADS
N_EXPERTS = 32
TOP_K = 4
D_FF = D_MODEL
SWIGLU_LIMIT = 7.0
SWIGLU_ALPHA = 1.702
EPS = 1e-6
EIG_CLIP = -1e-4

N_PROMPT = BATCH * SEQ
N_SAMPLE = DEC_BATCH * DEC_SEQ
N_TOK = N_PROMPT + N_SAMPLE

SUBLANES = 8
LANES = 128
MXU_DIM = 256
VMEM_LIMIT = 48 * 1024 * 1024

ROW_BLOCK = 512
S5_SEG_PROMPT = 64
S5_COL_BLOCK = 512
GM_ROWS_PROMPT = 256
XA_ROWS_PROMPT = 512
XA_SAMPLE_BATCHES = 4
MOE_ROWS = 512
MOE_BLOCKS = -(-N_TOK * TOP_K // MOE_ROWS) + N_EXPERTS
MOE_CAST_ROWS = 128


def _params(*sem):
    return pltpu.CompilerParams(dimension_semantics=sem, vmem_limit_bytes=VMEM_LIMIT)


def _rms(x, g):
    return x * lax.rsqrt(jnp.mean(x * x, axis=-1, keepdims=True) + EPS) * g


def _dot(a, b):
    return jnp.dot(a, b, preferred_element_type=F32)


def _norm_matmul_kernel(x_ref, g_ref, w_ref, o_ref):
    o_ref[...] = _dot(_rms(x_ref[...], g_ref[...]).astype(BF16), w_ref[...])


def _norm_matmul(x, g, w, rows):
    n, d = x.shape
    dout = w.shape[1]
    return pl.pallas_call(
        _norm_matmul_kernel,
        grid=(n // rows,),
        in_specs=[pl.BlockSpec((rows, d), lambda i: (i, 0)),
                  pl.BlockSpec((1, d), lambda i: (0, 0)),
                  pl.BlockSpec((d, dout), lambda i: (0, 0))],
        out_specs=pl.BlockSpec((rows, dout), lambda i: (i, 0)),
        out_shape=jax.ShapeDtypeStruct((n, dout), F32),
        compiler_params=_params("parallel"),
        name="norm_matmul",
    )(x, g.reshape(1, d), w)


def _out_proj_kernel(x_ref, a_ref, b_ref, wa_ref, wb_ref, o_ref):
    o_ref[...] = x_ref[...] + _dot(a_ref[...], wa_ref[...]) + _dot(b_ref[...], wb_ref[...])


def _out_proj(x, a, b, w):
    n, d = x.shape
    wa, wb = w[:S5_WIDTH], w[S5_WIDTH:]
    rows = ROW_BLOCK
    return pl.pallas_call(
        _out_proj_kernel,
        grid=(n // rows,),
        in_specs=[pl.BlockSpec((rows, d), lambda i: (i, 0)),
                  pl.BlockSpec((rows, S5_WIDTH), lambda i: (i, 0)),
                  pl.BlockSpec((rows, GM_WIDTH), lambda i: (i, 0)),
                  pl.BlockSpec((S5_WIDTH, d), lambda i: (0, 0)),
                  pl.BlockSpec((GM_WIDTH, d), lambda i: (0, 0))],
        out_specs=pl.BlockSpec((rows, d), lambda i: (i, 0)),
        out_shape=jax.ShapeDtypeStruct((n, d), F32),
        compiler_params=_params("parallel"),
        name="out_proj",
    )(x, a, b, wa, wb)


def _router_kernel(x_ref, g_ref, wt_ref, b_ref, h_ref, e_ref, gate_ref, cnt_ref):
    h = _rms(x_ref[...], g_ref[...])
    h_ref[...] = h
    logits = lax.dot_general(wt_ref[...], h.astype(BF16), (((1,), (1,)), ((), ())),
                             preferred_element_type=F32) + b_ref[...]
    sub = lax.broadcasted_iota(jnp.int32, logits.shape, 0)
    work = logits
    chosen = jnp.zeros(logits.shape, F32)
    vals, idxs = [], []
    for _ in range(TOP_K):
        m = jnp.max(work, axis=0, keepdims=True)
        idx = jnp.min(jnp.where(work == m, sub, N_EXPERTS), axis=0, keepdims=True)
        sel = sub == idx
        vals.append(m)
        idxs.append(idx)
        work = jnp.where(sel, -jnp.inf, work)
        chosen = jnp.where(sel, 1.0, chosen)
    ex = [jnp.exp(v - vals[0]) for v in vals]
    den = ex[0] + ex[1] + ex[2] + ex[3]
    gate_ref[...] = jnp.concatenate([e / den for e in ex], axis=0)
    e_ref[...] = jnp.concatenate(idxs, axis=0)
    cnt_ref[...] = jnp.broadcast_to(jnp.sum(chosen, axis=1, keepdims=True), (N_EXPERTS, LANES))


def _router(x, g, w, b):
    n, d = x.shape
    rows = ROW_BLOCK
    return pl.pallas_call(
        _router_kernel,
        grid=(n // rows,),
        in_specs=[pl.BlockSpec((rows, d), lambda i: (i, 0)),
                  pl.BlockSpec((1, d), lambda i: (0, 0)),
                  pl.BlockSpec((N_EXPERTS, d), lambda i: (0, 0)),
                  pl.BlockSpec((N_EXPERTS, 1), lambda i: (0, 0))],
        out_specs=[pl.BlockSpec((rows, d), lambda i: (i, 0)),
                   pl.BlockSpec((TOP_K, rows), lambda i: (0, i)),
                   pl.BlockSpec((TOP_K, rows), lambda i: (0, i)),
                   pl.BlockSpec((None, N_EXPERTS, LANES), lambda i: (i, 0, 0))],
        out_shape=[jax.ShapeDtypeStruct((n, d), F32),
                   jax.ShapeDtypeStruct((TOP_K, n), jnp.int32),
                   jax.ShapeDtypeStruct((TOP_K, n), F32),
                   jax.ShapeDtypeStruct((n // rows, N_EXPERTS, LANES), F32)],
        compiler_params=_params("parallel"),
        name="router",
    )(x, g.reshape(1, d), w.T.astype(BF16), b.reshape(N_EXPERTS, 1))


def _dest_kernel(e_ref, base_ref, tri_ref, d_ref):
    e = e_ref[...]
    sub = lax.broadcasted_iota(jnp.int32, (N_EXPERTS, e.shape[1]), 0)
    hot = [sub == e[k:k + 1, :] for k in range(TOP_K)]
    chosen = sum(jnp.where(hk, 1.0, 0.0) for hk in hot)
    pos = base_ref[...] + _dot(chosen.astype(BF16), tri_ref[...])
    d_ref[...] = jnp.concatenate(
        [jnp.sum(jnp.where(hk, pos, 0.0), axis=0, keepdims=True) for hk in hot], axis=0).astype(jnp.int32)


def _dest(e_t, base):
    n = e_t.shape[1]
    rows = ROW_BLOCK
    tri = jnp.triu(jnp.ones((rows, rows), F32), 1).astype(BF16)
    return pl.pallas_call(
        _dest_kernel,
        grid=(n // rows,),
        in_specs=[pl.BlockSpec((TOP_K, rows), lambda i: (0, i)),
                  pl.BlockSpec((None, N_EXPERTS, 1), lambda i: (i, 0, 0)),
                  pl.BlockSpec((rows, rows), lambda i: (0, 0))],
        out_specs=pl.BlockSpec((TOP_K, rows), lambda i: (0, i)),
        out_shape=jax.ShapeDtypeStruct((TOP_K, n), jnp.int32),
        compiler_params=_params("parallel"),
        name="moe_dest",
    )(e_t, base, tri)


def _combine_kernel(*refs, final):
    if final:
        x_ref, g0, g1, g2, g3, gate_ref, fg_ref, o_ref = refs
    else:
        x_ref, g0, g1, g2, g3, gate_ref, o_ref = refs
    acc = x_ref[...]
    for k, g in enumerate((g0, g1, g2, g3)):
        acc = acc + gate_ref[:, k:k + 1] * g[...]
    o_ref[...] = _rms(acc, fg_ref[...]) if final else acc


def _combine(x, picked, gates, final_g=None):
    n, d = x.shape
    rows = ROW_BLOCK
    picked = picked.reshape(TOP_K, n, d)
    final = final_g is not None
    in_specs = [pl.BlockSpec((rows, d), lambda i: (i, 0))]
    in_specs += [pl.BlockSpec((None, rows, d), functools.partial(lambda i, k: (k, i, 0), k=k))
                 for k in range(TOP_K)]
    in_specs.append(pl.BlockSpec((rows, TOP_K), lambda i: (i, 0)))
    args = [x, picked, picked, picked, picked, gates]
    if final:
        in_specs.append(pl.BlockSpec((1, d), lambda i: (0, 0)))
        args.append(final_g.reshape(1, d))
    return pl.pallas_call(
        functools.partial(_combine_kernel, final=final),
        grid=(n // rows,),
        in_specs=in_specs,
        out_specs=pl.BlockSpec((rows, d), lambda i: (i, 0)),
        out_shape=jax.ShapeDtypeStruct((n, d), F32),
        compiler_params=_params("parallel"),
        name="moe_combine",
    )(*args)


def _cmul_add(ar, ai, hr, hi, br, bi):
    return ar * hr - ai * hi + br, ar * hi + ai * hr + bi


def _s5_kernel(*refs, seg, chained):
    if chained:
        (u_ref, wbr_ref, wbi_ref, wcr_ref, wci_ref, ar_ref, ai_ref, pr_ref, pi_ref, d_ref,
         wglu_ref, og_ref, o_ref, er_ref, ei_ref,
         up_ref, hr_ref, hi_ref, hbr_ref, hbi_ref, sr_ref, si_ref, op_ref, cr_ref, ci_ref) = refs
    else:
        (u_ref, wbr_ref, wbi_ref, wcr_ref, wci_ref, ar_ref, ai_ref, pr_ref, pi_ref, d_ref,
         wglu_ref, og_ref, h0r_ref, h0i_ref, o_ref, er_ref, ei_ref,
         up_ref, hr_ref, hi_ref, hbr_ref, hbi_ref, sr_ref, si_ref, op_ref) = refs
    rows = seg * SUBLANES
    half = S5_COLS // 2

    n_lane_blk = S5_WIDTH // LANES
    for j in range(SUBLANES):
        for i0 in range(0, seg, SUBLANES):
            for c in range(n_lane_blk):
                up_ref[c, pl.ds(i0 * SUBLANES + j, SUBLANES, stride=SUBLANES), :] = (
                    u_ref[j * seg + i0:j * seg + i0 + SUBLANES, c * LANES:(c + 1) * LANES])

    up = jnp.concatenate([up_ref[c] for c in range(n_lane_blk)], axis=1)
    ub = up.astype(BF16)
    for k in range(2):
        uk = ub[:, k * MXU_DIM:(k + 1) * MXU_DIM]
        hr_ref[:, k * half:(k + 1) * half] = _dot(uk, wbr_ref[k])
        hi_ref[:, k * half:(k + 1) * half] = _dot(uk, wbi_ref[k])

    for cb in range(S5_COLS // S5_COL_BLOCK):
        cols = slice(cb * S5_COL_BLOCK, (cb + 1) * S5_COL_BLOCK)
        ar = jnp.broadcast_to(ar_ref[:, cols], (SUBLANES, S5_COL_BLOCK))
        ai = jnp.broadcast_to(ai_ref[:, cols], (SUBLANES, S5_COL_BLOCK))

        def step(i, carry, cols=cols, ar=ar, ai=ai):
            r = pl.multiple_of(i * SUBLANES, SUBLANES)
            h_r, h_i = _cmul_add(ar, ai, carry[0], carry[1],
                                 hr_ref[pl.ds(r, SUBLANES), cols], hi_ref[pl.ds(r, SUBLANES), cols])
            hr_ref[pl.ds(r, SUBLANES), cols] = h_r
            hi_ref[pl.ds(r, SUBLANES), cols] = h_i
            return h_r, h_i

        zero = jnp.zeros((SUBLANES, S5_COL_BLOCK), F32)
        lax.fori_loop(0, seg, step, (zero, zero), unroll=2)

    end_r = hr_ref[rows - SUBLANES:rows, :]
    end_i = hi_ref[rows - SUBLANES:rows, :]
    as_r = pr_ref[seg - 1:seg, :]
    as_i = pi_ref[seg - 1:seg, :]
    if chained:
        @pl.when(pl.program_id(1) == 0)
        def _():
            cr_ref[...] = jnp.zeros_like(cr_ref)
            ci_ref[...] = jnp.zeros_like(ci_ref)

        cur_r, cur_i = cr_ref[...], ci_ref[...]
        for j in range(SUBLANES):
            sr_ref[j:j + 1, :] = cur_r
            si_ref[j:j + 1, :] = cur_i
            cur_r, cur_i = _cmul_add(as_r, as_i, cur_r, cur_i, end_r[j:j + 1], end_i[j:j + 1])
        cr_ref[...] = cur_r
        ci_ref[...] = cur_i
    else:
        sr_ref[...] = h0r_ref[...]
        si_ref[...] = h0i_ref[...]
    t_r, t_i = _cmul_add(as_r, as_i, sr_ref[...], si_ref[...], end_r, end_i)
    er_ref[...] = t_r
    ei_ref[...] = t_i

    for cb in range(S5_COLS // S5_COL_BLOCK):
        cols = slice(cb * S5_COL_BLOCK, (cb + 1) * S5_COL_BLOCK)
        s_r = jnp.concatenate([sr_ref[:, cols]] * 2, axis=0)
        s_i = jnp.concatenate([si_ref[:, cols]] * 2, axis=0)

        def fix(i2, _, cols=cols, s_r=s_r, s_i=s_i):
            r = pl.multiple_of(i2 * 2 * SUBLANES, 2 * SUBLANES)
            shape = (SUBLANES, S5_COL_BLOCK)
            p_r = jnp.concatenate(
                [jnp.broadcast_to(pr_ref[pl.ds(2 * i2, 1), cols], shape),
                 jnp.broadcast_to(pr_ref[pl.ds(2 * i2 + 1, 1), cols], shape)], axis=0)
            p_i = jnp.concatenate(
                [jnp.broadcast_to(pi_ref[pl.ds(2 * i2, 1), cols], shape),
                 jnp.broadcast_to(pi_ref[pl.ds(2 * i2 + 1, 1), cols], shape)], axis=0)
            h_r, h_i = _cmul_add(p_r, p_i, s_r, s_i,
                                 hr_ref[pl.ds(r, 2 * SUBLANES), cols],
                                 hi_ref[pl.ds(r, 2 * SUBLANES), cols])
            hbr_ref[pl.ds(r, 2 * SUBLANES), cols] = h_r.astype(BF16)
            hbi_ref[pl.ds(r, 2 * SUBLANES), cols] = h_i.astype(BF16)
            return 0

        lax.fori_loop(0, seg // 2, fix, 0)

    slabs_per_window = MXU_DIM // (MXU_DIM // S5_STATE * S5_GROUP_CH)
    ys = []
    for w in range(S5_WIDTH // MXU_DIM):
        acc = d_ref[:, w * MXU_DIM:(w + 1) * MXU_DIM] * jnp.concatenate(
            [up_ref[c] for c in range(w * MXU_DIM // LANES, (w + 1) * MXU_DIM // LANES)], axis=1)
        for s in range(slabs_per_window):
            j = w * slabs_per_window + s
            acc += _dot(hbr_ref[:, j * MXU_DIM:(j + 1) * MXU_DIM], wcr_ref[j])
            acc += _dot(hbi_ref[:, j * MXU_DIM:(j + 1) * MXU_DIM], wci_ref[j])
        ys.append(jax.nn.gelu(acc).astype(BF16))

    z = _dot(jnp.concatenate(ys, axis=1), wglu_ref[...])
    o = _rms(z[:, :S5_WIDTH] * jax.nn.sigmoid(z[:, S5_WIDTH:]), og_ref[...])
    for c in range(n_lane_blk):
        op_ref[c] = o[:, c * LANES:(c + 1) * LANES]

    for j in range(SUBLANES):
        for i0 in range(0, seg, 2 * SUBLANES):
            for c in range(n_lane_blk):
                o_ref[j * seg + i0:j * seg + i0 + 2 * SUBLANES, c * LANES:(c + 1) * LANES] = (
                    op_ref[c, pl.ds(i0 * SUBLANES + j, 2 * SUBLANES, stride=SUBLANES), :].astype(BF16))


def _s5_weights(lam_re, lam_im, b_re, b_im, c_re, c_im, d, log_step, seg):
    step = jnp.exp(log_step)[:, None]
    lr = jnp.minimum(lam_re, EIG_CLIP)
    li = lam_im
    mag = jnp.exp(lr * step)
    ar = mag * jnp.cos(li * step)
    ai = mag * jnp.sin(li * step)
    den = lr * lr + li * li
    fr = ((ar - 1.0) * lr + ai * li) / den
    fi = (ai * lr - (ar - 1.0) * li) / den
    bbar_r = fr[..., None] * b_re - fi[..., None] * b_im
    bbar_i = fr[..., None] * b_im + fi[..., None] * b_re
    t = jnp.arange(1, seg + 1, dtype=F32)[:, None, None]
    pm = jnp.exp(lr * step * t)
    p_r = (pm * jnp.cos(li * step * t)).reshape(seg, S5_COLS)
    p_i = (pm * jnp.sin(li * step * t)).reshape(seg, S5_COLS)

    gk = MXU_DIM // S5_GROUP_CH

    def in_map(bbar):
        bb = bbar.reshape(S5_GROUPS // gk, gk, S5_STATE, S5_GROUP_CH)
        w = jnp.einsum('kgnh,gG->kghGn', bb, jnp.eye(gk, dtype=F32))
        return w.reshape(S5_GROUPS // gk, MXU_DIM, gk * S5_STATE).astype(BF16)

    gs = MXU_DIM // S5_STATE
    n_slab = S5_COLS // MXU_DIM
    per_win = MXU_DIM // (gs * S5_GROUP_CH)

    def out_map(c):
        cc = c.reshape(n_slab // per_win, per_win, gs, S5_GROUP_CH, S5_STATE)
        w = jnp.einsum('wsghn,sS,gG->wsgnSGh', cc, jnp.eye(per_win, dtype=F32), jnp.eye(gs, dtype=F32))
        return w.reshape(n_slab, MXU_DIM, MXU_DIM).astype(BF16)

    return dict(wbr=in_map(bbar_r), wbi=in_map(bbar_i), wcr=out_map(c_re), wci=out_map(-c_im),
                ar=ar.reshape(1, S5_COLS), ai=ai.reshape(1, S5_COLS), pr=p_r, pi=p_i,
                d=d.reshape(1, S5_WIDTH))


def _s5_call(proj, sw, w_glu, og, seg, chained, h0=None, prev=None):
    rows = seg * SUBLANES
    const2 = lambda *_: (0, 0)
    const3 = lambda *_: (0, 0, 0)
    if chained:
        n_chunk = SEQ // rows
        grid = (BATCH, n_chunk)
        row_map = lambda b, c: (b * n_chunk + c, 0)
        end_map = lambda b, c: (b, 0, 0)
        n_end = BATCH
        sem = ("parallel", "arbitrary")
    else:
        grid = (DEC_BATCH // SUBLANES,)
        row_map = lambda k: (N_PROMPT // rows + k, 0)
        end_map = lambda k: (k, 0, 0)
        n_end = DEC_BATCH // SUBLANES
        sem = ("parallel",)
    half = S5_COLS // 2
    in_specs = [pl.BlockSpec((rows, S5_WIDTH), row_map),
                pl.BlockSpec((2, MXU_DIM, half), const3),
                pl.BlockSpec((2, MXU_DIM, half), const3),
                pl.BlockSpec((S5_COLS // MXU_DIM, MXU_DIM, MXU_DIM), const3),
                pl.BlockSpec((S5_COLS // MXU_DIM, MXU_DIM, MXU_DIM), const3),
                pl.BlockSpec((1, S5_COLS), const2),
                pl.BlockSpec((1, S5_COLS), const2),
                pl.BlockSpec((seg, S5_COLS), const2),
                pl.BlockSpec((seg, S5_COLS), const2),
                pl.BlockSpec((1, S5_WIDTH), const2),
                pl.BlockSpec((S5_WIDTH, 2 * S5_WIDTH), const2),
                pl.BlockSpec((1, S5_WIDTH), const2)]
    args = [proj, sw['wbr'], sw['wbi'], sw['wcr'], sw['wci'], sw['ar'], sw['ai'],
            sw['pr'][:seg], sw['pi'][:seg], sw['d'], w_glu, og.reshape(1, S5_WIDTH)]
    scratch = [pltpu.VMEM((S5_WIDTH // LANES, rows, LANES), F32),
               pltpu.VMEM((rows, S5_COLS), F32), pltpu.VMEM((rows, S5_COLS), F32),
               pltpu.VMEM((rows, S5_COLS), BF16), pltpu.VMEM((rows, S5_COLS), BF16),
               pltpu.VMEM((SUBLANES, S5_COLS), F32), pltpu.VMEM((SUBLANES, S5_COLS), F32),
               pltpu.VMEM((S5_WIDTH // LANES, rows, LANES), F32)]
    aliases = {}
    if chained:
        scratch += [pltpu.VMEM((1, S5_COLS), F32), pltpu.VMEM((1, S5_COLS), F32)]
    else:
        in_specs += [pl.BlockSpec((None, SUBLANES, S5_COLS), end_map),
                     pl.BlockSpec((None, SUBLANES, S5_COLS), end_map)]
        args += [h0[0], h0[1]]
    if prev is not None:
        in_specs.append(pl.BlockSpec(memory_space=pl.ANY))
        args.append(prev)
        aliases = {len(args) - 1: 0}
    kern = functools.partial(_s5_kernel, seg=seg, chained=chained)
    if prev is not None:
        kern = _drop_last_input(kern, len(args))
    return pl.pallas_call(
        kern,
        grid=grid,
        in_specs=in_specs,
        out_specs=[pl.BlockSpec((rows, S5_WIDTH), row_map),
                   pl.BlockSpec((None, SUBLANES, S5_COLS), end_map),
                   pl.BlockSpec((None, SUBLANES, S5_COLS), end_map)],
        out_shape=[jax.ShapeDtypeStruct((N_TOK, S5_WIDTH), BF16),
                   jax.ShapeDtypeStruct((n_end, SUBLANES, S5_COLS), F32),
                   jax.ShapeDtypeStruct((n_end, SUBLANES, S5_COLS), F32)],
        scratch_shapes=scratch,
        input_output_aliases=aliases,
        compiler_params=_params(*sem),
        name="s5_chained" if chained else "s5_streams",
    )(*args)


def _drop_last_input(kern, n_in):
    def wrapped(*refs):
        return kern(*refs[:n_in - 1], *refs[n_in:])
    return wrapped


def _gmlp_kernel(*refs, chunk, emit_vn):
    if emit_vn:
        u_ref, v_ref, lng_ref, lnb_ref, ws_ref, bst_ref, og_ref, o_ref, vn_ref, acc_ref = refs
    else:
        u_ref, v_ref, lng_ref, lnb_ref, ws_ref, bst_ref, og_ref, o_ref, acc_ref = refs
    rows = u_ref.shape[0]
    u = jax.nn.gelu(u_ref[...])
    v = jax.nn.gelu(v_ref[...])
    tri = (lax.broadcasted_iota(jnp.int32, (chunk, chunk), 0)
           >= lax.broadcasted_iota(jnp.int32, (chunk, chunk), 1))
    for h in range(GM_HEADS):
        cols = slice(h * GM_HEAD_DIM, (h + 1) * GM_HEAD_DIM)
        vh = v[:, cols]
        mu = jnp.mean(vh, axis=-1, keepdims=True)
        cen = vh - mu
        var = jnp.mean(cen * cen, axis=-1, keepdims=True)
        vn = cen * lax.rsqrt(var + EPS) * lng_ref[:, cols] + lnb_ref[:, cols]
        if emit_vn:
            vn_ref[:, cols] = vn
        w = jnp.where(tri, ws_ref[h], 0.0).astype(BF16)
        vb = vn.astype(BF16)
        bias = bst_ref[:, h:h + 1]
        for c in range(rows // chunk):
            rs = slice(c * chunk, (c + 1) * chunk)
            acc_ref[rs, cols] = u[rs, cols] * (_dot(w, vb[rs]) + bias)
    o_ref[...] = _rms(acc_ref[...], og_ref[...]).astype(BF16)


def _gmlp_call(proj, ln_g, ln_b, w_s, b_s, og, rows, chunk, first_block, n_blocks, emit_vn, prev=None):
    const2 = lambda i: (0, 0)
    in_specs = [pl.BlockSpec((rows, GM_WIDTH), lambda i: (first_block + i, 1)),
                pl.BlockSpec((rows, GM_WIDTH), lambda i: (first_block + i, 2)),
                pl.BlockSpec((1, GM_WIDTH), const2),
                pl.BlockSpec((1, GM_WIDTH), const2),
                pl.BlockSpec((GM_HEADS, chunk, chunk), lambda i: (0, 0, 0)),
                pl.BlockSpec((chunk, GM_HEADS), const2),
                pl.BlockSpec((1, GM_WIDTH), const2)]
    args = [proj, proj, ln_g.reshape(1, GM_WIDTH), ln_b.reshape(1, GM_WIDTH),
            w_s[:, :chunk, :chunk], b_s[:, :chunk].T, og.reshape(1, GM_WIDTH)]
    out_specs = [pl.BlockSpec((rows, GM_WIDTH), lambda i: (first_block + i, 0))]
    out_shape = [jax.ShapeDtypeStruct((N_TOK, GM_WIDTH), BF16)]
    if emit_vn:
        out_specs.append(pl.BlockSpec((rows, GM_WIDTH), lambda i: (i, 0)))
        out_shape.append(jax.ShapeDtypeStruct((n_blocks * rows, GM_WIDTH), F32))
    kern = functools.partial(_gmlp_kernel, chunk=chunk, emit_vn=emit_vn)
    aliases = {}
    if prev is not None:
        in_specs.append(pl.BlockSpec(memory_space=pl.ANY))
        args.append(prev)
        aliases = {len(args) - 1: 0}
        kern = _drop_last_input(kern, len(args))
    return pl.pallas_call(
        kern,
        grid=(n_blocks,),
        in_specs=in_specs,
        out_specs=out_specs,
        out_shape=out_shape,
        scratch_shapes=[pltpu.VMEM((rows, GM_WIDTH), F32)],
        input_output_aliases=aliases,
        compiler_params=_params("parallel"),
        name="gmlp_vn" if emit_vn else "gmlp",
    )(*args)


def _xattn_kernel(x_ref, g_ref, wq_ref, k_ref, v_ref, wo_ref, o_ref, att_ref, *, n_batch):
    rows = x_ref.shape[0]
    tb = rows // n_batch
    x = x_ref[...]
    q = _dot(_rms(x, g_ref[...]).astype(BF16), wq_ref[...]).astype(BF16)
    scale = XA_HEAD_DIM ** -0.5
    for b in range(n_batch):
        rs = slice(b * tb, (b + 1) * tb)
        for h in range(XA_HEADS):
            cols = slice(h * XA_HEAD_DIM, (h + 1) * XA_HEAD_DIM)
            s = lax.dot_general(q[rs, cols], k_ref[b, :, cols], (((1,), (1,)), ((), ())),
                                preferred_element_type=F32) * scale
            e = jnp.exp(s - jnp.max(s, axis=-1, keepdims=True))
            p = e / jnp.sum(e, axis=-1, keepdims=True)
            att_ref[rs, cols] = _dot(p.astype(BF16), v_ref[b, :, cols]).astype(BF16)
    o_ref[...] = x + _dot(att_ref[...], wo_ref[...])


def _xattn_call(x, g, wq, k, v, wo, rows, n_batch, first_block, n_blocks, blocks_per_kv, prev=None):
    const2 = lambda i: (0, 0)
    in_specs = [pl.BlockSpec((rows, D_MODEL), lambda i: (first_block + i, 0)),
                pl.BlockSpec((1, D_MODEL), const2),
                pl.BlockSpec((D_MODEL, D_MODEL), const2),
                pl.BlockSpec((n_batch, N_MEM, D_MODEL), lambda i: (i // blocks_per_kv, 0, 0)),
                pl.BlockSpec((n_batch, N_MEM, D_MODEL), lambda i: (i // blocks_per_kv, 0, 0)),
                pl.BlockSpec((D_MODEL, D_MODEL), const2)]
    args = [x, g.reshape(1, D_MODEL), wq, k, v, wo]
    kern = functools.partial(_xattn_kernel, n_batch=n_batch)
    aliases = {}
    if prev is not None:
        in_specs.append(pl.BlockSpec(memory_space=pl.ANY))
        args.append(prev)
        aliases = {len(args) - 1: 0}
        kern = _drop_last_input(kern, len(args))
    return pl.pallas_call(
        kern,
        grid=(n_blocks,),
        in_specs=in_specs,
        out_specs=pl.BlockSpec((rows, D_MODEL), lambda i: (first_block + i, 0)),
        out_shape=jax.ShapeDtypeStruct((N_TOK, D_MODEL), F32),
        scratch_shapes=[pltpu.VMEM((rows, D_MODEL), BF16)],
        input_output_aliases=aliases,
        compiler_params=_params("parallel"),
        name="xattn",
    )(*args)


def _moe_kernel(be_ref, nu_ref, first_ref, slot_ref, nxt_ref,
                x_ref, wg_hbm, wu_hbm, wd_hbm, bg_ref, bu_ref, bd_ref, o_ref,
                wbuf_ref, wgb_ref, wub_ref, wdb_ref, sem, *, layer):
    i = pl.program_id(0)
    active = i < nu_ref[0]

    def copies(e, s):
        return [pltpu.make_async_copy(w.at[layer, e], wbuf_ref.at[s, j], sem.at[s, j])
                for j, w in enumerate((wg_hbm, wu_hbm, wd_hbm))]

    @pl.when(jnp.logical_and(active, first_ref[i] == 1))
    def _():
        s = slot_ref[i]

        @pl.when(i == 0)
        def _():
            for c in copies(be_ref[0], s):
                c.start()

        @pl.when(nxt_ref[i] >= 0)
        def _():
            for c in copies(nxt_ref[i], 1 - s):
                c.start()

        for c in copies(be_ref[i], s):
            c.wait()

        def cast(r, _):
            rs = pl.ds(pl.multiple_of(r * MOE_CAST_ROWS, MOE_CAST_ROWS), MOE_CAST_ROWS)
            wgb_ref[rs, :] = wbuf_ref[s, 0, rs, :].astype(BF16)
            wub_ref[rs, :] = wbuf_ref[s, 1, rs, :].astype(BF16)
            wdb_ref[rs, :] = wbuf_ref[s, 2, rs, :].astype(BF16)
            return 0
        lax.fori_loop(0, D_MODEL // MOE_CAST_ROWS, cast, 0)

    @pl.when(active)
    def _():
        x = x_ref[...].astype(BF16)
        g = jnp.minimum(_dot(x, wgb_ref[...]) + bg_ref[...], SWIGLU_LIMIT)
        u = jnp.clip(_dot(x, wub_ref[...]) + bu_ref[...], -SWIGLU_LIMIT, SWIGLU_LIMIT)
        a = g * jax.nn.sigmoid(SWIGLU_ALPHA * g) * (u + 1.0)
        o_ref[...] = _dot(a.astype(BF16), wdb_ref[...]) + bd_ref[...]


def _moe_call(layer, xs, sched, w_gate, b_gate, w_up, b_up, w_down, b_down):
    def row_map(i, be, nu, *_):
        return (jnp.minimum(i, nu[0] - 1), 0)

    def b_map(i, be, *_):
        return (layer, be[i], 0, 0)

    b_spec = pl.BlockSpec((None, None, 1, D_FF), b_map)
    w_spec = pl.BlockSpec(memory_space=pl.ANY)
    grid_spec = pltpu.PrefetchScalarGridSpec(
        num_scalar_prefetch=5,
        grid=(MOE_BLOCKS,),
        in_specs=[pl.BlockSpec((MOE_ROWS, D_MODEL), row_map),
                  w_spec, w_spec, w_spec, b_spec, b_spec, b_spec],
        out_specs=pl.BlockSpec((MOE_ROWS, D_MODEL), row_map),
        scratch_shapes=[pltpu.VMEM((2, 3, D_MODEL, D_FF), F32)]
        + [pltpu.VMEM((D_MODEL, D_FF), BF16)] * 3
        + [pltpu.SemaphoreType.DMA((2, 3))],
    )
    shape4 = (DEPTH, N_EXPERTS, 1, D_FF)
    return pl.pallas_call(
        functools.partial(_moe_kernel, layer=layer),
        grid_spec=grid_spec,
        out_shape=jax.ShapeDtypeStruct((MOE_BLOCKS * MOE_ROWS, D_MODEL), F32),
        compiler_params=pltpu.CompilerParams(dimension_semantics=("arbitrary",),
                                             vmem_limit_bytes=56 * 1024 * 1024),
        name="moe_experts",
    )(*sched, xs, w_gate, w_up, w_down, b_gate.reshape(shape4), b_up.reshape(shape4),
      b_down.reshape(shape4))


def _route(e_t, cnt):
    blk_cnt = cnt[:, :, 0].astype(jnp.int32)
    counts = jnp.sum(blk_cnt, axis=0)
    padded = ((counts + MOE_ROWS - 1) // MOE_ROWS) * MOE_ROWS
    pend = jnp.cumsum(padded)
    pstart = pend - padded
    base = pstart[None, :] + jnp.cumsum(blk_cnt, axis=0) - blk_cnt
    dest_t = _dest(e_t, base.astype(F32)[:, :, None])
    n_used = (pend[-1] // MOE_ROWS).astype(jnp.int32)
    blk = jnp.minimum(jnp.arange(MOE_BLOCKS, dtype=jnp.int32), n_used - 1) * MOE_ROWS
    blk_exp = jnp.minimum(jnp.sum((pend[None, :] <= blk[:, None]).astype(jnp.int32), axis=1), N_EXPERTS - 1)
    ids = jnp.arange(MOE_BLOCKS, dtype=jnp.int32)
    first = jnp.logical_and(ids < n_used, jnp.logical_or(ids == 0, blk_exp != jnp.roll(blk_exp, 1)))
    first = first.astype(jnp.int32)
    slot = (jnp.cumsum(first) - 1) % 2
    eid = jnp.arange(N_EXPERTS, dtype=jnp.int32)
    later = jnp.logical_and(eid[None, :] > eid[:, None], (counts > 0)[None, :])
    nxt_e = jnp.min(jnp.where(later, eid[None, :], N_EXPERTS), axis=1)
    nxt = jnp.where(nxt_e < N_EXPERTS, nxt_e, -1)[blk_exp].astype(jnp.int32)
    tok = jnp.tile(jnp.arange(N_TOK, dtype=jnp.int32), TOP_K)
    row_tok = jnp.zeros((MOE_BLOCKS * MOE_ROWS,), jnp.int32).at[dest_t.reshape(-1)].set(
        tok, unique_indices=True)
    return dest_t, row_tok, (blk_exp, n_used.reshape(1), first, slot.astype(jnp.int32), nxt)


def kernel(x_prompt, x_sample, mem_prompt, state_s5_re, state_s5_im, cache_mem_k, cache_mem_v, norm_mix_g, w_in, s5_lambda_re, s5_lambda_im, s5_b_re, s5_b_im, s5_c_re, s5_c_im, s5_d, s5_log_step, s5_w_glu, gm_ln_g, gm_ln_b, gm_w_s, gm_b_s, out_norm_g, w_out, xa_norm_g, mem_norm_g, w_xq, w_xk, w_xv, w_xo, ffn_norm_g, router_w, router_b, e_w_gate, e_b_gate, e_w_up, e_b_up, e_w_down, e_b_down, final_norm_g):
    x = jnp.concatenate([x_prompt.reshape(N_PROMPT, D_MODEL), x_sample.reshape(N_SAMPLE, D_MODEL)], axis=0)
    mem = mem_prompt.reshape(BATCH * N_MEM, D_MODEL)
    outs = {k: [] for k in ('s5r_p', 's5i_p', 'mk_p', 'mv_p', 's5r_s', 's5i_s', 'gmv_s')}
    for l in range(DEPTH):
        proj = _norm_matmul(x, norm_mix_g[l], w_in[l].astype(BF16), ROW_BLOCK)
        sw = _s5_weights(s5_lambda_re[l], s5_lambda_im[l], s5_b_re[l], s5_b_im[l], s5_c_re[l], s5_c_im[l],
                         s5_d[l], s5_log_step[l], S5_SEG_PROMPT)
        w_glu = s5_w_glu[l].astype(BF16)
        og = out_norm_g[l]
        ms5, er_p, ei_p = _s5_call(proj, sw, w_glu, og[:S5_WIDTH], S5_SEG_PROMPT, True)
        h0 = (state_s5_re[l].reshape(DEC_BATCH // SUBLANES, SUBLANES, S5_COLS),
              state_s5_im[l].reshape(DEC_BATCH // SUBLANES, SUBLANES, S5_COLS))
        ms5, er_s, ei_s = _s5_call(proj, sw, w_glu, og[:S5_WIDTH], DEC_SEQ, False, h0=h0, prev=ms5)
        outs['s5r_p'].append(er_p[:, SUBLANES - 1].reshape(BATCH, S5_GROUPS, S5_STATE))
        outs['s5i_p'].append(ei_p[:, SUBLANES - 1].reshape(BATCH, S5_GROUPS, S5_STATE))
        outs['s5r_s'].append(er_s.reshape(DEC_BATCH, S5_GROUPS, S5_STATE))
        outs['s5i_s'].append(ei_s.reshape(DEC_BATCH, S5_GROUPS, S5_STATE))

        (mgm,) = _gmlp_call(proj, gm_ln_g[l], gm_ln_b[l], gm_w_s[l], gm_b_s[l], og[S5_WIDTH:],
                            GM_ROWS_PROMPT, GM_CHUNK, 0, N_PROMPT // GM_ROWS_PROMPT, False)
        mgm, vn = _gmlp_call(proj, gm_ln_g[l], gm_ln_b[l], gm_w_s[l], gm_b_s[l], og[S5_WIDTH:],
                             N_SAMPLE, DEC_SEQ, N_PROMPT // N_SAMPLE, 1, True, prev=mgm)
        outs['gmv_s'].append(vn.reshape(DEC_BATCH, DEC_SEQ, GM_HEADS, GM_HEAD_DIM))
        x = _out_proj(x, ms5, mgm, w_out[l].astype(BF16))

        mk = _norm_matmul(mem, mem_norm_g[l], w_xk[l].astype(BF16), ROW_BLOCK)
        mv = _norm_matmul(mem, mem_norm_g[l], w_xv[l].astype(BF16), ROW_BLOCK)
        outs['mk_p'].append(mk.reshape(BATCH, N_MEM, XA_HEADS, XA_HEAD_DIM))
        outs['mv_p'].append(mv.reshape(BATCH, N_MEM, XA_HEADS, XA_HEAD_DIM))
        wq = w_xq[l].astype(BF16)
        wo = w_xo[l].astype(BF16)
        xa = _xattn_call(x, xa_norm_g[l], wq, mk.reshape(BATCH, N_MEM, D_MODEL).astype(BF16),
                         mv.reshape(BATCH, N_MEM, D_MODEL).astype(BF16), wo,
                         XA_ROWS_PROMPT, 1, 0, N_PROMPT // XA_ROWS_PROMPT, SEQ // XA_ROWS_PROMPT)
        rows_s = XA_SAMPLE_BATCHES * DEC_SEQ
        x = _xattn_call(x, xa_norm_g[l], wq,
                        cache_mem_k[l].reshape(DEC_BATCH, N_MEM, D_MODEL).astype(BF16),
                        cache_mem_v[l].reshape(DEC_BATCH, N_MEM, D_MODEL).astype(BF16), wo,
                        rows_s, XA_SAMPLE_BATCHES, N_PROMPT // rows_s, N_SAMPLE // rows_s, 1, prev=xa)

        h, e_t, gates_t, cnt = _router(x, ffn_norm_g[l], router_w[l], router_b[l])
        dest_t, row_tok, sched = _route(e_t, cnt)
        ys = _moe_call(l, h[row_tok], sched, e_w_gate, e_b_gate, e_w_up, e_b_up, e_w_down, e_b_down)
        x = _combine(x, ys[dest_t.reshape(-1)], gates_t.T, final_norm_g if l == DEPTH - 1 else None)

    y = x
    y_prompt = y[:N_PROMPT].reshape(BATCH, SEQ, D_MODEL)
    y_sample = y[N_PROMPT:].reshape(DEC_BATCH, DEC_SEQ, D_MODEL)
    st = jnp.stack
    return (y_prompt, y_sample, st(outs['s5r_p']), st(outs['s5i_p']), st(outs['mk_p']), st(outs['mv_p']),
            st(outs['s5r_s']), st(outs['s5i_s']), st(outs['gmv_s']))
```

```python
import functools

import jax
import jax.numpy as jnp
from jax import lax
from jax.experimental import pallas as pl
from jax.experimental.pallas import tpu as pltpu

F32 = jnp.float32
BF16 = jnp.bfloat16

D_MODEL = 1024
BATCH = 4
SEQ = 4096
DEPTH = 2
DEC_BATCH = 16
DEC_SEQ = 32
S5_GROUPS = 32
S5_GROUP_CH = 16
S5_STATE = 64
S5_WIDTH = S5_GROUPS * S5_GROUP_CH
S5_COLS = S5_GROUPS * S5_STATE
GM_HEADS = 4
GM_HEAD_DIM = 128
GM_WIDTH = GM_HEADS * GM_HEAD_DIM
GM_CHUNK = 128
IN_PROJ = S5_WIDTH + 2 * GM_WIDTH
N_MEM = 256
XA_HEADS = 4
XA_HEAD_DIM = D_MODEL // XA_HEADS
N_EXPERTS = 32
TOP_K = 4
D_FF = D_MODEL
SWIGLU_LIMIT = 7.0
SWIGLU_ALPHA = 1.702
EPS = 1e-6
EIG_CLIP = -1e-4

N_PROMPT = BATCH * SEQ
N_SAMPLE = DEC_BATCH * DEC_SEQ
N_TOK = N_PROMPT + N_SAMPLE

SUBLANES = 8
LANES = 128
MXU_DIM = 256
VMEM_LIMIT = 48 * 1024 * 1024

ROW_BLOCK = 512
S5_SEG_PROMPT = 64
S5_COL_BLOCK = 512
GM_ROWS_PROMPT = 256
XA_ROWS_PROMPT = 512
XA_SAMPLE_BATCHES = 4
MOE_ROWS = 512
MOE_BLOCKS = -(-N_TOK * TOP_K // MOE_ROWS) + N_EXPERTS
MOE_CAST_ROWS = 128


def _params(*sem):
    return pltpu.CompilerParams(dimension_semantics=sem, vmem_limit_bytes=VMEM_LIMIT)


def _rms(x, g):
    return x * lax.rsqrt(jnp.mean(x * x, axis=-1, keepdims=True) + EPS) * g


def _dot(a, b):
    return jnp.dot(a, b, preferred_element_type=F32)


PROMPT_BLOCKS = N_PROMPT // ROW_BLOCK
SAMPLE_BLOCKS = N_SAMPLE // ROW_BLOCK


def _token_sources(xs):
    if len(xs) == 1:
        return [pl.BlockSpec((ROW_BLOCK, D_MODEL), lambda i: (i, 0))]
    return [pl.BlockSpec((ROW_BLOCK, D_MODEL), lambda i: (jnp.minimum(i, PROMPT_BLOCKS - 1), 0)),
            pl.BlockSpec((ROW_BLOCK, D_MODEL), lambda i: (jnp.maximum(i - PROMPT_BLOCKS, 0), 0))]


def _for_group(refs, body):
    if len(refs) == 1:
        body(refs[0])
        return
    i = pl.program_id(0)
    pl.when(i < PROMPT_BLOCKS)(lambda: body(refs[0]))
    pl.when(i >= PROMPT_BLOCKS)(lambda: body(refs[1]))


def _norm_matmul_kernel(*refs, n_src):
    g_ref, w_ref, o_ref = refs[n_src:]

    def body(x_ref):
        o_ref[...] = _dot(_rms(x_ref[...], g_ref[...]).astype(BF16), w_ref[...])
    _for_group(refs[:n_src], body)


def _norm_matmul(xs, g, w):
    d = D_MODEL
    n = sum(x.shape[0] for x in xs)
    dout = w.shape[1]
    return pl.pallas_call(
        functools.partial(_norm_matmul_kernel, n_src=len(xs)),
        grid=(n // ROW_BLOCK,),
        in_specs=_token_sources(xs) + [pl.BlockSpec((1, d), lambda i: (0, 0)),
                                       pl.BlockSpec((d, dout), lambda i: (0, 0))],
        out_specs=pl.BlockSpec((ROW_BLOCK, dout), lambda i: (i, 0)),
        out_shape=jax.ShapeDtypeStruct((n, dout), F32),
        compiler_params=_params("parallel"),
        name="norm_matmul",
    )(*xs, g.reshape(1, d), w)


def _out_proj_kernel(*refs, n_src):
    a_ref, b_ref, wa_ref, wb_ref, o_ref = refs[n_src:]

    def body(x_ref):
        o_ref[...] = x_ref[...] + _dot(a_ref[...], wa_ref[...]) + _dot(b_ref[...], wb_ref[...])
    _for_group(refs[:n_src], body)


def _out_proj(xs, a, b, w):
    d = D_MODEL
    n = a.shape[0]
    wa, wb = w[:S5_WIDTH], w[S5_WIDTH:]
    rows = ROW_BLOCK
    return pl.pallas_call(
        functools.partial(_out_proj_kernel, n_src=len(xs)),
        grid=(n // rows,),
        in_specs=_token_sources(xs) + [pl.BlockSpec((rows, S5_WIDTH), lambda i: (i, 0)),
                                       pl.BlockSpec((rows, GM_WIDTH), lambda i: (i, 0)),
                                       pl.BlockSpec((S5_WIDTH, d), lambda i: (0, 0)),
                                       pl.BlockSpec((GM_WIDTH, d), lambda i: (0, 0))],
        out_specs=pl.BlockSpec((rows, d), lambda i: (i, 0)),
        out_shape=jax.ShapeDtypeStruct((n, d), F32),
        compiler_params=_params("parallel"),
        name="out_proj",
    )(*xs, a, b, wa, wb)


def _router_kernel(x_ref, g_ref, wt_ref, b_ref, h_ref, e_ref, gate_ref, cnt_ref):
    h = _rms(x_ref[...], g_ref[...])
    h_ref[...] = h
    logits = lax.dot_general(wt_ref[...], h.astype(BF16), (((1,), (1,)), ((), ())),
                             preferred_element_type=F32) + b_ref[...]
    sub = lax.broadcasted_iota(jnp.int32, logits.shape, 0)
    work = logits
    chosen = jnp.zeros(logits.shape, F32)
    vals, idxs = [], []
    for _ in range(TOP_K):
        m = jnp.max(work, axis=0, keepdims=True)
        idx = jnp.min(jnp.where(work == m, sub, N_EXPERTS), axis=0, keepdims=True)
        sel = sub == idx
        vals.append(m)
        idxs.append(idx)
        work = jnp.where(sel, -jnp.inf, work)
        chosen = jnp.where(sel, 1.0, chosen)
    ex = [jnp.exp(v - vals[0]) for v in vals]
    den = ex[0] + ex[1] + ex[2] + ex[3]
    gate_ref[...] = jnp.concatenate([e / den for e in ex], axis=0)
    e_ref[...] = jnp.concatenate(idxs, axis=0)
    cnt_ref[...] = jnp.broadcast_to(jnp.sum(chosen, axis=1, keepdims=True), (N_EXPERTS, LANES))


def _router(x, g, w, b):
    n, d = x.shape
    rows = ROW_BLOCK
    return pl.pallas_call(
        _router_kernel,
        grid=(n // rows,),
        in_specs=[pl.BlockSpec((rows, d), lambda i: (i, 0)),
                  pl.BlockSpec((1, d), lambda i: (0, 0)),
                  pl.BlockSpec((N_EXPERTS, d), lambda i: (0, 0)),
                  pl.BlockSpec((N_EXPERTS, 1), lambda i: (0, 0))],
        out_specs=[pl.BlockSpec((rows, d), lambda i: (i, 0)),
                   pl.BlockSpec((TOP_K, rows), lambda i: (0, i)),
                   pl.BlockSpec((TOP_K, rows), lambda i: (0, i)),
                   pl.BlockSpec((None, N_EXPERTS, LANES), lambda i: (i, 0, 0))],
        out_shape=[jax.ShapeDtypeStruct((n, d), F32),
                   jax.ShapeDtypeStruct((TOP_K, n), jnp.int32),
                   jax.ShapeDtypeStruct((TOP_K, n), F32),
                   jax.ShapeDtypeStruct((n // rows, N_EXPERTS, LANES), F32)],
        compiler_params=_params("parallel"),
        name="router",
    )(x, g.reshape(1, d), w.T.astype(BF16), b.reshape(N_EXPERTS, 1))


def _dest_kernel(e_ref, base_ref, tri_ref, d_ref):
    e = e_ref[...]
    sub = lax.broadcasted_iota(jnp.int32, (N_EXPERTS, e.shape[1]), 0)
    hot = [sub == e[k:k + 1, :] for k in range(TOP_K)]
    chosen = sum(jnp.where(hk, 1.0, 0.0) for hk in hot)
    pos = base_ref[...] + _dot(chosen.astype(BF16), tri_ref[...])
    d_ref[...] = jnp.concatenate(
        [jnp.sum(jnp.where(hk, pos, 0.0), axis=0, keepdims=True) for hk in hot], axis=0).astype(jnp.int32)


def _dest(e_t, base):
    n = e_t.shape[1]
    rows = ROW_BLOCK
    tri = jnp.triu(jnp.ones((rows, rows), F32), 1).astype(BF16)
    return pl.pallas_call(
        _dest_kernel,
        grid=(n // rows,),
        in_specs=[pl.BlockSpec((TOP_K, rows), lambda i: (0, i)),
                  pl.BlockSpec((None, N_EXPERTS, 1), lambda i: (i, 0, 0)),
                  pl.BlockSpec((rows, rows), lambda i: (0, 0))],
        out_specs=pl.BlockSpec((TOP_K, rows), lambda i: (0, i)),
        out_shape=jax.ShapeDtypeStruct((TOP_K, n), jnp.int32),
        compiler_params=_params("parallel"),
        name="moe_dest",
    )(e_t, base, tri)


def _combine_kernel(*refs, final):
    if final:
        x_ref, g0, g1, g2, g3, gate_ref, fg_ref, op_ref, os_ref = refs
    else:
        x_ref, g0, g1, g2, g3, gate_ref, o_ref = refs
    acc = x_ref[...]
    for k, g in enumerate((g0, g1, g2, g3)):
        acc = acc + gate_ref[:, k:k + 1] * g[...]
    if final:
        def store(o_ref):
            o_ref[...] = _rms(acc, fg_ref[...])
        _for_group((op_ref, os_ref), store)
    else:
        o_ref[...] = acc


def _combine(x, picked, gates, final_g=None):
    n, d = x.shape
    rows = ROW_BLOCK
    picked = picked.reshape(TOP_K, n, d)
    final = final_g is not None
    in_specs = [pl.BlockSpec((rows, d), lambda i: (i, 0))]
    in_specs += [pl.BlockSpec((None, rows, d), functools.partial(lambda i, k: (k, i, 0), k=k))
                 for k in range(TOP_K)]
    in_specs.append(pl.BlockSpec((rows, TOP_K), lambda i: (i, 0)))
    args = [x, picked, picked, picked, picked, gates]
    out_specs = pl.BlockSpec((rows, d), lambda i: (i, 0))
    out_shape = jax.ShapeDtypeStruct((n, d), F32)
    if final:
        in_specs.append(pl.BlockSpec((1, d), lambda i: (0, 0)))
        args.append(final_g.reshape(1, d))
        out_specs = _token_sources((None, None))
        out_shape = [jax.ShapeDtypeStruct((N_PROMPT, d), F32), jax.ShapeDtypeStruct((N_SAMPLE, d), F32)]
    return pl.pallas_call(
        functools.partial(_combine_kernel, final=final),
        grid=(n // rows,),
        in_specs=in_specs,
        out_specs=out_specs,
        out_shape=out_shape,
        compiler_params=_params("arbitrary" if final else "parallel"),
        name="moe_combine",
    )(*args)


def _cmul_add(ar, ai, hr, hi, br, bi):
    return ar * hr - ai * hi + br, ar * hi + ai * hr + bi


def _s5_kernel(*refs, seg, chained):
    if chained:
        (u_ref, wbr_ref, wbi_ref, wcr_ref, wci_ref, ar_ref, ai_ref, pr_ref, pi_ref, d_ref,
         wglu_ref, og_ref, o_ref, er_ref, ei_ref,
         up_ref, hr_ref, hi_ref, hbr_ref, hbi_ref, sr_ref, si_ref, op_ref, cr_ref, ci_ref) = refs
    else:
        (u_ref, wbr_ref, wbi_ref, wcr_ref, wci_ref, ar_ref, ai_ref, pr_ref, pi_ref, d_ref,
         wglu_ref, og_ref, h0r_ref, h0i_ref, o_ref, er_ref, ei_ref,
         up_ref, hr_ref, hi_ref, hbr_ref, hbi_ref, sr_ref, si_ref, op_ref) = refs
    rows = seg * SUBLANES
    half = S5_COLS // 2

    n_lane_blk = S5_WIDTH // LANES
    for j in range(SUBLANES):
        for i0 in range(0, seg, SUBLANES):
            for c in range(n_lane_blk):
                up_ref[c, pl.ds(i0 * SUBLANES + j, SUBLANES, stride=SUBLANES), :] = (
                    u_ref[j * seg + i0:j * seg + i0 + SUBLANES, c * LANES:(c + 1) * LANES])

    up = jnp.concatenate([up_ref[c] for c in range(n_lane_blk)], axis=1)
    ub = up.astype(BF16)
    for k in range(2):
        uk = ub[:, k * MXU_DIM:(k + 1) * MXU_DIM]
        hr_ref[:, k * half:(k + 1) * half] = _dot(uk, wbr_ref[k])
        hi_ref[:, k * half:(k + 1) * half] = _dot(uk, wbi_ref[k])

    for cb in range(S5_COLS // S5_COL_BLOCK):
        cols = slice(cb * S5_COL_BLOCK, (cb + 1) * S5_COL_BLOCK)
        ar = jnp.broadcast_to(ar_ref[:, cols], (SUBLANES, S5_COL_BLOCK))
        ai = jnp.broadcast_to(ai_ref[:, cols], (SUBLANES, S5_COL_BLOCK))

        def step(i, carry, cols=cols, ar=ar, ai=ai):
            r = pl.multiple_of(i * SUBLANES, SUBLANES)
            h_r, h_i = _cmul_add(ar, ai, carry[0], carry[1],
                                 hr_ref[pl.ds(r, SUBLANES), cols], hi_ref[pl.ds(r, SUBLANES), cols])
            hr_ref[pl.ds(r, SUBLANES), cols] = h_r
            hi_ref[pl.ds(r, SUBLANES), cols] = h_i
            return h_r, h_i

        zero = jnp.zeros((SUBLANES, S5_COL_BLOCK), F32)
        lax.fori_loop(0, seg, step, (zero, zero), unroll=2)

    end_r = hr_ref[rows - SUBLANES:rows, :]
    end_i = hi_ref[rows - SUBLANES:rows, :]
    as_r = pr_ref[seg - 1:seg, :]
    as_i = pi_ref[seg - 1:seg, :]
    if chained:
        @pl.when(pl.program_id(1) == 0)
        def _():
            cr_ref[...] = jnp.zeros_like(cr_ref)
            ci_ref[...] = jnp.zeros_like(ci_ref)

        cur_r, cur_i = cr_ref[...], ci_ref[...]
        for j in range(SUBLANES):
            sr_ref[j:j + 1, :] = cur_r
            si_ref[j:j + 1, :] = cur_i
            cur_r, cur_i = _cmul_add(as_r, as_i, cur_r, cur_i, end_r[j:j + 1], end_i[j:j + 1])
        cr_ref[...] = cur_r
        ci_ref[...] = cur_i
    else:
        sr_ref[...] = h0r_ref[...]
        si_ref[...] = h0i_ref[...]
    t_r, t_i = _cmul_add(as_r, as_i, sr_ref[...], si_ref[...], end_r, end_i)
    er_ref[...] = t_r
    ei_ref[...] = t_i

    for cb in range(S5_COLS // S5_COL_BLOCK):
        cols = slice(cb * S5_COL_BLOCK, (cb + 1) * S5_COL_BLOCK)
        s_r = jnp.concatenate([sr_ref[:, cols]] * 2, axis=0)
        s_i = jnp.concatenate([si_ref[:, cols]] * 2, axis=0)

        def fix(i2, _, cols=cols, s_r=s_r, s_i=s_i):
            r = pl.multiple_of(i2 * 2 * SUBLANES, 2 * SUBLANES)
            shape = (SUBLANES, S5_COL_BLOCK)
            p_r = jnp.concatenate(
                [jnp.broadcast_to(pr_ref[pl.ds(2 * i2, 1), cols], shape),
                 jnp.broadcast_to(pr_ref[pl.ds(2 * i2 + 1, 1), cols], shape)], axis=0)
            p_i = jnp.concatenate(
                [jnp.broadcast_to(pi_ref[pl.ds(2 * i2, 1), cols], shape),
                 jnp.broadcast_to(pi_ref[pl.ds(2 * i2 + 1, 1), cols], shape)], axis=0)
            h_r, h_i = _cmul_add(p_r, p_i, s_r, s_i,
                                 hr_ref[pl.ds(r, 2 * SUBLANES), cols],
                                 hi_ref[pl.ds(r, 2 * SUBLANES), cols])
            hbr_ref[pl.ds(r, 2 * SUBLANES), cols] = h_r.astype(BF16)
            hbi_ref[pl.ds(r, 2 * SUBLANES), cols] = h_i.astype(BF16)
            return 0

        lax.fori_loop(0, seg // 2, fix, 0)

    slabs_per_window = MXU_DIM // (MXU_DIM // S5_STATE * S5_GROUP_CH)
    ys = []
    for w in range(S5_WIDTH // MXU_DIM):
        acc = d_ref[:, w * MXU_DIM:(w + 1) * MXU_DIM] * jnp.concatenate(
            [up_ref[c] for c in range(w * MXU_DIM // LANES, (w + 1) * MXU_DIM // LANES)], axis=1)
        for s in range(slabs_per_window):
            j = w * slabs_per_window + s
            acc += _dot(hbr_ref[:, j * MXU_DIM:(j + 1) * MXU_DIM], wcr_ref[j])
            acc += _dot(hbi_ref[:, j * MXU_DIM:(j + 1) * MXU_DIM], wci_ref[j])
        ys.append(jax.nn.gelu(acc).astype(BF16))

    z = _dot(jnp.concatenate(ys, axis=1), wglu_ref[...])
    o = _rms(z[:, :S5_WIDTH] * jax.nn.sigmoid(z[:, S5_WIDTH:]), og_ref[...])
    for c in range(n_lane_blk):
        op_ref[c] = o[:, c * LANES:(c + 1) * LANES]

    for j in range(SUBLANES):
        for i0 in range(0, seg, 2 * SUBLANES):
            for c in range(n_lane_blk):
                o_ref[j * seg + i0:j * seg + i0 + 2 * SUBLANES, c * LANES:(c + 1) * LANES] = (
                    op_ref[c, pl.ds(i0 * SUBLANES + j, 2 * SUBLANES, stride=SUBLANES), :].astype(BF16))


def _s5_weights(lam_re, lam_im, b_re, b_im, c_re, c_im, d, log_step, seg):
    step = jnp.exp(log_step)[:, None]
    lr = jnp.minimum(lam_re, EIG_CLIP)
    li = lam_im
    mag = jnp.exp(lr * step)
    ar = mag * jnp.cos(li * step)
    ai = mag * jnp.sin(li * step)
    den = lr * lr + li * li
    fr = ((ar - 1.0) * lr + ai * li) / den
    fi = (ai * lr - (ar - 1.0) * li) / den
    bbar_r = fr[..., None] * b_re - fi[..., None] * b_im
    bbar_i = fr[..., None] * b_im + fi[..., None] * b_re
    t = jnp.arange(1, seg + 1, dtype=F32)[:, None, None]
    pm = jnp.exp(lr * step * t)
    p_r = (pm * jnp.cos(li * step * t)).reshape(seg, S5_COLS)
    p_i = (pm * jnp.sin(li * step * t)).reshape(seg, S5_COLS)

    gk = MXU_DIM // S5_GROUP_CH

    def in_map(bbar):
        bb = bbar.reshape(S5_GROUPS // gk, gk, S5_STATE, S5_GROUP_CH)
        w = jnp.einsum('kgnh,gG->kghGn', bb, jnp.eye(gk, dtype=F32))
        return w.reshape(S5_GROUPS // gk, MXU_DIM, gk * S5_STATE).astype(BF16)

    gs = MXU_DIM // S5_STATE
    n_slab = S5_COLS // MXU_DIM
    per_win = MXU_DIM // (gs * S5_GROUP_CH)

    def out_map(c):
        cc = c.reshape(n_slab // per_win, per_win, gs, S5_GROUP_CH, S5_STATE)
        w = jnp.einsum('wsghn,sS,gG->wsgnSGh', cc, jnp.eye(per_win, dtype=F32), jnp.eye(gs, dtype=F32))
        return w.reshape(n_slab, MXU_DIM, MXU_DIM).astype(BF16)

    return dict(wbr=in_map(bbar_r), wbi=in_map(bbar_i), wcr=out_map(c_re), wci=out_map(-c_im),
                ar=ar.reshape(1, S5_COLS), ai=ai.reshape(1, S5_COLS), pr=p_r, pi=p_i,
                d=d.reshape(1, S5_WIDTH))


def _s5_call(proj, sw, w_glu, og, seg, chained, h0=None, prev=None):
    rows = seg * SUBLANES
    const2 = lambda *_: (0, 0)
    const3 = lambda *_: (0, 0, 0)
    if chained:
        n_chunk = SEQ // rows
        grid = (BATCH, n_chunk)
        row_map = lambda b, c: (b * n_chunk + c, 0)
        end_map = lambda b, c: (b, 0, 0)
        n_end = BATCH
        sem = ("parallel", "arbitrary")
    else:
        grid = (DEC_BATCH // SUBLANES,)
        row_map = lambda k: (N_PROMPT // rows + k, 0)
        end_map = lambda k: (k, 0, 0)
        n_end = DEC_BATCH // SUBLANES
        sem = ("parallel",)
    half = S5_COLS // 2
    in_specs = [pl.BlockSpec((rows, S5_WIDTH), row_map),
                pl.BlockSpec((2, MXU_DIM, half), const3),
                pl.BlockSpec((2, MXU_DIM, half), const3),
                pl.BlockSpec((S5_COLS // MXU_DIM, MXU_DIM, MXU_DIM), const3),
                pl.BlockSpec((S5_COLS // MXU_DIM, MXU_DIM, MXU_DIM), const3),
                pl.BlockSpec((1, S5_COLS), const2),
                pl.BlockSpec((1, S5_COLS), const2),
                pl.BlockSpec((seg, S5_COLS), const2),
                pl.BlockSpec((seg, S5_COLS), const2),
                pl.BlockSpec((1, S5_WIDTH), const2),
                pl.BlockSpec((S5_WIDTH, 2 * S5_WIDTH), const2),
                pl.BlockSpec((1, S5_WIDTH), const2)]
    args = [proj, sw['wbr'], sw['wbi'], sw['wcr'], sw['wci'], sw['ar'], sw['ai'],
            sw['pr'][:seg], sw['pi'][:seg], sw['d'], w_glu, og.reshape(1, S5_WIDTH)]
    scratch = [pltpu.VMEM((S5_WIDTH // LANES, rows, LANES), F32),
               pltpu.VMEM((rows, S5_COLS), F32), pltpu.VMEM((rows, S5_COLS), F32),
               pltpu.VMEM((rows, S5_COLS), BF16), pltpu.VMEM((rows, S5_COLS), BF16),
               pltpu.VMEM((SUBLANES, S5_COLS), F32), pltpu.VMEM((SUBLANES, S5_COLS), F32),
               pltpu.VMEM((S5_WIDTH // LANES, rows, LANES), F32)]
    aliases = {}
    if chained:
        scratch += [pltpu.VMEM((1, S5_COLS), F32), pltpu.VMEM((1, S5_COLS), F32)]
    else:
        in_specs += [pl.BlockSpec((None, SUBLANES, S5_COLS), end_map),
                     pl.BlockSpec((None, SUBLANES, S5_COLS), end_map)]
        args += [h0[0], h0[1]]
    if prev is not None:
        in_specs.append(pl.BlockSpec(memory_space=pl.ANY))
        args.append(prev)
        aliases = {len(args) - 1: 0}
    kern = functools.partial(_s5_kernel, seg=seg, chained=chained)
    if prev is not None:
        kern = _drop_last_input(kern, len(args))
    return pl.pallas_call(
        kern,
        grid=grid,
        in_specs=in_specs,
        out_specs=[pl.BlockSpec((rows, S5_WIDTH), row_map),
                   pl.BlockSpec((None, SUBLANES, S5_COLS), end_map),
                   pl.BlockSpec((None, SUBLANES, S5_COLS), end_map)],
        out_shape=[jax.ShapeDtypeStruct((N_TOK, S5_WIDTH), BF16),
                   jax.ShapeDtypeStruct((n_end, SUBLANES, S5_COLS), F32),
                   jax.ShapeDtypeStruct((n_end, SUBLANES, S5_COLS), F32)],
        scratch_shapes=scratch,
        input_output_aliases=aliases,
        compiler_params=_params(*sem),
        name="s5_chained" if chained else "s5_streams",
    )(*args)


def _drop_last_input(kern, n_in):
    def wrapped(*refs):
        return kern(*refs[:n_in - 1], *refs[n_in:])
    return wrapped


def _gmlp_kernel(*refs, chunk, emit_vn):
    if emit_vn:
        u_ref, v_ref, lng_ref, lnb_ref, ws_ref, bst_ref, og_ref, o_ref, vn_ref, acc_ref = refs
    else:
        u_ref, v_ref, lng_ref, lnb_ref, ws_ref, bst_ref, og_ref, o_ref, acc_ref = refs
    rows = u_ref.shape[0]
    u = jax.nn.gelu(u_ref[...])
    v = jax.nn.gelu(v_ref[...])
    tri = (lax.broadcasted_iota(jnp.int32, (chunk, chunk), 0)
           >= lax.broadcasted_iota(jnp.int32, (chunk, chunk), 1))
    for h in range(GM_HEADS):
        cols = slice(h * GM_HEAD_DIM, (h + 1) * GM_HEAD_DIM)
        vh = v[:, cols]
        mu = jnp.mean(vh, axis=-1, keepdims=True)
        cen = vh - mu
        var = jnp.mean(cen * cen, axis=-1, keepdims=True)
        vn = cen * lax.rsqrt(var + EPS) * lng_ref[:, cols] + lnb_ref[:, cols]
        if emit_vn:
            vn_ref[:, cols] = vn
        w = jnp.where(tri, ws_ref[h], 0.0).astype(BF16)
        vb = vn.astype(BF16)
        bias = bst_ref[:, h:h + 1]
        for c in range(rows // chunk):
            rs = slice(c * chunk, (c + 1) * chunk)
            acc_ref[rs, cols] = u[rs, cols] * (_dot(w, vb[rs]) + bias)
    o_ref[...] = _rms(acc_ref[...], og_ref[...]).astype(BF16)


def _gmlp_call(proj, ln_g, ln_b, w_s, b_s, og, rows, chunk, first_block, n_blocks, emit_vn, prev=None):
    const2 = lambda i: (0, 0)
    in_specs = [pl.BlockSpec((rows, GM_WIDTH), lambda i: (first_block + i, 1)),
                pl.BlockSpec((rows, GM_WIDTH), lambda i: (first_block + i, 2)),
                pl.BlockSpec((1, GM_WIDTH), const2),
                pl.BlockSpec((1, GM_WIDTH), const2),
                pl.BlockSpec((GM_HEADS, chunk, chunk), lambda i: (0, 0, 0)),
                pl.BlockSpec((chunk, GM_HEADS), const2),
                pl.BlockSpec((1, GM_WIDTH), const2)]
    args = [proj, proj, ln_g.reshape(1, GM_WIDTH), ln_b.reshape(1, GM_WIDTH),
            w_s[:, :chunk, :chunk], b_s[:, :chunk].T, og.reshape(1, GM_WIDTH)]
    out_specs = [pl.BlockSpec((rows, GM_WIDTH), lambda i: (first_block + i, 0))]
    out_shape = [jax.ShapeDtypeStruct((N_TOK, GM_WIDTH), BF16)]
    if emit_vn:
        out_specs.append(pl.BlockSpec((rows, GM_WIDTH), lambda i: (i, 0)))
        out_shape.append(jax.ShapeDtypeStruct((n_blocks * rows, GM_WIDTH), F32))
    kern = functools.partial(_gmlp_kernel, chunk=chunk, emit_vn=emit_vn)
    aliases = {}
    if prev is not None:
        in_specs.append(pl.BlockSpec(memory_space=pl.ANY))
        args.append(prev)
        aliases = {len(args) - 1: 0}
        kern = _drop_last_input(kern, len(args))
    return pl.pallas_call(
        kern,
        grid=(n_blocks,),
        in_specs=in_specs,
        out_specs=out_specs,
        out_shape=out_shape,
        scratch_shapes=[pltpu.VMEM((rows, GM_WIDTH), F32)],
        input_output_aliases=aliases,
        compiler_params=_params("parallel"),
        name="gmlp_vn" if emit_vn else "gmlp",
    )(*args)


def _xattn_kernel(x_ref, g_ref, wq_ref, k_ref, v_ref, wo_ref, o_ref, att_ref, *, n_batch):
    rows = x_ref.shape[0]
    tb = rows // n_batch
    x = x_ref[...]
    q = _dot(_rms(x, g_ref[...]).astype(BF16), wq_ref[...]).astype(BF16)
    scale = XA_HEAD_DIM ** -0.5
    for b in range(n_batch):
        rs = slice(b * tb, (b + 1) * tb)
        for h in range(XA_HEADS):
            cols = slice(h * XA_HEAD_DIM, (h + 1) * XA_HEAD_DIM)
            s = lax.dot_general(q[rs, cols], k_ref[b, :, cols], (((1,), (1,)), ((), ())),
                                preferred_element_type=F32) * scale
            e = jnp.exp(s - jnp.max(s, axis=-1, keepdims=True))
            p = e / jnp.sum(e, axis=-1, keepdims=True)
            att_ref[rs, cols] = _dot(p.astype(BF16), v_ref[b, :, cols]).astype(BF16)
    o_ref[...] = x + _dot(att_ref[...], wo_ref[...])


def _xattn_call(x, g, wq, k, v, wo, rows, n_batch, first_block, n_blocks, blocks_per_kv, prev=None):
    const2 = lambda i: (0, 0)
    in_specs = [pl.BlockSpec((rows, D_MODEL), lambda i: (first_block + i, 0)),
                pl.BlockSpec((1, D_MODEL), const2),
                pl.BlockSpec((D_MODEL, D_MODEL), const2),
                pl.BlockSpec((n_batch, N_MEM, D_MODEL), lambda i: (i // blocks_per_kv, 0, 0)),
                pl.BlockSpec((n_batch, N_MEM, D_MODEL), lambda i: (i // blocks_per_kv, 0, 0)),
                pl.BlockSpec((D_MODEL, D_MODEL), const2)]
    args = [x, g.reshape(1, D_MODEL), wq, k, v, wo]
    kern = functools.partial(_xattn_kernel, n_batch=n_batch)
    aliases = {}
    if prev is not None:
        in_specs.append(pl.BlockSpec(memory_space=pl.ANY))
        args.append(prev)
        aliases = {len(args) - 1: 0}
        kern = _drop_last_input(kern, len(args))
    return pl.pallas_call(
        kern,
        grid=(n_blocks,),
        in_specs=in_specs,
        out_specs=pl.BlockSpec((rows, D_MODEL), lambda i: (first_block + i, 0)),
        out_shape=jax.ShapeDtypeStruct((N_TOK, D_MODEL), F32),
        scratch_shapes=[pltpu.VMEM((rows, D_MODEL), BF16)],
        input_output_aliases=aliases,
        compiler_params=_params("parallel"),
        name="xattn",
    )(*args)


def _moe_kernel(be_ref, nu_ref, first_ref, slot_ref, nxt_ref,
                x_ref, wg_hbm, wu_hbm, wd_hbm, bg_ref, bu_ref, bd_ref, o_ref,
                wbuf_ref, wgb_ref, wub_ref, wdb_ref, sem, *, layer):
    i = pl.program_id(0)
    active = i < nu_ref[0]

    def copies(e, s):
        return [pltpu.make_async_copy(w.at[layer, e], wbuf_ref.at[s, j], sem.at[s, j])
                for j, w in enumerate((wg_hbm, wu_hbm, wd_hbm))]

    @pl.when(jnp.logical_and(active, first_ref[i] == 1))
    def _():
        s = slot_ref[i]

        @pl.when(i == 0)
        def _():
            for c in copies(be_ref[0], s):
                c.start()

        @pl.when(nxt_ref[i] >= 0)
        def _():
            for c in copies(nxt_ref[i], 1 - s):
                c.start()

        for c in copies(be_ref[i], s):
            c.wait()

        def cast(r, _):
            rs = pl.ds(pl.multiple_of(r * MOE_CAST_ROWS, MOE_CAST_ROWS), MOE_CAST_ROWS)
            wgb_ref[rs, :] = wbuf_ref[s, 0, rs, :].astype(BF16)
            wub_ref[rs, :] = wbuf_ref[s, 1, rs, :].astype(BF16)
            wdb_ref[rs, :] = wbuf_ref[s, 2, rs, :].astype(BF16)
            return 0
        lax.fori_loop(0, D_MODEL // MOE_CAST_ROWS, cast, 0)

    @pl.when(active)
    def _():
        x = x_ref[...].astype(BF16)
        g = jnp.minimum(_dot(x, wgb_ref[...]) + bg_ref[...], SWIGLU_LIMIT)
        u = jnp.clip(_dot(x, wub_ref[...]) + bu_ref[...], -SWIGLU_LIMIT, SWIGLU_LIMIT)
        a = g * jax.nn.sigmoid(SWIGLU_ALPHA * g) * (u + 1.0)
        o_ref[...] = _dot(a.astype(BF16), wdb_ref[...]) + bd_ref[...]


def _moe_call(layer, xs, sched, w_gate, b_gate, w_up, b_up, w_down, b_down):
    def row_map(i, be, nu, *_):
        return (jnp.minimum(i, nu[0] - 1), 0)

    def b_map(i, be, *_):
        return (layer, be[i], 0, 0)

    b_spec = pl.BlockSpec((None, None, 1, D_FF), b_map)
    w_spec = pl.BlockSpec(memory_space=pl.ANY)
    grid_spec = pltpu.PrefetchScalarGridSpec(
        num_scalar_prefetch=5,
        grid=(MOE_BLOCKS,),
        in_specs=[pl.BlockSpec((MOE_ROWS, D_MODEL), row_map),
                  w_spec, w_spec, w_spec, b_spec, b_spec, b_spec],
        out_specs=pl.BlockSpec((MOE_ROWS, D_MODEL), row_map),
        scratch_shapes=[pltpu.VMEM((2, 3, D_MODEL, D_FF), F32)]
        + [pltpu.VMEM((D_MODEL, D_FF), BF16)] * 3
        + [pltpu.SemaphoreType.DMA((2, 3))],
    )
    shape4 = (DEPTH, N_EXPERTS, 1, D_FF)
    return pl.pallas_call(
        functools.partial(_moe_kernel, layer=layer),
        grid_spec=grid_spec,
        out_shape=jax.ShapeDtypeStruct((MOE_BLOCKS * MOE_ROWS, D_MODEL), F32),
        compiler_params=pltpu.CompilerParams(dimension_semantics=("arbitrary",),
                                             vmem_limit_bytes=56 * 1024 * 1024),
        name="moe_experts",
    )(*sched, xs, w_gate, w_up, w_down, b_gate.reshape(shape4), b_up.reshape(shape4),
      b_down.reshape(shape4))


def _route(e_t, cnt):
    blk_cnt = cnt[:, :, 0].astype(jnp.int32)
    counts = jnp.sum(blk_cnt, axis=0)
    padded = ((counts + MOE_ROWS - 1) // MOE_ROWS) * MOE_ROWS
    pend = jnp.cumsum(padded)
    pstart = pend - padded
    base = pstart[None, :] + jnp.cumsum(blk_cnt, axis=0) - blk_cnt
    dest_t = _dest(e_t, base.astype(F32)[:, :, None])
    n_used = (pend[-1] // MOE_ROWS).astype(jnp.int32)
    blk = jnp.minimum(jnp.arange(MOE_BLOCKS, dtype=jnp.int32), n_used - 1) * MOE_ROWS
    blk_exp = jnp.minimum(jnp.sum((pend[None, :] <= blk[:, None]).astype(jnp.int32), axis=1), N_EXPERTS - 1)
    ids = jnp.arange(MOE_BLOCKS, dtype=jnp.int32)
    first = jnp.logical_and(ids < n_used, jnp.logical_or(ids == 0, blk_exp != jnp.roll(blk_exp, 1)))
    first = first.astype(jnp.int32)
    slot = (jnp.cumsum(first) - 1) % 2
    eid = jnp.arange(N_EXPERTS, dtype=jnp.int32)
    later = jnp.logical_and(eid[None, :] > eid[:, None], (counts > 0)[None, :])
    nxt_e = jnp.min(jnp.where(later, eid[None, :], N_EXPERTS), axis=1)
    nxt = jnp.where(nxt_e < N_EXPERTS, nxt_e, -1)[blk_exp].astype(jnp.int32)
    tok = jnp.tile(jnp.arange(N_TOK, dtype=jnp.int32), TOP_K)
    row_tok = jnp.zeros((MOE_BLOCKS * MOE_ROWS,), jnp.int32).at[dest_t.reshape(-1)].set(tok)
    return dest_t, row_tok, (blk_exp, n_used.reshape(1), first, slot.astype(jnp.int32), nxt)


def kernel(x_prompt, x_sample, mem_prompt, state_s5_re, state_s5_im, cache_mem_k, cache_mem_v, norm_mix_g, w_in, s5_lambda_re, s5_lambda_im, s5_b_re, s5_b_im, s5_c_re, s5_c_im, s5_d, s5_log_step, s5_w_glu, gm_ln_g, gm_ln_b, gm_w_s, gm_b_s, out_norm_g, w_out, xa_norm_g, mem_norm_g, w_xq, w_xk, w_xv, w_xo, ffn_norm_g, router_w, router_b, e_w_gate, e_b_gate, e_w_up, e_b_up, e_w_down, e_b_down, final_norm_g):
    xs = (x_prompt.reshape(N_PROMPT, D_MODEL), x_sample.reshape(N_SAMPLE, D_MODEL))
    mem = mem_prompt.reshape(BATCH * N_MEM, D_MODEL)
    outs = {k: [] for k in ('s5r_p', 's5i_p', 'mk_p', 'mv_p', 's5r_s', 's5i_s', 'gmv_s')}
    for l in range(DEPTH):
        proj = _norm_matmul(xs, norm_mix_g[l], w_in[l].astype(BF16))
        sw = _s5_weights(s5_lambda_re[l], s5_lambda_im[l], s5_b_re[l], s5_b_im[l], s5_c_re[l], s5_c_im[l],
                         s5_d[l], s5_log_step[l], S5_SEG_PROMPT)
        w_glu = s5_w_glu[l].astype(BF16)
        og = out_norm_g[l]
        ms5, er_p, ei_p = _s5_call(proj, sw, w_glu, og[:S5_WIDTH], S5_SEG_PROMPT, True)
        h0 = (state_s5_re[l].reshape(DEC_BATCH // SUBLANES, SUBLANES, S5_COLS),
              state_s5_im[l].reshape(DEC_BATCH // SUBLANES, SUBLANES, S5_COLS))
        ms5, er_s, ei_s = _s5_call(proj, sw, w_glu, og[:S5_WIDTH], DEC_SEQ, False, h0=h0, prev=ms5)
        outs['s5r_p'].append(er_p[:, SUBLANES - 1].reshape(BATCH, S5_GROUPS, S5_STATE))
        outs['s5i_p'].append(ei_p[:, SUBLANES - 1].reshape(BATCH, S5_GROUPS, S5_STATE))
        outs['s5r_s'].append(er_s.reshape(DEC_BATCH, S5_GROUPS, S5_STATE))
        outs['s5i_s'].append(ei_s.reshape(DEC_BATCH, S5_GROUPS, S5_STATE))

        (mgm,) = _gmlp_call(proj, gm_ln_g[l], gm_ln_b[l], gm_w_s[l], gm_b_s[l], og[S5_WIDTH:],
                            GM_ROWS_PROMPT, GM_CHUNK, 0, N_PROMPT // GM_ROWS_PROMPT, False)
        mgm, vn = _gmlp_call(proj, gm_ln_g[l], gm_ln_b[l], gm_w_s[l], gm_b_s[l], og[S5_WIDTH:],
                             N_SAMPLE, DEC_SEQ, N_PROMPT // N_SAMPLE, 1, True, prev=mgm)
        outs['gmv_s'].append(vn.reshape(DEC_BATCH, DEC_SEQ, GM_HEADS, GM_HEAD_DIM))
        x = _out_proj(xs, ms5, mgm, w_out[l].astype(BF16))

        mk = _norm_matmul((mem,), mem_norm_g[l], w_xk[l].astype(BF16))
        mv = _norm_matmul((mem,), mem_norm_g[l], w_xv[l].astype(BF16))
        outs['mk_p'].append(mk.reshape(BATCH, N_MEM, XA_HEADS, XA_HEAD_DIM))
        outs['mv_p'].append(mv.reshape(BATCH, N_MEM, XA_HEADS, XA_HEAD_DIM))
        wq = w_xq[l].astype(BF16)
        wo = w_xo[l].astype(BF16)
        xa = _xattn_call(x, xa_norm_g[l], wq, mk.reshape(BATCH, N_MEM, D_MODEL).astype(BF16),
                         mv.reshape(BATCH, N_MEM, D_MODEL).astype(BF16), wo,
                         XA_ROWS_PROMPT, 1, 0, N_PROMPT // XA_ROWS_PROMPT, SEQ // XA_ROWS_PROMPT)
        rows_s = XA_SAMPLE_BATCHES * DEC_SEQ
        x = _xattn_call(x, xa_norm_g[l], wq,
                        cache_mem_k[l].reshape(DEC_BATCH, N_MEM, D_MODEL).astype(BF16),
                        cache_mem_v[l].reshape(DEC_BATCH, N_MEM, D_MODEL).astype(BF16), wo,
                        rows_s, XA_SAMPLE_BATCHES, N_PROMPT // rows_s, N_SAMPLE // rows_s, 1, prev=xa)

        h, e_t, gates_t, cnt = _router(x, ffn_norm_g[l], router_w[l], router_b[l])
        dest_t, row_tok, sched = _route(e_t, cnt)
        ys = _moe_call(l, h[row_tok], sched, e_w_gate, e_b_gate, e_w_up, e_b_up, e_w_down, e_b_down)
        x = _combine(x, ys[dest_t.reshape(-1)], gates_t.T, final_norm_g if l == DEPTH - 1 else None)
        xs = (x,)

    y_prompt = x[0].reshape(BATCH, SEQ, D_MODEL)
    y_sample = x[1].reshape(DEC_BATCH, DEC_SEQ, D_MODEL)
    st = jnp.stack
    return (y_prompt, y_sample, st(outs['s5r_p']), st(outs['s5i_p']), st(outs['mk_p']), st(outs['mv_p']),
            st(outs['s5r_s']), st(outs['s5i_s']), st(outs['gmv_s']))
```

```python
import functools

import jax
import jax.numpy as jnp
from jax import lax
from jax.experimental import pallas as pl
from jax.experimental.pallas import tpu as pltpu
from jax.experimental.pallas import tpu_sc as plsc

F32 = jnp.float32
BF16 = jnp.bfloat16

D_MODEL = 1024
BATCH = 4
SEQ = 4096
DEPTH = 2
DEC_BATCH = 16
DEC_SEQ = 32
S5_GROUPS = 32
S5_GROUP_CH = 16
S5_STATE = 64
S5_WIDTH = S5_GROUPS * S5_GROUP_CH
S5_COLS = S5_GROUPS * S5_STATE
GM_HEADS = 4
GM_HEAD_DIM = 128
GM_WIDTH = GM_HEADS * GM_HEAD_DIM
GM_CHUNK = 128
IN_PROJ = S5_WIDTH + 2 * GM_WIDTH
N_MEM = 256
XA_HEADS = 4
XA_HEAD_DIM = D_MODEL // XA_HEADS
N_EXPERTS = 32
TOP_K = 4
D_FF = D_MODEL
SWIGLU_LIMIT = 7.0
SWIGLU_ALPHA = 1.702
EPS = 1e-6
EIG_CLIP = -1e-4

N_PROMPT = BATCH * SEQ
N_SAMPLE = DEC_BATCH * DEC_SEQ
N_TOK = N_PROMPT + N_SAMPLE

SUBLANES = 8
LANES = 128
MXU_DIM = 256
VMEM_LIMIT = 48 * 1024 * 1024
SC_CORES = 2
SC_SUBCORES = 16
SC_WORKERS = SC_CORES * SC_SUBCORES

PIECE_TILE = (D_MODEL // LANES, SUBLANES, LANES)
PIECES_PER_ROW = D_MODEL // LANES

ROW_BLOCK = 512
S5_SEG_PROMPT = 64
S5_COL_BLOCK = 512
GM_ROWS_PROMPT = 256
XA_ROWS_PROMPT = 512
XA_SAMPLE_BATCHES = 4
MOE_ROWS = 512
MOE_BLOCKS = -(-N_TOK * TOP_K // MOE_ROWS) + N_EXPERTS
MOE_CAST_ROWS = 128


def _params(*sem):
    return pltpu.CompilerParams(dimension_semantics=sem, vmem_limit_bytes=VMEM_LIMIT)


def _rms(x, g):
    return x * lax.rsqrt(jnp.mean(x * x, axis=-1, keepdims=True) + EPS) * g


def _dot(a, b):
    return jnp.dot(a, b, preferred_element_type=F32)


PROMPT_BLOCKS = N_PROMPT // ROW_BLOCK
SAMPLE_BLOCKS = N_SAMPLE // ROW_BLOCK


def _token_sources(xs):
    if len(xs) == 1:
        return [pl.BlockSpec((ROW_BLOCK, D_MODEL), lambda i: (i, 0))]
    return [pl.BlockSpec((ROW_BLOCK, D_MODEL), lambda i: (jnp.minimum(i, PROMPT_BLOCKS - 1), 0)),
            pl.BlockSpec((ROW_BLOCK, D_MODEL), lambda i: (jnp.maximum(i - PROMPT_BLOCKS, 0), 0))]


def _for_group(refs, body):
    if len(refs) == 1:
        body(refs[0])
        return
    i = pl.program_id(0)
    pl.when(i < PROMPT_BLOCKS)(lambda: body(refs[0]))
    pl.when(i >= PROMPT_BLOCKS)(lambda: body(refs[1]))


def _norm_matmul_kernel(*refs, n_src):
    g_ref, w_ref, o_ref = refs[n_src:]

    def body(x_ref):
        o_ref[...] = _dot(_rms(x_ref[...], g_ref[...]).astype(BF16), w_ref[...])
    _for_group(refs[:n_src], body)


def _norm_matmul(xs, g, w):
    d = D_MODEL
    n = sum(x.shape[0] for x in xs)
    dout = w.shape[1]
    return pl.pallas_call(
        functools.partial(_norm_matmul_kernel, n_src=len(xs)),
        grid=(n // ROW_BLOCK,),
        in_specs=_token_sources(xs) + [pl.BlockSpec((1, d), lambda i: (0, 0)),
                                       pl.BlockSpec((d, dout), lambda i: (0, 0))],
        out_specs=pl.BlockSpec((ROW_BLOCK, dout), lambda i: (i, 0)),
        out_shape=jax.ShapeDtypeStruct((n, dout), F32),
        compiler_params=_params("parallel"),
        name="norm_matmul",
    )(*xs, g.reshape(1, d), w)


def _out_proj_kernel(*refs, n_src):
    a_ref, b_ref, wa_ref, wb_ref, o_ref = refs[n_src:]

    def body(x_ref):
        o_ref[...] = x_ref[...] + _dot(a_ref[...], wa_ref[...]) + _dot(b_ref[...], wb_ref[...])
    _for_group(refs[:n_src], body)


def _out_proj(xs, a, b, w):
    d = D_MODEL
    n = a.shape[0]
    wa, wb = w[:S5_WIDTH], w[S5_WIDTH:]
    rows = ROW_BLOCK
    return pl.pallas_call(
        functools.partial(_out_proj_kernel, n_src=len(xs)),
        grid=(n // rows,),
        in_specs=_token_sources(xs) + [pl.BlockSpec((rows, S5_WIDTH), lambda i: (i, 0)),
                                       pl.BlockSpec((rows, GM_WIDTH), lambda i: (i, 0)),
                                       pl.BlockSpec((S5_WIDTH, d), lambda i: (0, 0)),
                                       pl.BlockSpec((GM_WIDTH, d), lambda i: (0, 0))],
        out_specs=pl.BlockSpec((rows, d), lambda i: (i, 0)),
        out_shape=jax.ShapeDtypeStruct((n, d), F32),
        compiler_params=_params("parallel"),
        name="out_proj",
    )(*xs, a, b, wa, wb)


def _router_kernel(x_ref, g_ref, wt_ref, b_ref, h_ref, e_ref, gate_ref, cnt_ref):
    h = _rms(x_ref[...], g_ref[...])
    for c in range(D_MODEL // LANES):
        h_ref[:, c] = h[:, c * LANES:(c + 1) * LANES].reshape(h.shape[0] // SUBLANES, SUBLANES, LANES)
    logits = lax.dot_general(wt_ref[...], h.astype(BF16), (((1,), (1,)), ((), ())),
                             preferred_element_type=F32) + b_ref[...]
    sub = lax.broadcasted_iota(jnp.int32, logits.shape, 0)
    work = logits
    chosen = jnp.zeros(logits.shape, F32)
    vals, idxs = [], []
    for _ in range(TOP_K):
        m = jnp.max(work, axis=0, keepdims=True)
        idx = jnp.min(jnp.where(work == m, sub, N_EXPERTS), axis=0, keepdims=True)
        sel = sub == idx
        vals.append(m)
        idxs.append(idx)
        work = jnp.where(sel, -jnp.inf, work)
        chosen = jnp.where(sel, 1.0, chosen)
    ex = [jnp.exp(v - vals[0]) for v in vals]
    den = ex[0] + ex[1] + ex[2] + ex[3]
    gate_ref[...] = jnp.concatenate([e / den for e in ex], axis=0)
    e_ref[...] = jnp.concatenate(idxs, axis=0)
    cnt_ref[...] = jnp.broadcast_to(jnp.sum(chosen, axis=1, keepdims=True), (N_EXPERTS, LANES))


def _router(x, g, w, b):
    n, d = x.shape
    rows = ROW_BLOCK
    return pl.pallas_call(
        _router_kernel,
        grid=(n // rows,),
        in_specs=[pl.BlockSpec((rows, d), lambda i: (i, 0)),
                  pl.BlockSpec((1, d), lambda i: (0, 0)),
                  pl.BlockSpec((N_EXPERTS, d), lambda i: (0, 0)),
                  pl.BlockSpec((N_EXPERTS, 1), lambda i: (0, 0))],
        out_specs=[pl.BlockSpec((rows // SUBLANES,) + PIECE_TILE, lambda i: (i, 0, 0, 0)),
                   pl.BlockSpec((TOP_K, rows), lambda i: (0, i)),
                   pl.BlockSpec((TOP_K, rows), lambda i: (0, i)),
                   pl.BlockSpec((None, N_EXPERTS, LANES), lambda i: (i, 0, 0))],
        out_shape=[jax.ShapeDtypeStruct((n // SUBLANES,) + PIECE_TILE, F32),
                   jax.ShapeDtypeStruct((TOP_K, n), jnp.int32),
                   jax.ShapeDtypeStruct((TOP_K, n), F32),
                   jax.ShapeDtypeStruct((n // rows, N_EXPERTS, LANES), F32)],
        compiler_params=_params("parallel"),
        name="router",
    )(x, g.reshape(1, d), w.T.astype(BF16), b.reshape(N_EXPERTS, 1))


def _dest_kernel(e_ref, base_ref, tri_ref, d_ref):
    e = e_ref[...]
    sub = lax.broadcasted_iota(jnp.int32, (N_EXPERTS, e.shape[1]), 0)
    hot = [sub == e[k:k + 1, :] for k in range(TOP_K)]
    chosen = sum(jnp.where(hk, 1.0, 0.0) for hk in hot)
    pos = base_ref[...] + _dot(chosen.astype(BF16), tri_ref[...])
    d_ref[...] = jnp.concatenate(
        [jnp.sum(jnp.where(hk, pos, 0.0), axis=0, keepdims=True) for hk in hot], axis=0).astype(jnp.int32)


def _dest(e_t, base):
    n = e_t.shape[1]
    rows = ROW_BLOCK
    tri = jnp.triu(jnp.ones((rows, rows), F32), 1).astype(BF16)
    return pl.pallas_call(
        _dest_kernel,
        grid=(n // rows,),
        in_specs=[pl.BlockSpec((TOP_K, rows), lambda i: (0, i)),
                  pl.BlockSpec((None, N_EXPERTS, 1), lambda i: (i, 0, 0)),
                  pl.BlockSpec((rows, rows), lambda i: (0, 0))],
        out_specs=pl.BlockSpec((TOP_K, rows), lambda i: (0, i)),
        out_shape=jax.ShapeDtypeStruct((TOP_K, n), jnp.int32),
        compiler_params=_params("parallel"),
        name="moe_dest",
    )(e_t, base, tri)


def _combine_kernel(*refs, final):
    if final:
        x_ref, g0, g1, g2, g3, gate_ref, fg_ref, op_ref, os_ref = refs
    else:
        x_ref, g0, g1, g2, g3, gate_ref, o_ref = refs
    acc = x_ref[...]
    for k, g in enumerate((g0, g1, g2, g3)):
        acc = acc + gate_ref[:, k:k + 1] * g[...]
    if final:
        def store(o_ref):
            o_ref[...] = _rms(acc, fg_ref[...])
        _for_group((op_ref, os_ref), store)
    else:
        o_ref[...] = acc


def _combine(x, picked, gates, final_g=None):
    n, d = x.shape
    rows = ROW_BLOCK
    picked = picked.reshape(TOP_K, n, d)
    final = final_g is not None
    in_specs = [pl.BlockSpec((rows, d), lambda i: (i, 0))]
    in_specs += [pl.BlockSpec((None, rows, d), functools.partial(lambda i, k: (k, i, 0), k=k))
                 for k in range(TOP_K)]
    in_specs.append(pl.BlockSpec((rows, TOP_K), lambda i: (i, 0)))
    args = [x, picked, picked, picked, picked, gates]
    out_specs = pl.BlockSpec((rows, d), lambda i: (i, 0))
    out_shape = jax.ShapeDtypeStruct((n, d), F32)
    if final:
        in_specs.append(pl.BlockSpec((1, d), lambda i: (0, 0)))
        args.append(final_g.reshape(1, d))
        out_specs = _token_sources((None, None))
        out_shape = [jax.ShapeDtypeStruct((N_PROMPT, d), F32), jax.ShapeDtypeStruct((N_SAMPLE, d), F32)]
    return pl.pallas_call(
        functools.partial(_combine_kernel, final=final),
        grid=(n // rows,),
        in_specs=in_specs,
        out_specs=out_specs,
        out_shape=out_shape,
        compiler_params=_params("arbitrary" if final else "parallel"),
        name="moe_combine",
    )(*args)


def _cmul_add(ar, ai, hr, hi, br, bi):
    return ar * hr - ai * hi + br, ar * hi + ai * hr + bi


def _s5_kernel(*refs, seg, chained):
    if chained:
        (u_ref, wbr_ref, wbi_ref, wcr_ref, wci_ref, ar_ref, ai_ref, pr_ref, pi_ref, d_ref,
         wglu_ref, og_ref, o_ref, er_ref, ei_ref,
         up_ref, hr_ref, hi_ref, hbr_ref, hbi_ref, sr_ref, si_ref, op_ref, cr_ref, ci_ref) = refs
    else:
        (u_ref, wbr_ref, wbi_ref, wcr_ref, wci_ref, ar_ref, ai_ref, pr_ref, pi_ref, d_ref,
         wglu_ref, og_ref, h0r_ref, h0i_ref, o_ref, er_ref, ei_ref,
         up_ref, hr_ref, hi_ref, hbr_ref, hbi_ref, sr_ref, si_ref, op_ref) = refs
    rows = seg * SUBLANES
    half = S5_COLS // 2

    n_lane_blk = S5_WIDTH // LANES
    for j in range(SUBLANES):
        for i0 in range(0, seg, SUBLANES):
            for c in range(n_lane_blk):
                up_ref[c, pl.ds(i0 * SUBLANES + j, SUBLANES, stride=SUBLANES), :] = (
                    u_ref[j * seg + i0:j * seg + i0 + SUBLANES, c * LANES:(c + 1) * LANES])

    up = jnp.concatenate([up_ref[c] for c in range(n_lane_blk)], axis=1)
    ub = up.astype(BF16)
    for k in range(2):
        uk = ub[:, k * MXU_DIM:(k + 1) * MXU_DIM]
        hr_ref[:, k * half:(k + 1) * half] = _dot(uk, wbr_ref[k])
        hi_ref[:, k * half:(k + 1) * half] = _dot(uk, wbi_ref[k])

    for cb in range(S5_COLS // S5_COL_BLOCK):
        cols = slice(cb * S5_COL_BLOCK, (cb + 1) * S5_COL_BLOCK)
        ar = jnp.broadcast_to(ar_ref[:, cols], (SUBLANES, S5_COL_BLOCK))
        ai = jnp.broadcast_to(ai_ref[:, cols], (SUBLANES, S5_COL_BLOCK))

        def step(i, carry, cols=cols, ar=ar, ai=ai):
            r = pl.multiple_of(i * SUBLANES, SUBLANES)
            h_r, h_i = _cmul_add(ar, ai, carry[0], carry[1],
                                 hr_ref[pl.ds(r, SUBLANES), cols], hi_ref[pl.ds(r, SUBLANES), cols])
            hr_ref[pl.ds(r, SUBLANES), cols] = h_r
            hi_ref[pl.ds(r, SUBLANES), cols] = h_i
            return h_r, h_i

        zero = jnp.zeros((SUBLANES, S5_COL_BLOCK), F32)
        lax.fori_loop(0, seg, step, (zero, zero), unroll=2)

    end_r = hr_ref[rows - SUBLANES:rows, :]
    end_i = hi_ref[rows - SUBLANES:rows, :]
    as_r = pr_ref[seg - 1:seg, :]
    as_i = pi_ref[seg - 1:seg, :]
    if chained:
        @pl.when(pl.program_id(1) == 0)
        def _():
            cr_ref[...] = jnp.zeros_like(cr_ref)
            ci_ref[...] = jnp.zeros_like(ci_ref)

        cur_r, cur_i = cr_ref[...], ci_ref[...]
        for j in range(SUBLANES):
            sr_ref[j:j + 1, :] = cur_r
            si_ref[j:j + 1, :] = cur_i
            cur_r, cur_i = _cmul_add(as_r, as_i, cur_r, cur_i, end_r[j:j + 1], end_i[j:j + 1])
        cr_ref[...] = cur_r
        ci_ref[...] = cur_i
    else:
        sr_ref[...] = h0r_ref[...]
        si_ref[...] = h0i_ref[...]
    t_r, t_i = _cmul_add(as_r, as_i, sr_ref[...], si_ref[...], end_r, end_i)
    er_ref[...] = t_r
    ei_ref[...] = t_i

    for cb in range(S5_COLS // S5_COL_BLOCK):
        cols = slice(cb * S5_COL_BLOCK, (cb + 1) * S5_COL_BLOCK)
        s_r = jnp.concatenate([sr_ref[:, cols]] * 2, axis=0)
        s_i = jnp.concatenate([si_ref[:, cols]] * 2, axis=0)

        def fix(i2, _, cols=cols, s_r=s_r, s_i=s_i):
            r = pl.multiple_of(i2 * 2 * SUBLANES, 2 * SUBLANES)
            shape = (SUBLANES, S5_COL_BLOCK)
            p_r = jnp.concatenate(
                [jnp.broadcast_to(pr_ref[pl.ds(2 * i2, 1), cols], shape),
                 jnp.broadcast_to(pr_ref[pl.ds(2 * i2 + 1, 1), cols], shape)], axis=0)
            p_i = jnp.concatenate(
                [jnp.broadcast_to(pi_ref[pl.ds(2 * i2, 1), cols], shape),
                 jnp.broadcast_to(pi_ref[pl.ds(2 * i2 + 1, 1), cols], shape)], axis=0)
            h_r, h_i = _cmul_add(p_r, p_i, s_r, s_i,
                                 hr_ref[pl.ds(r, 2 * SUBLANES), cols],
                                 hi_ref[pl.ds(r, 2 * SUBLANES), cols])
            hbr_ref[pl.ds(r, 2 * SUBLANES), cols] = h_r.astype(BF16)
            hbi_ref[pl.ds(r, 2 * SUBLANES), cols] = h_i.astype(BF16)
            return 0

        lax.fori_loop(0, seg // 2, fix, 0)

    slabs_per_window = MXU_DIM // (MXU_DIM // S5_STATE * S5_GROUP_CH)
    ys = []
    for w in range(S5_WIDTH // MXU_DIM):
        acc = d_ref[:, w * MXU_DIM:(w + 1) * MXU_DIM] * jnp.concatenate(
            [up_ref[c] for c in range(w * MXU_DIM // LANES, (w + 1) * MXU_DIM // LANES)], axis=1)
        for s in range(slabs_per_window):
            j = w * slabs_per_window + s
            acc += _dot(hbr_ref[:, j * MXU_DIM:(j + 1) * MXU_DIM], wcr_ref[j])
            acc += _dot(hbi_ref[:, j * MXU_DIM:(j + 1) * MXU_DIM], wci_ref[j])
        ys.append(jax.nn.gelu(acc).astype(BF16))

    z = _dot(jnp.concatenate(ys, axis=1), wglu_ref[...])
    o = _rms(z[:, :S5_WIDTH] * jax.nn.sigmoid(z[:, S5_WIDTH:]), og_ref[...])
    for c in range(n_lane_blk):
        op_ref[c] = o[:, c * LANES:(c + 1) * LANES]

    for j in range(SUBLANES):
        for i0 in range(0, seg, 2 * SUBLANES):
            for c in range(n_lane_blk):
                o_ref[j * seg + i0:j * seg + i0 + 2 * SUBLANES, c * LANES:(c + 1) * LANES] = (
                    op_ref[c, pl.ds(i0 * SUBLANES + j, 2 * SUBLANES, stride=SUBLANES), :].astype(BF16))


def _s5_weights(lam_re, lam_im, b_re, b_im, c_re, c_im, d, log_step, seg):
    step = jnp.exp(log_step)[:, None]
    lr = jnp.minimum(lam_re, EIG_CLIP)
    li = lam_im
    mag = jnp.exp(lr * step)
    ar = mag * jnp.cos(li * step)
    ai = mag * jnp.sin(li * step)
    den = lr * lr + li * li
    fr = ((ar - 1.0) * lr + ai * li) / den
    fi = (ai * lr - (ar - 1.0) * li) / den
    bbar_r = fr[..., None] * b_re - fi[..., None] * b_im
    bbar_i = fr[..., None] * b_im + fi[..., None] * b_re
    t = jnp.arange(1, seg + 1, dtype=F32)[:, None, None]
    pm = jnp.exp(lr * step * t)
    p_r = (pm * jnp.cos(li * step * t)).reshape(seg, S5_COLS)
    p_i = (pm * jnp.sin(li * step * t)).reshape(seg, S5_COLS)

    gk = MXU_DIM // S5_GROUP_CH

    def in_map(bbar):
        bb = bbar.reshape(S5_GROUPS // gk, gk, S5_STATE, S5_GROUP_CH)
        w = jnp.einsum('kgnh,gG->kghGn', bb, jnp.eye(gk, dtype=F32))
        return w.reshape(S5_GROUPS // gk, MXU_DIM, gk * S5_STATE).astype(BF16)

    gs = MXU_DIM // S5_STATE
    n_slab = S5_COLS // MXU_DIM
    per_win = MXU_DIM // (gs * S5_GROUP_CH)

    def out_map(c):
        cc = c.reshape(n_slab // per_win, per_win, gs, S5_GROUP_CH, S5_STATE)
        w = jnp.einsum('wsghn,sS,gG->wsgnSGh', cc, jnp.eye(per_win, dtype=F32), jnp.eye(gs, dtype=F32))
        return w.reshape(n_slab, MXU_DIM, MXU_DIM).astype(BF16)

    return dict(wbr=in_map(bbar_r), wbi=in_map(bbar_i), wcr=out_map(c_re), wci=out_map(-c_im),
                ar=ar.reshape(1, S5_COLS), ai=ai.reshape(1, S5_COLS), pr=p_r, pi=p_i,
                d=d.reshape(1, S5_WIDTH))


def _s5_call(proj, sw, w_glu, og, seg, chained, h0=None, prev=None):
    rows = seg * SUBLANES
    const2 = lambda *_: (0, 0)
    const3 = lambda *_: (0, 0, 0)
    if chained:
        n_chunk = SEQ // rows
        grid = (BATCH, n_chunk)
        row_map = lambda b, c: (b * n_chunk + c, 0)
        end_map = lambda b, c: (b, 0, 0)
        n_end = BATCH
        sem = ("parallel", "arbitrary")
    else:
        grid = (DEC_BATCH // SUBLANES,)
        row_map = lambda k: (N_PROMPT // rows + k, 0)
        end_map = lambda k: (k, 0, 0)
        n_end = DEC_BATCH // SUBLANES
        sem = ("parallel",)
    half = S5_COLS // 2
    in_specs = [pl.BlockSpec((rows, S5_WIDTH), row_map),
                pl.BlockSpec((2, MXU_DIM, half), const3),
                pl.BlockSpec((2, MXU_DIM, half), const3),
                pl.BlockSpec((S5_COLS // MXU_DIM, MXU_DIM, MXU_DIM), const3),
                pl.BlockSpec((S5_COLS // MXU_DIM, MXU_DIM, MXU_DIM), const3),
                pl.BlockSpec((1, S5_COLS), const2),
                pl.BlockSpec((1, S5_COLS), const2),
                pl.BlockSpec((seg, S5_COLS), const2),
                pl.BlockSpec((seg, S5_COLS), const2),
                pl.BlockSpec((1, S5_WIDTH), const2),
                pl.BlockSpec((S5_WIDTH, 2 * S5_WIDTH), const2),
                pl.BlockSpec((1, S5_WIDTH), const2)]
    args = [proj, sw['wbr'], sw['wbi'], sw['wcr'], sw['wci'], sw['ar'], sw['ai'],
            sw['pr'][:seg], sw['pi'][:seg], sw['d'], w_glu, og.reshape(1, S5_WIDTH)]
    scratch = [pltpu.VMEM((S5_WIDTH // LANES, rows, LANES), F32),
               pltpu.VMEM((rows, S5_COLS), F32), pltpu.VMEM((rows, S5_COLS), F32),
               pltpu.VMEM((rows, S5_COLS), BF16), pltpu.VMEM((rows, S5_COLS), BF16),
               pltpu.VMEM((SUBLANES, S5_COLS), F32), pltpu.VMEM((SUBLANES, S5_COLS), F32),
               pltpu.VMEM((S5_WIDTH // LANES, rows, LANES), F32)]
    aliases = {}
    if chained:
        scratch += [pltpu.VMEM((1, S5_COLS), F32), pltpu.VMEM((1, S5_COLS), F32)]
    else:
        in_specs += [pl.BlockSpec((None, SUBLANES, S5_COLS), end_map),
                     pl.BlockSpec((None, SUBLANES, S5_COLS), end_map)]
        args += [h0[0], h0[1]]
    if prev is not None:
        in_specs.append(pl.BlockSpec(memory_space=pl.ANY))
        args.append(prev)
        aliases = {len(args) - 1: 0}
    kern = functools.partial(_s5_kernel, seg=seg, chained=chained)
    if prev is not None:
        kern = _drop_last_input(kern, len(args))
    return pl.pallas_call(
        kern,
        grid=grid,
        in_specs=in_specs,
        out_specs=[pl.BlockSpec((rows, S5_WIDTH), row_map),
                   pl.BlockSpec((None, SUBLANES, S5_COLS), end_map),
                   pl.BlockSpec((None, SUBLANES, S5_COLS), end_map)],
        out_shape=[jax.ShapeDtypeStruct((N_TOK, S5_WIDTH), BF16),
                   jax.ShapeDtypeStruct((n_end, SUBLANES, S5_COLS), F32),
                   jax.ShapeDtypeStruct((n_end, SUBLANES, S5_COLS), F32)],
        scratch_shapes=scratch,
        input_output_aliases=aliases,
        compiler_params=_params(*sem),
        name="s5_chained" if chained else "s5_streams",
    )(*args)


def _drop_last_input(kern, n_in):
    def wrapped(*refs):
        return kern(*refs[:n_in - 1], *refs[n_in:])
    return wrapped


def _gmlp_kernel(*refs, chunk, emit_vn):
    if emit_vn:
        u_ref, v_ref, lng_ref, lnb_ref, ws_ref, bst_ref, og_ref, o_ref, vn_ref, acc_ref = refs
    else:
        u_ref, v_ref, lng_ref, lnb_ref, ws_ref, bst_ref, og_ref, o_ref, acc_ref = refs
    rows = u_ref.shape[0]
    u = jax.nn.gelu(u_ref[...])
    v = jax.nn.gelu(v_ref[...])
    tri = (lax.broadcasted_iota(jnp.int32, (chunk, chunk), 0)
           >= lax.broadcasted_iota(jnp.int32, (chunk, chunk), 1))
    for h in range(GM_HEADS):
        cols = slice(h * GM_HEAD_DIM, (h + 1) * GM_HEAD_DIM)
        vh = v[:, cols]
        mu = jnp.mean(vh, axis=-1, keepdims=True)
        cen = vh - mu
        var = jnp.mean(cen * cen, axis=-1, keepdims=True)
        vn = cen * lax.rsqrt(var + EPS) * lng_ref[:, cols] + lnb_ref[:, cols]
        if emit_vn:
            vn_ref[:, cols] = vn
        w = jnp.where(tri, ws_ref[h], 0.0).astype(BF16)
        vb = vn.astype(BF16)
        bias = bst_ref[:, h:h + 1]
        for c in range(rows // chunk):
            rs = slice(c * chunk, (c + 1) * chunk)
            acc_ref[rs, cols] = u[rs, cols] * (_dot(w, vb[rs]) + bias)
    o_ref[...] = _rms(acc_ref[...], og_ref[...]).astype(BF16)


def _gmlp_call(proj, ln_g, ln_b, w_s, b_s, og, rows, chunk, first_block, n_blocks, emit_vn, prev=None):
    const2 = lambda i: (0, 0)
    in_specs = [pl.BlockSpec((rows, GM_WIDTH), lambda i: (first_block + i, 1)),
                pl.BlockSpec((rows, GM_WIDTH), lambda i: (first_block + i, 2)),
                pl.BlockSpec((1, GM_WIDTH), const2),
                pl.BlockSpec((1, GM_WIDTH), const2),
                pl.BlockSpec((GM_HEADS, chunk, chunk), lambda i: (0, 0, 0)),
                pl.BlockSpec((chunk, GM_HEADS), const2),
                pl.BlockSpec((1, GM_WIDTH), const2)]
    args = [proj, proj, ln_g.reshape(1, GM_WIDTH), ln_b.reshape(1, GM_WIDTH),
            w_s[:, :chunk, :chunk], b_s[:, :chunk].T, og.reshape(1, GM_WIDTH)]
    out_specs = [pl.BlockSpec((rows, GM_WIDTH), lambda i: (first_block + i, 0))]
    out_shape = [jax.ShapeDtypeStruct((N_TOK, GM_WIDTH), BF16)]
    if emit_vn:
        out_specs.append(pl.BlockSpec((rows, GM_WIDTH), lambda i: (i, 0)))
        out_shape.append(jax.ShapeDtypeStruct((n_blocks * rows, GM_WIDTH), F32))
    kern = functools.partial(_gmlp_kernel, chunk=chunk, emit_vn=emit_vn)
    aliases = {}
    if prev is not None:
        in_specs.append(pl.BlockSpec(memory_space=pl.ANY))
        args.append(prev)
        aliases = {len(args) - 1: 0}
        kern = _drop_last_input(kern, len(args))
    return pl.pallas_call(
        kern,
        grid=(n_blocks,),
        in_specs=in_specs,
        out_specs=out_specs,
        out_shape=out_shape,
        scratch_shapes=[pltpu.VMEM((rows, GM_WIDTH), F32)],
        input_output_aliases=aliases,
        compiler_params=_params("parallel"),
        name="gmlp_vn" if emit_vn else "gmlp",
    )(*args)


def _xattn_kernel(x_ref, g_ref, wq_ref, k_ref, v_ref, wo_ref, o_ref, att_ref, *, n_batch):
    rows = x_ref.shape[0]
    tb = rows // n_batch
    x = x_ref[...]
    q = _dot(_rms(x, g_ref[...]).astype(BF16), wq_ref[...]).astype(BF16)
    scale = XA_HEAD_DIM ** -0.5
    for b in range(n_batch):
        rs = slice(b * tb, (b + 1) * tb)
        for h in range(XA_HEADS):
            cols = slice(h * XA_HEAD_DIM, (h + 1) * XA_HEAD_DIM)
            s = lax.dot_general(q[rs, cols], k_ref[b, :, cols], (((1,), (1,)), ((), ())),
                                preferred_element_type=F32) * scale
            e = jnp.exp(s - jnp.max(s, axis=-1, keepdims=True))
            p = e / jnp.sum(e, axis=-1, keepdims=True)
            att_ref[rs, cols] = _dot(p.astype(BF16), v_ref[b, :, cols]).astype(BF16)
    o_ref[...] = x + _dot(att_ref[...], wo_ref[...])


def _xattn_call(x, g, wq, k, v, wo, rows, n_batch, first_block, n_blocks, blocks_per_kv, prev=None):
    const2 = lambda i: (0, 0)
    in_specs = [pl.BlockSpec((rows, D_MODEL), lambda i: (first_block + i, 0)),
                pl.BlockSpec((1, D_MODEL), const2),
                pl.BlockSpec((D_MODEL, D_MODEL), const2),
                pl.BlockSpec((n_batch, N_MEM, D_MODEL), lambda i: (i // blocks_per_kv, 0, 0)),
                pl.BlockSpec((n_batch, N_MEM, D_MODEL), lambda i: (i // blocks_per_kv, 0, 0)),
                pl.BlockSpec((D_MODEL, D_MODEL), const2)]
    args = [x, g.reshape(1, D_MODEL), wq, k, v, wo]
    kern = functools.partial(_xattn_kernel, n_batch=n_batch)
    aliases = {}
    if prev is not None:
        in_specs.append(pl.BlockSpec(memory_space=pl.ANY))
        args.append(prev)
        aliases = {len(args) - 1: 0}
        kern = _drop_last_input(kern, len(args))
    return pl.pallas_call(
        kern,
        grid=(n_blocks,),
        in_specs=in_specs,
        out_specs=pl.BlockSpec((rows, D_MODEL), lambda i: (first_block + i, 0)),
        out_shape=jax.ShapeDtypeStruct((N_TOK, D_MODEL), F32),
        scratch_shapes=[pltpu.VMEM((rows, D_MODEL), BF16)],
        input_output_aliases=aliases,
        compiler_params=_params("parallel"),
        name="xattn",
    )(*args)


def _moe_kernel(be_ref, nu_ref, first_ref, slot_ref, nxt_ref,
                x_ref, wg_hbm, wu_hbm, wd_hbm, bg_ref, bu_ref, bd_ref, o_ref,
                wbuf_ref, wgb_ref, wub_ref, wdb_ref, sem, *, layer):
    i = pl.program_id(0)
    active = i < nu_ref[0]

    def copies(e, s):
        return [pltpu.make_async_copy(w.at[layer, e], wbuf_ref.at[s, j], sem.at[s, j])
                for j, w in enumerate((wg_hbm, wu_hbm, wd_hbm))]

    @pl.when(jnp.logical_and(active, first_ref[i] == 1))
    def _():
        s = slot_ref[i]

        @pl.when(i == 0)
        def _():
            for c in copies(be_ref[0], s):
                c.start()

        @pl.when(nxt_ref[i] >= 0)
        def _():
            for c in copies(nxt_ref[i], 1 - s):
                c.start()

        for c in copies(be_ref[i], s):
            c.wait()

        def cast(r, _):
            rs = pl.ds(pl.multiple_of(r * MOE_CAST_ROWS, MOE_CAST_ROWS), MOE_CAST_ROWS)
            wgb_ref[rs, :] = wbuf_ref[s, 0, rs, :].astype(BF16)
            wub_ref[rs, :] = wbuf_ref[s, 1, rs, :].astype(BF16)
            wdb_ref[rs, :] = wbuf_ref[s, 2, rs, :].astype(BF16)
            return 0
        lax.fori_loop(0, D_MODEL // MOE_CAST_ROWS, cast, 0)

    @pl.when(active)
    def _():
        x = jnp.concatenate([x_ref[:, c].reshape(MOE_ROWS, LANES) for c in range(PIECES_PER_ROW)],
                            axis=1).astype(BF16)
        g = jnp.minimum(_dot(x, wgb_ref[...]) + bg_ref[...], SWIGLU_LIMIT)
        u = jnp.clip(_dot(x, wub_ref[...]) + bu_ref[...], -SWIGLU_LIMIT, SWIGLU_LIMIT)
        a = g * jax.nn.sigmoid(SWIGLU_ALPHA * g) * (u + 1.0)
        o_ref[...] = _dot(a.astype(BF16), wdb_ref[...]) + bd_ref[...]


def _moe_call(layer, xs, sched, w_gate, b_gate, w_up, b_up, w_down, b_down):
    def row_map(i, be, nu, *_):
        return (jnp.minimum(i, nu[0] - 1), 0)

    def piece_map(i, be, nu, *_):
        return (jnp.minimum(i, nu[0] - 1), 0, 0, 0)

    def b_map(i, be, *_):
        return (layer, be[i], 0, 0)

    b_spec = pl.BlockSpec((None, None, 1, D_FF), b_map)
    w_spec = pl.BlockSpec(memory_space=pl.ANY)
    grid_spec = pltpu.PrefetchScalarGridSpec(
        num_scalar_prefetch=5,
        grid=(MOE_BLOCKS,),
        in_specs=[pl.BlockSpec((MOE_ROWS // SUBLANES,) + PIECE_TILE, piece_map),
                  w_spec, w_spec, w_spec, b_spec, b_spec, b_spec],
        out_specs=pl.BlockSpec((MOE_ROWS, D_MODEL), row_map),
        scratch_shapes=[pltpu.VMEM((2, 3, D_MODEL, D_FF), F32)]
        + [pltpu.VMEM((D_MODEL, D_FF), BF16)] * 3
        + [pltpu.SemaphoreType.DMA((2, 3))],
    )
    shape4 = (DEPTH, N_EXPERTS, 1, D_FF)
    return pl.pallas_call(
        functools.partial(_moe_kernel, layer=layer),
        grid_spec=grid_spec,
        out_shape=jax.ShapeDtypeStruct((MOE_BLOCKS * MOE_ROWS, D_MODEL), F32),
        compiler_params=pltpu.CompilerParams(dimension_semantics=("arbitrary",),
                                             vmem_limit_bytes=56 * 1024 * 1024),
        name="moe_experts",
    )(*sched, xs, w_gate, w_up, w_down, b_gate.reshape(shape4), b_up.reshape(shape4),
      b_down.reshape(shape4))


DISPATCH_TOKENS = 48
DMA_PIECES = 128
DISPATCH_DMAS = DISPATCH_TOKENS * PIECES_PER_ROW // DMA_PIECES
DISPATCH_CHUNKS = N_TOK // SC_WORKERS // DISPATCH_TOKENS
IDX_ROWS = 16
MOE_ROWS_TOTAL = MOE_BLOCKS * MOE_ROWS


def _piece_rows(dest_t):
    r = dest_t.reshape(TOP_K, N_TOK // SUBLANES, 1, SUBLANES)
    c = jnp.arange(PIECES_PER_ROW, dtype=jnp.int32).reshape(1, 1, PIECES_PER_ROW, 1)
    return (r // SUBLANES) * (PIECES_PER_ROW * SUBLANES) + c * SUBLANES + r % SUBLANES


def _dispatch(h4, dest_t):
    n_idx = TOP_K * DISPATCH_DMAS
    idx = _piece_rows(dest_t).reshape(TOP_K, SC_WORKERS, DISPATCH_CHUNKS, DISPATCH_DMAS, DMA_PIECES)
    idx = idx.transpose(1, 2, 0, 3, 4).reshape(SC_WORKERS * DISPATCH_CHUNKS, n_idx, DMA_PIECES)
    idx = jnp.pad(idx, ((0, 0), (0, IDX_ROWS - n_idx), (0, 0))).reshape(-1, DMA_PIECES)
    chunk_pieces = DISPATCH_TOKENS * PIECES_PER_ROW
    mesh = plsc.VectorSubcoreMesh(core_axis_name="c", subcore_axis_name="s")

    @functools.partial(
        pl.kernel, mesh=mesh,
        out_type=jax.ShapeDtypeStruct((MOE_ROWS_TOTAL * PIECES_PER_ROW, LANES), F32),
        scratch_types=[pltpu.VMEM((chunk_pieces, LANES), F32), pltpu.VMEM((IDX_ROWS, DMA_PIECES), jnp.int32),
                       pltpu.SemaphoreType.DMA],
    )
    def scatter_rows(h_hbm, idx_hbm, o_hbm, rows_v, idx_v, sem):
        wid = lax.axis_index("s") * SC_CORES + lax.axis_index("c")

        @pl.loop(0, DISPATCH_CHUNKS)
        def _(j):
            q = wid * DISPATCH_CHUNKS + j
            pltpu.sync_copy(h_hbm.at[pl.ds(pl.multiple_of(q * chunk_pieces, SUBLANES), chunk_pieces)], rows_v)
            pltpu.sync_copy(idx_hbm.at[pl.ds(pl.multiple_of(q * IDX_ROWS, IDX_ROWS), IDX_ROWS)], idx_v)
            copies = [pltpu.async_copy(rows_v.at[pl.ds(m * DMA_PIECES, DMA_PIECES)],
                                       o_hbm.at[idx_v.at[k * DISPATCH_DMAS + m]], sem)
                      for k in range(TOP_K) for m in range(DISPATCH_DMAS)]
            for cp in copies:
                cp.wait()

    xs = scatter_rows(h4.reshape(N_TOK * PIECES_PER_ROW, LANES), idx)
    return xs.reshape((MOE_ROWS_TOTAL // SUBLANES,) + PIECE_TILE)


def _route(e_t, cnt):
    blk_cnt = cnt[:, :, 0].astype(jnp.int32)
    counts = jnp.sum(blk_cnt, axis=0)
    padded = ((counts + MOE_ROWS - 1) // MOE_ROWS) * MOE_ROWS
    pend = jnp.cumsum(padded)
    pstart = pend - padded
    base = pstart[None, :] + jnp.cumsum(blk_cnt, axis=0) - blk_cnt
    dest_t = _dest(e_t, base.astype(F32)[:, :, None])
    n_used = (pend[-1] // MOE_ROWS).astype(jnp.int32)
    blk = jnp.minimum(jnp.arange(MOE_BLOCKS, dtype=jnp.int32), n_used - 1) * MOE_ROWS
    blk_exp = jnp.minimum(jnp.sum((pend[None, :] <= blk[:, None]).astype(jnp.int32), axis=1), N_EXPERTS - 1)
    ids = jnp.arange(MOE_BLOCKS, dtype=jnp.int32)
    first = jnp.logical_and(ids < n_used, jnp.logical_or(ids == 0, blk_exp != jnp.roll(blk_exp, 1)))
    first = first.astype(jnp.int32)
    slot = (jnp.cumsum(first) - 1) % 2
    eid = jnp.arange(N_EXPERTS, dtype=jnp.int32)
    later = jnp.logical_and(eid[None, :] > eid[:, None], (counts > 0)[None, :])
    nxt_e = jnp.min(jnp.where(later, eid[None, :], N_EXPERTS), axis=1)
    nxt = jnp.where(nxt_e < N_EXPERTS, nxt_e, -1)[blk_exp].astype(jnp.int32)
    return dest_t, (blk_exp, n_used.reshape(1), first, slot.astype(jnp.int32), nxt)


def kernel(x_prompt, x_sample, mem_prompt, state_s5_re, state_s5_im, cache_mem_k, cache_mem_v, norm_mix_g, w_in, s5_lambda_re, s5_lambda_im, s5_b_re, s5_b_im, s5_c_re, s5_c_im, s5_d, s5_log_step, s5_w_glu, gm_ln_g, gm_ln_b, gm_w_s, gm_b_s, out_norm_g, w_out, xa_norm_g, mem_norm_g, w_xq, w_xk, w_xv, w_xo, ffn_norm_g, router_w, router_b, e_w_gate, e_b_gate, e_w_up, e_b_up, e_w_down, e_b_down, final_norm_g):
    xs = (x_prompt.reshape(N_PROMPT, D_MODEL), x_sample.reshape(N_SAMPLE, D_MODEL))
    mem = mem_prompt.reshape(BATCH * N_MEM, D_MODEL)
    outs = {k: [] for k in ('s5r_p', 's5i_p', 'mk_p', 'mv_p', 's5r_s', 's5i_s', 'gmv_s')}
    for l in range(DEPTH):
        proj = _norm_matmul(xs, norm_mix_g[l], w_in[l].astype(BF16))
        sw = _s5_weights(s5_lambda_re[l], s5_lambda_im[l], s5_b_re[l], s5_b_im[l], s5_c_re[l], s5_c_im[l],
                         s5_d[l], s5_log_step[l], S5_SEG_PROMPT)
        w_glu = s5_w_glu[l].astype(BF16)
        og = out_norm_g[l]
        ms5, er_p, ei_p = _s5_call(proj, sw, w_glu, og[:S5_WIDTH], S5_SEG_PROMPT, True)
        h0 = (state_s5_re[l].reshape(DEC_BATCH // SUBLANES, SUBLANES, S5_COLS),
              state_s5_im[l].reshape(DEC_BATCH // SUBLANES, SUBLANES, S5_COLS))
        ms5, er_s, ei_s = _s5_call(proj, sw, w_glu, og[:S5_WIDTH], DEC_SEQ, False, h0=h0, prev=ms5)
        outs['s5r_p'].append(er_p[:, SUBLANES - 1].reshape(BATCH, S5_GROUPS, S5_STATE))
        outs['s5i_p'].append(ei_p[:, SUBLANES - 1].reshape(BATCH, S5_GROUPS, S5_STATE))
        outs['s5r_s'].append(er_s.reshape(DEC_BATCH, S5_GROUPS, S5_STATE))
        outs['s5i_s'].append(ei_s.reshape(DEC_BATCH, S5_GROUPS, S5_STATE))

        (mgm,) = _gmlp_call(proj, gm_ln_g[l], gm_ln_b[l], gm_w_s[l], gm_b_s[l], og[S5_WIDTH:],
                            GM_ROWS_PROMPT, GM_CHUNK, 0, N_PROMPT // GM_ROWS_PROMPT, False)
        mgm, vn = _gmlp_call(proj, gm_ln_g[l], gm_ln_b[l], gm_w_s[l], gm_b_s[l], og[S5_WIDTH:],
                             N_SAMPLE, DEC_SEQ, N_PROMPT // N_SAMPLE, 1, True, prev=mgm)
        outs['gmv_s'].append(vn.reshape(DEC_BATCH, DEC_SEQ, GM_HEADS, GM_HEAD_DIM))
        x = _out_proj(xs, ms5, mgm, w_out[l].astype(BF16))

        mk = _norm_matmul((mem,), mem_norm_g[l], w_xk[l].astype(BF16))
        mv = _norm_matmul((mem,), mem_norm_g[l], w_xv[l].astype(BF16))
        outs['mk_p'].append(mk.reshape(BATCH, N_MEM, XA_HEADS, XA_HEAD_DIM))
        outs['mv_p'].append(mv.reshape(BATCH, N_MEM, XA_HEADS, XA_HEAD_DIM))
        wq = w_xq[l].astype(BF16)
        wo = w_xo[l].astype(BF16)
        xa = _xattn_call(x, xa_norm_g[l], wq, mk.reshape(BATCH, N_MEM, D_MODEL).astype(BF16),
                         mv.reshape(BATCH, N_MEM, D_MODEL).astype(BF16), wo,
                         XA_ROWS_PROMPT, 1, 0, N_PROMPT // XA_ROWS_PROMPT, SEQ // XA_ROWS_PROMPT)
        rows_s = XA_SAMPLE_BATCHES * DEC_SEQ
        x = _xattn_call(x, xa_norm_g[l], wq,
                        cache_mem_k[l].reshape(DEC_BATCH, N_MEM, D_MODEL).astype(BF16),
                        cache_mem_v[l].reshape(DEC_BATCH, N_MEM, D_MODEL).astype(BF16), wo,
                        rows_s, XA_SAMPLE_BATCHES, N_PROMPT // rows_s, N_SAMPLE // rows_s, 1, prev=xa)

        h, e_t, gates_t, cnt = _router(x, ffn_norm_g[l], router_w[l], router_b[l])
        dest_t, sched = _route(e_t, cnt)
        ys = _moe_call(l, _dispatch(h, dest_t), sched, e_w_gate, e_b_gate, e_w_up, e_b_up, e_w_down, e_b_down)
        x = _combine(x, ys[dest_t.reshape(-1)], gates_t.T, final_norm_g if l == DEPTH - 1 else None)
        xs = (x,)

    y_prompt = x[0].reshape(BATCH, SEQ, D_MODEL)
    y_sample = x[1].reshape(DEC_BATCH, DEC_SEQ, D_MODEL)
    st = jnp.stack
    return (y_prompt, y_sample, st(outs['s5r_p']), st(outs['s5i_p']), st(outs['mk_p']), st(outs['mv_p']),
            st(outs['s5r_s']), st(outs['s5i_s']), st(outs['gmv_s']))
```

```python
import functools

import jax
import jax.numpy as jnp
from jax import lax
from jax.experimental import pallas as pl
from jax.experimental.pallas import tpu as pltpu
from jax.experimental.pallas import tpu_sc as plsc

F32 = jnp.float32
BF16 = jnp.bfloat16

D_MODEL = 1024
BATCH = 4
SEQ = 4096
DEPTH = 2
DEC_BATCH = 16
DEC_SEQ = 32
S5_GROUPS = 32
S5_GROUP_CH = 16
S5_STATE = 64
S5_WIDTH = S5_GROUPS * S5_GROUP_CH
S5_COLS = S5_GROUPS * S5_STATE
GM_HEADS = 4
GM_HEAD_DIM = 128
GM_WIDTH = GM_HEADS * GM_HEAD_DIM
GM_CHUNK = 128
IN_PROJ = S5_WIDTH + 2 * GM_WIDTH
N_MEM = 256
XA_HEADS = 4
XA_HEAD_DIM = D_MODEL // XA_HEADS
N_EXPERTS = 32
TOP_K = 4
D_FF = D_MODEL
SWIGLU_LIMIT = 7.0
SWIGLU_ALPHA = 1.702
EPS = 1e-6
EIG_CLIP = -1e-4

N_PROMPT = BATCH * SEQ
N_SAMPLE = DEC_BATCH * DEC_SEQ
N_TOK = N_PROMPT + N_SAMPLE

SUBLANES = 8
LANES = 128
MXU_DIM = 256
VMEM_LIMIT = 48 * 1024 * 1024
SC_CORES = 2
SC_SUBCORES = 16
SC_WORKERS = SC_CORES * SC_SUBCORES

PIECE_TILE = (D_MODEL // LANES, SUBLANES, LANES)
PIECES_PER_ROW = D_MODEL // LANES

ROW_BLOCK = 512
S5_SEG_PROMPT = 64
S5_COL_BLOCK = 512
GM_ROWS_PROMPT = 256
XA_ROWS_PROMPT = 512
XA_SAMPLE_BATCHES = 4
MOE_ROWS = 256
MOE_BLOCKS = -(-N_TOK * TOP_K // MOE_ROWS) + N_EXPERTS
MOE_CAST_ROWS = 128


def _params(*sem):
    return pltpu.CompilerParams(dimension_semantics=sem, vmem_limit_bytes=VMEM_LIMIT)


def _rms(x, g):
    return x * lax.rsqrt(jnp.mean(x * x, axis=-1, keepdims=True) + EPS) * g


def _dot(a, b):
    return jnp.dot(a, b, preferred_element_type=F32)


PROMPT_BLOCKS = N_PROMPT // ROW_BLOCK
SAMPLE_BLOCKS = N_SAMPLE // ROW_BLOCK


def _token_sources(xs):
    if len(xs) == 1:
        return [pl.BlockSpec((ROW_BLOCK, D_MODEL), lambda i: (i, 0))]
    return [pl.BlockSpec((ROW_BLOCK, D_MODEL), lambda i: (jnp.minimum(i, PROMPT_BLOCKS - 1), 0)),
            pl.BlockSpec((ROW_BLOCK, D_MODEL), lambda i: (jnp.maximum(i - PROMPT_BLOCKS, 0), 0))]


def _for_group(refs, body):
    if len(refs) == 1:
        body(refs[0])
        return
    i = pl.program_id(0)
    pl.when(i < PROMPT_BLOCKS)(lambda: body(refs[0]))
    pl.when(i >= PROMPT_BLOCKS)(lambda: body(refs[1]))


def _norm_matmul_kernel(*refs, n_src):
    g_ref, w_ref, o_ref = refs[n_src:]

    def body(x_ref):
        o_ref[...] = _dot(_rms(x_ref[...], g_ref[...]).astype(BF16), w_ref[...])
    _for_group(refs[:n_src], body)


def _norm_matmul(xs, g, w):
    d = D_MODEL
    n = sum(x.shape[0] for x in xs)
    dout = w.shape[1]
    return pl.pallas_call(
        functools.partial(_norm_matmul_kernel, n_src=len(xs)),
        grid=(n // ROW_BLOCK,),
        in_specs=_token_sources(xs) + [pl.BlockSpec((1, d), lambda i: (0, 0)),
                                       pl.BlockSpec((d, dout), lambda i: (0, 0))],
        out_specs=pl.BlockSpec((ROW_BLOCK, dout), lambda i: (i, 0)),
        out_shape=jax.ShapeDtypeStruct((n, dout), F32),
        compiler_params=_params("parallel"),
        name="norm_matmul",
    )(*xs, g.reshape(1, d), w)


def _out_proj_kernel(*refs, n_src):
    a_ref, b_ref, wa_ref, wb_ref, o_ref = refs[n_src:]

    def body(x_ref):
        o_ref[...] = x_ref[...] + _dot(a_ref[...], wa_ref[...]) + _dot(b_ref[...], wb_ref[...])
    _for_group(refs[:n_src], body)


def _out_proj(xs, a, b, w):
    d = D_MODEL
    n = a.shape[0]
    wa, wb = w[:S5_WIDTH], w[S5_WIDTH:]
    rows = ROW_BLOCK
    return pl.pallas_call(
        functools.partial(_out_proj_kernel, n_src=len(xs)),
        grid=(n // rows,),
        in_specs=_token_sources(xs) + [pl.BlockSpec((rows, S5_WIDTH), lambda i: (i, 0)),
                                       pl.BlockSpec((rows, GM_WIDTH), lambda i: (i, 0)),
                                       pl.BlockSpec((S5_WIDTH, d), lambda i: (0, 0)),
                                       pl.BlockSpec((GM_WIDTH, d), lambda i: (0, 0))],
        out_specs=pl.BlockSpec((rows, d), lambda i: (i, 0)),
        out_shape=jax.ShapeDtypeStruct((n, d), F32),
        compiler_params=_params("parallel"),
        name="out_proj",
    )(*xs, a, b, wa, wb)


def _router_kernel(x_ref, g_ref, wt_ref, b_ref, h_ref, e_ref, gate_ref, cnt_ref):
    h = _rms(x_ref[...], g_ref[...])
    for c in range(D_MODEL // LANES):
        h_ref[:, c] = h[:, c * LANES:(c + 1) * LANES].reshape(h.shape[0] // SUBLANES, SUBLANES, LANES)
    logits = lax.dot_general(wt_ref[...], h.astype(BF16), (((1,), (1,)), ((), ())),
                             preferred_element_type=F32) + b_ref[...]
    sub = lax.broadcasted_iota(jnp.int32, logits.shape, 0)
    work = logits
    chosen = jnp.zeros(logits.shape, F32)
    vals, idxs = [], []
    for _ in range(TOP_K):
        m = jnp.max(work, axis=0, keepdims=True)
        idx = jnp.min(jnp.where(work == m, sub, N_EXPERTS), axis=0, keepdims=True)
        sel = sub == idx
        vals.append(m)
        idxs.append(idx)
        work = jnp.where(sel, -jnp.inf, work)
        chosen = jnp.where(sel, 1.0, chosen)
    ex = [jnp.exp(v - vals[0]) for v in vals]
    den = ex[0] + ex[1] + ex[2] + ex[3]
    gate_ref[...] = jnp.concatenate([e / den for e in ex], axis=0)
    e_ref[...] = jnp.concatenate(idxs, axis=0)
    cnt_ref[...] = jnp.broadcast_to(jnp.sum(chosen, axis=1, keepdims=True), (N_EXPERTS, LANES))


def _router(x, g, w, b):
    n, d = x.shape
    rows = ROW_BLOCK
    return pl.pallas_call(
        _router_kernel,
        grid=(n // rows,),
        in_specs=[pl.BlockSpec((rows, d), lambda i: (i, 0)),
                  pl.BlockSpec((1, d), lambda i: (0, 0)),
                  pl.BlockSpec((N_EXPERTS, d), lambda i: (0, 0)),
                  pl.BlockSpec((N_EXPERTS, 1), lambda i: (0, 0))],
        out_specs=[pl.BlockSpec((rows // SUBLANES,) + PIECE_TILE, lambda i: (i, 0, 0, 0)),
                   pl.BlockSpec((TOP_K, rows), lambda i: (0, i)),
                   pl.BlockSpec((TOP_K, rows), lambda i: (0, i)),
                   pl.BlockSpec((None, N_EXPERTS, LANES), lambda i: (i, 0, 0))],
        out_shape=[jax.ShapeDtypeStruct((n // SUBLANES,) + PIECE_TILE, F32),
                   jax.ShapeDtypeStruct((TOP_K, n), jnp.int32),
                   jax.ShapeDtypeStruct((TOP_K, n), F32),
                   jax.ShapeDtypeStruct((n // rows, N_EXPERTS, LANES), F32)],
        compiler_params=_params("parallel"),
        name="router",
    )(x, g.reshape(1, d), w.T.astype(BF16), b.reshape(N_EXPERTS, 1))


def _dest_kernel(e_ref, base_ref, tri_ref, d_ref):
    e = e_ref[...]
    sub = lax.broadcasted_iota(jnp.int32, (N_EXPERTS, e.shape[1]), 0)
    hot = [sub == e[k:k + 1, :] for k in range(TOP_K)]
    chosen = sum(jnp.where(hk, 1.0, 0.0) for hk in hot)
    pos = base_ref[...] + _dot(chosen.astype(BF16), tri_ref[...])
    d_ref[...] = jnp.concatenate(
        [jnp.sum(jnp.where(hk, pos, 0.0), axis=0, keepdims=True) for hk in hot], axis=0).astype(jnp.int32)


def _dest(e_t, base):
    n = e_t.shape[1]
    rows = ROW_BLOCK
    tri = jnp.triu(jnp.ones((rows, rows), F32), 1).astype(BF16)
    return pl.pallas_call(
        _dest_kernel,
        grid=(n // rows,),
        in_specs=[pl.BlockSpec((TOP_K, rows), lambda i: (0, i)),
                  pl.BlockSpec((None, N_EXPERTS, 1), lambda i: (i, 0, 0)),
                  pl.BlockSpec((rows, rows), lambda i: (0, 0))],
        out_specs=pl.BlockSpec((TOP_K, rows), lambda i: (0, i)),
        out_shape=jax.ShapeDtypeStruct((TOP_K, n), jnp.int32),
        compiler_params=_params("parallel"),
        name="moe_dest",
    )(e_t, base, tri)


def _combine_kernel(*refs, final):
    if final:
        x_ref, g0, g1, g2, g3, gate_ref, fg_ref, op_ref, os_ref = refs
    else:
        x_ref, g0, g1, g2, g3, gate_ref, o_ref = refs
    acc = x_ref[...]
    for k, g in enumerate((g0, g1, g2, g3)):
        acc = acc + gate_ref[:, k:k + 1] * g[...]
    if final:
        def store(o_ref):
            o_ref[...] = _rms(acc, fg_ref[...])
        _for_group((op_ref, os_ref), store)
    else:
        o_ref[...] = acc


def _combine(x, picked, gates, final_g=None):
    n, d = x.shape
    rows = ROW_BLOCK
    picked = picked.reshape(TOP_K, n, d)
    final = final_g is not None
    in_specs = [pl.BlockSpec((rows, d), lambda i: (i, 0))]
    in_specs += [pl.BlockSpec((None, rows, d), functools.partial(lambda i, k: (k, i, 0), k=k))
                 for k in range(TOP_K)]
    in_specs.append(pl.BlockSpec((rows, TOP_K), lambda i: (i, 0)))
    args = [x, picked, picked, picked, picked, gates]
    out_specs = pl.BlockSpec((rows, d), lambda i: (i, 0))
    out_shape = jax.ShapeDtypeStruct((n, d), F32)
    if final:
        in_specs.append(pl.BlockSpec((1, d), lambda i: (0, 0)))
        args.append(final_g.reshape(1, d))
        out_specs = _token_sources((None, None))
        out_shape = [jax.ShapeDtypeStruct((N_PROMPT, d), F32), jax.ShapeDtypeStruct((N_SAMPLE, d), F32)]
    return pl.pallas_call(
        functools.partial(_combine_kernel, final=final),
        grid=(n // rows,),
        in_specs=in_specs,
        out_specs=out_specs,
        out_shape=out_shape,
        compiler_params=_params("arbitrary" if final else "parallel"),
        name="moe_combine",
    )(*args)


def _cmul_add(ar, ai, hr, hi, br, bi):
    return ar * hr - ai * hi + br, ar * hi + ai * hr + bi


def _s5_kernel(*refs, seg, chained):
    if chained:
        (u_ref, wbr_ref, wbi_ref, wcr_ref, wci_ref, ar_ref, ai_ref, pr_ref, pi_ref, d_ref,
         wglu_ref, og_ref, o_ref, er_ref, ei_ref,
         up_ref, hr_ref, hi_ref, hbr_ref, hbi_ref, sr_ref, si_ref, op_ref, cr_ref, ci_ref) = refs
    else:
        (u_ref, wbr_ref, wbi_ref, wcr_ref, wci_ref, ar_ref, ai_ref, pr_ref, pi_ref, d_ref,
         wglu_ref, og_ref, h0r_ref, h0i_ref, o_ref, er_ref, ei_ref,
         up_ref, hr_ref, hi_ref, hbr_ref, hbi_ref, sr_ref, si_ref, op_ref) = refs
    rows = seg * SUBLANES
    half = S5_COLS // 2

    n_lane_blk = S5_WIDTH // LANES
    for j in range(SUBLANES):
        for i0 in range(0, seg, SUBLANES):
            for c in range(n_lane_blk):
                up_ref[c, pl.ds(i0 * SUBLANES + j, SUBLANES, stride=SUBLANES), :] = (
                    u_ref[j * seg + i0:j * seg + i0 + SUBLANES, c * LANES:(c + 1) * LANES])

    up = jnp.concatenate([up_ref[c] for c in range(n_lane_blk)], axis=1)
    ub = up.astype(BF16)
    for k in range(2):
        uk = ub[:, k * MXU_DIM:(k + 1) * MXU_DIM]
        hr_ref[:, k * half:(k + 1) * half] = _dot(uk, wbr_ref[k])
        hi_ref[:, k * half:(k + 1) * half] = _dot(uk, wbi_ref[k])

    for cb in range(S5_COLS // S5_COL_BLOCK):
        cols = slice(cb * S5_COL_BLOCK, (cb + 1) * S5_COL_BLOCK)
        ar = jnp.broadcast_to(ar_ref[:, cols], (SUBLANES, S5_COL_BLOCK))
        ai = jnp.broadcast_to(ai_ref[:, cols], (SUBLANES, S5_COL_BLOCK))

        def step(i, carry, cols=cols, ar=ar, ai=ai):
            r = pl.multiple_of(i * SUBLANES, SUBLANES)
            h_r, h_i = _cmul_add(ar, ai, carry[0], carry[1],
                                 hr_ref[pl.ds(r, SUBLANES), cols], hi_ref[pl.ds(r, SUBLANES), cols])
            hr_ref[pl.ds(r, SUBLANES), cols] = h_r
            hi_ref[pl.ds(r, SUBLANES), cols] = h_i
            return h_r, h_i

        zero = jnp.zeros((SUBLANES, S5_COL_BLOCK), F32)
        lax.fori_loop(0, seg, step, (zero, zero), unroll=2)

    end_r = hr_ref[rows - SUBLANES:rows, :]
    end_i = hi_ref[rows - SUBLANES:rows, :]
    as_r = pr_ref[seg - 1:seg, :]
    as_i = pi_ref[seg - 1:seg, :]
    if chained:
        @pl.when(pl.program_id(1) == 0)
        def _():
            cr_ref[...] = jnp.zeros_like(cr_ref)
            ci_ref[...] = jnp.zeros_like(ci_ref)

        cur_r, cur_i = cr_ref[...], ci_ref[...]
        for j in range(SUBLANES):
            sr_ref[j:j + 1, :] = cur_r
            si_ref[j:j + 1, :] = cur_i
            cur_r, cur_i = _cmul_add(as_r, as_i, cur_r, cur_i, end_r[j:j + 1], end_i[j:j + 1])
        cr_ref[...] = cur_r
        ci_ref[...] = cur_i
    else:
        sr_ref[...] = h0r_ref[...]
        si_ref[...] = h0i_ref[...]
    t_r, t_i = _cmul_add(as_r, as_i, sr_ref[...], si_ref[...], end_r, end_i)
    er_ref[...] = t_r
    ei_ref[...] = t_i

    for cb in range(S5_COLS // S5_COL_BLOCK):
        cols = slice(cb * S5_COL_BLOCK, (cb + 1) * S5_COL_BLOCK)
        s_r = jnp.concatenate([sr_ref[:, cols]] * 2, axis=0)
        s_i = jnp.concatenate([si_ref[:, cols]] * 2, axis=0)

        def fix(i2, _, cols=cols, s_r=s_r, s_i=s_i):
            r = pl.multiple_of(i2 * 2 * SUBLANES, 2 * SUBLANES)
            shape = (SUBLANES, S5_COL_BLOCK)
            p_r = jnp.concatenate(
                [jnp.broadcast_to(pr_ref[pl.ds(2 * i2, 1), cols], shape),
                 jnp.broadcast_to(pr_ref[pl.ds(2 * i2 + 1, 1), cols], shape)], axis=0)
            p_i = jnp.concatenate(
                [jnp.broadcast_to(pi_ref[pl.ds(2 * i2, 1), cols], shape),
                 jnp.broadcast_to(pi_ref[pl.ds(2 * i2 + 1, 1), cols], shape)], axis=0)
            h_r, h_i = _cmul_add(p_r, p_i, s_r, s_i,
                                 hr_ref[pl.ds(r, 2 * SUBLANES), cols],
                                 hi_ref[pl.ds(r, 2 * SUBLANES), cols])
            hbr_ref[pl.ds(r, 2 * SUBLANES), cols] = h_r.astype(BF16)
            hbi_ref[pl.ds(r, 2 * SUBLANES), cols] = h_i.astype(BF16)
            return 0

        lax.fori_loop(0, seg // 2, fix, 0)

    slabs_per_window = MXU_DIM // (MXU_DIM // S5_STATE * S5_GROUP_CH)
    ys = []
    for w in range(S5_WIDTH // MXU_DIM):
        acc = d_ref[:, w * MXU_DIM:(w + 1) * MXU_DIM] * jnp.concatenate(
            [up_ref[c] for c in range(w * MXU_DIM // LANES, (w + 1) * MXU_DIM // LANES)], axis=1)
        for s in range(slabs_per_window):
            j = w * slabs_per_window + s
            acc += _dot(hbr_ref[:, j * MXU_DIM:(j + 1) * MXU_DIM], wcr_ref[j])
            acc += _dot(hbi_ref[:, j * MXU_DIM:(j + 1) * MXU_DIM], wci_ref[j])
        ys.append(jax.nn.gelu(acc).astype(BF16))

    z = _dot(jnp.concatenate(ys, axis=1), wglu_ref[...])
    o = _rms(z[:, :S5_WIDTH] * jax.nn.sigmoid(z[:, S5_WIDTH:]), og_ref[...])
    for c in range(n_lane_blk):
        op_ref[c] = o[:, c * LANES:(c + 1) * LANES]

    for j in range(SUBLANES):
        for i0 in range(0, seg, 2 * SUBLANES):
            for c in range(n_lane_blk):
                o_ref[j * seg + i0:j * seg + i0 + 2 * SUBLANES, c * LANES:(c + 1) * LANES] = (
                    op_ref[c, pl.ds(i0 * SUBLANES + j, 2 * SUBLANES, stride=SUBLANES), :].astype(BF16))


def _s5_weights(lam_re, lam_im, b_re, b_im, c_re, c_im, d, log_step, seg):
    step = jnp.exp(log_step)[:, None]
    lr = jnp.minimum(lam_re, EIG_CLIP)
    li = lam_im
    mag = jnp.exp(lr * step)
    ar = mag * jnp.cos(li * step)
    ai = mag * jnp.sin(li * step)
    den = lr * lr + li * li
    fr = ((ar - 1.0) * lr + ai * li) / den
    fi = (ai * lr - (ar - 1.0) * li) / den
    bbar_r = fr[..., None] * b_re - fi[..., None] * b_im
    bbar_i = fr[..., None] * b_im + fi[..., None] * b_re
    t = jnp.arange(1, seg + 1, dtype=F32)[:, None, None]
    pm = jnp.exp(lr * step * t)
    p_r = (pm * jnp.cos(li * step * t)).reshape(seg, S5_COLS)
    p_i = (pm * jnp.sin(li * step * t)).reshape(seg, S5_COLS)

    gk = MXU_DIM // S5_GROUP_CH

    def in_map(bbar):
        bb = bbar.reshape(S5_GROUPS // gk, gk, S5_STATE, S5_GROUP_CH)
        w = jnp.einsum('kgnh,gG->kghGn', bb, jnp.eye(gk, dtype=F32))
        return w.reshape(S5_GROUPS // gk, MXU_DIM, gk * S5_STATE).astype(BF16)

    gs = MXU_DIM // S5_STATE
    n_slab = S5_COLS // MXU_DIM
    per_win = MXU_DIM // (gs * S5_GROUP_CH)

    def out_map(c):
        cc = c.reshape(n_slab // per_win, per_win, gs, S5_GROUP_CH, S5_STATE)
        w = jnp.einsum('wsghn,sS,gG->wsgnSGh', cc, jnp.eye(per_win, dtype=F32), jnp.eye(gs, dtype=F32))
        return w.reshape(n_slab, MXU_DIM, MXU_DIM).astype(BF16)

    return dict(wbr=in_map(bbar_r), wbi=in_map(bbar_i), wcr=out_map(c_re), wci=out_map(-c_im),
                ar=ar.reshape(1, S5_COLS), ai=ai.reshape(1, S5_COLS), pr=p_r, pi=p_i,
                d=d.reshape(1, S5_WIDTH))


def _s5_call(proj, sw, w_glu, og, seg, chained, h0=None, prev=None):
    rows = seg * SUBLANES
    const2 = lambda *_: (0, 0)
    const3 = lambda *_: (0, 0, 0)
    if chained:
        n_chunk = SEQ // rows
        grid = (BATCH, n_chunk)
        row_map = lambda b, c: (b * n_chunk + c, 0)
        end_map = lambda b, c: (b, 0, 0)
        n_end = BATCH
        sem = ("parallel", "arbitrary")
    else:
        grid = (DEC_BATCH // SUBLANES,)
        row_map = lambda k: (N_PROMPT // rows + k, 0)
        end_map = lambda k: (k, 0, 0)
        n_end = DEC_BATCH // SUBLANES
        sem = ("parallel",)
    half = S5_COLS // 2
    in_specs = [pl.BlockSpec((rows, S5_WIDTH), row_map),
                pl.BlockSpec((2, MXU_DIM, half), const3),
                pl.BlockSpec((2, MXU_DIM, half), const3),
                pl.BlockSpec((S5_COLS // MXU_DIM, MXU_DIM, MXU_DIM), const3),
                pl.BlockSpec((S5_COLS // MXU_DIM, MXU_DIM, MXU_DIM), const3),
                pl.BlockSpec((1, S5_COLS), const2),
                pl.BlockSpec((1, S5_COLS), const2),
                pl.BlockSpec((seg, S5_COLS), const2),
                pl.BlockSpec((seg, S5_COLS), const2),
                pl.BlockSpec((1, S5_WIDTH), const2),
                pl.BlockSpec((S5_WIDTH, 2 * S5_WIDTH), const2),
                pl.BlockSpec((1, S5_WIDTH), const2)]
    args = [proj, sw['wbr'], sw['wbi'], sw['wcr'], sw['wci'], sw['ar'], sw['ai'],
            sw['pr'][:seg], sw['pi'][:seg], sw['d'], w_glu, og.reshape(1, S5_WIDTH)]
    scratch = [pltpu.VMEM((S5_WIDTH // LANES, rows, LANES), F32),
               pltpu.VMEM((rows, S5_COLS), F32), pltpu.VMEM((rows, S5_COLS), F32),
               pltpu.VMEM((rows, S5_COLS), BF16), pltpu.VMEM((rows, S5_COLS), BF16),
               pltpu.VMEM((SUBLANES, S5_COLS), F32), pltpu.VMEM((SUBLANES, S5_COLS), F32),
               pltpu.VMEM((S5_WIDTH // LANES, rows, LANES), F32)]
    aliases = {}
    if chained:
        scratch += [pltpu.VMEM((1, S5_COLS), F32), pltpu.VMEM((1, S5_COLS), F32)]
    else:
        in_specs += [pl.BlockSpec((None, SUBLANES, S5_COLS), end_map),
                     pl.BlockSpec((None, SUBLANES, S5_COLS), end_map)]
        args += [h0[0], h0[1]]
    if prev is not None:
        in_specs.append(pl.BlockSpec(memory_space=pl.ANY))
        args.append(prev)
        aliases = {len(args) - 1: 0}
    kern = functools.partial(_s5_kernel, seg=seg, chained=chained)
    if prev is not None:
        kern = _drop_last_input(kern, len(args))
    return pl.pallas_call(
        kern,
        grid=grid,
        in_specs=in_specs,
        out_specs=[pl.BlockSpec((rows, S5_WIDTH), row_map),
                   pl.BlockSpec((None, SUBLANES, S5_COLS), end_map),
                   pl.BlockSpec((None, SUBLANES, S5_COLS), end_map)],
        out_shape=[jax.ShapeDtypeStruct((N_TOK, S5_WIDTH), BF16),
                   jax.ShapeDtypeStruct((n_end, SUBLANES, S5_COLS), F32),
                   jax.ShapeDtypeStruct((n_end, SUBLANES, S5_COLS), F32)],
        scratch_shapes=scratch,
        input_output_aliases=aliases,
        compiler_params=_params(*sem),
        name="s5_chained" if chained else "s5_streams",
    )(*args)


def _drop_last_input(kern, n_in):
    def wrapped(*refs):
        return kern(*refs[:n_in - 1], *refs[n_in:])
    return wrapped


def _gmlp_kernel(*refs, chunk, emit_vn):
    if emit_vn:
        u_ref, v_ref, lng_ref, lnb_ref, ws_ref, bst_ref, og_ref, o_ref, vn_ref, acc_ref = refs
    else:
        u_ref, v_ref, lng_ref, lnb_ref, ws_ref, bst_ref, og_ref, o_ref, acc_ref = refs
    rows = u_ref.shape[0]
    u = jax.nn.gelu(u_ref[...])
    v = jax.nn.gelu(v_ref[...])
    tri = (lax.broadcasted_iota(jnp.int32, (chunk, chunk), 0)
           >= lax.broadcasted_iota(jnp.int32, (chunk, chunk), 1))
    for h in range(GM_HEADS):
        cols = slice(h * GM_HEAD_DIM, (h + 1) * GM_HEAD_DIM)
        vh = v[:, cols]
        mu = jnp.mean(vh, axis=-1, keepdims=True)
        cen = vh - mu
        var = jnp.mean(cen * cen, axis=-1, keepdims=True)
        vn = cen * lax.rsqrt(var + EPS) * lng_ref[:, cols] + lnb_ref[:, cols]
        if emit_vn:
            vn_ref[:, cols] = vn
        w = jnp.where(tri, ws_ref[h], 0.0).astype(BF16)
        vb = vn.astype(BF16)
        bias = bst_ref[:, h:h + 1]
        for c in range(rows // chunk):
            rs = slice(c * chunk, (c + 1) * chunk)
            acc_ref[rs, cols] = u[rs, cols] * (_dot(w, vb[rs]) + bias)
    o_ref[...] = _rms(acc_ref[...], og_ref[...]).astype(BF16)


def _gmlp_call(proj, ln_g, ln_b, w_s, b_s, og, rows, chunk, first_block, n_blocks, emit_vn, prev=None):
    const2 = lambda i: (0, 0)
    in_specs = [pl.BlockSpec((rows, GM_WIDTH), lambda i: (first_block + i, 1)),
                pl.BlockSpec((rows, GM_WIDTH), lambda i: (first_block + i, 2)),
                pl.BlockSpec((1, GM_WIDTH), const2),
                pl.BlockSpec((1, GM_WIDTH), const2),
                pl.BlockSpec((GM_HEADS, chunk, chunk), lambda i: (0, 0, 0)),
                pl.BlockSpec((chunk, GM_HEADS), const2),
                pl.BlockSpec((1, GM_WIDTH), const2)]
    args = [proj, proj, ln_g.reshape(1, GM_WIDTH), ln_b.reshape(1, GM_WIDTH),
            w_s[:, :chunk, :chunk], b_s[:, :chunk].T, og.reshape(1, GM_WIDTH)]
    out_specs = [pl.BlockSpec((rows, GM_WIDTH), lambda i: (first_block + i, 0))]
    out_shape = [jax.ShapeDtypeStruct((N_TOK, GM_WIDTH), BF16)]
    if emit_vn:
        out_specs.append(pl.BlockSpec((rows, GM_WIDTH), lambda i: (i, 0)))
        out_shape.append(jax.ShapeDtypeStruct((n_blocks * rows, GM_WIDTH), F32))
    kern = functools.partial(_gmlp_kernel, chunk=chunk, emit_vn=emit_vn)
    aliases = {}
    if prev is not None:
        in_specs.append(pl.BlockSpec(memory_space=pl.ANY))
        args.append(prev)
        aliases = {len(args) - 1: 0}
        kern = _drop_last_input(kern, len(args))
    return pl.pallas_call(
        kern,
        grid=(n_blocks,),
        in_specs=in_specs,
        out_specs=out_specs,
        out_shape=out_shape,
        scratch_shapes=[pltpu.VMEM((rows, GM_WIDTH), F32)],
        input_output_aliases=aliases,
        compiler_params=_params("parallel"),
        name="gmlp_vn" if emit_vn else "gmlp",
    )(*args)


def _xattn_kernel(x_ref, g_ref, wq_ref, k_ref, v_ref, wo_ref, o_ref, att_ref, *, n_batch):
    rows = x_ref.shape[0]
    tb = rows // n_batch
    x = x_ref[...]
    q = _dot(_rms(x, g_ref[...]).astype(BF16), wq_ref[...]).astype(BF16)
    scale = XA_HEAD_DIM ** -0.5
    for b in range(n_batch):
        rs = slice(b * tb, (b + 1) * tb)
        for h in range(XA_HEADS):
            cols = slice(h * XA_HEAD_DIM, (h + 1) * XA_HEAD_DIM)
            s = lax.dot_general(q[rs, cols], k_ref[b, :, cols], (((1,), (1,)), ((), ())),
                                preferred_element_type=F32) * scale
            e = jnp.exp(s - jnp.max(s, axis=-1, keepdims=True))
            p = e / jnp.sum(e, axis=-1, keepdims=True)
            att_ref[rs, cols] = _dot(p.astype(BF16), v_ref[b, :, cols]).astype(BF16)
    o_ref[...] = x + _dot(att_ref[...], wo_ref[...])


def _xattn_call(x, g, wq, k, v, wo, rows, n_batch, first_block, n_blocks, blocks_per_kv, prev=None):
    const2 = lambda i: (0, 0)
    in_specs = [pl.BlockSpec((rows, D_MODEL), lambda i: (first_block + i, 0)),
                pl.BlockSpec((1, D_MODEL), const2),
                pl.BlockSpec((D_MODEL, D_MODEL), const2),
                pl.BlockSpec((n_batch, N_MEM, D_MODEL), lambda i: (i // blocks_per_kv, 0, 0)),
                pl.BlockSpec((n_batch, N_MEM, D_MODEL), lambda i: (i // blocks_per_kv, 0, 0)),
                pl.BlockSpec((D_MODEL, D_MODEL), const2)]
    args = [x, g.reshape(1, D_MODEL), wq, k, v, wo]
    kern = functools.partial(_xattn_kernel, n_batch=n_batch)
    aliases = {}
    if prev is not None:
        in_specs.append(pl.BlockSpec(memory_space=pl.ANY))
        args.append(prev)
        aliases = {len(args) - 1: 0}
        kern = _drop_last_input(kern, len(args))
    return pl.pallas_call(
        kern,
        grid=(n_blocks,),
        in_specs=in_specs,
        out_specs=pl.BlockSpec((rows, D_MODEL), lambda i: (first_block + i, 0)),
        out_shape=jax.ShapeDtypeStruct((N_TOK, D_MODEL), F32),
        scratch_shapes=[pltpu.VMEM((rows, D_MODEL), BF16)],
        input_output_aliases=aliases,
        compiler_params=_params("parallel"),
        name="xattn",
    )(*args)


def _moe_kernel(be_ref, nu_ref, first_ref, slot_ref, nxt_ref,
                x_ref, wg_hbm, wu_hbm, wd_hbm, bg_ref, bu_ref, bd_ref, o_ref,
                wbuf_ref, wgb_ref, wub_ref, wdb_ref, sem, *, layer):
    i = pl.program_id(0)
    active = i < nu_ref[0]

    def copies(e, s):
        return [pltpu.make_async_copy(w.at[layer, e], wbuf_ref.at[s, j], sem.at[s, j])
                for j, w in enumerate((wg_hbm, wu_hbm, wd_hbm))]

    @pl.when(jnp.logical_and(active, first_ref[i] == 1))
    def _():
        s = slot_ref[i]

        @pl.when(i == 0)
        def _():
            for c in copies(be_ref[0], s):
                c.start()

        @pl.when(nxt_ref[i] >= 0)
        def _():
            for c in copies(nxt_ref[i], 1 - s):
                c.start()

        for c in copies(be_ref[i], s):
            c.wait()

        def cast(r, _):
            rs = pl.ds(pl.multiple_of(r * MOE_CAST_ROWS, MOE_CAST_ROWS), MOE_CAST_ROWS)
            wgb_ref[rs, :] = wbuf_ref[s, 0, rs, :].astype(BF16)
            wub_ref[rs, :] = wbuf_ref[s, 1, rs, :].astype(BF16)
            wdb_ref[rs, :] = wbuf_ref[s, 2, rs, :].astype(BF16)
            return 0
        lax.fori_loop(0, D_MODEL // MOE_CAST_ROWS, cast, 0)

    @pl.when(active)
    def _():
        x = jnp.concatenate([x_ref[:, c].reshape(MOE_ROWS, LANES) for c in range(PIECES_PER_ROW)],
                            axis=1).astype(BF16)
        g = jnp.minimum(_dot(x, wgb_ref[...]) + bg_ref[...], SWIGLU_LIMIT)
        u = jnp.clip(_dot(x, wub_ref[...]) + bu_ref[...], -SWIGLU_LIMIT, SWIGLU_LIMIT)
        a = g * jax.nn.sigmoid(SWIGLU_ALPHA * g) * (u + 1.0)
        o_ref[...] = _dot(a.astype(BF16), wdb_ref[...]) + bd_ref[...]


def _moe_call(layer, xs, sched, w_gate, b_gate, w_up, b_up, w_down, b_down):
    def row_map(i, be, nu, *_):
        return (jnp.minimum(i, nu[0] - 1), 0)

    def piece_map(i, be, nu, *_):
        return (jnp.minimum(i, nu[0] - 1), 0, 0, 0)

    def b_map(i, be, *_):
        return (layer, be[i], 0, 0)

    b_spec = pl.BlockSpec((None, None, 1, D_FF), b_map)
    w_spec = pl.BlockSpec(memory_space=pl.ANY)
    grid_spec = pltpu.PrefetchScalarGridSpec(
        num_scalar_prefetch=5,
        grid=(MOE_BLOCKS,),
        in_specs=[pl.BlockSpec((MOE_ROWS // SUBLANES,) + PIECE_TILE, piece_map),
                  w_spec, w_spec, w_spec, b_spec, b_spec, b_spec],
        out_specs=pl.BlockSpec((MOE_ROWS, D_MODEL), row_map),
        scratch_shapes=[pltpu.VMEM((2, 3, D_MODEL, D_FF), F32)]
        + [pltpu.VMEM((D_MODEL, D_FF), BF16)] * 3
        + [pltpu.SemaphoreType.DMA((2, 3))],
    )
    shape4 = (DEPTH, N_EXPERTS, 1, D_FF)
    return pl.pallas_call(
        functools.partial(_moe_kernel, layer=layer),
        grid_spec=grid_spec,
        out_shape=jax.ShapeDtypeStruct((MOE_BLOCKS * MOE_ROWS, D_MODEL), F32),
        compiler_params=pltpu.CompilerParams(dimension_semantics=("arbitrary",),
                                             vmem_limit_bytes=56 * 1024 * 1024),
        name="moe_experts",
    )(*sched, xs, w_gate, w_up, w_down, b_gate.reshape(shape4), b_up.reshape(shape4),
      b_down.reshape(shape4))


DISPATCH_TOKENS = 48
DMA_PIECES = 128
DISPATCH_DMAS = DISPATCH_TOKENS * PIECES_PER_ROW // DMA_PIECES
DISPATCH_CHUNKS = N_TOK // SC_WORKERS // DISPATCH_TOKENS
IDX_ROWS = 16
SC_LANES = 16
MOE_ROWS_TOTAL = MOE_BLOCKS * MOE_ROWS
TILE_PIECES = PIECES_PER_ROW * SUBLANES


def _piece_index_rows(p_v, idx_v, row0, tok0, n_dma):
    lane = lax.iota(jnp.int32, SC_LANES)
    in_tile = lane & (SUBLANES - 1)
    piece = (lane >> 3) * SUBLANES
    per_vec = SC_LANES // SUBLANES
    tiles_per_dma = DMA_PIECES // TILE_PIECES
    for m in range(n_dma):
        for tr in range(tiles_per_dma):
            first = plsc.load_gather(p_v, [in_tile + (tok0 + (m * tiles_per_dma + tr) * SUBLANES)])
            for v in range(PIECES_PER_ROW // per_vec):
                idx_v[row0 + m, pl.ds(tr * TILE_PIECES + v * SC_LANES, SC_LANES)] = (
                    first + (piece + v * per_vec * SUBLANES))


def _dispatch(h4, dest_t):
    first_piece = ((dest_t // SUBLANES) * TILE_PIECES + dest_t % SUBLANES).reshape(-1)
    chunk_pieces = DISPATCH_TOKENS * PIECES_PER_ROW
    mesh = plsc.VectorSubcoreMesh(core_axis_name="c", subcore_axis_name="s")

    @functools.partial(
        pl.kernel, mesh=mesh,
        out_type=jax.ShapeDtypeStruct((MOE_ROWS_TOTAL * PIECES_PER_ROW, LANES), F32),
        scratch_types=[pltpu.VMEM((chunk_pieces, LANES), F32), pltpu.VMEM((IDX_ROWS, DMA_PIECES), jnp.int32),
                       pltpu.VMEM((TOP_K * DISPATCH_TOKENS,), jnp.int32), pltpu.SemaphoreType.DMA],
        compiler_params=pltpu.CompilerParams(needs_layout_passes=False),
    )
    def scatter_rows(h_hbm, p_hbm, o_hbm, rows_v, idx_v, p_v, sem):
        wid = lax.axis_index("s") * SC_CORES + lax.axis_index("c")

        @pl.loop(0, DISPATCH_CHUNKS)
        def _(j):
            q = wid * DISPATCH_CHUNKS + j
            pltpu.sync_copy(h_hbm.at[pl.ds(pl.multiple_of(q * chunk_pieces, SUBLANES), chunk_pieces)], rows_v)
            for k in range(TOP_K):
                pltpu.sync_copy(
                    p_hbm.at[pl.ds(pl.multiple_of(k * N_TOK + q * DISPATCH_TOKENS, SUBLANES), DISPATCH_TOKENS)],
                    p_v.at[pl.ds(k * DISPATCH_TOKENS, DISPATCH_TOKENS)])
            for k in range(TOP_K):
                _piece_index_rows(p_v, idx_v, k * DISPATCH_DMAS, k * DISPATCH_TOKENS, DISPATCH_DMAS)
            copies = [pltpu.async_copy(rows_v.at[pl.ds(m * DMA_PIECES, DMA_PIECES)],
                                       o_hbm.at[idx_v.at[k * DISPATCH_DMAS + m]], sem)
                      for k in range(TOP_K) for m in range(DISPATCH_DMAS)]
            for cp in copies:
                cp.wait()

    xs = scatter_rows(h4.reshape(N_TOK * PIECES_PER_ROW, LANES), first_piece)
    return xs.reshape((MOE_ROWS_TOTAL // SUBLANES,) + PIECE_TILE)


def _route(e_t, cnt):
    blk_cnt = cnt[:, :, 0].astype(jnp.int32)
    counts = jnp.sum(blk_cnt, axis=0)
    padded = ((counts + MOE_ROWS - 1) // MOE_ROWS) * MOE_ROWS
    pend = jnp.cumsum(padded)
    pstart = pend - padded
    base = pstart[None, :] + jnp.cumsum(blk_cnt, axis=0) - blk_cnt
    dest_t = _dest(e_t, base.astype(F32)[:, :, None])
    n_used = (pend[-1] // MOE_ROWS).astype(jnp.int32)
    blk = jnp.minimum(jnp.arange(MOE_BLOCKS, dtype=jnp.int32), n_used - 1) * MOE_ROWS
    blk_exp = jnp.minimum(jnp.sum((pend[None, :] <= blk[:, None]).astype(jnp.int32), axis=1), N_EXPERTS - 1)
    ids = jnp.arange(MOE_BLOCKS, dtype=jnp.int32)
    first = jnp.logical_and(ids < n_used, jnp.logical_or(ids == 0, blk_exp != jnp.roll(blk_exp, 1)))
    first = first.astype(jnp.int32)
    slot = (jnp.cumsum(first) - 1) % 2
    eid = jnp.arange(N_EXPERTS, dtype=jnp.int32)
    later = jnp.logical_and(eid[None, :] > eid[:, None], (counts > 0)[None, :])
    nxt_e = jnp.min(jnp.where(later, eid[None, :], N_EXPERTS), axis=1)
    nxt = jnp.where(nxt_e < N_EXPERTS, nxt_e, -1)[blk_exp].astype(jnp.int32)
    return dest_t, (blk_exp, n_used.reshape(1), first, slot.astype(jnp.int32), nxt)


def kernel(x_prompt, x_sample, mem_prompt, state_s5_re, state_s5_im, cache_mem_k, cache_mem_v, norm_mix_g, w_in, s5_lambda_re, s5_lambda_im, s5_b_re, s5_b_im, s5_c_re, s5_c_im, s5_d, s5_log_step, s5_w_glu, gm_ln_g, gm_ln_b, gm_w_s, gm_b_s, out_norm_g, w_out, xa_norm_g, mem_norm_g, w_xq, w_xk, w_xv, w_xo, ffn_norm_g, router_w, router_b, e_w_gate, e_b_gate, e_w_up, e_b_up, e_w_down, e_b_down, final_norm_g):
    xs = (x_prompt.reshape(N_PROMPT, D_MODEL), x_sample.reshape(N_SAMPLE, D_MODEL))
    mem = mem_prompt.reshape(BATCH * N_MEM, D_MODEL)
    outs = {k: [] for k in ('s5r_p', 's5i_p', 'mk_p', 'mv_p', 's5r_s', 's5i_s', 'gmv_s')}
    for l in range(DEPTH):
        proj = _norm_matmul(xs, norm_mix_g[l], w_in[l].astype(BF16))
        sw = _s5_weights(s5_lambda_re[l], s5_lambda_im[l], s5_b_re[l], s5_b_im[l], s5_c_re[l], s5_c_im[l],
                         s5_d[l], s5_log_step[l], S5_SEG_PROMPT)
        w_glu = s5_w_glu[l].astype(BF16)
        og = out_norm_g[l]
        ms5, er_p, ei_p = _s5_call(proj, sw, w_glu, og[:S5_WIDTH], S5_SEG_PROMPT, True)
        h0 = (state_s5_re[l].reshape(DEC_BATCH // SUBLANES, SUBLANES, S5_COLS),
              state_s5_im[l].reshape(DEC_BATCH // SUBLANES, SUBLANES, S5_COLS))
        ms5, er_s, ei_s = _s5_call(proj, sw, w_glu, og[:S5_WIDTH], DEC_SEQ, False, h0=h0, prev=ms5)
        outs['s5r_p'].append(er_p[:, SUBLANES - 1].reshape(BATCH, S5_GROUPS, S5_STATE))
        outs['s5i_p'].append(ei_p[:, SUBLANES - 1].reshape(BATCH, S5_GROUPS, S5_STATE))
        outs['s5r_s'].append(er_s.reshape(DEC_BATCH, S5_GROUPS, S5_STATE))
        outs['s5i_s'].append(ei_s.reshape(DEC_BATCH, S5_GROUPS, S5_STATE))

        (mgm,) = _gmlp_call(proj, gm_ln_g[l], gm_ln_b[l], gm_w_s[l], gm_b_s[l], og[S5_WIDTH:],
                            GM_ROWS_PROMPT, GM_CHUNK, 0, N_PROMPT // GM_ROWS_PROMPT, False)
        mgm, vn = _gmlp_call(proj, gm_ln_g[l], gm_ln_b[l], gm_w_s[l], gm_b_s[l], og[S5_WIDTH:],
                             N_SAMPLE, DEC_SEQ, N_PROMPT // N_SAMPLE, 1, True, prev=mgm)
        outs['gmv_s'].append(vn.reshape(DEC_BATCH, DEC_SEQ, GM_HEADS, GM_HEAD_DIM))
        x = _out_proj(xs, ms5, mgm, w_out[l].astype(BF16))

        mk = _norm_matmul((mem,), mem_norm_g[l], w_xk[l].astype(BF16))
        mv = _norm_matmul((mem,), mem_norm_g[l], w_xv[l].astype(BF16))
        outs['mk_p'].append(mk.reshape(BATCH, N_MEM, XA_HEADS, XA_HEAD_DIM))
        outs['mv_p'].append(mv.reshape(BATCH, N_MEM, XA_HEADS, XA_HEAD_DIM))
        wq = w_xq[l].astype(BF16)
        wo = w_xo[l].astype(BF16)
        xa = _xattn_call(x, xa_norm_g[l], wq, mk.reshape(BATCH, N_MEM, D_MODEL).astype(BF16),
                         mv.reshape(BATCH, N_MEM, D_MODEL).astype(BF16), wo,
                         XA_ROWS_PROMPT, 1, 0, N_PROMPT // XA_ROWS_PROMPT, SEQ // XA_ROWS_PROMPT)
        rows_s = XA_SAMPLE_BATCHES * DEC_SEQ
        x = _xattn_call(x, xa_norm_g[l], wq,
                        cache_mem_k[l].reshape(DEC_BATCH, N_MEM, D_MODEL).astype(BF16),
                        cache_mem_v[l].reshape(DEC_BATCH, N_MEM, D_MODEL).astype(BF16), wo,
                        rows_s, XA_SAMPLE_BATCHES, N_PROMPT // rows_s, N_SAMPLE // rows_s, 1, prev=xa)

        h, e_t, gates_t, cnt = _router(x, ffn_norm_g[l], router_w[l], router_b[l])
        dest_t, sched = _route(e_t, cnt)
        ys = _moe_call(l, _dispatch(h, dest_t), sched, e_w_gate, e_b_gate, e_w_up, e_b_up, e_w_down, e_b_down)
        x = _combine(x, ys[dest_t.reshape(-1)], gates_t.T, final_norm_g if l == DEPTH - 1 else None)
        xs = (x,)

    y_prompt = x[0].reshape(BATCH, SEQ, D_MODEL)
    y_sample = x[1].reshape(DEC_BATCH, DEC_SEQ, D_MODEL)
    st = jnp.stack
    return (y_prompt, y_sample, st(outs['s5r_p']), st(outs['s5i_p']), st(outs['mk_p']), st(outs['mv_p']),
            st(outs['s5r_s']), st(outs['s5i_s']), st(outs['gmv_s']))
```

```python
import functools

import jax
import jax.numpy as jnp
from jax import lax
from jax.experimental import pallas as pl
from jax.experimental.pallas import tpu as pltpu
from jax.experimental.pallas import tpu_sc as plsc

F32 = jnp.float32
BF16 = jnp.bfloat16

D_MODEL = 1024
BATCH = 4
SEQ = 4096
DEPTH = 2
DEC_BATCH = 16
DEC_SEQ = 32
S5_GROUPS = 32
S5_GROUP_CH = 16
S5_STATE = 64
S5_WIDTH = S5_GROUPS * S5_GROUP_CH
S5_COLS = S5_GROUPS * S5_STATE
GM_HEADS = 4
GM_HEAD_DIM = 128
GM_WIDTH = GM_HEADS * GM_HEAD_DIM
GM_CHUNK = 128
IN_PROJ = S5_WIDTH + 2 * GM_WIDTH
N_MEM = 256
XA_HEADS = 4
XA_HEAD_DIM = D_MODEL // XA_HEADS
N_EXPERTS = 32
TOP_K = 4
D_FF = D_MODEL
SWIGLU_LIMIT = 7.0
SWIGLU_ALPHA = 1.702
EPS = 1e-6
EIG_CLIP = -1e-4

N_PROMPT = BATCH * SEQ
N_SAMPLE = DEC_BATCH * DEC_SEQ
N_TOK = N_PROMPT + N_SAMPLE

SUBLANES = 8
LANES = 128
MXU_DIM = 256
VMEM_LIMIT = 48 * 1024 * 1024
SC_CORES = 2
SC_SUBCORES = 16
SC_WORKERS = SC_CORES * SC_SUBCORES

PIECE_TILE = (D_MODEL // LANES, SUBLANES, LANES)
PIECES_PER_ROW = D_MODEL // LANES

ROW_BLOCK = 512
S5_SEG_PROMPT = 64
S5_COL_BLOCK = 512
GM_ROWS_PROMPT = 256
XA_ROWS_PROMPT = 512
XA_SAMPLE_BATCHES = 4
MOE_ROWS = 512
MOE_BLOCKS = -(-N_TOK * TOP_K // MOE_ROWS) + N_EXPERTS
MOE_CAST_ROWS = 128


def _params(*sem):
    return pltpu.CompilerParams(dimension_semantics=sem, vmem_limit_bytes=VMEM_LIMIT)


def _rms(x, g):
    return x * lax.rsqrt(jnp.mean(x * x, axis=-1, keepdims=True) + EPS) * g


def _dot(a, b):
    return jnp.dot(a, b, preferred_element_type=F32)


PROMPT_BLOCKS = N_PROMPT // ROW_BLOCK
SAMPLE_BLOCKS = N_SAMPLE // ROW_BLOCK


def _token_sources(xs):
    if len(xs) == 1:
        return [pl.BlockSpec((ROW_BLOCK, D_MODEL), lambda i: (i, 0))]
    return [pl.BlockSpec((ROW_BLOCK, D_MODEL), lambda i: (jnp.minimum(i, PROMPT_BLOCKS - 1), 0)),
            pl.BlockSpec((ROW_BLOCK, D_MODEL), lambda i: (jnp.maximum(i - PROMPT_BLOCKS, 0), 0))]


def _for_group(refs, body):
    if len(refs) == 1:
        body(refs[0])
        return
    i = pl.program_id(0)
    pl.when(i < PROMPT_BLOCKS)(lambda: body(refs[0]))
    pl.when(i >= PROMPT_BLOCKS)(lambda: body(refs[1]))


def _norm_matmul_kernel(*refs, n_src):
    g_ref, w_ref, o_ref = refs[n_src:]

    def body(x_ref):
        o_ref[...] = _dot(_rms(x_ref[...], g_ref[...]).astype(BF16), w_ref[...])
    _for_group(refs[:n_src], body)


def _norm_matmul(xs, g, w):
    d = D_MODEL
    n = sum(x.shape[0] for x in xs)
    dout = w.shape[1]
    return pl.pallas_call(
        functools.partial(_norm_matmul_kernel, n_src=len(xs)),
        grid=(n // ROW_BLOCK,),
        in_specs=_token_sources(xs) + [pl.BlockSpec((1, d), lambda i: (0, 0)),
                                       pl.BlockSpec((d, dout), lambda i: (0, 0))],
        out_specs=pl.BlockSpec((ROW_BLOCK, dout), lambda i: (i, 0)),
        out_shape=jax.ShapeDtypeStruct((n, dout), F32),
        compiler_params=_params("parallel"),
        name="norm_matmul",
    )(*xs, g.reshape(1, d), w)


def _out_proj_kernel(*refs, n_src):
    a_ref, b_ref, wa_ref, wb_ref, o_ref = refs[n_src:]

    def body(x_ref):
        o_ref[...] = x_ref[...] + _dot(a_ref[...], wa_ref[...]) + _dot(b_ref[...], wb_ref[...])
    _for_group(refs[:n_src], body)


def _out_proj(xs, a, b, w):
    d = D_MODEL
    n = a.shape[0]
    wa, wb = w[:S5_WIDTH], w[S5_WIDTH:]
    rows = ROW_BLOCK
    return pl.pallas_call(
        functools.partial(_out_proj_kernel, n_src=len(xs)),
        grid=(n // rows,),
        in_specs=_token_sources(xs) + [pl.BlockSpec((rows, S5_WIDTH), lambda i: (i, 0)),
                                       pl.BlockSpec((rows, GM_WIDTH), lambda i: (i, 0)),
                                       pl.BlockSpec((S5_WIDTH, d), lambda i: (0, 0)),
                                       pl.BlockSpec((GM_WIDTH, d), lambda i: (0, 0))],
        out_specs=pl.BlockSpec((rows, d), lambda i: (i, 0)),
        out_shape=jax.ShapeDtypeStruct((n, d), F32),
        compiler_params=_params("parallel"),
        name="out_proj",
    )(*xs, a, b, wa, wb)


def _router_kernel(x_ref, g_ref, wt_ref, b_ref, h_ref, e_ref, gate_ref, cnt_ref):
    h = _rms(x_ref[...], g_ref[...])
    for c in range(D_MODEL // LANES):
        h_ref[:, c] = h[:, c * LANES:(c + 1) * LANES].reshape(h.shape[0] // SUBLANES, SUBLANES, LANES)
    logits = lax.dot_general(wt_ref[...], h.astype(BF16), (((1,), (1,)), ((), ())),
                             preferred_element_type=F32) + b_ref[...]
    sub = lax.broadcasted_iota(jnp.int32, logits.shape, 0)
    work = logits
    chosen = jnp.zeros(logits.shape, F32)
    vals, idxs = [], []
    for _ in range(TOP_K):
        m = jnp.max(work, axis=0, keepdims=True)
        idx = jnp.min(jnp.where(work == m, sub, N_EXPERTS), axis=0, keepdims=True)
        sel = sub == idx
        vals.append(m)
        idxs.append(idx)
        work = jnp.where(sel, -jnp.inf, work)
        chosen = jnp.where(sel, 1.0, chosen)
    ex = [jnp.exp(v - vals[0]) for v in vals]
    den = ex[0] + ex[1] + ex[2] + ex[3]
    gate_ref[...] = jnp.concatenate([e / den for e in ex], axis=0)
    e_ref[...] = jnp.concatenate(idxs, axis=0)
    cnt_ref[...] = jnp.broadcast_to(jnp.sum(chosen, axis=1, keepdims=True), (N_EXPERTS, LANES))


def _router(x, g, w, b):
    n, d = x.shape
    rows = ROW_BLOCK
    return pl.pallas_call(
        _router_kernel,
        grid=(n // rows,),
        in_specs=[pl.BlockSpec((rows, d), lambda i: (i, 0)),
                  pl.BlockSpec((1, d), lambda i: (0, 0)),
                  pl.BlockSpec((N_EXPERTS, d), lambda i: (0, 0)),
                  pl.BlockSpec((N_EXPERTS, 1), lambda i: (0, 0))],
        out_specs=[pl.BlockSpec((rows // SUBLANES,) + PIECE_TILE, lambda i: (i, 0, 0, 0)),
                   pl.BlockSpec((TOP_K, rows), lambda i: (0, i)),
                   pl.BlockSpec((TOP_K, rows), lambda i: (0, i)),
                   pl.BlockSpec((None, N_EXPERTS, LANES), lambda i: (i, 0, 0))],
        out_shape=[jax.ShapeDtypeStruct((n // SUBLANES,) + PIECE_TILE, F32),
                   jax.ShapeDtypeStruct((TOP_K, n), jnp.int32),
                   jax.ShapeDtypeStruct((TOP_K, n), F32),
                   jax.ShapeDtypeStruct((n // rows, N_EXPERTS, LANES), F32)],
        compiler_params=_params("parallel"),
        name="router",
    )(x, g.reshape(1, d), w.T.astype(BF16), b.reshape(N_EXPERTS, 1))


def _dest_kernel(e_ref, base_ref, tri_ref, d_ref):
    e = e_ref[...]
    sub = lax.broadcasted_iota(jnp.int32, (N_EXPERTS, e.shape[1]), 0)
    hot = [sub == e[k:k + 1, :] for k in range(TOP_K)]
    chosen = sum(jnp.where(hk, 1.0, 0.0) for hk in hot)
    pos = base_ref[...] + _dot(chosen.astype(BF16), tri_ref[...])
    d_ref[...] = jnp.concatenate(
        [jnp.sum(jnp.where(hk, pos, 0.0), axis=0, keepdims=True) for hk in hot], axis=0).astype(jnp.int32)


def _dest(e_t, base):
    n = e_t.shape[1]
    rows = ROW_BLOCK
    tri = jnp.triu(jnp.ones((rows, rows), F32), 1).astype(BF16)
    return pl.pallas_call(
        _dest_kernel,
        grid=(n // rows,),
        in_specs=[pl.BlockSpec((TOP_K, rows), lambda i: (0, i)),
                  pl.BlockSpec((None, N_EXPERTS, 1), lambda i: (i, 0, 0)),
                  pl.BlockSpec((rows, rows), lambda i: (0, 0))],
        out_specs=pl.BlockSpec((TOP_K, rows), lambda i: (0, i)),
        out_shape=jax.ShapeDtypeStruct((TOP_K, n), jnp.int32),
        compiler_params=_params("parallel"),
        name="moe_dest",
    )(e_t, base, tri)


def _combine_kernel(*refs, final):
    if final:
        x_ref, g0, g1, g2, g3, gate_ref, fg_ref, op_ref, os_ref = refs
    else:
        x_ref, g0, g1, g2, g3, gate_ref, o_ref = refs
    acc = x_ref[...]
    for k, g in enumerate((g0, g1, g2, g3)):
        acc = acc + gate_ref[:, k:k + 1] * g[...]
    if final:
        def store(o_ref):
            o_ref[...] = _rms(acc, fg_ref[...])
        _for_group((op_ref, os_ref), store)
    else:
        o_ref[...] = acc


def _combine(x, picked, gates, final_g=None):
    n, d = x.shape
    rows = ROW_BLOCK
    picked = picked.reshape(TOP_K, n, d)
    final = final_g is not None
    in_specs = [pl.BlockSpec((rows, d), lambda i: (i, 0))]
    in_specs += [pl.BlockSpec((None, rows, d), functools.partial(lambda i, k: (k, i, 0), k=k))
                 for k in range(TOP_K)]
    in_specs.append(pl.BlockSpec((rows, TOP_K), lambda i: (i, 0)))
    args = [x, picked, picked, picked, picked, gates]
    out_specs = pl.BlockSpec((rows, d), lambda i: (i, 0))
    out_shape = jax.ShapeDtypeStruct((n, d), F32)
    if final:
        in_specs.append(pl.BlockSpec((1, d), lambda i: (0, 0)))
        args.append(final_g.reshape(1, d))
        out_specs = _token_sources((None, None))
        out_shape = [jax.ShapeDtypeStruct((N_PROMPT, d), F32), jax.ShapeDtypeStruct((N_SAMPLE, d), F32)]
    return pl.pallas_call(
        functools.partial(_combine_kernel, final=final),
        grid=(n // rows,),
        in_specs=in_specs,
        out_specs=out_specs,
        out_shape=out_shape,
        compiler_params=_params("arbitrary" if final else "parallel"),
        name="moe_combine",
    )(*args)


def _cmul_add(ar, ai, hr, hi, br, bi):
    return ar * hr - ai * hi + br, ar * hi + ai * hr + bi


def _s5_kernel(*refs, seg, chained):
    if chained:
        (u_ref, wbr_ref, wbi_ref, wcr_ref, wci_ref, ar_ref, ai_ref, pr_ref, pi_ref, d_ref,
         wglu_ref, og_ref, o_ref, er_ref, ei_ref,
         up_ref, hr_ref, hi_ref, sr_ref, si_ref, op_ref, cr_ref, ci_ref) = refs
    else:
        (u_ref, wbr_ref, wbi_ref, wcr_ref, wci_ref, ar_ref, ai_ref, pr_ref, pi_ref, d_ref,
         wglu_ref, og_ref, h0r_ref, h0i_ref, o_ref, er_ref, ei_ref,
         up_ref, hr_ref, hi_ref, sr_ref, si_ref, op_ref) = refs
    rows = seg * SUBLANES
    half = S5_COLS // 2

    n_lane_blk = S5_WIDTH // LANES
    for j in range(SUBLANES):
        for i0 in range(0, seg, SUBLANES):
            for c in range(n_lane_blk):
                up_ref[c, pl.ds(i0 * SUBLANES + j, SUBLANES, stride=SUBLANES), :] = (
                    u_ref[j * seg + i0:j * seg + i0 + SUBLANES, c * LANES:(c + 1) * LANES])

    up = jnp.concatenate([up_ref[c] for c in range(n_lane_blk)], axis=1)
    ub = up.astype(BF16)
    for k in range(2):
        uk = ub[:, k * MXU_DIM:(k + 1) * MXU_DIM]
        hr_ref[:, k * half:(k + 1) * half] = _dot(uk, wbr_ref[k])
        hi_ref[:, k * half:(k + 1) * half] = _dot(uk, wbi_ref[k])

    for cb in range(S5_COLS // S5_COL_BLOCK):
        cols = slice(cb * S5_COL_BLOCK, (cb + 1) * S5_COL_BLOCK)
        ar = jnp.broadcast_to(ar_ref[:, cols], (SUBLANES, S5_COL_BLOCK))
        ai = jnp.broadcast_to(ai_ref[:, cols], (SUBLANES, S5_COL_BLOCK))

        def step(i, carry, cols=cols, ar=ar, ai=ai):
            r = pl.multiple_of(i * SUBLANES, SUBLANES)
            h_r, h_i = _cmul_add(ar, ai, carry[0], carry[1],
                                 hr_ref[pl.ds(r, SUBLANES), cols], hi_ref[pl.ds(r, SUBLANES), cols])
            hr_ref[pl.ds(r, SUBLANES), cols] = h_r
            hi_ref[pl.ds(r, SUBLANES), cols] = h_i
            return h_r, h_i

        zero = jnp.zeros((SUBLANES, S5_COL_BLOCK), F32)
        lax.fori_loop(0, seg, step, (zero, zero), unroll=2)

    end_r = hr_ref[rows - SUBLANES:rows, :]
    end_i = hi_ref[rows - SUBLANES:rows, :]
    as_r = pr_ref[rows - 1:rows, :]
    as_i = pi_ref[rows - 1:rows, :]
    if chained:
        @pl.when(pl.program_id(1) == 0)
        def _():
            cr_ref[...] = jnp.zeros_like(cr_ref)
            ci_ref[...] = jnp.zeros_like(ci_ref)

        cur_r, cur_i = cr_ref[...], ci_ref[...]
        for j in range(SUBLANES):
            sr_ref[j:j + 1, :] = cur_r
            si_ref[j:j + 1, :] = cur_i
            cur_r, cur_i = _cmul_add(as_r, as_i, cur_r, cur_i, end_r[j:j + 1], end_i[j:j + 1])
        cr_ref[...] = cur_r
        ci_ref[...] = cur_i
    else:
        sr_ref[...] = h0r_ref[...]
        si_ref[...] = h0i_ref[...]
    t_r, t_i = _cmul_add(as_r, as_i, sr_ref[...], si_ref[...], end_r, end_i)
    er_ref[...] = t_r
    ei_ref[...] = t_i

    slabs_per_window = MXU_DIM // (MXU_DIM // S5_STATE * S5_GROUP_CH)
    ys = []
    for w in range(S5_WIDTH // MXU_DIM):
        acc = d_ref[:, w * MXU_DIM:(w + 1) * MXU_DIM] * jnp.concatenate(
            [up_ref[c] for c in range(w * MXU_DIM // LANES, (w + 1) * MXU_DIM // LANES)], axis=1)
        for s in range(slabs_per_window):
            cols = slice((w * slabs_per_window + s) * MXU_DIM, (w * slabs_per_window + s + 1) * MXU_DIM)
            shape3 = (seg, SUBLANES, MXU_DIM)
            h_r, h_i = _cmul_add(pr_ref[:, cols].reshape(shape3), pi_ref[:, cols].reshape(shape3),
                                 sr_ref[:, cols][None], si_ref[:, cols][None],
                                 hr_ref[:, cols].reshape(shape3), hi_ref[:, cols].reshape(shape3))
            acc += _dot(h_r.reshape(rows, MXU_DIM).astype(BF16), wcr_ref[w * slabs_per_window + s])
            acc += _dot(h_i.reshape(rows, MXU_DIM).astype(BF16), wci_ref[w * slabs_per_window + s])
        ys.append(jax.nn.gelu(acc).astype(BF16))

    z = _dot(jnp.concatenate(ys, axis=1), wglu_ref[...])
    o = _rms(z[:, :S5_WIDTH] * jax.nn.sigmoid(z[:, S5_WIDTH:]), og_ref[...])
    for c in range(n_lane_blk):
        op_ref[c] = o[:, c * LANES:(c + 1) * LANES]

    for j in range(SUBLANES):
        for i0 in range(0, seg, 2 * SUBLANES):
            for c in range(n_lane_blk):
                o_ref[j * seg + i0:j * seg + i0 + 2 * SUBLANES, c * LANES:(c + 1) * LANES] = (
                    op_ref[c, pl.ds(i0 * SUBLANES + j, 2 * SUBLANES, stride=SUBLANES), :].astype(BF16))


def _s5_weights(lam_re, lam_im, b_re, b_im, c_re, c_im, d, log_step, seg):
    step = jnp.exp(log_step)[:, None]
    lr = jnp.minimum(lam_re, EIG_CLIP)
    li = lam_im
    mag = jnp.exp(lr * step)
    ar = mag * jnp.cos(li * step)
    ai = mag * jnp.sin(li * step)
    den = lr * lr + li * li
    fr = ((ar - 1.0) * lr + ai * li) / den
    fi = (ai * lr - (ar - 1.0) * li) / den
    bbar_r = fr[..., None] * b_re - fi[..., None] * b_im
    bbar_i = fr[..., None] * b_im + fi[..., None] * b_re
    t = jnp.arange(1, seg + 1, dtype=F32)[:, None, None]
    pm = jnp.exp(lr * step * t)
    p_r = jnp.repeat((pm * jnp.cos(li * step * t)).reshape(seg, S5_COLS), SUBLANES, axis=0)
    p_i = jnp.repeat((pm * jnp.sin(li * step * t)).reshape(seg, S5_COLS), SUBLANES, axis=0)

    gk = MXU_DIM // S5_GROUP_CH

    def in_map(bbar):
        bb = bbar.reshape(S5_GROUPS // gk, gk, S5_STATE, S5_GROUP_CH)
        w = jnp.einsum('kgnh,gG->kghGn', bb, jnp.eye(gk, dtype=F32))
        return w.reshape(S5_GROUPS // gk, MXU_DIM, gk * S5_STATE).astype(BF16)

    gs = MXU_DIM // S5_STATE
    n_slab = S5_COLS // MXU_DIM
    per_win = MXU_DIM // (gs * S5_GROUP_CH)

    def out_map(c):
        cc = c.reshape(n_slab // per_win, per_win, gs, S5_GROUP_CH, S5_STATE)
        w = jnp.einsum('wsghn,sS,gG->wsgnSGh', cc, jnp.eye(per_win, dtype=F32), jnp.eye(gs, dtype=F32))
        return w.reshape(n_slab, MXU_DIM, MXU_DIM).astype(BF16)

    return dict(wbr=in_map(bbar_r), wbi=in_map(bbar_i), wcr=out_map(c_re), wci=out_map(-c_im),
                ar=ar.reshape(1, S5_COLS), ai=ai.reshape(1, S5_COLS), pr=p_r, pi=p_i,
                d=d.reshape(1, S5_WIDTH))


def _s5_call(proj, sw, w_glu, og, seg, chained, h0=None, prev=None):
    rows = seg * SUBLANES
    const2 = lambda *_: (0, 0)
    const3 = lambda *_: (0, 0, 0)
    if chained:
        n_chunk = SEQ // rows
        grid = (BATCH, n_chunk)
        row_map = lambda b, c: (b * n_chunk + c, 0)
        end_map = lambda b, c: (b, 0, 0)
        n_end = BATCH
        sem = ("parallel", "arbitrary")
    else:
        grid = (DEC_BATCH // SUBLANES,)
        row_map = lambda k: (N_PROMPT // rows + k, 0)
        end_map = lambda k: (k, 0, 0)
        n_end = DEC_BATCH // SUBLANES
        sem = ("parallel",)
    half = S5_COLS // 2
    in_specs = [pl.BlockSpec((rows, S5_WIDTH), row_map),
                pl.BlockSpec((2, MXU_DIM, half), const3),
                pl.BlockSpec((2, MXU_DIM, half), const3),
                pl.BlockSpec((S5_COLS // MXU_DIM, MXU_DIM, MXU_DIM), const3),
                pl.BlockSpec((S5_COLS // MXU_DIM, MXU_DIM, MXU_DIM), const3),
                pl.BlockSpec((1, S5_COLS), const2),
                pl.BlockSpec((1, S5_COLS), const2),
                pl.BlockSpec((rows, S5_COLS), const2),
                pl.BlockSpec((rows, S5_COLS), const2),
                pl.BlockSpec((1, S5_WIDTH), const2),
                pl.BlockSpec((S5_WIDTH, 2 * S5_WIDTH), const2),
                pl.BlockSpec((1, S5_WIDTH), const2)]
    args = [proj, sw['wbr'], sw['wbi'], sw['wcr'], sw['wci'], sw['ar'], sw['ai'],
            sw['pr'][:rows], sw['pi'][:rows], sw['d'], w_glu, og.reshape(1, S5_WIDTH)]
    scratch = [pltpu.VMEM((S5_WIDTH // LANES, rows, LANES), F32),
               pltpu.VMEM((rows, S5_COLS), F32), pltpu.VMEM((rows, S5_COLS), F32),
               pltpu.VMEM((SUBLANES, S5_COLS), F32), pltpu.VMEM((SUBLANES, S5_COLS), F32),
               pltpu.VMEM((S5_WIDTH // LANES, rows, LANES), F32)]
    aliases = {}
    if chained:
        scratch += [pltpu.VMEM((1, S5_COLS), F32), pltpu.VMEM((1, S5_COLS), F32)]
    else:
        in_specs += [pl.BlockSpec((None, SUBLANES, S5_COLS), end_map),
                     pl.BlockSpec((None, SUBLANES, S5_COLS), end_map)]
        args += [h0[0], h0[1]]
    if prev is not None:
        in_specs.append(pl.BlockSpec(memory_space=pl.ANY))
        args.append(prev)
        aliases = {len(args) - 1: 0}
    kern = functools.partial(_s5_kernel, seg=seg, chained=chained)
    if prev is not None:
        kern = _drop_last_input(kern, len(args))
    return pl.pallas_call(
        kern,
        grid=grid,
        in_specs=in_specs,
        out_specs=[pl.BlockSpec((rows, S5_WIDTH), row_map),
                   pl.BlockSpec((None, SUBLANES, S5_COLS), end_map),
                   pl.BlockSpec((None, SUBLANES, S5_COLS), end_map)],
        out_shape=[jax.ShapeDtypeStruct((N_TOK, S5_WIDTH), BF16),
                   jax.ShapeDtypeStruct((n_end, SUBLANES, S5_COLS), F32),
                   jax.ShapeDtypeStruct((n_end, SUBLANES, S5_COLS), F32)],
        scratch_shapes=scratch,
        input_output_aliases=aliases,
        compiler_params=_params(*sem),
        name="s5_chained" if chained else "s5_streams",
    )(*args)


def _drop_last_input(kern, n_in):
    def wrapped(*refs):
        return kern(*refs[:n_in - 1], *refs[n_in:])
    return wrapped


def _gmlp_kernel(*refs, chunk, emit_vn):
    if emit_vn:
        u_ref, v_ref, lng_ref, lnb_ref, ws_ref, bst_ref, og_ref, o_ref, vn_ref, acc_ref = refs
    else:
        u_ref, v_ref, lng_ref, lnb_ref, ws_ref, bst_ref, og_ref, o_ref, acc_ref = refs
    rows = u_ref.shape[0]
    u = jax.nn.gelu(u_ref[...])
    v = jax.nn.gelu(v_ref[...])
    tri = (lax.broadcasted_iota(jnp.int32, (chunk, chunk), 0)
           >= lax.broadcasted_iota(jnp.int32, (chunk, chunk), 1))
    for h in range(GM_HEADS):
        cols = slice(h * GM_HEAD_DIM, (h + 1) * GM_HEAD_DIM)
        vh = v[:, cols]
        mu = jnp.mean(vh, axis=-1, keepdims=True)
        cen = vh - mu
        var = jnp.mean(cen * cen, axis=-1, keepdims=True)
        vn = cen * lax.rsqrt(var + EPS) * lng_ref[:, cols] + lnb_ref[:, cols]
        if emit_vn:
            vn_ref[:, cols] = vn
        w = jnp.where(tri, ws_ref[h], 0.0).astype(BF16)
        vb = vn.astype(BF16)
        bias = bst_ref[:, h:h + 1]
        for c in range(rows // chunk):
            rs = slice(c * chunk, (c + 1) * chunk)
            acc_ref[rs, cols] = u[rs, cols] * (_dot(w, vb[rs]) + bias)
    o_ref[...] = _rms(acc_ref[...], og_ref[...]).astype(BF16)


def _gmlp_call(proj, ln_g, ln_b, w_s, b_s, og, rows, chunk, first_block, n_blocks, emit_vn, prev=None):
    const2 = lambda i: (0, 0)
    in_specs = [pl.BlockSpec((rows, GM_WIDTH), lambda i: (first_block + i, 1)),
                pl.BlockSpec((rows, GM_WIDTH), lambda i: (first_block + i, 2)),
                pl.BlockSpec((1, GM_WIDTH), const2),
                pl.BlockSpec((1, GM_WIDTH), const2),
                pl.BlockSpec((GM_HEADS, chunk, chunk), lambda i: (0, 0, 0)),
                pl.BlockSpec((chunk, GM_HEADS), const2),
                pl.BlockSpec((1, GM_WIDTH), const2)]
    args = [proj, proj, ln_g.reshape(1, GM_WIDTH), ln_b.reshape(1, GM_WIDTH),
            w_s[:, :chunk, :chunk], b_s[:, :chunk].T, og.reshape(1, GM_WIDTH)]
    out_specs = [pl.BlockSpec((rows, GM_WIDTH), lambda i: (first_block + i, 0))]
    out_shape = [jax.ShapeDtypeStruct((N_TOK, GM_WIDTH), BF16)]
    if emit_vn:
        out_specs.append(pl.BlockSpec((rows, GM_WIDTH), lambda i: (i, 0)))
        out_shape.append(jax.ShapeDtypeStruct((n_blocks * rows, GM_WIDTH), F32))
    kern = functools.partial(_gmlp_kernel, chunk=chunk, emit_vn=emit_vn)
    aliases = {}
    if prev is not None:
        in_specs.append(pl.BlockSpec(memory_space=pl.ANY))
        args.append(prev)
        aliases = {len(args) - 1: 0}
        kern = _drop_last_input(kern, len(args))
    return pl.pallas_call(
        kern,
        grid=(n_blocks,),
        in_specs=in_specs,
        out_specs=out_specs,
        out_shape=out_shape,
        scratch_shapes=[pltpu.VMEM((rows, GM_WIDTH), F32)],
        input_output_aliases=aliases,
        compiler_params=_params("parallel"),
        name="gmlp_vn" if emit_vn else "gmlp",
    )(*args)


def _xattn_kernel(x_ref, g_ref, wq_ref, k_ref, v_ref, wo_ref, o_ref, att_ref, *, n_batch):
    rows = x_ref.shape[0]
    tb = rows // n_batch
    x = x_ref[...]
    q = _dot(_rms(x, g_ref[...]).astype(BF16), wq_ref[...]).astype(BF16)
    scale = XA_HEAD_DIM ** -0.5
    for b in range(n_batch):
        rs = slice(b * tb, (b + 1) * tb)
        for h in range(XA_HEADS):
            cols = slice(h * XA_HEAD_DIM, (h + 1) * XA_HEAD_DIM)
            s = lax.dot_general(q[rs, cols], k_ref[b, :, cols], (((1,), (1,)), ((), ())),
                                preferred_element_type=F32) * scale
            e = jnp.exp(s - jnp.max(s, axis=-1, keepdims=True))
            p = e / jnp.sum(e, axis=-1, keepdims=True)
            att_ref[rs, cols] = _dot(p.astype(BF16), v_ref[b, :, cols]).astype(BF16)
    o_ref[...] = x + _dot(att_ref[...], wo_ref[...])


def _xattn_call(x, g, wq, k, v, wo, rows, n_batch, first_block, n_blocks, blocks_per_kv, prev=None):
    const2 = lambda i: (0, 0)
    in_specs = [pl.BlockSpec((rows, D_MODEL), lambda i: (first_block + i, 0)),
                pl.BlockSpec((1, D_MODEL), const2),
                pl.BlockSpec((D_MODEL, D_MODEL), const2),
                pl.BlockSpec((n_batch, N_MEM, D_MODEL), lambda i: (i // blocks_per_kv, 0, 0)),
                pl.BlockSpec((n_batch, N_MEM, D_MODEL), lambda i: (i // blocks_per_kv, 0, 0)),
                pl.BlockSpec((D_MODEL, D_MODEL), const2)]
    args = [x, g.reshape(1, D_MODEL), wq, k, v, wo]
    kern = functools.partial(_xattn_kernel, n_batch=n_batch)
    aliases = {}
    if prev is not None:
        in_specs.append(pl.BlockSpec(memory_space=pl.ANY))
        args.append(prev)
        aliases = {len(args) - 1: 0}
        kern = _drop_last_input(kern, len(args))
    return pl.pallas_call(
        kern,
        grid=(n_blocks,),
        in_specs=in_specs,
        out_specs=pl.BlockSpec((rows, D_MODEL), lambda i: (first_block + i, 0)),
        out_shape=jax.ShapeDtypeStruct((N_TOK, D_MODEL), F32),
        scratch_shapes=[pltpu.VMEM((rows, D_MODEL), BF16)],
        input_output_aliases=aliases,
        compiler_params=_params("parallel"),
        name="xattn",
    )(*args)


def _moe_kernel(be_ref, nu_ref, first_ref, slot_ref, nxt_ref,
                x_ref, wg_hbm, wu_hbm, wd_hbm, bg_ref, bu_ref, bd_ref, o_ref,
                wbuf_ref, wgb_ref, wub_ref, wdb_ref, sem, *, layer):
    i = pl.program_id(0)
    active = i < nu_ref[0]

    def copies(e, s):
        return [pltpu.make_async_copy(w.at[layer, e], wbuf_ref.at[s, j], sem.at[s, j])
                for j, w in enumerate((wg_hbm, wu_hbm, wd_hbm))]

    @pl.when(jnp.logical_and(active, first_ref[i] == 1))
    def _():
        s = slot_ref[i]

        @pl.when(i == 0)
        def _():
            for c in copies(be_ref[0], s):
                c.start()

        @pl.when(nxt_ref[i] >= 0)
        def _():
            for c in copies(nxt_ref[i], 1 - s):
                c.start()

        for c in copies(be_ref[i], s):
            c.wait()

        def cast(r, _):
            rs = pl.ds(pl.multiple_of(r * MOE_CAST_ROWS, MOE_CAST_ROWS), MOE_CAST_ROWS)
            wgb_ref[rs, :] = wbuf_ref[s, 0, rs, :].astype(BF16)
            wub_ref[rs, :] = wbuf_ref[s, 1, rs, :].astype(BF16)
            wdb_ref[rs, :] = wbuf_ref[s, 2, rs, :].astype(BF16)
            return 0
        lax.fori_loop(0, D_MODEL // MOE_CAST_ROWS, cast, 0)

    @pl.when(active)
    def _():
        x = jnp.concatenate([x_ref[:, c].reshape(MOE_ROWS, LANES) for c in range(PIECES_PER_ROW)],
                            axis=1).astype(BF16)
        g = jnp.minimum(_dot(x, wgb_ref[...]) + bg_ref[...], SWIGLU_LIMIT)
        u = jnp.clip(_dot(x, wub_ref[...]) + bu_ref[...], -SWIGLU_LIMIT, SWIGLU_LIMIT)
        a = g * jax.nn.sigmoid(SWIGLU_ALPHA * g) * (u + 1.0)
        o_ref[...] = _dot(a.astype(BF16), wdb_ref[...]) + bd_ref[...]


def _moe_call(layer, xs, sched, w_gate, b_gate, w_up, b_up, w_down, b_down):
    def row_map(i, be, nu, *_):
        return (jnp.minimum(i, nu[0] - 1), 0)

    def piece_map(i, be, nu, *_):
        return (jnp.minimum(i, nu[0] - 1), 0, 0, 0)

    def b_map(i, be, *_):
        return (layer, be[i], 0, 0)

    b_spec = pl.BlockSpec((None, None, 1, D_FF), b_map)
    w_spec = pl.BlockSpec(memory_space=pl.ANY)
    grid_spec = pltpu.PrefetchScalarGridSpec(
        num_scalar_prefetch=5,
        grid=(MOE_BLOCKS,),
        in_specs=[pl.BlockSpec((MOE_ROWS // SUBLANES,) + PIECE_TILE, piece_map),
                  w_spec, w_spec, w_spec, b_spec, b_spec, b_spec],
        out_specs=pl.BlockSpec((MOE_ROWS, D_MODEL), row_map),
        scratch_shapes=[pltpu.VMEM((2, 3, D_MODEL, D_FF), F32)]
        + [pltpu.VMEM((D_MODEL, D_FF), BF16)] * 3
        + [pltpu.SemaphoreType.DMA((2, 3))],
    )
    shape4 = (DEPTH, N_EXPERTS, 1, D_FF)
    return pl.pallas_call(
        functools.partial(_moe_kernel, layer=layer),
        grid_spec=grid_spec,
        out_shape=jax.ShapeDtypeStruct((MOE_BLOCKS * MOE_ROWS, D_MODEL), F32),
        compiler_params=pltpu.CompilerParams(dimension_semantics=("arbitrary",),
                                             vmem_limit_bytes=56 * 1024 * 1024),
        name="moe_experts",
    )(*sched, xs, w_gate, w_up, w_down, b_gate.reshape(shape4), b_up.reshape(shape4),
      b_down.reshape(shape4))


DISPATCH_TOKENS = 48
DMA_PIECES = 128
DISPATCH_DMAS = DISPATCH_TOKENS * PIECES_PER_ROW // DMA_PIECES
DISPATCH_CHUNKS = N_TOK // SC_WORKERS // DISPATCH_TOKENS
IDX_ROWS = 16
SC_LANES = 16
MOE_ROWS_TOTAL = MOE_BLOCKS * MOE_ROWS
TILE_PIECES = PIECES_PER_ROW * SUBLANES


def _piece_index_rows(p_v, idx_v, row0, tok0, n_dma):
    lane = lax.iota(jnp.int32, SC_LANES)
    in_tile = lane & (SUBLANES - 1)
    piece = (lane >> 3) * SUBLANES
    per_vec = SC_LANES // SUBLANES
    tiles_per_dma = DMA_PIECES // TILE_PIECES
    for m in range(n_dma):
        for tr in range(tiles_per_dma):
            first = plsc.load_gather(p_v, [in_tile + (tok0 + (m * tiles_per_dma + tr) * SUBLANES)])
            for v in range(PIECES_PER_ROW // per_vec):
                idx_v[row0 + m, pl.ds(tr * TILE_PIECES + v * SC_LANES, SC_LANES)] = (
                    first + (piece + v * per_vec * SUBLANES))


def _dispatch(h4, dest_t):
    first_piece = ((dest_t // SUBLANES) * TILE_PIECES + dest_t % SUBLANES).reshape(-1)
    chunk_pieces = DISPATCH_TOKENS * PIECES_PER_ROW
    mesh = plsc.VectorSubcoreMesh(core_axis_name="c", subcore_axis_name="s")

    @functools.partial(
        pl.kernel, mesh=mesh,
        out_type=jax.ShapeDtypeStruct((MOE_ROWS_TOTAL * PIECES_PER_ROW, LANES), F32),
        scratch_types=[pltpu.VMEM((chunk_pieces, LANES), F32), pltpu.VMEM((IDX_ROWS, DMA_PIECES), jnp.int32),
                       pltpu.VMEM((TOP_K * DISPATCH_TOKENS,), jnp.int32), pltpu.SemaphoreType.DMA],
        compiler_params=pltpu.CompilerParams(needs_layout_passes=False),
    )
    def scatter_rows(h_hbm, p_hbm, o_hbm, rows_v, idx_v, p_v, sem):
        wid = lax.axis_index("s") * SC_CORES + lax.axis_index("c")

        @pl.loop(0, DISPATCH_CHUNKS)
        def _(j):
            q = wid * DISPATCH_CHUNKS + j
            pltpu.sync_copy(h_hbm.at[pl.ds(pl.multiple_of(q * chunk_pieces, SUBLANES), chunk_pieces)], rows_v)
            for k in range(TOP_K):
                pltpu.sync_copy(
                    p_hbm.at[pl.ds(pl.multiple_of(k * N_TOK + q * DISPATCH_TOKENS, SUBLANES), DISPATCH_TOKENS)],
                    p_v.at[pl.ds(k * DISPATCH_TOKENS, DISPATCH_TOKENS)])
            for k in range(TOP_K):
                _piece_index_rows(p_v, idx_v, k * DISPATCH_DMAS, k * DISPATCH_TOKENS, DISPATCH_DMAS)
            copies = [pltpu.async_copy(rows_v.at[pl.ds(m * DMA_PIECES, DMA_PIECES)],
                                       o_hbm.at[idx_v.at[k * DISPATCH_DMAS + m]], sem)
                      for k in range(TOP_K) for m in range(DISPATCH_DMAS)]
            for cp in copies:
                cp.wait()

    xs = scatter_rows(h4.reshape(N_TOK * PIECES_PER_ROW, LANES), first_piece)
    return xs.reshape((MOE_ROWS_TOTAL // SUBLANES,) + PIECE_TILE)


def _route(e_t, cnt):
    blk_cnt = cnt[:, :, 0].astype(jnp.int32)
    counts = jnp.sum(blk_cnt, axis=0)
    padded = ((counts + MOE_ROWS - 1) // MOE_ROWS) * MOE_ROWS
    pend = jnp.cumsum(padded)
    pstart = pend - padded
    base = pstart[None, :] + jnp.cumsum(blk_cnt, axis=0) - blk_cnt
    dest_t = _dest(e_t, base.astype(F32)[:, :, None])
    n_used = (pend[-1] // MOE_ROWS).astype(jnp.int32)
    blk = jnp.minimum(jnp.arange(MOE_BLOCKS, dtype=jnp.int32), n_used - 1) * MOE_ROWS
    blk_exp = jnp.minimum(jnp.sum((pend[None, :] <= blk[:, None]).astype(jnp.int32), axis=1), N_EXPERTS - 1)
    ids = jnp.arange(MOE_BLOCKS, dtype=jnp.int32)
    first = jnp.logical_and(ids < n_used, jnp.logical_or(ids == 0, blk_exp != jnp.roll(blk_exp, 1)))
    first = first.astype(jnp.int32)
    slot = (jnp.cumsum(first) - 1) % 2
    eid = jnp.arange(N_EXPERTS, dtype=jnp.int32)
    later = jnp.logical_and(eid[None, :] > eid[:, None], (counts > 0)[None, :])
    nxt_e = jnp.min(jnp.where(later, eid[None, :], N_EXPERTS), axis=1)
    nxt = jnp.where(nxt_e < N_EXPERTS, nxt_e, -1)[blk_exp].astype(jnp.int32)
    return dest_t, (blk_exp, n_used.reshape(1), first, slot.astype(jnp.int32), nxt)


def kernel(x_prompt, x_sample, mem_prompt, state_s5_re, state_s5_im, cache_mem_k, cache_mem_v, norm_mix_g, w_in, s5_lambda_re, s5_lambda_im, s5_b_re, s5_b_im, s5_c_re, s5_c_im, s5_d, s5_log_step, s5_w_glu, gm_ln_g, gm_ln_b, gm_w_s, gm_b_s, out_norm_g, w_out, xa_norm_g, mem_norm_g, w_xq, w_xk, w_xv, w_xo, ffn_norm_g, router_w, router_b, e_w_gate, e_b_gate, e_w_up, e_b_up, e_w_down, e_b_down, final_norm_g):
    xs = (x_prompt.reshape(N_PROMPT, D_MODEL), x_sample.reshape(N_SAMPLE, D_MODEL))
    mem = mem_prompt.reshape(BATCH * N_MEM, D_MODEL)
    outs = {k: [] for k in ('s5r_p', 's5i_p', 'mk_p', 'mv_p', 's5r_s', 's5i_s', 'gmv_s')}
    for l in range(DEPTH):
        proj = _norm_matmul(xs, norm_mix_g[l], w_in[l].astype(BF16))
        sw = _s5_weights(s5_lambda_re[l], s5_lambda_im[l], s5_b_re[l], s5_b_im[l], s5_c_re[l], s5_c_im[l],
                         s5_d[l], s5_log_step[l], S5_SEG_PROMPT)
        w_glu = s5_w_glu[l].astype(BF16)
        og = out_norm_g[l]
        ms5, er_p, ei_p = _s5_call(proj, sw, w_glu, og[:S5_WIDTH], S5_SEG_PROMPT, True)
        h0 = (state_s5_re[l].reshape(DEC_BATCH // SUBLANES, SUBLANES, S5_COLS),
              state_s5_im[l].reshape(DEC_BATCH // SUBLANES, SUBLANES, S5_COLS))
        ms5, er_s, ei_s = _s5_call(proj, sw, w_glu, og[:S5_WIDTH], DEC_SEQ, False, h0=h0, prev=ms5)
        outs['s5r_p'].append(er_p[:, SUBLANES - 1].reshape(BATCH, S5_GROUPS, S5_STATE))
        outs['s5i_p'].append(ei_p[:, SUBLANES - 1].reshape(BATCH, S5_GROUPS, S5_STATE))
        outs['s5r_s'].append(er_s.reshape(DEC_BATCH, S5_GROUPS, S5_STATE))
        outs['s5i_s'].append(ei_s.reshape(DEC_BATCH, S5_GROUPS, S5_STATE))

        (mgm,) = _gmlp_call(proj, gm_ln_g[l], gm_ln_b[l], gm_w_s[l], gm_b_s[l], og[S5_WIDTH:],
                            GM_ROWS_PROMPT, GM_CHUNK, 0, N_PROMPT // GM_ROWS_PROMPT, False)
        mgm, vn = _gmlp_call(proj, gm_ln_g[l], gm_ln_b[l], gm_w_s[l], gm_b_s[l], og[S5_WIDTH:],
                             N_SAMPLE, DEC_SEQ, N_PROMPT // N_SAMPLE, 1, True, prev=mgm)
        outs['gmv_s'].append(vn.reshape(DEC_BATCH, DEC_SEQ, GM_HEADS, GM_HEAD_DIM))
        x = _out_proj(xs, ms5, mgm, w_out[l].astype(BF16))

        mk = _norm_matmul((mem,), mem_norm_g[l], w_xk[l].astype(BF16))
        mv = _norm_matmul((mem,), mem_norm_g[l], w_xv[l].astype(BF16))
        outs['mk_p'].append(mk.reshape(BATCH, N_MEM, XA_HEADS, XA_HEAD_DIM))
        outs['mv_p'].append(mv.reshape(BATCH, N_MEM, XA_HEADS, XA_HEAD_DIM))
        wq = w_xq[l].astype(BF16)
        wo = w_xo[l].astype(BF16)
        xa = _xattn_call(x, xa_norm_g[l], wq, mk.reshape(BATCH, N_MEM, D_MODEL).astype(BF16),
                         mv.reshape(BATCH, N_MEM, D_MODEL).astype(BF16), wo,
                         XA_ROWS_PROMPT, 1, 0, N_PROMPT // XA_ROWS_PROMPT, SEQ // XA_ROWS_PROMPT)
        rows_s = XA_SAMPLE_BATCHES * DEC_SEQ
        x = _xattn_call(x, xa_norm_g[l], wq,
                        cache_mem_k[l].reshape(DEC_BATCH, N_MEM, D_MODEL).astype(BF16),
                        cache_mem_v[l].reshape(DEC_BATCH, N_MEM, D_MODEL).astype(BF16), wo,
                        rows_s, XA_SAMPLE_BATCHES, N_PROMPT // rows_s, N_SAMPLE // rows_s, 1, prev=xa)

        h, e_t, gates_t, cnt = _router(x, ffn_norm_g[l], router_w[l], router_b[l])
        dest_t, sched = _route(e_t, cnt)
        ys = _moe_call(l, _dispatch(h, dest_t), sched, e_w_gate, e_b_gate, e_w_up, e_b_up, e_w_down, e_b_down)
        x = _combine(x, ys[dest_t.reshape(-1)], gates_t.T, final_norm_g if l == DEPTH - 1 else None)
        xs = (x,)

    y_prompt = x[0].reshape(BATCH, SEQ, D_MODEL)
    y_sample = x[1].reshape(DEC_BATCH, DEC_SEQ, D_MODEL)
    st = jnp.stack
    return (y_prompt, y_sample, st(outs['s5r_p']), st(outs['s5i_p']), st(outs['mk_p']), st(outs['mv_p']),
            st(outs['s5r_s']), st(outs['s5i_s']), st(outs['gmv_s']))
```

```python
import functools

import jax
import jax.numpy as jnp
from jax import lax
from jax.experimental import pallas as pl
from jax.experimental.pallas import tpu as pltpu
from jax.experimental.pallas import tpu_sc as plsc

F32 = jnp.float32
BF16 = jnp.bfloat16

D_MODEL = 1024
BATCH = 4
SEQ = 4096
DEPTH = 2
DEC_BATCH = 16
DEC_SEQ = 32
S5_GROUPS = 32
S5_GROUP_CH = 16
S5_STATE = 64
S5_WIDTH = S5_GROUPS * S5_GROUP_CH
S5_COLS = S5_GROUPS * S5_STATE
GM_HEADS = 4
GM_HEAD_DIM = 128
GM_WIDTH = GM_HEADS * GM_HEAD_DIM
GM_CHUNK = 128
IN_PROJ = S5_WIDTH + 2 * GM_WIDTH
N_MEM = 256
XA_HEADS = 4
XA_HEAD_DIM = D_MODEL // XA_HEADS
N_EXPERTS = 32
TOP_K = 4
D_FF = D_MODEL
SWIGLU_LIMIT = 7.0
SWIGLU_ALPHA = 1.702
EPS = 1e-6
EIG_CLIP = -1e-4

N_PROMPT = BATCH * SEQ
N_SAMPLE = DEC_BATCH * DEC_SEQ
N_TOK = N_PROMPT + N_SAMPLE

SUBLANES = 8
LANES = 128
MXU_DIM = 256
VMEM_LIMIT = 48 * 1024 * 1024
SC_CORES = 2
SC_SUBCORES = 16
SC_WORKERS = SC_CORES * SC_SUBCORES

PIECE_TILE = (D_MODEL // LANES, SUBLANES, LANES)
PIECES_PER_ROW = D_MODEL // LANES

ROW_BLOCK = 512
S5_SEG_PROMPT = 64
S5_COL_BLOCK = 512
GM_ROWS_PROMPT = 256
XA_ROWS_PROMPT = 512
XA_SAMPLE_BATCHES = 4
MOE_ROWS = 512
MOE_BLOCKS = -(-N_TOK * TOP_K // MOE_ROWS) + N_EXPERTS
MOE_CAST_ROWS = 128


def _params(*sem):
    return pltpu.CompilerParams(dimension_semantics=sem, vmem_limit_bytes=VMEM_LIMIT)


def _rms(x, g):
    return x * lax.rsqrt(jnp.mean(x * x, axis=-1, keepdims=True) + EPS) * g


def _dot(a, b):
    return jnp.dot(a, b, preferred_element_type=F32)


PROMPT_BLOCKS = N_PROMPT // ROW_BLOCK
SAMPLE_BLOCKS = N_SAMPLE // ROW_BLOCK


def _token_sources(xs):
    if len(xs) == 1:
        return [pl.BlockSpec((ROW_BLOCK, D_MODEL), lambda i: (i, 0))]
    return [pl.BlockSpec((ROW_BLOCK, D_MODEL), lambda i: (jnp.minimum(i, PROMPT_BLOCKS - 1), 0)),
            pl.BlockSpec((ROW_BLOCK, D_MODEL), lambda i: (jnp.maximum(i - PROMPT_BLOCKS, 0), 0))]


def _for_group(refs, body):
    if len(refs) == 1:
        body(refs[0])
        return
    i = pl.program_id(0)
    pl.when(i < PROMPT_BLOCKS)(lambda: body(refs[0]))
    pl.when(i >= PROMPT_BLOCKS)(lambda: body(refs[1]))


def _norm_matmul_kernel(*refs, n_src):
    g_ref, w_ref, o_ref = refs[n_src:]

    def body(x_ref):
        o_ref[...] = _dot(_rms(x_ref[...], g_ref[...]).astype(BF16), w_ref[...])
    _for_group(refs[:n_src], body)


def _norm_matmul(xs, g, w):
    d = D_MODEL
    n = sum(x.shape[0] for x in xs)
    dout = w.shape[1]
    return pl.pallas_call(
        functools.partial(_norm_matmul_kernel, n_src=len(xs)),
        grid=(n // ROW_BLOCK,),
        in_specs=_token_sources(xs) + [pl.BlockSpec((1, d), lambda i: (0, 0)),
                                       pl.BlockSpec((d, dout), lambda i: (0, 0))],
        out_specs=pl.BlockSpec((ROW_BLOCK, dout), lambda i: (i, 0)),
        out_shape=jax.ShapeDtypeStruct((n, dout), F32),
        compiler_params=_params("parallel"),
        name="norm_matmul",
    )(*xs, g.reshape(1, d), w)


def _route_rows(x, g_ref, wt_ref, b_ref, h_ref, e_ref, gate_ref, cnt_ref):
    h = _rms(x, g_ref[...])
    for c in range(PIECES_PER_ROW):
        h_ref[:, c] = h[:, c * LANES:(c + 1) * LANES].reshape(h.shape[0] // SUBLANES, SUBLANES, LANES)
    logits = lax.dot_general(wt_ref[...], h.astype(BF16), (((1,), (1,)), ((), ())),
                             preferred_element_type=F32) + b_ref[...]
    sub = lax.broadcasted_iota(jnp.int32, logits.shape, 0)
    work = logits
    chosen = jnp.zeros(logits.shape, F32)
    vals, idxs = [], []
    for _ in range(TOP_K):
        m = jnp.max(work, axis=0, keepdims=True)
        idx = jnp.min(jnp.where(work == m, sub, N_EXPERTS), axis=0, keepdims=True)
        sel = sub == idx
        vals.append(m)
        idxs.append(idx)
        work = jnp.where(sel, -jnp.inf, work)
        chosen = jnp.where(sel, 1.0, chosen)
    ex = [jnp.exp(v - vals[0]) for v in vals]
    den = ex[0] + ex[1] + ex[2] + ex[3]
    gate_ref[...] = jnp.concatenate([e / den for e in ex], axis=0)
    e_ref[...] = jnp.concatenate(idxs, axis=0)
    cnt_ref[...] = jnp.broadcast_to(jnp.sum(chosen, axis=1, keepdims=True), (N_EXPERTS, LANES))


def _dest_kernel(e_ref, base_ref, tri_ref, d_ref):
    e = e_ref[...]
    sub = lax.broadcasted_iota(jnp.int32, (N_EXPERTS, e.shape[1]), 0)
    hot = [sub == e[k:k + 1, :] for k in range(TOP_K)]
    chosen = sum(jnp.where(hk, 1.0, 0.0) for hk in hot)
    pos = base_ref[...] + _dot(chosen.astype(BF16), tri_ref[...])
    d_ref[...] = jnp.concatenate(
        [jnp.sum(jnp.where(hk, pos, 0.0), axis=0, keepdims=True) for hk in hot], axis=0).astype(jnp.int32)


def _dest(e_t, base):
    n = e_t.shape[1]
    rows = ROW_BLOCK
    tri = jnp.triu(jnp.ones((rows, rows), F32), 1).astype(BF16)
    return pl.pallas_call(
        _dest_kernel,
        grid=(n // rows,),
        in_specs=[pl.BlockSpec((TOP_K, rows), lambda i: (0, i)),
                  pl.BlockSpec((None, N_EXPERTS, 1), lambda i: (i, 0, 0)),
                  pl.BlockSpec((rows, rows), lambda i: (0, 0))],
        out_specs=pl.BlockSpec((TOP_K, rows), lambda i: (0, i)),
        out_shape=jax.ShapeDtypeStruct((TOP_K, n), jnp.int32),
        compiler_params=_params("parallel"),
        name="moe_dest",
    )(e_t, base, tri)


def _combine_kernel(*refs, final):
    if final:
        x_ref, g0, g1, g2, g3, gate_ref, fg_ref, op_ref, os_ref = refs
    else:
        x_ref, g0, g1, g2, g3, gate_ref, o_ref = refs
    acc = x_ref[...]
    for k, g in enumerate((g0, g1, g2, g3)):
        acc = acc + gate_ref[:, k:k + 1] * g[...]
    if final:
        def store(o_ref):
            o_ref[...] = _rms(acc, fg_ref[...])
        _for_group((op_ref, os_ref), store)
    else:
        o_ref[...] = acc


def _combine(x, picked, gates, final_g=None):
    n, d = x.shape
    rows = ROW_BLOCK
    picked = picked.reshape(TOP_K, n, d)
    final = final_g is not None
    in_specs = [pl.BlockSpec((rows, d), lambda i: (i, 0))]
    in_specs += [pl.BlockSpec((None, rows, d), functools.partial(lambda i, k: (k, i, 0), k=k))
                 for k in range(TOP_K)]
    in_specs.append(pl.BlockSpec((rows, TOP_K), lambda i: (i, 0)))
    args = [x, picked, picked, picked, picked, gates]
    out_specs = pl.BlockSpec((rows, d), lambda i: (i, 0))
    out_shape = jax.ShapeDtypeStruct((n, d), F32)
    if final:
        in_specs.append(pl.BlockSpec((1, d), lambda i: (0, 0)))
        args.append(final_g.reshape(1, d))
        out_specs = _token_sources((None, None))
        out_shape = [jax.ShapeDtypeStruct((N_PROMPT, d), F32), jax.ShapeDtypeStruct((N_SAMPLE, d), F32)]
    return pl.pallas_call(
        functools.partial(_combine_kernel, final=final),
        grid=(n // rows,),
        in_specs=in_specs,
        out_specs=out_specs,
        out_shape=out_shape,
        compiler_params=_params("arbitrary" if final else "parallel"),
        name="moe_combine",
    )(*args)


def _cmul_add(ar, ai, hr, hi, br, bi):
    return ar * hr - ai * hi + br, ar * hi + ai * hr + bi


def _s5_kernel(*refs, seg, chained):
    if chained:
        (u_ref, wbr_ref, wbi_ref, wcr_ref, wci_ref, ar_ref, ai_ref, pr_ref, pi_ref, d_ref,
         wglu_ref, og_ref, o_ref, er_ref, ei_ref,
         up_ref, hr_ref, hi_ref, sr_ref, si_ref, op_ref, cr_ref, ci_ref) = refs
    else:
        (u_ref, wbr_ref, wbi_ref, wcr_ref, wci_ref, ar_ref, ai_ref, pr_ref, pi_ref, d_ref,
         wglu_ref, og_ref, h0r_ref, h0i_ref, o_ref, er_ref, ei_ref,
         up_ref, hr_ref, hi_ref, sr_ref, si_ref, op_ref) = refs
    rows = seg * SUBLANES
    half = S5_COLS // 2

    n_lane_blk = S5_WIDTH // LANES
    for j in range(SUBLANES):
        for i0 in range(0, seg, SUBLANES):
            for c in range(n_lane_blk):
                up_ref[c, pl.ds(i0 * SUBLANES + j, SUBLANES, stride=SUBLANES), :] = (
                    u_ref[j * seg + i0:j * seg + i0 + SUBLANES, c * LANES:(c + 1) * LANES])

    up = jnp.concatenate([up_ref[c] for c in range(n_lane_blk)], axis=1)
    ub = up.astype(BF16)
    for k in range(2):
        uk = ub[:, k * MXU_DIM:(k + 1) * MXU_DIM]
        hr_ref[:, k * half:(k + 1) * half] = _dot(uk, wbr_ref[k])
        hi_ref[:, k * half:(k + 1) * half] = _dot(uk, wbi_ref[k])

    for cb in range(S5_COLS // S5_COL_BLOCK):
        cols = slice(cb * S5_COL_BLOCK, (cb + 1) * S5_COL_BLOCK)
        ar = jnp.broadcast_to(ar_ref[:, cols], (SUBLANES, S5_COL_BLOCK))
        ai = jnp.broadcast_to(ai_ref[:, cols], (SUBLANES, S5_COL_BLOCK))

        def step(i, carry, cols=cols, ar=ar, ai=ai):
            r = pl.multiple_of(i * SUBLANES, SUBLANES)
            h_r, h_i = _cmul_add(ar, ai, carry[0], carry[1],
                                 hr_ref[pl.ds(r, SUBLANES), cols], hi_ref[pl.ds(r, SUBLANES), cols])
            hr_ref[pl.ds(r, SUBLANES), cols] = h_r
            hi_ref[pl.ds(r, SUBLANES), cols] = h_i
            return h_r, h_i

        zero = jnp.zeros((SUBLANES, S5_COL_BLOCK), F32)
        lax.fori_loop(0, seg, step, (zero, zero), unroll=2)

    end_r = hr_ref[rows - SUBLANES:rows, :]
    end_i = hi_ref[rows - SUBLANES:rows, :]
    as_r = pr_ref[rows - 1:rows, :]
    as_i = pi_ref[rows - 1:rows, :]
    if chained:
        @pl.when(pl.program_id(1) == 0)
        def _():
            cr_ref[...] = jnp.zeros_like(cr_ref)
            ci_ref[...] = jnp.zeros_like(ci_ref)

        cur_r, cur_i = cr_ref[...], ci_ref[...]
        for j in range(SUBLANES):
            sr_ref[j:j + 1, :] = cur_r
            si_ref[j:j + 1, :] = cur_i
            cur_r, cur_i = _cmul_add(as_r, as_i, cur_r, cur_i, end_r[j:j + 1], end_i[j:j + 1])
        cr_ref[...] = cur_r
        ci_ref[...] = cur_i
    else:
        sr_ref[...] = h0r_ref[...]
        si_ref[...] = h0i_ref[...]
    t_r, t_i = _cmul_add(as_r, as_i, sr_ref[...], si_ref[...], end_r, end_i)
    er_ref[...] = t_r
    ei_ref[...] = t_i

    slabs_per_window = MXU_DIM // (MXU_DIM // S5_STATE * S5_GROUP_CH)
    ys = []
    for w in range(S5_WIDTH // MXU_DIM):
        acc = d_ref[:, w * MXU_DIM:(w + 1) * MXU_DIM] * jnp.concatenate(
            [up_ref[c] for c in range(w * MXU_DIM // LANES, (w + 1) * MXU_DIM // LANES)], axis=1)
        for s in range(slabs_per_window):
            cols = slice((w * slabs_per_window + s) * MXU_DIM, (w * slabs_per_window + s + 1) * MXU_DIM)
            shape3 = (seg, SUBLANES, MXU_DIM)
            h_r, h_i = _cmul_add(pr_ref[:, cols].reshape(shape3), pi_ref[:, cols].reshape(shape3),
                                 sr_ref[:, cols][None], si_ref[:, cols][None],
                                 hr_ref[:, cols].reshape(shape3), hi_ref[:, cols].reshape(shape3))
            acc += _dot(h_r.reshape(rows, MXU_DIM).astype(BF16), wcr_ref[w * slabs_per_window + s])
            acc += _dot(h_i.reshape(rows, MXU_DIM).astype(BF16), wci_ref[w * slabs_per_window + s])
        ys.append(jax.nn.gelu(acc).astype(BF16))

    z = _dot(jnp.concatenate(ys, axis=1), wglu_ref[...])
    o = _rms(z[:, :S5_WIDTH] * jax.nn.sigmoid(z[:, S5_WIDTH:]), og_ref[...])
    for c in range(n_lane_blk):
        op_ref[c] = o[:, c * LANES:(c + 1) * LANES]

    for j in range(SUBLANES):
        for i0 in range(0, seg, 2 * SUBLANES):
            for c in range(n_lane_blk):
                o_ref[j * seg + i0:j * seg + i0 + 2 * SUBLANES, c * LANES:(c + 1) * LANES] = (
                    op_ref[c, pl.ds(i0 * SUBLANES + j, 2 * SUBLANES, stride=SUBLANES), :].astype(BF16))


def _s5_weights(lam_re, lam_im, b_re, b_im, c_re, c_im, d, log_step, seg):
    step = jnp.exp(log_step)[:, None]
    lr = jnp.minimum(lam_re, EIG_CLIP)
    li = lam_im
    mag = jnp.exp(lr * step)
    ar = mag * jnp.cos(li * step)
    ai = mag * jnp.sin(li * step)
    den = lr * lr + li * li
    fr = ((ar - 1.0) * lr + ai * li) / den
    fi = (ai * lr - (ar - 1.0) * li) / den
    bbar_r = fr[..., None] * b_re - fi[..., None] * b_im
    bbar_i = fr[..., None] * b_im + fi[..., None] * b_re
    t = jnp.arange(1, seg + 1, dtype=F32)[:, None, None]
    pm = jnp.exp(lr * step * t)
    p_r = jnp.repeat((pm * jnp.cos(li * step * t)).reshape(seg, S5_COLS), SUBLANES, axis=0)
    p_i = jnp.repeat((pm * jnp.sin(li * step * t)).reshape(seg, S5_COLS), SUBLANES, axis=0)

    gk = MXU_DIM // S5_GROUP_CH

    def in_map(bbar):
        bb = bbar.reshape(S5_GROUPS // gk, gk, S5_STATE, S5_GROUP_CH)
        w = jnp.einsum('kgnh,gG->kghGn', bb, jnp.eye(gk, dtype=F32))
        return w.reshape(S5_GROUPS // gk, MXU_DIM, gk * S5_STATE).astype(BF16)

    gs = MXU_DIM // S5_STATE
    n_slab = S5_COLS // MXU_DIM
    per_win = MXU_DIM // (gs * S5_GROUP_CH)

    def out_map(c):
        cc = c.reshape(n_slab // per_win, per_win, gs, S5_GROUP_CH, S5_STATE)
        w = jnp.einsum('wsghn,sS,gG->wsgnSGh', cc, jnp.eye(per_win, dtype=F32), jnp.eye(gs, dtype=F32))
        return w.reshape(n_slab, MXU_DIM, MXU_DIM).astype(BF16)

    return dict(wbr=in_map(bbar_r), wbi=in_map(bbar_i), wcr=out_map(c_re), wci=out_map(-c_im),
                ar=ar.reshape(1, S5_COLS), ai=ai.reshape(1, S5_COLS), pr=p_r, pi=p_i,
                d=d.reshape(1, S5_WIDTH))


def _s5_call(proj, sw, w_glu, og, seg, chained, h0=None, prev=None):
    rows = seg * SUBLANES
    const2 = lambda *_: (0, 0)
    const3 = lambda *_: (0, 0, 0)
    if chained:
        n_chunk = SEQ // rows
        grid = (BATCH, n_chunk)
        row_map = lambda b, c: (b * n_chunk + c, 0)
        end_map = lambda b, c: (b, 0, 0)
        n_end = BATCH
        sem = ("parallel", "arbitrary")
    else:
        grid = (DEC_BATCH // SUBLANES,)
        row_map = lambda k: (N_PROMPT // rows + k, 0)
        end_map = lambda k: (k, 0, 0)
        n_end = DEC_BATCH // SUBLANES
        sem = ("parallel",)
    half = S5_COLS // 2
    in_specs = [pl.BlockSpec((rows, S5_WIDTH), row_map),
                pl.BlockSpec((2, MXU_DIM, half), const3),
                pl.BlockSpec((2, MXU_DIM, half), const3),
                pl.BlockSpec((S5_COLS // MXU_DIM, MXU_DIM, MXU_DIM), const3),
                pl.BlockSpec((S5_COLS // MXU_DIM, MXU_DIM, MXU_DIM), const3),
                pl.BlockSpec((1, S5_COLS), const2),
                pl.BlockSpec((1, S5_COLS), const2),
                pl.BlockSpec((rows, S5_COLS), const2),
                pl.BlockSpec((rows, S5_COLS), const2),
                pl.BlockSpec((1, S5_WIDTH), const2),
                pl.BlockSpec((S5_WIDTH, 2 * S5_WIDTH), const2),
                pl.BlockSpec((1, S5_WIDTH), const2)]
    args = [proj, sw['wbr'], sw['wbi'], sw['wcr'], sw['wci'], sw['ar'], sw['ai'],
            sw['pr'][:rows], sw['pi'][:rows], sw['d'], w_glu, og.reshape(1, S5_WIDTH)]
    scratch = [pltpu.VMEM((S5_WIDTH // LANES, rows, LANES), F32),
               pltpu.VMEM((rows, S5_COLS), F32), pltpu.VMEM((rows, S5_COLS), F32),
               pltpu.VMEM((SUBLANES, S5_COLS), F32), pltpu.VMEM((SUBLANES, S5_COLS), F32),
               pltpu.VMEM((S5_WIDTH // LANES, rows, LANES), F32)]
    aliases = {}
    if chained:
        scratch += [pltpu.VMEM((1, S5_COLS), F32), pltpu.VMEM((1, S5_COLS), F32)]
    else:
        in_specs += [pl.BlockSpec((None, SUBLANES, S5_COLS), end_map),
                     pl.BlockSpec((None, SUBLANES, S5_COLS), end_map)]
        args += [h0[0], h0[1]]
    if prev is not None:
        in_specs.append(pl.BlockSpec(memory_space=pl.ANY))
        args.append(prev)
        aliases = {len(args) - 1: 0}
    kern = functools.partial(_s5_kernel, seg=seg, chained=chained)
    if prev is not None:
        kern = _drop_inputs(kern, len(args) - 1, 1)
    return pl.pallas_call(
        kern,
        grid=grid,
        in_specs=in_specs,
        out_specs=[pl.BlockSpec((rows, S5_WIDTH), row_map),
                   pl.BlockSpec((None, SUBLANES, S5_COLS), end_map),
                   pl.BlockSpec((None, SUBLANES, S5_COLS), end_map)],
        out_shape=[jax.ShapeDtypeStruct((N_TOK, S5_WIDTH), BF16),
                   jax.ShapeDtypeStruct((n_end, SUBLANES, S5_COLS), F32),
                   jax.ShapeDtypeStruct((n_end, SUBLANES, S5_COLS), F32)],
        scratch_shapes=scratch,
        input_output_aliases=aliases,
        compiler_params=_params(*sem),
        name="s5_chained" if chained else "s5_streams",
    )(*args)


def _drop_inputs(kern, start, n):
    def wrapped(*refs):
        return kern(*refs[:start], *refs[start + n:])
    return wrapped


def _gmlp_kernel(*refs, chunk, emit_vn):
    if emit_vn:
        u_ref, v_ref, lng_ref, lnb_ref, ws_ref, bst_ref, og_ref, o_ref, vn_ref, acc_ref = refs
    else:
        u_ref, v_ref, lng_ref, lnb_ref, ws_ref, bst_ref, og_ref, o_ref, acc_ref = refs
    rows = u_ref.shape[0]
    u = jax.nn.gelu(u_ref[...])
    v = jax.nn.gelu(v_ref[...])
    tri = (lax.broadcasted_iota(jnp.int32, (chunk, chunk), 0)
           >= lax.broadcasted_iota(jnp.int32, (chunk, chunk), 1))
    for h in range(GM_HEADS):
        cols = slice(h * GM_HEAD_DIM, (h + 1) * GM_HEAD_DIM)
        vh = v[:, cols]
        mu = jnp.mean(vh, axis=-1, keepdims=True)
        cen = vh - mu
        var = jnp.mean(cen * cen, axis=-1, keepdims=True)
        vn = cen * lax.rsqrt(var + EPS) * lng_ref[:, cols] + lnb_ref[:, cols]
        if emit_vn:
            vn_ref[:, cols] = vn
        w = jnp.where(tri, ws_ref[h], 0.0).astype(BF16)
        vb = vn.astype(BF16)
        bias = bst_ref[:, h:h + 1]
        for c in range(rows // chunk):
            rs = slice(c * chunk, (c + 1) * chunk)
            acc_ref[rs, cols] = u[rs, cols] * (_dot(w, vb[rs]) + bias)
    o_ref[...] = _rms(acc_ref[...], og_ref[...]).astype(BF16)


def _gmlp_call(proj, ln_g, ln_b, w_s, b_s, og, rows, chunk, first_block, n_blocks, emit_vn, prev=None):
    const2 = lambda i: (0, 0)
    in_specs = [pl.BlockSpec((rows, GM_WIDTH), lambda i: (first_block + i, 1)),
                pl.BlockSpec((rows, GM_WIDTH), lambda i: (first_block + i, 2)),
                pl.BlockSpec((1, GM_WIDTH), const2),
                pl.BlockSpec((1, GM_WIDTH), const2),
                pl.BlockSpec((GM_HEADS, chunk, chunk), lambda i: (0, 0, 0)),
                pl.BlockSpec((chunk, GM_HEADS), const2),
                pl.BlockSpec((1, GM_WIDTH), const2)]
    args = [proj, proj, ln_g.reshape(1, GM_WIDTH), ln_b.reshape(1, GM_WIDTH),
            w_s[:, :chunk, :chunk], b_s[:, :chunk].T, og.reshape(1, GM_WIDTH)]
    out_specs = [pl.BlockSpec((rows, GM_WIDTH), lambda i: (first_block + i, 0))]
    out_shape = [jax.ShapeDtypeStruct((N_TOK, GM_WIDTH), BF16)]
    if emit_vn:
        out_specs.append(pl.BlockSpec((rows, GM_WIDTH), lambda i: (i, 0)))
        out_shape.append(jax.ShapeDtypeStruct((n_blocks * rows, GM_WIDTH), F32))
    kern = functools.partial(_gmlp_kernel, chunk=chunk, emit_vn=emit_vn)
    aliases = {}
    if prev is not None:
        in_specs.append(pl.BlockSpec(memory_space=pl.ANY))
        args.append(prev)
        aliases = {len(args) - 1: 0}
        kern = _drop_inputs(kern, len(args) - 1, 1)
    return pl.pallas_call(
        kern,
        grid=(n_blocks,),
        in_specs=in_specs,
        out_specs=out_specs,
        out_shape=out_shape,
        scratch_shapes=[pltpu.VMEM((rows, GM_WIDTH), F32)],
        input_output_aliases=aliases,
        compiler_params=_params("parallel"),
        name="gmlp_vn" if emit_vn else "gmlp",
    )(*args)


def _post_mix_kernel(x_ref, a_ref, b_ref, wa_ref, wb_ref, xg_ref, wq_ref, k_ref, v_ref, wo_ref,
                     fg_ref, wt_ref, rb_ref, x2_ref, h_ref, e_ref, gate_ref, cnt_ref, att_ref, *, n_batch):
    rows = x_ref.shape[0]
    tb = rows // n_batch
    x = x_ref[...] + _dot(a_ref[...], wa_ref[...]) + _dot(b_ref[...], wb_ref[...])
    q = _dot(_rms(x, xg_ref[...]).astype(BF16), wq_ref[...]).astype(BF16)
    scale = XA_HEAD_DIM ** -0.5
    for b in range(n_batch):
        rs = slice(b * tb, (b + 1) * tb)
        for h in range(XA_HEADS):
            cols = slice(h * XA_HEAD_DIM, (h + 1) * XA_HEAD_DIM)
            s = lax.dot_general(q[rs, cols], k_ref[b, :, cols], (((1,), (1,)), ((), ())),
                                preferred_element_type=F32) * scale
            e = jnp.exp(s - jnp.max(s, axis=-1, keepdims=True))
            p = e / jnp.sum(e, axis=-1, keepdims=True)
            att_ref[rs, cols] = _dot(p.astype(BF16), v_ref[b, :, cols]).astype(BF16)
    x = x + _dot(att_ref[...], wo_ref[...])
    x2_ref[...] = x
    _route_rows(x, fg_ref, wt_ref, rb_ref, h_ref, e_ref, gate_ref, cnt_ref)


def _post_mix(x, x_first, a, b, w_out, xg, wq, k, v, wo, fg, rw, rb,
              rows, n_batch, first_block, n_blocks, blocks_per_kv, prev=None):
    const2 = lambda i: (0, 0)
    row_map = lambda i: (first_block + i, 0)
    d = D_MODEL
    in_specs = [pl.BlockSpec((rows, d), lambda i: (x_first + i, 0)),
                pl.BlockSpec((rows, S5_WIDTH), row_map),
                pl.BlockSpec((rows, GM_WIDTH), row_map),
                pl.BlockSpec((S5_WIDTH, d), const2),
                pl.BlockSpec((GM_WIDTH, d), const2),
                pl.BlockSpec((1, d), const2),
                pl.BlockSpec((d, d), const2),
                pl.BlockSpec((n_batch, N_MEM, d), lambda i: (i // blocks_per_kv, 0, 0)),
                pl.BlockSpec((n_batch, N_MEM, d), lambda i: (i // blocks_per_kv, 0, 0)),
                pl.BlockSpec((d, d), const2),
                pl.BlockSpec((1, d), const2),
                pl.BlockSpec((N_EXPERTS, d), const2),
                pl.BlockSpec((N_EXPERTS, 1), const2)]
    args = [x, a, b, w_out[:S5_WIDTH], w_out[S5_WIDTH:], xg.reshape(1, d), wq, k, v, wo,
            fg.reshape(1, d), rw.T.astype(BF16), rb.reshape(N_EXPERTS, 1)]
    kern = functools.partial(_post_mix_kernel, n_batch=n_batch)
    aliases = {}
    if prev is not None:
        n_in = len(args)
        in_specs += [pl.BlockSpec(memory_space=pl.ANY)] * len(prev)
        args += list(prev)
        aliases = {n_in + o: o for o in range(len(prev))}
        kern = _drop_inputs(kern, n_in, len(prev))
    return pl.pallas_call(
        kern,
        grid=(n_blocks,),
        in_specs=in_specs,
        out_specs=[pl.BlockSpec((rows, d), row_map),
                   pl.BlockSpec((rows // SUBLANES,) + PIECE_TILE, lambda i: (first_block + i, 0, 0, 0)),
                   pl.BlockSpec((TOP_K, rows), lambda i: (0, first_block + i)),
                   pl.BlockSpec((TOP_K, rows), lambda i: (0, first_block + i)),
                   pl.BlockSpec((None, N_EXPERTS, LANES), lambda i: (i, 0, 0))],
        out_shape=[jax.ShapeDtypeStruct((N_TOK, d), F32),
                   jax.ShapeDtypeStruct((N_TOK // SUBLANES,) + PIECE_TILE, F32),
                   jax.ShapeDtypeStruct((TOP_K, N_TOK), jnp.int32),
                   jax.ShapeDtypeStruct((TOP_K, N_TOK), F32),
                   jax.ShapeDtypeStruct((n_blocks, N_EXPERTS, LANES), F32)],
        scratch_shapes=[pltpu.VMEM((rows, d), BF16)],
        input_output_aliases=aliases,
        compiler_params=_params("parallel"),
        name="post_mix",
    )(*args)


def _moe_kernel(be_ref, nu_ref, first_ref, slot_ref, nxt_ref, half_ref,
                x_ref, wg_hbm, wu_hbm, wd_hbm, bg_ref, bu_ref, bd_ref, o_ref,
                wbuf_ref, wgb_ref, wub_ref, wdb_ref, sem, *, layer):
    i = pl.program_id(0)
    active = i < nu_ref[0]

    def copies(e, s):
        return [pltpu.make_async_copy(w.at[layer, e], wbuf_ref.at[s, j], sem.at[s, j])
                for j, w in enumerate((wg_hbm, wu_hbm, wd_hbm))]

    @pl.when(jnp.logical_and(active, first_ref[i] == 1))
    def _():
        s = slot_ref[i]

        @pl.when(i == 0)
        def _():
            for c in copies(be_ref[0], s):
                c.start()

        @pl.when(nxt_ref[i] >= 0)
        def _():
            for c in copies(nxt_ref[i], 1 - s):
                c.start()

        for c in copies(be_ref[i], s):
            c.wait()

        def cast(r, _):
            rs = pl.ds(pl.multiple_of(r * MOE_CAST_ROWS, MOE_CAST_ROWS), MOE_CAST_ROWS)
            wgb_ref[rs, :] = wbuf_ref[s, 0, rs, :].astype(BF16)
            wub_ref[rs, :] = wbuf_ref[s, 1, rs, :].astype(BF16)
            wdb_ref[rs, :] = wbuf_ref[s, 2, rs, :].astype(BF16)
            return 0
        lax.fori_loop(0, D_MODEL // MOE_CAST_ROWS, cast, 0)

    def experts(n_rows):
        x = jnp.concatenate([x_ref[:n_rows // SUBLANES, c].reshape(n_rows, LANES) for c in range(PIECES_PER_ROW)],
                            axis=1).astype(BF16)
        g = jnp.minimum(_dot(x, wgb_ref[...]) + bg_ref[...], SWIGLU_LIMIT)
        u = jnp.clip(_dot(x, wub_ref[...]) + bu_ref[...], -SWIGLU_LIMIT, SWIGLU_LIMIT)
        a = g * jax.nn.sigmoid(SWIGLU_ALPHA * g) * (u + 1.0)
        o_ref[:n_rows, :] = _dot(a.astype(BF16), wdb_ref[...]) + bd_ref[...]

    pl.when(jnp.logical_and(active, half_ref[i] == 0))(lambda: experts(MOE_ROWS))
    pl.when(jnp.logical_and(active, half_ref[i] == 1))(lambda: experts(MOE_ROWS // 2))


def _moe_call(layer, xs, sched, w_gate, b_gate, w_up, b_up, w_down, b_down):
    def row_map(i, be, nu, *_):
        return (jnp.minimum(i, nu[0] - 1), 0)

    def piece_map(i, be, nu, *_):
        return (jnp.minimum(i, nu[0] - 1), 0, 0, 0)

    def b_map(i, be, *_):
        return (layer, be[i], 0, 0)

    b_spec = pl.BlockSpec((None, None, 1, D_FF), b_map)
    w_spec = pl.BlockSpec(memory_space=pl.ANY)
    grid_spec = pltpu.PrefetchScalarGridSpec(
        num_scalar_prefetch=6,
        grid=(MOE_BLOCKS,),
        in_specs=[pl.BlockSpec((MOE_ROWS // SUBLANES,) + PIECE_TILE, piece_map),
                  w_spec, w_spec, w_spec, b_spec, b_spec, b_spec],
        out_specs=pl.BlockSpec((MOE_ROWS, D_MODEL), row_map),
        scratch_shapes=[pltpu.VMEM((2, 3, D_MODEL, D_FF), F32)]
        + [pltpu.VMEM((D_MODEL, D_FF), BF16)] * 3
        + [pltpu.SemaphoreType.DMA((2, 3))],
    )
    shape4 = (DEPTH, N_EXPERTS, 1, D_FF)
    return pl.pallas_call(
        functools.partial(_moe_kernel, layer=layer),
        grid_spec=grid_spec,
        out_shape=jax.ShapeDtypeStruct((MOE_BLOCKS * MOE_ROWS, D_MODEL), F32),
        compiler_params=pltpu.CompilerParams(dimension_semantics=("arbitrary",),
                                             vmem_limit_bytes=56 * 1024 * 1024),
        name="moe_experts",
    )(*sched, xs, w_gate, w_up, w_down, b_gate.reshape(shape4), b_up.reshape(shape4),
      b_down.reshape(shape4))


DISPATCH_TOKENS = 48
DMA_PIECES = 128
DISPATCH_DMAS = DISPATCH_TOKENS * PIECES_PER_ROW // DMA_PIECES
DISPATCH_CHUNKS = N_TOK // SC_WORKERS // DISPATCH_TOKENS
IDX_ROWS = 16
SC_LANES = 16
MOE_ROWS_TOTAL = MOE_BLOCKS * MOE_ROWS
TILE_PIECES = PIECES_PER_ROW * SUBLANES


def _piece_index_rows(p_v, idx_v, row0, tok0, n_dma):
    lane = lax.iota(jnp.int32, SC_LANES)
    in_tile = lane & (SUBLANES - 1)
    piece = (lane >> 3) * SUBLANES
    per_vec = SC_LANES // SUBLANES
    tiles_per_dma = DMA_PIECES // TILE_PIECES
    for m in range(n_dma):
        for tr in range(tiles_per_dma):
            first = plsc.load_gather(p_v, [in_tile + (tok0 + (m * tiles_per_dma + tr) * SUBLANES)])
            for v in range(PIECES_PER_ROW // per_vec):
                idx_v[row0 + m, pl.ds(tr * TILE_PIECES + v * SC_LANES, SC_LANES)] = (
                    first + (piece + v * per_vec * SUBLANES))


def _dispatch(h4, dest_t):
    first_piece = ((dest_t // SUBLANES) * TILE_PIECES + dest_t % SUBLANES).reshape(-1)
    chunk_pieces = DISPATCH_TOKENS * PIECES_PER_ROW
    mesh = plsc.VectorSubcoreMesh(core_axis_name="c", subcore_axis_name="s")

    @functools.partial(
        pl.kernel, mesh=mesh,
        out_type=jax.ShapeDtypeStruct((MOE_ROWS_TOTAL * PIECES_PER_ROW, LANES), F32),
        scratch_types=[pltpu.VMEM((chunk_pieces, LANES), F32), pltpu.VMEM((IDX_ROWS, DMA_PIECES), jnp.int32),
                       pltpu.VMEM((TOP_K * DISPATCH_TOKENS,), jnp.int32), pltpu.SemaphoreType.DMA],
        compiler_params=pltpu.CompilerParams(needs_layout_passes=False),
    )
    def scatter_rows(h_hbm, p_hbm, o_hbm, rows_v, idx_v, p_v, sem):
        wid = lax.axis_index("s") * SC_CORES + lax.axis_index("c")

        @pl.loop(0, DISPATCH_CHUNKS)
        def _(j):
            q = wid * DISPATCH_CHUNKS + j
            pltpu.sync_copy(h_hbm.at[pl.ds(pl.multiple_of(q * chunk_pieces, SUBLANES), chunk_pieces)], rows_v)
            for k in range(TOP_K):
                pltpu.sync_copy(
                    p_hbm.at[pl.ds(pl.multiple_of(k * N_TOK + q * DISPATCH_TOKENS, SUBLANES), DISPATCH_TOKENS)],
                    p_v.at[pl.ds(k * DISPATCH_TOKENS, DISPATCH_TOKENS)])
            for k in range(TOP_K):
                _piece_index_rows(p_v, idx_v, k * DISPATCH_DMAS, k * DISPATCH_TOKENS, DISPATCH_DMAS)
            copies = [pltpu.async_copy(rows_v.at[pl.ds(m * DMA_PIECES, DMA_PIECES)],
                                       o_hbm.at[idx_v.at[k * DISPATCH_DMAS + m]], sem)
                      for k in range(TOP_K) for m in range(DISPATCH_DMAS)]
            for cp in copies:
                cp.wait()

    xs = scatter_rows(h4.reshape(N_TOK * PIECES_PER_ROW, LANES), first_piece)
    return xs.reshape((MOE_ROWS_TOTAL // SUBLANES,) + PIECE_TILE)


def _route(e_t, cnt_prompt, cnt_sample):
    cnt_sample = jnp.sum(cnt_sample[:, :, 0].reshape(SAMPLE_BLOCKS, -1, N_EXPERTS), axis=1)
    blk_cnt = jnp.concatenate([cnt_prompt[:, :, 0], cnt_sample]).astype(jnp.int32)
    counts = jnp.sum(blk_cnt, axis=0)
    padded = ((counts + MOE_ROWS - 1) // MOE_ROWS) * MOE_ROWS
    pend = jnp.cumsum(padded)
    pstart = pend - padded
    base = pstart[None, :] + jnp.cumsum(blk_cnt, axis=0) - blk_cnt
    dest_t = _dest(e_t, base.astype(F32)[:, :, None])
    n_used = (pend[-1] // MOE_ROWS).astype(jnp.int32)
    blk = jnp.minimum(jnp.arange(MOE_BLOCKS, dtype=jnp.int32), n_used - 1) * MOE_ROWS
    blk_exp = jnp.minimum(jnp.sum((pend[None, :] <= blk[:, None]).astype(jnp.int32), axis=1), N_EXPERTS - 1)
    ids = jnp.arange(MOE_BLOCKS, dtype=jnp.int32)
    first = jnp.logical_and(ids < n_used, jnp.logical_or(ids == 0, blk_exp != jnp.roll(blk_exp, 1)))
    first = first.astype(jnp.int32)
    slot = (jnp.cumsum(first) - 1) % 2
    eid = jnp.arange(N_EXPERTS, dtype=jnp.int32)
    later = jnp.logical_and(eid[None, :] > eid[:, None], (counts > 0)[None, :])
    nxt_e = jnp.min(jnp.where(later, eid[None, :], N_EXPERTS), axis=1)
    nxt = jnp.where(nxt_e < N_EXPERTS, nxt_e, -1)[blk_exp].astype(jnp.int32)
    real = jnp.clip((pstart + counts)[blk_exp] - blk, 0, MOE_ROWS)
    half = (real <= MOE_ROWS // 2).astype(jnp.int32)
    return dest_t, (blk_exp, n_used.reshape(1), first, slot.astype(jnp.int32), nxt, half)


def kernel(x_prompt, x_sample, mem_prompt, state_s5_re, state_s5_im, cache_mem_k, cache_mem_v, norm_mix_g, w_in, s5_lambda_re, s5_lambda_im, s5_b_re, s5_b_im, s5_c_re, s5_c_im, s5_d, s5_log_step, s5_w_glu, gm_ln_g, gm_ln_b, gm_w_s, gm_b_s, out_norm_g, w_out, xa_norm_g, mem_norm_g, w_xq, w_xk, w_xv, w_xo, ffn_norm_g, router_w, router_b, e_w_gate, e_b_gate, e_w_up, e_b_up, e_w_down, e_b_down, final_norm_g):
    xs = (x_prompt.reshape(N_PROMPT, D_MODEL), x_sample.reshape(N_SAMPLE, D_MODEL))
    mem = mem_prompt.reshape(BATCH * N_MEM, D_MODEL)
    outs = {k: [] for k in ('s5r_p', 's5i_p', 'mk_p', 'mv_p', 's5r_s', 's5i_s', 'gmv_s')}
    for l in range(DEPTH):
        proj = _norm_matmul(xs, norm_mix_g[l], w_in[l].astype(BF16))
        sw = _s5_weights(s5_lambda_re[l], s5_lambda_im[l], s5_b_re[l], s5_b_im[l], s5_c_re[l], s5_c_im[l],
                         s5_d[l], s5_log_step[l], S5_SEG_PROMPT)
        w_glu = s5_w_glu[l].astype(BF16)
        og = out_norm_g[l]
        ms5, er_p, ei_p = _s5_call(proj, sw, w_glu, og[:S5_WIDTH], S5_SEG_PROMPT, True)
        h0 = (state_s5_re[l].reshape(DEC_BATCH // SUBLANES, SUBLANES, S5_COLS),
              state_s5_im[l].reshape(DEC_BATCH // SUBLANES, SUBLANES, S5_COLS))
        ms5, er_s, ei_s = _s5_call(proj, sw, w_glu, og[:S5_WIDTH], DEC_SEQ, False, h0=h0, prev=ms5)
        outs['s5r_p'].append(er_p[:, SUBLANES - 1].reshape(BATCH, S5_GROUPS, S5_STATE))
        outs['s5i_p'].append(ei_p[:, SUBLANES - 1].reshape(BATCH, S5_GROUPS, S5_STATE))
        outs['s5r_s'].append(er_s.reshape(DEC_BATCH, S5_GROUPS, S5_STATE))
        outs['s5i_s'].append(ei_s.reshape(DEC_BATCH, S5_GROUPS, S5_STATE))

        (mgm,) = _gmlp_call(proj, gm_ln_g[l], gm_ln_b[l], gm_w_s[l], gm_b_s[l], og[S5_WIDTH:],
                            GM_ROWS_PROMPT, GM_CHUNK, 0, N_PROMPT // GM_ROWS_PROMPT, False)
        mgm, vn = _gmlp_call(proj, gm_ln_g[l], gm_ln_b[l], gm_w_s[l], gm_b_s[l], og[S5_WIDTH:],
                             N_SAMPLE, DEC_SEQ, N_PROMPT // N_SAMPLE, 1, True, prev=mgm)
        outs['gmv_s'].append(vn.reshape(DEC_BATCH, DEC_SEQ, GM_HEADS, GM_HEAD_DIM))
        mk = _norm_matmul((mem,), mem_norm_g[l], w_xk[l].astype(BF16))
        mv = _norm_matmul((mem,), mem_norm_g[l], w_xv[l].astype(BF16))
        outs['mk_p'].append(mk.reshape(BATCH, N_MEM, XA_HEADS, XA_HEAD_DIM))
        outs['mv_p'].append(mv.reshape(BATCH, N_MEM, XA_HEADS, XA_HEAD_DIM))
        shared = (w_out[l].astype(BF16), xa_norm_g[l], w_xq[l].astype(BF16))
        tail = (w_xo[l].astype(BF16), ffn_norm_g[l], router_w[l], router_b[l])
        rows_s = XA_SAMPLE_BATCHES * DEC_SEQ
        unified = len(xs) == 1
        post_p = _post_mix(xs[0], 0, ms5, mgm, *shared, mk.reshape(BATCH, N_MEM, D_MODEL).astype(BF16),
                           mv.reshape(BATCH, N_MEM, D_MODEL).astype(BF16), *tail,
                           XA_ROWS_PROMPT, 1, 0, N_PROMPT // XA_ROWS_PROMPT, SEQ // XA_ROWS_PROMPT)
        post_s = _post_mix(xs[-1], N_PROMPT // rows_s if unified else 0, ms5, mgm, *shared,
                           cache_mem_k[l].reshape(DEC_BATCH, N_MEM, D_MODEL).astype(BF16),
                           cache_mem_v[l].reshape(DEC_BATCH, N_MEM, D_MODEL).astype(BF16), *tail,
                           rows_s, XA_SAMPLE_BATCHES, N_PROMPT // rows_s, N_SAMPLE // rows_s, 1,
                           prev=post_p[:4])
        x, h, e_t, gates_t = post_s[:4]

        dest_t, sched = _route(e_t, post_p[4], post_s[4])
        ys = _moe_call(l, _dispatch(h, dest_t), sched, e_w_gate, e_b_gate, e_w_up, e_b_up, e_w_down, e_b_down)
        x = _combine(x, ys[dest_t.reshape(-1)], gates_t.T, final_norm_g if l == DEPTH - 1 else None)
        xs = (x,)

    y_prompt = x[0].reshape(BATCH, SEQ, D_MODEL)
    y_sample = x[1].reshape(DEC_BATCH, DEC_SEQ, D_MODEL)
    st = jnp.stack
    return (y_prompt, y_sample, st(outs['s5r_p']), st(outs['s5i_p']), st(outs['mk_p']), st(outs['mv_p']),
            st(outs['s5r_s']), st(outs['s5i_s']), st(outs['gmv_s']))
```

```python
import functools

import jax
import jax.numpy as jnp
from jax import lax
from jax.experimental import pallas as pl
from jax.experimental.pallas import tpu as pltpu
from jax.experimental.pallas import tpu_sc as plsc

F32 = jnp.float32
BF16 = jnp.bfloat16

D_MODEL = 1024
BATCH = 4
SEQ = 4096
DEPTH = 2
DEC_BATCH = 16
DEC_SEQ = 32
S5_GROUPS = 32
S5_GROUP_CH = 16
S5_STATE = 64
S5_WIDTH = S5_GROUPS * S5_GROUP_CH
S5_COLS = S5_GROUPS * S5_STATE
GM_HEADS = 4
GM_HEAD_DIM = 128
GM_WIDTH = GM_HEADS * GM_HEAD_DIM
GM_CHUNK = 128
IN_PROJ = S5_WIDTH + 2 * GM_WIDTH
N_MEM = 256
XA_HEADS = 4
XA_HEAD_DIM = D_MODEL // XA_HEADS
N_EXPERTS = 32
TOP_K = 4
D_FF = D_MODEL
SWIGLU_LIMIT = 7.0
SWIGLU_ALPHA = 1.702
EPS = 1e-6
EIG_CLIP = -1e-4

N_PROMPT = BATCH * SEQ
N_SAMPLE = DEC_BATCH * DEC_SEQ
N_TOK = N_PROMPT + N_SAMPLE

SUBLANES = 8
LANES = 128
MXU_DIM = 256
VMEM_LIMIT = 48 * 1024 * 1024
SC_CORES = 2
SC_SUBCORES = 16
SC_WORKERS = SC_CORES * SC_SUBCORES

PIECE_TILE = (D_MODEL // LANES, SUBLANES, LANES)
PIECES_PER_ROW = D_MODEL // LANES

ROW_BLOCK = 512
S5_SEG_PROMPT = 64
S5_COL_BLOCK = 512
GM_ROWS_PROMPT = 256
XA_ROWS_PROMPT = 512
XA_SAMPLE_BATCHES = 4
MOE_ROWS = 512
MOE_BLOCKS = -(-N_TOK * TOP_K // MOE_ROWS) + N_EXPERTS
MOE_CAST_ROWS = 128


def _params(*sem):
    return pltpu.CompilerParams(dimension_semantics=sem, vmem_limit_bytes=VMEM_LIMIT)


def _rms(x, g):
    return x * lax.rsqrt(jnp.mean(x * x, axis=-1, keepdims=True) + EPS) * g


def _dot(a, b):
    return jnp.dot(a, b, preferred_element_type=F32)


PROMPT_BLOCKS = N_PROMPT // ROW_BLOCK
SAMPLE_BLOCKS = N_SAMPLE // ROW_BLOCK


def _token_sources(xs):
    if len(xs) == 1:
        return [pl.BlockSpec((ROW_BLOCK, D_MODEL), lambda i: (i, 0))]
    return [pl.BlockSpec((ROW_BLOCK, D_MODEL), lambda i: (jnp.minimum(i, PROMPT_BLOCKS - 1), 0)),
            pl.BlockSpec((ROW_BLOCK, D_MODEL), lambda i: (jnp.maximum(i - PROMPT_BLOCKS, 0), 0))]


def _for_group(refs, body):
    if len(refs) == 1:
        body(refs[0])
        return
    i = pl.program_id(0)
    pl.when(i < PROMPT_BLOCKS)(lambda: body(refs[0]))
    pl.when(i >= PROMPT_BLOCKS)(lambda: body(refs[1]))


def _norm_matmul_kernel(*refs, n_src):
    g_ref, w_ref, o_ref = refs[n_src:]

    def body(x_ref):
        o_ref[...] = _dot(_rms(x_ref[...], g_ref[...]).astype(BF16), w_ref[...])
    _for_group(refs[:n_src], body)


def _norm_matmul(xs, g, w):
    d = D_MODEL
    n = sum(x.shape[0] for x in xs)
    dout = w.shape[1]
    return pl.pallas_call(
        functools.partial(_norm_matmul_kernel, n_src=len(xs)),
        grid=(n // ROW_BLOCK,),
        in_specs=_token_sources(xs) + [pl.BlockSpec((1, d), lambda i: (0, 0)),
                                       pl.BlockSpec((d, dout), lambda i: (0, 0))],
        out_specs=pl.BlockSpec((ROW_BLOCK, dout), lambda i: (i, 0)),
        out_shape=jax.ShapeDtypeStruct((n, dout), F32),
        compiler_params=_params("parallel"),
        name="norm_matmul",
    )(*xs, g.reshape(1, d), w)


def _route_rows(x, g_ref, wt_ref, b_ref, h_ref, e_ref, gate_ref, cnt_ref):
    h = _rms(x, g_ref[...])
    for c in range(PIECES_PER_ROW):
        h_ref[:, c] = h[:, c * LANES:(c + 1) * LANES].reshape(h.shape[0] // SUBLANES, SUBLANES, LANES)
    logits = lax.dot_general(wt_ref[...], h.astype(BF16), (((1,), (1,)), ((), ())),
                             preferred_element_type=F32) + b_ref[...]
    sub = lax.broadcasted_iota(jnp.int32, logits.shape, 0)
    work = logits
    chosen = jnp.zeros(logits.shape, F32)
    vals, idxs = [], []
    for _ in range(TOP_K):
        m = jnp.max(work, axis=0, keepdims=True)
        idx = jnp.min(jnp.where(work == m, sub, N_EXPERTS), axis=0, keepdims=True)
        sel = sub == idx
        vals.append(m)
        idxs.append(idx)
        work = jnp.where(sel, -jnp.inf, work)
        chosen = jnp.where(sel, 1.0, chosen)
    ex = [jnp.exp(v - vals[0]) for v in vals]
    den = ex[0] + ex[1] + ex[2] + ex[3]
    gate_ref[...] = jnp.concatenate([e / den for e in ex], axis=0)
    e_ref[...] = jnp.concatenate(idxs, axis=0)
    cnt_ref[...] = jnp.broadcast_to(jnp.sum(chosen, axis=1, keepdims=True), (N_EXPERTS, LANES))


def _dest_kernel(e_ref, base_ref, tri_ref, d_ref):
    e = e_ref[...]
    sub = lax.broadcasted_iota(jnp.int32, (N_EXPERTS, e.shape[1]), 0)
    hot = [sub == e[k:k + 1, :] for k in range(TOP_K)]
    chosen = sum(jnp.where(hk, 1.0, 0.0) for hk in hot)
    pos = base_ref[...] + _dot(chosen.astype(BF16), tri_ref[...])
    d_ref[...] = jnp.concatenate(
        [jnp.sum(jnp.where(hk, pos, 0.0), axis=0, keepdims=True) for hk in hot], axis=0).astype(jnp.int32)


def _dest(e_t, base):
    n = e_t.shape[1]
    rows = ROW_BLOCK
    tri = jnp.triu(jnp.ones((rows, rows), F32), 1).astype(BF16)
    return pl.pallas_call(
        _dest_kernel,
        grid=(n // rows,),
        in_specs=[pl.BlockSpec((TOP_K, rows), lambda i: (0, i)),
                  pl.BlockSpec((None, N_EXPERTS, 1), lambda i: (i, 0, 0)),
                  pl.BlockSpec((rows, rows), lambda i: (0, 0))],
        out_specs=pl.BlockSpec((TOP_K, rows), lambda i: (0, i)),
        out_shape=jax.ShapeDtypeStruct((TOP_K, n), jnp.int32),
        compiler_params=_params("parallel"),
        name="moe_dest",
    )(e_t, base, tri)


def _combine_kernel(*refs, final):
    if final:
        x_ref, g0, g1, g2, g3, gate_ref, fg_ref, op_ref, os_ref = refs
    else:
        x_ref, g0, g1, g2, g3, gate_ref, o_ref = refs
    acc = x_ref[...]
    for k, g in enumerate((g0, g1, g2, g3)):
        acc = acc + gate_ref[:, k:k + 1] * g[...]
    if final:
        def store(o_ref):
            o_ref[...] = _rms(acc, fg_ref[...])
        _for_group((op_ref, os_ref), store)
    else:
        o_ref[...] = acc


def _combine(x, picked, gates, final_g=None):
    n, d = x.shape
    rows = ROW_BLOCK
    picked = picked.reshape(TOP_K, n, d)
    final = final_g is not None
    in_specs = [pl.BlockSpec((rows, d), lambda i: (i, 0))]
    in_specs += [pl.BlockSpec((None, rows, d), functools.partial(lambda i, k: (k, i, 0), k=k))
                 for k in range(TOP_K)]
    in_specs.append(pl.BlockSpec((rows, TOP_K), lambda i: (i, 0)))
    args = [x, picked, picked, picked, picked, gates]
    out_specs = pl.BlockSpec((rows, d), lambda i: (i, 0))
    out_shape = jax.ShapeDtypeStruct((n, d), F32)
    if final:
        in_specs.append(pl.BlockSpec((1, d), lambda i: (0, 0)))
        args.append(final_g.reshape(1, d))
        out_specs = _token_sources((None, None))
        out_shape = [jax.ShapeDtypeStruct((N_PROMPT, d), F32), jax.ShapeDtypeStruct((N_SAMPLE, d), F32)]
    return pl.pallas_call(
        functools.partial(_combine_kernel, final=final),
        grid=(n // rows,),
        in_specs=in_specs,
        out_specs=out_specs,
        out_shape=out_shape,
        compiler_params=_params("arbitrary" if final else "parallel"),
        name="moe_combine",
    )(*args)


def _cmul_add(ar, ai, hr, hi, br, bi):
    return ar * hr - ai * hi + br, ar * hi + ai * hr + bi


def _s5_kernel(*refs, seg, chained):
    if chained:
        (u_ref, wbr_ref, wbi_ref, wcr_ref, wci_ref, ar_ref, ai_ref, pr_ref, pi_ref, d_ref,
         wglu_ref, og_ref, o_ref, er_ref, ei_ref,
         up_ref, hr_ref, hi_ref, sr_ref, si_ref, op_ref, cr_ref, ci_ref) = refs
    else:
        (u_ref, wbr_ref, wbi_ref, wcr_ref, wci_ref, ar_ref, ai_ref, pr_ref, pi_ref, d_ref,
         wglu_ref, og_ref, h0r_ref, h0i_ref, o_ref, er_ref, ei_ref,
         up_ref, hr_ref, hi_ref, sr_ref, si_ref, op_ref) = refs
    rows = seg * SUBLANES
    half = S5_COLS // 2

    n_lane_blk = S5_WIDTH // LANES
    for j in range(SUBLANES):
        for i0 in range(0, seg, SUBLANES):
            for c in range(n_lane_blk):
                up_ref[c, pl.ds(i0 * SUBLANES + j, SUBLANES, stride=SUBLANES), :] = (
                    u_ref[j * seg + i0:j * seg + i0 + SUBLANES, c * LANES:(c + 1) * LANES])

    up = jnp.concatenate([up_ref[c] for c in range(n_lane_blk)], axis=1)
    ub = up.astype(BF16)
    for k in range(2):
        uk = ub[:, k * MXU_DIM:(k + 1) * MXU_DIM]
        hr_ref[:, k * half:(k + 1) * half] = _dot(uk, wbr_ref[k])
        hi_ref[:, k * half:(k + 1) * half] = _dot(uk, wbi_ref[k])

    for cb in range(S5_COLS // S5_COL_BLOCK):
        cols = slice(cb * S5_COL_BLOCK, (cb + 1) * S5_COL_BLOCK)
        ar = jnp.broadcast_to(ar_ref[:, cols], (SUBLANES, S5_COL_BLOCK))
        ai = jnp.broadcast_to(ai_ref[:, cols], (SUBLANES, S5_COL_BLOCK))

        def step(i, carry, cols=cols, ar=ar, ai=ai):
            r = pl.multiple_of(i * SUBLANES, SUBLANES)
            h_r, h_i = _cmul_add(ar, ai, carry[0], carry[1],
                                 hr_ref[pl.ds(r, SUBLANES), cols], hi_ref[pl.ds(r, SUBLANES), cols])
            hr_ref[pl.ds(r, SUBLANES), cols] = h_r
            hi_ref[pl.ds(r, SUBLANES), cols] = h_i
            return h_r, h_i

        zero = jnp.zeros((SUBLANES, S5_COL_BLOCK), F32)
        lax.fori_loop(0, seg, step, (zero, zero), unroll=2)

    end_r = hr_ref[rows - SUBLANES:rows, :]
    end_i = hi_ref[rows - SUBLANES:rows, :]
    as_r = pr_ref[rows - 1:rows, :]
    as_i = pi_ref[rows - 1:rows, :]
    if chained:
        @pl.when(pl.program_id(1) == 0)
        def _():
            cr_ref[...] = jnp.zeros_like(cr_ref)
            ci_ref[...] = jnp.zeros_like(ci_ref)

        cur_r, cur_i = cr_ref[...], ci_ref[...]
        for j in range(SUBLANES):
            sr_ref[j:j + 1, :] = cur_r
            si_ref[j:j + 1, :] = cur_i
            cur_r, cur_i = _cmul_add(as_r, as_i, cur_r, cur_i, end_r[j:j + 1], end_i[j:j + 1])
        cr_ref[...] = cur_r
        ci_ref[...] = cur_i
    else:
        sr_ref[...] = h0r_ref[...]
        si_ref[...] = h0i_ref[...]
    t_r, t_i = _cmul_add(as_r, as_i, sr_ref[...], si_ref[...], end_r, end_i)
    er_ref[...] = t_r
    ei_ref[...] = t_i

    slabs_per_window = MXU_DIM // (MXU_DIM // S5_STATE * S5_GROUP_CH)
    ys = []
    for w in range(S5_WIDTH // MXU_DIM):
        acc = d_ref[:, w * MXU_DIM:(w + 1) * MXU_DIM] * jnp.concatenate(
            [up_ref[c] for c in range(w * MXU_DIM // LANES, (w + 1) * MXU_DIM // LANES)], axis=1)
        for s in range(slabs_per_window):
            cols = slice((w * slabs_per_window + s) * MXU_DIM, (w * slabs_per_window + s + 1) * MXU_DIM)
            shape3 = (seg, SUBLANES, MXU_DIM)
            h_r, h_i = _cmul_add(pr_ref[:, cols].reshape(shape3), pi_ref[:, cols].reshape(shape3),
                                 sr_ref[:, cols][None], si_ref[:, cols][None],
                                 hr_ref[:, cols].reshape(shape3), hi_ref[:, cols].reshape(shape3))
            acc += _dot(h_r.reshape(rows, MXU_DIM).astype(BF16), wcr_ref[w * slabs_per_window + s])
            acc += _dot(h_i.reshape(rows, MXU_DIM).astype(BF16), wci_ref[w * slabs_per_window + s])
        ys.append(jax.nn.gelu(acc).astype(BF16))

    z = _dot(jnp.concatenate(ys, axis=1), wglu_ref[...])
    o = _rms(z[:, :S5_WIDTH] * jax.nn.sigmoid(z[:, S5_WIDTH:]), og_ref[...])
    for c in range(n_lane_blk):
        op_ref[c] = o[:, c * LANES:(c + 1) * LANES]

    for j in range(SUBLANES):
        for i0 in range(0, seg, 2 * SUBLANES):
            for c in range(n_lane_blk):
                o_ref[j * seg + i0:j * seg + i0 + 2 * SUBLANES, c * LANES:(c + 1) * LANES] = (
                    op_ref[c, pl.ds(i0 * SUBLANES + j, 2 * SUBLANES, stride=SUBLANES), :].astype(BF16))


def _s5_weights(lam_re, lam_im, b_re, b_im, c_re, c_im, d, log_step, seg):
    step = jnp.exp(log_step)[:, None]
    lr = jnp.minimum(lam_re, EIG_CLIP)
    li = lam_im
    mag = jnp.exp(lr * step)
    ar = mag * jnp.cos(li * step)
    ai = mag * jnp.sin(li * step)
    den = lr * lr + li * li
    fr = ((ar - 1.0) * lr + ai * li) / den
    fi = (ai * lr - (ar - 1.0) * li) / den
    bbar_r = fr[..., None] * b_re - fi[..., None] * b_im
    bbar_i = fr[..., None] * b_im + fi[..., None] * b_re
    t = jnp.arange(1, seg + 1, dtype=F32)[:, None, None]
    pm = jnp.exp(lr * step * t)
    p_r = jnp.repeat((pm * jnp.cos(li * step * t)).reshape(seg, S5_COLS), SUBLANES, axis=0)
    p_i = jnp.repeat((pm * jnp.sin(li * step * t)).reshape(seg, S5_COLS), SUBLANES, axis=0)

    gk = MXU_DIM // S5_GROUP_CH

    def in_map(bbar):
        bb = bbar.reshape(S5_GROUPS // gk, gk, S5_STATE, S5_GROUP_CH)
        w = jnp.einsum('kgnh,gG->kghGn', bb, jnp.eye(gk, dtype=F32))
        return w.reshape(S5_GROUPS // gk, MXU_DIM, gk * S5_STATE).astype(BF16)

    gs = MXU_DIM // S5_STATE
    n_slab = S5_COLS // MXU_DIM
    per_win = MXU_DIM // (gs * S5_GROUP_CH)

    def out_map(c):
        cc = c.reshape(n_slab // per_win, per_win, gs, S5_GROUP_CH, S5_STATE)
        w = jnp.einsum('wsghn,sS,gG->wsgnSGh', cc, jnp.eye(per_win, dtype=F32), jnp.eye(gs, dtype=F32))
        return w.reshape(n_slab, MXU_DIM, MXU_DIM).astype(BF16)

    return dict(wbr=in_map(bbar_r), wbi=in_map(bbar_i), wcr=out_map(c_re), wci=out_map(-c_im),
                ar=ar.reshape(1, S5_COLS), ai=ai.reshape(1, S5_COLS), pr=p_r, pi=p_i,
                d=d.reshape(1, S5_WIDTH))


def _s5_call(proj, sw, w_glu, og, seg, chained, h0=None, prev=None):
    rows = seg * SUBLANES
    const2 = lambda *_: (0, 0)
    const3 = lambda *_: (0, 0, 0)
    if chained:
        n_chunk = SEQ // rows
        grid = (BATCH, n_chunk)
        row_map = lambda b, c: (b * n_chunk + c, 0)
        end_map = lambda b, c: (b, 0, 0)
        n_end = BATCH
        sem = ("parallel", "arbitrary")
    else:
        grid = (DEC_BATCH // SUBLANES,)
        row_map = lambda k: (N_PROMPT // rows + k, 0)
        end_map = lambda k: (k, 0, 0)
        n_end = DEC_BATCH // SUBLANES
        sem = ("parallel",)
    half = S5_COLS // 2
    in_specs = [pl.BlockSpec((rows, S5_WIDTH), row_map),
                pl.BlockSpec((2, MXU_DIM, half), const3),
                pl.BlockSpec((2, MXU_DIM, half), const3),
                pl.BlockSpec((S5_COLS // MXU_DIM, MXU_DIM, MXU_DIM), const3),
                pl.BlockSpec((S5_COLS // MXU_DIM, MXU_DIM, MXU_DIM), const3),
                pl.BlockSpec((1, S5_COLS), const2),
                pl.BlockSpec((1, S5_COLS), const2),
                pl.BlockSpec((rows, S5_COLS), const2),
                pl.BlockSpec((rows, S5_COLS), const2),
                pl.BlockSpec((1, S5_WIDTH), const2),
                pl.BlockSpec((S5_WIDTH, 2 * S5_WIDTH), const2),
                pl.BlockSpec((1, S5_WIDTH), const2)]
    args = [proj, sw['wbr'], sw['wbi'], sw['wcr'], sw['wci'], sw['ar'], sw['ai'],
            sw['pr'][:rows], sw['pi'][:rows], sw['d'], w_glu, og.reshape(1, S5_WIDTH)]
    scratch = [pltpu.VMEM((S5_WIDTH // LANES, rows, LANES), F32),
               pltpu.VMEM((rows, S5_COLS), F32), pltpu.VMEM((rows, S5_COLS), F32),
               pltpu.VMEM((SUBLANES, S5_COLS), F32), pltpu.VMEM((SUBLANES, S5_COLS), F32),
               pltpu.VMEM((S5_WIDTH // LANES, rows, LANES), F32)]
    aliases = {}
    if chained:
        scratch += [pltpu.VMEM((1, S5_COLS), F32), pltpu.VMEM((1, S5_COLS), F32)]
    else:
        in_specs += [pl.BlockSpec((None, SUBLANES, S5_COLS), end_map),
                     pl.BlockSpec((None, SUBLANES, S5_COLS), end_map)]
        args += [h0[0], h0[1]]
    if prev is not None:
        in_specs.append(pl.BlockSpec(memory_space=pl.ANY))
        args.append(prev)
        aliases = {len(args) - 1: 0}
    kern = functools.partial(_s5_kernel, seg=seg, chained=chained)
    if prev is not None:
        kern = _drop_inputs(kern, len(args) - 1, 1)
    return pl.pallas_call(
        kern,
        grid=grid,
        in_specs=in_specs,
        out_specs=[pl.BlockSpec((rows, S5_WIDTH), row_map),
                   pl.BlockSpec((None, SUBLANES, S5_COLS), end_map),
                   pl.BlockSpec((None, SUBLANES, S5_COLS), end_map)],
        out_shape=[jax.ShapeDtypeStruct((N_TOK, S5_WIDTH), BF16),
                   jax.ShapeDtypeStruct((n_end, SUBLANES, S5_COLS), F32),
                   jax.ShapeDtypeStruct((n_end, SUBLANES, S5_COLS), F32)],
        scratch_shapes=scratch,
        input_output_aliases=aliases,
        compiler_params=_params(*sem),
        name="s5_chained" if chained else "s5_streams",
    )(*args)


def _drop_inputs(kern, start, n):
    def wrapped(*refs):
        return kern(*refs[:start], *refs[start + n:])
    return wrapped


def _gmlp_kernel(*refs, chunk, emit_vn):
    if emit_vn:
        u_ref, v_ref, lng_ref, lnb_ref, ws_ref, bst_ref, og_ref, o_ref, vn_ref, acc_ref = refs
    else:
        u_ref, v_ref, lng_ref, lnb_ref, ws_ref, bst_ref, og_ref, o_ref, acc_ref = refs
    rows = u_ref.shape[0]
    u = jax.nn.gelu(u_ref[...])
    v = jax.nn.gelu(v_ref[...])
    tri = (lax.broadcasted_iota(jnp.int32, (chunk, chunk), 0)
           >= lax.broadcasted_iota(jnp.int32, (chunk, chunk), 1))
    for h in range(GM_HEADS):
        cols = slice(h * GM_HEAD_DIM, (h + 1) * GM_HEAD_DIM)
        vh = v[:, cols]
        mu = jnp.mean(vh, axis=-1, keepdims=True)
        cen = vh - mu
        var = jnp.mean(cen * cen, axis=-1, keepdims=True)
        vn = cen * lax.rsqrt(var + EPS) * lng_ref[:, cols] + lnb_ref[:, cols]
        if emit_vn:
            vn_ref[:, cols] = vn
        w = jnp.where(tri, ws_ref[h], 0.0).astype(BF16)
        vb = vn.astype(BF16)
        bias = bst_ref[:, h:h + 1]
        for c in range(rows // chunk):
            rs = slice(c * chunk, (c + 1) * chunk)
            acc_ref[rs, cols] = u[rs, cols] * (_dot(w, vb[rs]) + bias)
    o_ref[...] = _rms(acc_ref[...], og_ref[...]).astype(BF16)


def _gmlp_call(proj, ln_g, ln_b, w_s, b_s, og, rows, chunk, first_block, n_blocks, emit_vn, prev=None):
    const2 = lambda i: (0, 0)
    in_specs = [pl.BlockSpec((rows, GM_WIDTH), lambda i: (first_block + i, 1)),
                pl.BlockSpec((rows, GM_WIDTH), lambda i: (first_block + i, 2)),
                pl.BlockSpec((1, GM_WIDTH), const2),
                pl.BlockSpec((1, GM_WIDTH), const2),
                pl.BlockSpec((GM_HEADS, chunk, chunk), lambda i: (0, 0, 0)),
                pl.BlockSpec((chunk, GM_HEADS), const2),
                pl.BlockSpec((1, GM_WIDTH), const2)]
    args = [proj, proj, ln_g.reshape(1, GM_WIDTH), ln_b.reshape(1, GM_WIDTH),
            w_s[:, :chunk, :chunk], b_s[:, :chunk].T, og.reshape(1, GM_WIDTH)]
    out_specs = [pl.BlockSpec((rows, GM_WIDTH), lambda i: (first_block + i, 0))]
    out_shape = [jax.ShapeDtypeStruct((N_TOK, GM_WIDTH), BF16)]
    if emit_vn:
        out_specs.append(pl.BlockSpec((rows, GM_WIDTH), lambda i: (i, 0)))
        out_shape.append(jax.ShapeDtypeStruct((n_blocks * rows, GM_WIDTH), F32))
    kern = functools.partial(_gmlp_kernel, chunk=chunk, emit_vn=emit_vn)
    aliases = {}
    if prev is not None:
        in_specs.append(pl.BlockSpec(memory_space=pl.ANY))
        args.append(prev)
        aliases = {len(args) - 1: 0}
        kern = _drop_inputs(kern, len(args) - 1, 1)
    return pl.pallas_call(
        kern,
        grid=(n_blocks,),
        in_specs=in_specs,
        out_specs=out_specs,
        out_shape=out_shape,
        scratch_shapes=[pltpu.VMEM((rows, GM_WIDTH), F32)],
        input_output_aliases=aliases,
        compiler_params=_params("parallel"),
        name="gmlp_vn" if emit_vn else "gmlp",
    )(*args)


def _post_mix_kernel(x_ref, a_ref, b_ref, wa_ref, wb_ref, xg_ref, wq_ref, k_ref, v_ref, wo_ref,
                     fg_ref, wt_ref, rb_ref, x2_ref, h_ref, e_ref, gate_ref, cnt_ref, att_ref, *, n_batch, kv_layer):
    rows = x_ref.shape[0]
    tb = rows // n_batch
    x = x_ref[...] + _dot(a_ref[...], wa_ref[...]) + _dot(b_ref[...], wb_ref[...])
    q = _dot(_rms(x, xg_ref[...]).astype(BF16), wq_ref[...]).astype(BF16)
    scale = XA_HEAD_DIM ** -0.5
    for b in range(n_batch):
        rs = slice(b * tb, (b + 1) * tb)
        for h in range(XA_HEADS):
            cols = slice(h * XA_HEAD_DIM, (h + 1) * XA_HEAD_DIM)
            if kv_layer is None:
                k_h, v_h = k_ref[b, :, cols], v_ref[b, :, cols]
            else:
                k_h, v_h = k_ref[b, :, h, :].astype(BF16), v_ref[b, :, h, :].astype(BF16)
            s = lax.dot_general(q[rs, cols], k_h, (((1,), (1,)), ((), ())),
                                preferred_element_type=F32) * scale
            e = jnp.exp(s - jnp.max(s, axis=-1, keepdims=True))
            p = e / jnp.sum(e, axis=-1, keepdims=True)
            att_ref[rs, cols] = _dot(p.astype(BF16), v_h).astype(BF16)
    x = x + _dot(att_ref[...], wo_ref[...])
    x2_ref[...] = x
    _route_rows(x, fg_ref, wt_ref, rb_ref, h_ref, e_ref, gate_ref, cnt_ref)


def _post_mix(x, x_first, a, b, w_out, xg, wq, k, v, wo, fg, rw, rb,
              rows, n_batch, first_block, n_blocks, blocks_per_kv, prev=None, kv_layer=None):
    const2 = lambda i: (0, 0)
    if kv_layer is None:
        kv_spec = pl.BlockSpec((n_batch, N_MEM, D_MODEL), lambda i: (i // blocks_per_kv, 0, 0))
    else:
        kv_spec = pl.BlockSpec((None, n_batch, N_MEM, XA_HEADS, XA_HEAD_DIM),
                               lambda i: (kv_layer, i // blocks_per_kv, 0, 0, 0))
    row_map = lambda i: (first_block + i, 0)
    d = D_MODEL
    in_specs = [pl.BlockSpec((rows, d), lambda i: (x_first + i, 0)),
                pl.BlockSpec((rows, S5_WIDTH), row_map),
                pl.BlockSpec((rows, GM_WIDTH), row_map),
                pl.BlockSpec((S5_WIDTH, d), const2),
                pl.BlockSpec((GM_WIDTH, d), const2),
                pl.BlockSpec((1, d), const2),
                pl.BlockSpec((d, d), const2),
                kv_spec, kv_spec,
                pl.BlockSpec((d, d), const2),
                pl.BlockSpec((1, d), const2),
                pl.BlockSpec((N_EXPERTS, d), const2),
                pl.BlockSpec((N_EXPERTS, 1), const2)]
    args = [x, a, b, w_out[:S5_WIDTH], w_out[S5_WIDTH:], xg.reshape(1, d), wq, k, v, wo,
            fg.reshape(1, d), rw.T.astype(BF16), rb.reshape(N_EXPERTS, 1)]
    kern = functools.partial(_post_mix_kernel, n_batch=n_batch, kv_layer=kv_layer)
    aliases = {}
    if prev is not None:
        n_in = len(args)
        in_specs += [pl.BlockSpec(memory_space=pl.ANY)] * len(prev)
        args += list(prev)
        aliases = {n_in + o: o for o in range(len(prev))}
        kern = _drop_inputs(kern, n_in, len(prev))
    return pl.pallas_call(
        kern,
        grid=(n_blocks,),
        in_specs=in_specs,
        out_specs=[pl.BlockSpec((rows, d), row_map),
                   pl.BlockSpec((rows // SUBLANES,) + PIECE_TILE, lambda i: (first_block + i, 0, 0, 0)),
                   pl.BlockSpec((TOP_K, rows), lambda i: (0, first_block + i)),
                   pl.BlockSpec((TOP_K, rows), lambda i: (0, first_block + i)),
                   pl.BlockSpec((None, N_EXPERTS, LANES), lambda i: (i, 0, 0))],
        out_shape=[jax.ShapeDtypeStruct((N_TOK, d), F32),
                   jax.ShapeDtypeStruct((N_TOK // SUBLANES,) + PIECE_TILE, F32),
                   jax.ShapeDtypeStruct((TOP_K, N_TOK), jnp.int32),
                   jax.ShapeDtypeStruct((TOP_K, N_TOK), F32),
                   jax.ShapeDtypeStruct((n_blocks, N_EXPERTS, LANES), F32)],
        scratch_shapes=[pltpu.VMEM((rows, d), BF16)],
        input_output_aliases=aliases,
        compiler_params=_params("parallel"),
        name="post_mix",
    )(*args)


def _moe_kernel(be_ref, nu_ref, first_ref, slot_ref, nxt_ref, half_ref,
                x_ref, wg_hbm, wu_hbm, wd_hbm, bg_ref, bu_ref, bd_ref, o_ref,
                wbuf_ref, wgb_ref, wub_ref, wdb_ref, sem, *, layer):
    i = pl.program_id(0)
    active = i < nu_ref[0]

    def copies(e, s):
        return [pltpu.make_async_copy(w.at[layer, e], wbuf_ref.at[s, j], sem.at[s, j])
                for j, w in enumerate((wg_hbm, wu_hbm, wd_hbm))]

    @pl.when(jnp.logical_and(active, first_ref[i] == 1))
    def _():
        s = slot_ref[i]

        @pl.when(i == 0)
        def _():
            for c in copies(be_ref[0], s):
                c.start()

        @pl.when(nxt_ref[i] >= 0)
        def _():
            for c in copies(nxt_ref[i], 1 - s):
                c.start()

        for c in copies(be_ref[i], s):
            c.wait()

        def cast(r, _):
            rs = pl.ds(pl.multiple_of(r * MOE_CAST_ROWS, MOE_CAST_ROWS), MOE_CAST_ROWS)
            wgb_ref[rs, :] = wbuf_ref[s, 0, rs, :].astype(BF16)
            wub_ref[rs, :] = wbuf_ref[s, 1, rs, :].astype(BF16)
            wdb_ref[rs, :] = wbuf_ref[s, 2, rs, :].astype(BF16)
            return 0
        lax.fori_loop(0, D_MODEL // MOE_CAST_ROWS, cast, 0)

    def experts(n_rows):
        x = jnp.concatenate([x_ref[:n_rows // SUBLANES, c].reshape(n_rows, LANES) for c in range(PIECES_PER_ROW)],
                            axis=1).astype(BF16)
        g = jnp.minimum(_dot(x, wgb_ref[...]) + bg_ref[...], SWIGLU_LIMIT)
        u = jnp.clip(_dot(x, wub_ref[...]) + bu_ref[...], -SWIGLU_LIMIT, SWIGLU_LIMIT)
        a = g * jax.nn.sigmoid(SWIGLU_ALPHA * g) * (u + 1.0)
        o_ref[:n_rows, :] = _dot(a.astype(BF16), wdb_ref[...]) + bd_ref[...]

    pl.when(jnp.logical_and(active, half_ref[i] == 0))(lambda: experts(MOE_ROWS))
    pl.when(jnp.logical_and(active, half_ref[i] == 1))(lambda: experts(MOE_ROWS // 2))


def _moe_call(layer, xs, sched, w_gate, b_gate, w_up, b_up, w_down, b_down):
    def row_map(i, be, nu, *_):
        return (jnp.minimum(i, nu[0] - 1), 0)

    def piece_map(i, be, nu, *_):
        return (jnp.minimum(i, nu[0] - 1), 0, 0, 0)

    def b_map(i, be, *_):
        return (layer, be[i], 0, 0)

    b_spec = pl.BlockSpec((None, None, 1, D_FF), b_map)
    w_spec = pl.BlockSpec(memory_space=pl.ANY)
    grid_spec = pltpu.PrefetchScalarGridSpec(
        num_scalar_prefetch=6,
        grid=(MOE_BLOCKS,),
        in_specs=[pl.BlockSpec((MOE_ROWS // SUBLANES,) + PIECE_TILE, piece_map),
                  w_spec, w_spec, w_spec, b_spec, b_spec, b_spec],
        out_specs=pl.BlockSpec((MOE_ROWS, D_MODEL), row_map),
        scratch_shapes=[pltpu.VMEM((2, 3, D_MODEL, D_FF), F32)]
        + [pltpu.VMEM((D_MODEL, D_FF), BF16)] * 3
        + [pltpu.SemaphoreType.DMA((2, 3))],
    )
    shape4 = (DEPTH, N_EXPERTS, 1, D_FF)
    return pl.pallas_call(
        functools.partial(_moe_kernel, layer=layer),
        grid_spec=grid_spec,
        out_shape=jax.ShapeDtypeStruct((MOE_BLOCKS * MOE_ROWS, D_MODEL), F32),
        compiler_params=pltpu.CompilerParams(dimension_semantics=("arbitrary",),
                                             vmem_limit_bytes=56 * 1024 * 1024),
        name="moe_experts",
    )(*sched, xs, w_gate, w_up, w_down, b_gate.reshape(shape4), b_up.reshape(shape4),
      b_down.reshape(shape4))


DISPATCH_TOKENS = 48
DMA_PIECES = 128
DISPATCH_DMAS = DISPATCH_TOKENS * PIECES_PER_ROW // DMA_PIECES
DISPATCH_CHUNKS = N_TOK // SC_WORKERS // DISPATCH_TOKENS
IDX_ROWS = 16
SC_LANES = 16
MOE_ROWS_TOTAL = MOE_BLOCKS * MOE_ROWS
TILE_PIECES = PIECES_PER_ROW * SUBLANES


def _piece_index_rows(p_v, idx_v, row0, tok0, n_dma):
    lane = lax.iota(jnp.int32, SC_LANES)
    in_tile = lane & (SUBLANES - 1)
    piece = (lane >> 3) * SUBLANES
    per_vec = SC_LANES // SUBLANES
    tiles_per_dma = DMA_PIECES // TILE_PIECES
    for m in range(n_dma):
        for tr in range(tiles_per_dma):
            first = plsc.load_gather(p_v, [in_tile + (tok0 + (m * tiles_per_dma + tr) * SUBLANES)])
            for v in range(PIECES_PER_ROW // per_vec):
                idx_v[row0 + m, pl.ds(tr * TILE_PIECES + v * SC_LANES, SC_LANES)] = (
                    first + (piece + v * per_vec * SUBLANES))


def _dispatch(h4, dest_t):
    first_piece = ((dest_t // SUBLANES) * TILE_PIECES + dest_t % SUBLANES).reshape(-1)
    chunk_pieces = DISPATCH_TOKENS * PIECES_PER_ROW
    mesh = plsc.VectorSubcoreMesh(core_axis_name="c", subcore_axis_name="s")

    @functools.partial(
        pl.kernel, mesh=mesh,
        out_type=jax.ShapeDtypeStruct((MOE_ROWS_TOTAL * PIECES_PER_ROW, LANES), F32),
        scratch_types=[pltpu.VMEM((chunk_pieces, LANES), F32), pltpu.VMEM((IDX_ROWS, DMA_PIECES), jnp.int32),
                       pltpu.VMEM((TOP_K * DISPATCH_TOKENS,), jnp.int32), pltpu.SemaphoreType.DMA],
        compiler_params=pltpu.CompilerParams(needs_layout_passes=False),
    )
    def scatter_rows(h_hbm, p_hbm, o_hbm, rows_v, idx_v, p_v, sem):
        wid = lax.axis_index("s") * SC_CORES + lax.axis_index("c")

        @pl.loop(0, DISPATCH_CHUNKS)
        def _(j):
            q = wid * DISPATCH_CHUNKS + j
            pltpu.sync_copy(h_hbm.at[pl.ds(pl.multiple_of(q * chunk_pieces, SUBLANES), chunk_pieces)], rows_v)
            for k in range(TOP_K):
                pltpu.sync_copy(
                    p_hbm.at[pl.ds(pl.multiple_of(k * N_TOK + q * DISPATCH_TOKENS, SUBLANES), DISPATCH_TOKENS)],
                    p_v.at[pl.ds(k * DISPATCH_TOKENS, DISPATCH_TOKENS)])
            for k in range(TOP_K):
                _piece_index_rows(p_v, idx_v, k * DISPATCH_DMAS, k * DISPATCH_TOKENS, DISPATCH_DMAS)
            copies = [pltpu.async_copy(rows_v.at[pl.ds(m * DMA_PIECES, DMA_PIECES)],
                                       o_hbm.at[idx_v.at[k * DISPATCH_DMAS + m]], sem)
                      for k in range(TOP_K) for m in range(DISPATCH_DMAS)]
            for cp in copies:
                cp.wait()

    xs = scatter_rows(h4.reshape(N_TOK * PIECES_PER_ROW, LANES), first_piece)
    return xs.reshape((MOE_ROWS_TOTAL // SUBLANES,) + PIECE_TILE)


def _route(e_t, cnt_prompt, cnt_sample):
    cnt_sample = jnp.sum(cnt_sample[:, :, 0].reshape(SAMPLE_BLOCKS, -1, N_EXPERTS), axis=1)
    blk_cnt = jnp.concatenate([cnt_prompt[:, :, 0], cnt_sample]).astype(jnp.int32)
    counts = jnp.sum(blk_cnt, axis=0)
    padded = ((counts + MOE_ROWS - 1) // MOE_ROWS) * MOE_ROWS
    pend = jnp.cumsum(padded)
    pstart = pend - padded
    base = pstart[None, :] + jnp.cumsum(blk_cnt, axis=0) - blk_cnt
    dest_t = _dest(e_t, base.astype(F32)[:, :, None])
    n_used = (pend[-1] // MOE_ROWS).astype(jnp.int32)
    blk = jnp.minimum(jnp.arange(MOE_BLOCKS, dtype=jnp.int32), n_used - 1) * MOE_ROWS
    blk_exp = jnp.minimum(jnp.sum((pend[None, :] <= blk[:, None]).astype(jnp.int32), axis=1), N_EXPERTS - 1)
    ids = jnp.arange(MOE_BLOCKS, dtype=jnp.int32)
    first = jnp.logical_and(ids < n_used, jnp.logical_or(ids == 0, blk_exp != jnp.roll(blk_exp, 1)))
    first = first.astype(jnp.int32)
    slot = (jnp.cumsum(first) - 1) % 2
    eid = jnp.arange(N_EXPERTS, dtype=jnp.int32)
    later = jnp.logical_and(eid[None, :] > eid[:, None], (counts > 0)[None, :])
    nxt_e = jnp.min(jnp.where(later, eid[None, :], N_EXPERTS), axis=1)
    nxt = jnp.where(nxt_e < N_EXPERTS, nxt_e, -1)[blk_exp].astype(jnp.int32)
    real = jnp.clip((pstart + counts)[blk_exp] - blk, 0, MOE_ROWS)
    half = (real <= MOE_ROWS // 2).astype(jnp.int32)
    return dest_t, (blk_exp, n_used.reshape(1), first, slot.astype(jnp.int32), nxt, half)


def kernel(x_prompt, x_sample, mem_prompt, state_s5_re, state_s5_im, cache_mem_k, cache_mem_v, norm_mix_g, w_in, s5_lambda_re, s5_lambda_im, s5_b_re, s5_b_im, s5_c_re, s5_c_im, s5_d, s5_log_step, s5_w_glu, gm_ln_g, gm_ln_b, gm_w_s, gm_b_s, out_norm_g, w_out, xa_norm_g, mem_norm_g, w_xq, w_xk, w_xv, w_xo, ffn_norm_g, router_w, router_b, e_w_gate, e_b_gate, e_w_up, e_b_up, e_w_down, e_b_down, final_norm_g):
    xs = (x_prompt.reshape(N_PROMPT, D_MODEL), x_sample.reshape(N_SAMPLE, D_MODEL))
    mem = mem_prompt.reshape(BATCH * N_MEM, D_MODEL)
    outs = {k: [] for k in ('s5r_p', 's5i_p', 'mk_p', 'mv_p', 's5r_s', 's5i_s', 'gmv_s')}
    for l in range(DEPTH):
        proj = _norm_matmul(xs, norm_mix_g[l], w_in[l].astype(BF16))
        sw = _s5_weights(s5_lambda_re[l], s5_lambda_im[l], s5_b_re[l], s5_b_im[l], s5_c_re[l], s5_c_im[l],
                         s5_d[l], s5_log_step[l], S5_SEG_PROMPT)
        w_glu = s5_w_glu[l].astype(BF16)
        og = out_norm_g[l]
        ms5, er_p, ei_p = _s5_call(proj, sw, w_glu, og[:S5_WIDTH], S5_SEG_PROMPT, True)
        h0 = (state_s5_re[l].reshape(DEC_BATCH // SUBLANES, SUBLANES, S5_COLS),
              state_s5_im[l].reshape(DEC_BATCH // SUBLANES, SUBLANES, S5_COLS))
        ms5, er_s, ei_s = _s5_call(proj, sw, w_glu, og[:S5_WIDTH], DEC_SEQ, False, h0=h0, prev=ms5)
        outs['s5r_p'].append(er_p[:, SUBLANES - 1].reshape(BATCH, S5_GROUPS, S5_STATE))
        outs['s5i_p'].append(ei_p[:, SUBLANES - 1].reshape(BATCH, S5_GROUPS, S5_STATE))
        outs['s5r_s'].append(er_s.reshape(DEC_BATCH, S5_GROUPS, S5_STATE))
        outs['s5i_s'].append(ei_s.reshape(DEC_BATCH, S5_GROUPS, S5_STATE))

        (mgm,) = _gmlp_call(proj, gm_ln_g[l], gm_ln_b[l], gm_w_s[l], gm_b_s[l], og[S5_WIDTH:],
                            GM_ROWS_PROMPT, GM_CHUNK, 0, N_PROMPT // GM_ROWS_PROMPT, False)
        mgm, vn = _gmlp_call(proj, gm_ln_g[l], gm_ln_b[l], gm_w_s[l], gm_b_s[l], og[S5_WIDTH:],
                             N_SAMPLE, DEC_SEQ, N_PROMPT // N_SAMPLE, 1, True, prev=mgm)
        outs['gmv_s'].append(vn.reshape(DEC_BATCH, DEC_SEQ, GM_HEADS, GM_HEAD_DIM))
        mk = _norm_matmul((mem,), mem_norm_g[l], w_xk[l].astype(BF16))
        mv = _norm_matmul((mem,), mem_norm_g[l], w_xv[l].astype(BF16))
        outs['mk_p'].append(mk.reshape(BATCH, N_MEM, XA_HEADS, XA_HEAD_DIM))
        outs['mv_p'].append(mv.reshape(BATCH, N_MEM, XA_HEADS, XA_HEAD_DIM))
        shared = (w_out[l].astype(BF16), xa_norm_g[l], w_xq[l].astype(BF16))
        tail = (w_xo[l].astype(BF16), ffn_norm_g[l], router_w[l], router_b[l])
        rows_s = XA_SAMPLE_BATCHES * DEC_SEQ
        unified = len(xs) == 1
        post_p = _post_mix(xs[0], 0, ms5, mgm, *shared, mk.reshape(BATCH, N_MEM, D_MODEL).astype(BF16),
                           mv.reshape(BATCH, N_MEM, D_MODEL).astype(BF16), *tail,
                           XA_ROWS_PROMPT, 1, 0, N_PROMPT // XA_ROWS_PROMPT, SEQ // XA_ROWS_PROMPT)
        post_s = _post_mix(xs[-1], N_PROMPT // rows_s if unified else 0, ms5, mgm, *shared,
                           cache_mem_k, cache_mem_v, *tail,
                           rows_s, XA_SAMPLE_BATCHES, N_PROMPT // rows_s, N_SAMPLE // rows_s, 1,
                           prev=post_p[:4], kv_layer=l)
        x, h, e_t, gates_t = post_s[:4]

        dest_t, sched = _route(e_t, post_p[4], post_s[4])
        ys = _moe_call(l, _dispatch(h, dest_t), sched, e_w_gate, e_b_gate, e_w_up, e_b_up, e_w_down, e_b_down)
        x = _combine(x, ys[dest_t.reshape(-1)], gates_t.T, final_norm_g if l == DEPTH - 1 else None)
        xs = (x,)

    y_prompt = x[0].reshape(BATCH, SEQ, D_MODEL)
    y_sample = x[1].reshape(DEC_BATCH, DEC_SEQ, D_MODEL)
    st = jnp.stack
    return (y_prompt, y_sample, st(outs['s5r_p']), st(outs['s5i_p']), st(outs['mk_p']), st(outs['mv_p']),
            st(outs['s5r_s']), st(outs['s5i_s']), st(outs['gmv_s']))
```

```python
import functools

import jax
import jax.numpy as jnp
from jax import lax
from jax.experimental import pallas as pl
from jax.experimental.pallas import tpu as pltpu
from jax.experimental.pallas import tpu_sc as plsc

F32 = jnp.float32
BF16 = jnp.bfloat16

D_MODEL = 1024
BATCH = 4
SEQ = 4096
DEPTH = 2
DEC_BATCH = 16
DEC_SEQ = 32
S5_GROUPS = 32
S5_GROUP_CH = 16
S5_STATE = 64
S5_WIDTH = S5_GROUPS * S5_GROUP_CH
S5_COLS = S5_GROUPS * S5_STATE
GM_HEADS = 4
GM_HEAD_DIM = 128
GM_WIDTH = GM_HEADS * GM_HEAD_DIM
GM_CHUNK = 128
IN_PROJ = S5_WIDTH + 2 * GM_WIDTH
N_MEM = 256
XA_HEADS = 4
XA_HEAD_DIM = D_MODEL // XA_HEADS
N_EXPERTS = 32
TOP_K = 4
D_FF = D_MODEL
SWIGLU_LIMIT = 7.0
SWIGLU_ALPHA = 1.702
EPS = 1e-6
EIG_CLIP = -1e-4

N_PROMPT = BATCH * SEQ
N_SAMPLE = DEC_BATCH * DEC_SEQ
N_TOK = N_PROMPT + N_SAMPLE

SUBLANES = 8
LANES = 128
MXU_DIM = 256
VMEM_LIMIT = 48 * 1024 * 1024
SC_CORES = 2
SC_SUBCORES = 16
SC_WORKERS = SC_CORES * SC_SUBCORES

PIECE_TILE = (D_MODEL // LANES, SUBLANES, LANES)
PIECES_PER_ROW = D_MODEL // LANES

ROW_BLOCK = 512
S5_SEG_PROMPT = 64
S5_COL_BLOCK = 512
GM_ROWS_PROMPT = 256
XA_ROWS_PROMPT = 512
XA_SAMPLE_BATCHES = 4
MOE_ROWS = 512
MOE_BLOCKS = -(-N_TOK * TOP_K // MOE_ROWS) + N_EXPERTS
MOE_CAST_ROWS = 128


def _params(*sem):
    return pltpu.CompilerParams(dimension_semantics=sem, vmem_limit_bytes=VMEM_LIMIT)


def _rms(x, g):
    return x * lax.rsqrt(jnp.mean(x * x, axis=-1, keepdims=True) + EPS) * g


def _dot(a, b):
    return jnp.dot(a, b, preferred_element_type=F32)


PROMPT_BLOCKS = N_PROMPT // ROW_BLOCK
SAMPLE_BLOCKS = N_SAMPLE // ROW_BLOCK


def _token_sources(xs):
    if len(xs) == 1:
        return [pl.BlockSpec((ROW_BLOCK, D_MODEL), lambda i: (i, 0))]
    return [pl.BlockSpec((ROW_BLOCK, D_MODEL), lambda i: (jnp.minimum(i, PROMPT_BLOCKS - 1), 0)),
            pl.BlockSpec((ROW_BLOCK, D_MODEL), lambda i: (jnp.maximum(i - PROMPT_BLOCKS, 0), 0))]


def _for_group(refs, body):
    if len(refs) == 1:
        body(refs[0])
        return
    i = pl.program_id(0)
    pl.when(i < PROMPT_BLOCKS)(lambda: body(refs[0]))
    pl.when(i >= PROMPT_BLOCKS)(lambda: body(refs[1]))


def _norm_matmul_kernel(*refs, n_src):
    g_ref, w_ref, o_ref = refs[n_src:]

    def body(x_ref):
        o_ref[...] = _dot(_rms(x_ref[...], g_ref[...]).astype(BF16), w_ref[...])
    _for_group(refs[:n_src], body)


def _norm_matmul(xs, g, w):
    d = D_MODEL
    n = sum(x.shape[0] for x in xs)
    dout = w.shape[1]
    return pl.pallas_call(
        functools.partial(_norm_matmul_kernel, n_src=len(xs)),
        grid=(n // ROW_BLOCK,),
        in_specs=_token_sources(xs) + [pl.BlockSpec((1, d), lambda i: (0, 0)),
                                       pl.BlockSpec((d, dout), lambda i: (0, 0))],
        out_specs=pl.BlockSpec((ROW_BLOCK, dout), lambda i: (i, 0)),
        out_shape=jax.ShapeDtypeStruct((n, dout), F32),
        compiler_params=_params("parallel"),
        name="norm_matmul",
    )(*xs, g.reshape(1, d), w)


def _route_rows(x, g_ref, wt_ref, b_ref, h_ref, e_ref, gate_ref, cnt_ref):
    h = _rms(x, g_ref[...])
    for c in range(PIECES_PER_ROW):
        h_ref[:, c] = h[:, c * LANES:(c + 1) * LANES].reshape(h.shape[0] // SUBLANES, SUBLANES, LANES)
    logits = lax.dot_general(wt_ref[...], h.astype(BF16), (((1,), (1,)), ((), ())),
                             preferred_element_type=F32) + b_ref[...]
    sub = lax.broadcasted_iota(jnp.int32, logits.shape, 0)
    work = logits
    chosen = jnp.zeros(logits.shape, F32)
    vals, idxs = [], []
    for _ in range(TOP_K):
        m = jnp.max(work, axis=0, keepdims=True)
        idx = jnp.min(jnp.where(work == m, sub, N_EXPERTS), axis=0, keepdims=True)
        sel = sub == idx
        vals.append(m)
        idxs.append(idx)
        work = jnp.where(sel, -jnp.inf, work)
        chosen = jnp.where(sel, 1.0, chosen)
    ex = [jnp.exp(v - vals[0]) for v in vals]
    den = ex[0] + ex[1] + ex[2] + ex[3]
    gate_ref[...] = jnp.concatenate([e / den for e in ex], axis=0)
    e_ref[...] = jnp.concatenate(idxs, axis=0)
    cnt_ref[...] = jnp.broadcast_to(jnp.sum(chosen, axis=1, keepdims=True), (N_EXPERTS, LANES))


def _dest_kernel(e_ref, base_ref, tri_ref, d_ref):
    e = e_ref[...]
    sub = lax.broadcasted_iota(jnp.int32, (N_EXPERTS, e.shape[1]), 0)
    hot = [sub == e[k:k + 1, :] for k in range(TOP_K)]
    chosen = sum(jnp.where(hk, 1.0, 0.0) for hk in hot)
    pos = base_ref[...] + _dot(chosen.astype(BF16), tri_ref[...])
    d_ref[...] = jnp.concatenate(
        [jnp.sum(jnp.where(hk, pos, 0.0), axis=0, keepdims=True) for hk in hot], axis=0).astype(jnp.int32)


def _dest(e_t, base):
    n = e_t.shape[1]
    rows = ROW_BLOCK
    tri = jnp.triu(jnp.ones((rows, rows), F32), 1).astype(BF16)
    return pl.pallas_call(
        _dest_kernel,
        grid=(n // rows,),
        in_specs=[pl.BlockSpec((TOP_K, rows), lambda i: (0, i)),
                  pl.BlockSpec((None, N_EXPERTS, 1), lambda i: (i, 0, 0)),
                  pl.BlockSpec((rows, rows), lambda i: (0, 0))],
        out_specs=pl.BlockSpec((TOP_K, rows), lambda i: (0, i)),
        out_shape=jax.ShapeDtypeStruct((TOP_K, n), jnp.int32),
        compiler_params=_params("parallel"),
        name="moe_dest",
    )(e_t, base, tri)


def _combine_kernel(*refs, final):
    if final:
        x_ref, g0, g1, g2, g3, gate_ref, fg_ref, op_ref, os_ref = refs
    else:
        x_ref, g0, g1, g2, g3, gate_ref, o_ref = refs
    acc = x_ref[...]
    for k, g in enumerate((g0, g1, g2, g3)):
        acc = acc + gate_ref[:, k:k + 1] * g[...]
    if final:
        def store(o_ref):
            o_ref[...] = _rms(acc, fg_ref[...])
        _for_group((op_ref, os_ref), store)
    else:
        o_ref[...] = acc


def _combine(x, picked, gates, final_g=None):
    n, d = x.shape
    rows = ROW_BLOCK
    picked = picked.reshape(TOP_K, n, d)
    final = final_g is not None
    in_specs = [pl.BlockSpec((rows, d), lambda i: (i, 0))]
    in_specs += [pl.BlockSpec((None, rows, d), functools.partial(lambda i, k: (k, i, 0), k=k))
                 for k in range(TOP_K)]
    in_specs.append(pl.BlockSpec((rows, TOP_K), lambda i: (i, 0)))
    args = [x, picked, picked, picked, picked, gates]
    out_specs = pl.BlockSpec((rows, d), lambda i: (i, 0))
    out_shape = jax.ShapeDtypeStruct((n, d), F32)
    if final:
        in_specs.append(pl.BlockSpec((1, d), lambda i: (0, 0)))
        args.append(final_g.reshape(1, d))
        out_specs = _token_sources((None, None))
        out_shape = [jax.ShapeDtypeStruct((N_PROMPT, d), F32), jax.ShapeDtypeStruct((N_SAMPLE, d), F32)]
    return pl.pallas_call(
        functools.partial(_combine_kernel, final=final),
        grid=(n // rows,),
        in_specs=in_specs,
        out_specs=out_specs,
        out_shape=out_shape,
        compiler_params=_params("arbitrary" if final else "parallel"),
        name="moe_combine",
    )(*args)


def _cmul_add(ar, ai, hr, hi, br, bi):
    return ar * hr - ai * hi + br, ar * hi + ai * hr + bi


def _s5_kernel(*refs, seg, chained):
    if chained:
        (x_ref, ng_ref, win_ref, wbr_ref, wbi_ref, wcr_ref, wci_ref, ar_ref, ai_ref, pr_ref, pi_ref, d_ref,
         wglu_ref, og_ref, o_ref, er_ref, ei_ref,
         up_ref, hr_ref, hi_ref, sr_ref, si_ref, op_ref, cr_ref, ci_ref) = refs
    else:
        (x_ref, ng_ref, win_ref, wbr_ref, wbi_ref, wcr_ref, wci_ref, ar_ref, ai_ref, pr_ref, pi_ref, d_ref,
         wglu_ref, og_ref, h0r_ref, h0i_ref, o_ref, er_ref, ei_ref,
         up_ref, hr_ref, hi_ref, sr_ref, si_ref, op_ref) = refs
    rows = seg * SUBLANES
    half = S5_COLS // 2

    u = _dot(_rms(x_ref[...], ng_ref[...]).astype(BF16), win_ref[...])
    n_lane_blk = S5_WIDTH // LANES
    for j in range(SUBLANES):
        for i0 in range(0, seg, SUBLANES):
            for c in range(n_lane_blk):
                up_ref[c, pl.ds(i0 * SUBLANES + j, SUBLANES, stride=SUBLANES), :] = (
                    u[j * seg + i0:j * seg + i0 + SUBLANES, c * LANES:(c + 1) * LANES])

    up = jnp.concatenate([up_ref[c] for c in range(n_lane_blk)], axis=1)
    ub = up.astype(BF16)
    for k in range(2):
        uk = ub[:, k * MXU_DIM:(k + 1) * MXU_DIM]
        hr_ref[:, k * half:(k + 1) * half] = _dot(uk, wbr_ref[k])
        hi_ref[:, k * half:(k + 1) * half] = _dot(uk, wbi_ref[k])

    for cb in range(S5_COLS // S5_COL_BLOCK):
        cols = slice(cb * S5_COL_BLOCK, (cb + 1) * S5_COL_BLOCK)
        ar = jnp.broadcast_to(ar_ref[:, cols], (SUBLANES, S5_COL_BLOCK))
        ai = jnp.broadcast_to(ai_ref[:, cols], (SUBLANES, S5_COL_BLOCK))

        def step(i, carry, cols=cols, ar=ar, ai=ai):
            r = pl.multiple_of(i * SUBLANES, SUBLANES)
            h_r, h_i = _cmul_add(ar, ai, carry[0], carry[1],
                                 hr_ref[pl.ds(r, SUBLANES), cols], hi_ref[pl.ds(r, SUBLANES), cols])
            hr_ref[pl.ds(r, SUBLANES), cols] = h_r
            hi_ref[pl.ds(r, SUBLANES), cols] = h_i
            return h_r, h_i

        zero = jnp.zeros((SUBLANES, S5_COL_BLOCK), F32)
        lax.fori_loop(0, seg, step, (zero, zero), unroll=2)

    end_r = hr_ref[rows - SUBLANES:rows, :]
    end_i = hi_ref[rows - SUBLANES:rows, :]
    as_r = pr_ref[rows - 1:rows, :]
    as_i = pi_ref[rows - 1:rows, :]
    if chained:
        @pl.when(pl.program_id(1) == 0)
        def _():
            cr_ref[...] = jnp.zeros_like(cr_ref)
            ci_ref[...] = jnp.zeros_like(ci_ref)

        cur_r, cur_i = cr_ref[...], ci_ref[...]
        for j in range(SUBLANES):
            sr_ref[j:j + 1, :] = cur_r
            si_ref[j:j + 1, :] = cur_i
            cur_r, cur_i = _cmul_add(as_r, as_i, cur_r, cur_i, end_r[j:j + 1], end_i[j:j + 1])
        cr_ref[...] = cur_r
        ci_ref[...] = cur_i
    else:
        sr_ref[...] = h0r_ref[...]
        si_ref[...] = h0i_ref[...]
    t_r, t_i = _cmul_add(as_r, as_i, sr_ref[...], si_ref[...], end_r, end_i)
    er_ref[...] = t_r
    ei_ref[...] = t_i

    slabs_per_window = MXU_DIM // (MXU_DIM // S5_STATE * S5_GROUP_CH)
    ys = []
    for w in range(S5_WIDTH // MXU_DIM):
        acc = d_ref[:, w * MXU_DIM:(w + 1) * MXU_DIM] * jnp.concatenate(
            [up_ref[c] for c in range(w * MXU_DIM // LANES, (w + 1) * MXU_DIM // LANES)], axis=1)
        for s in range(slabs_per_window):
            cols = slice((w * slabs_per_window + s) * MXU_DIM, (w * slabs_per_window + s + 1) * MXU_DIM)
            shape3 = (seg, SUBLANES, MXU_DIM)
            h_r, h_i = _cmul_add(pr_ref[:, cols].reshape(shape3), pi_ref[:, cols].reshape(shape3),
                                 sr_ref[:, cols][None], si_ref[:, cols][None],
                                 hr_ref[:, cols].reshape(shape3), hi_ref[:, cols].reshape(shape3))
            acc += _dot(h_r.reshape(rows, MXU_DIM).astype(BF16), wcr_ref[w * slabs_per_window + s])
            acc += _dot(h_i.reshape(rows, MXU_DIM).astype(BF16), wci_ref[w * slabs_per_window + s])
        ys.append(jax.nn.gelu(acc).astype(BF16))

    z = _dot(jnp.concatenate(ys, axis=1), wglu_ref[...])
    o = _rms(z[:, :S5_WIDTH] * jax.nn.sigmoid(z[:, S5_WIDTH:]), og_ref[...])
    for c in range(n_lane_blk):
        op_ref[c] = o[:, c * LANES:(c + 1) * LANES]

    for j in range(SUBLANES):
        for i0 in range(0, seg, 2 * SUBLANES):
            for c in range(n_lane_blk):
                o_ref[j * seg + i0:j * seg + i0 + 2 * SUBLANES, c * LANES:(c + 1) * LANES] = (
                    op_ref[c, pl.ds(i0 * SUBLANES + j, 2 * SUBLANES, stride=SUBLANES), :].astype(BF16))


def _s5_weights(lam_re, lam_im, b_re, b_im, c_re, c_im, d, log_step, seg):
    step = jnp.exp(log_step)[:, None]
    lr = jnp.minimum(lam_re, EIG_CLIP)
    li = lam_im
    mag = jnp.exp(lr * step)
    ar = mag * jnp.cos(li * step)
    ai = mag * jnp.sin(li * step)
    den = lr * lr + li * li
    fr = ((ar - 1.0) * lr + ai * li) / den
    fi = (ai * lr - (ar - 1.0) * li) / den
    bbar_r = fr[..., None] * b_re - fi[..., None] * b_im
    bbar_i = fr[..., None] * b_im + fi[..., None] * b_re
    t = jnp.arange(1, seg + 1, dtype=F32)[:, None, None]
    pm = jnp.exp(lr * step * t)
    p_r = jnp.repeat((pm * jnp.cos(li * step * t)).reshape(seg, S5_COLS), SUBLANES, axis=0)
    p_i = jnp.repeat((pm * jnp.sin(li * step * t)).reshape(seg, S5_COLS), SUBLANES, axis=0)

    gk = MXU_DIM // S5_GROUP_CH

    def in_map(bbar):
        bb = bbar.reshape(S5_GROUPS // gk, gk, S5_STATE, S5_GROUP_CH)
        w = jnp.einsum('kgnh,gG->kghGn', bb, jnp.eye(gk, dtype=F32))
        return w.reshape(S5_GROUPS // gk, MXU_DIM, gk * S5_STATE).astype(BF16)

    gs = MXU_DIM // S5_STATE
    n_slab = S5_COLS // MXU_DIM
    per_win = MXU_DIM // (gs * S5_GROUP_CH)

    def out_map(c):
        cc = c.reshape(n_slab // per_win, per_win, gs, S5_GROUP_CH, S5_STATE)
        w = jnp.einsum('wsghn,sS,gG->wsgnSGh', cc, jnp.eye(per_win, dtype=F32), jnp.eye(gs, dtype=F32))
        return w.reshape(n_slab, MXU_DIM, MXU_DIM).astype(BF16)

    return dict(wbr=in_map(bbar_r), wbi=in_map(bbar_i), wcr=out_map(c_re), wci=out_map(-c_im),
                ar=ar.reshape(1, S5_COLS), ai=ai.reshape(1, S5_COLS), pr=p_r, pi=p_i,
                d=d.reshape(1, S5_WIDTH))


def _s5_call(x, x_first, norm_g, w_in, sw, w_glu, og, seg, chained, h0=None, prev=None):
    rows = seg * SUBLANES
    const2 = lambda *_: (0, 0)
    const3 = lambda *_: (0, 0, 0)
    if chained:
        n_chunk = SEQ // rows
        grid = (BATCH, n_chunk)
        row_map = lambda b, c: (b * n_chunk + c, 0)
        x_map = row_map
        end_map = lambda b, c: (b, 0, 0)
        n_end = BATCH
        sem = ("parallel", "arbitrary")
    else:
        grid = (DEC_BATCH // SUBLANES,)
        row_map = lambda k: (N_PROMPT // rows + k, 0)
        x_map = lambda k: (x_first + k, 0)
        end_map = lambda k: (k, 0, 0)
        n_end = DEC_BATCH // SUBLANES
        sem = ("parallel",)
    half = S5_COLS // 2
    in_specs = [pl.BlockSpec((rows, D_MODEL), x_map),
                pl.BlockSpec((1, D_MODEL), const2),
                pl.BlockSpec((D_MODEL, S5_WIDTH), const2),
                pl.BlockSpec((2, MXU_DIM, half), const3),
                pl.BlockSpec((2, MXU_DIM, half), const3),
                pl.BlockSpec((S5_COLS // MXU_DIM, MXU_DIM, MXU_DIM), const3),
                pl.BlockSpec((S5_COLS // MXU_DIM, MXU_DIM, MXU_DIM), const3),
                pl.BlockSpec((1, S5_COLS), const2),
                pl.BlockSpec((1, S5_COLS), const2),
                pl.BlockSpec((rows, S5_COLS), const2),
                pl.BlockSpec((rows, S5_COLS), const2),
                pl.BlockSpec((1, S5_WIDTH), const2),
                pl.BlockSpec((S5_WIDTH, 2 * S5_WIDTH), const2),
                pl.BlockSpec((1, S5_WIDTH), const2)]
    args = [x, norm_g.reshape(1, D_MODEL), w_in[:, :S5_WIDTH], sw['wbr'], sw['wbi'], sw['wcr'], sw['wci'], sw['ar'], sw['ai'],
            sw['pr'][:rows], sw['pi'][:rows], sw['d'], w_glu, og.reshape(1, S5_WIDTH)]
    scratch = [pltpu.VMEM((S5_WIDTH // LANES, rows, LANES), F32),
               pltpu.VMEM((rows, S5_COLS), F32), pltpu.VMEM((rows, S5_COLS), F32),
               pltpu.VMEM((SUBLANES, S5_COLS), F32), pltpu.VMEM((SUBLANES, S5_COLS), F32),
               pltpu.VMEM((S5_WIDTH // LANES, rows, LANES), F32)]
    aliases = {}
    if chained:
        scratch += [pltpu.VMEM((1, S5_COLS), F32), pltpu.VMEM((1, S5_COLS), F32)]
    else:
        in_specs += [pl.BlockSpec((None, SUBLANES, S5_COLS), end_map),
                     pl.BlockSpec((None, SUBLANES, S5_COLS), end_map)]
        args += [h0[0], h0[1]]
    if prev is not None:
        in_specs.append(pl.BlockSpec(memory_space=pl.ANY))
        args.append(prev)
        aliases = {len(args) - 1: 0}
    kern = functools.partial(_s5_kernel, seg=seg, chained=chained)
    if prev is not None:
        kern = _drop_inputs(kern, len(args) - 1, 1)
    return pl.pallas_call(
        kern,
        grid=grid,
        in_specs=in_specs,
        out_specs=[pl.BlockSpec((rows, S5_WIDTH), row_map),
                   pl.BlockSpec((None, SUBLANES, S5_COLS), end_map),
                   pl.BlockSpec((None, SUBLANES, S5_COLS), end_map)],
        out_shape=[jax.ShapeDtypeStruct((N_TOK, S5_WIDTH), BF16),
                   jax.ShapeDtypeStruct((n_end, SUBLANES, S5_COLS), F32),
                   jax.ShapeDtypeStruct((n_end, SUBLANES, S5_COLS), F32)],
        scratch_shapes=scratch,
        input_output_aliases=aliases,
        compiler_params=_params(*sem),
        name="s5_chained" if chained else "s5_streams",
    )(*args)


def _drop_inputs(kern, start, n):
    def wrapped(*refs):
        return kern(*refs[:start], *refs[start + n:])
    return wrapped


def _gmlp_kernel(*refs, chunk, emit_vn):
    if emit_vn:
        x_ref, ng_ref, win_ref, lng_ref, lnb_ref, ws_ref, bst_ref, og_ref, o_ref, vn_ref, acc_ref = refs
    else:
        x_ref, ng_ref, win_ref, lng_ref, lnb_ref, ws_ref, bst_ref, og_ref, o_ref, acc_ref = refs
    rows = x_ref.shape[0]
    uv = _dot(_rms(x_ref[...], ng_ref[...]).astype(BF16), win_ref[...])
    u = jax.nn.gelu(uv[:, :GM_WIDTH])
    v = jax.nn.gelu(uv[:, GM_WIDTH:])
    tri = (lax.broadcasted_iota(jnp.int32, (chunk, chunk), 0)
           >= lax.broadcasted_iota(jnp.int32, (chunk, chunk), 1))
    for h in range(GM_HEADS):
        cols = slice(h * GM_HEAD_DIM, (h + 1) * GM_HEAD_DIM)
        vh = v[:, cols]
        mu = jnp.mean(vh, axis=-1, keepdims=True)
        cen = vh - mu
        var = jnp.mean(cen * cen, axis=-1, keepdims=True)
        vn = cen * lax.rsqrt(var + EPS) * lng_ref[:, cols] + lnb_ref[:, cols]
        if emit_vn:
            vn_ref[:, cols] = vn
        w = jnp.where(tri, ws_ref[h], 0.0).astype(BF16)
        vb = vn.astype(BF16)
        bias = bst_ref[:, h:h + 1]
        for c in range(rows // chunk):
            rs = slice(c * chunk, (c + 1) * chunk)
            acc_ref[rs, cols] = u[rs, cols] * (_dot(w, vb[rs]) + bias)
    o_ref[...] = _rms(acc_ref[...], og_ref[...]).astype(BF16)


def _gmlp_call(x, x_first, norm_g, w_in, ln_g, ln_b, w_s, b_s, og, rows, chunk, first_block, n_blocks, emit_vn,
               prev=None):
    const2 = lambda i: (0, 0)
    in_specs = [pl.BlockSpec((rows, D_MODEL), lambda i: (x_first + i, 0)),
                pl.BlockSpec((1, D_MODEL), const2),
                pl.BlockSpec((D_MODEL, 2 * GM_WIDTH), const2),
                pl.BlockSpec((1, GM_WIDTH), const2),
                pl.BlockSpec((1, GM_WIDTH), const2),
                pl.BlockSpec((GM_HEADS, chunk, chunk), lambda i: (0, 0, 0)),
                pl.BlockSpec((chunk, GM_HEADS), const2),
                pl.BlockSpec((1, GM_WIDTH), const2)]
    args = [x, norm_g.reshape(1, D_MODEL), w_in[:, S5_WIDTH:], ln_g.reshape(1, GM_WIDTH), ln_b.reshape(1, GM_WIDTH),
            w_s[:, :chunk, :chunk], b_s[:, :chunk].T, og.reshape(1, GM_WIDTH)]
    out_specs = [pl.BlockSpec((rows, GM_WIDTH), lambda i: (first_block + i, 0))]
    out_shape = [jax.ShapeDtypeStruct((N_TOK, GM_WIDTH), BF16)]
    if emit_vn:
        out_specs.append(pl.BlockSpec((rows, GM_WIDTH), lambda i: (i, 0)))
        out_shape.append(jax.ShapeDtypeStruct((n_blocks * rows, GM_WIDTH), F32))
    kern = functools.partial(_gmlp_kernel, chunk=chunk, emit_vn=emit_vn)
    aliases = {}
    if prev is not None:
        in_specs.append(pl.BlockSpec(memory_space=pl.ANY))
        args.append(prev)
        aliases = {len(args) - 1: 0}
        kern = _drop_inputs(kern, len(args) - 1, 1)
    return pl.pallas_call(
        kern,
        grid=(n_blocks,),
        in_specs=in_specs,
        out_specs=out_specs,
        out_shape=out_shape,
        scratch_shapes=[pltpu.VMEM((rows, GM_WIDTH), F32)],
        input_output_aliases=aliases,
        compiler_params=_params("parallel"),
        name="gmlp_vn" if emit_vn else "gmlp",
    )(*args)


def _post_mix_kernel(x_ref, a_ref, b_ref, wa_ref, wb_ref, xg_ref, wq_ref, k_ref, v_ref, wo_ref,
                     fg_ref, wt_ref, rb_ref, x2_ref, h_ref, e_ref, gate_ref, cnt_ref, att_ref, *, n_batch, kv_layer):
    rows = x_ref.shape[0]
    tb = rows // n_batch
    x = x_ref[...] + _dot(a_ref[...], wa_ref[...]) + _dot(b_ref[...], wb_ref[...])
    q = _dot(_rms(x, xg_ref[...]).astype(BF16), wq_ref[...]).astype(BF16)
    scale = XA_HEAD_DIM ** -0.5
    for b in range(n_batch):
        rs = slice(b * tb, (b + 1) * tb)
        for h in range(XA_HEADS):
            cols = slice(h * XA_HEAD_DIM, (h + 1) * XA_HEAD_DIM)
            if kv_layer is None:
                k_h, v_h = k_ref[b, :, cols], v_ref[b, :, cols]
            else:
                k_h, v_h = k_ref[b, :, h, :].astype(BF16), v_ref[b, :, h, :].astype(BF16)
            s = lax.dot_general(q[rs, cols], k_h, (((1,), (1,)), ((), ())),
                                preferred_element_type=F32) * scale
            e = jnp.exp(s - jnp.max(s, axis=-1, keepdims=True))
            p = e / jnp.sum(e, axis=-1, keepdims=True)
            att_ref[rs, cols] = _dot(p.astype(BF16), v_h).astype(BF16)
    x = x + _dot(att_ref[...], wo_ref[...])
    x2_ref[...] = x
    _route_rows(x, fg_ref, wt_ref, rb_ref, h_ref, e_ref, gate_ref, cnt_ref)


def _post_mix(x, x_first, a, b, w_out, xg, wq, k, v, wo, fg, rw, rb,
              rows, n_batch, first_block, n_blocks, blocks_per_kv, prev=None, kv_layer=None):
    const2 = lambda i: (0, 0)
    if kv_layer is None:
        kv_spec = pl.BlockSpec((n_batch, N_MEM, D_MODEL), lambda i: (i // blocks_per_kv, 0, 0))
    else:
        kv_spec = pl.BlockSpec((None, n_batch, N_MEM, XA_HEADS, XA_HEAD_DIM),
                               lambda i: (kv_layer, i // blocks_per_kv, 0, 0, 0))
    row_map = lambda i: (first_block + i, 0)
    d = D_MODEL
    in_specs = [pl.BlockSpec((rows, d), lambda i: (x_first + i, 0)),
                pl.BlockSpec((rows, S5_WIDTH), row_map),
                pl.BlockSpec((rows, GM_WIDTH), row_map),
                pl.BlockSpec((S5_WIDTH, d), const2),
                pl.BlockSpec((GM_WIDTH, d), const2),
                pl.BlockSpec((1, d), const2),
                pl.BlockSpec((d, d), const2),
                kv_spec, kv_spec,
                pl.BlockSpec((d, d), const2),
                pl.BlockSpec((1, d), const2),
                pl.BlockSpec((N_EXPERTS, d), const2),
                pl.BlockSpec((N_EXPERTS, 1), const2)]
    args = [x, a, b, w_out[:S5_WIDTH], w_out[S5_WIDTH:], xg.reshape(1, d), wq, k, v, wo,
            fg.reshape(1, d), rw.T.astype(BF16), rb.reshape(N_EXPERTS, 1)]
    kern = functools.partial(_post_mix_kernel, n_batch=n_batch, kv_layer=kv_layer)
    aliases = {}
    if prev is not None:
        n_in = len(args)
        in_specs += [pl.BlockSpec(memory_space=pl.ANY)] * len(prev)
        args += list(prev)
        aliases = {n_in + o: o for o in range(len(prev))}
        kern = _drop_inputs(kern, n_in, len(prev))
    return pl.pallas_call(
        kern,
        grid=(n_blocks,),
        in_specs=in_specs,
        out_specs=[pl.BlockSpec((rows, d), row_map),
                   pl.BlockSpec((rows // SUBLANES,) + PIECE_TILE, lambda i: (first_block + i, 0, 0, 0)),
                   pl.BlockSpec((TOP_K, rows), lambda i: (0, first_block + i)),
                   pl.BlockSpec((TOP_K, rows), lambda i: (0, first_block + i)),
                   pl.BlockSpec((None, N_EXPERTS, LANES), lambda i: (i, 0, 0))],
        out_shape=[jax.ShapeDtypeStruct((N_TOK, d), F32),
                   jax.ShapeDtypeStruct((N_TOK // SUBLANES,) + PIECE_TILE, F32),
                   jax.ShapeDtypeStruct((TOP_K, N_TOK), jnp.int32),
                   jax.ShapeDtypeStruct((TOP_K, N_TOK), F32),
                   jax.ShapeDtypeStruct((n_blocks, N_EXPERTS, LANES), F32)],
        scratch_shapes=[pltpu.VMEM((rows, d), BF16)],
        input_output_aliases=aliases,
        compiler_params=_params("parallel"),
        name="post_mix",
    )(*args)


def _moe_kernel(be_ref, nu_ref, first_ref, slot_ref, nxt_ref, half_ref,
                x_ref, wg_hbm, wu_hbm, wd_hbm, bg_ref, bu_ref, bd_ref, o_ref,
                wbuf_ref, wgb_ref, wub_ref, wdb_ref, sem, *, layer):
    i = pl.program_id(0)
    active = i < nu_ref[0]

    def copies(e, s):
        return [pltpu.make_async_copy(w.at[layer, e], wbuf_ref.at[s, j], sem.at[s, j])
                for j, w in enumerate((wg_hbm, wu_hbm, wd_hbm))]

    @pl.when(jnp.logical_and(active, first_ref[i] == 1))
    def _():
        s = slot_ref[i]

        @pl.when(i == 0)
        def _():
            for c in copies(be_ref[0], s):
                c.start()

        @pl.when(nxt_ref[i] >= 0)
        def _():
            for c in copies(nxt_ref[i], 1 - s):
                c.start()

        for c in copies(be_ref[i], s):
            c.wait()

        def cast(r, _):
            rs = pl.ds(pl.multiple_of(r * MOE_CAST_ROWS, MOE_CAST_ROWS), MOE_CAST_ROWS)
            wgb_ref[rs, :] = wbuf_ref[s, 0, rs, :].astype(BF16)
            wub_ref[rs, :] = wbuf_ref[s, 1, rs, :].astype(BF16)
            wdb_ref[rs, :] = wbuf_ref[s, 2, rs, :].astype(BF16)
            return 0
        lax.fori_loop(0, D_MODEL // MOE_CAST_ROWS, cast, 0)

    def experts(n_rows):
        x = jnp.concatenate([x_ref[:n_rows // SUBLANES, c].reshape(n_rows, LANES) for c in range(PIECES_PER_ROW)],
                            axis=1).astype(BF16)
        g = jnp.minimum(_dot(x, wgb_ref[...]) + bg_ref[...], SWIGLU_LIMIT)
        u = jnp.clip(_dot(x, wub_ref[...]) + bu_ref[...], -SWIGLU_LIMIT, SWIGLU_LIMIT)
        a = g * jax.nn.sigmoid(SWIGLU_ALPHA * g) * (u + 1.0)
        o_ref[:n_rows, :] = _dot(a.astype(BF16), wdb_ref[...]) + bd_ref[...]

    pl.when(jnp.logical_and(active, half_ref[i] == 0))(lambda: experts(MOE_ROWS))
    pl.when(jnp.logical_and(active, half_ref[i] == 1))(lambda: experts(MOE_ROWS // 2))


def _moe_call(layer, xs, sched, w_gate, b_gate, w_up, b_up, w_down, b_down):
    def row_map(i, be, nu, *_):
        return (jnp.minimum(i, nu[0] - 1), 0)

    def piece_map(i, be, nu, *_):
        return (jnp.minimum(i, nu[0] - 1), 0, 0, 0)

    def b_map(i, be, *_):
        return (layer, be[i], 0, 0)

    b_spec = pl.BlockSpec((None, None, 1, D_FF), b_map)
    w_spec = pl.BlockSpec(memory_space=pl.ANY)
    grid_spec = pltpu.PrefetchScalarGridSpec(
        num_scalar_prefetch=6,
        grid=(MOE_BLOCKS,),
        in_specs=[pl.BlockSpec((MOE_ROWS // SUBLANES,) + PIECE_TILE, piece_map),
                  w_spec, w_spec, w_spec, b_spec, b_spec, b_spec],
        out_specs=pl.BlockSpec((MOE_ROWS, D_MODEL), row_map),
        scratch_shapes=[pltpu.VMEM((2, 3, D_MODEL, D_FF), F32)]
        + [pltpu.VMEM((D_MODEL, D_FF), BF16)] * 3
        + [pltpu.SemaphoreType.DMA((2, 3))],
    )
    shape4 = (DEPTH, N_EXPERTS, 1, D_FF)
    return pl.pallas_call(
        functools.partial(_moe_kernel, layer=layer),
        grid_spec=grid_spec,
        out_shape=jax.ShapeDtypeStruct((MOE_BLOCKS * MOE_ROWS, D_MODEL), F32),
        compiler_params=pltpu.CompilerParams(dimension_semantics=("arbitrary",),
                                             vmem_limit_bytes=56 * 1024 * 1024),
        name="moe_experts",
    )(*sched, xs, w_gate, w_up, w_down, b_gate.reshape(shape4), b_up.reshape(shape4),
      b_down.reshape(shape4))


DISPATCH_TOKENS = 48
DMA_PIECES = 128
DISPATCH_DMAS = DISPATCH_TOKENS * PIECES_PER_ROW // DMA_PIECES
DISPATCH_CHUNKS = N_TOK // SC_WORKERS // DISPATCH_TOKENS
IDX_ROWS = 16
SC_LANES = 16
MOE_ROWS_TOTAL = MOE_BLOCKS * MOE_ROWS
TILE_PIECES = PIECES_PER_ROW * SUBLANES


def _piece_index_rows(p_v, idx_v, row0, tok0, n_dma):
    lane = lax.iota(jnp.int32, SC_LANES)
    in_tile = lane & (SUBLANES - 1)
    piece = (lane >> 3) * SUBLANES
    per_vec = SC_LANES // SUBLANES
    tiles_per_dma = DMA_PIECES // TILE_PIECES
    for m in range(n_dma):
        for tr in range(tiles_per_dma):
            first = plsc.load_gather(p_v, [in_tile + (tok0 + (m * tiles_per_dma + tr) * SUBLANES)])
            for v in range(PIECES_PER_ROW // per_vec):
                idx_v[row0 + m, pl.ds(tr * TILE_PIECES + v * SC_LANES, SC_LANES)] = (
                    first + (piece + v * per_vec * SUBLANES))


def _dispatch(h4, dest_t):
    first_piece = ((dest_t // SUBLANES) * TILE_PIECES + dest_t % SUBLANES).reshape(-1)
    chunk_pieces = DISPATCH_TOKENS * PIECES_PER_ROW
    mesh = plsc.VectorSubcoreMesh(core_axis_name="c", subcore_axis_name="s")

    @functools.partial(
        pl.kernel, mesh=mesh,
        out_type=jax.ShapeDtypeStruct((MOE_ROWS_TOTAL * PIECES_PER_ROW, LANES), F32),
        scratch_types=[pltpu.VMEM((chunk_pieces, LANES), F32), pltpu.VMEM((IDX_ROWS, DMA_PIECES), jnp.int32),
                       pltpu.VMEM((TOP_K * DISPATCH_TOKENS,), jnp.int32), pltpu.SemaphoreType.DMA],
        compiler_params=pltpu.CompilerParams(needs_layout_passes=False),
    )
    def scatter_rows(h_hbm, p_hbm, o_hbm, rows_v, idx_v, p_v, sem):
        wid = lax.axis_index("s") * SC_CORES + lax.axis_index("c")

        @pl.loop(0, DISPATCH_CHUNKS)
        def _(j):
            q = wid * DISPATCH_CHUNKS + j
            pltpu.sync_copy(h_hbm.at[pl.ds(pl.multiple_of(q * chunk_pieces, SUBLANES), chunk_pieces)], rows_v)
            for k in range(TOP_K):
                pltpu.sync_copy(
                    p_hbm.at[pl.ds(pl.multiple_of(k * N_TOK + q * DISPATCH_TOKENS, SUBLANES), DISPATCH_TOKENS)],
                    p_v.at[pl.ds(k * DISPATCH_TOKENS, DISPATCH_TOKENS)])
            for k in range(TOP_K):
                _piece_index_rows(p_v, idx_v, k * DISPATCH_DMAS, k * DISPATCH_TOKENS, DISPATCH_DMAS)
            copies = [pltpu.async_copy(rows_v.at[pl.ds(m * DMA_PIECES, DMA_PIECES)],
                                       o_hbm.at[idx_v.at[k * DISPATCH_DMAS + m]], sem)
                      for k in range(TOP_K) for m in range(DISPATCH_DMAS)]
            for cp in copies:
                cp.wait()

    xs = scatter_rows(h4.reshape(N_TOK * PIECES_PER_ROW, LANES), first_piece)
    return xs.reshape((MOE_ROWS_TOTAL // SUBLANES,) + PIECE_TILE)


def _route(e_t, cnt_prompt, cnt_sample):
    cnt_sample = jnp.sum(cnt_sample[:, :, 0].reshape(SAMPLE_BLOCKS, -1, N_EXPERTS), axis=1)
    blk_cnt = jnp.concatenate([cnt_prompt[:, :, 0], cnt_sample]).astype(jnp.int32)
    counts = jnp.sum(blk_cnt, axis=0)
    padded = ((counts + MOE_ROWS - 1) // MOE_ROWS) * MOE_ROWS
    pend = jnp.cumsum(padded)
    pstart = pend - padded
    base = pstart[None, :] + jnp.cumsum(blk_cnt, axis=0) - blk_cnt
    dest_t = _dest(e_t, base.astype(F32)[:, :, None])
    n_used = (pend[-1] // MOE_ROWS).astype(jnp.int32)
    blk = jnp.minimum(jnp.arange(MOE_BLOCKS, dtype=jnp.int32), n_used - 1) * MOE_ROWS
    blk_exp = jnp.minimum(jnp.sum((pend[None, :] <= blk[:, None]).astype(jnp.int32), axis=1), N_EXPERTS - 1)
    ids = jnp.arange(MOE_BLOCKS, dtype=jnp.int32)
    first = jnp.logical_and(ids < n_used, jnp.logical_or(ids == 0, blk_exp != jnp.roll(blk_exp, 1)))
    first = first.astype(jnp.int32)
    slot = (jnp.cumsum(first) - 1) % 2
    eid = jnp.arange(N_EXPERTS, dtype=jnp.int32)
    later = jnp.logical_and(eid[None, :] > eid[:, None], (counts > 0)[None, :])
    nxt_e = jnp.min(jnp.where(later, eid[None, :], N_EXPERTS), axis=1)
    nxt = jnp.where(nxt_e < N_EXPERTS, nxt_e, -1)[blk_exp].astype(jnp.int32)
    real = jnp.clip((pstart + counts)[blk_exp] - blk, 0, MOE_ROWS)
    half = (real <= MOE_ROWS // 2).astype(jnp.int32)
    return dest_t, (blk_exp, n_used.reshape(1), first, slot.astype(jnp.int32), nxt, half)


def kernel(x_prompt, x_sample, mem_prompt, state_s5_re, state_s5_im, cache_mem_k, cache_mem_v, norm_mix_g, w_in, s5_lambda_re, s5_lambda_im, s5_b_re, s5_b_im, s5_c_re, s5_c_im, s5_d, s5_log_step, s5_w_glu, gm_ln_g, gm_ln_b, gm_w_s, gm_b_s, out_norm_g, w_out, xa_norm_g, mem_norm_g, w_xq, w_xk, w_xv, w_xo, ffn_norm_g, router_w, router_b, e_w_gate, e_b_gate, e_w_up, e_b_up, e_w_down, e_b_down, final_norm_g):
    xs = (x_prompt.reshape(N_PROMPT, D_MODEL), x_sample.reshape(N_SAMPLE, D_MODEL))
    mem = mem_prompt.reshape(BATCH * N_MEM, D_MODEL)
    outs = {k: [] for k in ('s5r_p', 's5i_p', 'mk_p', 'mv_p', 's5r_s', 's5i_s', 'gmv_s')}
    for l in range(DEPTH):
        unified = len(xs) == 1

        def sample_first(rows):
            return N_PROMPT // rows if unified else 0

        in_proj = (norm_mix_g[l], w_in[l].astype(BF16))
        sw = _s5_weights(s5_lambda_re[l], s5_lambda_im[l], s5_b_re[l], s5_b_im[l], s5_c_re[l], s5_c_im[l],
                         s5_d[l], s5_log_step[l], S5_SEG_PROMPT)
        w_glu = s5_w_glu[l].astype(BF16)
        og = out_norm_g[l]
        ms5, er_p, ei_p = _s5_call(xs[0], 0, *in_proj, sw, w_glu, og[:S5_WIDTH], S5_SEG_PROMPT, True)
        h0 = (state_s5_re[l].reshape(DEC_BATCH // SUBLANES, SUBLANES, S5_COLS),
              state_s5_im[l].reshape(DEC_BATCH // SUBLANES, SUBLANES, S5_COLS))
        ms5, er_s, ei_s = _s5_call(xs[-1], sample_first(DEC_SEQ * SUBLANES), *in_proj, sw, w_glu, og[:S5_WIDTH],
                                   DEC_SEQ, False, h0=h0, prev=ms5)
        outs['s5r_p'].append(er_p[:, SUBLANES - 1].reshape(BATCH, S5_GROUPS, S5_STATE))
        outs['s5i_p'].append(ei_p[:, SUBLANES - 1].reshape(BATCH, S5_GROUPS, S5_STATE))
        outs['s5r_s'].append(er_s.reshape(DEC_BATCH, S5_GROUPS, S5_STATE))
        outs['s5i_s'].append(ei_s.reshape(DEC_BATCH, S5_GROUPS, S5_STATE))

        (mgm,) = _gmlp_call(xs[0], 0, *in_proj, gm_ln_g[l], gm_ln_b[l], gm_w_s[l], gm_b_s[l], og[S5_WIDTH:],
                            GM_ROWS_PROMPT, GM_CHUNK, 0, N_PROMPT // GM_ROWS_PROMPT, False)
        mgm, vn = _gmlp_call(xs[-1], sample_first(N_SAMPLE), *in_proj, gm_ln_g[l], gm_ln_b[l], gm_w_s[l],
                             gm_b_s[l], og[S5_WIDTH:], N_SAMPLE, DEC_SEQ, N_PROMPT // N_SAMPLE, 1, True, prev=mgm)
        outs['gmv_s'].append(vn.reshape(DEC_BATCH, DEC_SEQ, GM_HEADS, GM_HEAD_DIM))
        mk = _norm_matmul((mem,), mem_norm_g[l], w_xk[l].astype(BF16))
        mv = _norm_matmul((mem,), mem_norm_g[l], w_xv[l].astype(BF16))
        outs['mk_p'].append(mk.reshape(BATCH, N_MEM, XA_HEADS, XA_HEAD_DIM))
        outs['mv_p'].append(mv.reshape(BATCH, N_MEM, XA_HEADS, XA_HEAD_DIM))
        shared = (w_out[l].astype(BF16), xa_norm_g[l], w_xq[l].astype(BF16))
        tail = (w_xo[l].astype(BF16), ffn_norm_g[l], router_w[l], router_b[l])
        rows_s = XA_SAMPLE_BATCHES * DEC_SEQ
        post_p = _post_mix(xs[0], 0, ms5, mgm, *shared, mk.reshape(BATCH, N_MEM, D_MODEL).astype(BF16),
                           mv.reshape(BATCH, N_MEM, D_MODEL).astype(BF16), *tail,
                           XA_ROWS_PROMPT, 1, 0, N_PROMPT // XA_ROWS_PROMPT, SEQ // XA_ROWS_PROMPT)
        post_s = _post_mix(xs[-1], sample_first(rows_s), ms5, mgm, *shared,
                           cache_mem_k, cache_mem_v, *tail,
                           rows_s, XA_SAMPLE_BATCHES, N_PROMPT // rows_s, N_SAMPLE // rows_s, 1,
                           prev=post_p[:4], kv_layer=l)
        x, h, e_t, gates_t = post_s[:4]

        dest_t, sched = _route(e_t, post_p[4], post_s[4])
        ys = _moe_call(l, _dispatch(h, dest_t), sched, e_w_gate, e_b_gate, e_w_up, e_b_up, e_w_down, e_b_down)
        x = _combine(x, ys[dest_t.reshape(-1)], gates_t.T, final_norm_g if l == DEPTH - 1 else None)
        xs = (x,)

    y_prompt = x[0].reshape(BATCH, SEQ, D_MODEL)
    y_sample = x[1].reshape(DEC_BATCH, DEC_SEQ, D_MODEL)
    st = jnp.stack
    return (y_prompt, y_sample, st(outs['s5r_p']), st(outs['s5i_p']), st(outs['mk_p']), st(outs['mv_p']),
            st(outs['s5r_s']), st(outs['s5i_s']), st(outs['gmv_s']))
```

```python
import functools

import jax
import jax.numpy as jnp
from jax import lax
from jax.experimental import pallas as pl
from jax.experimental.pallas import tpu as pltpu
from jax.experimental.pallas import tpu_sc as plsc

F32 = jnp.float32
BF16 = jnp.bfloat16

D_MODEL = 1024
BATCH = 4
SEQ = 4096
DEPTH = 2
DEC_BATCH = 16
DEC_SEQ = 32
S5_GROUPS = 32
S5_GROUP_CH = 16
S5_STATE = 64
S5_WIDTH = S5_GROUPS * S5_GROUP_CH
S5_COLS = S5_GROUPS * S5_STATE
GM_HEADS = 4
GM_HEAD_DIM = 128
GM_WIDTH = GM_HEADS * GM_HEAD_DIM
GM_CHUNK = 128
IN_PROJ = S5_WIDTH + 2 * GM_WIDTH
N_MEM = 256
XA_HEADS = 4
XA_HEAD_DIM = D_MODEL // XA_HEADS
N_EXPERTS = 32
TOP_K = 4
D_FF = D_MODEL
SWIGLU_LIMIT = 7.0
SWIGLU_ALPHA = 1.702
EPS = 1e-6
EIG_CLIP = -1e-4

N_PROMPT = BATCH * SEQ
N_SAMPLE = DEC_BATCH * DEC_SEQ
N_TOK = N_PROMPT + N_SAMPLE

SUBLANES = 8
LANES = 128
MXU_DIM = 256
VMEM_LIMIT = 48 * 1024 * 1024
SC_CORES = 2
SC_SUBCORES = 16
SC_WORKERS = SC_CORES * SC_SUBCORES

PIECE_TILE = (D_MODEL // LANES, SUBLANES, LANES)
PIECES_PER_ROW = D_MODEL // LANES

ROW_BLOCK = 512
S5_SEG_PROMPT = 64
S5_COL_BLOCK = 512
GM_ROWS_PROMPT = 256
XA_ROWS_PROMPT = 512
XA_SAMPLE_BATCHES = 4
MOE_ROWS = 512
MOE_BLOCKS = -(-N_TOK * TOP_K // MOE_ROWS) + N_EXPERTS
MOE_CAST_ROWS = 128


def _params(*sem):
    return pltpu.CompilerParams(dimension_semantics=sem, vmem_limit_bytes=VMEM_LIMIT)


def _rms(x, g):
    return x * lax.rsqrt(jnp.mean(x * x, axis=-1, keepdims=True) + EPS) * g


def _dot(a, b):
    return jnp.dot(a, b, preferred_element_type=F32)


PROMPT_BLOCKS = N_PROMPT // ROW_BLOCK
SAMPLE_BLOCKS = N_SAMPLE // ROW_BLOCK


def _token_sources(xs):
    if len(xs) == 1:
        return [pl.BlockSpec((ROW_BLOCK, D_MODEL), lambda i: (i, 0))]
    return [pl.BlockSpec((ROW_BLOCK, D_MODEL), lambda i: (jnp.minimum(i, PROMPT_BLOCKS - 1), 0)),
            pl.BlockSpec((ROW_BLOCK, D_MODEL), lambda i: (jnp.maximum(i - PROMPT_BLOCKS, 0), 0))]


def _for_group(refs, body):
    if len(refs) == 1:
        body(refs[0])
        return
    i = pl.program_id(0)
    pl.when(i < PROMPT_BLOCKS)(lambda: body(refs[0]))
    pl.when(i >= PROMPT_BLOCKS)(lambda: body(refs[1]))


def _mem_kv_kernel(m_ref, g_ref, wk_ref, wv_ref, k5_ref, v5_ref, kb_ref, vb_ref):
    m = _rms(m_ref[...], g_ref[...]).astype(BF16)
    for w_ref, o5_ref, ob_ref in ((wk_ref, k5_ref, kb_ref), (wv_ref, v5_ref, vb_ref)):
        kv = _dot(m, w_ref[...])
        ob_ref[...] = kv.astype(BF16)
        for h in range(XA_HEADS):
            o5_ref[:, h, :] = kv[:, h * XA_HEAD_DIM:(h + 1) * XA_HEAD_DIM]


def _mem_kv(mem, g, wk, wv, layer, prev=None):
    d = D_MODEL
    const2 = lambda b: (0, 0)
    in_specs = [pl.BlockSpec((N_MEM, d), lambda b: (b, 0)),
                pl.BlockSpec((1, d), const2),
                pl.BlockSpec((d, d), const2),
                pl.BlockSpec((d, d), const2)]
    args = [mem, g.reshape(1, d), wk, wv]
    kern = _mem_kv_kernel
    aliases = {}
    if prev is not None:
        in_specs += [pl.BlockSpec(memory_space=pl.ANY)] * 2
        args += list(prev)
        aliases = {4: 0, 5: 1}
        kern = _drop_inputs(kern, 4, 2)
    out5 = pl.BlockSpec((None, None, N_MEM, XA_HEADS, XA_HEAD_DIM), lambda b: (layer, b, 0, 0, 0))
    outb = pl.BlockSpec((None, N_MEM, d), lambda b: (b, 0, 0))
    shape5 = jax.ShapeDtypeStruct((DEPTH, BATCH, N_MEM, XA_HEADS, XA_HEAD_DIM), F32)
    shapeb = jax.ShapeDtypeStruct((BATCH, N_MEM, d), BF16)
    return pl.pallas_call(
        kern,
        grid=(BATCH,),
        in_specs=in_specs,
        out_specs=[out5, out5, outb, outb],
        out_shape=[shape5, shape5, shapeb, shapeb],
        input_output_aliases=aliases,
        compiler_params=_params("parallel"),
        name="mem_kv",
    )(*args)


def _route_rows(x, g_ref, wt_ref, b_ref, h_ref, e_ref, gate_ref, cnt_ref):
    h = _rms(x, g_ref[...])
    for c in range(PIECES_PER_ROW):
        h_ref[:, c] = h[:, c * LANES:(c + 1) * LANES].reshape(h.shape[0] // SUBLANES, SUBLANES, LANES)
    logits = lax.dot_general(wt_ref[...], h.astype(BF16), (((1,), (1,)), ((), ())),
                             preferred_element_type=F32) + b_ref[...]
    sub = lax.broadcasted_iota(jnp.int32, logits.shape, 0)
    work = logits
    chosen = jnp.zeros(logits.shape, F32)
    vals, idxs = [], []
    for _ in range(TOP_K):
        m = jnp.max(work, axis=0, keepdims=True)
        idx = jnp.min(jnp.where(work == m, sub, N_EXPERTS), axis=0, keepdims=True)
        sel = sub == idx
        vals.append(m)
        idxs.append(idx)
        work = jnp.where(sel, -jnp.inf, work)
        chosen = jnp.where(sel, 1.0, chosen)
    ex = [jnp.exp(v - vals[0]) for v in vals]
    den = ex[0] + ex[1] + ex[2] + ex[3]
    gate_ref[...] = jnp.concatenate([e / den for e in ex], axis=0)
    e_ref[...] = jnp.concatenate(idxs, axis=0)
    cnt_ref[...] = jnp.broadcast_to(jnp.sum(chosen, axis=1, keepdims=True), (N_EXPERTS, LANES))


def _dest_kernel(e_ref, base_ref, d_ref):
    e = e_ref[...]
    rows = e.shape[1]
    sub = lax.broadcasted_iota(jnp.int32, (N_EXPERTS, rows), 0)
    hot = [sub == e[k:k + 1, :] for k in range(TOP_K)]
    chosen = sum(jnp.where(hk, 1.0, 0.0) for hk in hot)
    earlier = (lax.broadcasted_iota(jnp.int32, (rows, rows), 0)
               < lax.broadcasted_iota(jnp.int32, (rows, rows), 1))
    pos = base_ref[...] + _dot(chosen.astype(BF16), jnp.where(earlier, 1.0, 0.0).astype(BF16))
    d_ref[...] = jnp.concatenate(
        [jnp.sum(jnp.where(hk, pos, 0.0), axis=0, keepdims=True) for hk in hot], axis=0).astype(jnp.int32)


def _dest(e_t, base):
    n = e_t.shape[1]
    rows = ROW_BLOCK
    return pl.pallas_call(
        _dest_kernel,
        grid=(n // rows,),
        in_specs=[pl.BlockSpec((TOP_K, rows), lambda i: (0, i)),
                  pl.BlockSpec((None, N_EXPERTS, 1), lambda i: (i, 0, 0))],
        out_specs=pl.BlockSpec((TOP_K, rows), lambda i: (0, i)),
        out_shape=jax.ShapeDtypeStruct((TOP_K, n), jnp.int32),
        compiler_params=_params("parallel"),
        name="moe_dest",
    )(e_t, base)


def _combine_kernel(*refs, final):
    if final:
        x_ref, g0, g1, g2, g3, gate_ref, fg_ref, op_ref, os_ref = refs
    else:
        x_ref, g0, g1, g2, g3, gate_ref, o_ref = refs
    acc = x_ref[...]
    for k, g in enumerate((g0, g1, g2, g3)):
        acc = acc + gate_ref[:, k:k + 1] * g[...]
    if final:
        def store(o_ref):
            o_ref[...] = _rms(acc, fg_ref[...])
        _for_group((op_ref, os_ref), store)
    else:
        o_ref[...] = acc


def _combine(x, picked, gates, final_g=None):
    n, d = x.shape
    rows = ROW_BLOCK
    picked = picked.reshape(TOP_K, n, d)
    final = final_g is not None
    in_specs = [pl.BlockSpec((rows, d), lambda i: (i, 0))]
    in_specs += [pl.BlockSpec((None, rows, d), functools.partial(lambda i, k: (k, i, 0), k=k))
                 for k in range(TOP_K)]
    in_specs.append(pl.BlockSpec((rows, TOP_K), lambda i: (i, 0)))
    args = [x, picked, picked, picked, picked, gates]
    out_specs = pl.BlockSpec((rows, d), lambda i: (i, 0))
    out_shape = jax.ShapeDtypeStruct((n, d), F32)
    if final:
        in_specs.append(pl.BlockSpec((1, d), lambda i: (0, 0)))
        args.append(final_g.reshape(1, d))
        out_specs = _token_sources((None, None))
        out_shape = [jax.ShapeDtypeStruct((N_PROMPT, d), F32), jax.ShapeDtypeStruct((N_SAMPLE, d), F32)]
    return pl.pallas_call(
        functools.partial(_combine_kernel, final=final),
        grid=(n // rows,),
        in_specs=in_specs,
        out_specs=out_specs,
        out_shape=out_shape,
        compiler_params=_params("arbitrary" if final else "parallel"),
        name="moe_combine",
    )(*args)


def _cmul_add(ar, ai, hr, hi, br, bi):
    return ar * hr - ai * hi + br, ar * hi + ai * hr + bi


def _s5_kernel(*refs, seg, chained):
    if chained:
        (x_ref, ng_ref, win_ref, wbr_ref, wbi_ref, wcr_ref, wci_ref, ar_ref, ai_ref, pr_ref, pi_ref, d_ref,
         wglu_ref, og_ref, o_ref, er_ref, ei_ref,
         up_ref, hr_ref, hi_ref, sr_ref, si_ref, op_ref, cr_ref, ci_ref) = refs
    else:
        (x_ref, ng_ref, win_ref, wbr_ref, wbi_ref, wcr_ref, wci_ref, ar_ref, ai_ref, pr_ref, pi_ref, d_ref,
         wglu_ref, og_ref, h0r_ref, h0i_ref, o_ref, er_ref, ei_ref,
         up_ref, hr_ref, hi_ref, sr_ref, si_ref, op_ref) = refs
    rows = seg * SUBLANES
    half = S5_COLS // 2

    u = _dot(_rms(x_ref[...], ng_ref[...]).astype(BF16), win_ref[...])
    n_lane_blk = S5_WIDTH // LANES
    for j in range(SUBLANES):
        for i0 in range(0, seg, SUBLANES):
            for c in range(n_lane_blk):
                up_ref[c, pl.ds(i0 * SUBLANES + j, SUBLANES, stride=SUBLANES), :] = (
                    u[j * seg + i0:j * seg + i0 + SUBLANES, c * LANES:(c + 1) * LANES])

    up = jnp.concatenate([up_ref[c] for c in range(n_lane_blk)], axis=1)
    ub = up.astype(BF16)
    for k in range(2):
        uk = ub[:, k * MXU_DIM:(k + 1) * MXU_DIM]
        hr_ref[:, k * half:(k + 1) * half] = _dot(uk, wbr_ref[k])
        hi_ref[:, k * half:(k + 1) * half] = _dot(uk, wbi_ref[k])

    for cb in range(S5_COLS // S5_COL_BLOCK):
        cols = slice(cb * S5_COL_BLOCK, (cb + 1) * S5_COL_BLOCK)
        ar = jnp.broadcast_to(ar_ref[:, cols], (SUBLANES, S5_COL_BLOCK))
        ai = jnp.broadcast_to(ai_ref[:, cols], (SUBLANES, S5_COL_BLOCK))

        def step(i, carry, cols=cols, ar=ar, ai=ai):
            r = pl.multiple_of(i * SUBLANES, SUBLANES)
            h_r, h_i = _cmul_add(ar, ai, carry[0], carry[1],
                                 hr_ref[pl.ds(r, SUBLANES), cols], hi_ref[pl.ds(r, SUBLANES), cols])
            hr_ref[pl.ds(r, SUBLANES), cols] = h_r
            hi_ref[pl.ds(r, SUBLANES), cols] = h_i
            return h_r, h_i

        zero = jnp.zeros((SUBLANES, S5_COL_BLOCK), F32)
        lax.fori_loop(0, seg, step, (zero, zero), unroll=2)

    end_r = hr_ref[rows - SUBLANES:rows, :]
    end_i = hi_ref[rows - SUBLANES:rows, :]
    as_r = pr_ref[rows - 1:rows, :]
    as_i = pi_ref[rows - 1:rows, :]
    if chained:
        @pl.when(pl.program_id(1) == 0)
        def _():
            cr_ref[...] = jnp.zeros_like(cr_ref)
            ci_ref[...] = jnp.zeros_like(ci_ref)

        cur_r, cur_i = cr_ref[...], ci_ref[...]
        for j in range(SUBLANES):
            sr_ref[j:j + 1, :] = cur_r
            si_ref[j:j + 1, :] = cur_i
            cur_r, cur_i = _cmul_add(as_r, as_i, cur_r, cur_i, end_r[j:j + 1], end_i[j:j + 1])
        cr_ref[...] = cur_r
        ci_ref[...] = cur_i
    else:
        sr_ref[...] = h0r_ref[...]
        si_ref[...] = h0i_ref[...]
    t_r, t_i = _cmul_add(as_r, as_i, sr_ref[...], si_ref[...], end_r, end_i)
    er_ref[...] = t_r
    ei_ref[...] = t_i

    slabs_per_window = MXU_DIM // (MXU_DIM // S5_STATE * S5_GROUP_CH)
    ys = []
    for w in range(S5_WIDTH // MXU_DIM):
        acc = d_ref[:, w * MXU_DIM:(w + 1) * MXU_DIM] * jnp.concatenate(
            [up_ref[c] for c in range(w * MXU_DIM // LANES, (w + 1) * MXU_DIM // LANES)], axis=1)
        for s in range(slabs_per_window):
            cols = slice((w * slabs_per_window + s) * MXU_DIM, (w * slabs_per_window + s + 1) * MXU_DIM)
            shape3 = (seg, SUBLANES, MXU_DIM)
            h_r, h_i = _cmul_add(pr_ref[:, cols].reshape(shape3), pi_ref[:, cols].reshape(shape3),
                                 sr_ref[:, cols][None], si_ref[:, cols][None],
                                 hr_ref[:, cols].reshape(shape3), hi_ref[:, cols].reshape(shape3))
            acc += _dot(h_r.reshape(rows, MXU_DIM).astype(BF16), wcr_ref[w * slabs_per_window + s])
            acc += _dot(h_i.reshape(rows, MXU_DIM).astype(BF16), wci_ref[w * slabs_per_window + s])
        ys.append(jax.nn.gelu(acc).astype(BF16))

    z = _dot(jnp.concatenate(ys, axis=1), wglu_ref[...])
    o = _rms(z[:, :S5_WIDTH] * jax.nn.sigmoid(z[:, S5_WIDTH:]), og_ref[...])
    for c in range(n_lane_blk):
        op_ref[c] = o[:, c * LANES:(c + 1) * LANES]

    for j in range(SUBLANES):
        for i0 in range(0, seg, 2 * SUBLANES):
            for c in range(n_lane_blk):
                o_ref[j * seg + i0:j * seg + i0 + 2 * SUBLANES, c * LANES:(c + 1) * LANES] = (
                    op_ref[c, pl.ds(i0 * SUBLANES + j, 2 * SUBLANES, stride=SUBLANES), :].astype(BF16))


def _s5_weights(lam_re, lam_im, b_re, b_im, c_re, c_im, d, log_step, seg):
    step = jnp.exp(log_step)[:, None]
    lr = jnp.minimum(lam_re, EIG_CLIP)
    li = lam_im
    mag = jnp.exp(lr * step)
    ar = mag * jnp.cos(li * step)
    ai = mag * jnp.sin(li * step)
    den = lr * lr + li * li
    fr = ((ar - 1.0) * lr + ai * li) / den
    fi = (ai * lr - (ar - 1.0) * li) / den
    bbar_r = fr[..., None] * b_re - fi[..., None] * b_im
    bbar_i = fr[..., None] * b_im + fi[..., None] * b_re
    t = jnp.arange(1, seg + 1, dtype=F32)[:, None, None]
    pm = jnp.exp(lr * step * t)
    p_r = jnp.repeat((pm * jnp.cos(li * step * t)).reshape(seg, S5_COLS), SUBLANES, axis=0)
    p_i = jnp.repeat((pm * jnp.sin(li * step * t)).reshape(seg, S5_COLS), SUBLANES, axis=0)

    gk = MXU_DIM // S5_GROUP_CH

    def in_map(bbar):
        bb = bbar.reshape(S5_GROUPS // gk, gk, S5_STATE, S5_GROUP_CH)
        w = jnp.einsum('kgnh,gG->kghGn', bb, jnp.eye(gk, dtype=F32))
        return w.reshape(S5_GROUPS // gk, MXU_DIM, gk * S5_STATE).astype(BF16)

    gs = MXU_DIM // S5_STATE
    n_slab = S5_COLS // MXU_DIM
    per_win = MXU_DIM // (gs * S5_GROUP_CH)

    def out_map(c):
        cc = c.reshape(n_slab // per_win, per_win, gs, S5_GROUP_CH, S5_STATE)
        w = jnp.einsum('wsghn,sS,gG->wsgnSGh', cc, jnp.eye(per_win, dtype=F32), jnp.eye(gs, dtype=F32))
        return w.reshape(n_slab, MXU_DIM, MXU_DIM).astype(BF16)

    return dict(wbr=in_map(bbar_r), wbi=in_map(bbar_i), wcr=out_map(c_re), wci=out_map(-c_im),
                ar=ar.reshape(1, S5_COLS), ai=ai.reshape(1, S5_COLS), pr=p_r, pi=p_i,
                d=d.reshape(1, S5_WIDTH))


def _s5_call(x, x_first, norm_g, w_in, sw, w_glu, og, seg, chained, h0=None, prev=None):
    rows = seg * SUBLANES
    const2 = lambda *_: (0, 0)
    const3 = lambda *_: (0, 0, 0)
    if chained:
        n_chunk = SEQ // rows
        grid = (BATCH, n_chunk)
        row_map = lambda b, c: (b * n_chunk + c, 0)
        x_map = row_map
        end_map = lambda b, c: (b, 0, 0)
        n_end = BATCH
        sem = ("parallel", "arbitrary")
    else:
        grid = (DEC_BATCH // SUBLANES,)
        row_map = lambda k: (N_PROMPT // rows + k, 0)
        x_map = lambda k: (x_first + k, 0)
        end_map = lambda k: (k, 0, 0)
        n_end = DEC_BATCH // SUBLANES
        sem = ("parallel",)
    half = S5_COLS // 2
    in_specs = [pl.BlockSpec((rows, D_MODEL), x_map),
                pl.BlockSpec((1, D_MODEL), const2),
                pl.BlockSpec((D_MODEL, S5_WIDTH), const2),
                pl.BlockSpec((2, MXU_DIM, half), const3),
                pl.BlockSpec((2, MXU_DIM, half), const3),
                pl.BlockSpec((S5_COLS // MXU_DIM, MXU_DIM, MXU_DIM), const3),
                pl.BlockSpec((S5_COLS // MXU_DIM, MXU_DIM, MXU_DIM), const3),
                pl.BlockSpec((1, S5_COLS), const2),
                pl.BlockSpec((1, S5_COLS), const2),
                pl.BlockSpec((rows, S5_COLS), const2),
                pl.BlockSpec((rows, S5_COLS), const2),
                pl.BlockSpec((1, S5_WIDTH), const2),
                pl.BlockSpec((S5_WIDTH, 2 * S5_WIDTH), const2),
                pl.BlockSpec((1, S5_WIDTH), const2)]
    args = [x, norm_g.reshape(1, D_MODEL), w_in[:, :S5_WIDTH], sw['wbr'], sw['wbi'], sw['wcr'], sw['wci'], sw['ar'], sw['ai'],
            sw['pr'][:rows], sw['pi'][:rows], sw['d'], w_glu, og.reshape(1, S5_WIDTH)]
    scratch = [pltpu.VMEM((S5_WIDTH // LANES, rows, LANES), F32),
               pltpu.VMEM((rows, S5_COLS), F32), pltpu.VMEM((rows, S5_COLS), F32),
               pltpu.VMEM((SUBLANES, S5_COLS), F32), pltpu.VMEM((SUBLANES, S5_COLS), F32),
               pltpu.VMEM((S5_WIDTH // LANES, rows, LANES), F32)]
    aliases = {}
    if chained:
        scratch += [pltpu.VMEM((1, S5_COLS), F32), pltpu.VMEM((1, S5_COLS), F32)]
    else:
        in_specs += [pl.BlockSpec((None, SUBLANES, S5_COLS), end_map),
                     pl.BlockSpec((None, SUBLANES, S5_COLS), end_map)]
        args += [h0[0], h0[1]]
    if prev is not None:
        in_specs.append(pl.BlockSpec(memory_space=pl.ANY))
        args.append(prev)
        aliases = {len(args) - 1: 0}
    kern = functools.partial(_s5_kernel, seg=seg, chained=chained)
    if prev is not None:
        kern = _drop_inputs(kern, len(args) - 1, 1)
    return pl.pallas_call(
        kern,
        grid=grid,
        in_specs=in_specs,
        out_specs=[pl.BlockSpec((rows, S5_WIDTH), row_map),
                   pl.BlockSpec((None, SUBLANES, S5_COLS), end_map),
                   pl.BlockSpec((None, SUBLANES, S5_COLS), end_map)],
        out_shape=[jax.ShapeDtypeStruct((N_TOK, S5_WIDTH), BF16),
                   jax.ShapeDtypeStruct((n_end, SUBLANES, S5_COLS), F32),
                   jax.ShapeDtypeStruct((n_end, SUBLANES, S5_COLS), F32)],
        scratch_shapes=scratch,
        input_output_aliases=aliases,
        compiler_params=_params(*sem),
        name="s5_chained" if chained else "s5_streams",
    )(*args)


def _drop_inputs(kern, start, n):
    def wrapped(*refs):
        return kern(*refs[:start], *refs[start + n:])
    return wrapped


def _gmlp_kernel(*refs, chunk, emit_vn):
    if emit_vn:
        x_ref, ng_ref, win_ref, lng_ref, lnb_ref, ws_ref, bst_ref, og_ref, o_ref, vn_ref, acc_ref = refs
    else:
        x_ref, ng_ref, win_ref, lng_ref, lnb_ref, ws_ref, bst_ref, og_ref, o_ref, acc_ref = refs
    rows = x_ref.shape[0]
    uv = _dot(_rms(x_ref[...], ng_ref[...]).astype(BF16), win_ref[...])
    u = jax.nn.gelu(uv[:, :GM_WIDTH])
    v = jax.nn.gelu(uv[:, GM_WIDTH:])
    tri = (lax.broadcasted_iota(jnp.int32, (chunk, chunk), 0)
           >= lax.broadcasted_iota(jnp.int32, (chunk, chunk), 1))
    for h in range(GM_HEADS):
        cols = slice(h * GM_HEAD_DIM, (h + 1) * GM_HEAD_DIM)
        vh = v[:, cols]
        mu = jnp.mean(vh, axis=-1, keepdims=True)
        cen = vh - mu
        var = jnp.mean(cen * cen, axis=-1, keepdims=True)
        vn = cen * lax.rsqrt(var + EPS) * lng_ref[:, cols] + lnb_ref[:, cols]
        if emit_vn:
            vn_ref[:, cols] = vn
        w = jnp.where(tri, ws_ref[h], 0.0).astype(BF16)
        vb = vn.astype(BF16)
        bias = bst_ref[:, h:h + 1]
        for c in range(rows // chunk):
            rs = slice(c * chunk, (c + 1) * chunk)
            acc_ref[rs, cols] = u[rs, cols] * (_dot(w, vb[rs]) + bias)
    o_ref[...] = _rms(acc_ref[...], og_ref[...]).astype(BF16)


def _gmlp_call(x, x_first, norm_g, w_in, ln_g, ln_b, w_s, b_s, og, rows, chunk, first_block, n_blocks, emit_vn,
               prev=None):
    const2 = lambda i: (0, 0)
    in_specs = [pl.BlockSpec((rows, D_MODEL), lambda i: (x_first + i, 0)),
                pl.BlockSpec((1, D_MODEL), const2),
                pl.BlockSpec((D_MODEL, 2 * GM_WIDTH), const2),
                pl.BlockSpec((1, GM_WIDTH), const2),
                pl.BlockSpec((1, GM_WIDTH), const2),
                pl.BlockSpec((GM_HEADS, chunk, chunk), lambda i: (0, 0, 0)),
                pl.BlockSpec((chunk, GM_HEADS), const2),
                pl.BlockSpec((1, GM_WIDTH), const2)]
    args = [x, norm_g.reshape(1, D_MODEL), w_in[:, S5_WIDTH:], ln_g.reshape(1, GM_WIDTH), ln_b.reshape(1, GM_WIDTH),
            w_s[:, :chunk, :chunk], b_s[:, :chunk].T, og.reshape(1, GM_WIDTH)]
    out_specs = [pl.BlockSpec((rows, GM_WIDTH), lambda i: (first_block + i, 0))]
    out_shape = [jax.ShapeDtypeStruct((N_TOK, GM_WIDTH), BF16)]
    if emit_vn:
        out_specs.append(pl.BlockSpec((rows, GM_WIDTH), lambda i: (i, 0)))
        out_shape.append(jax.ShapeDtypeStruct((n_blocks * rows, GM_WIDTH), F32))
    kern = functools.partial(_gmlp_kernel, chunk=chunk, emit_vn=emit_vn)
    aliases = {}
    if prev is not None:
        in_specs.append(pl.BlockSpec(memory_space=pl.ANY))
        args.append(prev)
        aliases = {len(args) - 1: 0}
        kern = _drop_inputs(kern, len(args) - 1, 1)
    return pl.pallas_call(
        kern,
        grid=(n_blocks,),
        in_specs=in_specs,
        out_specs=out_specs,
        out_shape=out_shape,
        scratch_shapes=[pltpu.VMEM((rows, GM_WIDTH), F32)],
        input_output_aliases=aliases,
        compiler_params=_params("parallel"),
        name="gmlp_vn" if emit_vn else "gmlp",
    )(*args)


def _post_mix_kernel(x_ref, a_ref, b_ref, wa_ref, wb_ref, xg_ref, wq_ref, k_ref, v_ref, wo_ref,
                     fg_ref, wt_ref, rb_ref, x2_ref, h_ref, e_ref, gate_ref, cnt_ref, att_ref, *, n_batch, kv_layer):
    rows = x_ref.shape[0]
    tb = rows // n_batch
    x = x_ref[...] + _dot(a_ref[...], wa_ref[...]) + _dot(b_ref[...], wb_ref[...])
    q = _dot(_rms(x, xg_ref[...]).astype(BF16), wq_ref[...]).astype(BF16)
    scale = XA_HEAD_DIM ** -0.5
    for b in range(n_batch):
        rs = slice(b * tb, (b + 1) * tb)
        for h in range(XA_HEADS):
            cols = slice(h * XA_HEAD_DIM, (h + 1) * XA_HEAD_DIM)
            if kv_layer is None:
                k_h, v_h = k_ref[b, :, cols], v_ref[b, :, cols]
            else:
                k_h, v_h = k_ref[b, :, h, :].astype(BF16), v_ref[b, :, h, :].astype(BF16)
            s = lax.dot_general(q[rs, cols], k_h, (((1,), (1,)), ((), ())),
                                preferred_element_type=F32) * scale
            e = jnp.exp(s - jnp.max(s, axis=-1, keepdims=True))
            p = e / jnp.sum(e, axis=-1, keepdims=True)
            att_ref[rs, cols] = _dot(p.astype(BF16), v_h).astype(BF16)
    x = x + _dot(att_ref[...], wo_ref[...])
    x2_ref[...] = x
    _route_rows(x, fg_ref, wt_ref, rb_ref, h_ref, e_ref, gate_ref, cnt_ref)


def _post_mix(x, x_first, a, b, w_out, xg, wq, k, v, wo, fg, rw, rb,
              rows, n_batch, first_block, n_blocks, blocks_per_kv, prev=None, kv_layer=None):
    const2 = lambda i: (0, 0)
    if kv_layer is None:
        kv_spec = pl.BlockSpec((n_batch, N_MEM, D_MODEL), lambda i: (i // blocks_per_kv, 0, 0))
    else:
        kv_spec = pl.BlockSpec((None, n_batch, N_MEM, XA_HEADS, XA_HEAD_DIM),
                               lambda i: (kv_layer, i // blocks_per_kv, 0, 0, 0))
    row_map = lambda i: (first_block + i, 0)
    d = D_MODEL
    in_specs = [pl.BlockSpec((rows, d), lambda i: (x_first + i, 0)),
                pl.BlockSpec((rows, S5_WIDTH), row_map),
                pl.BlockSpec((rows, GM_WIDTH), row_map),
                pl.BlockSpec((S5_WIDTH, d), const2),
                pl.BlockSpec((GM_WIDTH, d), const2),
                pl.BlockSpec((1, d), const2),
                pl.BlockSpec((d, d), const2),
                kv_spec, kv_spec,
                pl.BlockSpec((d, d), const2),
                pl.BlockSpec((1, d), const2),
                pl.BlockSpec((N_EXPERTS, d), const2),
                pl.BlockSpec((N_EXPERTS, 1), const2)]
    args = [x, a, b, w_out[:S5_WIDTH], w_out[S5_WIDTH:], xg.reshape(1, d), wq, k, v, wo,
            fg.reshape(1, d), rw.T.astype(BF16), rb.reshape(N_EXPERTS, 1)]
    kern = functools.partial(_post_mix_kernel, n_batch=n_batch, kv_layer=kv_layer)
    aliases = {}
    if prev is not None:
        n_in = len(args)
        in_specs += [pl.BlockSpec(memory_space=pl.ANY)] * len(prev)
        args += list(prev)
        aliases = {n_in + o: o for o in range(len(prev))}
        kern = _drop_inputs(kern, n_in, len(prev))
    return pl.pallas_call(
        kern,
        grid=(n_blocks,),
        in_specs=in_specs,
        out_specs=[pl.BlockSpec((rows, d), row_map),
                   pl.BlockSpec((rows // SUBLANES,) + PIECE_TILE, lambda i: (first_block + i, 0, 0, 0)),
                   pl.BlockSpec((TOP_K, rows), lambda i: (0, first_block + i)),
                   pl.BlockSpec((TOP_K, rows), lambda i: (0, first_block + i)),
                   pl.BlockSpec((None, N_EXPERTS, LANES), lambda i: (i, 0, 0))],
        out_shape=[jax.ShapeDtypeStruct((N_TOK, d), F32),
                   jax.ShapeDtypeStruct((N_TOK // SUBLANES,) + PIECE_TILE, F32),
                   jax.ShapeDtypeStruct((TOP_K, N_TOK), jnp.int32),
                   jax.ShapeDtypeStruct((TOP_K, N_TOK), F32),
                   jax.ShapeDtypeStruct((n_blocks, N_EXPERTS, LANES), F32)],
        scratch_shapes=[pltpu.VMEM((rows, d), BF16)],
        input_output_aliases=aliases,
        compiler_params=_params("parallel"),
        name="post_mix",
    )(*args)


def _moe_kernel(be_ref, nu_ref, first_ref, slot_ref, nxt_ref, half_ref,
                x_ref, wg_hbm, wu_hbm, wd_hbm, bg_ref, bu_ref, bd_ref, o_ref,
                wbuf_ref, wgb_ref, wub_ref, wdb_ref, sem, *, layer):
    i = pl.program_id(0)
    active = i < nu_ref[0]

    def copies(e, s):
        return [pltpu.make_async_copy(w.at[layer, e], wbuf_ref.at[s, j], sem.at[s, j])
                for j, w in enumerate((wg_hbm, wu_hbm, wd_hbm))]

    @pl.when(jnp.logical_and(active, first_ref[i] == 1))
    def _():
        s = slot_ref[i]

        @pl.when(i == 0)
        def _():
            for c in copies(be_ref[0], s):
                c.start()

        @pl.when(nxt_ref[i] >= 0)
        def _():
            for c in copies(nxt_ref[i], 1 - s):
                c.start()

        for c in copies(be_ref[i], s):
            c.wait()

        def cast(r, _):
            rs = pl.ds(pl.multiple_of(r * MOE_CAST_ROWS, MOE_CAST_ROWS), MOE_CAST_ROWS)
            wgb_ref[rs, :] = wbuf_ref[s, 0, rs, :].astype(BF16)
            wub_ref[rs, :] = wbuf_ref[s, 1, rs, :].astype(BF16)
            wdb_ref[rs, :] = wbuf_ref[s, 2, rs, :].astype(BF16)
            return 0
        lax.fori_loop(0, D_MODEL // MOE_CAST_ROWS, cast, 0)

    def experts(n_rows):
        x = jnp.concatenate([x_ref[:n_rows // SUBLANES, c].reshape(n_rows, LANES) for c in range(PIECES_PER_ROW)],
                            axis=1).astype(BF16)
        g = jnp.minimum(_dot(x, wgb_ref[...]) + bg_ref[...], SWIGLU_LIMIT)
        u = jnp.clip(_dot(x, wub_ref[...]) + bu_ref[...], -SWIGLU_LIMIT, SWIGLU_LIMIT)
        a = g * jax.nn.sigmoid(SWIGLU_ALPHA * g) * (u + 1.0)
        o_ref[:n_rows, :] = _dot(a.astype(BF16), wdb_ref[...]) + bd_ref[...]

    pl.when(jnp.logical_and(active, half_ref[i] == 0))(lambda: experts(MOE_ROWS))
    pl.when(jnp.logical_and(active, half_ref[i] == 1))(lambda: experts(MOE_ROWS // 2))


def _moe_call(layer, xs, sched, w_gate, b_gate, w_up, b_up, w_down, b_down):
    def row_map(i, be, nu, *_):
        return (jnp.minimum(i, nu[0] - 1), 0)

    def piece_map(i, be, nu, *_):
        return (jnp.minimum(i, nu[0] - 1), 0, 0, 0)

    def b_map(i, be, *_):
        return (layer, be[i], 0, 0)

    b_spec = pl.BlockSpec((None, None, 1, D_FF), b_map)
    w_spec = pl.BlockSpec(memory_space=pl.ANY)
    grid_spec = pltpu.PrefetchScalarGridSpec(
        num_scalar_prefetch=6,
        grid=(MOE_BLOCKS,),
        in_specs=[pl.BlockSpec((MOE_ROWS // SUBLANES,) + PIECE_TILE, piece_map),
                  w_spec, w_spec, w_spec, b_spec, b_spec, b_spec],
        out_specs=pl.BlockSpec((MOE_ROWS, D_MODEL), row_map),
        scratch_shapes=[pltpu.VMEM((2, 3, D_MODEL, D_FF), F32)]
        + [pltpu.VMEM((D_MODEL, D_FF), BF16)] * 3
        + [pltpu.SemaphoreType.DMA((2, 3))],
    )
    shape4 = (DEPTH, N_EXPERTS, 1, D_FF)
    return pl.pallas_call(
        functools.partial(_moe_kernel, layer=layer),
        grid_spec=grid_spec,
        out_shape=jax.ShapeDtypeStruct((MOE_BLOCKS * MOE_ROWS, D_MODEL), F32),
        compiler_params=pltpu.CompilerParams(dimension_semantics=("arbitrary",),
                                             vmem_limit_bytes=56 * 1024 * 1024),
        name="moe_experts",
    )(*sched, xs, w_gate, w_up, w_down, b_gate.reshape(shape4), b_up.reshape(shape4),
      b_down.reshape(shape4))


DISPATCH_TOKENS = 48
DMA_PIECES = 128
DISPATCH_DMAS = DISPATCH_TOKENS * PIECES_PER_ROW // DMA_PIECES
DISPATCH_CHUNKS = N_TOK // SC_WORKERS // DISPATCH_TOKENS
IDX_ROWS = 16
SC_LANES = 16
MOE_ROWS_TOTAL = MOE_BLOCKS * MOE_ROWS
TILE_PIECES = PIECES_PER_ROW * SUBLANES


def _piece_index_rows(p_v, idx_v, row0, tok0, n_dma):
    lane = lax.iota(jnp.int32, SC_LANES)
    in_tile = lane & (SUBLANES - 1)
    piece = (lane >> 3) * SUBLANES
    per_vec = SC_LANES // SUBLANES
    tiles_per_dma = DMA_PIECES // TILE_PIECES
    for m in range(n_dma):
        for tr in range(tiles_per_dma):
            first = plsc.load_gather(p_v, [in_tile + (tok0 + (m * tiles_per_dma + tr) * SUBLANES)])
            for v in range(PIECES_PER_ROW // per_vec):
                idx_v[row0 + m, pl.ds(tr * TILE_PIECES + v * SC_LANES, SC_LANES)] = (
                    first + (piece + v * per_vec * SUBLANES))


def _dispatch(h4, dest_t):
    first_piece = ((dest_t // SUBLANES) * TILE_PIECES + dest_t % SUBLANES).reshape(-1)
    chunk_pieces = DISPATCH_TOKENS * PIECES_PER_ROW
    mesh = plsc.VectorSubcoreMesh(core_axis_name="c", subcore_axis_name="s")

    @functools.partial(
        pl.kernel, mesh=mesh,
        out_type=jax.ShapeDtypeStruct((MOE_ROWS_TOTAL * PIECES_PER_ROW, LANES), F32),
        scratch_types=[pltpu.VMEM((chunk_pieces, LANES), F32), pltpu.VMEM((IDX_ROWS, DMA_PIECES), jnp.int32),
                       pltpu.VMEM((TOP_K * DISPATCH_TOKENS,), jnp.int32), pltpu.SemaphoreType.DMA],
        compiler_params=pltpu.CompilerParams(needs_layout_passes=False),
    )
    def scatter_rows(h_hbm, p_hbm, o_hbm, rows_v, idx_v, p_v, sem):
        wid = lax.axis_index("s") * SC_CORES + lax.axis_index("c")

        @pl.loop(0, DISPATCH_CHUNKS)
        def _(j):
            q = wid * DISPATCH_CHUNKS + j
            pltpu.sync_copy(h_hbm.at[pl.ds(pl.multiple_of(q * chunk_pieces, SUBLANES), chunk_pieces)], rows_v)
            for k in range(TOP_K):
                pltpu.sync_copy(
                    p_hbm.at[pl.ds(pl.multiple_of(k * N_TOK + q * DISPATCH_TOKENS, SUBLANES), DISPATCH_TOKENS)],
                    p_v.at[pl.ds(k * DISPATCH_TOKENS, DISPATCH_TOKENS)])
            for k in range(TOP_K):
                _piece_index_rows(p_v, idx_v, k * DISPATCH_DMAS, k * DISPATCH_TOKENS, DISPATCH_DMAS)
            copies = [pltpu.async_copy(rows_v.at[pl.ds(m * DMA_PIECES, DMA_PIECES)],
                                       o_hbm.at[idx_v.at[k * DISPATCH_DMAS + m]], sem)
                      for k in range(TOP_K) for m in range(DISPATCH_DMAS)]
            for cp in copies:
                cp.wait()

    xs = scatter_rows(h4.reshape(N_TOK * PIECES_PER_ROW, LANES), first_piece)
    return xs.reshape((MOE_ROWS_TOTAL // SUBLANES,) + PIECE_TILE)


def _route(e_t, cnt_prompt, cnt_sample):
    cnt_sample = jnp.sum(cnt_sample[:, :, 0].reshape(SAMPLE_BLOCKS, -1, N_EXPERTS), axis=1)
    blk_cnt = jnp.concatenate([cnt_prompt[:, :, 0], cnt_sample]).astype(jnp.int32)
    def before(n):
        i = jnp.arange(n, dtype=jnp.int32)
        return i[None, :] < i[:, None]

    counts = jnp.sum(blk_cnt, axis=0)
    padded = ((counts + MOE_ROWS - 1) // MOE_ROWS) * MOE_ROWS
    pstart = jnp.sum(jnp.where(before(N_EXPERTS), padded[None, :], 0), axis=1)
    pend = pstart + padded
    base = pstart[None, :] + jnp.sum(jnp.where(before(blk_cnt.shape[0])[:, :, None], blk_cnt[None], 0), axis=1)
    dest_t = _dest(e_t, base.astype(F32)[:, :, None])
    n_used = jnp.sum(padded) // MOE_ROWS
    ids = jnp.arange(MOE_BLOCKS, dtype=jnp.int32)
    blk = jnp.minimum(ids, n_used - 1) * MOE_ROWS
    blk_exp = jnp.minimum(jnp.sum((pend[None, :] <= blk[:, None]).astype(jnp.int32), axis=1), N_EXPERTS - 1)
    eid = jnp.arange(N_EXPERTS, dtype=jnp.int32)
    of_blk = blk_exp[:, None] == eid[None, :]

    def lookup(table):
        return jnp.sum(jnp.where(of_blk, table[None, :], 0), axis=1)

    first = jnp.logical_and(ids < n_used, jnp.logical_or(ids == 0, blk_exp != jnp.roll(blk_exp, 1)))
    first = first.astype(jnp.int32)
    slot = jnp.sum(jnp.where(before(MOE_BLOCKS), first[None, :], 0), axis=1) % 2
    later = jnp.logical_and(eid[None, :] > eid[:, None], (counts > 0)[None, :])
    nxt_e = jnp.min(jnp.where(later, eid[None, :], N_EXPERTS), axis=1)
    nxt = lookup(jnp.where(nxt_e < N_EXPERTS, nxt_e, -1))
    real = jnp.clip(lookup(pstart + counts) - blk, 0, MOE_ROWS)
    half = (real <= MOE_ROWS // 2).astype(jnp.int32)
    return dest_t, (blk_exp, n_used.reshape(1).astype(jnp.int32), first, slot.astype(jnp.int32),
                    nxt.astype(jnp.int32), half)


def kernel(x_prompt, x_sample, mem_prompt, state_s5_re, state_s5_im, cache_mem_k, cache_mem_v, norm_mix_g, w_in, s5_lambda_re, s5_lambda_im, s5_b_re, s5_b_im, s5_c_re, s5_c_im, s5_d, s5_log_step, s5_w_glu, gm_ln_g, gm_ln_b, gm_w_s, gm_b_s, out_norm_g, w_out, xa_norm_g, mem_norm_g, w_xq, w_xk, w_xv, w_xo, ffn_norm_g, router_w, router_b, e_w_gate, e_b_gate, e_w_up, e_b_up, e_w_down, e_b_down, final_norm_g):
    xs = (x_prompt.reshape(N_PROMPT, D_MODEL), x_sample.reshape(N_SAMPLE, D_MODEL))
    mem = mem_prompt.reshape(BATCH * N_MEM, D_MODEL)
    outs = {k: [] for k in ('s5r_p', 's5i_p', 's5r_s', 's5i_s', 'gmv_s')}
    mem_kv = None
    for l in range(DEPTH):
        unified = len(xs) == 1

        def sample_first(rows):
            return N_PROMPT // rows if unified else 0

        in_proj = (norm_mix_g[l], w_in[l].astype(BF16))
        sw = _s5_weights(s5_lambda_re[l], s5_lambda_im[l], s5_b_re[l], s5_b_im[l], s5_c_re[l], s5_c_im[l],
                         s5_d[l], s5_log_step[l], S5_SEG_PROMPT)
        w_glu = s5_w_glu[l].astype(BF16)
        og = out_norm_g[l]
        ms5, er_p, ei_p = _s5_call(xs[0], 0, *in_proj, sw, w_glu, og[:S5_WIDTH], S5_SEG_PROMPT, True)
        h0 = (state_s5_re[l].reshape(DEC_BATCH // SUBLANES, SUBLANES, S5_COLS),
              state_s5_im[l].reshape(DEC_BATCH // SUBLANES, SUBLANES, S5_COLS))
        ms5, er_s, ei_s = _s5_call(xs[-1], sample_first(DEC_SEQ * SUBLANES), *in_proj, sw, w_glu, og[:S5_WIDTH],
                                   DEC_SEQ, False, h0=h0, prev=ms5)
        outs['s5r_p'].append(er_p[:, SUBLANES - 1].reshape(BATCH, S5_GROUPS, S5_STATE))
        outs['s5i_p'].append(ei_p[:, SUBLANES - 1].reshape(BATCH, S5_GROUPS, S5_STATE))
        outs['s5r_s'].append(er_s.reshape(DEC_BATCH, S5_GROUPS, S5_STATE))
        outs['s5i_s'].append(ei_s.reshape(DEC_BATCH, S5_GROUPS, S5_STATE))

        (mgm,) = _gmlp_call(xs[0], 0, *in_proj, gm_ln_g[l], gm_ln_b[l], gm_w_s[l], gm_b_s[l], og[S5_WIDTH:],
                            GM_ROWS_PROMPT, GM_CHUNK, 0, N_PROMPT // GM_ROWS_PROMPT, False)
        mgm, vn = _gmlp_call(xs[-1], sample_first(N_SAMPLE), *in_proj, gm_ln_g[l], gm_ln_b[l], gm_w_s[l],
                             gm_b_s[l], og[S5_WIDTH:], N_SAMPLE, DEC_SEQ, N_PROMPT // N_SAMPLE, 1, True, prev=mgm)
        outs['gmv_s'].append(vn.reshape(DEC_BATCH, DEC_SEQ, GM_HEADS, GM_HEAD_DIM))
        *mem_kv, mk, mv = _mem_kv(mem, mem_norm_g[l], w_xk[l].astype(BF16), w_xv[l].astype(BF16), l, prev=mem_kv)
        shared = (w_out[l].astype(BF16), xa_norm_g[l], w_xq[l].astype(BF16))
        tail = (w_xo[l].astype(BF16), ffn_norm_g[l], router_w[l], router_b[l])
        rows_s = XA_SAMPLE_BATCHES * DEC_SEQ
        post_p = _post_mix(xs[0], 0, ms5, mgm, *shared, mk, mv, *tail,
                           XA_ROWS_PROMPT, 1, 0, N_PROMPT // XA_ROWS_PROMPT, SEQ // XA_ROWS_PROMPT)
        post_s = _post_mix(xs[-1], sample_first(rows_s), ms5, mgm, *shared,
                           cache_mem_k, cache_mem_v, *tail,
                           rows_s, XA_SAMPLE_BATCHES, N_PROMPT // rows_s, N_SAMPLE // rows_s, 1,
                           prev=post_p[:4], kv_layer=l)
        x, h, e_t, gates_t = post_s[:4]

        dest_t, sched = _route(e_t, post_p[4], post_s[4])
        ys = _moe_call(l, _dispatch(h, dest_t), sched, e_w_gate, e_b_gate, e_w_up, e_b_up, e_w_down, e_b_down)
        x = _combine(x, ys[dest_t.reshape(-1)], gates_t.T, final_norm_g if l == DEPTH - 1 else None)
        xs = (x,)

    y_prompt = x[0].reshape(BATCH, SEQ, D_MODEL)
    y_sample = x[1].reshape(DEC_BATCH, DEC_SEQ, D_MODEL)
    st = jnp.stack
    return (y_prompt, y_sample, st(outs['s5r_p']), st(outs['s5i_p']), mem_kv[0], mem_kv[1],
            st(outs['s5r_s']), st(outs['s5i_s']), st(outs['gmv_s']))
```

```python
import functools

import jax
import jax.numpy as jnp
from jax import lax
from jax.experimental import pallas as pl
from jax.experimental.pallas import tpu as pltpu
from jax.experimental.pallas import tpu_sc as plsc

F32 = jnp.float32
BF16 = jnp.bfloat16

D_MODEL = 1024
BATCH = 4
SEQ = 4096
DEPTH = 2
DEC_BATCH = 16
DEC_SEQ = 32
S5_GROUPS = 32
S5_GROUP_CH = 16
S5_STATE = 64
S5_WIDTH = S5_GROUPS * S5_GROUP_CH
S5_COLS = S5_GROUPS * S5_STATE
GM_HEADS = 4
GM_HEAD_DIM = 128
GM_WIDTH = GM_HEADS * GM_HEAD_DIM
GM_CHUNK = 128
IN_PROJ = S5_WIDTH + 2 * GM_WIDTH
N_MEM = 256
XA_HEADS = 4
XA_HEAD_DIM = D_MODEL // XA_HEADS
N_EXPERTS = 32
TOP_K = 4
D_FF = D_MODEL
SWIGLU_LIMIT = 7.0
SWIGLU_ALPHA = 1.702
EPS = 1e-6
EIG_CLIP = -1e-4

N_PROMPT = BATCH * SEQ
N_SAMPLE = DEC_BATCH * DEC_SEQ
N_TOK = N_PROMPT + N_SAMPLE

SUBLANES = 8
LANES = 128
MXU_DIM = 256
VMEM_LIMIT = 48 * 1024 * 1024
MOE_VMEM_LIMIT = 60 * 1024 * 1024
SC_CORES = 2
SC_SUBCORES = 16
SC_WORKERS = SC_CORES * SC_SUBCORES

PIECE_TILE = (D_MODEL // LANES, SUBLANES, LANES)
PIECES_PER_ROW = D_MODEL // LANES

ROW_BLOCK = 512
S5_SEG_PROMPT = 64
S5_COL_BLOCK = 512
GM_ROWS_PROMPT = 256
XA_ROWS_PROMPT = 512
XA_SAMPLE_BATCHES = 4
MOE_ROWS = 1024
MOE_PART = 256
MOE_BLOCKS = -(-N_TOK * TOP_K // MOE_ROWS) + N_EXPERTS
MOE_CAST_ROWS = 128


def _params(*sem):
    return pltpu.CompilerParams(dimension_semantics=sem, vmem_limit_bytes=VMEM_LIMIT)


def _rms(x, g):
    return x * lax.rsqrt(jnp.mean(x * x, axis=-1, keepdims=True) + EPS) * g


def _dot(a, b):
    return jnp.dot(a, b, preferred_element_type=F32)


PROMPT_BLOCKS = N_PROMPT // ROW_BLOCK
SAMPLE_BLOCKS = N_SAMPLE // ROW_BLOCK


def _token_sources(xs):
    if len(xs) == 1:
        return [pl.BlockSpec((ROW_BLOCK, D_MODEL), lambda i: (i, 0))]
    return [pl.BlockSpec((ROW_BLOCK, D_MODEL), lambda i: (jnp.minimum(i, PROMPT_BLOCKS - 1), 0)),
            pl.BlockSpec((ROW_BLOCK, D_MODEL), lambda i: (jnp.maximum(i - PROMPT_BLOCKS, 0), 0))]


def _for_group(refs, body):
    if len(refs) == 1:
        body(refs[0])
        return
    i = pl.program_id(0)
    pl.when(i < PROMPT_BLOCKS)(lambda: body(refs[0]))
    pl.when(i >= PROMPT_BLOCKS)(lambda: body(refs[1]))


def _mem_kv_kernel(m_ref, g_ref, wk_ref, wv_ref, k5_ref, v5_ref, kb_ref, vb_ref):
    m = _rms(m_ref[...], g_ref[...]).astype(BF16)
    for w_ref, o5_ref, ob_ref in ((wk_ref, k5_ref, kb_ref), (wv_ref, v5_ref, vb_ref)):
        kv = _dot(m, w_ref[...])
        ob_ref[...] = kv.astype(BF16)
        for h in range(XA_HEADS):
            o5_ref[:, h, :] = kv[:, h * XA_HEAD_DIM:(h + 1) * XA_HEAD_DIM]


def _mem_kv(mem, g, wk, wv, layer, prev=None):
    d = D_MODEL
    const2 = lambda b: (0, 0)
    in_specs = [pl.BlockSpec((N_MEM, d), lambda b: (b, 0)),
                pl.BlockSpec((1, d), const2),
                pl.BlockSpec((d, d), const2),
                pl.BlockSpec((d, d), const2)]
    args = [mem, g.reshape(1, d), wk, wv]
    kern = _mem_kv_kernel
    aliases = {}
    if prev is not None:
        in_specs += [pl.BlockSpec(memory_space=pl.ANY)] * 2
        args += list(prev)
        aliases = {4: 0, 5: 1}
        kern = _drop_inputs(kern, 4, 2)
    out5 = pl.BlockSpec((None, None, N_MEM, XA_HEADS, XA_HEAD_DIM), lambda b: (layer, b, 0, 0, 0))
    outb = pl.BlockSpec((None, N_MEM, d), lambda b: (b, 0, 0))
    shape5 = jax.ShapeDtypeStruct((DEPTH, BATCH, N_MEM, XA_HEADS, XA_HEAD_DIM), F32)
    shapeb = jax.ShapeDtypeStruct((BATCH, N_MEM, d), BF16)
    return pl.pallas_call(
        kern,
        grid=(BATCH,),
        in_specs=in_specs,
        out_specs=[out5, out5, outb, outb],
        out_shape=[shape5, shape5, shapeb, shapeb],
        input_output_aliases=aliases,
        compiler_params=_params("parallel"),
        name="mem_kv",
    )(*args)


def _route_rows(x, g_ref, wt_ref, b_ref, h_ref, e_ref, gate_ref, cnt_ref):
    h = _rms(x, g_ref[...])
    for c in range(PIECES_PER_ROW):
        h_ref[:, c] = h[:, c * LANES:(c + 1) * LANES].reshape(h.shape[0] // SUBLANES, SUBLANES, LANES)
    logits = lax.dot_general(wt_ref[...], h.astype(BF16), (((1,), (1,)), ((), ())),
                             preferred_element_type=F32) + b_ref[...]
    sub = lax.broadcasted_iota(jnp.int32, logits.shape, 0)
    work = logits
    chosen = jnp.zeros(logits.shape, F32)
    vals, idxs = [], []
    for _ in range(TOP_K):
        m = jnp.max(work, axis=0, keepdims=True)
        idx = jnp.min(jnp.where(work == m, sub, N_EXPERTS), axis=0, keepdims=True)
        sel = sub == idx
        vals.append(m)
        idxs.append(idx)
        work = jnp.where(sel, -jnp.inf, work)
        chosen = jnp.where(sel, 1.0, chosen)
    ex = [jnp.exp(v - vals[0]) for v in vals]
    den = ex[0] + ex[1] + ex[2] + ex[3]
    gate_ref[...] = jnp.concatenate([e / den for e in ex], axis=0)
    e_ref[...] = jnp.concatenate(idxs, axis=0)
    cnt_ref[...] = jnp.broadcast_to(jnp.sum(chosen, axis=1, keepdims=True), (N_EXPERTS, LANES))


def _dest_kernel(e_ref, base_ref, d_ref):
    e = e_ref[...]
    rows = e.shape[1]
    sub = lax.broadcasted_iota(jnp.int32, (N_EXPERTS, rows), 0)
    hot = [sub == e[k:k + 1, :] for k in range(TOP_K)]
    chosen = sum(jnp.where(hk, 1.0, 0.0) for hk in hot)
    earlier = (lax.broadcasted_iota(jnp.int32, (rows, rows), 0)
               < lax.broadcasted_iota(jnp.int32, (rows, rows), 1))
    pos = base_ref[...] + _dot(chosen.astype(BF16), jnp.where(earlier, 1.0, 0.0).astype(BF16))
    d_ref[...] = jnp.concatenate(
        [jnp.sum(jnp.where(hk, pos, 0.0), axis=0, keepdims=True) for hk in hot], axis=0).astype(jnp.int32)


def _dest(e_t, base):
    n = e_t.shape[1]
    rows = ROW_BLOCK
    return pl.pallas_call(
        _dest_kernel,
        grid=(n // rows,),
        in_specs=[pl.BlockSpec((TOP_K, rows), lambda i: (0, i)),
                  pl.BlockSpec((None, N_EXPERTS, 1), lambda i: (i, 0, 0))],
        out_specs=pl.BlockSpec((TOP_K, rows), lambda i: (0, i)),
        out_shape=jax.ShapeDtypeStruct((TOP_K, n), jnp.int32),
        compiler_params=_params("parallel"),
        name="moe_dest",
    )(e_t, base)


def _combine_kernel(*refs, final):
    if final:
        x_ref, g0, g1, g2, g3, gate_ref, fg_ref, op_ref, os_ref = refs
    else:
        x_ref, g0, g1, g2, g3, gate_ref, o_ref = refs
    acc = x_ref[...]
    for k, g in enumerate((g0, g1, g2, g3)):
        acc = acc + gate_ref[:, k:k + 1] * g[...]
    if final:
        def store(o_ref):
            o_ref[...] = _rms(acc, fg_ref[...])
        _for_group((op_ref, os_ref), store)
    else:
        o_ref[...] = acc


def _combine(x, picked, gates, final_g=None):
    n, d = x.shape
    rows = ROW_BLOCK
    picked = picked.reshape(TOP_K, n, d)
    final = final_g is not None
    in_specs = [pl.BlockSpec((rows, d), lambda i: (i, 0))]
    in_specs += [pl.BlockSpec((None, rows, d), functools.partial(lambda i, k: (k, i, 0), k=k))
                 for k in range(TOP_K)]
    in_specs.append(pl.BlockSpec((rows, TOP_K), lambda i: (i, 0)))
    args = [x, picked, picked, picked, picked, gates]
    out_specs = pl.BlockSpec((rows, d), lambda i: (i, 0))
    out_shape = jax.ShapeDtypeStruct((n, d), F32)
    if final:
        in_specs.append(pl.BlockSpec((1, d), lambda i: (0, 0)))
        args.append(final_g.reshape(1, d))
        out_specs = _token_sources((None, None))
        out_shape = [jax.ShapeDtypeStruct((N_PROMPT, d), F32), jax.ShapeDtypeStruct((N_SAMPLE, d), F32)]
    return pl.pallas_call(
        functools.partial(_combine_kernel, final=final),
        grid=(n // rows,),
        in_specs=in_specs,
        out_specs=out_specs,
        out_shape=out_shape,
        compiler_params=_params("arbitrary" if final else "parallel"),
        name="moe_combine",
    )(*args)


def _cmul_add(ar, ai, hr, hi, br, bi):
    return ar * hr - ai * hi + br, ar * hi + ai * hr + bi


def _s5_kernel(*refs, seg, chained):
    if chained:
        (x_ref, ng_ref, win_ref, wbr_ref, wbi_ref, wcr_ref, wci_ref, ar_ref, ai_ref, pr_ref, pi_ref, d_ref,
         wglu_ref, og_ref, o_ref, er_ref, ei_ref,
         up_ref, hr_ref, hi_ref, sr_ref, si_ref, op_ref, cr_ref, ci_ref) = refs
    else:
        (x_ref, ng_ref, win_ref, wbr_ref, wbi_ref, wcr_ref, wci_ref, ar_ref, ai_ref, pr_ref, pi_ref, d_ref,
         wglu_ref, og_ref, h0r_ref, h0i_ref, o_ref, er_ref, ei_ref,
         up_ref, hr_ref, hi_ref, sr_ref, si_ref, op_ref) = refs
    rows = seg * SUBLANES
    half = S5_COLS // 2

    u = _dot(_rms(x_ref[...], ng_ref[...]).astype(BF16), win_ref[...])
    n_lane_blk = S5_WIDTH // LANES
    for j in range(SUBLANES):
        for i0 in range(0, seg, SUBLANES):
            for c in range(n_lane_blk):
                up_ref[c, pl.ds(i0 * SUBLANES + j, SUBLANES, stride=SUBLANES), :] = (
                    u[j * seg + i0:j * seg + i0 + SUBLANES, c * LANES:(c + 1) * LANES])

    up = jnp.concatenate([up_ref[c] for c in range(n_lane_blk)], axis=1)
    ub = up.astype(BF16)
    for k in range(2):
        uk = ub[:, k * MXU_DIM:(k + 1) * MXU_DIM]
        hr_ref[:, k * half:(k + 1) * half] = _dot(uk, wbr_ref[k])
        hi_ref[:, k * half:(k + 1) * half] = _dot(uk, wbi_ref[k])

    for cb in range(S5_COLS // S5_COL_BLOCK):
        cols = slice(cb * S5_COL_BLOCK, (cb + 1) * S5_COL_BLOCK)
        ar = jnp.broadcast_to(ar_ref[:, cols], (SUBLANES, S5_COL_BLOCK))
        ai = jnp.broadcast_to(ai_ref[:, cols], (SUBLANES, S5_COL_BLOCK))

        def step(i, carry, cols=cols, ar=ar, ai=ai):
            r = pl.multiple_of(i * SUBLANES, SUBLANES)
            h_r, h_i = _cmul_add(ar, ai, carry[0], carry[1],
                                 hr_ref[pl.ds(r, SUBLANES), cols], hi_ref[pl.ds(r, SUBLANES), cols])
            hr_ref[pl.ds(r, SUBLANES), cols] = h_r
            hi_ref[pl.ds(r, SUBLANES), cols] = h_i
            return h_r, h_i

        zero = jnp.zeros((SUBLANES, S5_COL_BLOCK), F32)
        lax.fori_loop(0, seg, step, (zero, zero), unroll=2)

    end_r = hr_ref[rows - SUBLANES:rows, :]
    end_i = hi_ref[rows - SUBLANES:rows, :]
    as_r = pr_ref[rows - 1:rows, :]
    as_i = pi_ref[rows - 1:rows, :]
    if chained:
        @pl.when(pl.program_id(1) == 0)
        def _():
            cr_ref[...] = jnp.zeros_like(cr_ref)
            ci_ref[...] = jnp.zeros_like(ci_ref)

        cur_r, cur_i = cr_ref[...], ci_ref[...]
        for j in range(SUBLANES):
            sr_ref[j:j + 1, :] = cur_r
            si_ref[j:j + 1, :] = cur_i
            cur_r, cur_i = _cmul_add(as_r, as_i, cur_r, cur_i, end_r[j:j + 1], end_i[j:j + 1])
        cr_ref[...] = cur_r
        ci_ref[...] = cur_i
    else:
        sr_ref[...] = h0r_ref[...]
        si_ref[...] = h0i_ref[...]
    t_r, t_i = _cmul_add(as_r, as_i, sr_ref[...], si_ref[...], end_r, end_i)
    er_ref[...] = t_r
    ei_ref[...] = t_i

    slabs_per_window = MXU_DIM // (MXU_DIM // S5_STATE * S5_GROUP_CH)
    ys = []
    for w in range(S5_WIDTH // MXU_DIM):
        acc = d_ref[:, w * MXU_DIM:(w + 1) * MXU_DIM] * jnp.concatenate(
            [up_ref[c] for c in range(w * MXU_DIM // LANES, (w + 1) * MXU_DIM // LANES)], axis=1)
        for s in range(slabs_per_window):
            cols = slice((w * slabs_per_window + s) * MXU_DIM, (w * slabs_per_window + s + 1) * MXU_DIM)
            shape3 = (seg, SUBLANES, MXU_DIM)
            h_r, h_i = _cmul_add(pr_ref[:, cols].reshape(shape3), pi_ref[:, cols].reshape(shape3),
                                 sr_ref[:, cols][None], si_ref[:, cols][None],
                                 hr_ref[:, cols].reshape(shape3), hi_ref[:, cols].reshape(shape3))
            acc += _dot(h_r.reshape(rows, MXU_DIM).astype(BF16), wcr_ref[w * slabs_per_window + s])
            acc += _dot(h_i.reshape(rows, MXU_DIM).astype(BF16), wci_ref[w * slabs_per_window + s])
        ys.append(jax.nn.gelu(acc).astype(BF16))

    z = _dot(jnp.concatenate(ys, axis=1), wglu_ref[...])
    o = _rms(z[:, :S5_WIDTH] * jax.nn.sigmoid(z[:, S5_WIDTH:]), og_ref[...])
    for c in range(n_lane_blk):
        op_ref[c] = o[:, c * LANES:(c + 1) * LANES]

    for j in range(SUBLANES):
        for i0 in range(0, seg, 2 * SUBLANES):
            for c in range(n_lane_blk):
                o_ref[j * seg + i0:j * seg + i0 + 2 * SUBLANES, c * LANES:(c + 1) * LANES] = (
                    op_ref[c, pl.ds(i0 * SUBLANES + j, 2 * SUBLANES, stride=SUBLANES), :].astype(BF16))


def _s5_weights(lam_re, lam_im, b_re, b_im, c_re, c_im, d, log_step, seg):
    step = jnp.exp(log_step)[:, None]
    lr = jnp.minimum(lam_re, EIG_CLIP)
    li = lam_im
    mag = jnp.exp(lr * step)
    ar = mag * jnp.cos(li * step)
    ai = mag * jnp.sin(li * step)
    den = lr * lr + li * li
    fr = ((ar - 1.0) * lr + ai * li) / den
    fi = (ai * lr - (ar - 1.0) * li) / den
    bbar_r = fr[..., None] * b_re - fi[..., None] * b_im
    bbar_i = fr[..., None] * b_im + fi[..., None] * b_re
    t = jnp.arange(1, seg + 1, dtype=F32)[:, None, None]
    pm = jnp.exp(lr * step * t)
    p_r = jnp.repeat((pm * jnp.cos(li * step * t)).reshape(seg, S5_COLS), SUBLANES, axis=0)
    p_i = jnp.repeat((pm * jnp.sin(li * step * t)).reshape(seg, S5_COLS), SUBLANES, axis=0)

    gk = MXU_DIM // S5_GROUP_CH

    def in_map(bbar):
        bb = bbar.reshape(S5_GROUPS // gk, gk, S5_STATE, S5_GROUP_CH)
        w = jnp.einsum('kgnh,gG->kghGn', bb, jnp.eye(gk, dtype=F32))
        return w.reshape(S5_GROUPS // gk, MXU_DIM, gk * S5_STATE).astype(BF16)

    gs = MXU_DIM // S5_STATE
    n_slab = S5_COLS // MXU_DIM
    per_win = MXU_DIM // (gs * S5_GROUP_CH)

    def out_map(c):
        cc = c.reshape(n_slab // per_win, per_win, gs, S5_GROUP_CH, S5_STATE)
        w = jnp.einsum('wsghn,sS,gG->wsgnSGh', cc, jnp.eye(per_win, dtype=F32), jnp.eye(gs, dtype=F32))
        return w.reshape(n_slab, MXU_DIM, MXU_DIM).astype(BF16)

    return dict(wbr=in_map(bbar_r), wbi=in_map(bbar_i), wcr=out_map(c_re), wci=out_map(-c_im),
                ar=ar.reshape(1, S5_COLS), ai=ai.reshape(1, S5_COLS), pr=p_r, pi=p_i,
                d=d.reshape(1, S5_WIDTH))


def _s5_call(x, x_first, norm_g, w_in, sw, w_glu, og, seg, chained, h0=None, prev=None):
    rows = seg * SUBLANES
    const2 = lambda *_: (0, 0)
    const3 = lambda *_: (0, 0, 0)
    if chained:
        n_chunk = SEQ // rows
        grid = (BATCH, n_chunk)
        row_map = lambda b, c: (b * n_chunk + c, 0)
        x_map = row_map
        end_map = lambda b, c: (b, 0, 0)
        n_end = BATCH
        sem = ("parallel", "arbitrary")
    else:
        grid = (DEC_BATCH // SUBLANES,)
        row_map = lambda k: (N_PROMPT // rows + k, 0)
        x_map = lambda k: (x_first + k, 0)
        end_map = lambda k: (k, 0, 0)
        n_end = DEC_BATCH // SUBLANES
        sem = ("parallel",)
    half = S5_COLS // 2
    in_specs = [pl.BlockSpec((rows, D_MODEL), x_map),
                pl.BlockSpec((1, D_MODEL), const2),
                pl.BlockSpec((D_MODEL, S5_WIDTH), const2),
                pl.BlockSpec((2, MXU_DIM, half), const3),
                pl.BlockSpec((2, MXU_DIM, half), const3),
                pl.BlockSpec((S5_COLS // MXU_DIM, MXU_DIM, MXU_DIM), const3),
                pl.BlockSpec((S5_COLS // MXU_DIM, MXU_DIM, MXU_DIM), const3),
                pl.BlockSpec((1, S5_COLS), const2),
                pl.BlockSpec((1, S5_COLS), const2),
                pl.BlockSpec((rows, S5_COLS), const2),
                pl.BlockSpec((rows, S5_COLS), const2),
                pl.BlockSpec((1, S5_WIDTH), const2),
                pl.BlockSpec((S5_WIDTH, 2 * S5_WIDTH), const2),
                pl.BlockSpec((1, S5_WIDTH), const2)]
    args = [x, norm_g.reshape(1, D_MODEL), w_in[:, :S5_WIDTH], sw['wbr'], sw['wbi'], sw['wcr'], sw['wci'], sw['ar'], sw['ai'],
            sw['pr'][:rows], sw['pi'][:rows], sw['d'], w_glu, og.reshape(1, S5_WIDTH)]
    scratch = [pltpu.VMEM((S5_WIDTH // LANES, rows, LANES), F32),
               pltpu.VMEM((rows, S5_COLS), F32), pltpu.VMEM((rows, S5_COLS), F32),
               pltpu.VMEM((SUBLANES, S5_COLS), F32), pltpu.VMEM((SUBLANES, S5_COLS), F32),
               pltpu.VMEM((S5_WIDTH // LANES, rows, LANES), F32)]
    aliases = {}
    if chained:
        scratch += [pltpu.VMEM((1, S5_COLS), F32), pltpu.VMEM((1, S5_COLS), F32)]
    else:
        in_specs += [pl.BlockSpec((None, SUBLANES, S5_COLS), end_map),
                     pl.BlockSpec((None, SUBLANES, S5_COLS), end_map)]
        args += [h0[0], h0[1]]
    if prev is not None:
        in_specs.append(pl.BlockSpec(memory_space=pl.ANY))
        args.append(prev)
        aliases = {len(args) - 1: 0}
    kern = functools.partial(_s5_kernel, seg=seg, chained=chained)
    if prev is not None:
        kern = _drop_inputs(kern, len(args) - 1, 1)
    return pl.pallas_call(
        kern,
        grid=grid,
        in_specs=in_specs,
        out_specs=[pl.BlockSpec((rows, S5_WIDTH), row_map),
                   pl.BlockSpec((None, SUBLANES, S5_COLS), end_map),
                   pl.BlockSpec((None, SUBLANES, S5_COLS), end_map)],
        out_shape=[jax.ShapeDtypeStruct((N_TOK, S5_WIDTH), BF16),
                   jax.ShapeDtypeStruct((n_end, SUBLANES, S5_COLS), F32),
                   jax.ShapeDtypeStruct((n_end, SUBLANES, S5_COLS), F32)],
        scratch_shapes=scratch,
        input_output_aliases=aliases,
        compiler_params=_params(*sem),
        name="s5_chained" if chained else "s5_streams",
    )(*args)


def _drop_inputs(kern, start, n):
    def wrapped(*refs):
        return kern(*refs[:start], *refs[start + n:])
    return wrapped


def _gmlp_kernel(*refs, chunk, emit_vn):
    if emit_vn:
        x_ref, ng_ref, win_ref, lng_ref, lnb_ref, ws_ref, bst_ref, og_ref, o_ref, vn_ref, acc_ref = refs
    else:
        x_ref, ng_ref, win_ref, lng_ref, lnb_ref, ws_ref, bst_ref, og_ref, o_ref, acc_ref = refs
    rows = x_ref.shape[0]
    uv = _dot(_rms(x_ref[...], ng_ref[...]).astype(BF16), win_ref[...])
    u = jax.nn.gelu(uv[:, :GM_WIDTH])
    v = jax.nn.gelu(uv[:, GM_WIDTH:])
    tri = (lax.broadcasted_iota(jnp.int32, (chunk, chunk), 0)
           >= lax.broadcasted_iota(jnp.int32, (chunk, chunk), 1))
    for h in range(GM_HEADS):
        cols = slice(h * GM_HEAD_DIM, (h + 1) * GM_HEAD_DIM)
        vh = v[:, cols]
        mu = jnp.mean(vh, axis=-1, keepdims=True)
        cen = vh - mu
        var = jnp.mean(cen * cen, axis=-1, keepdims=True)
        vn = cen * lax.rsqrt(var + EPS) * lng_ref[:, cols] + lnb_ref[:, cols]
        if emit_vn:
            vn_ref[:, cols] = vn
        w = jnp.where(tri, ws_ref[h], 0.0).astype(BF16)
        vb = vn.astype(BF16)
        bias = bst_ref[:, h:h + 1]
        for c in range(rows // chunk):
            rs = slice(c * chunk, (c + 1) * chunk)
            acc_ref[rs, cols] = u[rs, cols] * (_dot(w, vb[rs]) + bias)
    o_ref[...] = _rms(acc_ref[...], og_ref[...]).astype(BF16)


def _gmlp_call(x, x_first, norm_g, w_in, ln_g, ln_b, w_s, b_s, og, rows, chunk, first_block, n_blocks, emit_vn,
               prev=None):
    const2 = lambda i: (0, 0)
    in_specs = [pl.BlockSpec((rows, D_MODEL), lambda i: (x_first + i, 0)),
                pl.BlockSpec((1, D_MODEL), const2),
                pl.BlockSpec((D_MODEL, 2 * GM_WIDTH), const2),
                pl.BlockSpec((1, GM_WIDTH), const2),
                pl.BlockSpec((1, GM_WIDTH), const2),
                pl.BlockSpec((GM_HEADS, chunk, chunk), lambda i: (0, 0, 0)),
                pl.BlockSpec((chunk, GM_HEADS), const2),
                pl.BlockSpec((1, GM_WIDTH), const2)]
    args = [x, norm_g.reshape(1, D_MODEL), w_in[:, S5_WIDTH:], ln_g.reshape(1, GM_WIDTH), ln_b.reshape(1, GM_WIDTH),
            w_s[:, :chunk, :chunk], b_s[:, :chunk].T, og.reshape(1, GM_WIDTH)]
    out_specs = [pl.BlockSpec((rows, GM_WIDTH), lambda i: (first_block + i, 0))]
    out_shape = [jax.ShapeDtypeStruct((N_TOK, GM_WIDTH), BF16)]
    if emit_vn:
        out_specs.append(pl.BlockSpec((rows, GM_WIDTH), lambda i: (i, 0)))
        out_shape.append(jax.ShapeDtypeStruct((n_blocks * rows, GM_WIDTH), F32))
    kern = functools.partial(_gmlp_kernel, chunk=chunk, emit_vn=emit_vn)
    aliases = {}
    if prev is not None:
        in_specs.append(pl.BlockSpec(memory_space=pl.ANY))
        args.append(prev)
        aliases = {len(args) - 1: 0}
        kern = _drop_inputs(kern, len(args) - 1, 1)
    return pl.pallas_call(
        kern,
        grid=(n_blocks,),
        in_specs=in_specs,
        out_specs=out_specs,
        out_shape=out_shape,
        scratch_shapes=[pltpu.VMEM((rows, GM_WIDTH), F32)],
        input_output_aliases=aliases,
        compiler_params=_params("parallel"),
        name="gmlp_vn" if emit_vn else "gmlp",
    )(*args)


def _post_mix_kernel(x_ref, a_ref, b_ref, wa_ref, wb_ref, xg_ref, wq_ref, k_ref, v_ref, wo_ref,
                     fg_ref, wt_ref, rb_ref, x2_ref, h_ref, e_ref, gate_ref, cnt_ref, att_ref, *, n_batch, kv_layer):
    rows = x_ref.shape[0]
    tb = rows // n_batch
    x = x_ref[...] + _dot(a_ref[...], wa_ref[...]) + _dot(b_ref[...], wb_ref[...])
    q = _dot(_rms(x, xg_ref[...]).astype(BF16), wq_ref[...]).astype(BF16)
    scale = XA_HEAD_DIM ** -0.5
    for b in range(n_batch):
        rs = slice(b * tb, (b + 1) * tb)
        for h in range(XA_HEADS):
            cols = slice(h * XA_HEAD_DIM, (h + 1) * XA_HEAD_DIM)
            if kv_layer is None:
                k_h, v_h = k_ref[b, :, cols], v_ref[b, :, cols]
            else:
                k_h, v_h = k_ref[b, :, h, :].astype(BF16), v_ref[b, :, h, :].astype(BF16)
            s = lax.dot_general(q[rs, cols], k_h, (((1,), (1,)), ((), ())),
                                preferred_element_type=F32) * scale
            e = jnp.exp(s - jnp.max(s, axis=-1, keepdims=True))
            p = e / jnp.sum(e, axis=-1, keepdims=True)
            att_ref[rs, cols] = _dot(p.astype(BF16), v_h).astype(BF16)
    x = x + _dot(att_ref[...], wo_ref[...])
    x2_ref[...] = x
    _route_rows(x, fg_ref, wt_ref, rb_ref, h_ref, e_ref, gate_ref, cnt_ref)


def _post_mix(x, x_first, a, b, w_out, xg, wq, k, v, wo, fg, rw, rb,
              rows, n_batch, first_block, n_blocks, blocks_per_kv, prev=None, kv_layer=None):
    const2 = lambda i: (0, 0)
    if kv_layer is None:
        kv_spec = pl.BlockSpec((n_batch, N_MEM, D_MODEL), lambda i: (i // blocks_per_kv, 0, 0))
    else:
        kv_spec = pl.BlockSpec((None, n_batch, N_MEM, XA_HEADS, XA_HEAD_DIM),
                               lambda i: (kv_layer, i // blocks_per_kv, 0, 0, 0))
    row_map = lambda i: (first_block + i, 0)
    d = D_MODEL
    in_specs = [pl.BlockSpec((rows, d), lambda i: (x_first + i, 0)),
                pl.BlockSpec((rows, S5_WIDTH), row_map),
                pl.BlockSpec((rows, GM_WIDTH), row_map),
                pl.BlockSpec((S5_WIDTH, d), const2),
                pl.BlockSpec((GM_WIDTH, d), const2),
                pl.BlockSpec((1, d), const2),
                pl.BlockSpec((d, d), const2),
                kv_spec, kv_spec,
                pl.BlockSpec((d, d), const2),
                pl.BlockSpec((1, d), const2),
                pl.BlockSpec((N_EXPERTS, d), const2),
                pl.BlockSpec((N_EXPERTS, 1), const2)]
    args = [x, a, b, w_out[:S5_WIDTH], w_out[S5_WIDTH:], xg.reshape(1, d), wq, k, v, wo,
            fg.reshape(1, d), rw.T.astype(BF16), rb.reshape(N_EXPERTS, 1)]
    kern = functools.partial(_post_mix_kernel, n_batch=n_batch, kv_layer=kv_layer)
    aliases = {}
    if prev is not None:
        n_in = len(args)
        in_specs += [pl.BlockSpec(memory_space=pl.ANY)] * len(prev)
        args += list(prev)
        aliases = {n_in + o: o for o in range(len(prev))}
        kern = _drop_inputs(kern, n_in, len(prev))
    return pl.pallas_call(
        kern,
        grid=(n_blocks,),
        in_specs=in_specs,
        out_specs=[pl.BlockSpec((rows, d), row_map),
                   pl.BlockSpec((rows // SUBLANES,) + PIECE_TILE, lambda i: (first_block + i, 0, 0, 0)),
                   pl.BlockSpec((TOP_K, rows), lambda i: (0, first_block + i)),
                   pl.BlockSpec((TOP_K, rows), lambda i: (0, first_block + i)),
                   pl.BlockSpec((None, N_EXPERTS, LANES), lambda i: (i, 0, 0))],
        out_shape=[jax.ShapeDtypeStruct((N_TOK, d), F32),
                   jax.ShapeDtypeStruct((N_TOK // SUBLANES,) + PIECE_TILE, F32),
                   jax.ShapeDtypeStruct((TOP_K, N_TOK), jnp.int32),
                   jax.ShapeDtypeStruct((TOP_K, N_TOK), F32),
                   jax.ShapeDtypeStruct((n_blocks, N_EXPERTS, LANES), F32)],
        scratch_shapes=[pltpu.VMEM((rows, d), BF16)],
        input_output_aliases=aliases,
        compiler_params=_params("parallel"),
        name="post_mix",
    )(*args)


def _moe_kernel(be_ref, nu_ref, first_ref, slot_ref, nxt_ref, parts_ref,
                x_ref, wg_hbm, wu_hbm, wd_hbm, bg_ref, bu_ref, bd_ref, o_ref,
                wbuf_ref, wgb_ref, wub_ref, wdb_ref, sem, *, layer):
    i = pl.program_id(0)
    active = i < nu_ref[0]

    def copies(e, s):
        return [pltpu.make_async_copy(w.at[layer, e], wbuf_ref.at[s, j], sem.at[s, j])
                for j, w in enumerate((wg_hbm, wu_hbm, wd_hbm))]

    @pl.when(jnp.logical_and(active, first_ref[i] == 1))
    def _():
        s = slot_ref[i]

        @pl.when(i == 0)
        def _():
            for c in copies(be_ref[0], s):
                c.start()

        @pl.when(nxt_ref[i] >= 0)
        def _():
            for c in copies(nxt_ref[i], 1 - s):
                c.start()

        for c in copies(be_ref[i], s):
            c.wait()

        def cast(r, _):
            rs = pl.ds(pl.multiple_of(r * MOE_CAST_ROWS, MOE_CAST_ROWS), MOE_CAST_ROWS)
            wgb_ref[rs, :] = wbuf_ref[s, 0, rs, :].astype(BF16)
            wub_ref[rs, :] = wbuf_ref[s, 1, rs, :].astype(BF16)
            wdb_ref[rs, :] = wbuf_ref[s, 2, rs, :].astype(BF16)
            return 0
        lax.fori_loop(0, D_MODEL // MOE_CAST_ROWS, cast, 0)

    def experts(n_rows):
        x = jnp.concatenate([x_ref[:n_rows // SUBLANES, c].reshape(n_rows, LANES) for c in range(PIECES_PER_ROW)],
                            axis=1).astype(BF16)
        g = jnp.minimum(_dot(x, wgb_ref[...]) + bg_ref[...], SWIGLU_LIMIT)
        u = jnp.clip(_dot(x, wub_ref[...]) + bu_ref[...], -SWIGLU_LIMIT, SWIGLU_LIMIT)
        a = g * jax.nn.sigmoid(SWIGLU_ALPHA * g) * (u + 1.0)
        o_ref[:n_rows, :] = _dot(a.astype(BF16), wdb_ref[...]) + bd_ref[...]

    for p in range(1, MOE_ROWS // MOE_PART + 1):
        pl.when(jnp.logical_and(active, parts_ref[i] == p))(functools.partial(experts, p * MOE_PART))


def _moe_call(layer, xs, sched, w_gate, b_gate, w_up, b_up, w_down, b_down):
    def row_map(i, be, nu, *_):
        return (jnp.minimum(i, nu[0] - 1), 0)

    def piece_map(i, be, nu, *_):
        return (jnp.minimum(i, nu[0] - 1), 0, 0, 0)

    def b_map(i, be, *_):
        return (layer, be[i], 0, 0)

    b_spec = pl.BlockSpec((None, None, 1, D_FF), b_map)
    w_spec = pl.BlockSpec(memory_space=pl.ANY)
    grid_spec = pltpu.PrefetchScalarGridSpec(
        num_scalar_prefetch=6,
        grid=(MOE_BLOCKS,),
        in_specs=[pl.BlockSpec((MOE_ROWS // SUBLANES,) + PIECE_TILE, piece_map),
                  w_spec, w_spec, w_spec, b_spec, b_spec, b_spec],
        out_specs=pl.BlockSpec((MOE_ROWS, D_MODEL), row_map),
        scratch_shapes=[pltpu.VMEM((2, 3, D_MODEL, D_FF), F32)]
        + [pltpu.VMEM((D_MODEL, D_FF), BF16)] * 3
        + [pltpu.SemaphoreType.DMA((2, 3))],
    )
    shape4 = (DEPTH, N_EXPERTS, 1, D_FF)
    return pl.pallas_call(
        functools.partial(_moe_kernel, layer=layer),
        grid_spec=grid_spec,
        out_shape=jax.ShapeDtypeStruct((MOE_BLOCKS * MOE_ROWS, D_MODEL), F32),
        compiler_params=pltpu.CompilerParams(dimension_semantics=("arbitrary",),
                                             vmem_limit_bytes=MOE_VMEM_LIMIT),
        name="moe_experts",
    )(*sched, xs, w_gate, w_up, w_down, b_gate.reshape(shape4), b_up.reshape(shape4),
      b_down.reshape(shape4))


DISPATCH_TOKENS = 48
DMA_PIECES = 128
DISPATCH_DMAS = DISPATCH_TOKENS * PIECES_PER_ROW // DMA_PIECES
DISPATCH_CHUNKS = N_TOK // SC_WORKERS // DISPATCH_TOKENS
IDX_ROWS = 16
SC_LANES = 16
MOE_ROWS_TOTAL = MOE_BLOCKS * MOE_ROWS
TILE_PIECES = PIECES_PER_ROW * SUBLANES


def _piece_index_rows(p_v, idx_v, row0, tok0, n_dma):
    lane = lax.iota(jnp.int32, SC_LANES)
    in_tile = lane & (SUBLANES - 1)
    piece = (lane >> 3) * SUBLANES
    per_vec = SC_LANES // SUBLANES
    tiles_per_dma = DMA_PIECES // TILE_PIECES
    for m in range(n_dma):
        for tr in range(tiles_per_dma):
            first = plsc.load_gather(p_v, [in_tile + (tok0 + (m * tiles_per_dma + tr) * SUBLANES)])
            for v in range(PIECES_PER_ROW // per_vec):
                idx_v[row0 + m, pl.ds(tr * TILE_PIECES + v * SC_LANES, SC_LANES)] = (
                    first + (piece + v * per_vec * SUBLANES))


def _dispatch(h4, dest_t):
    first_piece = ((dest_t // SUBLANES) * TILE_PIECES + dest_t % SUBLANES).reshape(-1)
    chunk_pieces = DISPATCH_TOKENS * PIECES_PER_ROW
    mesh = plsc.VectorSubcoreMesh(core_axis_name="c", subcore_axis_name="s")

    @functools.partial(
        pl.kernel, mesh=mesh,
        out_type=jax.ShapeDtypeStruct((MOE_ROWS_TOTAL * PIECES_PER_ROW, LANES), F32),
        scratch_types=[pltpu.VMEM((chunk_pieces, LANES), F32), pltpu.VMEM((IDX_ROWS, DMA_PIECES), jnp.int32),
                       pltpu.VMEM((TOP_K * DISPATCH_TOKENS,), jnp.int32), pltpu.SemaphoreType.DMA],
        compiler_params=pltpu.CompilerParams(needs_layout_passes=False),
    )
    def scatter_rows(h_hbm, p_hbm, o_hbm, rows_v, idx_v, p_v, sem):
        wid = lax.axis_index("s") * SC_CORES + lax.axis_index("c")

        @pl.loop(0, DISPATCH_CHUNKS)
        def _(j):
            q = wid * DISPATCH_CHUNKS + j
            pltpu.sync_copy(h_hbm.at[pl.ds(pl.multiple_of(q * chunk_pieces, SUBLANES), chunk_pieces)], rows_v)
            for k in range(TOP_K):
                pltpu.sync_copy(
                    p_hbm.at[pl.ds(pl.multiple_of(k * N_TOK + q * DISPATCH_TOKENS, SUBLANES), DISPATCH_TOKENS)],
                    p_v.at[pl.ds(k * DISPATCH_TOKENS, DISPATCH_TOKENS)])
            for k in range(TOP_K):
                _piece_index_rows(p_v, idx_v, k * DISPATCH_DMAS, k * DISPATCH_TOKENS, DISPATCH_DMAS)
            copies = [pltpu.async_copy(rows_v.at[pl.ds(m * DMA_PIECES, DMA_PIECES)],
                                       o_hbm.at[idx_v.at[k * DISPATCH_DMAS + m]], sem)
                      for k in range(TOP_K) for m in range(DISPATCH_DMAS)]
            for cp in copies:
                cp.wait()

    xs = scatter_rows(h4.reshape(N_TOK * PIECES_PER_ROW, LANES), first_piece)
    return xs.reshape((MOE_ROWS_TOTAL // SUBLANES,) + PIECE_TILE)


def _route(e_t, cnt_prompt, cnt_sample):
    cnt_sample = jnp.sum(cnt_sample[:, :, 0].reshape(SAMPLE_BLOCKS, -1, N_EXPERTS), axis=1)
    blk_cnt = jnp.concatenate([cnt_prompt[:, :, 0], cnt_sample]).astype(jnp.int32)
    def before(n):
        i = jnp.arange(n, dtype=jnp.int32)
        return i[None, :] < i[:, None]

    counts = jnp.sum(blk_cnt, axis=0)
    padded = ((counts + MOE_ROWS - 1) // MOE_ROWS) * MOE_ROWS
    pstart = jnp.sum(jnp.where(before(N_EXPERTS), padded[None, :], 0), axis=1)
    pend = pstart + padded
    base = pstart[None, :] + jnp.sum(jnp.where(before(blk_cnt.shape[0])[:, :, None], blk_cnt[None], 0), axis=1)
    dest_t = _dest(e_t, base.astype(F32)[:, :, None])
    n_used = jnp.sum(padded) // MOE_ROWS
    ids = jnp.arange(MOE_BLOCKS, dtype=jnp.int32)
    blk = jnp.minimum(ids, n_used - 1) * MOE_ROWS
    blk_exp = jnp.minimum(jnp.sum((pend[None, :] <= blk[:, None]).astype(jnp.int32), axis=1), N_EXPERTS - 1)
    eid = jnp.arange(N_EXPERTS, dtype=jnp.int32)
    of_blk = blk_exp[:, None] == eid[None, :]

    def lookup(table):
        return jnp.sum(jnp.where(of_blk, table[None, :], 0), axis=1)

    first = jnp.logical_and(ids < n_used, jnp.logical_or(ids == 0, blk_exp != jnp.roll(blk_exp, 1)))
    first = first.astype(jnp.int32)
    slot = jnp.sum(jnp.where(before(MOE_BLOCKS), first[None, :], 0), axis=1) % 2
    later = jnp.logical_and(eid[None, :] > eid[:, None], (counts > 0)[None, :])
    nxt_e = jnp.min(jnp.where(later, eid[None, :], N_EXPERTS), axis=1)
    nxt = lookup(jnp.where(nxt_e < N_EXPERTS, nxt_e, -1))
    real = jnp.clip(lookup(pstart + counts) - blk, 0, MOE_ROWS)
    parts = (real + MOE_PART - 1) // MOE_PART
    return dest_t, (blk_exp, n_used.reshape(1).astype(jnp.int32), first, slot.astype(jnp.int32),
                    nxt.astype(jnp.int32), parts.astype(jnp.int32))


def kernel(x_prompt, x_sample, mem_prompt, state_s5_re, state_s5_im, cache_mem_k, cache_mem_v, norm_mix_g, w_in, s5_lambda_re, s5_lambda_im, s5_b_re, s5_b_im, s5_c_re, s5_c_im, s5_d, s5_log_step, s5_w_glu, gm_ln_g, gm_ln_b, gm_w_s, gm_b_s, out_norm_g, w_out, xa_norm_g, mem_norm_g, w_xq, w_xk, w_xv, w_xo, ffn_norm_g, router_w, router_b, e_w_gate, e_b_gate, e_w_up, e_b_up, e_w_down, e_b_down, final_norm_g):
    xs = (x_prompt.reshape(N_PROMPT, D_MODEL), x_sample.reshape(N_SAMPLE, D_MODEL))
    mem = mem_prompt.reshape(BATCH * N_MEM, D_MODEL)
    outs = {k: [] for k in ('s5r_p', 's5i_p', 's5r_s', 's5i_s', 'gmv_s')}
    mem_kv = None
    for l in range(DEPTH):
        unified = len(xs) == 1

        def sample_first(rows):
            return N_PROMPT // rows if unified else 0

        in_proj = (norm_mix_g[l], w_in[l].astype(BF16))
        sw = _s5_weights(s5_lambda_re[l], s5_lambda_im[l], s5_b_re[l], s5_b_im[l], s5_c_re[l], s5_c_im[l],
                         s5_d[l], s5_log_step[l], S5_SEG_PROMPT)
        w_glu = s5_w_glu[l].astype(BF16)
        og = out_norm_g[l]
        ms5, er_p, ei_p = _s5_call(xs[0], 0, *in_proj, sw, w_glu, og[:S5_WIDTH], S5_SEG_PROMPT, True)
        h0 = (state_s5_re[l].reshape(DEC_BATCH // SUBLANES, SUBLANES, S5_COLS),
              state_s5_im[l].reshape(DEC_BATCH // SUBLANES, SUBLANES, S5_COLS))
        ms5, er_s, ei_s = _s5_call(xs[-1], sample_first(DEC_SEQ * SUBLANES), *in_proj, sw, w_glu, og[:S5_WIDTH],
                                   DEC_SEQ, False, h0=h0, prev=ms5)
        outs['s5r_p'].append(er_p[:, SUBLANES - 1].reshape(BATCH, S5_GROUPS, S5_STATE))
        outs['s5i_p'].append(ei_p[:, SUBLANES - 1].reshape(BATCH, S5_GROUPS, S5_STATE))
        outs['s5r_s'].append(er_s.reshape(DEC_BATCH, S5_GROUPS, S5_STATE))
        outs['s5i_s'].append(ei_s.reshape(DEC_BATCH, S5_GROUPS, S5_STATE))

        (mgm,) = _gmlp_call(xs[0], 0, *in_proj, gm_ln_g[l], gm_ln_b[l], gm_w_s[l], gm_b_s[l], og[S5_WIDTH:],
                            GM_ROWS_PROMPT, GM_CHUNK, 0, N_PROMPT // GM_ROWS_PROMPT, False)
        mgm, vn = _gmlp_call(xs[-1], sample_first(N_SAMPLE), *in_proj, gm_ln_g[l], gm_ln_b[l], gm_w_s[l],
                             gm_b_s[l], og[S5_WIDTH:], N_SAMPLE, DEC_SEQ, N_PROMPT // N_SAMPLE, 1, True, prev=mgm)
        outs['gmv_s'].append(vn.reshape(DEC_BATCH, DEC_SEQ, GM_HEADS, GM_HEAD_DIM))
        *mem_kv, mk, mv = _mem_kv(mem, mem_norm_g[l], w_xk[l].astype(BF16), w_xv[l].astype(BF16), l, prev=mem_kv)
        shared = (w_out[l].astype(BF16), xa_norm_g[l], w_xq[l].astype(BF16))
        tail = (w_xo[l].astype(BF16), ffn_norm_g[l], router_w[l], router_b[l])
        rows_s = XA_SAMPLE_BATCHES * DEC_SEQ
        post_p = _post_mix(xs[0], 0, ms5, mgm, *shared, mk, mv, *tail,
                           XA_ROWS_PROMPT, 1, 0, N_PROMPT // XA_ROWS_PROMPT, SEQ // XA_ROWS_PROMPT)
        post_s = _post_mix(xs[-1], sample_first(rows_s), ms5, mgm, *shared,
                           cache_mem_k, cache_mem_v, *tail,
                           rows_s, XA_SAMPLE_BATCHES, N_PROMPT // rows_s, N_SAMPLE // rows_s, 1,
                           prev=post_p[:4], kv_layer=l)
        x, h, e_t, gates_t = post_s[:4]

        dest_t, sched = _route(e_t, post_p[4], post_s[4])
        ys = _moe_call(l, _dispatch(h, dest_t), sched, e_w_gate, e_b_gate, e_w_up, e_b_up, e_w_down, e_b_down)
        x = _combine(x, ys[dest_t.reshape(-1)], gates_t.T, final_norm_g if l == DEPTH - 1 else None)
        xs = (x,)

    y_prompt = x[0].reshape(BATCH, SEQ, D_MODEL)
    y_sample = x[1].reshape(DEC_BATCH, DEC_SEQ, D_MODEL)
    st = jnp.stack
    return (y_prompt, y_sample, st(outs['s5r_p']), st(outs['s5i_p']), mem_kv[0], mem_kv[1],
            st(outs['s5r_s']), st(outs['s5i_s']), st(outs['gmv_s']))
```

```python
import functools

import jax
import jax.numpy as jnp
from jax import lax
from jax.experimental import pallas as pl
from jax.experimental.pallas import tpu as pltpu
from jax.experimental.pallas import tpu_sc as plsc

F32 = jnp.float32
BF16 = jnp.bfloat16

D_MODEL = 1024
BATCH = 4
SEQ = 4096
DEPTH = 2
DEC_BATCH = 16
DEC_SEQ = 32
S5_GROUPS = 32
S5_GROUP_CH = 16
S5_STATE = 64
S5_WIDTH = S5_GROUPS * S5_GROUP_CH
S5_COLS = S5_GROUPS * S5_STATE
GM_HEADS = 4
GM_HEAD_DIM = 128
GM_WIDTH = GM_HEADS * GM_HEAD_DIM
GM_CHUNK = 128
IN_PROJ = S5_WIDTH + 2 * GM_WIDTH
N_MEM = 256
XA_HEADS = 4
XA_HEAD_DIM = D_MODEL // XA_HEADS
N_EXPERTS = 32
TOP_K = 4
D_FF = D_MODEL
SWIGLU_LIMIT = 7.0
SWIGLU_ALPHA = 1.702
EPS = 1e-6
EIG_CLIP = -1e-4

N_PROMPT = BATCH * SEQ
N_SAMPLE = DEC_BATCH * DEC_SEQ
N_TOK = N_PROMPT + N_SAMPLE

SUBLANES = 8
LANES = 128
MXU_DIM = 256
VMEM_LIMIT = 48 * 1024 * 1024
MOE_VMEM_LIMIT = 60 * 1024 * 1024
SC_CORES = 2
SC_SUBCORES = 16
SC_WORKERS = SC_CORES * SC_SUBCORES

PACK_WIDTH = D_MODEL // 2
PIECES_PER_ROW = PACK_WIDTH // LANES
PIECE_TILE = (PIECES_PER_ROW, SUBLANES, LANES)
HIGH_HALF = 0xFFFF0000

ROW_BLOCK = 512
S5_SEG_PROMPT = 64
S5_COL_BLOCK = 512
GM_ROWS_PROMPT = 256
XA_ROWS_PROMPT = 512
XA_SAMPLE_BATCHES = 4
MOE_ROWS = 1024
MOE_PART = 256
MOE_BLOCKS = -(-N_TOK * TOP_K // MOE_ROWS) + N_EXPERTS
MOE_CAST_ROWS = 128


def _params(*sem):
    return pltpu.CompilerParams(dimension_semantics=sem, vmem_limit_bytes=VMEM_LIMIT)


def _rms(x, g):
    return x * lax.rsqrt(jnp.mean(x * x, axis=-1, keepdims=True) + EPS) * g


def _dot(a, b):
    return jnp.dot(a, b, preferred_element_type=F32)


PROMPT_BLOCKS = N_PROMPT // ROW_BLOCK
SAMPLE_BLOCKS = N_SAMPLE // ROW_BLOCK


def _token_sources(xs):
    if len(xs) == 1:
        return [pl.BlockSpec((ROW_BLOCK, D_MODEL), lambda i: (i, 0))]
    return [pl.BlockSpec((ROW_BLOCK, D_MODEL), lambda i: (jnp.minimum(i, PROMPT_BLOCKS - 1), 0)),
            pl.BlockSpec((ROW_BLOCK, D_MODEL), lambda i: (jnp.maximum(i - PROMPT_BLOCKS, 0), 0))]


def _for_group(refs, body):
    if len(refs) == 1:
        body(refs[0])
        return
    i = pl.program_id(0)
    pl.when(i < PROMPT_BLOCKS)(lambda: body(refs[0]))
    pl.when(i >= PROMPT_BLOCKS)(lambda: body(refs[1]))


def _mem_kv_kernel(m_ref, g_ref, wk_ref, wv_ref, k5_ref, v5_ref, kb_ref, vb_ref):
    m = _rms(m_ref[...], g_ref[...]).astype(BF16)
    for w_ref, o5_ref, ob_ref in ((wk_ref, k5_ref, kb_ref), (wv_ref, v5_ref, vb_ref)):
        kv = _dot(m, w_ref[...])
        ob_ref[...] = kv.astype(BF16)
        for h in range(XA_HEADS):
            o5_ref[:, h, :] = kv[:, h * XA_HEAD_DIM:(h + 1) * XA_HEAD_DIM]


def _mem_kv(mem, g, wk, wv, layer, prev=None):
    d = D_MODEL
    const2 = lambda b: (0, 0)
    in_specs = [pl.BlockSpec((N_MEM, d), lambda b: (b, 0)),
                pl.BlockSpec((1, d), const2),
                pl.BlockSpec((d, d), const2),
                pl.BlockSpec((d, d), const2)]
    args = [mem, g.reshape(1, d), wk, wv]
    kern = _mem_kv_kernel
    aliases = {}
    if prev is not None:
        in_specs += [pl.BlockSpec(memory_space=pl.ANY)] * 2
        args += list(prev)
        aliases = {4: 0, 5: 1}
        kern = _drop_inputs(kern, 4, 2)
    out5 = pl.BlockSpec((None, None, N_MEM, XA_HEADS, XA_HEAD_DIM), lambda b: (layer, b, 0, 0, 0))
    outb = pl.BlockSpec((None, N_MEM, d), lambda b: (b, 0, 0))
    shape5 = jax.ShapeDtypeStruct((DEPTH, BATCH, N_MEM, XA_HEADS, XA_HEAD_DIM), F32)
    shapeb = jax.ShapeDtypeStruct((BATCH, N_MEM, d), BF16)
    return pl.pallas_call(
        kern,
        grid=(BATCH,),
        in_specs=in_specs,
        out_specs=[out5, out5, outb, outb],
        out_shape=[shape5, shape5, shapeb, shapeb],
        input_output_aliases=aliases,
        compiler_params=_params("parallel"),
        name="mem_kv",
    )(*args)


def _route_rows(x, g_ref, wt_ref, b_ref, h_ref, e_ref, gate_ref, cnt_ref):
    h = _rms(x, g_ref[...]).astype(BF16)
    bits = lax.bitcast_convert_type(h.astype(F32), jnp.uint32)
    words = lax.bitcast_convert_type(bits[:, PACK_WIDTH:] | (bits[:, :PACK_WIDTH] >> 16), F32)
    for c in range(PIECES_PER_ROW):
        h_ref[:, c] = words[:, c * LANES:(c + 1) * LANES].reshape(h.shape[0] // SUBLANES, SUBLANES, LANES)
    logits = lax.dot_general(wt_ref[...], h, (((1,), (1,)), ((), ())),
                             preferred_element_type=F32) + b_ref[...]
    sub = lax.broadcasted_iota(jnp.int32, logits.shape, 0)
    work = logits
    chosen = jnp.zeros(logits.shape, F32)
    vals, idxs = [], []
    for _ in range(TOP_K):
        m = jnp.max(work, axis=0, keepdims=True)
        idx = jnp.min(jnp.where(work == m, sub, N_EXPERTS), axis=0, keepdims=True)
        sel = sub == idx
        vals.append(m)
        idxs.append(idx)
        work = jnp.where(sel, -jnp.inf, work)
        chosen = jnp.where(sel, 1.0, chosen)
    ex = [jnp.exp(v - vals[0]) for v in vals]
    den = ex[0] + ex[1] + ex[2] + ex[3]
    gate_ref[...] = jnp.concatenate([e / den for e in ex], axis=0)
    e_ref[...] = jnp.concatenate(idxs, axis=0)
    cnt_ref[...] = jnp.broadcast_to(jnp.sum(chosen, axis=1, keepdims=True), (N_EXPERTS, LANES))


def _dest_kernel(e_ref, base_ref, d_ref):
    e = e_ref[...]
    rows = e.shape[1]
    sub = lax.broadcasted_iota(jnp.int32, (N_EXPERTS, rows), 0)
    hot = [sub == e[k:k + 1, :] for k in range(TOP_K)]
    chosen = sum(jnp.where(hk, 1.0, 0.0) for hk in hot)
    earlier = (lax.broadcasted_iota(jnp.int32, (rows, rows), 0)
               < lax.broadcasted_iota(jnp.int32, (rows, rows), 1))
    pos = base_ref[...] + _dot(chosen.astype(BF16), jnp.where(earlier, 1.0, 0.0).astype(BF16))
    d_ref[...] = jnp.concatenate(
        [jnp.sum(jnp.where(hk, pos, 0.0), axis=0, keepdims=True) for hk in hot], axis=0).astype(jnp.int32)


def _dest(e_t, base):
    n = e_t.shape[1]
    rows = ROW_BLOCK
    return pl.pallas_call(
        _dest_kernel,
        grid=(n // rows,),
        in_specs=[pl.BlockSpec((TOP_K, rows), lambda i: (0, i)),
                  pl.BlockSpec((None, N_EXPERTS, 1), lambda i: (i, 0, 0))],
        out_specs=pl.BlockSpec((TOP_K, rows), lambda i: (0, i)),
        out_shape=jax.ShapeDtypeStruct((TOP_K, n), jnp.int32),
        compiler_params=_params("parallel"),
        name="moe_dest",
    )(e_t, base)


def _combine_kernel(*refs, final):
    if final:
        x_ref, g0, g1, g2, g3, gate_ref, fg_ref, op_ref, os_ref = refs
    else:
        x_ref, g0, g1, g2, g3, gate_ref, o_ref = refs
    acc = x_ref[...]
    for k, g in enumerate((g0, g1, g2, g3)):
        acc = acc + gate_ref[:, k:k + 1] * g[...]
    if final:
        def store(o_ref):
            o_ref[...] = _rms(acc, fg_ref[...])
        _for_group((op_ref, os_ref), store)
    else:
        o_ref[...] = acc


def _combine(x, picked, gates, final_g=None):
    n, d = x.shape
    rows = ROW_BLOCK
    picked = picked.reshape(TOP_K, n, d)
    final = final_g is not None
    in_specs = [pl.BlockSpec((rows, d), lambda i: (i, 0))]
    in_specs += [pl.BlockSpec((None, rows, d), functools.partial(lambda i, k: (k, i, 0), k=k))
                 for k in range(TOP_K)]
    in_specs.append(pl.BlockSpec((rows, TOP_K), lambda i: (i, 0)))
    args = [x, picked, picked, picked, picked, gates]
    out_specs = pl.BlockSpec((rows, d), lambda i: (i, 0))
    out_shape = jax.ShapeDtypeStruct((n, d), F32)
    if final:
        in_specs.append(pl.BlockSpec((1, d), lambda i: (0, 0)))
        args.append(final_g.reshape(1, d))
        out_specs = _token_sources((None, None))
        out_shape = [jax.ShapeDtypeStruct((N_PROMPT, d), F32), jax.ShapeDtypeStruct((N_SAMPLE, d), F32)]
    return pl.pallas_call(
        functools.partial(_combine_kernel, final=final),
        grid=(n // rows,),
        in_specs=in_specs,
        out_specs=out_specs,
        out_shape=out_shape,
        compiler_params=_params("arbitrary" if final else "parallel"),
        name="moe_combine",
    )(*args)


def _cmul_add(ar, ai, hr, hi, br, bi):
    return ar * hr - ai * hi + br, ar * hi + ai * hr + bi


def _s5_kernel(*refs, seg, chained):
    if chained:
        (x_ref, ng_ref, win_ref, wbr_ref, wbi_ref, wcr_ref, wci_ref, ar_ref, ai_ref, pr_ref, pi_ref, d_ref,
         wglu_ref, og_ref, o_ref, er_ref, ei_ref,
         up_ref, hr_ref, hi_ref, sr_ref, si_ref, op_ref, cr_ref, ci_ref) = refs
    else:
        (x_ref, ng_ref, win_ref, wbr_ref, wbi_ref, wcr_ref, wci_ref, ar_ref, ai_ref, pr_ref, pi_ref, d_ref,
         wglu_ref, og_ref, h0r_ref, h0i_ref, o_ref, er_ref, ei_ref,
         up_ref, hr_ref, hi_ref, sr_ref, si_ref, op_ref) = refs
    rows = seg * SUBLANES
    half = S5_COLS // 2

    u = _dot(_rms(x_ref[...], ng_ref[...]).astype(BF16), win_ref[...])
    n_lane_blk = S5_WIDTH // LANES
    for j in range(SUBLANES):
        for i0 in range(0, seg, SUBLANES):
            for c in range(n_lane_blk):
                up_ref[c, pl.ds(i0 * SUBLANES + j, SUBLANES, stride=SUBLANES), :] = (
                    u[j * seg + i0:j * seg + i0 + SUBLANES, c * LANES:(c + 1) * LANES])

    up = jnp.concatenate([up_ref[c] for c in range(n_lane_blk)], axis=1)
    ub = up.astype(BF16)
    for k in range(2):
        uk = ub[:, k * MXU_DIM:(k + 1) * MXU_DIM]
        hr_ref[:, k * half:(k + 1) * half] = _dot(uk, wbr_ref[k])
        hi_ref[:, k * half:(k + 1) * half] = _dot(uk, wbi_ref[k])

    for cb in range(S5_COLS // S5_COL_BLOCK):
        cols = slice(cb * S5_COL_BLOCK, (cb + 1) * S5_COL_BLOCK)
        ar = jnp.broadcast_to(ar_ref[:, cols], (SUBLANES, S5_COL_BLOCK))
        ai = jnp.broadcast_to(ai_ref[:, cols], (SUBLANES, S5_COL_BLOCK))

        def step(i, carry, cols=cols, ar=ar, ai=ai):
            r = pl.multiple_of(i * SUBLANES, SUBLANES)
            h_r, h_i = _cmul_add(ar, ai, carry[0], carry[1],
                                 hr_ref[pl.ds(r, SUBLANES), cols], hi_ref[pl.ds(r, SUBLANES), cols])
            hr_ref[pl.ds(r, SUBLANES), cols] = h_r
            hi_ref[pl.ds(r, SUBLANES), cols] = h_i
            return h_r, h_i

        zero = jnp.zeros((SUBLANES, S5_COL_BLOCK), F32)
        lax.fori_loop(0, seg, step, (zero, zero), unroll=2)

    end_r = hr_ref[rows - SUBLANES:rows, :]
    end_i = hi_ref[rows - SUBLANES:rows, :]
    as_r = pr_ref[rows - 1:rows, :]
    as_i = pi_ref[rows - 1:rows, :]
    if chained:
        @pl.when(pl.program_id(1) == 0)
        def _():
            cr_ref[...] = jnp.zeros_like(cr_ref)
            ci_ref[...] = jnp.zeros_like(ci_ref)

        cur_r, cur_i = cr_ref[...], ci_ref[...]
        for j in range(SUBLANES):
            sr_ref[j:j + 1, :] = cur_r
            si_ref[j:j + 1, :] = cur_i
            cur_r, cur_i = _cmul_add(as_r, as_i, cur_r, cur_i, end_r[j:j + 1], end_i[j:j + 1])
        cr_ref[...] = cur_r
        ci_ref[...] = cur_i
    else:
        sr_ref[...] = h0r_ref[...]
        si_ref[...] = h0i_ref[...]
    t_r, t_i = _cmul_add(as_r, as_i, sr_ref[...], si_ref[...], end_r, end_i)
    er_ref[...] = t_r
    ei_ref[...] = t_i

    slabs_per_window = MXU_DIM // (MXU_DIM // S5_STATE * S5_GROUP_CH)
    ys = []
    for w in range(S5_WIDTH // MXU_DIM):
        acc = d_ref[:, w * MXU_DIM:(w + 1) * MXU_DIM] * jnp.concatenate(
            [up_ref[c] for c in range(w * MXU_DIM // LANES, (w + 1) * MXU_DIM // LANES)], axis=1)
        for s in range(slabs_per_window):
            cols = slice((w * slabs_per_window + s) * MXU_DIM, (w * slabs_per_window + s + 1) * MXU_DIM)
            shape3 = (seg, SUBLANES, MXU_DIM)
            h_r, h_i = _cmul_add(pr_ref[:, cols].reshape(shape3), pi_ref[:, cols].reshape(shape3),
                                 sr_ref[:, cols][None], si_ref[:, cols][None],
                                 hr_ref[:, cols].reshape(shape3), hi_ref[:, cols].reshape(shape3))
            acc += _dot(h_r.reshape(rows, MXU_DIM).astype(BF16), wcr_ref[w * slabs_per_window + s])
            acc += _dot(h_i.reshape(rows, MXU_DIM).astype(BF16), wci_ref[w * slabs_per_window + s])
        ys.append(jax.nn.gelu(acc).astype(BF16))

    z = _dot(jnp.concatenate(ys, axis=1), wglu_ref[...])
    o = _rms(z[:, :S5_WIDTH] * jax.nn.sigmoid(z[:, S5_WIDTH:]), og_ref[...])
    for c in range(n_lane_blk):
        op_ref[c] = o[:, c * LANES:(c + 1) * LANES]

    for j in range(SUBLANES):
        for i0 in range(0, seg, 2 * SUBLANES):
            for c in range(n_lane_blk):
                o_ref[j * seg + i0:j * seg + i0 + 2 * SUBLANES, c * LANES:(c + 1) * LANES] = (
                    op_ref[c, pl.ds(i0 * SUBLANES + j, 2 * SUBLANES, stride=SUBLANES), :].astype(BF16))


def _s5_weights(lam_re, lam_im, b_re, b_im, c_re, c_im, d, log_step, seg):
    step = jnp.exp(log_step)[:, None]
    lr = jnp.minimum(lam_re, EIG_CLIP)
    li = lam_im
    mag = jnp.exp(lr * step)
    ar = mag * jnp.cos(li * step)
    ai = mag * jnp.sin(li * step)
    den = lr * lr + li * li
    fr = ((ar - 1.0) * lr + ai * li) / den
    fi = (ai * lr - (ar - 1.0) * li) / den
    bbar_r = fr[..., None] * b_re - fi[..., None] * b_im
    bbar_i = fr[..., None] * b_im + fi[..., None] * b_re
    t = jnp.arange(1, seg + 1, dtype=F32)[:, None, None]
    pm = jnp.exp(lr * step * t)
    p_r = jnp.repeat((pm * jnp.cos(li * step * t)).reshape(seg, S5_COLS), SUBLANES, axis=0)
    p_i = jnp.repeat((pm * jnp.sin(li * step * t)).reshape(seg, S5_COLS), SUBLANES, axis=0)

    gk = MXU_DIM // S5_GROUP_CH

    def in_map(bbar):
        bb = bbar.reshape(S5_GROUPS // gk, gk, S5_STATE, S5_GROUP_CH)
        w = jnp.einsum('kgnh,gG->kghGn', bb, jnp.eye(gk, dtype=F32))
        return w.reshape(S5_GROUPS // gk, MXU_DIM, gk * S5_STATE).astype(BF16)

    gs = MXU_DIM // S5_STATE
    n_slab = S5_COLS // MXU_DIM
    per_win = MXU_DIM // (gs * S5_GROUP_CH)

    def out_map(c):
        cc = c.reshape(n_slab // per_win, per_win, gs, S5_GROUP_CH, S5_STATE)
        w = jnp.einsum('wsghn,sS,gG->wsgnSGh', cc, jnp.eye(per_win, dtype=F32), jnp.eye(gs, dtype=F32))
        return w.reshape(n_slab, MXU_DIM, MXU_DIM).astype(BF16)

    return dict(wbr=in_map(bbar_r), wbi=in_map(bbar_i), wcr=out_map(c_re), wci=out_map(-c_im),
                ar=ar.reshape(1, S5_COLS), ai=ai.reshape(1, S5_COLS), pr=p_r, pi=p_i,
                d=d.reshape(1, S5_WIDTH))


def _s5_call(x, x_first, norm_g, w_in, sw, w_glu, og, seg, chained, h0=None, prev=None):
    rows = seg * SUBLANES
    const2 = lambda *_: (0, 0)
    const3 = lambda *_: (0, 0, 0)
    if chained:
        n_chunk = SEQ // rows
        grid = (BATCH, n_chunk)
        row_map = lambda b, c: (b * n_chunk + c, 0)
        x_map = row_map
        end_map = lambda b, c: (b, 0, 0)
        n_end = BATCH
        sem = ("parallel", "arbitrary")
    else:
        grid = (DEC_BATCH // SUBLANES,)
        row_map = lambda k: (N_PROMPT // rows + k, 0)
        x_map = lambda k: (x_first + k, 0)
        end_map = lambda k: (k, 0, 0)
        n_end = DEC_BATCH // SUBLANES
        sem = ("parallel",)
    half = S5_COLS // 2
    in_specs = [pl.BlockSpec((rows, D_MODEL), x_map),
                pl.BlockSpec((1, D_MODEL), const2),
                pl.BlockSpec((D_MODEL, S5_WIDTH), const2),
                pl.BlockSpec((2, MXU_DIM, half), const3),
                pl.BlockSpec((2, MXU_DIM, half), const3),
                pl.BlockSpec((S5_COLS // MXU_DIM, MXU_DIM, MXU_DIM), const3),
                pl.BlockSpec((S5_COLS // MXU_DIM, MXU_DIM, MXU_DIM), const3),
                pl.BlockSpec((1, S5_COLS), const2),
                pl.BlockSpec((1, S5_COLS), const2),
                pl.BlockSpec((rows, S5_COLS), const2),
                pl.BlockSpec((rows, S5_COLS), const2),
                pl.BlockSpec((1, S5_WIDTH), const2),
                pl.BlockSpec((S5_WIDTH, 2 * S5_WIDTH), const2),
                pl.BlockSpec((1, S5_WIDTH), const2)]
    args = [x, norm_g.reshape(1, D_MODEL), w_in[:, :S5_WIDTH], sw['wbr'], sw['wbi'], sw['wcr'], sw['wci'], sw['ar'], sw['ai'],
            sw['pr'][:rows], sw['pi'][:rows], sw['d'], w_glu, og.reshape(1, S5_WIDTH)]
    scratch = [pltpu.VMEM((S5_WIDTH // LANES, rows, LANES), F32),
               pltpu.VMEM((rows, S5_COLS), F32), pltpu.VMEM((rows, S5_COLS), F32),
               pltpu.VMEM((SUBLANES, S5_COLS), F32), pltpu.VMEM((SUBLANES, S5_COLS), F32),
               pltpu.VMEM((S5_WIDTH // LANES, rows, LANES), F32)]
    aliases = {}
    if chained:
        scratch += [pltpu.VMEM((1, S5_COLS), F32), pltpu.VMEM((1, S5_COLS), F32)]
    else:
        in_specs += [pl.BlockSpec((None, SUBLANES, S5_COLS), end_map),
                     pl.BlockSpec((None, SUBLANES, S5_COLS), end_map)]
        args += [h0[0], h0[1]]
    if prev is not None:
        in_specs.append(pl.BlockSpec(memory_space=pl.ANY))
        args.append(prev)
        aliases = {len(args) - 1: 0}
    kern = functools.partial(_s5_kernel, seg=seg, chained=chained)
    if prev is not None:
        kern = _drop_inputs(kern, len(args) - 1, 1)
    return pl.pallas_call(
        kern,
        grid=grid,
        in_specs=in_specs,
        out_specs=[pl.BlockSpec((rows, S5_WIDTH), row_map),
                   pl.BlockSpec((None, SUBLANES, S5_COLS), end_map),
                   pl.BlockSpec((None, SUBLANES, S5_COLS), end_map)],
        out_shape=[jax.ShapeDtypeStruct((N_TOK, S5_WIDTH), BF16),
                   jax.ShapeDtypeStruct((n_end, SUBLANES, S5_COLS), F32),
                   jax.ShapeDtypeStruct((n_end, SUBLANES, S5_COLS), F32)],
        scratch_shapes=scratch,
        input_output_aliases=aliases,
        compiler_params=_params(*sem),
        name="s5_chained" if chained else "s5_streams",
    )(*args)


def _drop_inputs(kern, start, n):
    def wrapped(*refs):
        return kern(*refs[:start], *refs[start + n:])
    return wrapped


def _gmlp_kernel(*refs, chunk, emit_vn):
    if emit_vn:
        x_ref, ng_ref, win_ref, lng_ref, lnb_ref, ws_ref, bst_ref, og_ref, o_ref, vn_ref, acc_ref = refs
    else:
        x_ref, ng_ref, win_ref, lng_ref, lnb_ref, ws_ref, bst_ref, og_ref, o_ref, acc_ref = refs
    rows = x_ref.shape[0]
    uv = _dot(_rms(x_ref[...], ng_ref[...]).astype(BF16), win_ref[...])
    u = jax.nn.gelu(uv[:, :GM_WIDTH])
    v = jax.nn.gelu(uv[:, GM_WIDTH:])
    tri = (lax.broadcasted_iota(jnp.int32, (chunk, chunk), 0)
           >= lax.broadcasted_iota(jnp.int32, (chunk, chunk), 1))
    for h in range(GM_HEADS):
        cols = slice(h * GM_HEAD_DIM, (h + 1) * GM_HEAD_DIM)
        vh = v[:, cols]
        mu = jnp.mean(vh, axis=-1, keepdims=True)
        cen = vh - mu
        var = jnp.mean(cen * cen, axis=-1, keepdims=True)
        vn = cen * lax.rsqrt(var + EPS) * lng_ref[:, cols] + lnb_ref[:, cols]
        if emit_vn:
            vn_ref[:, cols] = vn
        w = jnp.where(tri, ws_ref[h], 0.0).astype(BF16)
        vb = vn.astype(BF16)
        bias = bst_ref[:, h:h + 1]
        for c in range(rows // chunk):
            rs = slice(c * chunk, (c + 1) * chunk)
            acc_ref[rs, cols] = u[rs, cols] * (_dot(w, vb[rs]) + bias)
    o_ref[...] = _rms(acc_ref[...], og_ref[...]).astype(BF16)


def _gmlp_call(x, x_first, norm_g, w_in, ln_g, ln_b, w_s, b_s, og, rows, chunk, first_block, n_blocks, emit_vn,
               prev=None):
    const2 = lambda i: (0, 0)
    in_specs = [pl.BlockSpec((rows, D_MODEL), lambda i: (x_first + i, 0)),
                pl.BlockSpec((1, D_MODEL), const2),
                pl.BlockSpec((D_MODEL, 2 * GM_WIDTH), const2),
                pl.BlockSpec((1, GM_WIDTH), const2),
                pl.BlockSpec((1, GM_WIDTH), const2),
                pl.BlockSpec((GM_HEADS, chunk, chunk), lambda i: (0, 0, 0)),
                pl.BlockSpec((chunk, GM_HEADS), const2),
                pl.BlockSpec((1, GM_WIDTH), const2)]
    args = [x, norm_g.reshape(1, D_MODEL), w_in[:, S5_WIDTH:], ln_g.reshape(1, GM_WIDTH), ln_b.reshape(1, GM_WIDTH),
            w_s[:, :chunk, :chunk], b_s[:, :chunk].T, og.reshape(1, GM_WIDTH)]
    out_specs = [pl.BlockSpec((rows, GM_WIDTH), lambda i: (first_block + i, 0))]
    out_shape = [jax.ShapeDtypeStruct((N_TOK, GM_WIDTH), BF16)]
    if emit_vn:
        out_specs.append(pl.BlockSpec((rows, GM_WIDTH), lambda i: (i, 0)))
        out_shape.append(jax.ShapeDtypeStruct((n_blocks * rows, GM_WIDTH), F32))
    kern = functools.partial(_gmlp_kernel, chunk=chunk, emit_vn=emit_vn)
    aliases = {}
    if prev is not None:
        in_specs.append(pl.BlockSpec(memory_space=pl.ANY))
        args.append(prev)
        aliases = {len(args) - 1: 0}
        kern = _drop_inputs(kern, len(args) - 1, 1)
    return pl.pallas_call(
        kern,
        grid=(n_blocks,),
        in_specs=in_specs,
        out_specs=out_specs,
        out_shape=out_shape,
        scratch_shapes=[pltpu.VMEM((rows, GM_WIDTH), F32)],
        input_output_aliases=aliases,
        compiler_params=_params("parallel"),
        name="gmlp_vn" if emit_vn else "gmlp",
    )(*args)


def _post_mix_kernel(x_ref, a_ref, b_ref, wa_ref, wb_ref, xg_ref, wq_ref, k_ref, v_ref, wo_ref,
                     fg_ref, wt_ref, rb_ref, x2_ref, h_ref, e_ref, gate_ref, cnt_ref, att_ref, *, n_batch, kv_layer):
    rows = x_ref.shape[0]
    tb = rows // n_batch
    x = x_ref[...] + _dot(a_ref[...], wa_ref[...]) + _dot(b_ref[...], wb_ref[...])
    q = _dot(_rms(x, xg_ref[...]).astype(BF16), wq_ref[...]).astype(BF16)
    scale = XA_HEAD_DIM ** -0.5
    for b in range(n_batch):
        rs = slice(b * tb, (b + 1) * tb)
        for h in range(XA_HEADS):
            cols = slice(h * XA_HEAD_DIM, (h + 1) * XA_HEAD_DIM)
            if kv_layer is None:
                k_h, v_h = k_ref[b, :, cols], v_ref[b, :, cols]
            else:
                k_h, v_h = k_ref[b, :, h, :].astype(BF16), v_ref[b, :, h, :].astype(BF16)
            s = lax.dot_general(q[rs, cols], k_h, (((1,), (1,)), ((), ())),
                                preferred_element_type=F32) * scale
            e = jnp.exp(s - jnp.max(s, axis=-1, keepdims=True))
            p = e / jnp.sum(e, axis=-1, keepdims=True)
            att_ref[rs, cols] = _dot(p.astype(BF16), v_h).astype(BF16)
    x = x + _dot(att_ref[...], wo_ref[...])
    x2_ref[...] = x
    _route_rows(x, fg_ref, wt_ref, rb_ref, h_ref, e_ref, gate_ref, cnt_ref)


def _post_mix(x, x_first, a, b, w_out, xg, wq, k, v, wo, fg, rw, rb,
              rows, n_batch, first_block, n_blocks, blocks_per_kv, prev=None, kv_layer=None):
    const2 = lambda i: (0, 0)
    if kv_layer is None:
        kv_spec = pl.BlockSpec((n_batch, N_MEM, D_MODEL), lambda i: (i // blocks_per_kv, 0, 0))
    else:
        kv_spec = pl.BlockSpec((None, n_batch, N_MEM, XA_HEADS, XA_HEAD_DIM),
                               lambda i: (kv_layer, i // blocks_per_kv, 0, 0, 0))
    row_map = lambda i: (first_block + i, 0)
    d = D_MODEL
    in_specs = [pl.BlockSpec((rows, d), lambda i: (x_first + i, 0)),
                pl.BlockSpec((rows, S5_WIDTH), row_map),
                pl.BlockSpec((rows, GM_WIDTH), row_map),
                pl.BlockSpec((S5_WIDTH, d), const2),
                pl.BlockSpec((GM_WIDTH, d), const2),
                pl.BlockSpec((1, d), const2),
                pl.BlockSpec((d, d), const2),
                kv_spec, kv_spec,
                pl.BlockSpec((d, d), const2),
                pl.BlockSpec((1, d), const2),
                pl.BlockSpec((N_EXPERTS, d), const2),
                pl.BlockSpec((N_EXPERTS, 1), const2)]
    args = [x, a, b, w_out[:S5_WIDTH], w_out[S5_WIDTH:], xg.reshape(1, d), wq, k, v, wo,
            fg.reshape(1, d), rw.T.astype(BF16), rb.reshape(N_EXPERTS, 1)]
    kern = functools.partial(_post_mix_kernel, n_batch=n_batch, kv_layer=kv_layer)
    aliases = {}
    if prev is not None:
        n_in = len(args)
        in_specs += [pl.BlockSpec(memory_space=pl.ANY)] * len(prev)
        args += list(prev)
        aliases = {n_in + o: o for o in range(len(prev))}
        kern = _drop_inputs(kern, n_in, len(prev))
    return pl.pallas_call(
        kern,
        grid=(n_blocks,),
        in_specs=in_specs,
        out_specs=[pl.BlockSpec((rows, d), row_map),
                   pl.BlockSpec((rows // SUBLANES,) + PIECE_TILE, lambda i: (first_block + i, 0, 0, 0)),
                   pl.BlockSpec((TOP_K, rows), lambda i: (0, first_block + i)),
                   pl.BlockSpec((TOP_K, rows), lambda i: (0, first_block + i)),
                   pl.BlockSpec((None, N_EXPERTS, LANES), lambda i: (i, 0, 0))],
        out_shape=[jax.ShapeDtypeStruct((N_TOK, d), F32),
                   jax.ShapeDtypeStruct((N_TOK // SUBLANES,) + PIECE_TILE, F32),
                   jax.ShapeDtypeStruct((TOP_K, N_TOK), jnp.int32),
                   jax.ShapeDtypeStruct((TOP_K, N_TOK), F32),
                   jax.ShapeDtypeStruct((n_blocks, N_EXPERTS, LANES), F32)],
        scratch_shapes=[pltpu.VMEM((rows, d), BF16)],
        input_output_aliases=aliases,
        compiler_params=_params("parallel"),
        name="post_mix",
    )(*args)


def _moe_kernel(be_ref, nu_ref, first_ref, slot_ref, nxt_ref, parts_ref,
                x_ref, wg_hbm, wu_hbm, wd_hbm, bg_ref, bu_ref, bd_ref, o_ref,
                wbuf_ref, wgb_ref, wub_ref, wdb_ref, sem, *, layer):
    i = pl.program_id(0)
    active = i < nu_ref[0]

    def copies(e, s):
        return [pltpu.make_async_copy(w.at[layer, e], wbuf_ref.at[s, j], sem.at[s, j])
                for j, w in enumerate((wg_hbm, wu_hbm, wd_hbm))]

    @pl.when(jnp.logical_and(active, first_ref[i] == 1))
    def _():
        s = slot_ref[i]

        @pl.when(i == 0)
        def _():
            for c in copies(be_ref[0], s):
                c.start()

        @pl.when(nxt_ref[i] >= 0)
        def _():
            for c in copies(nxt_ref[i], 1 - s):
                c.start()

        for c in copies(be_ref[i], s):
            c.wait()

        def cast(r, _):
            rs = pl.ds(pl.multiple_of(r * MOE_CAST_ROWS, MOE_CAST_ROWS), MOE_CAST_ROWS)
            wgb_ref[rs, :] = wbuf_ref[s, 0, rs, :].astype(BF16)
            wub_ref[rs, :] = wbuf_ref[s, 1, rs, :].astype(BF16)
            wdb_ref[rs, :] = wbuf_ref[s, 2, rs, :].astype(BF16)
            return 0
        lax.fori_loop(0, D_MODEL // MOE_CAST_ROWS, cast, 0)

    def experts(n_rows):
        words = lax.bitcast_convert_type(
            jnp.concatenate([x_ref[:n_rows // SUBLANES, c].reshape(n_rows, LANES) for c in range(PIECES_PER_ROW)],
                            axis=1), jnp.uint32)
        x = jnp.concatenate([lax.bitcast_convert_type(words << 16, F32),
                             lax.bitcast_convert_type(words & jnp.uint32(HIGH_HALF), F32)], axis=1).astype(BF16)
        g = jnp.minimum(_dot(x, wgb_ref[...]) + bg_ref[...], SWIGLU_LIMIT)
        u = jnp.clip(_dot(x, wub_ref[...]) + bu_ref[...], -SWIGLU_LIMIT, SWIGLU_LIMIT)
        a = g * jax.nn.sigmoid(SWIGLU_ALPHA * g) * (u + 1.0)
        o_ref[:n_rows, :] = _dot(a.astype(BF16), wdb_ref[...]) + bd_ref[...]

    for p in range(1, MOE_ROWS // MOE_PART + 1):
        pl.when(jnp.logical_and(active, parts_ref[i] == p))(functools.partial(experts, p * MOE_PART))


def _moe_call(layer, xs, sched, w_gate, b_gate, w_up, b_up, w_down, b_down):
    def row_map(i, be, nu, *_):
        return (jnp.minimum(i, nu[0] - 1), 0)

    def piece_map(i, be, nu, *_):
        return (jnp.minimum(i, nu[0] - 1), 0, 0, 0)

    def b_map(i, be, *_):
        return (layer, be[i], 0, 0)

    b_spec = pl.BlockSpec((None, None, 1, D_FF), b_map)
    w_spec = pl.BlockSpec(memory_space=pl.ANY)
    grid_spec = pltpu.PrefetchScalarGridSpec(
        num_scalar_prefetch=6,
        grid=(MOE_BLOCKS,),
        in_specs=[pl.BlockSpec((MOE_ROWS // SUBLANES,) + PIECE_TILE, piece_map),
                  w_spec, w_spec, w_spec, b_spec, b_spec, b_spec],
        out_specs=pl.BlockSpec((MOE_ROWS, D_MODEL), row_map),
        scratch_shapes=[pltpu.VMEM((2, 3, D_MODEL, D_FF), F32)]
        + [pltpu.VMEM((D_MODEL, D_FF), BF16)] * 3
        + [pltpu.SemaphoreType.DMA((2, 3))],
    )
    shape4 = (DEPTH, N_EXPERTS, 1, D_FF)
    return pl.pallas_call(
        functools.partial(_moe_kernel, layer=layer),
        grid_spec=grid_spec,
        out_shape=jax.ShapeDtypeStruct((MOE_BLOCKS * MOE_ROWS, D_MODEL), F32),
        compiler_params=pltpu.CompilerParams(dimension_semantics=("arbitrary",),
                                             vmem_limit_bytes=MOE_VMEM_LIMIT),
        name="moe_experts",
    )(*sched, xs, w_gate, w_up, w_down, b_gate.reshape(shape4), b_up.reshape(shape4),
      b_down.reshape(shape4))


DISPATCH_TOKENS = 48
DMA_PIECES = 64
DISPATCH_DMAS = DISPATCH_TOKENS * PIECES_PER_ROW // DMA_PIECES
DISPATCH_CHUNKS = N_TOK // SC_WORKERS // DISPATCH_TOKENS
IDX_ROWS = 16
SC_LANES = 16
MOE_ROWS_TOTAL = MOE_BLOCKS * MOE_ROWS
TILE_PIECES = PIECES_PER_ROW * SUBLANES


def _piece_index_rows(p_v, idx_v, row0, tok0, n_dma):
    lane = lax.iota(jnp.int32, SC_LANES)
    in_tile = lane & (SUBLANES - 1)
    piece = (lane >> 3) * SUBLANES
    per_vec = SC_LANES // SUBLANES
    tiles_per_dma = DMA_PIECES // TILE_PIECES
    for m in range(n_dma):
        for tr in range(tiles_per_dma):
            first = plsc.load_gather(p_v, [in_tile + (tok0 + (m * tiles_per_dma + tr) * SUBLANES)])
            for v in range(PIECES_PER_ROW // per_vec):
                idx_v[row0 + m, pl.ds(tr * TILE_PIECES + v * SC_LANES, SC_LANES)] = (
                    first + (piece + v * per_vec * SUBLANES))


def _dispatch(h4, dest_t):
    first_piece = ((dest_t // SUBLANES) * TILE_PIECES + dest_t % SUBLANES).reshape(-1)
    chunk_pieces = DISPATCH_TOKENS * PIECES_PER_ROW
    mesh = plsc.VectorSubcoreMesh(core_axis_name="c", subcore_axis_name="s")

    @functools.partial(
        pl.kernel, mesh=mesh,
        out_type=jax.ShapeDtypeStruct((MOE_ROWS_TOTAL * PIECES_PER_ROW, LANES), F32),
        scratch_types=[pltpu.VMEM((chunk_pieces, LANES), F32), pltpu.VMEM((IDX_ROWS, DMA_PIECES), jnp.int32),
                       pltpu.VMEM((TOP_K * DISPATCH_TOKENS,), jnp.int32), pltpu.SemaphoreType.DMA],
        compiler_params=pltpu.CompilerParams(needs_layout_passes=False),
    )
    def scatter_rows(h_hbm, p_hbm, o_hbm, rows_v, idx_v, p_v, sem):
        wid = lax.axis_index("s") * SC_CORES + lax.axis_index("c")

        @pl.loop(0, DISPATCH_CHUNKS)
        def _(j):
            q = wid * DISPATCH_CHUNKS + j
            pltpu.sync_copy(h_hbm.at[pl.ds(pl.multiple_of(q * chunk_pieces, SUBLANES), chunk_pieces)], rows_v)
            for k in range(TOP_K):
                pltpu.sync_copy(
                    p_hbm.at[pl.ds(pl.multiple_of(k * N_TOK + q * DISPATCH_TOKENS, SUBLANES), DISPATCH_TOKENS)],
                    p_v.at[pl.ds(k * DISPATCH_TOKENS, DISPATCH_TOKENS)])
            for k in range(TOP_K):
                _piece_index_rows(p_v, idx_v, k * DISPATCH_DMAS, k * DISPATCH_TOKENS, DISPATCH_DMAS)
            copies = [pltpu.async_copy(rows_v.at[pl.ds(m * DMA_PIECES, DMA_PIECES)],
                                       o_hbm.at[idx_v.at[k * DISPATCH_DMAS + m]], sem)
                      for k in range(TOP_K) for m in range(DISPATCH_DMAS)]
            for cp in copies:
                cp.wait()

    xs = scatter_rows(h4.reshape(N_TOK * PIECES_PER_ROW, LANES), first_piece)
    return xs.reshape((MOE_ROWS_TOTAL // SUBLANES,) + PIECE_TILE)


def _route(e_t, cnt_prompt, cnt_sample):
    cnt_sample = jnp.sum(cnt_sample[:, :, 0].reshape(SAMPLE_BLOCKS, -1, N_EXPERTS), axis=1)
    blk_cnt = jnp.concatenate([cnt_prompt[:, :, 0], cnt_sample]).astype(jnp.int32)
    def before(n):
        i = jnp.arange(n, dtype=jnp.int32)
        return i[None, :] < i[:, None]

    counts = jnp.sum(blk_cnt, axis=0)
    padded = ((counts + MOE_ROWS - 1) // MOE_ROWS) * MOE_ROWS
    pstart = jnp.sum(jnp.where(before(N_EXPERTS), padded[None, :], 0), axis=1)
    pend = pstart + padded
    base = pstart[None, :] + jnp.sum(jnp.where(before(blk_cnt.shape[0])[:, :, None], blk_cnt[None], 0), axis=1)
    dest_t = _dest(e_t, base.astype(F32)[:, :, None])
    n_used = jnp.sum(padded) // MOE_ROWS
    ids = jnp.arange(MOE_BLOCKS, dtype=jnp.int32)
    blk = jnp.minimum(ids, n_used - 1) * MOE_ROWS
    blk_exp = jnp.minimum(jnp.sum((pend[None, :] <= blk[:, None]).astype(jnp.int32), axis=1), N_EXPERTS - 1)
    eid = jnp.arange(N_EXPERTS, dtype=jnp.int32)
    of_blk = blk_exp[:, None] == eid[None, :]

    def lookup(table):
        return jnp.sum(jnp.where(of_blk, table[None, :], 0), axis=1)

    first = jnp.logical_and(ids < n_used, jnp.logical_or(ids == 0, blk_exp != jnp.roll(blk_exp, 1)))
    first = first.astype(jnp.int32)
    slot = jnp.sum(jnp.where(before(MOE_BLOCKS), first[None, :], 0), axis=1) % 2
    later = jnp.logical_and(eid[None, :] > eid[:, None], (counts > 0)[None, :])
    nxt_e = jnp.min(jnp.where(later, eid[None, :], N_EXPERTS), axis=1)
    nxt = lookup(jnp.where(nxt_e < N_EXPERTS, nxt_e, -1))
    real = jnp.clip(lookup(pstart + counts) - blk, 0, MOE_ROWS)
    parts = (real + MOE_PART - 1) // MOE_PART
    return dest_t, (blk_exp, n_used.reshape(1).astype(jnp.int32), first, slot.astype(jnp.int32),
                    nxt.astype(jnp.int32), parts.astype(jnp.int32))


def kernel(x_prompt, x_sample, mem_prompt, state_s5_re, state_s5_im, cache_mem_k, cache_mem_v, norm_mix_g, w_in, s5_lambda_re, s5_lambda_im, s5_b_re, s5_b_im, s5_c_re, s5_c_im, s5_d, s5_log_step, s5_w_glu, gm_ln_g, gm_ln_b, gm_w_s, gm_b_s, out_norm_g, w_out, xa_norm_g, mem_norm_g, w_xq, w_xk, w_xv, w_xo, ffn_norm_g, router_w, router_b, e_w_gate, e_b_gate, e_w_up, e_b_up, e_w_down, e_b_down, final_norm_g):
    xs = (x_prompt.reshape(N_PROMPT, D_MODEL), x_sample.reshape(N_SAMPLE, D_MODEL))
    mem = mem_prompt.reshape(BATCH * N_MEM, D_MODEL)
    outs = {k: [] for k in ('s5r_p', 's5i_p', 's5r_s', 's5i_s', 'gmv_s')}
    mem_kv = None
    for l in range(DEPTH):
        unified = len(xs) == 1

        def sample_first(rows):
            return N_PROMPT // rows if unified else 0

        in_proj = (norm_mix_g[l], w_in[l].astype(BF16))
        sw = _s5_weights(s5_lambda_re[l], s5_lambda_im[l], s5_b_re[l], s5_b_im[l], s5_c_re[l], s5_c_im[l],
                         s5_d[l], s5_log_step[l], S5_SEG_PROMPT)
        w_glu = s5_w_glu[l].astype(BF16)
        og = out_norm_g[l]
        ms5, er_p, ei_p = _s5_call(xs[0], 0, *in_proj, sw, w_glu, og[:S5_WIDTH], S5_SEG_PROMPT, True)
        h0 = (state_s5_re[l].reshape(DEC_BATCH // SUBLANES, SUBLANES, S5_COLS),
              state_s5_im[l].reshape(DEC_BATCH // SUBLANES, SUBLANES, S5_COLS))
        ms5, er_s, ei_s = _s5_call(xs[-1], sample_first(DEC_SEQ * SUBLANES), *in_proj, sw, w_glu, og[:S5_WIDTH],
                                   DEC_SEQ, False, h0=h0, prev=ms5)
        outs['s5r_p'].append(er_p[:, SUBLANES - 1].reshape(BATCH, S5_GROUPS, S5_STATE))
        outs['s5i_p'].append(ei_p[:, SUBLANES - 1].reshape(BATCH, S5_GROUPS, S5_STATE))
        outs['s5r_s'].append(er_s.reshape(DEC_BATCH, S5_GROUPS, S5_STATE))
        outs['s5i_s'].append(ei_s.reshape(DEC_BATCH, S5_GROUPS, S5_STATE))

        (mgm,) = _gmlp_call(xs[0], 0, *in_proj, gm_ln_g[l], gm_ln_b[l], gm_w_s[l], gm_b_s[l], og[S5_WIDTH:],
                            GM_ROWS_PROMPT, GM_CHUNK, 0, N_PROMPT // GM_ROWS_PROMPT, False)
        mgm, vn = _gmlp_call(xs[-1], sample_first(N_SAMPLE), *in_proj, gm_ln_g[l], gm_ln_b[l], gm_w_s[l],
                             gm_b_s[l], og[S5_WIDTH:], N_SAMPLE, DEC_SEQ, N_PROMPT // N_SAMPLE, 1, True, prev=mgm)
        outs['gmv_s'].append(vn.reshape(DEC_BATCH, DEC_SEQ, GM_HEADS, GM_HEAD_DIM))
        *mem_kv, mk, mv = _mem_kv(mem, mem_norm_g[l], w_xk[l].astype(BF16), w_xv[l].astype(BF16), l, prev=mem_kv)
        shared = (w_out[l].astype(BF16), xa_norm_g[l], w_xq[l].astype(BF16))
        tail = (w_xo[l].astype(BF16), ffn_norm_g[l], router_w[l], router_b[l])
        rows_s = XA_SAMPLE_BATCHES * DEC_SEQ
        post_p = _post_mix(xs[0], 0, ms5, mgm, *shared, mk, mv, *tail,
                           XA_ROWS_PROMPT, 1, 0, N_PROMPT // XA_ROWS_PROMPT, SEQ // XA_ROWS_PROMPT)
        post_s = _post_mix(xs[-1], sample_first(rows_s), ms5, mgm, *shared,
                           cache_mem_k, cache_mem_v, *tail,
                           rows_s, XA_SAMPLE_BATCHES, N_PROMPT // rows_s, N_SAMPLE // rows_s, 1,
                           prev=post_p[:4], kv_layer=l)
        x, h, e_t, gates_t = post_s[:4]

        dest_t, sched = _route(e_t, post_p[4], post_s[4])
        ys = _moe_call(l, _dispatch(h, dest_t), sched, e_w_gate, e_b_gate, e_w_up, e_b_up, e_w_down, e_b_down)
        x = _combine(x, ys[dest_t.reshape(-1)], gates_t.T, final_norm_g if l == DEPTH - 1 else None)
        xs = (x,)

    y_prompt = x[0].reshape(BATCH, SEQ, D_MODEL)
    y_sample = x[1].reshape(DEC_BATCH, DEC_SEQ, D_MODEL)
    st = jnp.stack
    return (y_prompt, y_sample, st(outs['s5r_p']), st(outs['s5i_p']), mem_kv[0], mem_kv[1],
            st(outs['s5r_s']), st(outs['s5i_s']), st(outs['gmv_s']))
```

```python
import functools

import jax
import jax.numpy as jnp
from jax import lax
from jax.experimental import pallas as pl
from jax.experimental.pallas import tpu as pltpu
from jax.experimental.pallas import tpu_sc as plsc

F32 = jnp.float32
BF16 = jnp.bfloat16

D_MODEL = 1024
BATCH = 4
SEQ = 4096
DEPTH = 2
DEC_BATCH = 16
DEC_SEQ = 32
S5_GROUPS = 32
S5_GROUP_CH = 16
S5_STATE = 64
S5_WIDTH = S5_GROUPS * S5_GROUP_CH
S5_COLS = S5_GROUPS * S5_STATE
GM_HEADS = 4
GM_HEAD_DIM = 128
GM_WIDTH = GM_HEADS * GM_HEAD_DIM
GM_CHUNK = 128
IN_PROJ = S5_WIDTH + 2 * GM_WIDTH
N_MEM = 256
XA_HEADS = 4
XA_HEAD_DIM = D_MODEL // XA_HEADS
N_EXPERTS = 32
TOP_K = 4
D_FF = D_MODEL
SWIGLU_LIMIT = 7.0
SWIGLU_ALPHA = 1.702
EPS = 1e-6
EIG_CLIP = -1e-4

N_PROMPT = BATCH * SEQ
N_SAMPLE = DEC_BATCH * DEC_SEQ
N_TOK = N_PROMPT + N_SAMPLE

SUBLANES = 8
LANES = 128
MXU_DIM = 256
VMEM_LIMIT = 48 * 1024 * 1024
MOE_VMEM_LIMIT = 60 * 1024 * 1024
SC_CORES = 2
SC_SUBCORES = 16
SC_WORKERS = SC_CORES * SC_SUBCORES

PACK_WIDTH = D_MODEL // 2
PIECES_PER_ROW = PACK_WIDTH // LANES
PIECE_TILE = (PIECES_PER_ROW, SUBLANES, LANES)
HIGH_HALF = 0xFFFF0000

ROW_BLOCK = 512
S5_SEG_PROMPT = 64
S5_COL_BLOCK = 512
GM_ROWS_PROMPT = 256
XA_ROWS_PROMPT = 512
XA_SAMPLE_BATCHES = 4
MOE_ROWS = 1024
MOE_PART = 256
MOE_BLOCKS = -(-N_TOK * TOP_K // MOE_ROWS) + N_EXPERTS
MOE_CAST_ROWS = 128


def _params(*sem):
    return pltpu.CompilerParams(dimension_semantics=sem, vmem_limit_bytes=VMEM_LIMIT)


def _rms(x, g):
    return x * lax.rsqrt(jnp.mean(x * x, axis=-1, keepdims=True) + EPS) * g


def _dot(a, b):
    return jnp.dot(a, b, preferred_element_type=F32)


PROMPT_BLOCKS = N_PROMPT // ROW_BLOCK
SAMPLE_BLOCKS = N_SAMPLE // ROW_BLOCK


def _token_sources(xs):
    if len(xs) == 1:
        return [pl.BlockSpec((ROW_BLOCK, D_MODEL), lambda i: (i, 0))]
    return [pl.BlockSpec((ROW_BLOCK, D_MODEL), lambda i: (jnp.minimum(i, PROMPT_BLOCKS - 1), 0)),
            pl.BlockSpec((ROW_BLOCK, D_MODEL), lambda i: (jnp.maximum(i - PROMPT_BLOCKS, 0), 0))]


def _for_group(refs, body):
    if len(refs) == 1:
        body(refs[0])
        return
    i = pl.program_id(0)
    pl.when(i < PROMPT_BLOCKS)(lambda: body(refs[0]))
    pl.when(i >= PROMPT_BLOCKS)(lambda: body(refs[1]))


def _mem_kv_kernel(m_ref, g_ref, wk_ref, wv_ref, k5_ref, v5_ref, kb_ref, vb_ref):
    m = _rms(m_ref[...], g_ref[...]).astype(BF16)
    for w_ref, o5_ref, ob_ref in ((wk_ref, k5_ref, kb_ref), (wv_ref, v5_ref, vb_ref)):
        kv = _dot(m, w_ref[...])
        ob_ref[...] = kv.astype(BF16)
        for h in range(XA_HEADS):
            o5_ref[:, h, :] = kv[:, h * XA_HEAD_DIM:(h + 1) * XA_HEAD_DIM]


def _mem_kv(mem, g, wk, wv, layer, prev=None):
    d = D_MODEL
    const2 = lambda b: (0, 0)
    in_specs = [pl.BlockSpec((N_MEM, d), lambda b: (b, 0)),
                pl.BlockSpec((1, d), const2),
                pl.BlockSpec((d, d), const2),
                pl.BlockSpec((d, d), const2)]
    args = [mem, g.reshape(1, d), wk, wv]
    kern = _mem_kv_kernel
    aliases = {}
    if prev is not None:
        in_specs += [pl.BlockSpec(memory_space=pl.ANY)] * 2
        args += list(prev)
        aliases = {4: 0, 5: 1}
        kern = _drop_inputs(kern, 4, 2)
    out5 = pl.BlockSpec((None, None, N_MEM, XA_HEADS, XA_HEAD_DIM), lambda b: (layer, b, 0, 0, 0))
    outb = pl.BlockSpec((None, N_MEM, d), lambda b: (b, 0, 0))
    shape5 = jax.ShapeDtypeStruct((DEPTH, BATCH, N_MEM, XA_HEADS, XA_HEAD_DIM), F32)
    shapeb = jax.ShapeDtypeStruct((BATCH, N_MEM, d), BF16)
    return pl.pallas_call(
        kern,
        grid=(BATCH,),
        in_specs=in_specs,
        out_specs=[out5, out5, outb, outb],
        out_shape=[shape5, shape5, shapeb, shapeb],
        input_output_aliases=aliases,
        compiler_params=_params("parallel"),
        name="mem_kv",
    )(*args)


def _route_rows(x, g_ref, wt_ref, b_ref, h_ref, e_ref, gate_ref, cnt_ref):
    h = _rms(x, g_ref[...]).astype(BF16)
    bits = lax.bitcast_convert_type(h.astype(F32), jnp.uint32)
    words = lax.bitcast_convert_type(bits[:, PACK_WIDTH:] | (bits[:, :PACK_WIDTH] >> 16), F32)
    for c in range(PIECES_PER_ROW):
        h_ref[:, c] = words[:, c * LANES:(c + 1) * LANES].reshape(h.shape[0] // SUBLANES, SUBLANES, LANES)
    logits = lax.dot_general(wt_ref[...], h, (((1,), (1,)), ((), ())),
                             preferred_element_type=F32) + b_ref[...]
    sub = lax.broadcasted_iota(jnp.int32, logits.shape, 0)
    work = logits
    chosen = jnp.zeros(logits.shape, F32)
    vals, idxs = [], []
    for _ in range(TOP_K):
        m = jnp.max(work, axis=0, keepdims=True)
        idx = jnp.min(jnp.where(work == m, sub, N_EXPERTS), axis=0, keepdims=True)
        sel = sub == idx
        vals.append(m)
        idxs.append(idx)
        work = jnp.where(sel, -jnp.inf, work)
        chosen = jnp.where(sel, 1.0, chosen)
    ex = [jnp.exp(v - vals[0]) for v in vals]
    den = ex[0] + ex[1] + ex[2] + ex[3]
    gate_ref[...] = jnp.concatenate([e / den for e in ex], axis=0)
    e_ref[...] = jnp.concatenate(idxs, axis=0)
    cnt_ref[...] = jnp.broadcast_to(jnp.sum(chosen, axis=1, keepdims=True), (N_EXPERTS, LANES))


def _dest_kernel(e_ref, base_ref, d_ref):
    e = e_ref[...]
    rows = e.shape[1]
    sub = lax.broadcasted_iota(jnp.int32, (N_EXPERTS, rows), 0)
    hot = [sub == e[k:k + 1, :] for k in range(TOP_K)]
    chosen = sum(jnp.where(hk, 1.0, 0.0) for hk in hot)
    earlier = (lax.broadcasted_iota(jnp.int32, (rows, rows), 0)
               < lax.broadcasted_iota(jnp.int32, (rows, rows), 1))
    pos = base_ref[...] + _dot(chosen.astype(BF16), jnp.where(earlier, 1.0, 0.0).astype(BF16))
    d_ref[...] = jnp.concatenate(
        [jnp.sum(jnp.where(hk, pos, 0.0), axis=0, keepdims=True) for hk in hot], axis=0).astype(jnp.int32)


def _dest(e_t, base):
    n = e_t.shape[1]
    rows = ROW_BLOCK
    return pl.pallas_call(
        _dest_kernel,
        grid=(n // rows,),
        in_specs=[pl.BlockSpec((TOP_K, rows), lambda i: (0, i)),
                  pl.BlockSpec((None, N_EXPERTS, 1), lambda i: (i, 0, 0))],
        out_specs=pl.BlockSpec((TOP_K, rows), lambda i: (0, i)),
        out_shape=jax.ShapeDtypeStruct((TOP_K, n), jnp.int32),
        compiler_params=_params("parallel"),
        name="moe_dest",
    )(e_t, base)


def _combine_kernel(*refs, final):
    if final:
        x_ref, g0, g1, g2, g3, gate_ref, fg_ref, op_ref, os_ref = refs
    else:
        x_ref, g0, g1, g2, g3, gate_ref, o_ref = refs
    rows = x_ref.shape[0]
    slabs = []
    for c in range(OUT_PIECES):
        acc = x_ref[:, c * LANES:(c + 1) * LANES]
        for k, g in enumerate((g0, g1, g2, g3)):
            acc = acc + gate_ref[:, k:k + 1] * g[:, c].reshape(rows, LANES)
        slabs.append(acc)
    acc = jnp.concatenate(slabs, axis=1)
    if final:
        def store(o_ref):
            o_ref[...] = _rms(acc, fg_ref[...])
        _for_group((op_ref, os_ref), store)
    else:
        o_ref[...] = acc


def _combine(x, picked, gates, final_g=None):
    n, d = x.shape
    rows = ROW_BLOCK
    final = final_g is not None
    in_specs = [pl.BlockSpec((rows, d), lambda i: (i, 0))]
    in_specs += [pl.BlockSpec((None, rows // SUBLANES) + OUT_TILE, functools.partial(lambda i, k: (k, i, 0, 0, 0), k=k))
                 for k in range(TOP_K)]
    in_specs.append(pl.BlockSpec((rows, TOP_K), lambda i: (i, 0)))
    args = [x, picked, picked, picked, picked, gates]
    out_specs = pl.BlockSpec((rows, d), lambda i: (i, 0))
    out_shape = jax.ShapeDtypeStruct((n, d), F32)
    if final:
        in_specs.append(pl.BlockSpec((1, d), lambda i: (0, 0)))
        args.append(final_g.reshape(1, d))
        out_specs = _token_sources((None, None))
        out_shape = [jax.ShapeDtypeStruct((N_PROMPT, d), F32), jax.ShapeDtypeStruct((N_SAMPLE, d), F32)]
    return pl.pallas_call(
        functools.partial(_combine_kernel, final=final),
        grid=(n // rows,),
        in_specs=in_specs,
        out_specs=out_specs,
        out_shape=out_shape,
        compiler_params=_params("arbitrary" if final else "parallel"),
        name="moe_combine",
    )(*args)


def _cmul_add(ar, ai, hr, hi, br, bi):
    return ar * hr - ai * hi + br, ar * hi + ai * hr + bi


def _s5_kernel(*refs, seg, chained):
    if chained:
        (x_ref, ng_ref, win_ref, wbr_ref, wbi_ref, wcr_ref, wci_ref, ar_ref, ai_ref, pr_ref, pi_ref, d_ref,
         wglu_ref, og_ref, o_ref, er_ref, ei_ref,
         up_ref, hr_ref, hi_ref, sr_ref, si_ref, op_ref, cr_ref, ci_ref) = refs
    else:
        (x_ref, ng_ref, win_ref, wbr_ref, wbi_ref, wcr_ref, wci_ref, ar_ref, ai_ref, pr_ref, pi_ref, d_ref,
         wglu_ref, og_ref, h0r_ref, h0i_ref, o_ref, er_ref, ei_ref,
         up_ref, hr_ref, hi_ref, sr_ref, si_ref, op_ref) = refs
    rows = seg * SUBLANES
    half = S5_COLS // 2

    u = _dot(_rms(x_ref[...], ng_ref[...]).astype(BF16), win_ref[...])
    n_lane_blk = S5_WIDTH // LANES
    for j in range(SUBLANES):
        for i0 in range(0, seg, SUBLANES):
            for c in range(n_lane_blk):
                up_ref[c, pl.ds(i0 * SUBLANES + j, SUBLANES, stride=SUBLANES), :] = (
                    u[j * seg + i0:j * seg + i0 + SUBLANES, c * LANES:(c + 1) * LANES])

    up = jnp.concatenate([up_ref[c] for c in range(n_lane_blk)], axis=1)
    ub = up.astype(BF16)
    for k in range(2):
        uk = ub[:, k * MXU_DIM:(k + 1) * MXU_DIM]
        hr_ref[:, k * half:(k + 1) * half] = _dot(uk, wbr_ref[k])
        hi_ref[:, k * half:(k + 1) * half] = _dot(uk, wbi_ref[k])

    for cb in range(S5_COLS // S5_COL_BLOCK):
        cols = slice(cb * S5_COL_BLOCK, (cb + 1) * S5_COL_BLOCK)
        ar = jnp.broadcast_to(ar_ref[:, cols], (SUBLANES, S5_COL_BLOCK))
        ai = jnp.broadcast_to(ai_ref[:, cols], (SUBLANES, S5_COL_BLOCK))

        def step(i, carry, cols=cols, ar=ar, ai=ai):
            r = pl.multiple_of(i * SUBLANES, SUBLANES)
            h_r, h_i = _cmul_add(ar, ai, carry[0], carry[1],
                                 hr_ref[pl.ds(r, SUBLANES), cols], hi_ref[pl.ds(r, SUBLANES), cols])
            hr_ref[pl.ds(r, SUBLANES), cols] = h_r
            hi_ref[pl.ds(r, SUBLANES), cols] = h_i
            return h_r, h_i

        zero = jnp.zeros((SUBLANES, S5_COL_BLOCK), F32)
        lax.fori_loop(0, seg, step, (zero, zero), unroll=2)

    end_r = hr_ref[rows - SUBLANES:rows, :]
    end_i = hi_ref[rows - SUBLANES:rows, :]
    as_r = pr_ref[rows - 1:rows, :]
    as_i = pi_ref[rows - 1:rows, :]
    if chained:
        @pl.when(pl.program_id(1) == 0)
        def _():
            cr_ref[...] = jnp.zeros_like(cr_ref)
            ci_ref[...] = jnp.zeros_like(ci_ref)

        cur_r, cur_i = cr_ref[...], ci_ref[...]
        for j in range(SUBLANES):
            sr_ref[j:j + 1, :] = cur_r
            si_ref[j:j + 1, :] = cur_i
            cur_r, cur_i = _cmul_add(as_r, as_i, cur_r, cur_i, end_r[j:j + 1], end_i[j:j + 1])
        cr_ref[...] = cur_r
        ci_ref[...] = cur_i
    else:
        sr_ref[...] = h0r_ref[...]
        si_ref[...] = h0i_ref[...]
    t_r, t_i = _cmul_add(as_r, as_i, sr_ref[...], si_ref[...], end_r, end_i)
    er_ref[...] = t_r
    ei_ref[...] = t_i

    slabs_per_window = MXU_DIM // (MXU_DIM // S5_STATE * S5_GROUP_CH)
    ys = []
    for w in range(S5_WIDTH // MXU_DIM):
        acc = d_ref[:, w * MXU_DIM:(w + 1) * MXU_DIM] * jnp.concatenate(
            [up_ref[c] for c in range(w * MXU_DIM // LANES, (w + 1) * MXU_DIM // LANES)], axis=1)
        for s in range(slabs_per_window):
            cols = slice((w * slabs_per_window + s) * MXU_DIM, (w * slabs_per_window + s + 1) * MXU_DIM)
            shape3 = (seg, SUBLANES, MXU_DIM)
            h_r, h_i = _cmul_add(pr_ref[:, cols].reshape(shape3), pi_ref[:, cols].reshape(shape3),
                                 sr_ref[:, cols][None], si_ref[:, cols][None],
                                 hr_ref[:, cols].reshape(shape3), hi_ref[:, cols].reshape(shape3))
            acc += _dot(h_r.reshape(rows, MXU_DIM).astype(BF16), wcr_ref[w * slabs_per_window + s])
            acc += _dot(h_i.reshape(rows, MXU_DIM).astype(BF16), wci_ref[w * slabs_per_window + s])
        ys.append(jax.nn.gelu(acc).astype(BF16))

    z = _dot(jnp.concatenate(ys, axis=1), wglu_ref[...])
    o = _rms(z[:, :S5_WIDTH] * jax.nn.sigmoid(z[:, S5_WIDTH:]), og_ref[...])
    for c in range(n_lane_blk):
        op_ref[c] = o[:, c * LANES:(c + 1) * LANES]

    for j in range(SUBLANES):
        for i0 in range(0, seg, 2 * SUBLANES):
            for c in range(n_lane_blk):
                o_ref[j * seg + i0:j * seg + i0 + 2 * SUBLANES, c * LANES:(c + 1) * LANES] = (
                    op_ref[c, pl.ds(i0 * SUBLANES + j, 2 * SUBLANES, stride=SUBLANES), :].astype(BF16))


def _s5_weights(lam_re, lam_im, b_re, b_im, c_re, c_im, d, log_step, seg):
    step = jnp.exp(log_step)[:, None]
    lr = jnp.minimum(lam_re, EIG_CLIP)
    li = lam_im
    mag = jnp.exp(lr * step)
    ar = mag * jnp.cos(li * step)
    ai = mag * jnp.sin(li * step)
    den = lr * lr + li * li
    fr = ((ar - 1.0) * lr + ai * li) / den
    fi = (ai * lr - (ar - 1.0) * li) / den
    bbar_r = fr[..., None] * b_re - fi[..., None] * b_im
    bbar_i = fr[..., None] * b_im + fi[..., None] * b_re
    t = jnp.arange(1, seg + 1, dtype=F32)[:, None, None]
    pm = jnp.exp(lr * step * t)
    p_r = jnp.repeat((pm * jnp.cos(li * step * t)).reshape(seg, S5_COLS), SUBLANES, axis=0)
    p_i = jnp.repeat((pm * jnp.sin(li * step * t)).reshape(seg, S5_COLS), SUBLANES, axis=0)

    gk = MXU_DIM // S5_GROUP_CH

    def in_map(bbar):
        bb = bbar.reshape(S5_GROUPS // gk, gk, S5_STATE, S5_GROUP_CH)
        w = jnp.einsum('kgnh,gG->kghGn', bb, jnp.eye(gk, dtype=F32))
        return w.reshape(S5_GROUPS // gk, MXU_DIM, gk * S5_STATE).astype(BF16)

    gs = MXU_DIM // S5_STATE
    n_slab = S5_COLS // MXU_DIM
    per_win = MXU_DIM // (gs * S5_GROUP_CH)

    def out_map(c):
        cc = c.reshape(n_slab // per_win, per_win, gs, S5_GROUP_CH, S5_STATE)
        w = jnp.einsum('wsghn,sS,gG->wsgnSGh', cc, jnp.eye(per_win, dtype=F32), jnp.eye(gs, dtype=F32))
        return w.reshape(n_slab, MXU_DIM, MXU_DIM).astype(BF16)

    return dict(wbr=in_map(bbar_r), wbi=in_map(bbar_i), wcr=out_map(c_re), wci=out_map(-c_im),
                ar=ar.reshape(1, S5_COLS), ai=ai.reshape(1, S5_COLS), pr=p_r, pi=p_i,
                d=d.reshape(1, S5_WIDTH))


def _s5_call(x, x_first, norm_g, w_in, sw, w_glu, og, seg, chained, h0=None, prev=None):
    rows = seg * SUBLANES
    const2 = lambda *_: (0, 0)
    const3 = lambda *_: (0, 0, 0)
    if chained:
        n_chunk = SEQ // rows
        grid = (BATCH, n_chunk)
        row_map = lambda b, c: (b * n_chunk + c, 0)
        x_map = row_map
        end_map = lambda b, c: (b, 0, 0)
        n_end = BATCH
        sem = ("parallel", "arbitrary")
    else:
        grid = (DEC_BATCH // SUBLANES,)
        row_map = lambda k: (N_PROMPT // rows + k, 0)
        x_map = lambda k: (x_first + k, 0)
        end_map = lambda k: (k, 0, 0)
        n_end = DEC_BATCH // SUBLANES
        sem = ("parallel",)
    half = S5_COLS // 2
    in_specs = [pl.BlockSpec((rows, D_MODEL), x_map),
                pl.BlockSpec((1, D_MODEL), const2),
                pl.BlockSpec((D_MODEL, S5_WIDTH), const2),
                pl.BlockSpec((2, MXU_DIM, half), const3),
                pl.BlockSpec((2, MXU_DIM, half), const3),
                pl.BlockSpec((S5_COLS // MXU_DIM, MXU_DIM, MXU_DIM), const3),
                pl.BlockSpec((S5_COLS // MXU_DIM, MXU_DIM, MXU_DIM), const3),
                pl.BlockSpec((1, S5_COLS), const2),
                pl.BlockSpec((1, S5_COLS), const2),
                pl.BlockSpec((rows, S5_COLS), const2),
                pl.BlockSpec((rows, S5_COLS), const2),
                pl.BlockSpec((1, S5_WIDTH), const2),
                pl.BlockSpec((S5_WIDTH, 2 * S5_WIDTH), const2),
                pl.BlockSpec((1, S5_WIDTH), const2)]
    args = [x, norm_g.reshape(1, D_MODEL), w_in[:, :S5_WIDTH], sw['wbr'], sw['wbi'], sw['wcr'], sw['wci'], sw['ar'], sw['ai'],
            sw['pr'][:rows], sw['pi'][:rows], sw['d'], w_glu, og.reshape(1, S5_WIDTH)]
    scratch = [pltpu.VMEM((S5_WIDTH // LANES, rows, LANES), F32),
               pltpu.VMEM((rows, S5_COLS), F32), pltpu.VMEM((rows, S5_COLS), F32),
               pltpu.VMEM((SUBLANES, S5_COLS), F32), pltpu.VMEM((SUBLANES, S5_COLS), F32),
               pltpu.VMEM((S5_WIDTH // LANES, rows, LANES), F32)]
    aliases = {}
    if chained:
        scratch += [pltpu.VMEM((1, S5_COLS), F32), pltpu.VMEM((1, S5_COLS), F32)]
    else:
        in_specs += [pl.BlockSpec((None, SUBLANES, S5_COLS), end_map),
                     pl.BlockSpec((None, SUBLANES, S5_COLS), end_map)]
        args += [h0[0], h0[1]]
    if prev is not None:
        in_specs.append(pl.BlockSpec(memory_space=pl.ANY))
        args.append(prev)
        aliases = {len(args) - 1: 0}
    kern = functools.partial(_s5_kernel, seg=seg, chained=chained)
    if prev is not None:
        kern = _drop_inputs(kern, len(args) - 1, 1)
    return pl.pallas_call(
        kern,
        grid=grid,
        in_specs=in_specs,
        out_specs=[pl.BlockSpec((rows, S5_WIDTH), row_map),
                   pl.BlockSpec((None, SUBLANES, S5_COLS), end_map),
                   pl.BlockSpec((None, SUBLANES, S5_COLS), end_map)],
        out_shape=[jax.ShapeDtypeStruct((N_TOK, S5_WIDTH), BF16),
                   jax.ShapeDtypeStruct((n_end, SUBLANES, S5_COLS), F32),
                   jax.ShapeDtypeStruct((n_end, SUBLANES, S5_COLS), F32)],
        scratch_shapes=scratch,
        input_output_aliases=aliases,
        compiler_params=_params(*sem),
        name="s5_chained" if chained else "s5_streams",
    )(*args)


def _drop_inputs(kern, start, n):
    def wrapped(*refs):
        return kern(*refs[:start], *refs[start + n:])
    return wrapped


def _gmlp_kernel(*refs, chunk, emit_vn):
    if emit_vn:
        x_ref, ng_ref, win_ref, lng_ref, lnb_ref, ws_ref, bst_ref, og_ref, o_ref, vn_ref, acc_ref = refs
    else:
        x_ref, ng_ref, win_ref, lng_ref, lnb_ref, ws_ref, bst_ref, og_ref, o_ref, acc_ref = refs
    rows = x_ref.shape[0]
    uv = _dot(_rms(x_ref[...], ng_ref[...]).astype(BF16), win_ref[...])
    u = jax.nn.gelu(uv[:, :GM_WIDTH])
    v = jax.nn.gelu(uv[:, GM_WIDTH:])
    tri = (lax.broadcasted_iota(jnp.int32, (chunk, chunk), 0)
           >= lax.broadcasted_iota(jnp.int32, (chunk, chunk), 1))
    for h in range(GM_HEADS):
        cols = slice(h * GM_HEAD_DIM, (h + 1) * GM_HEAD_DIM)
        vh = v[:, cols]
        mu = jnp.mean(vh, axis=-1, keepdims=True)
        cen = vh - mu
        var = jnp.mean(cen * cen, axis=-1, keepdims=True)
        vn = cen * lax.rsqrt(var + EPS) * lng_ref[:, cols] + lnb_ref[:, cols]
        if emit_vn:
            vn_ref[:, cols] = vn
        w = jnp.where(tri, ws_ref[h], 0.0).astype(BF16)
        vb = vn.astype(BF16)
        bias = bst_ref[:, h:h + 1]
        for c in range(rows // chunk):
            rs = slice(c * chunk, (c + 1) * chunk)
            acc_ref[rs, cols] = u[rs, cols] * (_dot(w, vb[rs]) + bias)
    o_ref[...] = _rms(acc_ref[...], og_ref[...]).astype(BF16)


def _gmlp_call(x, x_first, norm_g, w_in, ln_g, ln_b, w_s, b_s, og, rows, chunk, first_block, n_blocks, emit_vn,
               prev=None):
    const2 = lambda i: (0, 0)
    in_specs = [pl.BlockSpec((rows, D_MODEL), lambda i: (x_first + i, 0)),
                pl.BlockSpec((1, D_MODEL), const2),
                pl.BlockSpec((D_MODEL, 2 * GM_WIDTH), const2),
                pl.BlockSpec((1, GM_WIDTH), const2),
                pl.BlockSpec((1, GM_WIDTH), const2),
                pl.BlockSpec((GM_HEADS, chunk, chunk), lambda i: (0, 0, 0)),
                pl.BlockSpec((chunk, GM_HEADS), const2),
                pl.BlockSpec((1, GM_WIDTH), const2)]
    args = [x, norm_g.reshape(1, D_MODEL), w_in[:, S5_WIDTH:], ln_g.reshape(1, GM_WIDTH), ln_b.reshape(1, GM_WIDTH),
            w_s[:, :chunk, :chunk], b_s[:, :chunk].T, og.reshape(1, GM_WIDTH)]
    out_specs = [pl.BlockSpec((rows, GM_WIDTH), lambda i: (first_block + i, 0))]
    out_shape = [jax.ShapeDtypeStruct((N_TOK, GM_WIDTH), BF16)]
    if emit_vn:
        out_specs.append(pl.BlockSpec((rows, GM_WIDTH), lambda i: (i, 0)))
        out_shape.append(jax.ShapeDtypeStruct((n_blocks * rows, GM_WIDTH), F32))
    kern = functools.partial(_gmlp_kernel, chunk=chunk, emit_vn=emit_vn)
    aliases = {}
    if prev is not None:
        in_specs.append(pl.BlockSpec(memory_space=pl.ANY))
        args.append(prev)
        aliases = {len(args) - 1: 0}
        kern = _drop_inputs(kern, len(args) - 1, 1)
    return pl.pallas_call(
        kern,
        grid=(n_blocks,),
        in_specs=in_specs,
        out_specs=out_specs,
        out_shape=out_shape,
        scratch_shapes=[pltpu.VMEM((rows, GM_WIDTH), F32)],
        input_output_aliases=aliases,
        compiler_params=_params("parallel"),
        name="gmlp_vn" if emit_vn else "gmlp",
    )(*args)


def _post_mix_kernel(x_ref, a_ref, b_ref, wa_ref, wb_ref, xg_ref, wq_ref, k_ref, v_ref, wo_ref,
                     fg_ref, wt_ref, rb_ref, x2_ref, h_ref, e_ref, gate_ref, cnt_ref, att_ref, *, n_batch, kv_layer):
    rows = x_ref.shape[0]
    tb = rows // n_batch
    x = x_ref[...] + _dot(a_ref[...], wa_ref[...]) + _dot(b_ref[...], wb_ref[...])
    q = _dot(_rms(x, xg_ref[...]).astype(BF16), wq_ref[...]).astype(BF16)
    scale = XA_HEAD_DIM ** -0.5
    for b in range(n_batch):
        rs = slice(b * tb, (b + 1) * tb)
        for h in range(XA_HEADS):
            cols = slice(h * XA_HEAD_DIM, (h + 1) * XA_HEAD_DIM)
            if kv_layer is None:
                k_h, v_h = k_ref[b, :, cols], v_ref[b, :, cols]
            else:
                k_h, v_h = k_ref[b, :, h, :].astype(BF16), v_ref[b, :, h, :].astype(BF16)
            s = lax.dot_general(q[rs, cols], k_h, (((1,), (1,)), ((), ())),
                                preferred_element_type=F32) * scale
            e = jnp.exp(s - jnp.max(s, axis=-1, keepdims=True))
            p = e / jnp.sum(e, axis=-1, keepdims=True)
            att_ref[rs, cols] = _dot(p.astype(BF16), v_h).astype(BF16)
    x = x + _dot(att_ref[...], wo_ref[...])
    x2_ref[...] = x
    _route_rows(x, fg_ref, wt_ref, rb_ref, h_ref, e_ref, gate_ref, cnt_ref)


def _post_mix(x, x_first, a, b, w_out, xg, wq, k, v, wo, fg, rw, rb,
              rows, n_batch, first_block, n_blocks, blocks_per_kv, prev=None, kv_layer=None):
    const2 = lambda i: (0, 0)
    if kv_layer is None:
        kv_spec = pl.BlockSpec((n_batch, N_MEM, D_MODEL), lambda i: (i // blocks_per_kv, 0, 0))
    else:
        kv_spec = pl.BlockSpec((None, n_batch, N_MEM, XA_HEADS, XA_HEAD_DIM),
                               lambda i: (kv_layer, i // blocks_per_kv, 0, 0, 0))
    row_map = lambda i: (first_block + i, 0)
    d = D_MODEL
    in_specs = [pl.BlockSpec((rows, d), lambda i: (x_first + i, 0)),
                pl.BlockSpec((rows, S5_WIDTH), row_map),
                pl.BlockSpec((rows, GM_WIDTH), row_map),
                pl.BlockSpec((S5_WIDTH, d), const2),
                pl.BlockSpec((GM_WIDTH, d), const2),
                pl.BlockSpec((1, d), const2),
                pl.BlockSpec((d, d), const2),
                kv_spec, kv_spec,
                pl.BlockSpec((d, d), const2),
                pl.BlockSpec((1, d), const2),
                pl.BlockSpec((N_EXPERTS, d), const2),
                pl.BlockSpec((N_EXPERTS, 1), const2)]
    args = [x, a, b, w_out[:S5_WIDTH], w_out[S5_WIDTH:], xg.reshape(1, d), wq, k, v, wo,
            fg.reshape(1, d), rw.T.astype(BF16), rb.reshape(N_EXPERTS, 1)]
    kern = functools.partial(_post_mix_kernel, n_batch=n_batch, kv_layer=kv_layer)
    aliases = {}
    if prev is not None:
        n_in = len(args)
        in_specs += [pl.BlockSpec(memory_space=pl.ANY)] * len(prev)
        args += list(prev)
        aliases = {n_in + o: o for o in range(len(prev))}
        kern = _drop_inputs(kern, n_in, len(prev))
    return pl.pallas_call(
        kern,
        grid=(n_blocks,),
        in_specs=in_specs,
        out_specs=[pl.BlockSpec((rows, d), row_map),
                   pl.BlockSpec((rows // SUBLANES,) + PIECE_TILE, lambda i: (first_block + i, 0, 0, 0)),
                   pl.BlockSpec((TOP_K, rows), lambda i: (0, first_block + i)),
                   pl.BlockSpec((TOP_K, rows), lambda i: (0, first_block + i)),
                   pl.BlockSpec((None, N_EXPERTS, LANES), lambda i: (i, 0, 0))],
        out_shape=[jax.ShapeDtypeStruct((N_TOK, d), F32),
                   jax.ShapeDtypeStruct((N_TOK // SUBLANES,) + PIECE_TILE, F32),
                   jax.ShapeDtypeStruct((TOP_K, N_TOK), jnp.int32),
                   jax.ShapeDtypeStruct((TOP_K, N_TOK), F32),
                   jax.ShapeDtypeStruct((n_blocks, N_EXPERTS, LANES), F32)],
        scratch_shapes=[pltpu.VMEM((rows, d), BF16)],
        input_output_aliases=aliases,
        compiler_params=_params("parallel"),
        name="post_mix",
    )(*args)


def _moe_kernel(be_ref, nu_ref, first_ref, slot_ref, nxt_ref, parts_ref,
                x_ref, wg_hbm, wu_hbm, wd_hbm, bg_ref, bu_ref, bd_ref, o_ref,
                wbuf_ref, wgb_ref, wub_ref, wdb_ref, sem, *, layer):
    i = pl.program_id(0)
    active = i < nu_ref[0]

    def copies(e, s):
        return [pltpu.make_async_copy(w.at[layer, e], wbuf_ref.at[s, j], sem.at[s, j])
                for j, w in enumerate((wg_hbm, wu_hbm, wd_hbm))]

    @pl.when(jnp.logical_and(active, first_ref[i] == 1))
    def _():
        s = slot_ref[i]

        @pl.when(i == 0)
        def _():
            for c in copies(be_ref[0], s):
                c.start()

        @pl.when(nxt_ref[i] >= 0)
        def _():
            for c in copies(nxt_ref[i], 1 - s):
                c.start()

        for c in copies(be_ref[i], s):
            c.wait()

        def cast(r, _):
            rs = pl.ds(pl.multiple_of(r * MOE_CAST_ROWS, MOE_CAST_ROWS), MOE_CAST_ROWS)
            wgb_ref[rs, :] = wbuf_ref[s, 0, rs, :].astype(BF16)
            wub_ref[rs, :] = wbuf_ref[s, 1, rs, :].astype(BF16)
            wdb_ref[rs, :] = wbuf_ref[s, 2, rs, :].astype(BF16)
            return 0
        lax.fori_loop(0, D_MODEL // MOE_CAST_ROWS, cast, 0)

    def experts(n_rows):
        words = lax.bitcast_convert_type(
            jnp.concatenate([x_ref[:n_rows // SUBLANES, c].reshape(n_rows, LANES) for c in range(PIECES_PER_ROW)],
                            axis=1), jnp.uint32)
        x = jnp.concatenate([lax.bitcast_convert_type(words << 16, F32),
                             lax.bitcast_convert_type(words & jnp.uint32(HIGH_HALF), F32)], axis=1).astype(BF16)
        g = jnp.minimum(_dot(x, wgb_ref[...]) + bg_ref[...], SWIGLU_LIMIT)
        u = jnp.clip(_dot(x, wub_ref[...]) + bu_ref[...], -SWIGLU_LIMIT, SWIGLU_LIMIT)
        a = g * jax.nn.sigmoid(SWIGLU_ALPHA * g) * (u + 1.0)
        y = _dot(a.astype(BF16), wdb_ref[...]) + bd_ref[...]
        for c in range(OUT_PIECES):
            o_ref[:n_rows // SUBLANES, c] = y[:, c * LANES:(c + 1) * LANES].reshape(n_rows // SUBLANES, SUBLANES, LANES)

    for p in range(1, MOE_ROWS // MOE_PART + 1):
        pl.when(jnp.logical_and(active, parts_ref[i] == p))(functools.partial(experts, p * MOE_PART))


def _moe_call(layer, xs, sched, w_gate, b_gate, w_up, b_up, w_down, b_down):
    def piece_map(i, be, nu, *_):
        return (jnp.minimum(i, nu[0] - 1), 0, 0, 0)

    def b_map(i, be, *_):
        return (layer, be[i], 0, 0)

    b_spec = pl.BlockSpec((None, None, 1, D_FF), b_map)
    w_spec = pl.BlockSpec(memory_space=pl.ANY)
    grid_spec = pltpu.PrefetchScalarGridSpec(
        num_scalar_prefetch=6,
        grid=(MOE_BLOCKS,),
        in_specs=[pl.BlockSpec((MOE_ROWS // SUBLANES,) + PIECE_TILE, piece_map),
                  w_spec, w_spec, w_spec, b_spec, b_spec, b_spec],
        out_specs=pl.BlockSpec((MOE_ROWS // SUBLANES,) + OUT_TILE, piece_map),
        scratch_shapes=[pltpu.VMEM((2, 3, D_MODEL, D_FF), F32)]
        + [pltpu.VMEM((D_MODEL, D_FF), BF16)] * 3
        + [pltpu.SemaphoreType.DMA((2, 3))],
    )
    shape4 = (DEPTH, N_EXPERTS, 1, D_FF)
    return pl.pallas_call(
        functools.partial(_moe_kernel, layer=layer),
        grid_spec=grid_spec,
        out_shape=jax.ShapeDtypeStruct((MOE_ROWS_TOTAL // SUBLANES,) + OUT_TILE, F32),
        compiler_params=pltpu.CompilerParams(dimension_semantics=("arbitrary",),
                                             vmem_limit_bytes=MOE_VMEM_LIMIT),
        name="moe_experts",
    )(*sched, xs, w_gate, w_up, w_down, b_gate.reshape(shape4), b_up.reshape(shape4),
      b_down.reshape(shape4))


DISPATCH_TOKENS = 48
DMA_PIECES = 64
DISPATCH_DMAS = DISPATCH_TOKENS * PIECES_PER_ROW // DMA_PIECES
DISPATCH_CHUNKS = N_TOK // SC_WORKERS // DISPATCH_TOKENS
IDX_ROWS = 16
SC_LANES = 16
MOE_ROWS_TOTAL = MOE_BLOCKS * MOE_ROWS
TILE_PIECES = PIECES_PER_ROW * SUBLANES


def _piece_index_rows(p_v, idx_v, row0, tok0, n_dma, pieces_per_row, dma_pieces):
    lane = lax.iota(jnp.int32, SC_LANES)
    in_tile = lane & (SUBLANES - 1)
    piece = (lane >> 3) * SUBLANES
    per_vec = SC_LANES // SUBLANES
    tile_pieces = pieces_per_row * SUBLANES
    tiles_per_dma = dma_pieces // tile_pieces
    for m in range(n_dma):
        for tr in range(tiles_per_dma):
            first = plsc.load_gather(p_v, [in_tile + (tok0 + (m * tiles_per_dma + tr) * SUBLANES)])
            for v in range(pieces_per_row // per_vec):
                idx_v[row0 + m, pl.ds(tr * tile_pieces + v * SC_LANES, SC_LANES)] = (
                    first + (piece + v * per_vec * SUBLANES))


COLLECT_ROWS = 48
COLLECT_PIECES = 128
OUT_PIECES = D_MODEL // LANES
OUT_TILE = (OUT_PIECES, SUBLANES, LANES)


def _collect(ys4, dest_t):
    first_piece = ((dest_t // SUBLANES) * (OUT_PIECES * SUBLANES) + dest_t % SUBLANES).reshape(-1)
    chunk_pieces = COLLECT_ROWS * OUT_PIECES
    n_dma = chunk_pieces // COLLECT_PIECES
    chunks_per_choice = N_TOK // SC_WORKERS // COLLECT_ROWS
    n_chunks = TOP_K * chunks_per_choice
    mesh = plsc.VectorSubcoreMesh(core_axis_name="c", subcore_axis_name="s")

    @functools.partial(
        pl.kernel, mesh=mesh,
        out_type=jax.ShapeDtypeStruct((TOP_K * N_TOK * OUT_PIECES, LANES), F32),
        scratch_types=[pltpu.VMEM((chunk_pieces, LANES), F32), pltpu.VMEM((chunk_pieces, LANES), F32),
                       pltpu.VMEM((SUBLANES, COLLECT_PIECES), jnp.int32),
                       pltpu.VMEM((SUBLANES, COLLECT_PIECES), jnp.int32),
                       pltpu.VMEM((COLLECT_ROWS,), jnp.int32),
                       pltpu.SemaphoreType.DMA, pltpu.SemaphoreType.DMA],
        compiler_params=pltpu.CompilerParams(needs_layout_passes=False),
    )
    def gather_rows(y_hbm, p_hbm, o_hbm, rows0, rows1, idx0, idx1, p_v, sem0, sem1):
        wid = lax.axis_index("s") * SC_CORES + lax.axis_index("c")
        bufs = ((rows0, idx0, sem0), (rows1, idx1, sem1))

        def first_row(j):
            k = j // chunks_per_choice
            return k * N_TOK + (wid * chunks_per_choice + (j - k * chunks_per_choice)) * COLLECT_ROWS

        def copies(b):
            rows_v, idx_v, sem = bufs[b]
            return [pltpu.make_async_copy(y_hbm.at[idx_v.at[m]],
                                          rows_v.at[pl.ds(m * COLLECT_PIECES, COLLECT_PIECES)], sem)
                    for m in range(n_dma)]

        def fire(j, b):
            pltpu.sync_copy(p_hbm.at[pl.ds(pl.multiple_of(first_row(j), SUBLANES), COLLECT_ROWS)], p_v)
            _piece_index_rows(p_v, bufs[b][1], 0, 0, n_dma, OUT_PIECES, COLLECT_PIECES)
            for cp in copies(b):
                cp.start()

        def drain(j, b):
            for cp in copies(b):
                cp.wait()
            pltpu.sync_copy(bufs[b][0],
                            o_hbm.at[pl.ds(pl.multiple_of(first_row(j) * OUT_PIECES, SUBLANES), chunk_pieces)])

        fire(0, 0)

        @pl.loop(0, n_chunks, step=2)
        def _(j):
            fire(j + 1, 1)
            drain(j, 0)
            pl.when(j + 2 < n_chunks)(lambda: fire(j + 2, 0))
            drain(j + 1, 1)

    picked = gather_rows(ys4.reshape(MOE_ROWS_TOTAL * OUT_PIECES, LANES), first_piece)
    return picked.reshape((TOP_K, N_TOK // SUBLANES) + OUT_TILE)


def _dispatch(h4, dest_t):
    first_piece = ((dest_t // SUBLANES) * TILE_PIECES + dest_t % SUBLANES).reshape(-1)
    chunk_pieces = DISPATCH_TOKENS * PIECES_PER_ROW
    mesh = plsc.VectorSubcoreMesh(core_axis_name="c", subcore_axis_name="s")

    @functools.partial(
        pl.kernel, mesh=mesh,
        out_type=jax.ShapeDtypeStruct((MOE_ROWS_TOTAL * PIECES_PER_ROW, LANES), F32),
        scratch_types=[pltpu.VMEM((chunk_pieces, LANES), F32), pltpu.VMEM((IDX_ROWS, DMA_PIECES), jnp.int32),
                       pltpu.VMEM((TOP_K * DISPATCH_TOKENS,), jnp.int32), pltpu.SemaphoreType.DMA],
        compiler_params=pltpu.CompilerParams(needs_layout_passes=False),
    )
    def scatter_rows(h_hbm, p_hbm, o_hbm, rows_v, idx_v, p_v, sem):
        wid = lax.axis_index("s") * SC_CORES + lax.axis_index("c")

        @pl.loop(0, DISPATCH_CHUNKS)
        def _(j):
            q = wid * DISPATCH_CHUNKS + j
            pltpu.sync_copy(h_hbm.at[pl.ds(pl.multiple_of(q * chunk_pieces, SUBLANES), chunk_pieces)], rows_v)
            for k in range(TOP_K):
                pltpu.sync_copy(
                    p_hbm.at[pl.ds(pl.multiple_of(k * N_TOK + q * DISPATCH_TOKENS, SUBLANES), DISPATCH_TOKENS)],
                    p_v.at[pl.ds(k * DISPATCH_TOKENS, DISPATCH_TOKENS)])
            for k in range(TOP_K):
                _piece_index_rows(p_v, idx_v, k * DISPATCH_DMAS, k * DISPATCH_TOKENS, DISPATCH_DMAS,
                                  PIECES_PER_ROW, DMA_PIECES)
            copies = [pltpu.async_copy(rows_v.at[pl.ds(m * DMA_PIECES, DMA_PIECES)],
                                       o_hbm.at[idx_v.at[k * DISPATCH_DMAS + m]], sem)
                      for k in range(TOP_K) for m in range(DISPATCH_DMAS)]
            for cp in copies:
                cp.wait()

    xs = scatter_rows(h4.reshape(N_TOK * PIECES_PER_ROW, LANES), first_piece)
    return xs.reshape((MOE_ROWS_TOTAL // SUBLANES,) + PIECE_TILE)


def _route(e_t, cnt_prompt, cnt_sample):
    cnt_sample = jnp.sum(cnt_sample[:, :, 0].reshape(SAMPLE_BLOCKS, -1, N_EXPERTS), axis=1)
    blk_cnt = jnp.concatenate([cnt_prompt[:, :, 0], cnt_sample]).astype(jnp.int32)
    def before(n):
        i = jnp.arange(n, dtype=jnp.int32)
        return i[None, :] < i[:, None]

    counts = jnp.sum(blk_cnt, axis=0)
    padded = ((counts + MOE_ROWS - 1) // MOE_ROWS) * MOE_ROWS
    pstart = jnp.sum(jnp.where(before(N_EXPERTS), padded[None, :], 0), axis=1)
    pend = pstart + padded
    base = pstart[None, :] + jnp.sum(jnp.where(before(blk_cnt.shape[0])[:, :, None], blk_cnt[None], 0), axis=1)
    dest_t = _dest(e_t, base.astype(F32)[:, :, None])
    n_used = jnp.sum(padded) // MOE_ROWS
    ids = jnp.arange(MOE_BLOCKS, dtype=jnp.int32)
    blk = jnp.minimum(ids, n_used - 1) * MOE_ROWS
    blk_exp = jnp.minimum(jnp.sum((pend[None, :] <= blk[:, None]).astype(jnp.int32), axis=1), N_EXPERTS - 1)
    eid = jnp.arange(N_EXPERTS, dtype=jnp.int32)
    of_blk = blk_exp[:, None] == eid[None, :]

    def lookup(table):
        return jnp.sum(jnp.where(of_blk, table[None, :], 0), axis=1)

    first = jnp.logical_and(ids < n_used, jnp.logical_or(ids == 0, blk_exp != jnp.roll(blk_exp, 1)))
    first = first.astype(jnp.int32)
    slot = jnp.sum(jnp.where(before(MOE_BLOCKS), first[None, :], 0), axis=1) % 2
    later = jnp.logical_and(eid[None, :] > eid[:, None], (counts > 0)[None, :])
    nxt_e = jnp.min(jnp.where(later, eid[None, :], N_EXPERTS), axis=1)
    nxt = lookup(jnp.where(nxt_e < N_EXPERTS, nxt_e, -1))
    real = jnp.clip(lookup(pstart + counts) - blk, 0, MOE_ROWS)
    parts = (real + MOE_PART - 1) // MOE_PART
    return dest_t, (blk_exp, n_used.reshape(1).astype(jnp.int32), first, slot.astype(jnp.int32),
                    nxt.astype(jnp.int32), parts.astype(jnp.int32))


def kernel(x_prompt, x_sample, mem_prompt, state_s5_re, state_s5_im, cache_mem_k, cache_mem_v, norm_mix_g, w_in, s5_lambda_re, s5_lambda_im, s5_b_re, s5_b_im, s5_c_re, s5_c_im, s5_d, s5_log_step, s5_w_glu, gm_ln_g, gm_ln_b, gm_w_s, gm_b_s, out_norm_g, w_out, xa_norm_g, mem_norm_g, w_xq, w_xk, w_xv, w_xo, ffn_norm_g, router_w, router_b, e_w_gate, e_b_gate, e_w_up, e_b_up, e_w_down, e_b_down, final_norm_g):
    xs = (x_prompt.reshape(N_PROMPT, D_MODEL), x_sample.reshape(N_SAMPLE, D_MODEL))
    mem = mem_prompt.reshape(BATCH * N_MEM, D_MODEL)
    outs = {k: [] for k in ('s5r_p', 's5i_p', 's5r_s', 's5i_s', 'gmv_s')}
    mem_kv = None
    for l in range(DEPTH):
        unified = len(xs) == 1

        def sample_first(rows):
            return N_PROMPT // rows if unified else 0

        in_proj = (norm_mix_g[l], w_in[l].astype(BF16))
        sw = _s5_weights(s5_lambda_re[l], s5_lambda_im[l], s5_b_re[l], s5_b_im[l], s5_c_re[l], s5_c_im[l],
                         s5_d[l], s5_log_step[l], S5_SEG_PROMPT)
        w_glu = s5_w_glu[l].astype(BF16)
        og = out_norm_g[l]
        ms5, er_p, ei_p = _s5_call(xs[0], 0, *in_proj, sw, w_glu, og[:S5_WIDTH], S5_SEG_PROMPT, True)
        h0 = (state_s5_re[l].reshape(DEC_BATCH // SUBLANES, SUBLANES, S5_COLS),
              state_s5_im[l].reshape(DEC_BATCH // SUBLANES, SUBLANES, S5_COLS))
        ms5, er_s, ei_s = _s5_call(xs[-1], sample_first(DEC_SEQ * SUBLANES), *in_proj, sw, w_glu, og[:S5_WIDTH],
                                   DEC_SEQ, False, h0=h0, prev=ms5)
        outs['s5r_p'].append(er_p[:, SUBLANES - 1].reshape(BATCH, S5_GROUPS, S5_STATE))
        outs['s5i_p'].append(ei_p[:, SUBLANES - 1].reshape(BATCH, S5_GROUPS, S5_STATE))
        outs['s5r_s'].append(er_s.reshape(DEC_BATCH, S5_GROUPS, S5_STATE))
        outs['s5i_s'].append(ei_s.reshape(DEC_BATCH, S5_GROUPS, S5_STATE))

        (mgm,) = _gmlp_call(xs[0], 0, *in_proj, gm_ln_g[l], gm_ln_b[l], gm_w_s[l], gm_b_s[l], og[S5_WIDTH:],
                            GM_ROWS_PROMPT, GM_CHUNK, 0, N_PROMPT // GM_ROWS_PROMPT, False)
        mgm, vn = _gmlp_call(xs[-1], sample_first(N_SAMPLE), *in_proj, gm_ln_g[l], gm_ln_b[l], gm_w_s[l],
                             gm_b_s[l], og[S5_WIDTH:], N_SAMPLE, DEC_SEQ, N_PROMPT // N_SAMPLE, 1, True, prev=mgm)
        outs['gmv_s'].append(vn.reshape(DEC_BATCH, DEC_SEQ, GM_HEADS, GM_HEAD_DIM))
        *mem_kv, mk, mv = _mem_kv(mem, mem_norm_g[l], w_xk[l].astype(BF16), w_xv[l].astype(BF16), l, prev=mem_kv)
        shared = (w_out[l].astype(BF16), xa_norm_g[l], w_xq[l].astype(BF16))
        tail = (w_xo[l].astype(BF16), ffn_norm_g[l], router_w[l], router_b[l])
        rows_s = XA_SAMPLE_BATCHES * DEC_SEQ
        post_p = _post_mix(xs[0], 0, ms5, mgm, *shared, mk, mv, *tail,
                           XA_ROWS_PROMPT, 1, 0, N_PROMPT // XA_ROWS_PROMPT, SEQ // XA_ROWS_PROMPT)
        post_s = _post_mix(xs[-1], sample_first(rows_s), ms5, mgm, *shared,
                           cache_mem_k, cache_mem_v, *tail,
                           rows_s, XA_SAMPLE_BATCHES, N_PROMPT // rows_s, N_SAMPLE // rows_s, 1,
                           prev=post_p[:4], kv_layer=l)
        x, h, e_t, gates_t = post_s[:4]

        dest_t, sched = _route(e_t, post_p[4], post_s[4])
        ys = _moe_call(l, _dispatch(h, dest_t), sched, e_w_gate, e_b_gate, e_w_up, e_b_up, e_w_down, e_b_down)
        x = _combine(x, _collect(ys, dest_t), gates_t.T, final_norm_g if l == DEPTH - 1 else None)
        xs = (x,)

    y_prompt = x[0].reshape(BATCH, SEQ, D_MODEL)
    y_sample = x[1].reshape(DEC_BATCH, DEC_SEQ, D_MODEL)
    st = jnp.stack
    return (y_prompt, y_sample, st(outs['s5r_p']), st(outs['s5i_p']), mem_kv[0], mem_kv[1],
            st(outs['s5r_s']), st(outs['s5i_s']), st(outs['gmv_s']))
```

```python
import functools

import jax
import jax.numpy as jnp
from jax import lax
from jax.experimental import pallas as pl
from jax.experimental.pallas import tpu as pltpu
from jax.experimental.pallas import tpu_sc as plsc

F32 = jnp.float32
BF16 = jnp.bfloat16

D_MODEL = 1024
BATCH = 4
SEQ = 4096
DEPTH = 2
DEC_BATCH = 16
DEC_SEQ = 32
S5_GROUPS = 32
S5_GROUP_CH = 16
S5_STATE = 64
S5_WIDTH = S5_GROUPS * S5_GROUP_CH
S5_COLS = S5_GROUPS * S5_STATE
GM_HEADS = 4
GM_HEAD_DIM = 128
GM_WIDTH = GM_HEADS * GM_HEAD_DIM
GM_CHUNK = 128
N_MEM = 256
XA_HEADS = 4
XA_HEAD_DIM = D_MODEL // XA_HEADS
N_EXPERTS = 32
TOP_K = 4
D_FF = D_MODEL
SWIGLU_LIMIT = 7.0
SWIGLU_ALPHA = 1.702
EPS = 1e-6
EIG_CLIP = -1e-4

N_PROMPT = BATCH * SEQ
N_SAMPLE = DEC_BATCH * DEC_SEQ
N_TOK = N_PROMPT + N_SAMPLE

SUBLANES = 8
LANES = 128
MXU_DIM = 256
VMEM_LIMIT = 48 * 1024 * 1024
MOE_VMEM_LIMIT = 60 * 1024 * 1024
SC_CORES = 2
SC_SUBCORES = 16
SC_WORKERS = SC_CORES * SC_SUBCORES

PACK_WIDTH = D_MODEL // 2
PIECES_PER_ROW = PACK_WIDTH // LANES
PIECE_TILE = (PIECES_PER_ROW, SUBLANES, LANES)
HIGH_HALF = 0xFFFF0000

ROW_BLOCK = 512
S5_SEG_PROMPT = 64
S5_COL_BLOCK = 512
GM_ROWS_PROMPT = 512
XA_ROWS_PROMPT = 512
XA_SAMPLE_BATCHES = 4
MOE_ROWS = 1024
MOE_PART = 256
MOE_BLOCKS = -(-N_TOK * TOP_K // MOE_ROWS) + N_EXPERTS
MOE_CAST_ROWS = 128


def _params(*sem):
    return pltpu.CompilerParams(dimension_semantics=sem, vmem_limit_bytes=VMEM_LIMIT)


def _rms(x, g):
    return x * lax.rsqrt(jnp.mean(x * x, axis=-1, keepdims=True) + EPS) * g


def _dot(a, b):
    return jnp.dot(a, b, preferred_element_type=F32)


PROMPT_BLOCKS = N_PROMPT // ROW_BLOCK
SAMPLE_BLOCKS = N_SAMPLE // ROW_BLOCK


def _group_specs():
    return [pl.BlockSpec((ROW_BLOCK, D_MODEL), lambda i: (jnp.minimum(i, PROMPT_BLOCKS - 1), 0)),
            pl.BlockSpec((ROW_BLOCK, D_MODEL), lambda i: (jnp.maximum(i - PROMPT_BLOCKS, 0), 0))]


def _for_group(refs, body):
    i = pl.program_id(0)
    pl.when(i < PROMPT_BLOCKS)(lambda: body(refs[0]))
    pl.when(i >= PROMPT_BLOCKS)(lambda: body(refs[1]))


def _mem_kv_kernel(m_ref, g_ref, wk_ref, wv_ref, k5_ref, v5_ref, kb_ref, vb_ref):
    m = _rms(m_ref[...], g_ref[...]).astype(BF16)
    for w_ref, o5_ref, ob_ref in ((wk_ref, k5_ref, kb_ref), (wv_ref, v5_ref, vb_ref)):
        kv = _dot(m, w_ref[...])
        ob_ref[...] = kv.astype(BF16)
        for h in range(XA_HEADS):
            o5_ref[:, h, :] = kv[:, h * XA_HEAD_DIM:(h + 1) * XA_HEAD_DIM]


def _mem_kv(mem, g, wk, wv, layer, prev=None):
    d = D_MODEL
    const2 = lambda b: (0, 0)
    in_specs = [pl.BlockSpec((N_MEM, d), lambda b: (b, 0)),
                pl.BlockSpec((1, d), const2),
                pl.BlockSpec((d, d), const2),
                pl.BlockSpec((d, d), const2)]
    args = [mem, g.reshape(1, d), wk, wv]
    kern = _mem_kv_kernel
    aliases = {}
    if prev is not None:
        in_specs += [pl.BlockSpec(memory_space=pl.ANY)] * 2
        args += list(prev)
        aliases = {4: 0, 5: 1}
        kern = _drop_inputs(kern, 4, 2)
    out5 = pl.BlockSpec((None, None, N_MEM, XA_HEADS, XA_HEAD_DIM), lambda b: (layer, b, 0, 0, 0))
    outb = pl.BlockSpec((None, N_MEM, d), lambda b: (b, 0, 0))
    shape5 = jax.ShapeDtypeStruct((DEPTH, BATCH, N_MEM, XA_HEADS, XA_HEAD_DIM), F32)
    shapeb = jax.ShapeDtypeStruct((BATCH, N_MEM, d), BF16)
    return pl.pallas_call(
        kern,
        grid=(BATCH,),
        in_specs=in_specs,
        out_specs=[out5, out5, outb, outb],
        out_shape=[shape5, shape5, shapeb, shapeb],
        input_output_aliases=aliases,
        compiler_params=_params("parallel"),
        name="mem_kv",
    )(*args)


def _route_rows(x, g_ref, wt_ref, b_ref, h_ref, e_ref, gate_ref, cnt_ref):
    h = _rms(x, g_ref[...]).astype(BF16)
    bits = lax.bitcast_convert_type(h.astype(F32), jnp.uint32)
    words = lax.bitcast_convert_type(bits[:, PACK_WIDTH:] | (bits[:, :PACK_WIDTH] >> 16), F32)
    for c in range(PIECES_PER_ROW):
        h_ref[:, c] = words[:, c * LANES:(c + 1) * LANES].reshape(h.shape[0] // SUBLANES, SUBLANES, LANES)
    logits = lax.dot_general(wt_ref[...], h, (((1,), (1,)), ((), ())),
                             preferred_element_type=F32) + b_ref[...]
    sub = lax.broadcasted_iota(jnp.int32, logits.shape, 0)
    work = logits
    chosen = jnp.zeros(logits.shape, F32)
    vals, idxs = [], []
    for _ in range(TOP_K):
        m = jnp.max(work, axis=0, keepdims=True)
        idx = jnp.min(jnp.where(work == m, sub, N_EXPERTS), axis=0, keepdims=True)
        sel = sub == idx
        vals.append(m)
        idxs.append(idx)
        work = jnp.where(sel, -jnp.inf, work)
        chosen = jnp.where(sel, 1.0, chosen)
    ex = [jnp.exp(v - vals[0]) for v in vals]
    den = ex[0] + ex[1] + ex[2] + ex[3]
    gate_ref[...] = jnp.concatenate([e / den for e in ex], axis=0)
    e_ref[...] = jnp.concatenate(idxs, axis=0)
    cnt_ref[...] = jnp.broadcast_to(jnp.sum(chosen, axis=1, keepdims=True), (N_EXPERTS, LANES))


def _dest_kernel(e_ref, base_ref, d_ref):
    e = e_ref[...]
    rows = e.shape[1]
    sub = lax.broadcasted_iota(jnp.int32, (N_EXPERTS, rows), 0)
    hot = [sub == e[k:k + 1, :] for k in range(TOP_K)]
    chosen = sum(jnp.where(hk, 1.0, 0.0) for hk in hot)
    earlier = (lax.broadcasted_iota(jnp.int32, (rows, rows), 0)
               < lax.broadcasted_iota(jnp.int32, (rows, rows), 1))
    pos = base_ref[...] + _dot(chosen.astype(BF16), jnp.where(earlier, 1.0, 0.0).astype(BF16))
    d_ref[...] = jnp.concatenate(
        [jnp.sum(jnp.where(hk, pos, 0.0), axis=0, keepdims=True) for hk in hot], axis=0).astype(jnp.int32)


def _dest(e_t, base):
    n = e_t.shape[1]
    rows = ROW_BLOCK
    return pl.pallas_call(
        _dest_kernel,
        grid=(n // rows,),
        in_specs=[pl.BlockSpec((TOP_K, rows), lambda i: (0, i)),
                  pl.BlockSpec((None, N_EXPERTS, 1), lambda i: (i, 0, 0))],
        out_specs=pl.BlockSpec((TOP_K, rows), lambda i: (0, i)),
        out_shape=jax.ShapeDtypeStruct((TOP_K, n), jnp.int32),
        compiler_params=_params("parallel"),
        name="moe_dest",
    )(e_t, base)


def _combine_kernel(*refs, final):
    if final:
        x_ref, g0, g1, g2, g3, gate_ref, fg_ref, op_ref, os_ref = refs
    else:
        x_ref, g0, g1, g2, g3, gate_ref, o_ref = refs
    rows = x_ref.shape[0]
    slabs = []
    for c in range(OUT_PIECES):
        acc = x_ref[:, c * LANES:(c + 1) * LANES]
        for k, g in enumerate((g0, g1, g2, g3)):
            acc = acc + gate_ref[:, k:k + 1] * g[:, c].reshape(rows, LANES)
        slabs.append(acc)
    acc = jnp.concatenate(slabs, axis=1)
    if final:
        def store(o_ref):
            o_ref[...] = _rms(acc, fg_ref[...])
        _for_group((op_ref, os_ref), store)
    else:
        o_ref[...] = acc


def _combine(x, picked, gates, final_g=None):
    n, d = x.shape
    rows = ROW_BLOCK
    final = final_g is not None
    in_specs = [pl.BlockSpec((rows, d), lambda i: (i, 0))]
    in_specs += [pl.BlockSpec((None, rows // SUBLANES) + OUT_TILE, functools.partial(lambda i, k: (k, i, 0, 0, 0), k=k))
                 for k in range(TOP_K)]
    in_specs.append(pl.BlockSpec((rows, TOP_K), lambda i: (i, 0)))
    args = [x, picked, picked, picked, picked, gates]
    out_specs = pl.BlockSpec((rows, d), lambda i: (i, 0))
    out_shape = jax.ShapeDtypeStruct((n, d), F32)
    if final:
        in_specs.append(pl.BlockSpec((1, d), lambda i: (0, 0)))
        args.append(final_g.reshape(1, d))
        out_specs = _group_specs()
        out_shape = [jax.ShapeDtypeStruct((N_PROMPT, d), F32), jax.ShapeDtypeStruct((N_SAMPLE, d), F32)]
    return pl.pallas_call(
        functools.partial(_combine_kernel, final=final),
        grid=(n // rows,),
        in_specs=in_specs,
        out_specs=out_specs,
        out_shape=out_shape,
        compiler_params=_params("arbitrary" if final else "parallel"),
        name="moe_combine",
    )(*args)


def _cmul_add(ar, ai, hr, hi, br, bi):
    return ar * hr - ai * hi + br, ar * hi + ai * hr + bi


def _s5_kernel(*refs, seg, chained):
    if chained:
        (x_ref, ng_ref, win_ref, wbr_ref, wbi_ref, wcr_ref, wci_ref, ar_ref, ai_ref, pr_ref, pi_ref, d_ref,
         wglu_ref, og_ref, o_ref, er_ref, ei_ref,
         up_ref, hr_ref, hi_ref, sr_ref, si_ref, op_ref, cr_ref, ci_ref) = refs
    else:
        (x_ref, ng_ref, win_ref, wbr_ref, wbi_ref, wcr_ref, wci_ref, ar_ref, ai_ref, pr_ref, pi_ref, d_ref,
         wglu_ref, og_ref, h0r_ref, h0i_ref, o_ref, er_ref, ei_ref,
         up_ref, hr_ref, hi_ref, sr_ref, si_ref, op_ref) = refs
    rows = seg * SUBLANES
    half = S5_COLS // 2

    u = _dot(_rms(x_ref[...], ng_ref[...]).astype(BF16), win_ref[...])
    n_lane_blk = S5_WIDTH // LANES
    for j in range(SUBLANES):
        for i0 in range(0, seg, SUBLANES):
            for c in range(n_lane_blk):
                up_ref[c, pl.ds(i0 * SUBLANES + j, SUBLANES, stride=SUBLANES), :] = (
                    u[j * seg + i0:j * seg + i0 + SUBLANES, c * LANES:(c + 1) * LANES])

    up = jnp.concatenate([up_ref[c] for c in range(n_lane_blk)], axis=1)
    ub = up.astype(BF16)
    for k in range(2):
        uk = ub[:, k * MXU_DIM:(k + 1) * MXU_DIM]
        hr_ref[:, k * half:(k + 1) * half] = _dot(uk, wbr_ref[k])
        hi_ref[:, k * half:(k + 1) * half] = _dot(uk, wbi_ref[k])

    for cb in range(S5_COLS // S5_COL_BLOCK):
        cols = slice(cb * S5_COL_BLOCK, (cb + 1) * S5_COL_BLOCK)
        ar = jnp.broadcast_to(ar_ref[:, cols], (SUBLANES, S5_COL_BLOCK))
        ai = jnp.broadcast_to(ai_ref[:, cols], (SUBLANES, S5_COL_BLOCK))

        def step(i, carry, cols=cols, ar=ar, ai=ai):
            r = pl.multiple_of(i * SUBLANES, SUBLANES)
            h_r, h_i = _cmul_add(ar, ai, carry[0], carry[1],
                                 hr_ref[pl.ds(r, SUBLANES), cols], hi_ref[pl.ds(r, SUBLANES), cols])
            hr_ref[pl.ds(r, SUBLANES), cols] = h_r
            hi_ref[pl.ds(r, SUBLANES), cols] = h_i
            return h_r, h_i

        zero = jnp.zeros((SUBLANES, S5_COL_BLOCK), F32)
        lax.fori_loop(0, seg, step, (zero, zero), unroll=2)

    end_r = hr_ref[rows - SUBLANES:rows, :]
    end_i = hi_ref[rows - SUBLANES:rows, :]
    as_r = pr_ref[rows - 1:rows, :]
    as_i = pi_ref[rows - 1:rows, :]
    if chained:
        @pl.when(pl.program_id(1) == 0)
        def _():
            cr_ref[...] = jnp.zeros_like(cr_ref)
            ci_ref[...] = jnp.zeros_like(ci_ref)

        cur_r, cur_i = cr_ref[...], ci_ref[...]
        for j in range(SUBLANES):
            sr_ref[j:j + 1, :] = cur_r
            si_ref[j:j + 1, :] = cur_i
            cur_r, cur_i = _cmul_add(as_r, as_i, cur_r, cur_i, end_r[j:j + 1], end_i[j:j + 1])
        cr_ref[...] = cur_r
        ci_ref[...] = cur_i
    else:
        sr_ref[...] = h0r_ref[...]
        si_ref[...] = h0i_ref[...]
    t_r, t_i = _cmul_add(as_r, as_i, sr_ref[...], si_ref[...], end_r, end_i)
    er_ref[...] = t_r
    ei_ref[...] = t_i

    slabs_per_window = MXU_DIM // (MXU_DIM // S5_STATE * S5_GROUP_CH)
    ys = []
    for w in range(S5_WIDTH // MXU_DIM):
        acc = d_ref[:, w * MXU_DIM:(w + 1) * MXU_DIM] * jnp.concatenate(
            [up_ref[c] for c in range(w * MXU_DIM // LANES, (w + 1) * MXU_DIM // LANES)], axis=1)
        for s in range(slabs_per_window):
            cols = slice((w * slabs_per_window + s) * MXU_DIM, (w * slabs_per_window + s + 1) * MXU_DIM)
            shape3 = (seg, SUBLANES, MXU_DIM)
            h_r, h_i = _cmul_add(pr_ref[:, cols].reshape(shape3), pi_ref[:, cols].reshape(shape3),
                                 sr_ref[:, cols][None], si_ref[:, cols][None],
                                 hr_ref[:, cols].reshape(shape3), hi_ref[:, cols].reshape(shape3))
            acc += _dot(h_r.reshape(rows, MXU_DIM).astype(BF16), wcr_ref[w * slabs_per_window + s])
            acc += _dot(h_i.reshape(rows, MXU_DIM).astype(BF16), wci_ref[w * slabs_per_window + s])
        ys.append(jax.nn.gelu(acc).astype(BF16))

    z = _dot(jnp.concatenate(ys, axis=1), wglu_ref[...])
    o = _rms(z[:, :S5_WIDTH] * jax.nn.sigmoid(z[:, S5_WIDTH:]), og_ref[...])
    for c in range(n_lane_blk):
        op_ref[c] = o[:, c * LANES:(c + 1) * LANES]

    for j in range(SUBLANES):
        for i0 in range(0, seg, 2 * SUBLANES):
            for c in range(n_lane_blk):
                o_ref[j * seg + i0:j * seg + i0 + 2 * SUBLANES, c * LANES:(c + 1) * LANES] = (
                    op_ref[c, pl.ds(i0 * SUBLANES + j, 2 * SUBLANES, stride=SUBLANES), :].astype(BF16))


def _s5_weights(lam_re, lam_im, b_re, b_im, c_re, c_im, d, log_step, seg):
    step = jnp.exp(log_step)[:, None]
    lr = jnp.minimum(lam_re, EIG_CLIP)
    li = lam_im
    mag = jnp.exp(lr * step)
    ar = mag * jnp.cos(li * step)
    ai = mag * jnp.sin(li * step)
    den = lr * lr + li * li
    fr = ((ar - 1.0) * lr + ai * li) / den
    fi = (ai * lr - (ar - 1.0) * li) / den
    bbar_r = fr[..., None] * b_re - fi[..., None] * b_im
    bbar_i = fr[..., None] * b_im + fi[..., None] * b_re
    t = jnp.arange(1, seg + 1, dtype=F32)[:, None, None]
    pm = jnp.exp(lr * step * t)
    p_r = jnp.repeat((pm * jnp.cos(li * step * t)).reshape(seg, S5_COLS), SUBLANES, axis=0)
    p_i = jnp.repeat((pm * jnp.sin(li * step * t)).reshape(seg, S5_COLS), SUBLANES, axis=0)

    gk = MXU_DIM // S5_GROUP_CH

    def in_map(bbar):
        bb = bbar.reshape(S5_GROUPS // gk, gk, S5_STATE, S5_GROUP_CH)
        w = jnp.einsum('kgnh,gG->kghGn', bb, jnp.eye(gk, dtype=F32))
        return w.reshape(S5_GROUPS // gk, MXU_DIM, gk * S5_STATE).astype(BF16)

    gs = MXU_DIM // S5_STATE
    n_slab = S5_COLS // MXU_DIM
    per_win = MXU_DIM // (gs * S5_GROUP_CH)

    def out_map(c):
        cc = c.reshape(n_slab // per_win, per_win, gs, S5_GROUP_CH, S5_STATE)
        w = jnp.einsum('wsghn,sS,gG->wsgnSGh', cc, jnp.eye(per_win, dtype=F32), jnp.eye(gs, dtype=F32))
        return w.reshape(n_slab, MXU_DIM, MXU_DIM).astype(BF16)

    return dict(wbr=in_map(bbar_r), wbi=in_map(bbar_i), wcr=out_map(c_re), wci=out_map(-c_im),
                ar=ar.reshape(1, S5_COLS), ai=ai.reshape(1, S5_COLS), pr=p_r, pi=p_i,
                d=d.reshape(1, S5_WIDTH))


def _s5_call(x, x_first, norm_g, w_in, sw, w_glu, og, seg, chained, h0=None, prev=None):
    rows = seg * SUBLANES
    const2 = lambda *_: (0, 0)
    const3 = lambda *_: (0, 0, 0)
    if chained:
        n_chunk = SEQ // rows
        grid = (BATCH, n_chunk)
        row_map = lambda b, c: (b * n_chunk + c, 0)
        x_map = row_map
        end_map = lambda b, c: (b, 0, 0)
        n_end = BATCH
        sem = ("parallel", "arbitrary")
    else:
        grid = (DEC_BATCH // SUBLANES,)
        row_map = lambda k: (N_PROMPT // rows + k, 0)
        x_map = lambda k: (x_first + k, 0)
        end_map = lambda k: (k, 0, 0)
        n_end = DEC_BATCH // SUBLANES
        sem = ("parallel",)
    half = S5_COLS // 2
    in_specs = [pl.BlockSpec((rows, D_MODEL), x_map),
                pl.BlockSpec((1, D_MODEL), const2),
                pl.BlockSpec((D_MODEL, S5_WIDTH), const2),
                pl.BlockSpec((2, MXU_DIM, half), const3),
                pl.BlockSpec((2, MXU_DIM, half), const3),
                pl.BlockSpec((S5_COLS // MXU_DIM, MXU_DIM, MXU_DIM), const3),
                pl.BlockSpec((S5_COLS // MXU_DIM, MXU_DIM, MXU_DIM), const3),
                pl.BlockSpec((1, S5_COLS), const2),
                pl.BlockSpec((1, S5_COLS), const2),
                pl.BlockSpec((rows, S5_COLS), const2),
                pl.BlockSpec((rows, S5_COLS), const2),
                pl.BlockSpec((1, S5_WIDTH), const2),
                pl.BlockSpec((S5_WIDTH, 2 * S5_WIDTH), const2),
                pl.BlockSpec((1, S5_WIDTH), const2)]
    args = [x, norm_g.reshape(1, D_MODEL), w_in[:, :S5_WIDTH], sw['wbr'], sw['wbi'], sw['wcr'], sw['wci'], sw['ar'], sw['ai'],
            sw['pr'][:rows], sw['pi'][:rows], sw['d'], w_glu, og.reshape(1, S5_WIDTH)]
    scratch = [pltpu.VMEM((S5_WIDTH // LANES, rows, LANES), F32),
               pltpu.VMEM((rows, S5_COLS), F32), pltpu.VMEM((rows, S5_COLS), F32),
               pltpu.VMEM((SUBLANES, S5_COLS), F32), pltpu.VMEM((SUBLANES, S5_COLS), F32),
               pltpu.VMEM((S5_WIDTH // LANES, rows, LANES), F32)]
    aliases = {}
    if chained:
        scratch += [pltpu.VMEM((1, S5_COLS), F32), pltpu.VMEM((1, S5_COLS), F32)]
    else:
        in_specs += [pl.BlockSpec((None, SUBLANES, S5_COLS), end_map),
                     pl.BlockSpec((None, SUBLANES, S5_COLS), end_map)]
        args += [h0[0], h0[1]]
    if prev is not None:
        in_specs.append(pl.BlockSpec(memory_space=pl.ANY))
        args.append(prev)
        aliases = {len(args) - 1: 0}
    kern = functools.partial(_s5_kernel, seg=seg, chained=chained)
    if prev is not None:
        kern = _drop_inputs(kern, len(args) - 1, 1)
    return pl.pallas_call(
        kern,
        grid=grid,
        in_specs=in_specs,
        out_specs=[pl.BlockSpec((rows, S5_WIDTH), row_map),
                   pl.BlockSpec((None, SUBLANES, S5_COLS), end_map),
                   pl.BlockSpec((None, SUBLANES, S5_COLS), end_map)],
        out_shape=[jax.ShapeDtypeStruct((N_TOK, S5_WIDTH), BF16),
                   jax.ShapeDtypeStruct((n_end, SUBLANES, S5_COLS), F32),
                   jax.ShapeDtypeStruct((n_end, SUBLANES, S5_COLS), F32)],
        scratch_shapes=scratch,
        input_output_aliases=aliases,
        compiler_params=_params(*sem),
        name="s5_chained" if chained else "s5_streams",
    )(*args)


def _drop_inputs(kern, start, n):
    def wrapped(*refs):
        return kern(*refs[:start], *refs[start + n:])
    return wrapped


def _gmlp_kernel(*refs, chunk, emit_vn):
    if emit_vn:
        x_ref, ng_ref, win_ref, lng_ref, lnb_ref, ws_ref, bst_ref, og_ref, o_ref, vn_ref, acc_ref = refs
    else:
        x_ref, ng_ref, win_ref, lng_ref, lnb_ref, ws_ref, bst_ref, og_ref, o_ref, acc_ref = refs
    rows = x_ref.shape[0]
    uv = _dot(_rms(x_ref[...], ng_ref[...]).astype(BF16), win_ref[...])
    u = jax.nn.gelu(uv[:, :GM_WIDTH])
    v = jax.nn.gelu(uv[:, GM_WIDTH:])
    tri = (lax.broadcasted_iota(jnp.int32, (chunk, chunk), 0)
           >= lax.broadcasted_iota(jnp.int32, (chunk, chunk), 1))
    for h in range(GM_HEADS):
        cols = slice(h * GM_HEAD_DIM, (h + 1) * GM_HEAD_DIM)
        vh = v[:, cols]
        mu = jnp.mean(vh, axis=-1, keepdims=True)
        cen = vh - mu
        var = jnp.mean(cen * cen, axis=-1, keepdims=True)
        vn = cen * lax.rsqrt(var + EPS) * lng_ref[:, cols] + lnb_ref[:, cols]
        if emit_vn:
            vn_ref[:, cols] = vn
        w = jnp.where(tri, ws_ref[h], 0.0).astype(BF16)
        vb = vn.astype(BF16)
        bias = bst_ref[:, h:h + 1]
        for c in range(rows // chunk):
            rs = slice(c * chunk, (c + 1) * chunk)
            acc_ref[rs, cols] = u[rs, cols] * (_dot(w, vb[rs]) + bias)
    o_ref[...] = _rms(acc_ref[...], og_ref[...]).astype(BF16)


def _gmlp_call(x, x_first, norm_g, w_in, ln_g, ln_b, w_s, b_s, og, rows, chunk, first_block, n_blocks, emit_vn,
               prev=None):
    const2 = lambda i: (0, 0)
    in_specs = [pl.BlockSpec((rows, D_MODEL), lambda i: (x_first + i, 0)),
                pl.BlockSpec((1, D_MODEL), const2),
                pl.BlockSpec((D_MODEL, 2 * GM_WIDTH), const2),
                pl.BlockSpec((1, GM_WIDTH), const2),
                pl.BlockSpec((1, GM_WIDTH), const2),
                pl.BlockSpec((GM_HEADS, chunk, chunk), lambda i: (0, 0, 0)),
                pl.BlockSpec((chunk, GM_HEADS), const2),
                pl.BlockSpec((1, GM_WIDTH), const2)]
    args = [x, norm_g.reshape(1, D_MODEL), w_in[:, S5_WIDTH:], ln_g.reshape(1, GM_WIDTH), ln_b.reshape(1, GM_WIDTH),
            w_s[:, :chunk, :chunk], b_s[:, :chunk].T, og.reshape(1, GM_WIDTH)]
    out_specs = [pl.BlockSpec((rows, GM_WIDTH), lambda i: (first_block + i, 0))]
    out_shape = [jax.ShapeDtypeStruct((N_TOK, GM_WIDTH), BF16)]
    if emit_vn:
        out_specs.append(pl.BlockSpec((rows, GM_WIDTH), lambda i: (i, 0)))
        out_shape.append(jax.ShapeDtypeStruct((n_blocks * rows, GM_WIDTH), F32))
    kern = functools.partial(_gmlp_kernel, chunk=chunk, emit_vn=emit_vn)
    aliases = {}
    if prev is not None:
        in_specs.append(pl.BlockSpec(memory_space=pl.ANY))
        args.append(prev)
        aliases = {len(args) - 1: 0}
        kern = _drop_inputs(kern, len(args) - 1, 1)
    return pl.pallas_call(
        kern,
        grid=(n_blocks,),
        in_specs=in_specs,
        out_specs=out_specs,
        out_shape=out_shape,
        scratch_shapes=[pltpu.VMEM((rows, GM_WIDTH), F32)],
        input_output_aliases=aliases,
        compiler_params=_params("parallel"),
        name="gmlp_vn" if emit_vn else "gmlp",
    )(*args)


def _post_mix_kernel(x_ref, a_ref, b_ref, wa_ref, wb_ref, xg_ref, wq_ref, k_ref, v_ref, wo_ref,
                     fg_ref, wt_ref, rb_ref, x2_ref, h_ref, e_ref, gate_ref, cnt_ref, att_ref, *, n_batch, kv_layer):
    rows = x_ref.shape[0]
    tb = rows // n_batch
    x = x_ref[...] + _dot(a_ref[...], wa_ref[...]) + _dot(b_ref[...], wb_ref[...])
    q = _dot(_rms(x, xg_ref[...]).astype(BF16), wq_ref[...]).astype(BF16)
    scale = XA_HEAD_DIM ** -0.5
    for b in range(n_batch):
        rs = slice(b * tb, (b + 1) * tb)
        for h in range(XA_HEADS):
            cols = slice(h * XA_HEAD_DIM, (h + 1) * XA_HEAD_DIM)
            if kv_layer is None:
                k_h, v_h = k_ref[b, :, cols], v_ref[b, :, cols]
            else:
                k_h, v_h = k_ref[b, :, h, :].astype(BF16), v_ref[b, :, h, :].astype(BF16)
            s = lax.dot_general(q[rs, cols], k_h, (((1,), (1,)), ((), ())),
                                preferred_element_type=F32) * scale
            e = jnp.exp(s - jnp.max(s, axis=-1, keepdims=True))
            p = e / jnp.sum(e, axis=-1, keepdims=True)
            att_ref[rs, cols] = _dot(p.astype(BF16), v_h).astype(BF16)
    x = x + _dot(att_ref[...], wo_ref[...])
    x2_ref[...] = x
    _route_rows(x, fg_ref, wt_ref, rb_ref, h_ref, e_ref, gate_ref, cnt_ref)


def _post_mix(x, x_first, a, b, w_out, xg, wq, k, v, wo, fg, rw, rb,
              rows, n_batch, first_block, n_blocks, blocks_per_kv, prev=None, kv_layer=None):
    const2 = lambda i: (0, 0)
    if kv_layer is None:
        kv_spec = pl.BlockSpec((n_batch, N_MEM, D_MODEL), lambda i: (i // blocks_per_kv, 0, 0))
    else:
        kv_spec = pl.BlockSpec((None, n_batch, N_MEM, XA_HEADS, XA_HEAD_DIM),
                               lambda i: (kv_layer, i // blocks_per_kv, 0, 0, 0))
    row_map = lambda i: (first_block + i, 0)
    d = D_MODEL
    in_specs = [pl.BlockSpec((rows, d), lambda i: (x_first + i, 0)),
                pl.BlockSpec((rows, S5_WIDTH), row_map),
                pl.BlockSpec((rows, GM_WIDTH), row_map),
                pl.BlockSpec((S5_WIDTH, d), const2),
                pl.BlockSpec((GM_WIDTH, d), const2),
                pl.BlockSpec((1, d), const2),
                pl.BlockSpec((d, d), const2),
                kv_spec, kv_spec,
                pl.BlockSpec((d, d), const2),
                pl.BlockSpec((1, d), const2),
                pl.BlockSpec((N_EXPERTS, d), const2),
                pl.BlockSpec((N_EXPERTS, 1), const2)]
    args = [x, a, b, w_out[:S5_WIDTH], w_out[S5_WIDTH:], xg.reshape(1, d), wq, k, v, wo,
            fg.reshape(1, d), rw.T.astype(BF16), rb.reshape(N_EXPERTS, 1)]
    kern = functools.partial(_post_mix_kernel, n_batch=n_batch, kv_layer=kv_layer)
    aliases = {}
    if prev is not None:
        n_in = len(args)
        in_specs += [pl.BlockSpec(memory_space=pl.ANY)] * len(prev)
        args += list(prev)
        aliases = {n_in + o: o for o in range(len(prev))}
        kern = _drop_inputs(kern, n_in, len(prev))
    return pl.pallas_call(
        kern,
        grid=(n_blocks,),
        in_specs=in_specs,
        out_specs=[pl.BlockSpec((rows, d), row_map),
                   pl.BlockSpec((rows // SUBLANES,) + PIECE_TILE, lambda i: (first_block + i, 0, 0, 0)),
                   pl.BlockSpec((TOP_K, rows), lambda i: (0, first_block + i)),
                   pl.BlockSpec((TOP_K, rows), lambda i: (0, first_block + i)),
                   pl.BlockSpec((None, N_EXPERTS, LANES), lambda i: (i, 0, 0))],
        out_shape=[jax.ShapeDtypeStruct((N_TOK, d), F32),
                   jax.ShapeDtypeStruct((N_TOK // SUBLANES,) + PIECE_TILE, F32),
                   jax.ShapeDtypeStruct((TOP_K, N_TOK), jnp.int32),
                   jax.ShapeDtypeStruct((TOP_K, N_TOK), F32),
                   jax.ShapeDtypeStruct((n_blocks, N_EXPERTS, LANES), F32)],
        scratch_shapes=[pltpu.VMEM((rows, d), BF16)],
        input_output_aliases=aliases,
        compiler_params=_params("parallel"),
        name="post_mix",
    )(*args)


def _moe_kernel(be_ref, nu_ref, first_ref, slot_ref, nxt_ref, parts_ref,
                x_ref, wg_hbm, wu_hbm, wd_hbm, bg_ref, bu_ref, bd_ref, o_ref,
                wbuf_ref, wgb_ref, wub_ref, wdb_ref, sem, *, layer):
    i = pl.program_id(0)
    active = i < nu_ref[0]

    def copies(e, s):
        return [pltpu.make_async_copy(w.at[layer, e], wbuf_ref.at[s, j], sem.at[s, j])
                for j, w in enumerate((wg_hbm, wu_hbm, wd_hbm))]

    @pl.when(jnp.logical_and(active, first_ref[i] == 1))
    def _():
        s = slot_ref[i]

        @pl.when(i == 0)
        def _():
            for c in copies(be_ref[0], s):
                c.start()

        @pl.when(nxt_ref[i] >= 0)
        def _():
            for c in copies(nxt_ref[i], 1 - s):
                c.start()

        for c in copies(be_ref[i], s):
            c.wait()

        def cast(r, _):
            rs = pl.ds(pl.multiple_of(r * MOE_CAST_ROWS, MOE_CAST_ROWS), MOE_CAST_ROWS)
            wgb_ref[rs, :] = wbuf_ref[s, 0, rs, :].astype(BF16)
            wub_ref[rs, :] = wbuf_ref[s, 1, rs, :].astype(BF16)
            wdb_ref[rs, :] = wbuf_ref[s, 2, rs, :].astype(BF16)
            return 0
        lax.fori_loop(0, D_MODEL // MOE_CAST_ROWS, cast, 0)

    def experts(n_rows):
        words = lax.bitcast_convert_type(
            jnp.concatenate([x_ref[:n_rows // SUBLANES, c].reshape(n_rows, LANES) for c in range(PIECES_PER_ROW)],
                            axis=1), jnp.uint32)
        x = jnp.concatenate([lax.bitcast_convert_type(words << 16, F32),
                             lax.bitcast_convert_type(words & jnp.uint32(HIGH_HALF), F32)], axis=1).astype(BF16)
        g = jnp.minimum(_dot(x, wgb_ref[...]) + bg_ref[...], SWIGLU_LIMIT)
        u = jnp.clip(_dot(x, wub_ref[...]) + bu_ref[...], -SWIGLU_LIMIT, SWIGLU_LIMIT)
        a = g * jax.nn.sigmoid(SWIGLU_ALPHA * g) * (u + 1.0)
        y = _dot(a.astype(BF16), wdb_ref[...]) + bd_ref[...]
        for c in range(OUT_PIECES):
            o_ref[:n_rows // SUBLANES, c] = y[:, c * LANES:(c + 1) * LANES].reshape(n_rows // SUBLANES, SUBLANES, LANES)

    for p in range(1, MOE_ROWS // MOE_PART + 1):
        pl.when(jnp.logical_and(active, parts_ref[i] == p))(functools.partial(experts, p * MOE_PART))


def _moe_call(layer, xs, sched, w_gate, b_gate, w_up, b_up, w_down, b_down):
    def piece_map(i, be, nu, *_):
        return (jnp.minimum(i, nu[0] - 1), 0, 0, 0)

    def b_map(i, be, *_):
        return (layer, be[i], 0, 0)

    b_spec = pl.BlockSpec((None, None, 1, D_FF), b_map)
    w_spec = pl.BlockSpec(memory_space=pl.ANY)
    grid_spec = pltpu.PrefetchScalarGridSpec(
        num_scalar_prefetch=6,
        grid=(MOE_BLOCKS,),
        in_specs=[pl.BlockSpec((MOE_ROWS // SUBLANES,) + PIECE_TILE, piece_map),
                  w_spec, w_spec, w_spec, b_spec, b_spec, b_spec],
        out_specs=pl.BlockSpec((MOE_ROWS // SUBLANES,) + OUT_TILE, piece_map),
        scratch_shapes=[pltpu.VMEM((2, 3, D_MODEL, D_FF), F32)]
        + [pltpu.VMEM((D_MODEL, D_FF), BF16)] * 3
        + [pltpu.SemaphoreType.DMA((2, 3))],
    )
    shape4 = (DEPTH, N_EXPERTS, 1, D_FF)
    return pl.pallas_call(
        functools.partial(_moe_kernel, layer=layer),
        grid_spec=grid_spec,
        out_shape=jax.ShapeDtypeStruct((MOE_ROWS_TOTAL // SUBLANES,) + OUT_TILE, F32),
        compiler_params=pltpu.CompilerParams(dimension_semantics=("arbitrary",),
                                             vmem_limit_bytes=MOE_VMEM_LIMIT),
        name="moe_experts",
    )(*sched, xs, w_gate, w_up, w_down, b_gate.reshape(shape4), b_up.reshape(shape4),
      b_down.reshape(shape4))


DISPATCH_TOKENS = 48
DMA_PIECES = 64
DISPATCH_DMAS = DISPATCH_TOKENS * PIECES_PER_ROW // DMA_PIECES
DISPATCH_CHUNKS = N_TOK // SC_WORKERS // DISPATCH_TOKENS
IDX_ROWS = 16
SC_LANES = 16
MOE_ROWS_TOTAL = MOE_BLOCKS * MOE_ROWS
TILE_PIECES = PIECES_PER_ROW * SUBLANES


def _piece_index_rows(p_v, idx_v, row0, tok0, n_dma, pieces_per_row, dma_pieces):
    lane = lax.iota(jnp.int32, SC_LANES)
    in_tile = lane & (SUBLANES - 1)
    piece = (lane >> 3) * SUBLANES
    per_vec = SC_LANES // SUBLANES
    tile_pieces = pieces_per_row * SUBLANES
    tiles_per_dma = dma_pieces // tile_pieces
    for m in range(n_dma):
        for tr in range(tiles_per_dma):
            first = plsc.load_gather(p_v, [in_tile + (tok0 + (m * tiles_per_dma + tr) * SUBLANES)])
            for v in range(pieces_per_row // per_vec):
                idx_v[row0 + m, pl.ds(tr * tile_pieces + v * SC_LANES, SC_LANES)] = (
                    first + (piece + v * per_vec * SUBLANES))


COLLECT_ROWS = 48
COLLECT_PIECES = 128
OUT_PIECES = D_MODEL // LANES
OUT_TILE = (OUT_PIECES, SUBLANES, LANES)


def _collect(ys4, dest_t):
    first_piece = ((dest_t // SUBLANES) * (OUT_PIECES * SUBLANES) + dest_t % SUBLANES).reshape(-1)
    chunk_pieces = COLLECT_ROWS * OUT_PIECES
    n_dma = chunk_pieces // COLLECT_PIECES
    chunks_per_choice = N_TOK // SC_WORKERS // COLLECT_ROWS
    n_chunks = TOP_K * chunks_per_choice
    mesh = plsc.VectorSubcoreMesh(core_axis_name="c", subcore_axis_name="s")

    @functools.partial(
        pl.kernel, mesh=mesh,
        out_type=jax.ShapeDtypeStruct((TOP_K * N_TOK * OUT_PIECES, LANES), F32),
        scratch_types=[pltpu.VMEM((chunk_pieces, LANES), F32), pltpu.VMEM((chunk_pieces, LANES), F32),
                       pltpu.VMEM((SUBLANES, COLLECT_PIECES), jnp.int32),
                       pltpu.VMEM((SUBLANES, COLLECT_PIECES), jnp.int32),
                       pltpu.VMEM((COLLECT_ROWS,), jnp.int32),
                       pltpu.SemaphoreType.DMA, pltpu.SemaphoreType.DMA],
        compiler_params=pltpu.CompilerParams(needs_layout_passes=False),
    )
    def gather_rows(y_hbm, p_hbm, o_hbm, rows0, rows1, idx0, idx1, p_v, sem0, sem1):
        wid = lax.axis_index("s") * SC_CORES + lax.axis_index("c")
        bufs = ((rows0, idx0, sem0), (rows1, idx1, sem1))

        def first_row(j):
            k = j // chunks_per_choice
            return k * N_TOK + (wid * chunks_per_choice + (j - k * chunks_per_choice)) * COLLECT_ROWS

        def copies(b):
            rows_v, idx_v, sem = bufs[b]
            return [pltpu.make_async_copy(y_hbm.at[idx_v.at[m]],
                                          rows_v.at[pl.ds(m * COLLECT_PIECES, COLLECT_PIECES)], sem)
                    for m in range(n_dma)]

        def fire(j, b):
            pltpu.sync_copy(p_hbm.at[pl.ds(pl.multiple_of(first_row(j), SUBLANES), COLLECT_ROWS)], p_v)
            _piece_index_rows(p_v, bufs[b][1], 0, 0, n_dma, OUT_PIECES, COLLECT_PIECES)
            for cp in copies(b):
                cp.start()

        def drain(j, b):
            for cp in copies(b):
                cp.wait()
            pltpu.sync_copy(bufs[b][0],
                            o_hbm.at[pl.ds(pl.multiple_of(first_row(j) * OUT_PIECES, SUBLANES), chunk_pieces)])

        fire(0, 0)

        @pl.loop(0, n_chunks, step=2)
        def _(j):
            fire(j + 1, 1)
            drain(j, 0)
            pl.when(j + 2 < n_chunks)(lambda: fire(j + 2, 0))
            drain(j + 1, 1)

    picked = gather_rows(ys4.reshape(MOE_ROWS_TOTAL * OUT_PIECES, LANES), first_piece)
    return picked.reshape((TOP_K, N_TOK // SUBLANES) + OUT_TILE)


def _dispatch(h4, dest_t):
    first_piece = ((dest_t // SUBLANES) * TILE_PIECES + dest_t % SUBLANES).reshape(-1)
    chunk_pieces = DISPATCH_TOKENS * PIECES_PER_ROW
    mesh = plsc.VectorSubcoreMesh(core_axis_name="c", subcore_axis_name="s")

    @functools.partial(
        pl.kernel, mesh=mesh,
        out_type=jax.ShapeDtypeStruct((MOE_ROWS_TOTAL * PIECES_PER_ROW, LANES), F32),
        scratch_types=[pltpu.VMEM((chunk_pieces, LANES), F32), pltpu.VMEM((IDX_ROWS, DMA_PIECES), jnp.int32),
                       pltpu.VMEM((TOP_K * DISPATCH_TOKENS,), jnp.int32), pltpu.SemaphoreType.DMA],
        compiler_params=pltpu.CompilerParams(needs_layout_passes=False),
    )
    def scatter_rows(h_hbm, p_hbm, o_hbm, rows_v, idx_v, p_v, sem):
        wid = lax.axis_index("s") * SC_CORES + lax.axis_index("c")

        @pl.loop(0, DISPATCH_CHUNKS)
        def _(j):
            q = wid * DISPATCH_CHUNKS + j
            pltpu.sync_copy(h_hbm.at[pl.ds(pl.multiple_of(q * chunk_pieces, SUBLANES), chunk_pieces)], rows_v)
            for k in range(TOP_K):
                pltpu.sync_copy(
                    p_hbm.at[pl.ds(pl.multiple_of(k * N_TOK + q * DISPATCH_TOKENS, SUBLANES), DISPATCH_TOKENS)],
                    p_v.at[pl.ds(k * DISPATCH_TOKENS, DISPATCH_TOKENS)])
            for k in range(TOP_K):
                _piece_index_rows(p_v, idx_v, k * DISPATCH_DMAS, k * DISPATCH_TOKENS, DISPATCH_DMAS,
                                  PIECES_PER_ROW, DMA_PIECES)
            copies = [pltpu.async_copy(rows_v.at[pl.ds(m * DMA_PIECES, DMA_PIECES)],
                                       o_hbm.at[idx_v.at[k * DISPATCH_DMAS + m]], sem)
                      for k in range(TOP_K) for m in range(DISPATCH_DMAS)]
            for cp in copies:
                cp.wait()

    xs = scatter_rows(h4.reshape(N_TOK * PIECES_PER_ROW, LANES), first_piece)
    return xs.reshape((MOE_ROWS_TOTAL // SUBLANES,) + PIECE_TILE)


def _route(e_t, cnt_prompt, cnt_sample):
    cnt_sample = jnp.sum(cnt_sample[:, :, 0].reshape(SAMPLE_BLOCKS, -1, N_EXPERTS), axis=1)
    blk_cnt = jnp.concatenate([cnt_prompt[:, :, 0], cnt_sample]).astype(jnp.int32)
    def before(n):
        i = jnp.arange(n, dtype=jnp.int32)
        return i[None, :] < i[:, None]

    counts = jnp.sum(blk_cnt, axis=0)
    padded = ((counts + MOE_ROWS - 1) // MOE_ROWS) * MOE_ROWS
    pstart = jnp.sum(jnp.where(before(N_EXPERTS), padded[None, :], 0), axis=1)
    pend = pstart + padded
    base = pstart[None, :] + jnp.sum(jnp.where(before(blk_cnt.shape[0])[:, :, None], blk_cnt[None], 0), axis=1)
    dest_t = _dest(e_t, base.astype(F32)[:, :, None])
    n_used = jnp.sum(padded) // MOE_ROWS
    ids = jnp.arange(MOE_BLOCKS, dtype=jnp.int32)
    blk = jnp.minimum(ids, n_used - 1) * MOE_ROWS
    blk_exp = jnp.minimum(jnp.sum((pend[None, :] <= blk[:, None]).astype(jnp.int32), axis=1), N_EXPERTS - 1)
    eid = jnp.arange(N_EXPERTS, dtype=jnp.int32)
    of_blk = blk_exp[:, None] == eid[None, :]

    def lookup(table):
        return jnp.sum(jnp.where(of_blk, table[None, :], 0), axis=1)

    first = jnp.logical_and(ids < n_used, jnp.logical_or(ids == 0, blk_exp != jnp.roll(blk_exp, 1)))
    first = first.astype(jnp.int32)
    slot = jnp.sum(jnp.where(before(MOE_BLOCKS), first[None, :], 0), axis=1) % 2
    later = jnp.logical_and(eid[None, :] > eid[:, None], (counts > 0)[None, :])
    nxt_e = jnp.min(jnp.where(later, eid[None, :], N_EXPERTS), axis=1)
    nxt = lookup(jnp.where(nxt_e < N_EXPERTS, nxt_e, -1))
    real = jnp.clip(lookup(pstart + counts) - blk, 0, MOE_ROWS)
    parts = (real + MOE_PART - 1) // MOE_PART
    return dest_t, (blk_exp, n_used.reshape(1).astype(jnp.int32), first, slot.astype(jnp.int32),
                    nxt.astype(jnp.int32), parts.astype(jnp.int32))


def kernel(x_prompt, x_sample, mem_prompt, state_s5_re, state_s5_im, cache_mem_k, cache_mem_v, norm_mix_g, w_in, s5_lambda_re, s5_lambda_im, s5_b_re, s5_b_im, s5_c_re, s5_c_im, s5_d, s5_log_step, s5_w_glu, gm_ln_g, gm_ln_b, gm_w_s, gm_b_s, out_norm_g, w_out, xa_norm_g, mem_norm_g, w_xq, w_xk, w_xv, w_xo, ffn_norm_g, router_w, router_b, e_w_gate, e_b_gate, e_w_up, e_b_up, e_w_down, e_b_down, final_norm_g):
    xs = (x_prompt.reshape(N_PROMPT, D_MODEL), x_sample.reshape(N_SAMPLE, D_MODEL))
    mem = mem_prompt.reshape(BATCH * N_MEM, D_MODEL)
    outs = {k: [] for k in ('s5r_p', 's5i_p', 's5r_s', 's5i_s', 'gmv_s')}
    mem_kv = None
    for l in range(DEPTH):
        unified = len(xs) == 1

        def sample_first(rows):
            return N_PROMPT // rows if unified else 0

        in_proj = (norm_mix_g[l], w_in[l].astype(BF16))
        sw = _s5_weights(s5_lambda_re[l], s5_lambda_im[l], s5_b_re[l], s5_b_im[l], s5_c_re[l], s5_c_im[l],
                         s5_d[l], s5_log_step[l], S5_SEG_PROMPT)
        w_glu = s5_w_glu[l].astype(BF16)
        og = out_norm_g[l]
        ms5, er_p, ei_p = _s5_call(xs[0], 0, *in_proj, sw, w_glu, og[:S5_WIDTH], S5_SEG_PROMPT, True)
        h0 = (state_s5_re[l].reshape(DEC_BATCH // SUBLANES, SUBLANES, S5_COLS),
              state_s5_im[l].reshape(DEC_BATCH // SUBLANES, SUBLANES, S5_COLS))
        ms5, er_s, ei_s = _s5_call(xs[-1], sample_first(DEC_SEQ * SUBLANES), *in_proj, sw, w_glu, og[:S5_WIDTH],
                                   DEC_SEQ, False, h0=h0, prev=ms5)
        outs['s5r_p'].append(er_p[:, SUBLANES - 1].reshape(BATCH, S5_GROUPS, S5_STATE))
        outs['s5i_p'].append(ei_p[:, SUBLANES - 1].reshape(BATCH, S5_GROUPS, S5_STATE))
        outs['s5r_s'].append(er_s.reshape(DEC_BATCH, S5_GROUPS, S5_STATE))
        outs['s5i_s'].append(ei_s.reshape(DEC_BATCH, S5_GROUPS, S5_STATE))

        (mgm,) = _gmlp_call(xs[0], 0, *in_proj, gm_ln_g[l], gm_ln_b[l], gm_w_s[l], gm_b_s[l], og[S5_WIDTH:],
                            GM_ROWS_PROMPT, GM_CHUNK, 0, N_PROMPT // GM_ROWS_PROMPT, False)
        mgm, vn = _gmlp_call(xs[-1], sample_first(N_SAMPLE), *in_proj, gm_ln_g[l], gm_ln_b[l], gm_w_s[l],
                             gm_b_s[l], og[S5_WIDTH:], N_SAMPLE, DEC_SEQ, N_PROMPT // N_SAMPLE, 1, True, prev=mgm)
        outs['gmv_s'].append(vn.reshape(DEC_BATCH, DEC_SEQ, GM_HEADS, GM_HEAD_DIM))
        *mem_kv, mk, mv = _mem_kv(mem, mem_norm_g[l], w_xk[l].astype(BF16), w_xv[l].astype(BF16), l, prev=mem_kv)
        shared = (w_out[l].astype(BF16), xa_norm_g[l], w_xq[l].astype(BF16))
        tail = (w_xo[l].astype(BF16), ffn_norm_g[l], router_w[l], router_b[l])
        rows_s = XA_SAMPLE_BATCHES * DEC_SEQ
        post_p = _post_mix(xs[0], 0, ms5, mgm, *shared, mk, mv, *tail,
                           XA_ROWS_PROMPT, 1, 0, N_PROMPT // XA_ROWS_PROMPT, SEQ // XA_ROWS_PROMPT)
        post_s = _post_mix(xs[-1], sample_first(rows_s), ms5, mgm, *shared,
                           cache_mem_k, cache_mem_v, *tail,
                           rows_s, XA_SAMPLE_BATCHES, N_PROMPT // rows_s, N_SAMPLE // rows_s, 1,
                           prev=post_p[:4], kv_layer=l)
        x, h, e_t, gates_t = post_s[:4]

        dest_t, sched = _route(e_t, post_p[4], post_s[4])
        ys = _moe_call(l, _dispatch(h, dest_t), sched, e_w_gate, e_b_gate, e_w_up, e_b_up, e_w_down, e_b_down)
        x = _combine(x, _collect(ys, dest_t), gates_t.T, final_norm_g if l == DEPTH - 1 else None)
        xs = (x,)

    y_prompt = x[0].reshape(BATCH, SEQ, D_MODEL)
    y_sample = x[1].reshape(DEC_BATCH, DEC_SEQ, D_MODEL)
    st = jnp.stack
    return (y_prompt, y_sample, st(outs['s5r_p']), st(outs['s5i_p']), mem_kv[0], mem_kv[1],
            st(outs['s5r_s']), st(outs['s5i_s']), st(outs['gmv_s']))
```

```python
import functools

import jax
import jax.numpy as jnp
from jax import lax
from jax.experimental import pallas as pl
from jax.experimental.pallas import tpu as pltpu
from jax.experimental.pallas import tpu_sc as plsc

F32 = jnp.float32
BF16 = jnp.bfloat16

D_MODEL = 1024
BATCH = 4
SEQ = 4096
DEPTH = 2
DEC_BATCH = 16
DEC_SEQ = 32
S5_GROUPS = 32
S5_GROUP_CH = 16
S5_STATE = 64
S5_WIDTH = S5_GROUPS * S5_GROUP_CH
S5_COLS = S5_GROUPS * S5_STATE
GM_HEADS = 4
GM_HEAD_DIM = 128
GM_WIDTH = GM_HEADS * GM_HEAD_DIM
GM_CHUNK = 128
N_MEM = 256
XA_HEADS = 4
XA_HEAD_DIM = D_MODEL // XA_HEADS
N_EXPERTS = 32
TOP_K = 4
D_FF = D_MODEL
SWIGLU_LIMIT = 7.0
SWIGLU_ALPHA = 1.702
EPS = 1e-6
EIG_CLIP = -1e-4

N_PROMPT = BATCH * SEQ
N_SAMPLE = DEC_BATCH * DEC_SEQ
N_TOK = N_PROMPT + N_SAMPLE

SUBLANES = 8
LANES = 128
MXU_DIM = 256
VMEM_LIMIT = 48 * 1024 * 1024
MOE_VMEM_LIMIT = 60 * 1024 * 1024
SC_CORES = 2
SC_SUBCORES = 16
SC_WORKERS = SC_CORES * SC_SUBCORES

PACK_WIDTH = D_MODEL // 2
PIECES_PER_ROW = PACK_WIDTH // LANES
PIECE_TILE = (PIECES_PER_ROW, SUBLANES, LANES)
HIGH_HALF = 0xFFFF0000

ROW_BLOCK = 512
S5_SEG_PROMPT = 64
S5_COL_BLOCK = 1024
GM_ROWS_PROMPT = 512
XA_ROWS_PROMPT = 512
XA_SAMPLE_BATCHES = 4
MOE_ROWS = 1024
MOE_PART = 256
MOE_BLOCKS = -(-N_TOK * TOP_K // MOE_ROWS) + N_EXPERTS
MOE_CAST_ROWS = 128


def _params(*sem):
    return pltpu.CompilerParams(dimension_semantics=sem, vmem_limit_bytes=VMEM_LIMIT)


def _rms(x, g):
    return x * lax.rsqrt(jnp.mean(x * x, axis=-1, keepdims=True) + EPS) * g


def _dot(a, b):
    return jnp.dot(a, b, preferred_element_type=F32)


PROMPT_BLOCKS = N_PROMPT // ROW_BLOCK
SAMPLE_BLOCKS = N_SAMPLE // ROW_BLOCK


def _group_specs():
    return [pl.BlockSpec((ROW_BLOCK, D_MODEL), lambda i: (jnp.minimum(i, PROMPT_BLOCKS - 1), 0)),
            pl.BlockSpec((ROW_BLOCK, D_MODEL), lambda i: (jnp.maximum(i - PROMPT_BLOCKS, 0), 0))]


def _for_group(refs, body):
    i = pl.program_id(0)
    pl.when(i < PROMPT_BLOCKS)(lambda: body(refs[0]))
    pl.when(i >= PROMPT_BLOCKS)(lambda: body(refs[1]))


def _mem_kv_kernel(m_ref, g_ref, wk_ref, wv_ref, k5_ref, v5_ref, kb_ref, vb_ref):
    m = _rms(m_ref[...], g_ref[...]).astype(BF16)
    for w_ref, o5_ref, ob_ref in ((wk_ref, k5_ref, kb_ref), (wv_ref, v5_ref, vb_ref)):
        kv = _dot(m, w_ref[...])
        ob_ref[...] = kv.astype(BF16)
        for h in range(XA_HEADS):
            o5_ref[:, h, :] = kv[:, h * XA_HEAD_DIM:(h + 1) * XA_HEAD_DIM]


def _mem_kv(mem, g, wk, wv, layer, prev=None):
    d = D_MODEL
    const2 = lambda b: (0, 0)
    in_specs = [pl.BlockSpec((N_MEM, d), lambda b: (b, 0)),
                pl.BlockSpec((1, d), const2),
                pl.BlockSpec((d, d), const2),
                pl.BlockSpec((d, d), const2)]
    args = [mem, g.reshape(1, d), wk, wv]
    kern = _mem_kv_kernel
    aliases = {}
    if prev is not None:
        in_specs += [pl.BlockSpec(memory_space=pl.ANY)] * 2
        args += list(prev)
        aliases = {4: 0, 5: 1}
        kern = _drop_inputs(kern, 4, 2)
    out5 = pl.BlockSpec((None, None, N_MEM, XA_HEADS, XA_HEAD_DIM), lambda b: (layer, b, 0, 0, 0))
    outb = pl.BlockSpec((None, N_MEM, d), lambda b: (b, 0, 0))
    shape5 = jax.ShapeDtypeStruct((DEPTH, BATCH, N_MEM, XA_HEADS, XA_HEAD_DIM), F32)
    shapeb = jax.ShapeDtypeStruct((BATCH, N_MEM, d), BF16)
    return pl.pallas_call(
        kern,
        grid=(BATCH,),
        in_specs=in_specs,
        out_specs=[out5, out5, outb, outb],
        out_shape=[shape5, shape5, shapeb, shapeb],
        input_output_aliases=aliases,
        compiler_params=_params("parallel"),
        name="mem_kv",
    )(*args)


def _route_rows(x, g_ref, wt_ref, b_ref, h_ref, e_ref, gate_ref, cnt_ref):
    h = _rms(x, g_ref[...]).astype(BF16)
    bits = lax.bitcast_convert_type(h.astype(F32), jnp.uint32)
    words = lax.bitcast_convert_type(bits[:, PACK_WIDTH:] | (bits[:, :PACK_WIDTH] >> 16), F32)
    for c in range(PIECES_PER_ROW):
        h_ref[:, c] = words[:, c * LANES:(c + 1) * LANES].reshape(h.shape[0] // SUBLANES, SUBLANES, LANES)
    logits = lax.dot_general(wt_ref[...], h, (((1,), (1,)), ((), ())),
                             preferred_element_type=F32) + b_ref[...]
    sub = lax.broadcasted_iota(jnp.int32, logits.shape, 0)
    work = logits
    chosen = jnp.zeros(logits.shape, F32)
    vals, idxs = [], []
    for _ in range(TOP_K):
        m = jnp.max(work, axis=0, keepdims=True)
        idx = jnp.min(jnp.where(work == m, sub, N_EXPERTS), axis=0, keepdims=True)
        sel = sub == idx
        vals.append(m)
        idxs.append(idx)
        work = jnp.where(sel, -jnp.inf, work)
        chosen = jnp.where(sel, 1.0, chosen)
    ex = [jnp.exp(v - vals[0]) for v in vals]
    den = ex[0] + ex[1] + ex[2] + ex[3]
    gate_ref[...] = jnp.concatenate([e / den for e in ex], axis=0)
    e_ref[...] = jnp.concatenate(idxs, axis=0)
    cnt_ref[...] = jnp.broadcast_to(jnp.sum(chosen, axis=1, keepdims=True), (N_EXPERTS, LANES))


def _dest_kernel(e_ref, base_ref, d_ref):
    e = e_ref[...]
    rows = e.shape[1]
    sub = lax.broadcasted_iota(jnp.int32, (N_EXPERTS, rows), 0)
    hot = [sub == e[k:k + 1, :] for k in range(TOP_K)]
    chosen = sum(jnp.where(hk, 1.0, 0.0) for hk in hot)
    earlier = (lax.broadcasted_iota(jnp.int32, (rows, rows), 0)
               < lax.broadcasted_iota(jnp.int32, (rows, rows), 1))
    pos = base_ref[...] + _dot(chosen.astype(BF16), jnp.where(earlier, 1.0, 0.0).astype(BF16))
    d_ref[...] = jnp.concatenate(
        [jnp.sum(jnp.where(hk, pos, 0.0), axis=0, keepdims=True) for hk in hot], axis=0).astype(jnp.int32)


def _dest(e_t, base):
    n = e_t.shape[1]
    rows = ROW_BLOCK
    return pl.pallas_call(
        _dest_kernel,
        grid=(n // rows,),
        in_specs=[pl.BlockSpec((TOP_K, rows), lambda i: (0, i)),
                  pl.BlockSpec((None, N_EXPERTS, 1), lambda i: (i, 0, 0))],
        out_specs=pl.BlockSpec((TOP_K, rows), lambda i: (0, i)),
        out_shape=jax.ShapeDtypeStruct((TOP_K, n), jnp.int32),
        compiler_params=_params("parallel"),
        name="moe_dest",
    )(e_t, base)


def _combine_kernel(*refs, final):
    if final:
        x_ref, g0, g1, g2, g3, gate_ref, fg_ref, op_ref, os_ref = refs
    else:
        x_ref, g0, g1, g2, g3, gate_ref, o_ref = refs
    rows = x_ref.shape[0]
    slabs = []
    for c in range(OUT_PIECES):
        acc = x_ref[:, c * LANES:(c + 1) * LANES]
        for k, g in enumerate((g0, g1, g2, g3)):
            acc = acc + gate_ref[:, k:k + 1] * g[:, c].reshape(rows, LANES)
        slabs.append(acc)
    acc = jnp.concatenate(slabs, axis=1)
    if final:
        def store(o_ref):
            o_ref[...] = _rms(acc, fg_ref[...])
        _for_group((op_ref, os_ref), store)
    else:
        o_ref[...] = acc


def _combine(x, picked, gates, final_g=None):
    n, d = x.shape
    rows = ROW_BLOCK
    final = final_g is not None
    in_specs = [pl.BlockSpec((rows, d), lambda i: (i, 0))]
    in_specs += [pl.BlockSpec((None, rows // SUBLANES) + OUT_TILE, functools.partial(lambda i, k: (k, i, 0, 0, 0), k=k))
                 for k in range(TOP_K)]
    in_specs.append(pl.BlockSpec((rows, TOP_K), lambda i: (i, 0)))
    args = [x, picked, picked, picked, picked, gates]
    out_specs = pl.BlockSpec((rows, d), lambda i: (i, 0))
    out_shape = jax.ShapeDtypeStruct((n, d), F32)
    if final:
        in_specs.append(pl.BlockSpec((1, d), lambda i: (0, 0)))
        args.append(final_g.reshape(1, d))
        out_specs = _group_specs()
        out_shape = [jax.ShapeDtypeStruct((N_PROMPT, d), F32), jax.ShapeDtypeStruct((N_SAMPLE, d), F32)]
    return pl.pallas_call(
        functools.partial(_combine_kernel, final=final),
        grid=(n // rows,),
        in_specs=in_specs,
        out_specs=out_specs,
        out_shape=out_shape,
        compiler_params=_params("arbitrary" if final else "parallel"),
        name="moe_combine",
    )(*args)


def _cmul_add(ar, ai, hr, hi, br, bi):
    return ar * hr - ai * hi + br, ar * hi + ai * hr + bi


def _s5_kernel(*refs, seg, chained):
    if chained:
        (x_ref, ng_ref, win_ref, wbr_ref, wbi_ref, wcr_ref, wci_ref, ar_ref, ai_ref, pr_ref, pi_ref, d_ref,
         wglu_ref, og_ref, o_ref, er_ref, ei_ref,
         up_ref, hr_ref, hi_ref, sr_ref, si_ref, op_ref, cr_ref, ci_ref) = refs
    else:
        (x_ref, ng_ref, win_ref, wbr_ref, wbi_ref, wcr_ref, wci_ref, ar_ref, ai_ref, pr_ref, pi_ref, d_ref,
         wglu_ref, og_ref, h0r_ref, h0i_ref, o_ref, er_ref, ei_ref,
         up_ref, hr_ref, hi_ref, sr_ref, si_ref, op_ref) = refs
    rows = seg * SUBLANES
    half = S5_COLS // 2

    u = _dot(_rms(x_ref[...], ng_ref[...]).astype(BF16), win_ref[...])
    n_lane_blk = S5_WIDTH // LANES
    for j in range(SUBLANES):
        for i0 in range(0, seg, SUBLANES):
            for c in range(n_lane_blk):
                up_ref[c, pl.ds(i0 * SUBLANES + j, SUBLANES, stride=SUBLANES), :] = (
                    u[j * seg + i0:j * seg + i0 + SUBLANES, c * LANES:(c + 1) * LANES])

    up = jnp.concatenate([up_ref[c] for c in range(n_lane_blk)], axis=1)
    ub = up.astype(BF16)
    for k in range(2):
        uk = ub[:, k * MXU_DIM:(k + 1) * MXU_DIM]
        hr_ref[:, k * half:(k + 1) * half] = _dot(uk, wbr_ref[k])
        hi_ref[:, k * half:(k + 1) * half] = _dot(uk, wbi_ref[k])

    for cb in range(S5_COLS // S5_COL_BLOCK):
        cols = slice(cb * S5_COL_BLOCK, (cb + 1) * S5_COL_BLOCK)
        ar = jnp.broadcast_to(ar_ref[:, cols], (SUBLANES, S5_COL_BLOCK))
        ai = jnp.broadcast_to(ai_ref[:, cols], (SUBLANES, S5_COL_BLOCK))

        def step(i, carry, cols=cols, ar=ar, ai=ai):
            r = pl.multiple_of(i * SUBLANES, SUBLANES)
            h_r, h_i = _cmul_add(ar, ai, carry[0], carry[1],
                                 hr_ref[pl.ds(r, SUBLANES), cols], hi_ref[pl.ds(r, SUBLANES), cols])
            hr_ref[pl.ds(r, SUBLANES), cols] = h_r
            hi_ref[pl.ds(r, SUBLANES), cols] = h_i
            return h_r, h_i

        zero = jnp.zeros((SUBLANES, S5_COL_BLOCK), F32)
        lax.fori_loop(0, seg, step, (zero, zero), unroll=2)

    end_r = hr_ref[rows - SUBLANES:rows, :]
    end_i = hi_ref[rows - SUBLANES:rows, :]
    as_r = pr_ref[rows - 1:rows, :]
    as_i = pi_ref[rows - 1:rows, :]
    if chained:
        @pl.when(pl.program_id(1) == 0)
        def _():
            cr_ref[...] = jnp.zeros_like(cr_ref)
            ci_ref[...] = jnp.zeros_like(ci_ref)

        cur_r, cur_i = cr_ref[...], ci_ref[...]
        for j in range(SUBLANES):
            sr_ref[j:j + 1, :] = cur_r
            si_ref[j:j + 1, :] = cur_i
            cur_r, cur_i = _cmul_add(as_r, as_i, cur_r, cur_i, end_r[j:j + 1], end_i[j:j + 1])
        cr_ref[...] = cur_r
        ci_ref[...] = cur_i
    else:
        sr_ref[...] = h0r_ref[...]
        si_ref[...] = h0i_ref[...]
    t_r, t_i = _cmul_add(as_r, as_i, sr_ref[...], si_ref[...], end_r, end_i)
    er_ref[...] = t_r
    ei_ref[...] = t_i

    slabs_per_window = MXU_DIM // (MXU_DIM // S5_STATE * S5_GROUP_CH)
    ys = []
    for w in range(S5_WIDTH // MXU_DIM):
        acc = d_ref[:, w * MXU_DIM:(w + 1) * MXU_DIM] * jnp.concatenate(
            [up_ref[c] for c in range(w * MXU_DIM // LANES, (w + 1) * MXU_DIM // LANES)], axis=1)
        for s in range(slabs_per_window):
            cols = slice((w * slabs_per_window + s) * MXU_DIM, (w * slabs_per_window + s + 1) * MXU_DIM)
            shape3 = (seg, SUBLANES, MXU_DIM)
            h_r, h_i = _cmul_add(pr_ref[:, cols].reshape(shape3), pi_ref[:, cols].reshape(shape3),
                                 sr_ref[:, cols][None], si_ref[:, cols][None],
                                 hr_ref[:, cols].reshape(shape3), hi_ref[:, cols].reshape(shape3))
            acc += _dot(h_r.reshape(rows, MXU_DIM).astype(BF16), wcr_ref[w * slabs_per_window + s])
            acc += _dot(h_i.reshape(rows, MXU_DIM).astype(BF16), wci_ref[w * slabs_per_window + s])
        ys.append(jax.nn.gelu(acc).astype(BF16))

    z = _dot(jnp.concatenate(ys, axis=1), wglu_ref[...])
    o = _rms(z[:, :S5_WIDTH] * jax.nn.sigmoid(z[:, S5_WIDTH:]), og_ref[...])
    for c in range(n_lane_blk):
        op_ref[c] = o[:, c * LANES:(c + 1) * LANES]

    for j in range(SUBLANES):
        for i0 in range(0, seg, 2 * SUBLANES):
            for c in range(n_lane_blk):
                o_ref[j * seg + i0:j * seg + i0 + 2 * SUBLANES, c * LANES:(c + 1) * LANES] = (
                    op_ref[c, pl.ds(i0 * SUBLANES + j, 2 * SUBLANES, stride=SUBLANES), :].astype(BF16))


def _s5_weights(lam_re, lam_im, b_re, b_im, c_re, c_im, d, log_step, seg):
    step = jnp.exp(log_step)[:, None]
    lr = jnp.minimum(lam_re, EIG_CLIP)
    li = lam_im
    mag = jnp.exp(lr * step)
    ar = mag * jnp.cos(li * step)
    ai = mag * jnp.sin(li * step)
    den = lr * lr + li * li
    fr = ((ar - 1.0) * lr + ai * li) / den
    fi = (ai * lr - (ar - 1.0) * li) / den
    bbar_r = fr[..., None] * b_re - fi[..., None] * b_im
    bbar_i = fr[..., None] * b_im + fi[..., None] * b_re
    t = jnp.arange(1, seg + 1, dtype=F32)[:, None, None]
    pm = jnp.exp(lr * step * t)
    p_r = jnp.repeat((pm * jnp.cos(li * step * t)).reshape(seg, S5_COLS), SUBLANES, axis=0)
    p_i = jnp.repeat((pm * jnp.sin(li * step * t)).reshape(seg, S5_COLS), SUBLANES, axis=0)

    gk = MXU_DIM // S5_GROUP_CH

    def in_map(bbar):
        bb = bbar.reshape(S5_GROUPS // gk, gk, S5_STATE, S5_GROUP_CH)
        w = jnp.einsum('kgnh,gG->kghGn', bb, jnp.eye(gk, dtype=F32))
        return w.reshape(S5_GROUPS // gk, MXU_DIM, gk * S5_STATE).astype(BF16)

    gs = MXU_DIM // S5_STATE
    n_slab = S5_COLS // MXU_DIM
    per_win = MXU_DIM // (gs * S5_GROUP_CH)

    def out_map(c):
        cc = c.reshape(n_slab // per_win, per_win, gs, S5_GROUP_CH, S5_STATE)
        w = jnp.einsum('wsghn,sS,gG->wsgnSGh', cc, jnp.eye(per_win, dtype=F32), jnp.eye(gs, dtype=F32))
        return w.reshape(n_slab, MXU_DIM, MXU_DIM).astype(BF16)

    return dict(wbr=in_map(bbar_r), wbi=in_map(bbar_i), wcr=out_map(c_re), wci=out_map(-c_im),
                ar=ar.reshape(1, S5_COLS), ai=ai.reshape(1, S5_COLS), pr=p_r, pi=p_i,
                d=d.reshape(1, S5_WIDTH))


def _s5_call(x, x_first, norm_g, w_in, sw, w_glu, og, seg, chained, h0=None, prev=None):
    rows = seg * SUBLANES
    const2 = lambda *_: (0, 0)
    const3 = lambda *_: (0, 0, 0)
    if chained:
        n_chunk = SEQ // rows
        grid = (BATCH, n_chunk)
        row_map = lambda b, c: (b * n_chunk + c, 0)
        x_map = row_map
        end_map = lambda b, c: (b, 0, 0)
        n_end = BATCH
        sem = ("parallel", "arbitrary")
    else:
        grid = (DEC_BATCH // SUBLANES,)
        row_map = lambda k: (N_PROMPT // rows + k, 0)
        x_map = lambda k: (x_first + k, 0)
        end_map = lambda k: (k, 0, 0)
        n_end = DEC_BATCH // SUBLANES
        sem = ("parallel",)
    half = S5_COLS // 2
    in_specs = [pl.BlockSpec((rows, D_MODEL), x_map),
                pl.BlockSpec((1, D_MODEL), const2),
                pl.BlockSpec((D_MODEL, S5_WIDTH), const2),
                pl.BlockSpec((2, MXU_DIM, half), const3),
                pl.BlockSpec((2, MXU_DIM, half), const3),
                pl.BlockSpec((S5_COLS // MXU_DIM, MXU_DIM, MXU_DIM), const3),
                pl.BlockSpec((S5_COLS // MXU_DIM, MXU_DIM, MXU_DIM), const3),
                pl.BlockSpec((1, S5_COLS), const2),
                pl.BlockSpec((1, S5_COLS), const2),
                pl.BlockSpec((rows, S5_COLS), const2),
                pl.BlockSpec((rows, S5_COLS), const2),
                pl.BlockSpec((1, S5_WIDTH), const2),
                pl.BlockSpec((S5_WIDTH, 2 * S5_WIDTH), const2),
                pl.BlockSpec((1, S5_WIDTH), const2)]
    args = [x, norm_g.reshape(1, D_MODEL), w_in[:, :S5_WIDTH], sw['wbr'], sw['wbi'], sw['wcr'], sw['wci'], sw['ar'], sw['ai'],
            sw['pr'][:rows], sw['pi'][:rows], sw['d'], w_glu, og.reshape(1, S5_WIDTH)]
    scratch = [pltpu.VMEM((S5_WIDTH // LANES, rows, LANES), F32),
               pltpu.VMEM((rows, S5_COLS), F32), pltpu.VMEM((rows, S5_COLS), F32),
               pltpu.VMEM((SUBLANES, S5_COLS), F32), pltpu.VMEM((SUBLANES, S5_COLS), F32),
               pltpu.VMEM((S5_WIDTH // LANES, rows, LANES), F32)]
    aliases = {}
    if chained:
        scratch += [pltpu.VMEM((1, S5_COLS), F32), pltpu.VMEM((1, S5_COLS), F32)]
    else:
        in_specs += [pl.BlockSpec((None, SUBLANES, S5_COLS), end_map),
                     pl.BlockSpec((None, SUBLANES, S5_COLS), end_map)]
        args += [h0[0], h0[1]]
    if prev is not None:
        in_specs.append(pl.BlockSpec(memory_space=pl.ANY))
        args.append(prev)
        aliases = {len(args) - 1: 0}
    kern = functools.partial(_s5_kernel, seg=seg, chained=chained)
    if prev is not None:
        kern = _drop_inputs(kern, len(args) - 1, 1)
    return pl.pallas_call(
        kern,
        grid=grid,
        in_specs=in_specs,
        out_specs=[pl.BlockSpec((rows, S5_WIDTH), row_map),
                   pl.BlockSpec((None, SUBLANES, S5_COLS), end_map),
                   pl.BlockSpec((None, SUBLANES, S5_COLS), end_map)],
        out_shape=[jax.ShapeDtypeStruct((N_TOK, S5_WIDTH), BF16),
                   jax.ShapeDtypeStruct((n_end, SUBLANES, S5_COLS), F32),
                   jax.ShapeDtypeStruct((n_end, SUBLANES, S5_COLS), F32)],
        scratch_shapes=scratch,
        input_output_aliases=aliases,
        compiler_params=_params(*sem),
        name="s5_chained" if chained else "s5_streams",
    )(*args)


def _drop_inputs(kern, start, n):
    def wrapped(*refs):
        return kern(*refs[:start], *refs[start + n:])
    return wrapped


def _gmlp_kernel(*refs, chunk, emit_vn):
    if emit_vn:
        x_ref, ng_ref, win_ref, lng_ref, lnb_ref, ws_ref, bst_ref, og_ref, o_ref, vn_ref, acc_ref = refs
    else:
        x_ref, ng_ref, win_ref, lng_ref, lnb_ref, ws_ref, bst_ref, og_ref, o_ref, acc_ref = refs
    rows = x_ref.shape[0]
    uv = _dot(_rms(x_ref[...], ng_ref[...]).astype(BF16), win_ref[...])
    u = jax.nn.gelu(uv[:, :GM_WIDTH])
    v = jax.nn.gelu(uv[:, GM_WIDTH:])
    tri = (lax.broadcasted_iota(jnp.int32, (chunk, chunk), 0)
           >= lax.broadcasted_iota(jnp.int32, (chunk, chunk), 1))
    for h in range(GM_HEADS):
        cols = slice(h * GM_HEAD_DIM, (h + 1) * GM_HEAD_DIM)
        vh = v[:, cols]
        mu = jnp.mean(vh, axis=-1, keepdims=True)
        cen = vh - mu
        var = jnp.mean(cen * cen, axis=-1, keepdims=True)
        vn = cen * lax.rsqrt(var + EPS) * lng_ref[:, cols] + lnb_ref[:, cols]
        if emit_vn:
            vn_ref[:, cols] = vn
        w = jnp.where(tri, ws_ref[h], 0.0).astype(BF16)
        vb = vn.astype(BF16)
        bias = bst_ref[:, h:h + 1]
        for c in range(rows // chunk):
            rs = slice(c * chunk, (c + 1) * chunk)
            acc_ref[rs, cols] = u[rs, cols] * (_dot(w, vb[rs]) + bias)
    o_ref[...] = _rms(acc_ref[...], og_ref[...]).astype(BF16)


def _gmlp_call(x, x_first, norm_g, w_in, ln_g, ln_b, w_s, b_s, og, rows, chunk, first_block, n_blocks, emit_vn,
               prev=None):
    const2 = lambda i: (0, 0)
    in_specs = [pl.BlockSpec((rows, D_MODEL), lambda i: (x_first + i, 0)),
                pl.BlockSpec((1, D_MODEL), const2),
                pl.BlockSpec((D_MODEL, 2 * GM_WIDTH), const2),
                pl.BlockSpec((1, GM_WIDTH), const2),
                pl.BlockSpec((1, GM_WIDTH), const2),
                pl.BlockSpec((GM_HEADS, chunk, chunk), lambda i: (0, 0, 0)),
                pl.BlockSpec((chunk, GM_HEADS), const2),
                pl.BlockSpec((1, GM_WIDTH), const2)]
    args = [x, norm_g.reshape(1, D_MODEL), w_in[:, S5_WIDTH:], ln_g.reshape(1, GM_WIDTH), ln_b.reshape(1, GM_WIDTH),
            w_s[:, :chunk, :chunk], b_s[:, :chunk].T, og.reshape(1, GM_WIDTH)]
    out_specs = [pl.BlockSpec((rows, GM_WIDTH), lambda i: (first_block + i, 0))]
    out_shape = [jax.ShapeDtypeStruct((N_TOK, GM_WIDTH), BF16)]
    if emit_vn:
        out_specs.append(pl.BlockSpec((rows, GM_WIDTH), lambda i: (i, 0)))
        out_shape.append(jax.ShapeDtypeStruct((n_blocks * rows, GM_WIDTH), F32))
    kern = functools.partial(_gmlp_kernel, chunk=chunk, emit_vn=emit_vn)
    aliases = {}
    if prev is not None:
        in_specs.append(pl.BlockSpec(memory_space=pl.ANY))
        args.append(prev)
        aliases = {len(args) - 1: 0}
        kern = _drop_inputs(kern, len(args) - 1, 1)
    return pl.pallas_call(
        kern,
        grid=(n_blocks,),
        in_specs=in_specs,
        out_specs=out_specs,
        out_shape=out_shape,
        scratch_shapes=[pltpu.VMEM((rows, GM_WIDTH), F32)],
        input_output_aliases=aliases,
        compiler_params=_params("parallel"),
        name="gmlp_vn" if emit_vn else "gmlp",
    )(*args)


def _post_mix_kernel(x_ref, a_ref, b_ref, wa_ref, wb_ref, xg_ref, wq_ref, k_ref, v_ref, wo_ref,
                     fg_ref, wt_ref, rb_ref, x2_ref, h_ref, e_ref, gate_ref, cnt_ref, att_ref, *, n_batch, kv_layer):
    rows = x_ref.shape[0]
    tb = rows // n_batch
    x = x_ref[...] + _dot(a_ref[...], wa_ref[...]) + _dot(b_ref[...], wb_ref[...])
    q = _dot(_rms(x, xg_ref[...]).astype(BF16), wq_ref[...]).astype(BF16)
    scale = XA_HEAD_DIM ** -0.5
    for b in range(n_batch):
        rs = slice(b * tb, (b + 1) * tb)
        for h in range(XA_HEADS):
            cols = slice(h * XA_HEAD_DIM, (h + 1) * XA_HEAD_DIM)
            if kv_layer is None:
                k_h, v_h = k_ref[b, :, cols], v_ref[b, :, cols]
            else:
                k_h, v_h = k_ref[b, :, h, :].astype(BF16), v_ref[b, :, h, :].astype(BF16)
            s = lax.dot_general(q[rs, cols], k_h, (((1,), (1,)), ((), ())),
                                preferred_element_type=F32) * scale
            e = jnp.exp(s - jnp.max(s, axis=-1, keepdims=True))
            p = e / jnp.sum(e, axis=-1, keepdims=True)
            att_ref[rs, cols] = _dot(p.astype(BF16), v_h).astype(BF16)
    x = x + _dot(att_ref[...], wo_ref[...])
    x2_ref[...] = x
    _route_rows(x, fg_ref, wt_ref, rb_ref, h_ref, e_ref, gate_ref, cnt_ref)


def _post_mix(x, x_first, a, b, w_out, xg, wq, k, v, wo, fg, rw, rb,
              rows, n_batch, first_block, n_blocks, blocks_per_kv, prev=None, kv_layer=None):
    const2 = lambda i: (0, 0)
    if kv_layer is None:
        kv_spec = pl.BlockSpec((n_batch, N_MEM, D_MODEL), lambda i: (i // blocks_per_kv, 0, 0))
    else:
        kv_spec = pl.BlockSpec((None, n_batch, N_MEM, XA_HEADS, XA_HEAD_DIM),
                               lambda i: (kv_layer, i // blocks_per_kv, 0, 0, 0))
    row_map = lambda i: (first_block + i, 0)
    d = D_MODEL
    in_specs = [pl.BlockSpec((rows, d), lambda i: (x_first + i, 0)),
                pl.BlockSpec((rows, S5_WIDTH), row_map),
                pl.BlockSpec((rows, GM_WIDTH), row_map),
                pl.BlockSpec((S5_WIDTH, d), const2),
                pl.BlockSpec((GM_WIDTH, d), const2),
                pl.BlockSpec((1, d), const2),
                pl.BlockSpec((d, d), const2),
                kv_spec, kv_spec,
                pl.BlockSpec((d, d), const2),
                pl.BlockSpec((1, d), const2),
                pl.BlockSpec((N_EXPERTS, d), const2),
                pl.BlockSpec((N_EXPERTS, 1), const2)]
    args = [x, a, b, w_out[:S5_WIDTH], w_out[S5_WIDTH:], xg.reshape(1, d), wq, k, v, wo,
            fg.reshape(1, d), rw.T.astype(BF16), rb.reshape(N_EXPERTS, 1)]
    kern = functools.partial(_post_mix_kernel, n_batch=n_batch, kv_layer=kv_layer)
    aliases = {}
    if prev is not None:
        n_in = len(args)
        in_specs += [pl.BlockSpec(memory_space=pl.ANY)] * len(prev)
        args += list(prev)
        aliases = {n_in + o: o for o in range(len(prev))}
        kern = _drop_inputs(kern, n_in, len(prev))
    return pl.pallas_call(
        kern,
        grid=(n_blocks,),
        in_specs=in_specs,
        out_specs=[pl.BlockSpec((rows, d), row_map),
                   pl.BlockSpec((rows // SUBLANES,) + PIECE_TILE, lambda i: (first_block + i, 0, 0, 0)),
                   pl.BlockSpec((TOP_K, rows), lambda i: (0, first_block + i)),
                   pl.BlockSpec((TOP_K, rows), lambda i: (0, first_block + i)),
                   pl.BlockSpec((None, N_EXPERTS, LANES), lambda i: (i, 0, 0))],
        out_shape=[jax.ShapeDtypeStruct((N_TOK, d), F32),
                   jax.ShapeDtypeStruct((N_TOK // SUBLANES,) + PIECE_TILE, F32),
                   jax.ShapeDtypeStruct((TOP_K, N_TOK), jnp.int32),
                   jax.ShapeDtypeStruct((TOP_K, N_TOK), F32),
                   jax.ShapeDtypeStruct((n_blocks, N_EXPERTS, LANES), F32)],
        scratch_shapes=[pltpu.VMEM((rows, d), BF16)],
        input_output_aliases=aliases,
        compiler_params=_params("parallel"),
        name="post_mix",
    )(*args)


def _moe_kernel(be_ref, nu_ref, first_ref, slot_ref, nxt_ref, parts_ref,
                x_ref, wg_hbm, wu_hbm, wd_hbm, bg_ref, bu_ref, bd_ref, o_ref,
                wbuf_ref, wgb_ref, wub_ref, wdb_ref, sem, *, layer):
    i = pl.program_id(0)
    active = i < nu_ref[0]

    def copies(e, s):
        return [pltpu.make_async_copy(w.at[layer, e], wbuf_ref.at[s, j], sem.at[s, j])
                for j, w in enumerate((wg_hbm, wu_hbm, wd_hbm))]

    @pl.when(jnp.logical_and(active, first_ref[i] == 1))
    def _():
        s = slot_ref[i]

        @pl.when(i == 0)
        def _():
            for c in copies(be_ref[0], s):
                c.start()

        @pl.when(nxt_ref[i] >= 0)
        def _():
            for c in copies(nxt_ref[i], 1 - s):
                c.start()

        for c in copies(be_ref[i], s):
            c.wait()

        def cast(r, _):
            rs = pl.ds(pl.multiple_of(r * MOE_CAST_ROWS, MOE_CAST_ROWS), MOE_CAST_ROWS)
            wgb_ref[rs, :] = wbuf_ref[s, 0, rs, :].astype(BF16)
            wub_ref[rs, :] = wbuf_ref[s, 1, rs, :].astype(BF16)
            wdb_ref[rs, :] = wbuf_ref[s, 2, rs, :].astype(BF16)
            return 0
        lax.fori_loop(0, D_MODEL // MOE_CAST_ROWS, cast, 0)

    def experts(n_rows):
        words = lax.bitcast_convert_type(
            jnp.concatenate([x_ref[:n_rows // SUBLANES, c].reshape(n_rows, LANES) for c in range(PIECES_PER_ROW)],
                            axis=1), jnp.uint32)
        x = jnp.concatenate([lax.bitcast_convert_type(words << 16, F32),
                             lax.bitcast_convert_type(words & jnp.uint32(HIGH_HALF), F32)], axis=1).astype(BF16)
        g = jnp.minimum(_dot(x, wgb_ref[...]) + bg_ref[...], SWIGLU_LIMIT)
        u = jnp.clip(_dot(x, wub_ref[...]) + bu_ref[...], -SWIGLU_LIMIT, SWIGLU_LIMIT)
        a = g * jax.nn.sigmoid(SWIGLU_ALPHA * g) * (u + 1.0)
        y = _dot(a.astype(BF16), wdb_ref[...]) + bd_ref[...]
        for c in range(OUT_PIECES):
            o_ref[:n_rows // SUBLANES, c] = y[:, c * LANES:(c + 1) * LANES].reshape(n_rows // SUBLANES, SUBLANES, LANES)

    for p in range(1, MOE_ROWS // MOE_PART + 1):
        pl.when(jnp.logical_and(active, parts_ref[i] == p))(functools.partial(experts, p * MOE_PART))


def _moe_call(layer, xs, sched, w_gate, b_gate, w_up, b_up, w_down, b_down):
    def piece_map(i, be, nu, *_):
        return (jnp.minimum(i, nu[0] - 1), 0, 0, 0)

    def b_map(i, be, *_):
        return (layer, be[i], 0, 0)

    b_spec = pl.BlockSpec((None, None, 1, D_FF), b_map)
    w_spec = pl.BlockSpec(memory_space=pl.ANY)
    grid_spec = pltpu.PrefetchScalarGridSpec(
        num_scalar_prefetch=6,
        grid=(MOE_BLOCKS,),
        in_specs=[pl.BlockSpec((MOE_ROWS // SUBLANES,) + PIECE_TILE, piece_map),
                  w_spec, w_spec, w_spec, b_spec, b_spec, b_spec],
        out_specs=pl.BlockSpec((MOE_ROWS // SUBLANES,) + OUT_TILE, piece_map),
        scratch_shapes=[pltpu.VMEM((2, 3, D_MODEL, D_FF), F32)]
        + [pltpu.VMEM((D_MODEL, D_FF), BF16)] * 3
        + [pltpu.SemaphoreType.DMA((2, 3))],
    )
    shape4 = (DEPTH, N_EXPERTS, 1, D_FF)
    return pl.pallas_call(
        functools.partial(_moe_kernel, layer=layer),
        grid_spec=grid_spec,
        out_shape=jax.ShapeDtypeStruct((MOE_ROWS_TOTAL // SUBLANES,) + OUT_TILE, F32),
        compiler_params=pltpu.CompilerParams(dimension_semantics=("arbitrary",),
                                             vmem_limit_bytes=MOE_VMEM_LIMIT),
        name="moe_experts",
    )(*sched, xs, w_gate, w_up, w_down, b_gate.reshape(shape4), b_up.reshape(shape4),
      b_down.reshape(shape4))


DISPATCH_TOKENS = 48
DMA_PIECES = 64
DISPATCH_DMAS = DISPATCH_TOKENS * PIECES_PER_ROW // DMA_PIECES
DISPATCH_CHUNKS = N_TOK // SC_WORKERS // DISPATCH_TOKENS
IDX_ROWS = 16
SC_LANES = 16
MOE_ROWS_TOTAL = MOE_BLOCKS * MOE_ROWS
TILE_PIECES = PIECES_PER_ROW * SUBLANES


def _piece_index_rows(p_v, idx_v, row0, tok0, n_dma, pieces_per_row, dma_pieces):
    lane = lax.iota(jnp.int32, SC_LANES)
    in_tile = lane & (SUBLANES - 1)
    piece = (lane >> 3) * SUBLANES
    per_vec = SC_LANES // SUBLANES
    tile_pieces = pieces_per_row * SUBLANES
    tiles_per_dma = dma_pieces // tile_pieces
    for m in range(n_dma):
        for tr in range(tiles_per_dma):
            first = plsc.load_gather(p_v, [in_tile + (tok0 + (m * tiles_per_dma + tr) * SUBLANES)])
            for v in range(pieces_per_row // per_vec):
                idx_v[row0 + m, pl.ds(tr * tile_pieces + v * SC_LANES, SC_LANES)] = (
                    first + (piece + v * per_vec * SUBLANES))


COLLECT_ROWS = 48
COLLECT_PIECES = 128
OUT_PIECES = D_MODEL // LANES
OUT_TILE = (OUT_PIECES, SUBLANES, LANES)


def _collect(ys4, dest_t):
    first_piece = ((dest_t // SUBLANES) * (OUT_PIECES * SUBLANES) + dest_t % SUBLANES).reshape(-1)
    chunk_pieces = COLLECT_ROWS * OUT_PIECES
    n_dma = chunk_pieces // COLLECT_PIECES
    chunks_per_choice = N_TOK // SC_WORKERS // COLLECT_ROWS
    n_chunks = TOP_K * chunks_per_choice
    mesh = plsc.VectorSubcoreMesh(core_axis_name="c", subcore_axis_name="s")

    @functools.partial(
        pl.kernel, mesh=mesh,
        out_type=jax.ShapeDtypeStruct((TOP_K * N_TOK * OUT_PIECES, LANES), F32),
        scratch_types=[pltpu.VMEM((chunk_pieces, LANES), F32), pltpu.VMEM((chunk_pieces, LANES), F32),
                       pltpu.VMEM((SUBLANES, COLLECT_PIECES), jnp.int32),
                       pltpu.VMEM((SUBLANES, COLLECT_PIECES), jnp.int32),
                       pltpu.VMEM((COLLECT_ROWS,), jnp.int32),
                       pltpu.SemaphoreType.DMA, pltpu.SemaphoreType.DMA],
        compiler_params=pltpu.CompilerParams(needs_layout_passes=False),
    )
    def gather_rows(y_hbm, p_hbm, o_hbm, rows0, rows1, idx0, idx1, p_v, sem0, sem1):
        wid = lax.axis_index("s") * SC_CORES + lax.axis_index("c")
        bufs = ((rows0, idx0, sem0), (rows1, idx1, sem1))

        def first_row(j):
            k = j // chunks_per_choice
            return k * N_TOK + (wid * chunks_per_choice + (j - k * chunks_per_choice)) * COLLECT_ROWS

        def copies(b):
            rows_v, idx_v, sem = bufs[b]
            return [pltpu.make_async_copy(y_hbm.at[idx_v.at[m]],
                                          rows_v.at[pl.ds(m * COLLECT_PIECES, COLLECT_PIECES)], sem)
                    for m in range(n_dma)]

        def fire(j, b):
            pltpu.sync_copy(p_hbm.at[pl.ds(pl.multiple_of(first_row(j), SUBLANES), COLLECT_ROWS)], p_v)
            _piece_index_rows(p_v, bufs[b][1], 0, 0, n_dma, OUT_PIECES, COLLECT_PIECES)
            for cp in copies(b):
                cp.start()

        def drain(j, b):
            for cp in copies(b):
                cp.wait()
            pltpu.sync_copy(bufs[b][0],
                            o_hbm.at[pl.ds(pl.multiple_of(first_row(j) * OUT_PIECES, SUBLANES), chunk_pieces)])

        fire(0, 0)

        @pl.loop(0, n_chunks, step=2)
        def _(j):
            fire(j + 1, 1)
            drain(j, 0)
            pl.when(j + 2 < n_chunks)(lambda: fire(j + 2, 0))
            drain(j + 1, 1)

    picked = gather_rows(ys4.reshape(MOE_ROWS_TOTAL * OUT_PIECES, LANES), first_piece)
    return picked.reshape((TOP_K, N_TOK // SUBLANES) + OUT_TILE)


def _dispatch(h4, dest_t):
    first_piece = ((dest_t // SUBLANES) * TILE_PIECES + dest_t % SUBLANES).reshape(-1)
    chunk_pieces = DISPATCH_TOKENS * PIECES_PER_ROW
    mesh = plsc.VectorSubcoreMesh(core_axis_name="c", subcore_axis_name="s")

    @functools.partial(
        pl.kernel, mesh=mesh,
        out_type=jax.ShapeDtypeStruct((MOE_ROWS_TOTAL * PIECES_PER_ROW, LANES), F32),
        scratch_types=[pltpu.VMEM((chunk_pieces, LANES), F32), pltpu.VMEM((IDX_ROWS, DMA_PIECES), jnp.int32),
                       pltpu.VMEM((TOP_K * DISPATCH_TOKENS,), jnp.int32), pltpu.SemaphoreType.DMA],
        compiler_params=pltpu.CompilerParams(needs_layout_passes=False),
    )
    def scatter_rows(h_hbm, p_hbm, o_hbm, rows_v, idx_v, p_v, sem):
        wid = lax.axis_index("s") * SC_CORES + lax.axis_index("c")

        @pl.loop(0, DISPATCH_CHUNKS)
        def _(j):
            q = wid * DISPATCH_CHUNKS + j
            pltpu.sync_copy(h_hbm.at[pl.ds(pl.multiple_of(q * chunk_pieces, SUBLANES), chunk_pieces)], rows_v)
            for k in range(TOP_K):
                pltpu.sync_copy(
                    p_hbm.at[pl.ds(pl.multiple_of(k * N_TOK + q * DISPATCH_TOKENS, SUBLANES), DISPATCH_TOKENS)],
                    p_v.at[pl.ds(k * DISPATCH_TOKENS, DISPATCH_TOKENS)])
            for k in range(TOP_K):
                _piece_index_rows(p_v, idx_v, k * DISPATCH_DMAS, k * DISPATCH_TOKENS, DISPATCH_DMAS,
                                  PIECES_PER_ROW, DMA_PIECES)
            copies = [pltpu.async_copy(rows_v.at[pl.ds(m * DMA_PIECES, DMA_PIECES)],
                                       o_hbm.at[idx_v.at[k * DISPATCH_DMAS + m]], sem)
                      for k in range(TOP_K) for m in range(DISPATCH_DMAS)]
            for cp in copies:
                cp.wait()

    xs = scatter_rows(h4.reshape(N_TOK * PIECES_PER_ROW, LANES), first_piece)
    return xs.reshape((MOE_ROWS_TOTAL // SUBLANES,) + PIECE_TILE)


def _route(e_t, cnt_prompt, cnt_sample):
    cnt_sample = jnp.sum(cnt_sample[:, :, 0].reshape(SAMPLE_BLOCKS, -1, N_EXPERTS), axis=1)
    blk_cnt = jnp.concatenate([cnt_prompt[:, :, 0], cnt_sample]).astype(jnp.int32)
    def before(n):
        i = jnp.arange(n, dtype=jnp.int32)
        return i[None, :] < i[:, None]

    counts = jnp.sum(blk_cnt, axis=0)
    padded = ((counts + MOE_ROWS - 1) // MOE_ROWS) * MOE_ROWS
    pstart = jnp.sum(jnp.where(before(N_EXPERTS), padded[None, :], 0), axis=1)
    pend = pstart + padded
    base = pstart[None, :] + jnp.sum(jnp.where(before(blk_cnt.shape[0])[:, :, None], blk_cnt[None], 0), axis=1)
    dest_t = _dest(e_t, base.astype(F32)[:, :, None])
    n_used = jnp.sum(padded) // MOE_ROWS
    ids = jnp.arange(MOE_BLOCKS, dtype=jnp.int32)
    blk = jnp.minimum(ids, n_used - 1) * MOE_ROWS
    blk_exp = jnp.minimum(jnp.sum((pend[None, :] <= blk[:, None]).astype(jnp.int32), axis=1), N_EXPERTS - 1)
    eid = jnp.arange(N_EXPERTS, dtype=jnp.int32)
    of_blk = blk_exp[:, None] == eid[None, :]

    def lookup(table):
        return jnp.sum(jnp.where(of_blk, table[None, :], 0), axis=1)

    first = jnp.logical_and(ids < n_used, jnp.logical_or(ids == 0, blk_exp != jnp.roll(blk_exp, 1)))
    first = first.astype(jnp.int32)
    slot = jnp.sum(jnp.where(before(MOE_BLOCKS), first[None, :], 0), axis=1) % 2
    later = jnp.logical_and(eid[None, :] > eid[:, None], (counts > 0)[None, :])
    nxt_e = jnp.min(jnp.where(later, eid[None, :], N_EXPERTS), axis=1)
    nxt = lookup(jnp.where(nxt_e < N_EXPERTS, nxt_e, -1))
    real = jnp.clip(lookup(pstart + counts) - blk, 0, MOE_ROWS)
    parts = (real + MOE_PART - 1) // MOE_PART
    return dest_t, (blk_exp, n_used.reshape(1).astype(jnp.int32), first, slot.astype(jnp.int32),
                    nxt.astype(jnp.int32), parts.astype(jnp.int32))


def kernel(x_prompt, x_sample, mem_prompt, state_s5_re, state_s5_im, cache_mem_k, cache_mem_v, norm_mix_g, w_in, s5_lambda_re, s5_lambda_im, s5_b_re, s5_b_im, s5_c_re, s5_c_im, s5_d, s5_log_step, s5_w_glu, gm_ln_g, gm_ln_b, gm_w_s, gm_b_s, out_norm_g, w_out, xa_norm_g, mem_norm_g, w_xq, w_xk, w_xv, w_xo, ffn_norm_g, router_w, router_b, e_w_gate, e_b_gate, e_w_up, e_b_up, e_w_down, e_b_down, final_norm_g):
    xs = (x_prompt.reshape(N_PROMPT, D_MODEL), x_sample.reshape(N_SAMPLE, D_MODEL))
    mem = mem_prompt.reshape(BATCH * N_MEM, D_MODEL)
    outs = {k: [] for k in ('s5r_p', 's5i_p', 's5r_s', 's5i_s', 'gmv_s')}
    mem_kv = None
    for l in range(DEPTH):
        unified = len(xs) == 1

        def sample_first(rows):
            return N_PROMPT // rows if unified else 0

        in_proj = (norm_mix_g[l], w_in[l].astype(BF16))
        sw = _s5_weights(s5_lambda_re[l], s5_lambda_im[l], s5_b_re[l], s5_b_im[l], s5_c_re[l], s5_c_im[l],
                         s5_d[l], s5_log_step[l], S5_SEG_PROMPT)
        w_glu = s5_w_glu[l].astype(BF16)
        og = out_norm_g[l]
        ms5, er_p, ei_p = _s5_call(xs[0], 0, *in_proj, sw, w_glu, og[:S5_WIDTH], S5_SEG_PROMPT, True)
        h0 = (state_s5_re[l].reshape(DEC_BATCH // SUBLANES, SUBLANES, S5_COLS),
              state_s5_im[l].reshape(DEC_BATCH // SUBLANES, SUBLANES, S5_COLS))
        ms5, er_s, ei_s = _s5_call(xs[-1], sample_first(DEC_SEQ * SUBLANES), *in_proj, sw, w_glu, og[:S5_WIDTH],
                                   DEC_SEQ, False, h0=h0, prev=ms5)
        outs['s5r_p'].append(er_p[:, SUBLANES - 1].reshape(BATCH, S5_GROUPS, S5_STATE))
        outs['s5i_p'].append(ei_p[:, SUBLANES - 1].reshape(BATCH, S5_GROUPS, S5_STATE))
        outs['s5r_s'].append(er_s.reshape(DEC_BATCH, S5_GROUPS, S5_STATE))
        outs['s5i_s'].append(ei_s.reshape(DEC_BATCH, S5_GROUPS, S5_STATE))

        (mgm,) = _gmlp_call(xs[0], 0, *in_proj, gm_ln_g[l], gm_ln_b[l], gm_w_s[l], gm_b_s[l], og[S5_WIDTH:],
                            GM_ROWS_PROMPT, GM_CHUNK, 0, N_PROMPT // GM_ROWS_PROMPT, False)
        mgm, vn = _gmlp_call(xs[-1], sample_first(N_SAMPLE), *in_proj, gm_ln_g[l], gm_ln_b[l], gm_w_s[l],
                             gm_b_s[l], og[S5_WIDTH:], N_SAMPLE, DEC_SEQ, N_PROMPT // N_SAMPLE, 1, True, prev=mgm)
        outs['gmv_s'].append(vn.reshape(DEC_BATCH, DEC_SEQ, GM_HEADS, GM_HEAD_DIM))
        *mem_kv, mk, mv = _mem_kv(mem, mem_norm_g[l], w_xk[l].astype(BF16), w_xv[l].astype(BF16), l, prev=mem_kv)
        shared = (w_out[l].astype(BF16), xa_norm_g[l], w_xq[l].astype(BF16))
        tail = (w_xo[l].astype(BF16), ffn_norm_g[l], router_w[l], router_b[l])
        rows_s = XA_SAMPLE_BATCHES * DEC_SEQ
        post_p = _post_mix(xs[0], 0, ms5, mgm, *shared, mk, mv, *tail,
                           XA_ROWS_PROMPT, 1, 0, N_PROMPT // XA_ROWS_PROMPT, SEQ // XA_ROWS_PROMPT)
        post_s = _post_mix(xs[-1], sample_first(rows_s), ms5, mgm, *shared,
                           cache_mem_k, cache_mem_v, *tail,
                           rows_s, XA_SAMPLE_BATCHES, N_PROMPT // rows_s, N_SAMPLE // rows_s, 1,
                           prev=post_p[:4], kv_layer=l)
        x, h, e_t, gates_t = post_s[:4]

        dest_t, sched = _route(e_t, post_p[4], post_s[4])
        ys = _moe_call(l, _dispatch(h, dest_t), sched, e_w_gate, e_b_gate, e_w_up, e_b_up, e_w_down, e_b_down)
        x = _combine(x, _collect(ys, dest_t), gates_t.T, final_norm_g if l == DEPTH - 1 else None)
        xs = (x,)

    y_prompt = x[0].reshape(BATCH, SEQ, D_MODEL)
    y_sample = x[1].reshape(DEC_BATCH, DEC_SEQ, D_MODEL)
    st = jnp.stack
    return (y_prompt, y_sample, st(outs['s5r_p']), st(outs['s5i_p']), mem_kv[0], mem_kv[1],
            st(outs['s5r_s']), st(outs['s5i_s']), st(outs['gmv_s']))
```

```python
import functools

import jax
import jax.numpy as jnp
from jax import lax
from jax.experimental import pallas as pl
from jax.experimental.pallas import tpu as pltpu
from jax.experimental.pallas import tpu_sc as plsc

F32 = jnp.float32
BF16 = jnp.bfloat16

D_MODEL = 1024
BATCH = 4
SEQ = 4096
DEPTH = 2
DEC_BATCH = 16
DEC_SEQ = 32
S5_GROUPS = 32
S5_GROUP_CH = 16
S5_STATE = 64
S5_WIDTH = S5_GROUPS * S5_GROUP_CH
S5_COLS = S5_GROUPS * S5_STATE
GM_HEADS = 4
GM_HEAD_DIM = 128
GM_WIDTH = GM_HEADS * GM_HEAD_DIM
GM_CHUNK = 128
N_MEM = 256
XA_HEADS = 4
XA_HEAD_DIM = D_MODEL // XA_HEADS
N_EXPERTS = 32
TOP_K = 4
D_FF = D_MODEL
SWIGLU_LIMIT = 7.0
SWIGLU_ALPHA = 1.702
EPS = 1e-6
EIG_CLIP = -1e-4

N_PROMPT = BATCH * SEQ
N_SAMPLE = DEC_BATCH * DEC_SEQ
N_TOK = N_PROMPT + N_SAMPLE

SUBLANES = 8
LANES = 128
MXU_DIM = 256
VMEM_LIMIT = 48 * 1024 * 1024
MOE_VMEM_LIMIT = 60 * 1024 * 1024
SC_CORES = 2
SC_SUBCORES = 16
SC_WORKERS = SC_CORES * SC_SUBCORES

PACK_WIDTH = D_MODEL // 2
PIECES_PER_ROW = PACK_WIDTH // LANES
PIECE_TILE = (PIECES_PER_ROW, SUBLANES, LANES)
HIGH_HALF = 0xFFFF0000

ROW_BLOCK = 512
S5_SEG_PROMPT = 64
S5_COL_BLOCK = 1024
GM_ROWS_PROMPT = 512
XA_ROWS_PROMPT = 512
XA_SAMPLE_BATCHES = 4
MOE_ROWS = 1024
MOE_PART = 128
MOE_BLOCKS = -(-N_TOK * TOP_K // MOE_ROWS) + N_EXPERTS
MOE_CAST_ROWS = 128


def _params(*sem):
    return pltpu.CompilerParams(dimension_semantics=sem, vmem_limit_bytes=VMEM_LIMIT)


def _rms(x, g):
    return x * lax.rsqrt(jnp.mean(x * x, axis=-1, keepdims=True) + EPS) * g


def _dot(a, b):
    return jnp.dot(a, b, preferred_element_type=F32)


PROMPT_BLOCKS = N_PROMPT // ROW_BLOCK
SAMPLE_BLOCKS = N_SAMPLE // ROW_BLOCK


def _group_specs():
    return [pl.BlockSpec((ROW_BLOCK, D_MODEL), lambda i: (jnp.minimum(i, PROMPT_BLOCKS - 1), 0)),
            pl.BlockSpec((ROW_BLOCK, D_MODEL), lambda i: (jnp.maximum(i - PROMPT_BLOCKS, 0), 0))]


def _for_group(refs, body):
    i = pl.program_id(0)
    pl.when(i < PROMPT_BLOCKS)(lambda: body(refs[0]))
    pl.when(i >= PROMPT_BLOCKS)(lambda: body(refs[1]))


def _mem_kv_kernel(m_ref, g_ref, wk_ref, wv_ref, k5_ref, v5_ref, kb_ref, vb_ref):
    m = _rms(m_ref[...], g_ref[...]).astype(BF16)
    for w_ref, o5_ref, ob_ref in ((wk_ref, k5_ref, kb_ref), (wv_ref, v5_ref, vb_ref)):
        kv = _dot(m, w_ref[...])
        ob_ref[...] = kv.astype(BF16)
        for h in range(XA_HEADS):
            o5_ref[:, h, :] = kv[:, h * XA_HEAD_DIM:(h + 1) * XA_HEAD_DIM]


def _mem_kv(mem, g, wk, wv, layer, prev=None):
    d = D_MODEL
    const2 = lambda b: (0, 0)
    in_specs = [pl.BlockSpec((N_MEM, d), lambda b: (b, 0)),
                pl.BlockSpec((1, d), const2),
                pl.BlockSpec((d, d), const2),
                pl.BlockSpec((d, d), const2)]
    args = [mem, g.reshape(1, d), wk, wv]
    kern = _mem_kv_kernel
    aliases = {}
    if prev is not None:
        in_specs += [pl.BlockSpec(memory_space=pl.ANY)] * 2
        args += list(prev)
        aliases = {4: 0, 5: 1}
        kern = _drop_inputs(kern, 4, 2)
    out5 = pl.BlockSpec((None, None, N_MEM, XA_HEADS, XA_HEAD_DIM), lambda b: (layer, b, 0, 0, 0))
    outb = pl.BlockSpec((None, N_MEM, d), lambda b: (b, 0, 0))
    shape5 = jax.ShapeDtypeStruct((DEPTH, BATCH, N_MEM, XA_HEADS, XA_HEAD_DIM), F32)
    shapeb = jax.ShapeDtypeStruct((BATCH, N_MEM, d), BF16)
    return pl.pallas_call(
        kern,
        grid=(BATCH,),
        in_specs=in_specs,
        out_specs=[out5, out5, outb, outb],
        out_shape=[shape5, shape5, shapeb, shapeb],
        input_output_aliases=aliases,
        compiler_params=_params("parallel"),
        name="mem_kv",
    )(*args)


def _route_rows(x, g_ref, wt_ref, b_ref, h_ref, e_ref, gate_ref, cnt_ref):
    h = _rms(x, g_ref[...]).astype(BF16)
    bits = lax.bitcast_convert_type(h.astype(F32), jnp.uint32)
    words = lax.bitcast_convert_type(bits[:, PACK_WIDTH:] | (bits[:, :PACK_WIDTH] >> 16), F32)
    for c in range(PIECES_PER_ROW):
        h_ref[:, c] = words[:, c * LANES:(c + 1) * LANES].reshape(h.shape[0] // SUBLANES, SUBLANES, LANES)
    logits = lax.dot_general(wt_ref[...], h, (((1,), (1,)), ((), ())),
                             preferred_element_type=F32) + b_ref[...]
    sub = lax.broadcasted_iota(jnp.int32, logits.shape, 0)
    work = logits
    chosen = jnp.zeros(logits.shape, F32)
    vals, idxs = [], []
    for _ in range(TOP_K):
        m = jnp.max(work, axis=0, keepdims=True)
        idx = jnp.min(jnp.where(work == m, sub, N_EXPERTS), axis=0, keepdims=True)
        sel = sub == idx
        vals.append(m)
        idxs.append(idx)
        work = jnp.where(sel, -jnp.inf, work)
        chosen = jnp.where(sel, 1.0, chosen)
    ex = [jnp.exp(v - vals[0]) for v in vals]
    den = ex[0] + ex[1] + ex[2] + ex[3]
    gate_ref[...] = jnp.concatenate([e / den for e in ex], axis=0)
    e_ref[...] = jnp.concatenate(idxs, axis=0)
    cnt_ref[...] = jnp.broadcast_to(jnp.sum(chosen, axis=1, keepdims=True), (N_EXPERTS, LANES))


def _dest_kernel(e_ref, base_ref, d_ref):
    e = e_ref[...]
    rows = e.shape[1]
    sub = lax.broadcasted_iota(jnp.int32, (N_EXPERTS, rows), 0)
    hot = [sub == e[k:k + 1, :] for k in range(TOP_K)]
    chosen = sum(jnp.where(hk, 1.0, 0.0) for hk in hot)
    earlier = (lax.broadcasted_iota(jnp.int32, (rows, rows), 0)
               < lax.broadcasted_iota(jnp.int32, (rows, rows), 1))
    pos = base_ref[...] + _dot(chosen.astype(BF16), jnp.where(earlier, 1.0, 0.0).astype(BF16))
    d_ref[...] = jnp.concatenate(
        [jnp.sum(jnp.where(hk, pos, 0.0), axis=0, keepdims=True) for hk in hot], axis=0).astype(jnp.int32)


def _dest(e_t, base):
    n = e_t.shape[1]
    rows = ROW_BLOCK
    return pl.pallas_call(
        _dest_kernel,
        grid=(n // rows,),
        in_specs=[pl.BlockSpec((TOP_K, rows), lambda i: (0, i)),
                  pl.BlockSpec((None, N_EXPERTS, 1), lambda i: (i, 0, 0))],
        out_specs=pl.BlockSpec((TOP_K, rows), lambda i: (0, i)),
        out_shape=jax.ShapeDtypeStruct((TOP_K, n), jnp.int32),
        compiler_params=_params("parallel"),
        name="moe_dest",
    )(e_t, base)


def _combine_kernel(*refs, final):
    if final:
        x_ref, g0, g1, g2, g3, gate_ref, fg_ref, op_ref, os_ref = refs
    else:
        x_ref, g0, g1, g2, g3, gate_ref, o_ref = refs
    rows = x_ref.shape[0]
    slabs = []
    for c in range(OUT_PIECES):
        acc = x_ref[:, c * LANES:(c + 1) * LANES]
        for k, g in enumerate((g0, g1, g2, g3)):
            acc = acc + gate_ref[:, k:k + 1] * g[:, c].reshape(rows, LANES)
        slabs.append(acc)
    acc = jnp.concatenate(slabs, axis=1)
    if final:
        def store(o_ref):
            o_ref[...] = _rms(acc, fg_ref[...])
        _for_group((op_ref, os_ref), store)
    else:
        o_ref[...] = acc


def _combine(x, picked, gates, final_g=None):
    n, d = x.shape
    rows = ROW_BLOCK
    final = final_g is not None
    in_specs = [pl.BlockSpec((rows, d), lambda i: (i, 0))]
    in_specs += [pl.BlockSpec((None, rows // SUBLANES) + OUT_TILE, functools.partial(lambda i, k: (k, i, 0, 0, 0), k=k))
                 for k in range(TOP_K)]
    in_specs.append(pl.BlockSpec((rows, TOP_K), lambda i: (i, 0)))
    args = [x, picked, picked, picked, picked, gates]
    out_specs = pl.BlockSpec((rows, d), lambda i: (i, 0))
    out_shape = jax.ShapeDtypeStruct((n, d), F32)
    if final:
        in_specs.append(pl.BlockSpec((1, d), lambda i: (0, 0)))
        args.append(final_g.reshape(1, d))
        out_specs = _group_specs()
        out_shape = [jax.ShapeDtypeStruct((N_PROMPT, d), F32), jax.ShapeDtypeStruct((N_SAMPLE, d), F32)]
    return pl.pallas_call(
        functools.partial(_combine_kernel, final=final),
        grid=(n // rows,),
        in_specs=in_specs,
        out_specs=out_specs,
        out_shape=out_shape,
        compiler_params=_params("arbitrary" if final else "parallel"),
        name="moe_combine",
    )(*args)


def _cmul_add(ar, ai, hr, hi, br, bi):
    return ar * hr - ai * hi + br, ar * hi + ai * hr + bi


def _s5_kernel(*refs, seg, chained):
    if chained:
        (x_ref, ng_ref, win_ref, wbr_ref, wbi_ref, wcr_ref, wci_ref, ar_ref, ai_ref, pr_ref, pi_ref, d_ref,
         wglu_ref, og_ref, o_ref, er_ref, ei_ref,
         up_ref, hr_ref, hi_ref, sr_ref, si_ref, op_ref, cr_ref, ci_ref) = refs
    else:
        (x_ref, ng_ref, win_ref, wbr_ref, wbi_ref, wcr_ref, wci_ref, ar_ref, ai_ref, pr_ref, pi_ref, d_ref,
         wglu_ref, og_ref, h0r_ref, h0i_ref, o_ref, er_ref, ei_ref,
         up_ref, hr_ref, hi_ref, sr_ref, si_ref, op_ref) = refs
    rows = seg * SUBLANES
    half = S5_COLS // 2

    u = _dot(_rms(x_ref[...], ng_ref[...]).astype(BF16), win_ref[...])
    n_lane_blk = S5_WIDTH // LANES
    for j in range(SUBLANES):
        for i0 in range(0, seg, SUBLANES):
            for c in range(n_lane_blk):
                up_ref[c, pl.ds(i0 * SUBLANES + j, SUBLANES, stride=SUBLANES), :] = (
                    u[j * seg + i0:j * seg + i0 + SUBLANES, c * LANES:(c + 1) * LANES])

    up = jnp.concatenate([up_ref[c] for c in range(n_lane_blk)], axis=1)
    ub = up.astype(BF16)
    for k in range(2):
        uk = ub[:, k * MXU_DIM:(k + 1) * MXU_DIM]
        hr_ref[:, k * half:(k + 1) * half] = _dot(uk, wbr_ref[k])
        hi_ref[:, k * half:(k + 1) * half] = _dot(uk, wbi_ref[k])

    for cb in range(S5_COLS // S5_COL_BLOCK):
        cols = slice(cb * S5_COL_BLOCK, (cb + 1) * S5_COL_BLOCK)
        ar = jnp.broadcast_to(ar_ref[:, cols], (SUBLANES, S5_COL_BLOCK))
        ai = jnp.broadcast_to(ai_ref[:, cols], (SUBLANES, S5_COL_BLOCK))

        def step(i, carry, cols=cols, ar=ar, ai=ai):
            r = pl.multiple_of(i * SUBLANES, SUBLANES)
            h_r, h_i = _cmul_add(ar, ai, carry[0], carry[1],
                                 hr_ref[pl.ds(r, SUBLANES), cols], hi_ref[pl.ds(r, SUBLANES), cols])
            hr_ref[pl.ds(r, SUBLANES), cols] = h_r
            hi_ref[pl.ds(r, SUBLANES), cols] = h_i
            return h_r, h_i

        zero = jnp.zeros((SUBLANES, S5_COL_BLOCK), F32)
        lax.fori_loop(0, seg, step, (zero, zero), unroll=2)

    end_r = hr_ref[rows - SUBLANES:rows, :]
    end_i = hi_ref[rows - SUBLANES:rows, :]
    as_r = pr_ref[rows - 1:rows, :]
    as_i = pi_ref[rows - 1:rows, :]
    if chained:
        @pl.when(pl.program_id(1) == 0)
        def _():
            cr_ref[...] = jnp.zeros_like(cr_ref)
            ci_ref[...] = jnp.zeros_like(ci_ref)

        cur_r, cur_i = cr_ref[...], ci_ref[...]
        for j in range(SUBLANES):
            sr_ref[j:j + 1, :] = cur_r
            si_ref[j:j + 1, :] = cur_i
            cur_r, cur_i = _cmul_add(as_r, as_i, cur_r, cur_i, end_r[j:j + 1], end_i[j:j + 1])
        cr_ref[...] = cur_r
        ci_ref[...] = cur_i
    else:
        sr_ref[...] = h0r_ref[...]
        si_ref[...] = h0i_ref[...]
    t_r, t_i = _cmul_add(as_r, as_i, sr_ref[...], si_ref[...], end_r, end_i)
    er_ref[...] = t_r
    ei_ref[...] = t_i

    slabs_per_window = MXU_DIM // (MXU_DIM // S5_STATE * S5_GROUP_CH)
    ys = []
    for w in range(S5_WIDTH // MXU_DIM):
        acc = d_ref[:, w * MXU_DIM:(w + 1) * MXU_DIM] * jnp.concatenate(
            [up_ref[c] for c in range(w * MXU_DIM // LANES, (w + 1) * MXU_DIM // LANES)], axis=1)
        for s in range(slabs_per_window):
            cols = slice((w * slabs_per_window + s) * MXU_DIM, (w * slabs_per_window + s + 1) * MXU_DIM)
            shape3 = (seg, SUBLANES, MXU_DIM)
            h_r, h_i = _cmul_add(pr_ref[:, cols].reshape(shape3), pi_ref[:, cols].reshape(shape3),
                                 sr_ref[:, cols][None], si_ref[:, cols][None],
                                 hr_ref[:, cols].reshape(shape3), hi_ref[:, cols].reshape(shape3))
            acc += _dot(h_r.reshape(rows, MXU_DIM).astype(BF16), wcr_ref[w * slabs_per_window + s])
            acc += _dot(h_i.reshape(rows, MXU_DIM).astype(BF16), wci_ref[w * slabs_per_window + s])
        ys.append(jax.nn.gelu(acc).astype(BF16))

    z = _dot(jnp.concatenate(ys, axis=1), wglu_ref[...])
    o = _rms(z[:, :S5_WIDTH] * jax.nn.sigmoid(z[:, S5_WIDTH:]), og_ref[...])
    for c in range(n_lane_blk):
        op_ref[c] = o[:, c * LANES:(c + 1) * LANES]

    for j in range(SUBLANES):
        for i0 in range(0, seg, 2 * SUBLANES):
            for c in range(n_lane_blk):
                o_ref[j * seg + i0:j * seg + i0 + 2 * SUBLANES, c * LANES:(c + 1) * LANES] = (
                    op_ref[c, pl.ds(i0 * SUBLANES + j, 2 * SUBLANES, stride=SUBLANES), :].astype(BF16))


def _s5_weights(lam_re, lam_im, b_re, b_im, c_re, c_im, d, log_step, seg):
    step = jnp.exp(log_step)[:, None]
    lr = jnp.minimum(lam_re, EIG_CLIP)
    li = lam_im
    mag = jnp.exp(lr * step)
    ar = mag * jnp.cos(li * step)
    ai = mag * jnp.sin(li * step)
    den = lr * lr + li * li
    fr = ((ar - 1.0) * lr + ai * li) / den
    fi = (ai * lr - (ar - 1.0) * li) / den
    bbar_r = fr[..., None] * b_re - fi[..., None] * b_im
    bbar_i = fr[..., None] * b_im + fi[..., None] * b_re
    t = jnp.arange(1, seg + 1, dtype=F32)[:, None, None]
    pm = jnp.exp(lr * step * t)
    p_r = jnp.repeat((pm * jnp.cos(li * step * t)).reshape(seg, S5_COLS), SUBLANES, axis=0)
    p_i = jnp.repeat((pm * jnp.sin(li * step * t)).reshape(seg, S5_COLS), SUBLANES, axis=0)

    gk = MXU_DIM // S5_GROUP_CH

    def in_map(bbar):
        bb = bbar.reshape(S5_GROUPS // gk, gk, S5_STATE, S5_GROUP_CH)
        w = jnp.einsum('kgnh,gG->kghGn', bb, jnp.eye(gk, dtype=F32))
        return w.reshape(S5_GROUPS // gk, MXU_DIM, gk * S5_STATE).astype(BF16)

    gs = MXU_DIM // S5_STATE
    n_slab = S5_COLS // MXU_DIM
    per_win = MXU_DIM // (gs * S5_GROUP_CH)

    def out_map(c):
        cc = c.reshape(n_slab // per_win, per_win, gs, S5_GROUP_CH, S5_STATE)
        w = jnp.einsum('wsghn,sS,gG->wsgnSGh', cc, jnp.eye(per_win, dtype=F32), jnp.eye(gs, dtype=F32))
        return w.reshape(n_slab, MXU_DIM, MXU_DIM).astype(BF16)

    return dict(wbr=in_map(bbar_r), wbi=in_map(bbar_i), wcr=out_map(c_re), wci=out_map(-c_im),
                ar=ar.reshape(1, S5_COLS), ai=ai.reshape(1, S5_COLS), pr=p_r, pi=p_i,
                d=d.reshape(1, S5_WIDTH))


def _s5_call(x, x_first, norm_g, w_in, sw, w_glu, og, seg, chained, h0=None, prev=None):
    rows = seg * SUBLANES
    const2 = lambda *_: (0, 0)
    const3 = lambda *_: (0, 0, 0)
    if chained:
        n_chunk = SEQ // rows
        grid = (BATCH, n_chunk)
        row_map = lambda b, c: (b * n_chunk + c, 0)
        x_map = row_map
        end_map = lambda b, c: (b, 0, 0)
        n_end = BATCH
        sem = ("parallel", "arbitrary")
    else:
        grid = (DEC_BATCH // SUBLANES,)
        row_map = lambda k: (N_PROMPT // rows + k, 0)
        x_map = lambda k: (x_first + k, 0)
        end_map = lambda k: (k, 0, 0)
        n_end = DEC_BATCH // SUBLANES
        sem = ("parallel",)
    half = S5_COLS // 2
    in_specs = [pl.BlockSpec((rows, D_MODEL), x_map),
                pl.BlockSpec((1, D_MODEL), const2),
                pl.BlockSpec((D_MODEL, S5_WIDTH), const2),
                pl.BlockSpec((2, MXU_DIM, half), const3),
                pl.BlockSpec((2, MXU_DIM, half), const3),
                pl.BlockSpec((S5_COLS // MXU_DIM, MXU_DIM, MXU_DIM), const3),
                pl.BlockSpec((S5_COLS // MXU_DIM, MXU_DIM, MXU_DIM), const3),
                pl.BlockSpec((1, S5_COLS), const2),
                pl.BlockSpec((1, S5_COLS), const2),
                pl.BlockSpec((rows, S5_COLS), const2),
                pl.BlockSpec((rows, S5_COLS), const2),
                pl.BlockSpec((1, S5_WIDTH), const2),
                pl.BlockSpec((S5_WIDTH, 2 * S5_WIDTH), const2),
                pl.BlockSpec((1, S5_WIDTH), const2)]
    args = [x, norm_g.reshape(1, D_MODEL), w_in[:, :S5_WIDTH], sw['wbr'], sw['wbi'], sw['wcr'], sw['wci'], sw['ar'], sw['ai'],
            sw['pr'][:rows], sw['pi'][:rows], sw['d'], w_glu, og.reshape(1, S5_WIDTH)]
    scratch = [pltpu.VMEM((S5_WIDTH // LANES, rows, LANES), F32),
               pltpu.VMEM((rows, S5_COLS), F32), pltpu.VMEM((rows, S5_COLS), F32),
               pltpu.VMEM((SUBLANES, S5_COLS), F32), pltpu.VMEM((SUBLANES, S5_COLS), F32),
               pltpu.VMEM((S5_WIDTH // LANES, rows, LANES), F32)]
    aliases = {}
    if chained:
        scratch += [pltpu.VMEM((1, S5_COLS), F32), pltpu.VMEM((1, S5_COLS), F32)]
    else:
        in_specs += [pl.BlockSpec((None, SUBLANES, S5_COLS), end_map),
                     pl.BlockSpec((None, SUBLANES, S5_COLS), end_map)]
        args += [h0[0], h0[1]]
    if prev is not None:
        in_specs.append(pl.BlockSpec(memory_space=pl.ANY))
        args.append(prev)
        aliases = {len(args) - 1: 0}
    kern = functools.partial(_s5_kernel, seg=seg, chained=chained)
    if prev is not None:
        kern = _drop_inputs(kern, len(args) - 1, 1)
    return pl.pallas_call(
        kern,
        grid=grid,
        in_specs=in_specs,
        out_specs=[pl.BlockSpec((rows, S5_WIDTH), row_map),
                   pl.BlockSpec((None, SUBLANES, S5_COLS), end_map),
                   pl.BlockSpec((None, SUBLANES, S5_COLS), end_map)],
        out_shape=[jax.ShapeDtypeStruct((N_TOK, S5_WIDTH), BF16),
                   jax.ShapeDtypeStruct((n_end, SUBLANES, S5_COLS), F32),
                   jax.ShapeDtypeStruct((n_end, SUBLANES, S5_COLS), F32)],
        scratch_shapes=scratch,
        input_output_aliases=aliases,
        compiler_params=_params(*sem),
        name="s5_chained" if chained else "s5_streams",
    )(*args)


def _drop_inputs(kern, start, n):
    def wrapped(*refs):
        return kern(*refs[:start], *refs[start + n:])
    return wrapped


def _gmlp_kernel(*refs, chunk, emit_vn):
    if emit_vn:
        x_ref, ng_ref, win_ref, lng_ref, lnb_ref, ws_ref, bst_ref, og_ref, o_ref, vn_ref, acc_ref = refs
    else:
        x_ref, ng_ref, win_ref, lng_ref, lnb_ref, ws_ref, bst_ref, og_ref, o_ref, acc_ref = refs
    rows = x_ref.shape[0]
    uv = _dot(_rms(x_ref[...], ng_ref[...]).astype(BF16), win_ref[...])
    u = jax.nn.gelu(uv[:, :GM_WIDTH])
    v = jax.nn.gelu(uv[:, GM_WIDTH:])
    tri = (lax.broadcasted_iota(jnp.int32, (chunk, chunk), 0)
           >= lax.broadcasted_iota(jnp.int32, (chunk, chunk), 1))
    for h in range(GM_HEADS):
        cols = slice(h * GM_HEAD_DIM, (h + 1) * GM_HEAD_DIM)
        vh = v[:, cols]
        mu = jnp.mean(vh, axis=-1, keepdims=True)
        cen = vh - mu
        var = jnp.mean(cen * cen, axis=-1, keepdims=True)
        vn = cen * lax.rsqrt(var + EPS) * lng_ref[:, cols] + lnb_ref[:, cols]
        if emit_vn:
            vn_ref[:, cols] = vn
        w = jnp.where(tri, ws_ref[h], 0.0).astype(BF16)
        vb = vn.astype(BF16)
        bias = bst_ref[:, h:h + 1]
        for c in range(rows // chunk):
            rs = slice(c * chunk, (c + 1) * chunk)
            acc_ref[rs, cols] = u[rs, cols] * (_dot(w, vb[rs]) + bias)
    o_ref[...] = _rms(acc_ref[...], og_ref[...]).astype(BF16)


def _gmlp_call(x, x_first, norm_g, w_in, ln_g, ln_b, w_s, b_s, og, rows, chunk, first_block, n_blocks, emit_vn,
               prev=None):
    const2 = lambda i: (0, 0)
    in_specs = [pl.BlockSpec((rows, D_MODEL), lambda i: (x_first + i, 0)),
                pl.BlockSpec((1, D_MODEL), const2),
                pl.BlockSpec((D_MODEL, 2 * GM_WIDTH), const2),
                pl.BlockSpec((1, GM_WIDTH), const2),
                pl.BlockSpec((1, GM_WIDTH), const2),
                pl.BlockSpec((GM_HEADS, chunk, chunk), lambda i: (0, 0, 0)),
                pl.BlockSpec((chunk, GM_HEADS), const2),
                pl.BlockSpec((1, GM_WIDTH), const2)]
    args = [x, norm_g.reshape(1, D_MODEL), w_in[:, S5_WIDTH:], ln_g.reshape(1, GM_WIDTH), ln_b.reshape(1, GM_WIDTH),
            w_s[:, :chunk, :chunk], b_s[:, :chunk].T, og.reshape(1, GM_WIDTH)]
    out_specs = [pl.BlockSpec((rows, GM_WIDTH), lambda i: (first_block + i, 0))]
    out_shape = [jax.ShapeDtypeStruct((N_TOK, GM_WIDTH), BF16)]
    if emit_vn:
        out_specs.append(pl.BlockSpec((rows, GM_WIDTH), lambda i: (i, 0)))
        out_shape.append(jax.ShapeDtypeStruct((n_blocks * rows, GM_WIDTH), F32))
    kern = functools.partial(_gmlp_kernel, chunk=chunk, emit_vn=emit_vn)
    aliases = {}
    if prev is not None:
        in_specs.append(pl.BlockSpec(memory_space=pl.ANY))
        args.append(prev)
        aliases = {len(args) - 1: 0}
        kern = _drop_inputs(kern, len(args) - 1, 1)
    return pl.pallas_call(
        kern,
        grid=(n_blocks,),
        in_specs=in_specs,
        out_specs=out_specs,
        out_shape=out_shape,
        scratch_shapes=[pltpu.VMEM((rows, GM_WIDTH), F32)],
        input_output_aliases=aliases,
        compiler_params=_params("parallel"),
        name="gmlp_vn" if emit_vn else "gmlp",
    )(*args)


def _post_mix_kernel(x_ref, a_ref, b_ref, wa_ref, wb_ref, xg_ref, wq_ref, k_ref, v_ref, wo_ref,
                     fg_ref, wt_ref, rb_ref, x2_ref, h_ref, e_ref, gate_ref, cnt_ref, att_ref, *, n_batch, kv_layer):
    rows = x_ref.shape[0]
    tb = rows // n_batch
    x = x_ref[...] + _dot(a_ref[...], wa_ref[...]) + _dot(b_ref[...], wb_ref[...])
    q = _dot(_rms(x, xg_ref[...]).astype(BF16), wq_ref[...]).astype(BF16)
    scale = XA_HEAD_DIM ** -0.5
    for b in range(n_batch):
        rs = slice(b * tb, (b + 1) * tb)
        for h in range(XA_HEADS):
            cols = slice(h * XA_HEAD_DIM, (h + 1) * XA_HEAD_DIM)
            if kv_layer is None:
                k_h, v_h = k_ref[b, :, cols], v_ref[b, :, cols]
            else:
                k_h, v_h = k_ref[b, :, h, :].astype(BF16), v_ref[b, :, h, :].astype(BF16)
            s = lax.dot_general(q[rs, cols], k_h, (((1,), (1,)), ((), ())),
                                preferred_element_type=F32) * scale
            e = jnp.exp(s - jnp.max(s, axis=-1, keepdims=True))
            p = e / jnp.sum(e, axis=-1, keepdims=True)
            att_ref[rs, cols] = _dot(p.astype(BF16), v_h).astype(BF16)
    x = x + _dot(att_ref[...], wo_ref[...])
    x2_ref[...] = x
    _route_rows(x, fg_ref, wt_ref, rb_ref, h_ref, e_ref, gate_ref, cnt_ref)


def _post_mix(x, x_first, a, b, w_out, xg, wq, k, v, wo, fg, rw, rb,
              rows, n_batch, first_block, n_blocks, blocks_per_kv, prev=None, kv_layer=None):
    const2 = lambda i: (0, 0)
    if kv_layer is None:
        kv_spec = pl.BlockSpec((n_batch, N_MEM, D_MODEL), lambda i: (i // blocks_per_kv, 0, 0))
    else:
        kv_spec = pl.BlockSpec((None, n_batch, N_MEM, XA_HEADS, XA_HEAD_DIM),
                               lambda i: (kv_layer, i // blocks_per_kv, 0, 0, 0))
    row_map = lambda i: (first_block + i, 0)
    d = D_MODEL
    in_specs = [pl.BlockSpec((rows, d), lambda i: (x_first + i, 0)),
                pl.BlockSpec((rows, S5_WIDTH), row_map),
                pl.BlockSpec((rows, GM_WIDTH), row_map),
                pl.BlockSpec((S5_WIDTH, d), const2),
                pl.BlockSpec((GM_WIDTH, d), const2),
                pl.BlockSpec((1, d), const2),
                pl.BlockSpec((d, d), const2),
                kv_spec, kv_spec,
                pl.BlockSpec((d, d), const2),
                pl.BlockSpec((1, d), const2),
                pl.BlockSpec((N_EXPERTS, d), const2),
                pl.BlockSpec((N_EXPERTS, 1), const2)]
    args = [x, a, b, w_out[:S5_WIDTH], w_out[S5_WIDTH:], xg.reshape(1, d), wq, k, v, wo,
            fg.reshape(1, d), rw.T.astype(BF16), rb.reshape(N_EXPERTS, 1)]
    kern = functools.partial(_post_mix_kernel, n_batch=n_batch, kv_layer=kv_layer)
    aliases = {}
    if prev is not None:
        n_in = len(args)
        in_specs += [pl.BlockSpec(memory_space=pl.ANY)] * len(prev)
        args += list(prev)
        aliases = {n_in + o: o for o in range(len(prev))}
        kern = _drop_inputs(kern, n_in, len(prev))
    return pl.pallas_call(
        kern,
        grid=(n_blocks,),
        in_specs=in_specs,
        out_specs=[pl.BlockSpec((rows, d), row_map),
                   pl.BlockSpec((rows // SUBLANES,) + PIECE_TILE, lambda i: (first_block + i, 0, 0, 0)),
                   pl.BlockSpec((TOP_K, rows), lambda i: (0, first_block + i)),
                   pl.BlockSpec((TOP_K, rows), lambda i: (0, first_block + i)),
                   pl.BlockSpec((None, N_EXPERTS, LANES), lambda i: (i, 0, 0))],
        out_shape=[jax.ShapeDtypeStruct((N_TOK, d), F32),
                   jax.ShapeDtypeStruct((N_TOK // SUBLANES,) + PIECE_TILE, F32),
                   jax.ShapeDtypeStruct((TOP_K, N_TOK), jnp.int32),
                   jax.ShapeDtypeStruct((TOP_K, N_TOK), F32),
                   jax.ShapeDtypeStruct((n_blocks, N_EXPERTS, LANES), F32)],
        scratch_shapes=[pltpu.VMEM((rows, d), BF16)],
        input_output_aliases=aliases,
        compiler_params=_params("parallel"),
        name="post_mix",
    )(*args)


def _moe_kernel(be_ref, nu_ref, first_ref, slot_ref, nxt_ref, parts_ref,
                x_ref, wg_hbm, wu_hbm, wd_hbm, bg_ref, bu_ref, bd_ref, o_ref,
                wbuf_ref, wgb_ref, wub_ref, wdb_ref, sem, *, layer):
    i = pl.program_id(0)
    active = i < nu_ref[0]

    def copies(e, s):
        return [pltpu.make_async_copy(w.at[layer, e], wbuf_ref.at[s, j], sem.at[s, j])
                for j, w in enumerate((wg_hbm, wu_hbm, wd_hbm))]

    @pl.when(jnp.logical_and(active, first_ref[i] == 1))
    def _():
        s = slot_ref[i]

        @pl.when(i == 0)
        def _():
            for c in copies(be_ref[0], s):
                c.start()

        @pl.when(nxt_ref[i] >= 0)
        def _():
            for c in copies(nxt_ref[i], 1 - s):
                c.start()

        for c in copies(be_ref[i], s):
            c.wait()

        def cast(r, _):
            rs = pl.ds(pl.multiple_of(r * MOE_CAST_ROWS, MOE_CAST_ROWS), MOE_CAST_ROWS)
            wgb_ref[rs, :] = wbuf_ref[s, 0, rs, :].astype(BF16)
            wub_ref[rs, :] = wbuf_ref[s, 1, rs, :].astype(BF16)
            wdb_ref[rs, :] = wbuf_ref[s, 2, rs, :].astype(BF16)
            return 0
        lax.fori_loop(0, D_MODEL // MOE_CAST_ROWS, cast, 0)

    def experts(n_rows):
        words = lax.bitcast_convert_type(
            jnp.concatenate([x_ref[:n_rows // SUBLANES, c].reshape(n_rows, LANES) for c in range(PIECES_PER_ROW)],
                            axis=1), jnp.uint32)
        x = jnp.concatenate([lax.bitcast_convert_type(words << 16, F32),
                             lax.bitcast_convert_type(words & jnp.uint32(HIGH_HALF), F32)], axis=1).astype(BF16)
        g = jnp.minimum(_dot(x, wgb_ref[...]) + bg_ref[...], SWIGLU_LIMIT)
        u = jnp.clip(_dot(x, wub_ref[...]) + bu_ref[...], -SWIGLU_LIMIT, SWIGLU_LIMIT)
        a = g * jax.nn.sigmoid(SWIGLU_ALPHA * g) * (u + 1.0)
        y = _dot(a.astype(BF16), wdb_ref[...]) + bd_ref[...]
        for c in range(OUT_PIECES):
            o_ref[:n_rows // SUBLANES, c] = y[:, c * LANES:(c + 1) * LANES].reshape(n_rows // SUBLANES, SUBLANES, LANES)

    for p in range(1, MOE_ROWS // MOE_PART + 1):
        pl.when(jnp.logical_and(active, parts_ref[i] == p))(functools.partial(experts, p * MOE_PART))


def _moe_call(layer, xs, sched, w_gate, b_gate, w_up, b_up, w_down, b_down):
    def piece_map(i, be, nu, *_):
        return (jnp.minimum(i, nu[0] - 1), 0, 0, 0)

    def b_map(i, be, *_):
        return (layer, be[i], 0, 0)

    b_spec = pl.BlockSpec((None, None, 1, D_FF), b_map)
    w_spec = pl.BlockSpec(memory_space=pl.ANY)
    grid_spec = pltpu.PrefetchScalarGridSpec(
        num_scalar_prefetch=6,
        grid=(MOE_BLOCKS,),
        in_specs=[pl.BlockSpec((MOE_ROWS // SUBLANES,) + PIECE_TILE, piece_map),
                  w_spec, w_spec, w_spec, b_spec, b_spec, b_spec],
        out_specs=pl.BlockSpec((MOE_ROWS // SUBLANES,) + OUT_TILE, piece_map),
        scratch_shapes=[pltpu.VMEM((2, 3, D_MODEL, D_FF), F32)]
        + [pltpu.VMEM((D_MODEL, D_FF), BF16)] * 3
        + [pltpu.SemaphoreType.DMA((2, 3))],
    )
    shape4 = (DEPTH, N_EXPERTS, 1, D_FF)
    return pl.pallas_call(
        functools.partial(_moe_kernel, layer=layer),
        grid_spec=grid_spec,
        out_shape=jax.ShapeDtypeStruct((MOE_ROWS_TOTAL // SUBLANES,) + OUT_TILE, F32),
        compiler_params=pltpu.CompilerParams(dimension_semantics=("arbitrary",),
                                             vmem_limit_bytes=MOE_VMEM_LIMIT),
        name="moe_experts",
    )(*sched, xs, w_gate, w_up, w_down, b_gate.reshape(shape4), b_up.reshape(shape4),
      b_down.reshape(shape4))


DISPATCH_TOKENS = 48
DMA_PIECES = 64
DISPATCH_DMAS = DISPATCH_TOKENS * PIECES_PER_ROW // DMA_PIECES
DISPATCH_CHUNKS = N_TOK // SC_WORKERS // DISPATCH_TOKENS
IDX_ROWS = 16
SC_LANES = 16
MOE_ROWS_TOTAL = MOE_BLOCKS * MOE_ROWS
TILE_PIECES = PIECES_PER_ROW * SUBLANES


def _piece_index_rows(p_v, idx_v, row0, tok0, n_dma, pieces_per_row, dma_pieces):
    lane = lax.iota(jnp.int32, SC_LANES)
    in_tile = lane & (SUBLANES - 1)
    piece = (lane >> 3) * SUBLANES
    per_vec = SC_LANES // SUBLANES
    tile_pieces = pieces_per_row * SUBLANES
    tiles_per_dma = dma_pieces // tile_pieces
    for m in range(n_dma):
        for tr in range(tiles_per_dma):
            first = plsc.load_gather(p_v, [in_tile + (tok0 + (m * tiles_per_dma + tr) * SUBLANES)])
            for v in range(pieces_per_row // per_vec):
                idx_v[row0 + m, pl.ds(tr * tile_pieces + v * SC_LANES, SC_LANES)] = (
                    first + (piece + v * per_vec * SUBLANES))


COLLECT_ROWS = 48
COLLECT_PIECES = 128
OUT_PIECES = D_MODEL // LANES
OUT_TILE = (OUT_PIECES, SUBLANES, LANES)


def _collect(ys4, dest_t):
    first_piece = ((dest_t // SUBLANES) * (OUT_PIECES * SUBLANES) + dest_t % SUBLANES).reshape(-1)
    chunk_pieces = COLLECT_ROWS * OUT_PIECES
    n_dma = chunk_pieces // COLLECT_PIECES
    chunks_per_choice = N_TOK // SC_WORKERS // COLLECT_ROWS
    n_chunks = TOP_K * chunks_per_choice
    mesh = plsc.VectorSubcoreMesh(core_axis_name="c", subcore_axis_name="s")

    @functools.partial(
        pl.kernel, mesh=mesh,
        out_type=jax.ShapeDtypeStruct((TOP_K * N_TOK * OUT_PIECES, LANES), F32),
        scratch_types=[pltpu.VMEM((chunk_pieces, LANES), F32), pltpu.VMEM((chunk_pieces, LANES), F32),
                       pltpu.VMEM((SUBLANES, COLLECT_PIECES), jnp.int32),
                       pltpu.VMEM((SUBLANES, COLLECT_PIECES), jnp.int32),
                       pltpu.VMEM((COLLECT_ROWS,), jnp.int32),
                       pltpu.SemaphoreType.DMA, pltpu.SemaphoreType.DMA],
        compiler_params=pltpu.CompilerParams(needs_layout_passes=False),
    )
    def gather_rows(y_hbm, p_hbm, o_hbm, rows0, rows1, idx0, idx1, p_v, sem0, sem1):
        wid = lax.axis_index("s") * SC_CORES + lax.axis_index("c")
        bufs = ((rows0, idx0, sem0), (rows1, idx1, sem1))

        def first_row(j):
            k = j // chunks_per_choice
            return k * N_TOK + (wid * chunks_per_choice + (j - k * chunks_per_choice)) * COLLECT_ROWS

        def copies(b):
            rows_v, idx_v, sem = bufs[b]
            return [pltpu.make_async_copy(y_hbm.at[idx_v.at[m]],
                                          rows_v.at[pl.ds(m * COLLECT_PIECES, COLLECT_PIECES)], sem)
                    for m in range(n_dma)]

        def fire(j, b):
            pltpu.sync_copy(p_hbm.at[pl.ds(pl.multiple_of(first_row(j), SUBLANES), COLLECT_ROWS)], p_v)
            _piece_index_rows(p_v, bufs[b][1], 0, 0, n_dma, OUT_PIECES, COLLECT_PIECES)
            for cp in copies(b):
                cp.start()

        def drain(j, b):
            for cp in copies(b):
                cp.wait()
            pltpu.sync_copy(bufs[b][0],
                            o_hbm.at[pl.ds(pl.multiple_of(first_row(j) * OUT_PIECES, SUBLANES), chunk_pieces)])

        fire(0, 0)

        @pl.loop(0, n_chunks, step=2)
        def _(j):
            fire(j + 1, 1)
            drain(j, 0)
            pl.when(j + 2 < n_chunks)(lambda: fire(j + 2, 0))
            drain(j + 1, 1)

    picked = gather_rows(ys4.reshape(MOE_ROWS_TOTAL * OUT_PIECES, LANES), first_piece)
    return picked.reshape((TOP_K, N_TOK // SUBLANES) + OUT_TILE)


def _dispatch(h4, dest_t):
    first_piece = ((dest_t // SUBLANES) * TILE_PIECES + dest_t % SUBLANES).reshape(-1)
    chunk_pieces = DISPATCH_TOKENS * PIECES_PER_ROW
    mesh = plsc.VectorSubcoreMesh(core_axis_name="c", subcore_axis_name="s")

    @functools.partial(
        pl.kernel, mesh=mesh,
        out_type=jax.ShapeDtypeStruct((MOE_ROWS_TOTAL * PIECES_PER_ROW, LANES), F32),
        scratch_types=[pltpu.VMEM((chunk_pieces, LANES), F32), pltpu.VMEM((IDX_ROWS, DMA_PIECES), jnp.int32),
                       pltpu.VMEM((TOP_K * DISPATCH_TOKENS,), jnp.int32), pltpu.SemaphoreType.DMA],
        compiler_params=pltpu.CompilerParams(needs_layout_passes=False),
    )
    def scatter_rows(h_hbm, p_hbm, o_hbm, rows_v, idx_v, p_v, sem):
        wid = lax.axis_index("s") * SC_CORES + lax.axis_index("c")

        @pl.loop(0, DISPATCH_CHUNKS)
        def _(j):
            q = wid * DISPATCH_CHUNKS + j
            pltpu.sync_copy(h_hbm.at[pl.ds(pl.multiple_of(q * chunk_pieces, SUBLANES), chunk_pieces)], rows_v)
            for k in range(TOP_K):
                pltpu.sync_copy(
                    p_hbm.at[pl.ds(pl.multiple_of(k * N_TOK + q * DISPATCH_TOKENS, SUBLANES), DISPATCH_TOKENS)],
                    p_v.at[pl.ds(k * DISPATCH_TOKENS, DISPATCH_TOKENS)])
            for k in range(TOP_K):
                _piece_index_rows(p_v, idx_v, k * DISPATCH_DMAS, k * DISPATCH_TOKENS, DISPATCH_DMAS,
                                  PIECES_PER_ROW, DMA_PIECES)
            copies = [pltpu.async_copy(rows_v.at[pl.ds(m * DMA_PIECES, DMA_PIECES)],
                                       o_hbm.at[idx_v.at[k * DISPATCH_DMAS + m]], sem)
                      for k in range(TOP_K) for m in range(DISPATCH_DMAS)]
            for cp in copies:
                cp.wait()

    xs = scatter_rows(h4.reshape(N_TOK * PIECES_PER_ROW, LANES), first_piece)
    return xs.reshape((MOE_ROWS_TOTAL // SUBLANES,) + PIECE_TILE)


def _route(e_t, cnt_prompt, cnt_sample):
    cnt_sample = jnp.sum(cnt_sample[:, :, 0].reshape(SAMPLE_BLOCKS, -1, N_EXPERTS), axis=1)
    blk_cnt = jnp.concatenate([cnt_prompt[:, :, 0], cnt_sample]).astype(jnp.int32)
    def before(n):
        i = jnp.arange(n, dtype=jnp.int32)
        return i[None, :] < i[:, None]

    counts = jnp.sum(blk_cnt, axis=0)
    padded = ((counts + MOE_ROWS - 1) // MOE_ROWS) * MOE_ROWS
    pstart = jnp.sum(jnp.where(before(N_EXPERTS), padded[None, :], 0), axis=1)
    pend = pstart + padded
    base = pstart[None, :] + jnp.sum(jnp.where(before(blk_cnt.shape[0])[:, :, None], blk_cnt[None], 0), axis=1)
    dest_t = _dest(e_t, base.astype(F32)[:, :, None])
    n_used = jnp.sum(padded) // MOE_ROWS
    ids = jnp.arange(MOE_BLOCKS, dtype=jnp.int32)
    blk = jnp.minimum(ids, n_used - 1) * MOE_ROWS
    blk_exp = jnp.minimum(jnp.sum((pend[None, :] <= blk[:, None]).astype(jnp.int32), axis=1), N_EXPERTS - 1)
    eid = jnp.arange(N_EXPERTS, dtype=jnp.int32)
    of_blk = blk_exp[:, None] == eid[None, :]

    def lookup(table):
        return jnp.sum(jnp.where(of_blk, table[None, :], 0), axis=1)

    first = jnp.logical_and(ids < n_used, jnp.logical_or(ids == 0, blk_exp != jnp.roll(blk_exp, 1)))
    first = first.astype(jnp.int32)
    slot = jnp.sum(jnp.where(before(MOE_BLOCKS), first[None, :], 0), axis=1) % 2
    later = jnp.logical_and(eid[None, :] > eid[:, None], (counts > 0)[None, :])
    nxt_e = jnp.min(jnp.where(later, eid[None, :], N_EXPERTS), axis=1)
    nxt = lookup(jnp.where(nxt_e < N_EXPERTS, nxt_e, -1))
    real = jnp.clip(lookup(pstart + counts) - blk, 0, MOE_ROWS)
    parts = (real + MOE_PART - 1) // MOE_PART
    return dest_t, (blk_exp, n_used.reshape(1).astype(jnp.int32), first, slot.astype(jnp.int32),
                    nxt.astype(jnp.int32), parts.astype(jnp.int32))


def kernel(x_prompt, x_sample, mem_prompt, state_s5_re, state_s5_im, cache_mem_k, cache_mem_v, norm_mix_g, w_in, s5_lambda_re, s5_lambda_im, s5_b_re, s5_b_im, s5_c_re, s5_c_im, s5_d, s5_log_step, s5_w_glu, gm_ln_g, gm_ln_b, gm_w_s, gm_b_s, out_norm_g, w_out, xa_norm_g, mem_norm_g, w_xq, w_xk, w_xv, w_xo, ffn_norm_g, router_w, router_b, e_w_gate, e_b_gate, e_w_up, e_b_up, e_w_down, e_b_down, final_norm_g):
    xs = (x_prompt.reshape(N_PROMPT, D_MODEL), x_sample.reshape(N_SAMPLE, D_MODEL))
    mem = mem_prompt.reshape(BATCH * N_MEM, D_MODEL)
    outs = {k: [] for k in ('s5r_p', 's5i_p', 's5r_s', 's5i_s', 'gmv_s')}
    mem_kv = None
    for l in range(DEPTH):
        unified = len(xs) == 1

        def sample_first(rows):
            return N_PROMPT // rows if unified else 0

        in_proj = (norm_mix_g[l], w_in[l].astype(BF16))
        sw = _s5_weights(s5_lambda_re[l], s5_lambda_im[l], s5_b_re[l], s5_b_im[l], s5_c_re[l], s5_c_im[l],
                         s5_d[l], s5_log_step[l], S5_SEG_PROMPT)
        w_glu = s5_w_glu[l].astype(BF16)
        og = out_norm_g[l]
        ms5, er_p, ei_p = _s5_call(xs[0], 0, *in_proj, sw, w_glu, og[:S5_WIDTH], S5_SEG_PROMPT, True)
        h0 = (state_s5_re[l].reshape(DEC_BATCH // SUBLANES, SUBLANES, S5_COLS),
              state_s5_im[l].reshape(DEC_BATCH // SUBLANES, SUBLANES, S5_COLS))
        ms5, er_s, ei_s = _s5_call(xs[-1], sample_first(DEC_SEQ * SUBLANES), *in_proj, sw, w_glu, og[:S5_WIDTH],
                                   DEC_SEQ, False, h0=h0, prev=ms5)
        outs['s5r_p'].append(er_p[:, SUBLANES - 1].reshape(BATCH, S5_GROUPS, S5_STATE))
        outs['s5i_p'].append(ei_p[:, SUBLANES - 1].reshape(BATCH, S5_GROUPS, S5_STATE))
        outs['s5r_s'].append(er_s.reshape(DEC_BATCH, S5_GROUPS, S5_STATE))
        outs['s5i_s'].append(ei_s.reshape(DEC_BATCH, S5_GROUPS, S5_STATE))

        (mgm,) = _gmlp_call(xs[0], 0, *in_proj, gm_ln_g[l], gm_ln_b[l], gm_w_s[l], gm_b_s[l], og[S5_WIDTH:],
                            GM_ROWS_PROMPT, GM_CHUNK, 0, N_PROMPT // GM_ROWS_PROMPT, False)
        mgm, vn = _gmlp_call(xs[-1], sample_first(N_SAMPLE), *in_proj, gm_ln_g[l], gm_ln_b[l], gm_w_s[l],
                             gm_b_s[l], og[S5_WIDTH:], N_SAMPLE, DEC_SEQ, N_PROMPT // N_SAMPLE, 1, True, prev=mgm)
        outs['gmv_s'].append(vn.reshape(DEC_BATCH, DEC_SEQ, GM_HEADS, GM_HEAD_DIM))
        *mem_kv, mk, mv = _mem_kv(mem, mem_norm_g[l], w_xk[l].astype(BF16), w_xv[l].astype(BF16), l, prev=mem_kv)
        shared = (w_out[l].astype(BF16), xa_norm_g[l], w_xq[l].astype(BF16))
        tail = (w_xo[l].astype(BF16), ffn_norm_g[l], router_w[l], router_b[l])
        rows_s = XA_SAMPLE_BATCHES * DEC_SEQ
        post_p = _post_mix(xs[0], 0, ms5, mgm, *shared, mk, mv, *tail,
                           XA_ROWS_PROMPT, 1, 0, N_PROMPT // XA_ROWS_PROMPT, SEQ // XA_ROWS_PROMPT)
        post_s = _post_mix(xs[-1], sample_first(rows_s), ms5, mgm, *shared,
                           cache_mem_k, cache_mem_v, *tail,
                           rows_s, XA_SAMPLE_BATCHES, N_PROMPT // rows_s, N_SAMPLE // rows_s, 1,
                           prev=post_p[:4], kv_layer=l)
        x, h, e_t, gates_t = post_s[:4]

        dest_t, sched = _route(e_t, post_p[4], post_s[4])
        ys = _moe_call(l, _dispatch(h, dest_t), sched, e_w_gate, e_b_gate, e_w_up, e_b_up, e_w_down, e_b_down)
        x = _combine(x, _collect(ys, dest_t), gates_t.T, final_norm_g if l == DEPTH - 1 else None)
        xs = (x,)

    y_prompt = x[0].reshape(BATCH, SEQ, D_MODEL)
    y_sample = x[1].reshape(DEC_BATCH, DEC_SEQ, D_MODEL)
    st = jnp.stack
    return (y_prompt, y_sample, st(outs['s5r_p']), st(outs['s5i_p']), mem_kv[0], mem_kv[1],
            st(outs['s5r_s']), st(outs['s5i_s']), st(outs['gmv_s']))
```

```python
import functools

import jax
import jax.numpy as jnp
from jax import lax
from jax.experimental import pallas as pl
from jax.experimental.pallas import tpu as pltpu
from jax.experimental.pallas import tpu_sc as plsc

F32 = jnp.float32
BF16 = jnp.bfloat16

D_MODEL = 1024
BATCH = 4
SEQ = 4096
DEPTH = 2
DEC_BATCH = 16
DEC_SEQ = 32
S5_GROUPS = 32
S5_GROUP_CH = 16
S5_STATE = 64
S5_WIDTH = S5_GROUPS * S5_GROUP_CH
S5_COLS = S5_GROUPS * S5_STATE
GM_HEADS = 4
GM_HEAD_DIM = 128
GM_WIDTH = GM_HEADS * GM_HEAD_DIM
GM_CHUNK = 128
N_MEM = 256
XA_HEADS = 4
XA_HEAD_DIM = D_MODEL // XA_HEADS
N_EXPERTS = 32
TOP_K = 4
D_FF = D_MODEL
SWIGLU_LIMIT = 7.0
SWIGLU_ALPHA = 1.702
EPS = 1e-6
EIG_CLIP = -1e-4

N_PROMPT = BATCH * SEQ
N_SAMPLE = DEC_BATCH * DEC_SEQ
N_TOK = N_PROMPT + N_SAMPLE

SUBLANES = 8
LANES = 128
MXU_DIM = 256
VMEM_LIMIT = 48 * 1024 * 1024
MOE_VMEM_LIMIT = 60 * 1024 * 1024
SC_CORES = 2
SC_SUBCORES = 16
SC_WORKERS = SC_CORES * SC_SUBCORES

PACK_WIDTH = D_MODEL // 2
PIECES_PER_ROW = PACK_WIDTH // LANES
PIECE_TILE = (PIECES_PER_ROW, SUBLANES, LANES)
HIGH_HALF = 0xFFFF0000

ROW_BLOCK = 512
S5_SEG_PROMPT = 64
S5_COL_BLOCK = 1024
GM_ROWS_PROMPT = 512
XA_ROWS_PROMPT = 512
XA_SAMPLE_BATCHES = 4
MOE_ROWS = 1024
MOE_PART = 256
MOE_BLOCKS = -(-N_TOK * TOP_K // MOE_ROWS) + N_EXPERTS
MOE_CAST_ROWS = 128


def _params(*sem):
    return pltpu.CompilerParams(dimension_semantics=sem, vmem_limit_bytes=VMEM_LIMIT)


def _rms(x, g):
    return x * lax.rsqrt(jnp.mean(x * x, axis=-1, keepdims=True) + EPS) * g


def _dot(a, b):
    return jnp.dot(a, b, preferred_element_type=F32)


PROMPT_BLOCKS = N_PROMPT // ROW_BLOCK
SAMPLE_BLOCKS = N_SAMPLE // ROW_BLOCK


def _group_specs():
    return [pl.BlockSpec((ROW_BLOCK, D_MODEL), lambda i: (jnp.minimum(i, PROMPT_BLOCKS - 1), 0)),
            pl.BlockSpec((ROW_BLOCK, D_MODEL), lambda i: (jnp.maximum(i - PROMPT_BLOCKS, 0), 0))]


def _for_group(refs, body):
    i = pl.program_id(0)
    pl.when(i < PROMPT_BLOCKS)(lambda: body(refs[0]))
    pl.when(i >= PROMPT_BLOCKS)(lambda: body(refs[1]))


def _mem_kv_kernel(m_ref, g_ref, wk_ref, wv_ref, k5_ref, v5_ref, kb_ref, vb_ref):
    m = _rms(m_ref[...], g_ref[...]).astype(BF16)
    for w_ref, o5_ref, ob_ref in ((wk_ref, k5_ref, kb_ref), (wv_ref, v5_ref, vb_ref)):
        kv = _dot(m, w_ref[...])
        ob_ref[...] = kv.astype(BF16)
        for h in range(XA_HEADS):
            o5_ref[:, h, :] = kv[:, h * XA_HEAD_DIM:(h + 1) * XA_HEAD_DIM]


def _mem_kv(mem, g, wk, wv, layer, prev=None):
    d = D_MODEL
    const2 = lambda b: (0, 0)
    in_specs = [pl.BlockSpec((N_MEM, d), lambda b: (b, 0)),
                pl.BlockSpec((1, d), const2),
                pl.BlockSpec((d, d), const2),
                pl.BlockSpec((d, d), const2)]
    args = [mem, g.reshape(1, d), wk, wv]
    kern = _mem_kv_kernel
    aliases = {}
    if prev is not None:
        in_specs += [pl.BlockSpec(memory_space=pl.ANY)] * 2
        args += list(prev)
        aliases = {4: 0, 5: 1}
        kern = _drop_inputs(kern, 4, 2)
    out5 = pl.BlockSpec((None, None, N_MEM, XA_HEADS, XA_HEAD_DIM), lambda b: (layer, b, 0, 0, 0))
    outb = pl.BlockSpec((None, N_MEM, d), lambda b: (b, 0, 0))
    shape5 = jax.ShapeDtypeStruct((DEPTH, BATCH, N_MEM, XA_HEADS, XA_HEAD_DIM), F32)
    shapeb = jax.ShapeDtypeStruct((BATCH, N_MEM, d), BF16)
    return pl.pallas_call(
        kern,
        grid=(BATCH,),
        in_specs=in_specs,
        out_specs=[out5, out5, outb, outb],
        out_shape=[shape5, shape5, shapeb, shapeb],
        input_output_aliases=aliases,
        compiler_params=_params("parallel"),
        name="mem_kv",
    )(*args)


def _route_rows(x, g_ref, wt_ref, b_ref, h_ref, e_ref, gate_ref, cnt_ref):
    h = _rms(x, g_ref[...]).astype(BF16)
    bits = lax.bitcast_convert_type(h.astype(F32), jnp.uint32)
    words = lax.bitcast_convert_type(bits[:, PACK_WIDTH:] | (bits[:, :PACK_WIDTH] >> 16), F32)
    for c in range(PIECES_PER_ROW):
        h_ref[:, c] = words[:, c * LANES:(c + 1) * LANES].reshape(h.shape[0] // SUBLANES, SUBLANES, LANES)
    logits = lax.dot_general(wt_ref[...], h, (((1,), (1,)), ((), ())),
                             preferred_element_type=F32) + b_ref[...]
    sub = lax.broadcasted_iota(jnp.int32, logits.shape, 0)
    work = logits
    chosen = jnp.zeros(logits.shape, F32)
    vals, idxs = [], []
    for _ in range(TOP_K):
        m = jnp.max(work, axis=0, keepdims=True)
        idx = jnp.min(jnp.where(work == m, sub, N_EXPERTS), axis=0, keepdims=True)
        sel = sub == idx
        vals.append(m)
        idxs.append(idx)
        work = jnp.where(sel, -jnp.inf, work)
        chosen = jnp.where(sel, 1.0, chosen)
    ex = [jnp.exp(v - vals[0]) for v in vals]
    den = ex[0] + ex[1] + ex[2] + ex[3]
    gate_ref[...] = jnp.concatenate([e / den for e in ex], axis=0)
    e_ref[...] = jnp.concatenate(idxs, axis=0)
    cnt_ref[...] = jnp.broadcast_to(jnp.sum(chosen, axis=1, keepdims=True), (N_EXPERTS, LANES))


def _dest_kernel(e_ref, base_ref, d_ref):
    e = e_ref[...]
    rows = e.shape[1]
    sub = lax.broadcasted_iota(jnp.int32, (N_EXPERTS, rows), 0)
    hot = [sub == e[k:k + 1, :] for k in range(TOP_K)]
    chosen = sum(jnp.where(hk, 1.0, 0.0) for hk in hot)
    earlier = (lax.broadcasted_iota(jnp.int32, (rows, rows), 0)
               < lax.broadcasted_iota(jnp.int32, (rows, rows), 1))
    pos = base_ref[...] + _dot(chosen.astype(BF16), jnp.where(earlier, 1.0, 0.0).astype(BF16))
    d_ref[...] = jnp.concatenate(
        [jnp.sum(jnp.where(hk, pos, 0.0), axis=0, keepdims=True) for hk in hot], axis=0).astype(jnp.int32)


def _dest(e_t, base):
    n = e_t.shape[1]
    rows = ROW_BLOCK
    return pl.pallas_call(
        _dest_kernel,
        grid=(n // rows,),
        in_specs=[pl.BlockSpec((TOP_K, rows), lambda i: (0, i)),
                  pl.BlockSpec((None, N_EXPERTS, 1), lambda i: (i, 0, 0))],
        out_specs=pl.BlockSpec((TOP_K, rows), lambda i: (0, i)),
        out_shape=jax.ShapeDtypeStruct((TOP_K, n), jnp.int32),
        compiler_params=_params("parallel"),
        name="moe_dest",
    )(e_t, base)


def _combine_kernel(*refs, final):
    if final:
        x_ref, g0, g1, g2, g3, gate_ref, fg_ref, op_ref, os_ref = refs
    else:
        x_ref, g0, g1, g2, g3, gate_ref, o_ref = refs
    rows = x_ref.shape[0]
    slabs = []
    for c in range(OUT_PIECES):
        acc = x_ref[:, c * LANES:(c + 1) * LANES]
        for k, g in enumerate((g0, g1, g2, g3)):
            acc = acc + gate_ref[:, k:k + 1] * g[:, c].reshape(rows, LANES)
        slabs.append(acc)
    acc = jnp.concatenate(slabs, axis=1)
    if final:
        def store(o_ref):
            o_ref[...] = _rms(acc, fg_ref[...])
        _for_group((op_ref, os_ref), store)
    else:
        o_ref[...] = acc


def _combine(x, picked, gates, final_g=None):
    n, d = x.shape
    rows = ROW_BLOCK
    final = final_g is not None
    in_specs = [pl.BlockSpec((rows, d), lambda i: (i, 0))]
    in_specs += [pl.BlockSpec((None, rows // SUBLANES) + OUT_TILE, functools.partial(lambda i, k: (k, i, 0, 0, 0), k=k))
                 for k in range(TOP_K)]
    in_specs.append(pl.BlockSpec((rows, TOP_K), lambda i: (i, 0)))
    args = [x, picked, picked, picked, picked, gates]
    out_specs = pl.BlockSpec((rows, d), lambda i: (i, 0))
    out_shape = jax.ShapeDtypeStruct((n, d), F32)
    if final:
        in_specs.append(pl.BlockSpec((1, d), lambda i: (0, 0)))
        args.append(final_g.reshape(1, d))
        out_specs = _group_specs()
        out_shape = [jax.ShapeDtypeStruct((N_PROMPT, d), F32), jax.ShapeDtypeStruct((N_SAMPLE, d), F32)]
    return pl.pallas_call(
        functools.partial(_combine_kernel, final=final),
        grid=(n // rows,),
        in_specs=in_specs,
        out_specs=out_specs,
        out_shape=out_shape,
        compiler_params=_params("arbitrary" if final else "parallel"),
        name="moe_combine",
    )(*args)


def _cmul_add(ar, ai, hr, hi, br, bi):
    return ar * hr - ai * hi + br, ar * hi + ai * hr + bi


def _s5_kernel(*refs, seg, chained):
    if chained:
        (x_ref, ng_ref, win_ref, wbr_ref, wbi_ref, wcr_ref, wci_ref, ar_ref, ai_ref, pr_ref, pi_ref, d_ref,
         wglu_ref, og_ref, o_ref, er_ref, ei_ref,
         up_ref, hr_ref, hi_ref, sr_ref, si_ref, op_ref, cr_ref, ci_ref) = refs
    else:
        (x_ref, ng_ref, win_ref, wbr_ref, wbi_ref, wcr_ref, wci_ref, ar_ref, ai_ref, pr_ref, pi_ref, d_ref,
         wglu_ref, og_ref, h0r_ref, h0i_ref, o_ref, er_ref, ei_ref,
         up_ref, hr_ref, hi_ref, sr_ref, si_ref, op_ref) = refs
    rows = seg * SUBLANES
    half = S5_COLS // 2

    u = _dot(_rms(x_ref[...], ng_ref[...]).astype(BF16), win_ref[...])
    n_lane_blk = S5_WIDTH // LANES
    for j in range(SUBLANES):
        for i0 in range(0, seg, SUBLANES):
            for c in range(n_lane_blk):
                up_ref[c, pl.ds(i0 * SUBLANES + j, SUBLANES, stride=SUBLANES), :] = (
                    u[j * seg + i0:j * seg + i0 + SUBLANES, c * LANES:(c + 1) * LANES])

    up = jnp.concatenate([up_ref[c] for c in range(n_lane_blk)], axis=1)
    ub = up.astype(BF16)
    for k in range(2):
        uk = ub[:, k * MXU_DIM:(k + 1) * MXU_DIM]
        hr_ref[:, k * half:(k + 1) * half] = _dot(uk, wbr_ref[k])
        hi_ref[:, k * half:(k + 1) * half] = _dot(uk, wbi_ref[k])

    for cb in range(S5_COLS // S5_COL_BLOCK):
        cols = slice(cb * S5_COL_BLOCK, (cb + 1) * S5_COL_BLOCK)
        ar = jnp.broadcast_to(ar_ref[:, cols], (SUBLANES, S5_COL_BLOCK))
        ai = jnp.broadcast_to(ai_ref[:, cols], (SUBLANES, S5_COL_BLOCK))

        def step(i, carry, cols=cols, ar=ar, ai=ai):
            r = pl.multiple_of(i * SUBLANES, SUBLANES)
            h_r, h_i = _cmul_add(ar, ai, carry[0], carry[1],
                                 hr_ref[pl.ds(r, SUBLANES), cols], hi_ref[pl.ds(r, SUBLANES), cols])
            hr_ref[pl.ds(r, SUBLANES), cols] = h_r
            hi_ref[pl.ds(r, SUBLANES), cols] = h_i
            return h_r, h_i

        zero = jnp.zeros((SUBLANES, S5_COL_BLOCK), F32)
        lax.fori_loop(0, seg, step, (zero, zero), unroll=2)

    end_r = hr_ref[rows - SUBLANES:rows, :]
    end_i = hi_ref[rows - SUBLANES:rows, :]
    as_r = pr_ref[rows - 1:rows, :]
    as_i = pi_ref[rows - 1:rows, :]
    if chained:
        @pl.when(pl.program_id(1) == 0)
        def _():
            cr_ref[...] = jnp.zeros_like(cr_ref)
            ci_ref[...] = jnp.zeros_like(ci_ref)

        cur_r, cur_i = cr_ref[...], ci_ref[...]
        for j in range(SUBLANES):
            sr_ref[j:j + 1, :] = cur_r
            si_ref[j:j + 1, :] = cur_i
            cur_r, cur_i = _cmul_add(as_r, as_i, cur_r, cur_i, end_r[j:j + 1], end_i[j:j + 1])
        cr_ref[...] = cur_r
        ci_ref[...] = cur_i
    else:
        sr_ref[...] = h0r_ref[...]
        si_ref[...] = h0i_ref[...]
    t_r, t_i = _cmul_add(as_r, as_i, sr_ref[...], si_ref[...], end_r, end_i)
    er_ref[...] = t_r
    ei_ref[...] = t_i

    slabs_per_window = MXU_DIM // (MXU_DIM // S5_STATE * S5_GROUP_CH)
    ys = []
    for w in range(S5_WIDTH // MXU_DIM):
        acc = d_ref[:, w * MXU_DIM:(w + 1) * MXU_DIM] * jnp.concatenate(
            [up_ref[c] for c in range(w * MXU_DIM // LANES, (w + 1) * MXU_DIM // LANES)], axis=1)
        for s in range(slabs_per_window):
            cols = slice((w * slabs_per_window + s) * MXU_DIM, (w * slabs_per_window + s + 1) * MXU_DIM)
            shape3 = (seg, SUBLANES, MXU_DIM)
            h_r, h_i = _cmul_add(pr_ref[:, cols].reshape(shape3), pi_ref[:, cols].reshape(shape3),
                                 sr_ref[:, cols][None], si_ref[:, cols][None],
                                 hr_ref[:, cols].reshape(shape3), hi_ref[:, cols].reshape(shape3))
            acc += _dot(h_r.reshape(rows, MXU_DIM).astype(BF16), wcr_ref[w * slabs_per_window + s])
            acc += _dot(h_i.reshape(rows, MXU_DIM).astype(BF16), wci_ref[w * slabs_per_window + s])
        ys.append(jax.nn.gelu(acc).astype(BF16))

    z = _dot(jnp.concatenate(ys, axis=1), wglu_ref[...])
    o = _rms(z[:, :S5_WIDTH] * jax.nn.sigmoid(z[:, S5_WIDTH:]), og_ref[...])
    for c in range(n_lane_blk):
        op_ref[c] = o[:, c * LANES:(c + 1) * LANES]

    for j in range(SUBLANES):
        for i0 in range(0, seg, 2 * SUBLANES):
            for c in range(n_lane_blk):
                o_ref[j * seg + i0:j * seg + i0 + 2 * SUBLANES, c * LANES:(c + 1) * LANES] = (
                    op_ref[c, pl.ds(i0 * SUBLANES + j, 2 * SUBLANES, stride=SUBLANES), :].astype(BF16))


def _s5_weights(lam_re, lam_im, b_re, b_im, c_re, c_im, d, log_step, seg):
    step = jnp.exp(log_step)[:, None]
    lr = jnp.minimum(lam_re, EIG_CLIP)
    li = lam_im
    mag = jnp.exp(lr * step)
    ar = mag * jnp.cos(li * step)
    ai = mag * jnp.sin(li * step)
    den = lr * lr + li * li
    fr = ((ar - 1.0) * lr + ai * li) / den
    fi = (ai * lr - (ar - 1.0) * li) / den
    bbar_r = fr[..., None] * b_re - fi[..., None] * b_im
    bbar_i = fr[..., None] * b_im + fi[..., None] * b_re
    t = jnp.arange(1, seg + 1, dtype=F32)[:, None, None]
    pm = jnp.exp(lr * step * t)
    p_r = jnp.repeat((pm * jnp.cos(li * step * t)).reshape(seg, S5_COLS), SUBLANES, axis=0)
    p_i = jnp.repeat((pm * jnp.sin(li * step * t)).reshape(seg, S5_COLS), SUBLANES, axis=0)

    gk = MXU_DIM // S5_GROUP_CH

    def in_map(bbar):
        bb = bbar.reshape(S5_GROUPS // gk, gk, S5_STATE, S5_GROUP_CH)
        w = jnp.einsum('kgnh,gG->kghGn', bb, jnp.eye(gk, dtype=F32))
        return w.reshape(S5_GROUPS // gk, MXU_DIM, gk * S5_STATE).astype(BF16)

    gs = MXU_DIM // S5_STATE
    n_slab = S5_COLS // MXU_DIM
    per_win = MXU_DIM // (gs * S5_GROUP_CH)

    def out_map(c):
        cc = c.reshape(n_slab // per_win, per_win, gs, S5_GROUP_CH, S5_STATE)
        w = jnp.einsum('wsghn,sS,gG->wsgnSGh', cc, jnp.eye(per_win, dtype=F32), jnp.eye(gs, dtype=F32))
        return w.reshape(n_slab, MXU_DIM, MXU_DIM).astype(BF16)

    return dict(wbr=in_map(bbar_r), wbi=in_map(bbar_i), wcr=out_map(c_re), wci=out_map(-c_im),
                ar=ar.reshape(1, S5_COLS), ai=ai.reshape(1, S5_COLS), pr=p_r, pi=p_i,
                d=d.reshape(1, S5_WIDTH))


def _s5_call(x, x_first, norm_g, w_in, sw, w_glu, og, seg, chained, h0=None, prev=None):
    rows = seg * SUBLANES
    const2 = lambda *_: (0, 0)
    const3 = lambda *_: (0, 0, 0)
    if chained:
        n_chunk = SEQ // rows
        grid = (BATCH, n_chunk)
        row_map = lambda b, c: (b * n_chunk + c, 0)
        x_map = row_map
        end_map = lambda b, c: (b, 0, 0)
        n_end = BATCH
        sem = ("parallel", "arbitrary")
    else:
        grid = (DEC_BATCH // SUBLANES,)
        row_map = lambda k: (N_PROMPT // rows + k, 0)
        x_map = lambda k: (x_first + k, 0)
        end_map = lambda k: (k, 0, 0)
        n_end = DEC_BATCH // SUBLANES
        sem = ("parallel",)
    half = S5_COLS // 2
    in_specs = [pl.BlockSpec((rows, D_MODEL), x_map),
                pl.BlockSpec((1, D_MODEL), const2),
                pl.BlockSpec((D_MODEL, S5_WIDTH), const2),
                pl.BlockSpec((2, MXU_DIM, half), const3),
                pl.BlockSpec((2, MXU_DIM, half), const3),
                pl.BlockSpec((S5_COLS // MXU_DIM, MXU_DIM, MXU_DIM), const3),
                pl.BlockSpec((S5_COLS // MXU_DIM, MXU_DIM, MXU_DIM), const3),
                pl.BlockSpec((1, S5_COLS), const2),
                pl.BlockSpec((1, S5_COLS), const2),
                pl.BlockSpec((rows, S5_COLS), const2),
                pl.BlockSpec((rows, S5_COLS), const2),
                pl.BlockSpec((1, S5_WIDTH), const2),
                pl.BlockSpec((S5_WIDTH, 2 * S5_WIDTH), const2),
                pl.BlockSpec((1, S5_WIDTH), const2)]
    args = [x, norm_g.reshape(1, D_MODEL), w_in[:, :S5_WIDTH], sw['wbr'], sw['wbi'], sw['wcr'], sw['wci'], sw['ar'], sw['ai'],
            sw['pr'][:rows], sw['pi'][:rows], sw['d'], w_glu, og.reshape(1, S5_WIDTH)]
    scratch = [pltpu.VMEM((S5_WIDTH // LANES, rows, LANES), F32),
               pltpu.VMEM((rows, S5_COLS), F32), pltpu.VMEM((rows, S5_COLS), F32),
               pltpu.VMEM((SUBLANES, S5_COLS), F32), pltpu.VMEM((SUBLANES, S5_COLS), F32),
               pltpu.VMEM((S5_WIDTH // LANES, rows, LANES), F32)]
    aliases = {}
    if chained:
        scratch += [pltpu.VMEM((1, S5_COLS), F32), pltpu.VMEM((1, S5_COLS), F32)]
    else:
        in_specs += [pl.BlockSpec((None, SUBLANES, S5_COLS), end_map),
                     pl.BlockSpec((None, SUBLANES, S5_COLS), end_map)]
        args += [h0[0], h0[1]]
    if prev is not None:
        in_specs.append(pl.BlockSpec(memory_space=pl.ANY))
        args.append(prev)
        aliases = {len(args) - 1: 0}
    kern = functools.partial(_s5_kernel, seg=seg, chained=chained)
    if prev is not None:
        kern = _drop_inputs(kern, len(args) - 1, 1)
    return pl.pallas_call(
        kern,
        grid=grid,
        in_specs=in_specs,
        out_specs=[pl.BlockSpec((rows, S5_WIDTH), row_map),
                   pl.BlockSpec((None, SUBLANES, S5_COLS), end_map),
                   pl.BlockSpec((None, SUBLANES, S5_COLS), end_map)],
        out_shape=[jax.ShapeDtypeStruct((N_TOK, S5_WIDTH), BF16),
                   jax.ShapeDtypeStruct((n_end, SUBLANES, S5_COLS), F32),
                   jax.ShapeDtypeStruct((n_end, SUBLANES, S5_COLS), F32)],
        scratch_shapes=scratch,
        input_output_aliases=aliases,
        compiler_params=_params(*sem),
        name="s5_chained" if chained else "s5_streams",
    )(*args)


def _drop_inputs(kern, start, n):
    def wrapped(*refs):
        return kern(*refs[:start], *refs[start + n:])
    return wrapped


def _gmlp_kernel(*refs, chunk, emit_vn):
    if emit_vn:
        x_ref, ng_ref, win_ref, lng_ref, lnb_ref, ws_ref, bst_ref, og_ref, o_ref, vn_ref, acc_ref = refs
    else:
        x_ref, ng_ref, win_ref, lng_ref, lnb_ref, ws_ref, bst_ref, og_ref, o_ref, acc_ref = refs
    rows = x_ref.shape[0]
    uv = _dot(_rms(x_ref[...], ng_ref[...]).astype(BF16), win_ref[...])
    u = jax.nn.gelu(uv[:, :GM_WIDTH])
    v = jax.nn.gelu(uv[:, GM_WIDTH:])
    tri = (lax.broadcasted_iota(jnp.int32, (chunk, chunk), 0)
           >= lax.broadcasted_iota(jnp.int32, (chunk, chunk), 1))
    for h in range(GM_HEADS):
        cols = slice(h * GM_HEAD_DIM, (h + 1) * GM_HEAD_DIM)
        vh = v[:, cols]
        mu = jnp.mean(vh, axis=-1, keepdims=True)
        cen = vh - mu
        var = jnp.mean(cen * cen, axis=-1, keepdims=True)
        vn = cen * lax.rsqrt(var + EPS) * lng_ref[:, cols] + lnb_ref[:, cols]
        if emit_vn:
            vn_ref[:, cols] = vn
        w = jnp.where(tri, ws_ref[h], 0.0).astype(BF16)
        vb = vn.astype(BF16)
        bias = bst_ref[:, h:h + 1]
        for c in range(rows // chunk):
            rs = slice(c * chunk, (c + 1) * chunk)
            acc_ref[rs, cols] = u[rs, cols] * (_dot(w, vb[rs]) + bias)
    o_ref[...] = _rms(acc_ref[...], og_ref[...]).astype(BF16)


def _gmlp_call(x, x_first, norm_g, w_in, ln_g, ln_b, w_s, b_s, og, rows, chunk, first_block, n_blocks, emit_vn,
               prev=None):
    const2 = lambda i: (0, 0)
    in_specs = [pl.BlockSpec((rows, D_MODEL), lambda i: (x_first + i, 0)),
                pl.BlockSpec((1, D_MODEL), const2),
                pl.BlockSpec((D_MODEL, 2 * GM_WIDTH), const2),
                pl.BlockSpec((1, GM_WIDTH), const2),
                pl.BlockSpec((1, GM_WIDTH), const2),
                pl.BlockSpec((GM_HEADS, chunk, chunk), lambda i: (0, 0, 0)),
                pl.BlockSpec((chunk, GM_HEADS), const2),
                pl.BlockSpec((1, GM_WIDTH), const2)]
    args = [x, norm_g.reshape(1, D_MODEL), w_in[:, S5_WIDTH:], ln_g.reshape(1, GM_WIDTH), ln_b.reshape(1, GM_WIDTH),
            w_s[:, :chunk, :chunk], b_s[:, :chunk].T, og.reshape(1, GM_WIDTH)]
    out_specs = [pl.BlockSpec((rows, GM_WIDTH), lambda i: (first_block + i, 0))]
    out_shape = [jax.ShapeDtypeStruct((N_TOK, GM_WIDTH), BF16)]
    if emit_vn:
        out_specs.append(pl.BlockSpec((rows, GM_WIDTH), lambda i: (i, 0)))
        out_shape.append(jax.ShapeDtypeStruct((n_blocks * rows, GM_WIDTH), F32))
    kern = functools.partial(_gmlp_kernel, chunk=chunk, emit_vn=emit_vn)
    aliases = {}
    if prev is not None:
        in_specs.append(pl.BlockSpec(memory_space=pl.ANY))
        args.append(prev)
        aliases = {len(args) - 1: 0}
        kern = _drop_inputs(kern, len(args) - 1, 1)
    return pl.pallas_call(
        kern,
        grid=(n_blocks,),
        in_specs=in_specs,
        out_specs=out_specs,
        out_shape=out_shape,
        scratch_shapes=[pltpu.VMEM((rows, GM_WIDTH), F32)],
        input_output_aliases=aliases,
        compiler_params=_params("parallel"),
        name="gmlp_vn" if emit_vn else "gmlp",
    )(*args)


def _post_mix_kernel(x_ref, a_ref, b_ref, wa_ref, wb_ref, xg_ref, wq_ref, k_ref, v_ref, wo_ref,
                     fg_ref, wt_ref, rb_ref, x2_ref, h_ref, e_ref, gate_ref, cnt_ref, att_ref, *, n_batch, kv_layer):
    rows = x_ref.shape[0]
    tb = rows // n_batch
    x = x_ref[...] + _dot(a_ref[...], wa_ref[...]) + _dot(b_ref[...], wb_ref[...])
    q = _dot(_rms(x, xg_ref[...]).astype(BF16), wq_ref[...]).astype(BF16)
    scale = XA_HEAD_DIM ** -0.5
    for b in range(n_batch):
        rs = slice(b * tb, (b + 1) * tb)
        for h in range(XA_HEADS):
            cols = slice(h * XA_HEAD_DIM, (h + 1) * XA_HEAD_DIM)
            if kv_layer is None:
                k_h, v_h = k_ref[b, :, cols], v_ref[b, :, cols]
            else:
                k_h, v_h = k_ref[b, :, h, :].astype(BF16), v_ref[b, :, h, :].astype(BF16)
            s = lax.dot_general(q[rs, cols], k_h, (((1,), (1,)), ((), ())),
                                preferred_element_type=F32) * scale
            e = jnp.exp(s - jnp.max(s, axis=-1, keepdims=True))
            p = e / jnp.sum(e, axis=-1, keepdims=True)
            att_ref[rs, cols] = _dot(p.astype(BF16), v_h).astype(BF16)
    x = x + _dot(att_ref[...], wo_ref[...])
    x2_ref[...] = x
    _route_rows(x, fg_ref, wt_ref, rb_ref, h_ref, e_ref, gate_ref, cnt_ref)


def _post_mix(x, x_first, a, b, w_out, xg, wq, k, v, wo, fg, rw, rb,
              rows, n_batch, first_block, n_blocks, blocks_per_kv, prev=None, kv_layer=None):
    const2 = lambda i: (0, 0)
    if kv_layer is None:
        kv_spec = pl.BlockSpec((n_batch, N_MEM, D_MODEL), lambda i: (i // blocks_per_kv, 0, 0))
    else:
        kv_spec = pl.BlockSpec((None, n_batch, N_MEM, XA_HEADS, XA_HEAD_DIM),
                               lambda i: (kv_layer, i // blocks_per_kv, 0, 0, 0))
    row_map = lambda i: (first_block + i, 0)
    d = D_MODEL
    in_specs = [pl.BlockSpec((rows, d), lambda i: (x_first + i, 0)),
                pl.BlockSpec((rows, S5_WIDTH), row_map),
                pl.BlockSpec((rows, GM_WIDTH), row_map),
                pl.BlockSpec((S5_WIDTH, d), const2),
                pl.BlockSpec((GM_WIDTH, d), const2),
                pl.BlockSpec((1, d), const2),
                pl.BlockSpec((d, d), const2),
                kv_spec, kv_spec,
                pl.BlockSpec((d, d), const2),
                pl.BlockSpec((1, d), const2),
                pl.BlockSpec((N_EXPERTS, d), const2),
                pl.BlockSpec((N_EXPERTS, 1), const2)]
    args = [x, a, b, w_out[:S5_WIDTH], w_out[S5_WIDTH:], xg.reshape(1, d), wq, k, v, wo,
            fg.reshape(1, d), rw.T.astype(BF16), rb.reshape(N_EXPERTS, 1)]
    kern = functools.partial(_post_mix_kernel, n_batch=n_batch, kv_layer=kv_layer)
    aliases = {}
    if prev is not None:
        n_in = len(args)
        in_specs += [pl.BlockSpec(memory_space=pl.ANY)] * len(prev)
        args += list(prev)
        aliases = {n_in + o: o for o in range(len(prev))}
        kern = _drop_inputs(kern, n_in, len(prev))
    return pl.pallas_call(
        kern,
        grid=(n_blocks,),
        in_specs=in_specs,
        out_specs=[pl.BlockSpec((rows, d), row_map),
                   pl.BlockSpec((rows // SUBLANES,) + PIECE_TILE, lambda i: (first_block + i, 0, 0, 0)),
                   pl.BlockSpec((TOP_K, rows), lambda i: (0, first_block + i)),
                   pl.BlockSpec((TOP_K, rows), lambda i: (0, first_block + i)),
                   pl.BlockSpec((None, N_EXPERTS, LANES), lambda i: (i, 0, 0))],
        out_shape=[jax.ShapeDtypeStruct((N_TOK, d), F32),
                   jax.ShapeDtypeStruct((N_TOK // SUBLANES,) + PIECE_TILE, F32),
                   jax.ShapeDtypeStruct((TOP_K, N_TOK), jnp.int32),
                   jax.ShapeDtypeStruct((TOP_K, N_TOK), F32),
                   jax.ShapeDtypeStruct((n_blocks, N_EXPERTS, LANES), F32)],
        scratch_shapes=[pltpu.VMEM((rows, d), BF16)],
        input_output_aliases=aliases,
        compiler_params=_params("parallel"),
        name="post_mix",
    )(*args)


def _moe_kernel(be_ref, nu_ref, first_ref, slot_ref, nxt_ref, parts_ref,
                x_ref, wg_hbm, wu_hbm, wd_hbm, bg_ref, bu_ref, bd_ref, o_ref,
                wbuf_ref, wgb_ref, wub_ref, wdb_ref, sem, *, layer):
    i = pl.program_id(0)
    active = i < nu_ref[0]

    def copies(e, s):
        return [pltpu.make_async_copy(w.at[layer, e], wbuf_ref.at[s, j], sem.at[s, j])
                for j, w in enumerate((wg_hbm, wu_hbm, wd_hbm))]

    @pl.when(jnp.logical_and(active, first_ref[i] == 1))
    def _():
        s = slot_ref[i]

        @pl.when(i == 0)
        def _():
            for c in copies(be_ref[0], s):
                c.start()

        @pl.when(nxt_ref[i] >= 0)
        def _():
            for c in copies(nxt_ref[i], 1 - s):
                c.start()

        for c in copies(be_ref[i], s):
            c.wait()

        def cast(r, _):
            rs = pl.ds(pl.multiple_of(r * MOE_CAST_ROWS, MOE_CAST_ROWS), MOE_CAST_ROWS)
            wgb_ref[rs, :] = wbuf_ref[s, 0, rs, :].astype(BF16)
            wub_ref[rs, :] = wbuf_ref[s, 1, rs, :].astype(BF16)
            wdb_ref[rs, :] = wbuf_ref[s, 2, rs, :].astype(BF16)
            return 0
        lax.fori_loop(0, D_MODEL // MOE_CAST_ROWS, cast, 0)

    def experts(n_rows):
        words = lax.bitcast_convert_type(
            jnp.concatenate([x_ref[:n_rows // SUBLANES, c].reshape(n_rows, LANES) for c in range(PIECES_PER_ROW)],
                            axis=1), jnp.uint32)
        x = jnp.concatenate([lax.bitcast_convert_type(words << 16, F32),
                             lax.bitcast_convert_type(words & jnp.uint32(HIGH_HALF), F32)], axis=1).astype(BF16)
        g = jnp.minimum(_dot(x, wgb_ref[...]) + bg_ref[...], SWIGLU_LIMIT)
        u = jnp.clip(_dot(x, wub_ref[...]) + bu_ref[...], -SWIGLU_LIMIT, SWIGLU_LIMIT)
        a = g * jax.nn.sigmoid(SWIGLU_ALPHA * g) * (u + 1.0)
        y = _dot(a.astype(BF16), wdb_ref[...]) + bd_ref[...]
        for c in range(OUT_PIECES):
            o_ref[:n_rows // SUBLANES, c] = y[:, c * LANES:(c + 1) * LANES].reshape(n_rows // SUBLANES, SUBLANES, LANES)

    for p in range(1, MOE_ROWS // MOE_PART + 1):
        pl.when(jnp.logical_and(active, parts_ref[i] == p))(functools.partial(experts, p * MOE_PART))


def _moe_call(layer, xs, sched, w_gate, b_gate, w_up, b_up, w_down, b_down):
    def piece_map(i, be, nu, *_):
        return (jnp.minimum(i, nu[0] - 1), 0, 0, 0)

    def b_map(i, be, *_):
        return (layer, be[i], 0, 0)

    b_spec = pl.BlockSpec((None, None, 1, D_FF), b_map)
    w_spec = pl.BlockSpec(memory_space=pl.ANY)
    grid_spec = pltpu.PrefetchScalarGridSpec(
        num_scalar_prefetch=6,
        grid=(MOE_BLOCKS,),
        in_specs=[pl.BlockSpec((MOE_ROWS // SUBLANES,) + PIECE_TILE, piece_map),
                  w_spec, w_spec, w_spec, b_spec, b_spec, b_spec],
        out_specs=pl.BlockSpec((MOE_ROWS // SUBLANES,) + OUT_TILE, piece_map),
        scratch_shapes=[pltpu.VMEM((2, 3, D_MODEL, D_FF), F32)]
        + [pltpu.VMEM((D_MODEL, D_FF), BF16)] * 3
        + [pltpu.SemaphoreType.DMA((2, 3))],
    )
    shape4 = (DEPTH, N_EXPERTS, 1, D_FF)
    return pl.pallas_call(
        functools.partial(_moe_kernel, layer=layer),
        grid_spec=grid_spec,
        out_shape=jax.ShapeDtypeStruct((MOE_ROWS_TOTAL // SUBLANES,) + OUT_TILE, F32),
        compiler_params=pltpu.CompilerParams(dimension_semantics=("arbitrary",),
                                             vmem_limit_bytes=MOE_VMEM_LIMIT),
        name="moe_experts",
    )(*sched, xs, w_gate, w_up, w_down, b_gate.reshape(shape4), b_up.reshape(shape4),
      b_down.reshape(shape4))


DISPATCH_TOKENS = 176
DMA_PIECES = 64
DISPATCH_DMAS = DISPATCH_TOKENS * PIECES_PER_ROW // DMA_PIECES
DISPATCH_CHUNKS = N_TOK // SC_WORKERS // DISPATCH_TOKENS
IDX_ROWS = 48
SC_LANES = 16
MOE_ROWS_TOTAL = MOE_BLOCKS * MOE_ROWS
TILE_PIECES = PIECES_PER_ROW * SUBLANES


def _piece_index_rows(p_v, idx_v, row0, tok0, n_dma, pieces_per_row, dma_pieces):
    lane = lax.iota(jnp.int32, SC_LANES)
    in_tile = lane & (SUBLANES - 1)
    piece = (lane >> 3) * SUBLANES
    per_vec = SC_LANES // SUBLANES
    tile_pieces = pieces_per_row * SUBLANES
    tiles_per_dma = dma_pieces // tile_pieces
    for m in range(n_dma):
        for tr in range(tiles_per_dma):
            first = plsc.load_gather(p_v, [in_tile + (tok0 + (m * tiles_per_dma + tr) * SUBLANES)])
            for v in range(pieces_per_row // per_vec):
                idx_v[row0 + m, pl.ds(tr * tile_pieces + v * SC_LANES, SC_LANES)] = (
                    first + (piece + v * per_vec * SUBLANES))


COLLECT_ROWS = 48
COLLECT_PIECES = 128
OUT_PIECES = D_MODEL // LANES
OUT_TILE = (OUT_PIECES, SUBLANES, LANES)


def _collect(ys4, dest_t):
    first_piece = ((dest_t // SUBLANES) * (OUT_PIECES * SUBLANES) + dest_t % SUBLANES).reshape(-1)
    chunk_pieces = COLLECT_ROWS * OUT_PIECES
    n_dma = chunk_pieces // COLLECT_PIECES
    chunks_per_choice = N_TOK // SC_WORKERS // COLLECT_ROWS
    n_chunks = TOP_K * chunks_per_choice
    mesh = plsc.VectorSubcoreMesh(core_axis_name="c", subcore_axis_name="s")

    @functools.partial(
        pl.kernel, mesh=mesh,
        out_type=jax.ShapeDtypeStruct((TOP_K * N_TOK * OUT_PIECES, LANES), F32),
        scratch_types=[pltpu.VMEM((chunk_pieces, LANES), F32), pltpu.VMEM((chunk_pieces, LANES), F32),
                       pltpu.VMEM((SUBLANES, COLLECT_PIECES), jnp.int32),
                       pltpu.VMEM((SUBLANES, COLLECT_PIECES), jnp.int32),
                       pltpu.VMEM((COLLECT_ROWS,), jnp.int32),
                       pltpu.SemaphoreType.DMA, pltpu.SemaphoreType.DMA],
        compiler_params=pltpu.CompilerParams(needs_layout_passes=False),
    )
    def gather_rows(y_hbm, p_hbm, o_hbm, rows0, rows1, idx0, idx1, p_v, sem0, sem1):
        wid = lax.axis_index("s") * SC_CORES + lax.axis_index("c")
        bufs = ((rows0, idx0, sem0), (rows1, idx1, sem1))

        def first_row(j):
            k = j // chunks_per_choice
            return k * N_TOK + (wid * chunks_per_choice + (j - k * chunks_per_choice)) * COLLECT_ROWS

        def copies(b):
            rows_v, idx_v, sem = bufs[b]
            return [pltpu.make_async_copy(y_hbm.at[idx_v.at[m]],
                                          rows_v.at[pl.ds(m * COLLECT_PIECES, COLLECT_PIECES)], sem)
                    for m in range(n_dma)]

        def fire(j, b):
            pltpu.sync_copy(p_hbm.at[pl.ds(pl.multiple_of(first_row(j), SUBLANES), COLLECT_ROWS)], p_v)
            _piece_index_rows(p_v, bufs[b][1], 0, 0, n_dma, OUT_PIECES, COLLECT_PIECES)
            for cp in copies(b):
                cp.start()

        def drain(j, b):
            for cp in copies(b):
                cp.wait()
            pltpu.sync_copy(bufs[b][0],
                            o_hbm.at[pl.ds(pl.multiple_of(first_row(j) * OUT_PIECES, SUBLANES), chunk_pieces)])

        fire(0, 0)

        @pl.loop(0, n_chunks, step=2)
        def _(j):
            fire(j + 1, 1)
            drain(j, 0)
            pl.when(j + 2 < n_chunks)(lambda: fire(j + 2, 0))
            drain(j + 1, 1)

    picked = gather_rows(ys4.reshape(MOE_ROWS_TOTAL * OUT_PIECES, LANES), first_piece)
    return picked.reshape((TOP_K, N_TOK // SUBLANES) + OUT_TILE)


def _dispatch(h4, dest_t):
    first_piece = ((dest_t // SUBLANES) * TILE_PIECES + dest_t % SUBLANES).reshape(-1)
    chunk_pieces = DISPATCH_TOKENS * PIECES_PER_ROW
    mesh = plsc.VectorSubcoreMesh(core_axis_name="c", subcore_axis_name="s")

    @functools.partial(
        pl.kernel, mesh=mesh,
        out_type=jax.ShapeDtypeStruct((MOE_ROWS_TOTAL * PIECES_PER_ROW, LANES), F32),
        scratch_types=[pltpu.VMEM((chunk_pieces, LANES), F32), pltpu.VMEM((IDX_ROWS, DMA_PIECES), jnp.int32),
                       pltpu.VMEM((TOP_K * DISPATCH_TOKENS,), jnp.int32), pltpu.SemaphoreType.DMA],
        compiler_params=pltpu.CompilerParams(needs_layout_passes=False),
    )
    def scatter_rows(h_hbm, p_hbm, o_hbm, rows_v, idx_v, p_v, sem):
        wid = lax.axis_index("s") * SC_CORES + lax.axis_index("c")

        @pl.loop(0, DISPATCH_CHUNKS)
        def _(j):
            q = wid * DISPATCH_CHUNKS + j
            pltpu.sync_copy(h_hbm.at[pl.ds(pl.multiple_of(q * chunk_pieces, SUBLANES), chunk_pieces)], rows_v)
            for k in range(TOP_K):
                pltpu.sync_copy(
                    p_hbm.at[pl.ds(pl.multiple_of(k * N_TOK + q * DISPATCH_TOKENS, SUBLANES), DISPATCH_TOKENS)],
                    p_v.at[pl.ds(k * DISPATCH_TOKENS, DISPATCH_TOKENS)])
            for k in range(TOP_K):
                _piece_index_rows(p_v, idx_v, k * DISPATCH_DMAS, k * DISPATCH_TOKENS, DISPATCH_DMAS,
                                  PIECES_PER_ROW, DMA_PIECES)
            copies = [pltpu.async_copy(rows_v.at[pl.ds(m * DMA_PIECES, DMA_PIECES)],
                                       o_hbm.at[idx_v.at[k * DISPATCH_DMAS + m]], sem)
                      for k in range(TOP_K) for m in range(DISPATCH_DMAS)]
            for cp in copies:
                cp.wait()

    xs = scatter_rows(h4.reshape(N_TOK * PIECES_PER_ROW, LANES), first_piece)
    return xs.reshape((MOE_ROWS_TOTAL // SUBLANES,) + PIECE_TILE)


def _route(e_t, cnt_prompt, cnt_sample):
    cnt_sample = jnp.sum(cnt_sample[:, :, 0].reshape(SAMPLE_BLOCKS, -1, N_EXPERTS), axis=1)
    blk_cnt = jnp.concatenate([cnt_prompt[:, :, 0], cnt_sample]).astype(jnp.int32)
    def before(n):
        i = jnp.arange(n, dtype=jnp.int32)
        return i[None, :] < i[:, None]

    counts = jnp.sum(blk_cnt, axis=0)
    padded = ((counts + MOE_ROWS - 1) // MOE_ROWS) * MOE_ROWS
    pstart = jnp.sum(jnp.where(before(N_EXPERTS), padded[None, :], 0), axis=1)
    pend = pstart + padded
    base = pstart[None, :] + jnp.sum(jnp.where(before(blk_cnt.shape[0])[:, :, None], blk_cnt[None], 0), axis=1)
    dest_t = _dest(e_t, base.astype(F32)[:, :, None])
    n_used = jnp.sum(padded) // MOE_ROWS
    ids = jnp.arange(MOE_BLOCKS, dtype=jnp.int32)
    blk = jnp.minimum(ids, n_used - 1) * MOE_ROWS
    blk_exp = jnp.minimum(jnp.sum((pend[None, :] <= blk[:, None]).astype(jnp.int32), axis=1), N_EXPERTS - 1)
    eid = jnp.arange(N_EXPERTS, dtype=jnp.int32)
    of_blk = blk_exp[:, None] == eid[None, :]

    def lookup(table):
        return jnp.sum(jnp.where(of_blk, table[None, :], 0), axis=1)

    first = jnp.logical_and(ids < n_used, jnp.logical_or(ids == 0, blk_exp != jnp.roll(blk_exp, 1)))
    first = first.astype(jnp.int32)
    slot = jnp.sum(jnp.where(before(MOE_BLOCKS), first[None, :], 0), axis=1) % 2
    later = jnp.logical_and(eid[None, :] > eid[:, None], (counts > 0)[None, :])
    nxt_e = jnp.min(jnp.where(later, eid[None, :], N_EXPERTS), axis=1)
    nxt = lookup(jnp.where(nxt_e < N_EXPERTS, nxt_e, -1))
    real = jnp.clip(lookup(pstart + counts) - blk, 0, MOE_ROWS)
    parts = (real + MOE_PART - 1) // MOE_PART
    return dest_t, (blk_exp, n_used.reshape(1).astype(jnp.int32), first, slot.astype(jnp.int32),
                    nxt.astype(jnp.int32), parts.astype(jnp.int32))


def kernel(x_prompt, x_sample, mem_prompt, state_s5_re, state_s5_im, cache_mem_k, cache_mem_v, norm_mix_g, w_in, s5_lambda_re, s5_lambda_im, s5_b_re, s5_b_im, s5_c_re, s5_c_im, s5_d, s5_log_step, s5_w_glu, gm_ln_g, gm_ln_b, gm_w_s, gm_b_s, out_norm_g, w_out, xa_norm_g, mem_norm_g, w_xq, w_xk, w_xv, w_xo, ffn_norm_g, router_w, router_b, e_w_gate, e_b_gate, e_w_up, e_b_up, e_w_down, e_b_down, final_norm_g):
    xs = (x_prompt.reshape(N_PROMPT, D_MODEL), x_sample.reshape(N_SAMPLE, D_MODEL))
    mem = mem_prompt.reshape(BATCH * N_MEM, D_MODEL)
    outs = {k: [] for k in ('s5r_p', 's5i_p', 's5r_s', 's5i_s', 'gmv_s')}
    mem_kv = None
    for l in range(DEPTH):
        unified = len(xs) == 1

        def sample_first(rows):
            return N_PROMPT // rows if unified else 0

        in_proj = (norm_mix_g[l], w_in[l].astype(BF16))
        sw = _s5_weights(s5_lambda_re[l], s5_lambda_im[l], s5_b_re[l], s5_b_im[l], s5_c_re[l], s5_c_im[l],
                         s5_d[l], s5_log_step[l], S5_SEG_PROMPT)
        w_glu = s5_w_glu[l].astype(BF16)
        og = out_norm_g[l]
        ms5, er_p, ei_p = _s5_call(xs[0], 0, *in_proj, sw, w_glu, og[:S5_WIDTH], S5_SEG_PROMPT, True)
        h0 = (state_s5_re[l].reshape(DEC_BATCH // SUBLANES, SUBLANES, S5_COLS),
              state_s5_im[l].reshape(DEC_BATCH // SUBLANES, SUBLANES, S5_COLS))
        ms5, er_s, ei_s = _s5_call(xs[-1], sample_first(DEC_SEQ * SUBLANES), *in_proj, sw, w_glu, og[:S5_WIDTH],
                                   DEC_SEQ, False, h0=h0, prev=ms5)
        outs['s5r_p'].append(er_p[:, SUBLANES - 1].reshape(BATCH, S5_GROUPS, S5_STATE))
        outs['s5i_p'].append(ei_p[:, SUBLANES - 1].reshape(BATCH, S5_GROUPS, S5_STATE))
        outs['s5r_s'].append(er_s.reshape(DEC_BATCH, S5_GROUPS, S5_STATE))
        outs['s5i_s'].append(ei_s.reshape(DEC_BATCH, S5_GROUPS, S5_STATE))

        (mgm,) = _gmlp_call(xs[0], 0, *in_proj, gm_ln_g[l], gm_ln_b[l], gm_w_s[l], gm_b_s[l], og[S5_WIDTH:],
                            GM_ROWS_PROMPT, GM_CHUNK, 0, N_PROMPT // GM_ROWS_PROMPT, False)
        mgm, vn = _gmlp_call(xs[-1], sample_first(N_SAMPLE), *in_proj, gm_ln_g[l], gm_ln_b[l], gm_w_s[l],
                             gm_b_s[l], og[S5_WIDTH:], N_SAMPLE, DEC_SEQ, N_PROMPT // N_SAMPLE, 1, True, prev=mgm)
        outs['gmv_s'].append(vn.reshape(DEC_BATCH, DEC_SEQ, GM_HEADS, GM_HEAD_DIM))
        *mem_kv, mk, mv = _mem_kv(mem, mem_norm_g[l], w_xk[l].astype(BF16), w_xv[l].astype(BF16), l, prev=mem_kv)
        shared = (w_out[l].astype(BF16), xa_norm_g[l], w_xq[l].astype(BF16))
        tail = (w_xo[l].astype(BF16), ffn_norm_g[l], router_w[l], router_b[l])
        rows_s = XA_SAMPLE_BATCHES * DEC_SEQ
        post_p = _post_mix(xs[0], 0, ms5, mgm, *shared, mk, mv, *tail,
                           XA_ROWS_PROMPT, 1, 0, N_PROMPT // XA_ROWS_PROMPT, SEQ // XA_ROWS_PROMPT)
        post_s = _post_mix(xs[-1], sample_first(rows_s), ms5, mgm, *shared,
                           cache_mem_k, cache_mem_v, *tail,
                           rows_s, XA_SAMPLE_BATCHES, N_PROMPT // rows_s, N_SAMPLE // rows_s, 1,
                           prev=post_p[:4], kv_layer=l)
        x, h, e_t, gates_t = post_s[:4]

        dest_t, sched = _route(e_t, post_p[4], post_s[4])
        ys = _moe_call(l, _dispatch(h, dest_t), sched, e_w_gate, e_b_gate, e_w_up, e_b_up, e_w_down, e_b_down)
        x = _combine(x, _collect(ys, dest_t), gates_t.T, final_norm_g if l == DEPTH - 1 else None)
        xs = (x,)

    y_prompt = x[0].reshape(BATCH, SEQ, D_MODEL)
    y_sample = x[1].reshape(DEC_BATCH, DEC_SEQ, D_MODEL)
    st = jnp.stack
    return (y_prompt, y_sample, st(outs['s5r_p']), st(outs['s5i_p']), mem_kv[0], mem_kv[1],
            st(outs['s5r_s']), st(outs['s5i_s']), st(outs['gmv_s']))
```
